```python
import jax, jax.numpy as jnp
from jax import lax
import numpy as np

D_MODEL = 2048
BATCH = 8
SEQ = 4096
DEPTH = 2

N_EVEN = (DEPTH + 1) // 2
N_ODD = DEPTH // 2
W_A = D_MODEL // 2
W_B = D_MODEL // 2
HEAD_DIM = 64
CONV_A = 3
CONV_B = 31
W_C = D_MODEL
POOL_WINDOWS = (2, 4, 8, 16)
N_POOL_GROUPS = len(POOL_WINDOWS)
G_C = W_C // N_POOL_GROUPS
EVEN_IN = 4 * W_A + 3 * W_B
ODD_IN = 2 * W_C
EPS = 1e-6

kernel_name = "hybrid_shortconv_conformer_pool_sandwich"


def _rmsnorm(x, g):
    xf = x.astype(jnp.float32)
    r = lax.rsqrt(jnp.mean(xf * xf, axis=-1, keepdims=True) + EPS)
    return (xf * r).astype(x.dtype) * g


def _layernorm(x, g, b):
    xf = x.astype(jnp.float32)
    mu = jnp.mean(xf, axis=-1, keepdims=True)
    var = jnp.mean(jnp.square(xf - mu), axis=-1, keepdims=True)
    return ((xf - mu) * lax.rsqrt(var + EPS)).astype(x.dtype) * g + b


def _causal_dwconv(x, w):
    k, c = w.shape
    return lax.conv_general_dilated(
        x, w[:, None, :].astype(x.dtype), window_strides=(1,),
        padding=[(k - 1, 0)], dimension_numbers=("NWC", "WIO", "NWC"),
        feature_group_count=c)


def _causal_pool_means(v):
    s = v.shape[1]
    cs = jnp.cumsum(v.astype(jnp.float32), axis=1)
    pos = jnp.arange(1, s + 1, dtype=jnp.int32)
    outs = []
    for win, c in zip(POOL_WINDOWS, jnp.split(cs, N_POOL_GROUPS, axis=-1)):
        shifted = jnp.pad(c, ((0, 0), (win, 0), (0, 0)))[:, :s]
        cnt = jnp.minimum(pos, win).astype(jnp.float32)[None, :, None]
        outs.append((c - shifted) / cnt)
    return jnp.concatenate(outs, axis=-1).astype(v.dtype)


def _even_mixer(h, w_in, a_conv, b_conv, b_conv_bias, b_ln_g, b_ln_b, w_out):
    p = h @ w_in
    a_x, a_b, a_c, a_z, b_val, b_gate, b_z = jnp.split(p, 7, axis=-1)
    ya = a_b * _causal_dwconv(a_c * a_x, a_conv)
    yb = b_val * jax.nn.sigmoid(b_gate)
    yb = _causal_dwconv(yb, b_conv) + b_conv_bias
    yb = jax.nn.silu(_layernorm(yb, b_ln_g, b_ln_b))
    u = jnp.concatenate([ya * jax.nn.silu(a_z), yb * jax.nn.silu(b_z)], axis=-1)
    return u @ w_out


def _odd_mixer(h, w_in, c_w, c_b, c_scale, w_out):
    p = h @ w_in
    v, z = jnp.split(p, 2, axis=-1)
    pooled = _causal_pool_means(v) - v
    bsz, s, _ = v.shape
    g = pooled.reshape(bsz, s, N_POOL_GROUPS, G_C)
    g = jnp.einsum("bsgc,gcd->bsgd", g, c_w) + c_b
    y = g.reshape(bsz, s, W_C) * c_scale
    return (y * jax.nn.silu(z)) @ w_out


def _fwd_setup_inputs(seed: int = 0) -> dict:
    key = jax.random.key(seed)
    ks = jax.random.split(key, 24)
    f32 = jnp.float32

    def nrm(k, shape, scale):
        return jax.random.normal(k, shape, f32) * scale

    def gain(k, shape):
        return 1.0 + 0.05 * jax.random.normal(k, shape, f32)

    return {
        "x": jax.random.normal(ks[0], (BATCH, SEQ, D_MODEL), f32),
        "e_norm_pre": gain(ks[1], (N_EVEN, D_MODEL)),
        "e_norm_post": gain(ks[2], (N_EVEN, D_MODEL)),
        "e_w_in": nrm(ks[3], (N_EVEN, D_MODEL, EVEN_IN), D_MODEL ** -0.5),
        "e_a_conv": nrm(ks[4], (N_EVEN, CONV_A, W_A), CONV_A ** -0.5),
        "e_b_conv": nrm(ks[5], (N_EVEN, CONV_B, W_B), CONV_B ** -0.5),
        "e_b_conv_bias": nrm(ks[6], (N_EVEN, W_B), 0.02),
        "e_b_ln_g": gain(ks[7], (N_EVEN, W_B)),
        "e_b_ln_b": nrm(ks[8], (N_EVEN, W_B), 0.02),
        "e_w_out": nrm(ks[9], (N_EVEN, W_A + W_B, D_MODEL), (W_A + W_B) ** -0.5),
        "o_norm_pre": gain(ks[10], (N_ODD, D_MODEL)),
        "o_norm_post": gain(ks[11], (N_ODD, D_MODEL)),
        "o_w_in": nrm(ks[12], (N_ODD, D_MODEL, ODD_IN), D_MODEL ** -0.5),
        "o_c_w": nrm(ks[13], (N_ODD, N_POOL_GROUPS, G_C, G_C), G_C ** -0.5),
        "o_c_b": nrm(ks[14], (N_ODD, N_POOL_GROUPS, G_C), 0.02),
        "o_c_scale": gain(ks[15], (N_ODD, W_C)),
        "o_w_out": nrm(ks[16], (N_ODD, W_C, D_MODEL), W_C ** -0.5),
    }


def _fwd_reference(x, e_norm_pre, e_norm_post, e_w_in, e_a_conv, e_b_conv, e_b_conv_bias,
              e_b_ln_g, e_b_ln_b, e_w_out, o_norm_pre, o_norm_post, o_w_in, o_c_w,
              o_c_b, o_c_scale, o_w_out):
    for layer in range(DEPTH):
        i = layer // 2
        if layer % 2 == 0:
            h = _rmsnorm(x, e_norm_pre[i])
            y = _even_mixer(h, e_w_in[i], e_a_conv[i], e_b_conv[i], e_b_conv_bias[i],
                            e_b_ln_g[i], e_b_ln_b[i], e_w_out[i])
            x = x + _rmsnorm(y, e_norm_post[i])
        else:
            h = _rmsnorm(x, o_norm_pre[i])
            y = _odd_mixer(h, o_w_in[i], o_c_w[i], o_c_b[i], o_c_scale[i], o_w_out[i])
            x = x + _rmsnorm(y, o_norm_post[i])
    return x


import jax as _jax
import jax.numpy as _jnp

TWIN_FORMAT = 'train_step'
FWD_PARAMS = ['x', 'e_norm_pre', 'e_norm_post', 'e_w_in', 'e_a_conv', 'e_b_conv', 'e_b_conv_bias', 'e_b_ln_g', 'e_b_ln_b', 'e_w_out', 'o_norm_pre', 'o_norm_post', 'o_w_in', 'o_c_w', 'o_c_b', 'o_c_scale', 'o_w_out']
TWIN_WEIGHTS = ['e_norm_pre', 'e_norm_post', 'e_w_in', 'e_a_conv', 'e_b_conv', 'e_b_conv_bias', 'e_b_ln_g', 'e_b_ln_b', 'e_w_out', 'o_norm_pre', 'o_norm_post', 'o_w_in', 'o_c_w', 'o_c_b', 'o_c_scale', 'o_w_out']
TWIN_DIFF_INPUT = 'x'
TWIN_INPUTS = ['x', 'e_norm_pre', 'e_norm_post', 'e_w_in', 'e_a_conv', 'e_b_conv', 'e_b_conv_bias', 'e_b_ln_g', 'e_b_ln_b', 'e_w_out', 'o_norm_pre', 'o_norm_post', 'o_w_in', 'o_c_w', 'o_c_b', 'o_c_scale', 'o_w_out', 'loss_target', 'm_e_norm_pre', 'm_e_norm_post', 'm_e_w_in', 'm_e_a_conv', 'm_e_b_conv', 'm_e_b_conv_bias', 'm_e_b_ln_g', 'm_e_b_ln_b', 'm_e_w_out', 'm_o_norm_pre', 'm_o_norm_post', 'm_o_w_in', 'm_o_c_w', 'm_o_c_b', 'm_o_c_scale', 'm_o_w_out', 'v_e_norm_pre', 'v_e_norm_post', 'v_e_w_in', 'v_e_a_conv', 'v_e_b_conv', 'v_e_b_conv_bias', 'v_e_b_ln_g', 'v_e_b_ln_b', 'v_e_w_out', 'v_o_norm_pre', 'v_o_norm_post', 'v_o_w_in', 'v_o_c_w', 'v_o_c_b', 'v_o_c_scale', 'v_o_w_out']
TWIN_OUTPUTS = ['loss', 'grad_x', 'grad_e_norm_pre', 'grad_e_norm_post', 'grad_e_w_in', 'grad_e_a_conv', 'grad_e_b_conv', 'grad_e_b_conv_bias', 'grad_e_b_ln_g', 'grad_e_b_ln_b', 'grad_e_w_out', 'grad_o_norm_pre', 'grad_o_norm_post', 'grad_o_w_in', 'grad_o_c_w', 'grad_o_c_b', 'grad_o_c_scale', 'grad_o_w_out', 'delta_e_norm_pre', 'delta_e_norm_post', 'delta_e_w_in', 'delta_e_a_conv', 'delta_e_b_conv', 'delta_e_b_conv_bias', 'delta_e_b_ln_g', 'delta_e_b_ln_b', 'delta_e_w_out', 'delta_o_norm_pre', 'delta_o_norm_post', 'delta_o_w_in', 'delta_o_c_w', 'delta_o_c_b', 'delta_o_c_scale', 'delta_o_w_out', 'new_m_e_norm_pre', 'new_m_e_norm_post', 'new_m_e_w_in', 'new_m_e_a_conv', 'new_m_e_b_conv', 'new_m_e_b_conv_bias', 'new_m_e_b_ln_g', 'new_m_e_b_ln_b', 'new_m_e_w_out', 'new_m_o_norm_pre', 'new_m_o_norm_post', 'new_m_o_w_in', 'new_m_o_c_w', 'new_m_o_c_b', 'new_m_o_c_scale', 'new_m_o_w_out', 'new_v_e_norm_pre', 'new_v_e_norm_post', 'new_v_e_w_in', 'new_v_e_a_conv', 'new_v_e_b_conv', 'new_v_e_b_conv_bias', 'new_v_e_b_ln_g', 'new_v_e_b_ln_b', 'new_v_e_w_out', 'new_v_o_norm_pre', 'new_v_o_norm_post', 'new_v_o_w_in', 'new_v_o_c_w', 'new_v_o_c_b', 'new_v_o_c_scale', 'new_v_o_w_out']
TWIN_LEAF_KINDS = {'loss': 'loss', 'grad_x': 'grad_x', 'grad_e_norm_pre': 'grad_w', 'grad_e_norm_post': 'grad_w', 'grad_e_w_in': 'grad_w', 'grad_e_a_conv': 'grad_w', 'grad_e_b_conv': 'grad_w', 'grad_e_b_conv_bias': 'grad_w', 'grad_e_b_ln_g': 'grad_w', 'grad_e_b_ln_b': 'grad_w', 'grad_e_w_out': 'grad_w', 'grad_o_norm_pre': 'grad_w', 'grad_o_norm_post': 'grad_w', 'grad_o_w_in': 'grad_w', 'grad_o_c_w': 'grad_w', 'grad_o_c_b': 'grad_w', 'grad_o_c_scale': 'grad_w', 'grad_o_w_out': 'grad_w', 'delta_e_norm_pre': 'delta_w', 'delta_e_norm_post': 'delta_w', 'delta_e_w_in': 'delta_w', 'delta_e_a_conv': 'delta_w', 'delta_e_b_conv': 'delta_w', 'delta_e_b_conv_bias': 'delta_w', 'delta_e_b_ln_g': 'delta_w', 'delta_e_b_ln_b': 'delta_w', 'delta_e_w_out': 'delta_w', 'delta_o_norm_pre': 'delta_w', 'delta_o_norm_post': 'delta_w', 'delta_o_w_in': 'delta_w', 'delta_o_c_w': 'delta_w', 'delta_o_c_b': 'delta_w', 'delta_o_c_scale': 'delta_w', 'delta_o_w_out': 'delta_w', 'new_m_e_norm_pre': 'new_m', 'new_m_e_norm_post': 'new_m', 'new_m_e_w_in': 'new_m', 'new_m_e_a_conv': 'new_m', 'new_m_e_b_conv': 'new_m', 'new_m_e_b_conv_bias': 'new_m', 'new_m_e_b_ln_g': 'new_m', 'new_m_e_b_ln_b': 'new_m', 'new_m_e_w_out': 'new_m', 'new_m_o_norm_pre': 'new_m', 'new_m_o_norm_post': 'new_m', 'new_m_o_w_in': 'new_m', 'new_m_o_c_w': 'new_m', 'new_m_o_c_b': 'new_m', 'new_m_o_c_scale': 'new_m', 'new_m_o_w_out': 'new_m', 'new_v_e_norm_pre': 'new_v', 'new_v_e_norm_post': 'new_v', 'new_v_e_w_in': 'new_v', 'new_v_e_a_conv': 'new_v', 'new_v_e_b_conv': 'new_v', 'new_v_e_b_conv_bias': 'new_v', 'new_v_e_b_ln_g': 'new_v', 'new_v_e_b_ln_b': 'new_v', 'new_v_e_w_out': 'new_v', 'new_v_o_norm_pre': 'new_v', 'new_v_o_norm_post': 'new_v', 'new_v_o_w_in': 'new_v', 'new_v_o_c_w': 'new_v', 'new_v_o_c_b': 'new_v', 'new_v_o_c_scale': 'new_v', 'new_v_o_w_out': 'new_v'}


def _forward(args):
    return _fwd_reference(*[args[k] for k in FWD_PARAMS])


def _output_shape():
    def fwd():
        inp = _fwd_setup_inputs(0)
        return _fwd_reference(*[inp[k] for k in FWD_PARAMS])
    out = _jax.eval_shape(fwd)
    return out.shape, out.dtype

N_MICROBATCH = 1
ADAM_LR = 0.001
ADAM_B1 = 0.9
ADAM_B2 = 0.999
ADAM_EPS = 1e-08
ADAM_WD = 0.01
ADAM_STEP = 10
PER_EXAMPLE_BATCH_AXIS = {'x': 0, 'loss_target': 0}
SHARED_INPUTS = []
_WEIGHT_DTYPES = {'e_norm_pre': _jnp.float32, 'e_norm_post': _jnp.float32, 'e_w_in': _jnp.float32, 'e_a_conv': _jnp.float32, 'e_b_conv': _jnp.float32, 'e_b_conv_bias': _jnp.float32, 'e_b_ln_g': _jnp.float32, 'e_b_ln_b': _jnp.float32, 'e_w_out': _jnp.float32, 'o_norm_pre': _jnp.float32, 'o_norm_post': _jnp.float32, 'o_w_in': _jnp.float32, 'o_c_w': _jnp.float32, 'o_c_b': _jnp.float32, 'o_c_scale': _jnp.float32, 'o_w_out': _jnp.float32}
MOMENT_SCALE = {'e_norm_pre': 4.215920e-01, 'e_norm_post': 1.604113e+01, 'e_w_in': 2.255502e-01, 'e_a_conv': 2.713630e-01, 'e_b_conv': 1.732893e-01, 'e_b_conv_bias': 4.034038e-01, 'e_b_ln_g': 2.499983e-01, 'e_b_ln_b': 2.357291e-01, 'e_w_out': 2.480195e-01, 'o_norm_pre': 2.544326e-01, 'o_norm_post': 1.601853e+01, 'o_w_in': 1.822913e-01, 'o_c_w': 2.038958e-01, 'o_c_b': 5.552491e-01, 'o_c_scale': 2.010905e-01, 'o_w_out': 2.051954e-01}


def _to_microbatches(a, axis):
    t = _jnp.moveaxis(a, axis, 0)
    t = t.reshape((N_MICROBATCH, t.shape[0] // N_MICROBATCH) + t.shape[1:])
    return _jnp.moveaxis(t, 1, axis + 1)


def setup_inputs(seed: int = 0) -> dict:
    inp = _fwd_setup_inputs(seed)
    key = _jax.random.fold_in(_jax.random.key(seed), 7919)
    shape, _ = _output_shape()
    out = dict(inp)
    out["loss_target"] = _jax.random.normal(_jax.random.fold_in(key, 0), shape, _jnp.float32)
    for i, name in enumerate(TWIN_WEIGHTS):
        w = inp[name].astype(_jnp.float32)
        if MOMENT_SCALE is None:
            s = _jnp.sqrt(_jnp.mean(_jnp.square(w)) + 1e-30)
        else:
            s = MOMENT_SCALE[name]
        km, kv = _jax.random.split(_jax.random.fold_in(key, i + 1))
        out[name] = w
        out["m_" + name] = s * _jax.random.normal(km, w.shape, _jnp.float32)
        out["v_" + name] = (s * s) * _jax.random.uniform(kv, w.shape, _jnp.float32, 0.5, 1.5)
    if N_MICROBATCH > 1:
        for name, axis in PER_EXAMPLE_BATCH_AXIS.items():
            out[name] = _to_microbatches(out[name], axis)
    return {'x': out['x'], 'e_norm_pre': out['e_norm_pre'], 'e_norm_post': out['e_norm_post'], 'e_w_in': out['e_w_in'], 'e_a_conv': out['e_a_conv'], 'e_b_conv': out['e_b_conv'], 'e_b_conv_bias': out['e_b_conv_bias'], 'e_b_ln_g': out['e_b_ln_g'], 'e_b_ln_b': out['e_b_ln_b'], 'e_w_out': out['e_w_out'], 'o_norm_pre': out['o_norm_pre'], 'o_norm_post': out['o_norm_post'], 'o_w_in': out['o_w_in'], 'o_c_w': out['o_c_w'], 'o_c_b': out['o_c_b'], 'o_c_scale': out['o_c_scale'], 'o_w_out': out['o_w_out'], 'loss_target': out['loss_target'], 'm_e_norm_pre': out['m_e_norm_pre'], 'm_e_norm_post': out['m_e_norm_post'], 'm_e_w_in': out['m_e_w_in'], 'm_e_a_conv': out['m_e_a_conv'], 'm_e_b_conv': out['m_e_b_conv'], 'm_e_b_conv_bias': out['m_e_b_conv_bias'], 'm_e_b_ln_g': out['m_e_b_ln_g'], 'm_e_b_ln_b': out['m_e_b_ln_b'], 'm_e_w_out': out['m_e_w_out'], 'm_o_norm_pre': out['m_o_norm_pre'], 'm_o_norm_post': out['m_o_norm_post'], 'm_o_w_in': out['m_o_w_in'], 'm_o_c_w': out['m_o_c_w'], 'm_o_c_b': out['m_o_c_b'], 'm_o_c_scale': out['m_o_c_scale'], 'm_o_w_out': out['m_o_w_out'], 'v_e_norm_pre': out['v_e_norm_pre'], 'v_e_norm_post': out['v_e_norm_post'], 'v_e_w_in': out['v_e_w_in'], 'v_e_a_conv': out['v_e_a_conv'], 'v_e_b_conv': out['v_e_b_conv'], 'v_e_b_conv_bias': out['v_e_b_conv_bias'], 'v_e_b_ln_g': out['v_e_b_ln_g'], 'v_e_b_ln_b': out['v_e_b_ln_b'], 'v_e_w_out': out['v_e_w_out'], 'v_o_norm_pre': out['v_o_norm_pre'], 'v_o_norm_post': out['v_o_norm_post'], 'v_o_w_in': out['v_o_w_in'], 'v_o_c_w': out['v_o_c_w'], 'v_o_c_b': out['v_o_c_b'], 'v_o_c_scale': out['v_o_c_scale'], 'v_o_w_out': out['v_o_w_out']}


def _loss(weights, diff, rest, loss_target):
    with _jax.named_scope("forward"):
        args = {**rest, TWIN_DIFF_INPUT: diff, **{k: w.astype(_WEIGHT_DTYPES[k]) for k, w in weights.items()}}
        y = _forward(args)
    with _jax.named_scope("loss_head"):
        err = _jnp.square(y.astype(_jnp.float32) - loss_target)
        return 0.5 * _jnp.sum(_jnp.mean(err, axis=-1)) if err.ndim else 0.5 * err


def _adamw(w, g, m, v):
    m = ADAM_B1 * m + (1.0 - ADAM_B1) * g
    v = ADAM_B2 * v + (1.0 - ADAM_B2) * _jnp.square(g)
    m_hat = m / (1.0 - ADAM_B1 ** ADAM_STEP)
    v_hat = v / (1.0 - ADAM_B2 ** ADAM_STEP)
    delta = -ADAM_LR * (m_hat / (_jnp.sqrt(v_hat) + ADAM_EPS) + ADAM_WD * w)
    return delta, m, v


def reference(x, e_norm_pre, e_norm_post, e_w_in, e_a_conv, e_b_conv, e_b_conv_bias, e_b_ln_g, e_b_ln_b, e_w_out, o_norm_pre, o_norm_post, o_w_in, o_c_w, o_c_b, o_c_scale, o_w_out, loss_target, m_e_norm_pre, m_e_norm_post, m_e_w_in, m_e_a_conv, m_e_b_conv, m_e_b_conv_bias, m_e_b_ln_g, m_e_b_ln_b, m_e_w_out, m_o_norm_pre, m_o_norm_post, m_o_w_in, m_o_c_w, m_o_c_b, m_o_c_scale, m_o_w_out, v_e_norm_pre, v_e_norm_post, v_e_w_in, v_e_a_conv, v_e_b_conv, v_e_b_conv_bias, v_e_b_ln_g, v_e_b_ln_b, v_e_w_out, v_o_norm_pre, v_o_norm_post, v_o_w_in, v_o_c_w, v_o_c_b, v_o_c_scale, v_o_w_out):
    given = dict(x=x, e_norm_pre=e_norm_pre, e_norm_post=e_norm_post, e_w_in=e_w_in, e_a_conv=e_a_conv, e_b_conv=e_b_conv, e_b_conv_bias=e_b_conv_bias, e_b_ln_g=e_b_ln_g, e_b_ln_b=e_b_ln_b, e_w_out=e_w_out, o_norm_pre=o_norm_pre, o_norm_post=o_norm_post, o_w_in=o_w_in, o_c_w=o_c_w, o_c_b=o_c_b, o_c_scale=o_c_scale, o_w_out=o_w_out, loss_target=loss_target, m_e_norm_pre=m_e_norm_pre, m_e_norm_post=m_e_norm_post, m_e_w_in=m_e_w_in, m_e_a_conv=m_e_a_conv, m_e_b_conv=m_e_b_conv, m_e_b_conv_bias=m_e_b_conv_bias, m_e_b_ln_g=m_e_b_ln_g, m_e_b_ln_b=m_e_b_ln_b, m_e_w_out=m_e_w_out, m_o_norm_pre=m_o_norm_pre, m_o_norm_post=m_o_norm_post, m_o_w_in=m_o_w_in, m_o_c_w=m_o_c_w, m_o_c_b=m_o_c_b, m_o_c_scale=m_o_c_scale, m_o_w_out=m_o_w_out, v_e_norm_pre=v_e_norm_pre, v_e_norm_post=v_e_norm_post, v_e_w_in=v_e_w_in, v_e_a_conv=v_e_a_conv, v_e_b_conv=v_e_b_conv, v_e_b_conv_bias=v_e_b_conv_bias, v_e_b_ln_g=v_e_b_ln_g, v_e_b_ln_b=v_e_b_ln_b, v_e_w_out=v_e_w_out, v_o_norm_pre=v_o_norm_pre, v_o_norm_post=v_o_norm_post, v_o_w_in=v_o_w_in, v_o_c_w=v_o_c_w, v_o_c_b=v_o_c_b, v_o_c_scale=v_o_c_scale, v_o_w_out=v_o_w_out)
    weights = {n: given[n] for n in TWIN_WEIGHTS}
    shared = {n: given[n] for n in SHARED_INPUTS}
    per_example = {n: given[n] for n in ['x']}
    grad_fn = _jax.value_and_grad(_loss, argnums=(0, 1))

    def one_microbatch(ex, loss_target):
        ex = dict(ex)
        diff = ex.pop(TWIN_DIFF_INPUT)
        return grad_fn(weights, diff, {**shared, **ex}, loss_target)

    if N_MICROBATCH == 1:
        loss, (grad_w, grad_x) = one_microbatch(per_example, given["loss_target"])
    else:
        def body(carry, xs):
            loss_sum, grad_sum = carry
            l_k, (gw_k, gx_k) = one_microbatch(xs[0], xs[1])
            with _jax.named_scope("update"):
                return (loss_sum + l_k, _jax.tree.map(_jnp.add, grad_sum, gw_k)), gx_k

        init = (_jnp.zeros((), _jnp.float32), _jax.tree.map(_jnp.zeros_like, weights))
        (loss, grad_w), grad_x = _jax.lax.scan(body, init, (per_example, given["loss_target"]))
    with _jax.named_scope("update"):
        delta_w, new_m, new_v = {}, {}, {}
        for n in TWIN_WEIGHTS:
            delta_w[n], new_m[n], new_v[n] = _adamw(weights[n], grad_w[n], given["m_" + n], given["v_" + n])
    return (loss, grad_x, *[grad_w[n] for n in TWIN_WEIGHTS], *[delta_w[n] for n in TWIN_WEIGHTS],
            *[new_m[n] for n in TWIN_WEIGHTS], *[new_v[n] for n in TWIN_WEIGHTS])
```

```python
import functools

import jax
import jax.numpy as jnp
from jax import lax
from jax.experimental import pallas as pl
from jax.experimental.pallas import tpu as pltpu

F32 = jnp.float32
BF16 = jnp.bfloat16
MESH = pl.DeviceIdType.MESH

EPS = 1e-6
CONV_A = 3
CONV_B = 31
POOL_WINDOWS = (2, 4, 8, 16)
N_GROUPS = len(POOL_WINDOWS)
N_SHARDS = 4
N_DEVICES = 8
ADAM_LR = 0.001
ADAM_B1 = 0.9
ADAM_B2 = 0.999
ADAM_EPS = 1e-08
ADAM_WD = 0.01
ADAM_STEP = 10

LANES = 128
SUBLANES_BF16 = 16
HALO_A = 8
HALO_B = 32
HALO_P = 16
VMEM_LIMIT = 56 * 1024 * 1024


def _row_tile(n, pref):
    t = min(n, pref)
    while t > SUBLANES_BF16 and (n % t or t % SUBLANES_BF16):
        t -= SUBLANES_BF16
    assert n % t == 0, (n, pref)
    return t


def _col_chunk(n, pref):
    t = (min(n, pref) // LANES) * LANES
    while t > LANES and n % t:
        t -= LANES
    assert t >= LANES and n % t == 0, (n, pref)
    return t


def _params(*sem):
    return pltpu.CompilerParams(dimension_semantics=tuple(sem) if sem else None, vmem_limit_bytes=VMEM_LIMIT)


def _sigmoid(v):
    return jax.nn.sigmoid(v)


def _dsilu(v, s):
    return s * (1.0 + v * (1.0 - s))


def _mean_last(v):
    return jnp.mean(v, axis=-1, keepdims=True)


def _sum_rows(v):
    return jnp.sum(v, axis=0, keepdims=True)


def _norm_matmul(x, gain, w_sm, name):
    s, d = x.shape
    n_sh, _, ns = w_sm.shape
    tm = _row_tile(s, 512)

    def body(x_ref, g_ref, w_ref, p_ref, h_ref):
        @pl.when(pl.program_id(1) == 0)
        def _():
            xv = x_ref[...]
            r = lax.rsqrt(_mean_last(xv * xv) + EPS)
            h_ref[...] = (xv * r * g_ref[...]).astype(BF16)

        p_ref[...] = jnp.dot(h_ref[...], w_ref[0], preferred_element_type=F32)

    return pl.pallas_call(
        body, name=name, grid=(s // tm, n_sh),
        in_specs=[pl.BlockSpec((tm, d), lambda i, j: (i, 0)),
                  pl.BlockSpec((1, d), lambda i, j: (0, 0)),
                  pl.BlockSpec((1, d, ns), lambda i, j: (j, 0, 0))],
        out_specs=[pl.BlockSpec((tm, ns), lambda i, j: (i, j)),
                   pl.BlockSpec((tm, d), lambda i, j: (i, 0))],
        out_shape=[jax.ShapeDtypeStruct((s, n_sh * ns), F32), jax.ShapeDtypeStruct((s, d), BF16)],
        compiler_params=_params("arbitrary", "arbitrary"),
    )(x, gain, w_sm)


def _matmul_post(u, w, x_res, gain, name):
    s, k = u.shape
    d = w.shape[1]
    tm = _row_tile(s, 256)

    def body(u_ref, w_ref, x_ref, g_ref, xn_ref, y_ref):
        y = jnp.dot(u_ref[...], w_ref[...], preferred_element_type=F32)
        r = lax.rsqrt(_mean_last(y * y) + EPS)
        y_ref[...] = y
        xn_ref[...] = x_ref[...] + (y * r) * g_ref[...]

    return pl.pallas_call(
        body, name=name, grid=(s // tm,),
        in_specs=[pl.BlockSpec((tm, k), lambda i: (i, 0)),
                  pl.BlockSpec((k, d), lambda i: (0, 0)),
                  pl.BlockSpec((tm, d), lambda i: (i, 0)),
                  pl.BlockSpec((1, d), lambda i: (0, 0))],
        out_specs=[pl.BlockSpec((tm, d), lambda i: (i, 0)), pl.BlockSpec((tm, d), lambda i: (i, 0))],
        out_shape=[jax.ShapeDtypeStruct((s, d), F32), jax.ShapeDtypeStruct((s, d), F32)],
        compiler_params=_params("arbitrary"),
    )(u, w, x_res, gain)


def _matmul_post_loss(u, w, x_res, gain, target, name):
    s, k = u.shape
    d = w.shape[1]
    tm = _row_tile(s, 256)

    def body(u_ref, w_ref, x_ref, g_ref, t_ref, dy_ref, dout_ref, dg_ref, loss_ref):
        @pl.when(pl.program_id(0) == 0)
        def _():
            dg_ref[...] = jnp.zeros_like(dg_ref)
            loss_ref[...] = jnp.zeros_like(loss_ref)

        y = jnp.dot(u_ref[...], w_ref[...], preferred_element_type=F32)
        r = lax.rsqrt(_mean_last(y * y) + EPS)
        n = y * r
        g = g_ref[...]
        err = x_ref[...] + n * g - t_ref[...]
        loss_ref[...] += 0.5 * jnp.sum(_mean_last(err * err))
        dout = err * (1.0 / d)
        dout_ref[...] = dout
        dg_ref[...] += _sum_rows(dout * n)
        dn = dout * g
        dy_ref[...] = (r * (dn - n * _mean_last(dn * n))).astype(BF16)

    return pl.pallas_call(
        body, name=name, grid=(s // tm,),
        in_specs=[pl.BlockSpec((tm, k), lambda i: (i, 0)),
                  pl.BlockSpec((k, d), lambda i: (0, 0)),
                  pl.BlockSpec((tm, d), lambda i: (i, 0)),
                  pl.BlockSpec((1, d), lambda i: (0, 0)),
                  pl.BlockSpec((tm, d), lambda i: (i, 0))],
        out_specs=[pl.BlockSpec((tm, d), lambda i: (i, 0)),
                   pl.BlockSpec((tm, d), lambda i: (i, 0)),
                   pl.BlockSpec((1, d), lambda i: (0, 0)),
                   pl.BlockSpec((8, LANES), lambda i: (0, 0))],
        out_shape=[jax.ShapeDtypeStruct((s, d), BF16), jax.ShapeDtypeStruct((s, d), F32),
                   jax.ShapeDtypeStruct((1, d), F32), jax.ShapeDtypeStruct((8, LANES), F32)],
        compiler_params=_params("arbitrary"),
    )(u, w, x_res, gain, target)


def _matmul_nt(a, w_sm, name):
    s, ncols = a.shape
    n_sh, r, ns = w_sm.shape
    assert ncols == n_sh * ns
    tm = _row_tile(s, 512)
    nc = _col_chunk(ns, 1024)
    per = ns // nc
    steps = n_sh * per

    def body(a_ref, w_ref, o_ref):
        part = lax.dot_general(a_ref[...], w_ref[0], (((1,), (1,)), ((), ())), preferred_element_type=F32)

        @pl.when(pl.program_id(1) == 0)
        def _():
            o_ref[...] = part

        @pl.when(pl.program_id(1) > 0)
        def _():
            o_ref[...] += part

    return pl.pallas_call(
        body, name=name, grid=(s // tm, steps),
        in_specs=[pl.BlockSpec((tm, nc), lambda i, j: (i, j)),
                  pl.BlockSpec((1, r, nc), lambda i, j: (j // per, 0, j % per))],
        out_specs=pl.BlockSpec((tm, r), lambda i, j: (i, 0)),
        out_shape=jax.ShapeDtypeStruct((s, r), F32),
        compiler_params=_params("arbitrary", "arbitrary"),
    )(a, w_sm)


def _matmul_tn(a, b, n_sh, name):
    s, k = a.shape
    n = b.shape[1]
    ns = n // n_sh
    tk = _row_tile(k, 1024)
    ts = _row_tile(s, 512)
    n_s = s // ts

    def body(a_ref, b_ref, o_ref, acc_ref):
        part = lax.dot_general(a_ref[...], b_ref[...], (((0,), (0,)), ((), ())), preferred_element_type=F32)

        @pl.when(pl.program_id(2) == 0)
        def _():
            acc_ref[...] = part

        @pl.when(pl.program_id(2) > 0)
        def _():
            acc_ref[...] += part

        @pl.when(pl.program_id(2) == n_s - 1)
        def _():
            o_ref[0] = acc_ref[...].astype(BF16)

    return pl.pallas_call(
        body, name=name, grid=(n_sh, k // tk, n_s),
        in_specs=[pl.BlockSpec((ts, tk), lambda j, i, t: (t, i)),
                  pl.BlockSpec((ts, ns), lambda j, i, t: (t, j))],
        out_specs=pl.BlockSpec((1, tk, ns), lambda j, i, t: (j, i, 0)),
        out_shape=jax.ShapeDtypeStruct((n_sh, k, ns), BF16),
        scratch_shapes=[pltpu.VMEM((tk, ns), F32)],
        compiler_params=_params("arbitrary", "arbitrary", "arbitrary"),
    )(a, b)


def _norm_bwd(dh, x, gain, dres, name, post=None):
    s, d = x.shape
    tm = _row_tile(s, 256)
    with_post = post is not None

    def rms_bwd(dout, v, g):
        r = lax.rsqrt(_mean_last(v * v) + EPS)
        n = v * r
        dn = dout * g
        return r * (dn - n * _mean_last(dn * n)), _sum_rows(dout * n)

    def body(*refs):
        if with_post:
            dh_ref, x_ref, g_ref, dres_ref, y_ref, gp_ref, dx_ref, dg_ref, dy_ref, dgp_ref = refs
        else:
            dh_ref, x_ref, g_ref, dres_ref, dx_ref, dg_ref = refs

        @pl.when(pl.program_id(0) == 0)
        def _():
            dg_ref[...] = jnp.zeros_like(dg_ref)
            if with_post:
                dgp_ref[...] = jnp.zeros_like(dgp_ref)

        dv, dg = rms_bwd(dh_ref[...], x_ref[...], g_ref[...])
        dx = dres_ref[...] + dv
        dx_ref[...] = dx
        dg_ref[...] += dg
        if with_post:
            dy, dgp = rms_bwd(dx, y_ref[...], gp_ref[...])
            dy_ref[...] = dy.astype(BF16)
            dgp_ref[...] += dgp

    row = pl.BlockSpec((tm, d), lambda i: (i, 0))
    vec = pl.BlockSpec((1, d), lambda i: (0, 0))
    in_specs = [row, row, vec, row]
    out_specs = [row, vec]
    out_shape = [jax.ShapeDtypeStruct((s, d), F32), jax.ShapeDtypeStruct((1, d), F32)]
    args = [dh, x, gain, dres]
    if with_post:
        in_specs += [row, vec]
        out_specs += [row, vec]
        out_shape += [jax.ShapeDtypeStruct((s, d), BF16), jax.ShapeDtypeStruct((1, d), F32)]
        args += list(post)
    return pl.pallas_call(
        body, name=name, grid=(s // tm,), in_specs=in_specs, out_specs=out_specs, out_shape=out_shape,
        compiler_params=_params("arbitrary"),
    )(*args)


def _even_forward_tile(p_ref, halo_ref, first, a_conv_ref, b_conv_ref, bias_ref, lng_ref, lnb_ref, qbuf, ybuf, w, ts):
    def col(ref, k, rows=slice(None)):
        return ref[rows, k * w:(k + 1) * w]

    a_x, a_b, a_c, a_z = col(p_ref, 0), col(p_ref, 1), col(p_ref, 2), col(p_ref, 3)
    b_val, b_gate, b_z = col(p_ref, 4), col(p_ref, 5), col(p_ref, 6)
    keep = jnp.where(first, 0.0, 1.0)

    rows_a = slice(HALO_B - HALO_A, HALO_B)
    qbuf[0:HALO_A, :] = col(halo_ref, 2, rows_a) * col(halo_ref, 0, rows_a) * keep
    qbuf[HALO_A:HALO_A + ts, :] = a_c * a_x
    cq = jnp.zeros((ts, w), F32)
    for j in range(CONV_A):
        cq = cq + a_conv_ref[CONV_A - 1 - j:CONV_A - j, :] * qbuf[HALO_A - j:HALO_A - j + ts, :]
    ya = a_b * cq

    ybuf[0:HALO_B, :] = col(halo_ref, 4) * _sigmoid(col(halo_ref, 5)) * keep
    ybuf[HALO_B:HALO_B + ts, :] = b_val * _sigmoid(b_gate)
    yb1 = jnp.zeros((ts, w), F32) + bias_ref[...]
    for j in range(CONV_B):
        yb1 = yb1 + b_conv_ref[CONV_B - 1 - j:CONV_B - j, :] * ybuf[HALO_B - j:HALO_B - j + ts, :]
    xc = yb1 - _mean_last(yb1)
    rstd = lax.rsqrt(_mean_last(xc * xc) + EPS)
    xhat = xc * rstd
    yb2 = xhat * lng_ref[...] + lnb_ref[...]
    return dict(a_x=a_x, a_b=a_b, a_c=a_c, a_z=a_z, b_val=b_val, b_gate=b_gate, b_z=b_z,
                cq=cq, ya=ya, rstd=rstd, xhat=xhat, yb2=yb2)


def _even_specs(s, w, ts):
    tile = pl.BlockSpec((ts, 7 * w), lambda i: (i, 0))
    halo = pl.BlockSpec((HALO_B, 7 * w), lambda i: (jnp.maximum(i * (ts // HALO_B) - 1, 0), 0))
    return tile, halo


def _small_specs(shapes, index=lambda i: (0, 0)):
    return [pl.BlockSpec(sh, index) for sh in shapes]


def _even_mixer_fwd(p, a_conv, b_conv, bias, ln_g, ln_b, name):
    s = p.shape[0]
    w = p.shape[1] // 7
    ts = _row_tile(s, 256)
    assert ts % HALO_B == 0

    def body(p_ref, halo_ref, ac_ref, bc_ref, bias_ref, lng_ref, lnb_ref, u_ref, qbuf, ybuf):
        first = pl.program_id(0) == 0
        f = _even_forward_tile(p_ref, halo_ref, first, ac_ref, bc_ref, bias_ref, lng_ref, lnb_ref, qbuf, ybuf, w, ts)
        yb3 = f["yb2"] * _sigmoid(f["yb2"])
        u_ref[:, 0:w] = (f["ya"] * (f["a_z"] * _sigmoid(f["a_z"]))).astype(BF16)
        u_ref[:, w:2 * w] = (yb3 * (f["b_z"] * _sigmoid(f["b_z"]))).astype(BF16)

    tile, halo = _even_specs(s, w, ts)
    return pl.pallas_call(
        body, name=name, grid=(s // ts,),
        in_specs=[tile, halo] + _small_specs([(CONV_A, w), (CONV_B, w), (1, w), (1, w), (1, w)]),
        out_specs=pl.BlockSpec((ts, 2 * w), lambda i: (i, 0)),
        out_shape=jax.ShapeDtypeStruct((s, 2 * w), BF16),
        scratch_shapes=[pltpu.VMEM((HALO_A + ts, w), F32), pltpu.VMEM((HALO_B + ts, w), F32)],
        compiler_params=_params("arbitrary"),
    )(p, p, a_conv, b_conv, bias, ln_g, ln_b)


def _even_mixer_bwd(p, du, a_conv, b_conv, bias, ln_g, ln_b, name):
    s = p.shape[0]
    w = p.shape[1] // 7
    ts = _row_tile(s, 256)
    nt = s // ts
    assert ts % HALO_B == 0

    def body(p_ref, halo_ref, du_ref, ac_ref, bc_ref, bias_ref, lng_ref, lnb_ref,
             dp_ref, dac_ref, dbc_ref, dbias_ref, dlng_ref, dlnb_ref, qbuf, ybuf, dqbuf, dybuf, carry_dq, carry_dy):
        step = pl.program_id(0)
        first = step == nt - 1

        @pl.when(step == 0)
        def _():
            for ref in (dac_ref, dbc_ref, dbias_ref, dlng_ref, dlnb_ref, carry_dq, carry_dy):
                ref[...] = jnp.zeros_like(ref)

        f = _even_forward_tile(p_ref, halo_ref, first, ac_ref, bc_ref, bias_ref, lng_ref, lnb_ref, qbuf, ybuf, w, ts)
        a_z, b_z, yb2 = f["a_z"], f["b_z"], f["yb2"]
        s_az, s_bz, s_y2 = _sigmoid(a_z), _sigmoid(b_z), _sigmoid(yb2)
        du_a = du_ref[:, 0:w]
        du_b = du_ref[:, w:2 * w]

        d_ya = du_a * (a_z * s_az)
        dp_ref[:, 3 * w:4 * w] = (du_a * f["ya"] * _dsilu(a_z, s_az)).astype(BF16)
        dp_ref[:, 1 * w:2 * w] = (d_ya * f["cq"]).astype(BF16)
        d_cq = d_ya * f["a_b"]
        dqbuf[0:ts, :] = d_cq
        dqbuf[ts:ts + HALO_A, :] = carry_dq[...]
        d_q = jnp.zeros((ts, w), F32)
        for o in range(CONV_A):
            d_q = d_q + ac_ref[CONV_A - 1 - o:CONV_A - o, :] * dqbuf[o:o + ts, :]
        for j in range(CONV_A):
            k = CONV_A - 1 - j
            dac_ref[k:k + 1, :] += _sum_rows(d_cq * qbuf[HALO_A - j:HALO_A - j + ts, :])
        carry_dq[...] = d_cq[0:HALO_A, :]
        dp_ref[:, 2 * w:3 * w] = (d_q * f["a_x"]).astype(BF16)
        dp_ref[:, 0 * w:1 * w] = (d_q * f["a_c"]).astype(BF16)

        d_yb3 = du_b * (b_z * s_bz)
        dp_ref[:, 6 * w:7 * w] = (du_b * (yb2 * s_y2) * _dsilu(b_z, s_bz)).astype(BF16)
        d_yb2 = d_yb3 * _dsilu(yb2, s_y2)
        xhat = f["xhat"]
        dlng_ref[...] += _sum_rows(d_yb2 * xhat)
        dlnb_ref[...] += _sum_rows(d_yb2)
        d_xh = d_yb2 * lng_ref[...]
        d_yb1 = f["rstd"] * (d_xh - _mean_last(d_xh) - xhat * _mean_last(d_xh * xhat))
        dbias_ref[...] += _sum_rows(d_yb1)
        dybuf[0:ts, :] = d_yb1
        dybuf[ts:ts + HALO_B, :] = carry_dy[...]
        d_yb0 = jnp.zeros((ts, w), F32)
        for o in range(CONV_B):
            d_yb0 = d_yb0 + bc_ref[CONV_B - 1 - o:CONV_B - o, :] * dybuf[o:o + ts, :]
        for j in range(CONV_B):
            k = CONV_B - 1 - j
            dbc_ref[k:k + 1, :] += _sum_rows(d_yb1 * ybuf[HALO_B - j:HALO_B - j + ts, :])
        carry_dy[...] = d_yb1[0:HALO_B, :]
        s_g = _sigmoid(f["b_gate"])
        dp_ref[:, 4 * w:5 * w] = (d_yb0 * s_g).astype(BF16)
        dp_ref[:, 5 * w:6 * w] = (d_yb0 * f["b_val"] * s_g * (1.0 - s_g)).astype(BF16)

    rev = lambda i: (nt - 1 - i, 0)
    tile = pl.BlockSpec((ts, 7 * w), rev)
    halo = pl.BlockSpec((HALO_B, 7 * w), lambda i: (jnp.maximum((nt - 1 - i) * (ts // HALO_B) - 1, 0), 0))
    small = [(CONV_A, w), (CONV_B, w), (1, w), (1, w), (1, w)]
    return pl.pallas_call(
        body, name=name, grid=(nt,),
        in_specs=[tile, halo, pl.BlockSpec((ts, 2 * w), rev)] + _small_specs(small),
        out_specs=[pl.BlockSpec((ts, 7 * w), rev)] + _small_specs(small),
        out_shape=[jax.ShapeDtypeStruct((s, 7 * w), BF16)] + [jax.ShapeDtypeStruct(sh, F32) for sh in small],
        scratch_shapes=[pltpu.VMEM((HALO_A + ts, w), F32), pltpu.VMEM((HALO_B + ts, w), F32),
                        pltpu.VMEM((ts + HALO_A, w), F32), pltpu.VMEM((ts + HALO_B, w), F32),
                        pltpu.VMEM((HALO_A, w), F32), pltpu.VMEM((HALO_B, w), F32)],
        compiler_params=_params("arbitrary"),
    )(p, p, du, a_conv, b_conv, bias, ln_g, ln_b)


def _pool_forward_tile(p_ref, halo_ref, first, tile_index, cw_ref, cb_ref, cs_ref, vbuf, c, gc, ts):
    vbuf[0:HALO_P, :] = halo_ref[...] * jnp.where(first, 0.0, 1.0)
    vbuf[HALO_P:HALO_P + ts, :] = p_ref[:, 0:c]
    pos = tile_index * ts + lax.broadcasted_iota(jnp.int32, (ts, 1), 0) + 1
    pooled, inv, gout = [], [], []
    for g, win in enumerate(POOL_WINDOWS):
        cols = slice(g * gc, (g + 1) * gc)
        acc = jnp.zeros((ts, gc), F32)
        for j in range(win):
            acc = acc + vbuf[HALO_P - j:HALO_P - j + ts, cols]
        inv_g = 1.0 / jnp.minimum(pos, win).astype(F32)
        pooled_g = (acc * inv_g - p_ref[:, cols]).astype(BF16)
        pooled.append(pooled_g)
        inv.append(inv_g)
        gout.append(jnp.dot(pooled_g, cw_ref[g], preferred_element_type=F32) + cb_ref[:, cols])
    return pooled, inv, gout


def _odd_mixer_fwd(p, cw, cb, cs, name):
    s = p.shape[0]
    c = p.shape[1] // 2
    gc = c // N_GROUPS
    ts = _row_tile(s, 256)

    def body(p_ref, halo_ref, cw_ref, cb_ref, cs_ref, u_ref, vbuf):
        i = pl.program_id(0)
        _, _, gout = _pool_forward_tile(p_ref, halo_ref, i == 0, i, cw_ref, cb_ref, cs_ref, vbuf, c, gc, ts)
        for g in range(N_GROUPS):
            cols = slice(g * gc, (g + 1) * gc)
            z = p_ref[:, c + g * gc:c + (g + 1) * gc]
            u_ref[:, cols] = (gout[g] * cs_ref[:, cols] * (z * _sigmoid(z))).astype(BF16)

    return pl.pallas_call(
        body, name=name, grid=(s // ts,),
        in_specs=[pl.BlockSpec((ts, 2 * c), lambda i: (i, 0)),
                  pl.BlockSpec((HALO_P, c), lambda i: (jnp.maximum(i * (ts // HALO_P) - 1, 0), 0)),
                  pl.BlockSpec((N_GROUPS, gc, gc), lambda i: (0, 0, 0)),
                  pl.BlockSpec((1, c), lambda i: (0, 0)), pl.BlockSpec((1, c), lambda i: (0, 0))],
        out_specs=pl.BlockSpec((ts, c), lambda i: (i, 0)),
        out_shape=jax.ShapeDtypeStruct((s, c), BF16),
        scratch_shapes=[pltpu.VMEM((HALO_P + ts, c), F32)],
        compiler_params=_params("arbitrary"),
    )(p, p, cw, cb, cs)


def _odd_mixer_bwd(p, du, cw, cb, cs, name):
    s = p.shape[0]
    c = p.shape[1] // 2
    gc = c // N_GROUPS
    ts = _row_tile(s, 256)
    nt = s // ts

    def body(p_ref, halo_ref, du_ref, cw_ref, cb_ref, cs_ref, dp_ref, dcw_ref, dcb_ref, dcs_ref, vbuf, ebuf, carry_e):
        step = pl.program_id(0)
        tile_index = nt - 1 - step

        @pl.when(step == 0)
        def _():
            for ref in (dcw_ref, dcb_ref, dcs_ref, carry_e):
                ref[...] = jnp.zeros_like(ref)

        pooled, inv, gout = _pool_forward_tile(p_ref, halo_ref, tile_index == 0, tile_index, cw_ref, cb_ref, cs_ref,
                                               vbuf, c, gc, ts)
        ebuf[ts:ts + HALO_P, :] = carry_e[...]
        for g, win in enumerate(POOL_WINDOWS):
            cols = slice(g * gc, (g + 1) * gc)
            z = p_ref[:, c + g * gc:c + (g + 1) * gc]
            sz = _sigmoid(z)
            du_g = du_ref[:, cols]
            scale = cs_ref[:, cols]
            d_y = du_g * (z * sz)
            dp_ref[:, c + g * gc:c + (g + 1) * gc] = (du_g * (gout[g] * scale) * _dsilu(z, sz)).astype(BF16)
            dcs_ref[:, cols] += _sum_rows(d_y * gout[g])
            d_gout = d_y * scale
            dcb_ref[:, cols] += _sum_rows(d_gout)
            d_gout_b = d_gout.astype(BF16)
            dcw_ref[g] += lax.dot_general(pooled[g], d_gout_b, (((0,), (0,)), ((), ())), preferred_element_type=F32)
            d_pool = lax.dot_general(d_gout_b, cw_ref[g], (((1,), (1,)), ((), ())), preferred_element_type=F32)
            e = d_pool * inv[g]
            ebuf[0:ts, cols] = e
            d_v = -d_pool
            for o in range(win):
                d_v = d_v + ebuf[o:o + ts, cols]
            dp_ref[:, cols] = d_v.astype(BF16)
            carry_e[:, cols] = e[0:HALO_P, :]

    rev = lambda i: (nt - 1 - i, 0)
    small = [(N_GROUPS, gc, gc), (1, c), (1, c)]
    return pl.pallas_call(
        body, name=name, grid=(nt,),
        in_specs=[pl.BlockSpec((ts, 2 * c), rev),
                  pl.BlockSpec((HALO_P, c), lambda i: (jnp.maximum((nt - 1 - i) * (ts // HALO_P) - 1, 0), 0)),
                  pl.BlockSpec((ts, c), rev),
                  pl.BlockSpec((N_GROUPS, gc, gc), lambda i: (0, 0, 0)),
                  pl.BlockSpec((1, c), lambda i: (0, 0)), pl.BlockSpec((1, c), lambda i: (0, 0))],
        out_specs=[pl.BlockSpec((ts, 2 * c), rev),
                   pl.BlockSpec((N_GROUPS, gc, gc), lambda i: (0, 0, 0)),
                   pl.BlockSpec((1, c), lambda i: (0, 0)), pl.BlockSpec((1, c), lambda i: (0, 0))],
        out_shape=[jax.ShapeDtypeStruct((s, 2 * c), BF16)] + [jax.ShapeDtypeStruct(sh, F32) for sh in small],
        scratch_shapes=[pltpu.VMEM((HALO_P + ts, c), F32), pltpu.VMEM((ts + HALO_P, c), F32),
                        pltpu.VMEM((HALO_P, c), F32)],
        compiler_params=_params("arbitrary"),
    )(p, p, du, cw, cb, cs)


def _sum_leading(a, out_dtype, name):
    n, r, cols = a.shape
    tr = _row_tile(r, 256)

    def body(a_ref, o_ref):
        acc = a_ref[0].astype(F32)
        for k in range(1, n):
            acc = acc + a_ref[k].astype(F32)
        o_ref[...] = acc.astype(out_dtype)

    return pl.pallas_call(
        body, name=name, grid=(r // tr,),
        in_specs=[pl.BlockSpec((n, tr, cols), lambda i: (0, i, 0))],
        out_specs=pl.BlockSpec((tr, cols), lambda i: (i, 0)),
        out_shape=jax.ShapeDtypeStruct((r, cols), out_dtype),
        compiler_params=_params("arbitrary"),
    )(a)


def _sum_small(a, name):
    n, r, cols = a.shape

    def body(a_ref, o_ref):
        acc = a_ref[0]
        for k in range(1, n):
            acc = acc + a_ref[k]
        o_ref[...] = acc

    return pl.pallas_call(
        body, name=name,
        in_specs=[pl.BlockSpec((n, r, cols), lambda: (0, 0, 0))],
        out_specs=pl.BlockSpec((r, cols), lambda: (0, 0)),
        out_shape=jax.ShapeDtypeStruct((r, cols), F32),
        compiler_params=_params(),
    )(a)


def _adamw(w, g, m, v, name):
    r, cols = w.shape
    tr = _row_tile(r, 256) if r % SUBLANES_BF16 == 0 else r

    def body(w_ref, g_ref, m_ref, v_ref, d_ref, nm_ref, nv_ref):
        g = g_ref[...]
        m = ADAM_B1 * m_ref[...] + (1.0 - ADAM_B1) * g
        v = ADAM_B2 * v_ref[...] + (1.0 - ADAM_B2) * (g * g)
        m_hat = m / (1.0 - ADAM_B1 ** ADAM_STEP)
        v_hat = v / (1.0 - ADAM_B2 ** ADAM_STEP)
        d_ref[...] = -ADAM_LR * (m_hat / (jnp.sqrt(v_hat) + ADAM_EPS) + ADAM_WD * w_ref[...])
        nm_ref[...] = m
        nv_ref[...] = v

    blk = pl.BlockSpec((tr, cols), lambda i: (i, 0))
    return pl.pallas_call(
        body, name=name, grid=(r // tr,),
        in_specs=[blk] * 4, out_specs=[blk] * 3,
        out_shape=[jax.ShapeDtypeStruct((r, cols), F32)] * 3,
        compiler_params=_params("arbitrary"),
    )(w, g, m, v)


ANY = pl.BlockSpec(memory_space=pl.ANY)


def _place():
    x, y, c = lax.axis_index("x"), lax.axis_index("y"), lax.axis_index("c")
    other_chips = [(1 - x, y), (x, 1 - y), (1 - x, 1 - y)]
    return x, y, c, other_chips


def _chip(xy):
    return 2 * xy[0] + xy[1]


def _remote(src, dst, send_sem, recv_sem, to):
    return pltpu.make_async_remote_copy(src_ref=src, dst_ref=dst, send_sem=send_sem, recv_sem=recv_sem,
                                        device_id=to, device_id_type=MESH)


def _all_gather_weights(shards, small):
    n = len(shards)

    def body(*refs):
        ins, small_in = refs[:n], refs[n]
        outs, small_out = refs[n + 1:2 * n + 1], refs[2 * n + 1]
        send_sems, recv_sems, local_sems = refs[2 * n + 2:]
        x, y, c, chips = _place()
        me = _chip((x, y))
        sibling = (x, y, 1 - c)

        local = [pltpu.make_async_copy(ins[k], outs[k].at[me], local_sems.at[k]) for k in range(n)]
        local.append(pltpu.make_async_copy(small_in, small_out.at[me], local_sems.at[n]))
        for cp in local:
            cp.start()

        def ici(k, j, start):
            src = ins[k].at[c] if start else outs[k].at[_chip(chips[j]), c]
            dst = outs[k].at[me, c] if start else outs[k].at[_chip(chips[j]), c]
            return _remote(src, dst, send_sems.at[6 * k + j], recv_sems.at[6 * k + j], (*chips[j], c))

        def d2d(k, j, start):
            half = c if start else 1 - c
            blk = outs[k].at[_chip(chips[j]), half]
            return _remote(blk, blk, send_sems.at[6 * k + 3 + j], recv_sems.at[6 * k + 3 + j], sibling)

        def small_copy(j, start):
            dst = small_out.at[me] if start else small_out.at[_chip(chips[j])]
            return _remote(small_in, dst, send_sems.at[6 * n + j], recv_sems.at[6 * n + j], (*chips[j], c))

        for j in range(3):
            small_copy(j, True).start()
        for k in range(n):
            for j in range(3):
                ici(k, j, True).start()
        for k in range(n):
            for j in range(3):
                ici(k, j, False).wait_recv()
                d2d(k, j, True).start()
        for j in range(3):
            small_copy(j, False).wait_recv()
            small_copy(j, True).wait_send()
        for k in range(n):
            for j in range(3):
                d2d(k, j, False).wait_recv()
                ici(k, j, True).wait_send()
                d2d(k, j, True).wait_send()
        for cp in local:
            cp.wait()

    out_shape = [jax.ShapeDtypeStruct((N_SHARDS,) + a.shape, a.dtype) for a in shards]
    out_shape.append(jax.ShapeDtypeStruct((N_SHARDS,) + small.shape, small.dtype))
    return pl.pallas_call(
        body, name="all_gather_weights", in_specs=[ANY] * (n + 1), out_specs=[ANY] * (n + 1), out_shape=out_shape,
        scratch_shapes=[pltpu.SemaphoreType.DMA((6 * n + 3,)), pltpu.SemaphoreType.DMA((6 * n + 3,)),
                        pltpu.SemaphoreType.DMA((n + 1,))],
    )(*shards, small)


def _exchange_core_halves(grads):
    n = len(grads)

    def body(*refs):
        ins, outs = refs[:n], refs[n:2 * n]
        send_sems, recv_sems, local_sems = refs[2 * n:]
        x, y, c, _ = _place()
        sibling = (x, y, 1 - c)
        copies = []
        for k in range(n):
            for s in range(N_SHARDS):
                q = N_SHARDS * k + s
                loc = pltpu.make_async_copy(ins[k].at[s, c], outs[k].at[0, s], local_sems.at[q])
                rem = _remote(ins[k].at[s, 1 - c], outs[k].at[1, s], send_sems.at[q], recv_sems.at[q], sibling)
                loc.start()
                rem.start()
                copies.append((loc, rem))
        for loc, rem in copies:
            rem.wait()
            loc.wait()

    out_shape = [jax.ShapeDtypeStruct((2, N_SHARDS) + a.shape[2:], a.dtype) for a in grads]
    return pl.pallas_call(
        body, name="exchange_core_halves", in_specs=[ANY] * n, out_specs=[ANY] * n, out_shape=out_shape,
        scratch_shapes=[pltpu.SemaphoreType.DMA((N_SHARDS * n,))] * 3,
    )(*grads)


def _scatter_chip_sums(chip_sums, small):
    n = len(chip_sums)

    def body(*refs):
        ins, small_in = refs[:n], refs[n]
        outs, small_out = refs[n + 1:2 * n + 1], refs[2 * n + 1]
        send_sems, recv_sems, local_sems = refs[2 * n + 2:]
        x, y, c, chips = _place()
        me = _chip((x, y))
        dev = 2 * me + c

        local = [pltpu.make_async_copy(ins[k].at[me], outs[k].at[me], local_sems.at[k]) for k in range(n)]
        local.append(pltpu.make_async_copy(small_in, small_out.at[dev], local_sems.at[n]))
        for cp in local:
            cp.start()

        def big(k, j, start):
            src = ins[k].at[_chip(chips[j])]
            dst = outs[k].at[me] if start else outs[k].at[_chip(chips[j])]
            return _remote(src, dst, send_sems.at[3 * k + j], recv_sems.at[3 * k + j], (*chips[j], c))

        flips = [(fx, fy, fc) for fx in (0, 1) for fy in (0, 1) for fc in (0, 1)][1:]

        def small_copy(q, start):
            fx, fy, fc = flips[q]
            px, py, pc = x ^ fx, y ^ fy, c ^ fc
            dst = small_out.at[dev] if start else small_out.at[4 * px + 2 * py + pc]
            return _remote(small_in, dst, send_sems.at[3 * n + q], recv_sems.at[3 * n + q], (px, py, pc))

        for q in range(len(flips)):
            small_copy(q, True).start()
        for k in range(n):
            for j in range(3):
                big(k, j, True).start()
        for q in range(len(flips)):
            small_copy(q, False).wait_recv()
            small_copy(q, True).wait_send()
        for k in range(n):
            for j in range(3):
                big(k, j, False).wait_recv()
                big(k, j, True).wait_send()
        for cp in local:
            cp.wait()

    out_shape = [jax.ShapeDtypeStruct(a.shape, a.dtype) for a in chip_sums]
    out_shape.append(jax.ShapeDtypeStruct((N_DEVICES,) + small.shape, small.dtype))
    return pl.pallas_call(
        body, name="scatter_chip_sums", in_specs=[ANY] * (n + 1), out_specs=[ANY] * (n + 1), out_shape=out_shape,
        scratch_shapes=[pltpu.SemaphoreType.DMA((3 * n + 7,)), pltpu.SemaphoreType.DMA((3 * n + 7,)),
                        pltpu.SemaphoreType.DMA((n + 1,))],
    )(*chip_sums, small)


def _join_core_halves(halves):
    n = len(halves)

    def body(*refs):
        ins, outs = refs[:n], refs[n:2 * n]
        send_sems, recv_sems, local_sems = refs[2 * n:]
        x, y, c, _ = _place()
        sibling = (x, y, 1 - c)
        copies = []
        for k in range(n):
            loc = pltpu.make_async_copy(ins[k], outs[k].at[c], local_sems.at[k])
            loc.start()
            send = _remote(ins[k], outs[k].at[c], send_sems.at[k], recv_sems.at[k], sibling)
            send.start()
            recv = _remote(ins[k], outs[k].at[1 - c], send_sems.at[k], recv_sems.at[k], sibling)
            copies.append((loc, send, recv))
        for loc, send, recv in copies:
            recv.wait_recv()
            send.wait_send()
            loc.wait()

    out_shape = [jax.ShapeDtypeStruct((2,) + a.shape, a.dtype) for a in halves]
    return pl.pallas_call(
        body, name="join_core_halves", in_specs=[ANY] * n, out_specs=[ANY] * n, out_shape=out_shape,
        scratch_shapes=[pltpu.SemaphoreType.DMA((n,))] * 3,
    )(*halves)


def _flat_rows(parts):
    flat = jnp.concatenate([p.reshape(-1) for p in parts])
    assert flat.shape[0] % LANES == 0
    return flat.reshape(-1, LANES)


def _unflatten(flat, shapes):
    out, at = [], 0
    for sh in shapes:
        size = 1
        for dim in sh:
            size *= dim
        out.append(flat[at:at + size].reshape(sh))
        at += size
    assert at == flat.shape[0], (at, flat.shape)
    return out


def _col_shards_to_full(a, rows):
    q = a.shape[1] // rows
    return a.reshape(N_SHARDS, rows, q).transpose(1, 0, 2).reshape(rows, N_SHARDS * q)


def _my_col_shard(full, chip):
    rows, cols = full.shape
    q = cols // N_SHARDS
    return lax.dynamic_index_in_dim(full.reshape(rows, N_SHARDS, q), chip, axis=1, keepdims=False)


def kernel(x, e_norm_pre, e_norm_post, e_w_in, e_a_conv, e_b_conv, e_b_conv_bias, e_b_ln_g, e_b_ln_b, e_w_out, o_norm_pre, o_norm_post, o_w_in, o_c_w, o_c_b, o_c_scale, o_w_out, loss_target, m_e_norm_pre, m_e_norm_post, m_e_w_in, m_e_a_conv, m_e_b_conv, m_e_b_conv_bias, m_e_b_ln_g, m_e_b_ln_b, m_e_w_out, m_o_norm_pre, m_o_norm_post, m_o_w_in, m_o_c_w, m_o_c_b, m_o_c_scale, m_o_w_out, v_e_norm_pre, v_e_norm_post, v_e_w_in, v_e_a_conv, v_e_b_conv, v_e_b_conv_bias, v_e_b_ln_g, v_e_b_ln_b, v_e_w_out, v_o_norm_pre, v_o_norm_post, v_o_w_in, v_o_c_w, v_o_c_b, v_o_c_scale, v_o_w_out):
    _, s, d = x.shape
    w = d // 2
    c = d
    gc = c // N_GROUPS
    wq, cq, gq = w // N_SHARDS, c // N_SHARDS, gc // N_SHARDS
    chip = 2 * lax.axis_index("x") + lax.axis_index("y")
    x2 = x.reshape(s, d)
    target = loss_target.reshape(s, d)

    big_w = [e_w_in[0], e_w_out[0], o_w_in[0], o_c_w[0].reshape(N_GROUPS * gq, gc), o_w_out[0]]
    big_m = [m_e_w_in[0], m_e_w_out[0], m_o_w_in[0], m_o_c_w[0].reshape(N_GROUPS * gq, gc), m_o_w_out[0]]
    big_v = [v_e_w_in[0], v_e_w_out[0], v_o_w_in[0], v_o_c_w[0].reshape(N_GROUPS * gq, gc), v_o_w_out[0]]
    halves = [a.astype(BF16).reshape(2, a.shape[0] // 2, a.shape[1]) for a in big_w]
    sharded_small = [e_a_conv[0], e_b_conv[0], o_norm_pre, o_norm_post, o_c_scale, o_c_b[0]]
    gathered = _all_gather_weights(halves, _flat_rows(sharded_small))
    full = [g.reshape(N_SHARDS, a.shape[0], a.shape[1]) for g, a in zip(gathered[:5], big_w)]
    e_w_in_sm, o_w_in_sm = full[0], full[2]
    e_w_out_f = full[1].reshape(w + w, d)
    o_w_out_f = full[4].reshape(c, d)
    cw_f = full[3].reshape(N_SHARDS, N_GROUPS, gq, gc).transpose(1, 0, 2, 3).reshape(N_GROUPS, gc, gc)
    sm = gathered[5].reshape(N_SHARDS, -1)
    at = [0]

    def take(rows, q):
        blk = sm[:, at[0]:at[0] + rows * q]
        at[0] += rows * q
        return _col_shards_to_full(blk, rows)

    a_conv_f = take(CONV_A, wq)
    b_conv_f = take(CONV_B, wq)
    o_pre_f = take(1, cq)
    o_post_f = take(1, cq)
    cs_f = take(1, cq)
    cb_f = take(N_GROUPS, gq).reshape(1, c)

    p0, h0 = _norm_matmul(x2, e_norm_pre, e_w_in_sm, "e_in_proj")
    u0 = _even_mixer_fwd(p0, a_conv_f, b_conv_f, e_b_conv_bias, e_b_ln_g, e_b_ln_b, "e_mixer_fwd")
    x1, y0 = _matmul_post(u0, e_w_out_f, x2, e_norm_post, "e_out_proj")
    p1, h1 = _norm_matmul(x1, o_pre_f, o_w_in_sm, "o_in_proj")
    u1 = _odd_mixer_fwd(p1, cw_f, cb_f, cs_f, "o_mixer_fwd")
    d_y1, d_x2, d_o_post, loss_part = _matmul_post_loss(u1, o_w_out_f, x1, o_post_f, target, "o_out_proj_loss")

    g_o_w_out = _matmul_tn(u1, d_y1, 1, "o_w_out_grad")
    d_u1 = _matmul_nt(d_y1, o_w_out_f[None], "o_out_proj_bwd")
    d_p1, d_cw, d_cb, d_cs = _odd_mixer_bwd(p1, d_u1, cw_f, cb_f, cs_f, "o_mixer_bwd")
    g_o_w_in = _matmul_tn(h1, d_p1, N_SHARDS, "o_w_in_grad")
    d_h1 = _matmul_nt(d_p1, o_w_in_sm, "o_in_proj_bwd")
    d_x1, d_o_pre, d_y0, d_e_post = _norm_bwd(d_h1, x1, o_pre_f, d_x2, "o_pre_norm_bwd", post=(y0, e_norm_post))

    g_e_w_out = _matmul_tn(u0, d_y0, 1, "e_w_out_grad")
    d_u0 = _matmul_nt(d_y0, e_w_out_f[None], "e_out_proj_bwd")
    d_p0, d_a_conv, d_b_conv, d_bias, d_ln_g, d_ln_b = _even_mixer_bwd(
        p0, d_u0, a_conv_f, b_conv_f, e_b_conv_bias, e_b_ln_g, e_b_ln_b, "e_mixer_bwd")
    g_e_w_in = _matmul_tn(h0, d_p0, N_SHARDS, "e_w_in_grad")
    d_h0 = _matmul_nt(d_p0, e_w_in_sm, "e_in_proj_bwd")
    grad_x, d_e_pre = _norm_bwd(d_h0, x2, e_norm_pre, d_x1, "e_pre_norm_bwd")

    g_cw = d_cw.reshape(N_GROUPS, N_SHARDS, gq, gc).transpose(1, 0, 2, 3).astype(BF16)
    partial = [g_e_w_in, g_e_w_out, g_o_w_in, g_cw, g_o_w_out]
    partial = [g.reshape(N_SHARDS, 2, a.shape[0] // 2, a.shape[1]) for g, a in zip(partial, big_w)]
    pairs = _exchange_core_halves(partial)
    chip_sums = [_sum_leading(pr.reshape(2, -1, pr.shape[-1]), BF16, "chip_sum_%d" % k).reshape(pr.shape[1:])
                 for k, pr in enumerate(pairs)]
    small_parts = [loss_part[0], d_e_pre, d_e_post, d_bias, d_ln_g, d_ln_b, d_a_conv, d_b_conv,
                   d_o_pre, d_o_post, d_cs, d_cb]
    scattered = _scatter_chip_sums(chip_sums, _flat_rows(small_parts))
    reduced = [_sum_leading(sc, F32, "shard_sum_%d" % k) for k, sc in enumerate(scattered[:5])]
    joined = _join_core_halves(reduced)
    big_g = [j.reshape(a.shape) for j, a in zip(joined, big_w)]
    small_sum = _sum_small(scattered[5], "small_sum").reshape(-1)
    (loss_row, g_e_pre, g_e_post, g_bias, g_ln_g, g_ln_b, g_a_conv_f, g_b_conv_f, g_o_pre_f, g_o_post_f, g_cs_f,
     g_cb_f) = _unflatten(small_sum, [(LANES,), (1, d), (1, d), (1, w), (1, w), (1, w), (CONV_A, w), (CONV_B, w),
                                      (1, c), (1, c), (1, c), (1, c)])
    loss = loss_row[0]
    g_a_conv = _my_col_shard(g_a_conv_f, chip)
    g_b_conv = _my_col_shard(g_b_conv_f, chip)
    g_o_pre = _my_col_shard(g_o_pre_f, chip)
    g_o_post = _my_col_shard(g_o_post_f, chip)
    g_cs = _my_col_shard(g_cs_f, chip)
    g_cb = _my_col_shard(g_cb_f.reshape(N_GROUPS, gc), chip)

    big_upd = [_adamw(wt, g, m, v, "adamw_%d" % k) for k, (wt, g, m, v) in enumerate(zip(big_w, big_g, big_m, big_v))]
    small_w = [e_norm_pre, e_norm_post, e_b_conv_bias, e_b_ln_g, e_b_ln_b, e_a_conv[0], e_b_conv[0],
               o_norm_pre, o_norm_post, o_c_b[0], o_c_scale]
    small_m = [m_e_norm_pre, m_e_norm_post, m_e_b_conv_bias, m_e_b_ln_g, m_e_b_ln_b, m_e_a_conv[0], m_e_b_conv[0],
               m_o_norm_pre, m_o_norm_post, m_o_c_b[0], m_o_c_scale]
    small_v = [v_e_norm_pre, v_e_norm_post, v_e_b_conv_bias, v_e_b_ln_g, v_e_b_ln_b, v_e_a_conv[0], v_e_b_conv[0],
               v_o_norm_pre, v_o_norm_post, v_o_c_b[0], v_o_c_scale]
    small_g = [g_e_pre, g_e_post, g_bias, g_ln_g, g_ln_b, g_a_conv, g_b_conv, g_o_pre, g_o_post, g_cb, g_cs]
    small_shapes = [a.shape for a in small_w]
    small_upd = _adamw(_flat_rows(small_w), _flat_rows(small_g), _flat_rows(small_m), _flat_rows(small_v),
                       "adamw_small")
    small_delta, small_new_m, small_new_v = [_unflatten(u.reshape(-1), small_shapes) for u in small_upd]

    def ordered(small, big, lead):
        (n_pre, n_post, bias, ln_g, ln_b, a_conv, b_conv, o_pre, o_post, cb, cs) = small
        (w_in, w_out, ow_in, cw, ow_out) = big
        out = [n_pre, n_post, w_in[None], a_conv[None], b_conv[None], bias, ln_g, ln_b, w_out[None], o_pre, o_post,
               ow_in[None], cw.reshape(1, N_GROUPS, gq, gc), cb[None], cs, ow_out[None]]
        return out

    grads = ordered(small_g, big_g, None)
    deltas = ordered(small_delta, [u[0] for u in big_upd], None)
    new_m = ordered(small_new_m, [u[1] for u in big_upd], None)
    new_v = ordered(small_new_v, [u[2] for u in big_upd], None)
    return (loss, grad_x.reshape(1, s, d), *grads, *deltas, *new_m, *new_v)
```

```python
import functools

import jax
import jax.numpy as jnp
from jax import lax
from jax.experimental import pallas as pl
from jax.experimental.pallas import tpu as pltpu

F32 = jnp.float32
BF16 = jnp.bfloat16
MESH = pl.DeviceIdType.MESH

EPS = 1e-6
CONV_A = 3
CONV_B = 31
POOL_WINDOWS = (2, 4, 8, 16)
N_GROUPS = len(POOL_WINDOWS)
N_SHARDS = 4
N_DEVICES = 8
ADAM_LR = 0.001
ADAM_B1 = 0.9
ADAM_B2 = 0.999
ADAM_EPS = 1e-08
ADAM_WD = 0.01
ADAM_STEP = 10

LANES = 128
SUBLANES_BF16 = 16
HALO_A = 8
HALO_B = 32
HALO_P = 16
VMEM_LIMIT = 56 * 1024 * 1024


def _row_tile(n, pref):
    t = min(n, pref)
    while t > SUBLANES_BF16 and (n % t or t % SUBLANES_BF16):
        t -= SUBLANES_BF16
    assert n % t == 0, (n, pref)
    return t


def _col_chunk(n, pref):
    t = (min(n, pref) // LANES) * LANES
    while t > LANES and n % t:
        t -= LANES
    assert t >= LANES and n % t == 0, (n, pref)
    return t


def _params(*sem):
    return pltpu.CompilerParams(dimension_semantics=tuple(sem) if sem else None, vmem_limit_bytes=VMEM_LIMIT)


def _sigmoid(v):
    return jax.nn.sigmoid(v)


def _dsilu(v, s):
    return s * (1.0 + v * (1.0 - s))


def _mean_last(v):
    return jnp.mean(v, axis=-1, keepdims=True)


def _sum_rows(v):
    return jnp.sum(v, axis=0, keepdims=True)


def _norm_matmul(x, gain, w_sm, name):
    s, d = x.shape
    n_sh, _, ns = w_sm.shape
    tm = _row_tile(s, 512)

    def body(x_ref, g_ref, w_ref, p_ref, h_ref):
        @pl.when(pl.program_id(1) == 0)
        def _():
            xv = x_ref[...]
            r = lax.rsqrt(_mean_last(xv * xv) + EPS)
            h_ref[...] = (xv * r * g_ref[...]).astype(BF16)

        p_ref[...] = jnp.dot(h_ref[...], w_ref[0], preferred_element_type=F32)

    return pl.pallas_call(
        body, name=name, grid=(s // tm, n_sh),
        in_specs=[pl.BlockSpec((tm, d), lambda i, j: (i, 0)),
                  pl.BlockSpec((1, d), lambda i, j: (0, 0)),
                  pl.BlockSpec((1, d, ns), lambda i, j: (j, 0, 0))],
        out_specs=[pl.BlockSpec((tm, ns), lambda i, j: (i, j)),
                   pl.BlockSpec((tm, d), lambda i, j: (i, 0))],
        out_shape=[jax.ShapeDtypeStruct((s, n_sh * ns), F32), jax.ShapeDtypeStruct((s, d), BF16)],
        compiler_params=_params("arbitrary", "arbitrary"),
    )(x, gain, w_sm)


def _matmul_post(u, w, x_res, gain, name):
    s, k = u.shape
    d = w.shape[1]
    tm = _row_tile(s, 256)

    def body(u_ref, w_ref, x_ref, g_ref, xn_ref, y_ref):
        y = jnp.dot(u_ref[...], w_ref[...], preferred_element_type=F32)
        r = lax.rsqrt(_mean_last(y * y) + EPS)
        y_ref[...] = y
        xn_ref[...] = x_ref[...] + (y * r) * g_ref[...]

    return pl.pallas_call(
        body, name=name, grid=(s // tm,),
        in_specs=[pl.BlockSpec((tm, k), lambda i: (i, 0)),
                  pl.BlockSpec((k, d), lambda i: (0, 0)),
                  pl.BlockSpec((tm, d), lambda i: (i, 0)),
                  pl.BlockSpec((1, d), lambda i: (0, 0))],
        out_specs=[pl.BlockSpec((tm, d), lambda i: (i, 0)), pl.BlockSpec((tm, d), lambda i: (i, 0))],
        out_shape=[jax.ShapeDtypeStruct((s, d), F32), jax.ShapeDtypeStruct((s, d), F32)],
        compiler_params=_params("arbitrary"),
    )(u, w, x_res, gain)


def _matmul_post_loss(u, w, x_res, gain, target, name):
    s, k = u.shape
    d = w.shape[1]
    tm = _row_tile(s, 256)

    def body(u_ref, w_ref, x_ref, g_ref, t_ref, dy_ref, dout_ref, dg_ref, loss_ref):
        @pl.when(pl.program_id(0) == 0)
        def _():
            dg_ref[...] = jnp.zeros_like(dg_ref)
            loss_ref[...] = jnp.zeros_like(loss_ref)

        y = jnp.dot(u_ref[...], w_ref[...], preferred_element_type=F32)
        r = lax.rsqrt(_mean_last(y * y) + EPS)
        n = y * r
        g = g_ref[...]
        err = x_ref[...] + n * g - t_ref[...]
        loss_ref[...] += 0.5 * jnp.sum(_mean_last(err * err))
        dout = err * (1.0 / d)
        dout_ref[...] = dout
        dg_ref[...] += _sum_rows(dout * n)
        dn = dout * g
        dy_ref[...] = (r * (dn - n * _mean_last(dn * n))).astype(BF16)

    return pl.pallas_call(
        body, name=name, grid=(s // tm,),
        in_specs=[pl.BlockSpec((tm, k), lambda i: (i, 0)),
                  pl.BlockSpec((k, d), lambda i: (0, 0)),
                  pl.BlockSpec((tm, d), lambda i: (i, 0)),
                  pl.BlockSpec((1, d), lambda i: (0, 0)),
                  pl.BlockSpec((tm, d), lambda i: (i, 0))],
        out_specs=[pl.BlockSpec((tm, d), lambda i: (i, 0)),
                   pl.BlockSpec((tm, d), lambda i: (i, 0)),
                   pl.BlockSpec((1, d), lambda i: (0, 0)),
                   pl.BlockSpec((8, LANES), lambda i: (0, 0))],
        out_shape=[jax.ShapeDtypeStruct((s, d), BF16), jax.ShapeDtypeStruct((s, d), F32),
                   jax.ShapeDtypeStruct((1, d), F32), jax.ShapeDtypeStruct((8, LANES), F32)],
        compiler_params=_params("arbitrary"),
    )(u, w, x_res, gain, target)


def _matmul_nt(a, w_sm, name):
    s, ncols = a.shape
    n_sh, r, ns = w_sm.shape
    assert ncols == n_sh * ns
    tm = _row_tile(s, 512)
    nc = _col_chunk(ns, 1024)
    per = ns // nc
    steps = n_sh * per

    def body(a_ref, w_ref, o_ref):
        part = lax.dot_general(a_ref[...], w_ref[0], (((1,), (1,)), ((), ())), preferred_element_type=F32)

        @pl.when(pl.program_id(1) == 0)
        def _():
            o_ref[...] = part

        @pl.when(pl.program_id(1) > 0)
        def _():
            o_ref[...] += part

    return pl.pallas_call(
        body, name=name, grid=(s // tm, steps),
        in_specs=[pl.BlockSpec((tm, nc), lambda i, j: (i, j)),
                  pl.BlockSpec((1, r, nc), lambda i, j: (j // per, 0, j % per))],
        out_specs=pl.BlockSpec((tm, r), lambda i, j: (i, 0)),
        out_shape=jax.ShapeDtypeStruct((s, r), F32),
        compiler_params=_params("arbitrary", "arbitrary"),
    )(a, w_sm)


def _matmul_tn(a, b, n_sh, name):
    s, k = a.shape
    n = b.shape[1]
    ns = n // n_sh
    tk = _row_tile(k, 1024)
    ts = _row_tile(s, 512)
    n_s = s // ts

    def body(a_ref, b_ref, o_ref, acc_ref):
        part = lax.dot_general(a_ref[...], b_ref[...], (((0,), (0,)), ((), ())), preferred_element_type=F32)

        @pl.when(pl.program_id(2) == 0)
        def _():
            acc_ref[...] = part

        @pl.when(pl.program_id(2) > 0)
        def _():
            acc_ref[...] += part

        @pl.when(pl.program_id(2) == n_s - 1)
        def _():
            o_ref[0] = acc_ref[...].astype(BF16)

    return pl.pallas_call(
        body, name=name, grid=(n_sh, k // tk, n_s),
        in_specs=[pl.BlockSpec((ts, tk), lambda j, i, t: (t, i)),
                  pl.BlockSpec((ts, ns), lambda j, i, t: (t, j))],
        out_specs=pl.BlockSpec((1, tk, ns), lambda j, i, t: (j, i, 0)),
        out_shape=jax.ShapeDtypeStruct((n_sh, k, ns), BF16),
        scratch_shapes=[pltpu.VMEM((tk, ns), F32)],
        compiler_params=_params("arbitrary", "arbitrary", "arbitrary"),
    )(a, b)


def _norm_bwd(dh, x, gain, dres, name, post=None):
    s, d = x.shape
    tm = _row_tile(s, 256)
    with_post = post is not None

    def rms_bwd(dout, v, g):
        r = lax.rsqrt(_mean_last(v * v) + EPS)
        n = v * r
        dn = dout * g
        return r * (dn - n * _mean_last(dn * n)), _sum_rows(dout * n)

    def body(*refs):
        if with_post:
            dh_ref, x_ref, g_ref, dres_ref, y_ref, gp_ref, dx_ref, dg_ref, dy_ref, dgp_ref = refs
        else:
            dh_ref, x_ref, g_ref, dres_ref, dx_ref, dg_ref = refs

        @pl.when(pl.program_id(0) == 0)
        def _():
            dg_ref[...] = jnp.zeros_like(dg_ref)
            if with_post:
                dgp_ref[...] = jnp.zeros_like(dgp_ref)

        dv, dg = rms_bwd(dh_ref[...], x_ref[...], g_ref[...])
        dx = dres_ref[...] + dv
        dx_ref[...] = dx
        dg_ref[...] += dg
        if with_post:
            dy, dgp = rms_bwd(dx, y_ref[...], gp_ref[...])
            dy_ref[...] = dy.astype(BF16)
            dgp_ref[...] += dgp

    row = pl.BlockSpec((tm, d), lambda i: (i, 0))
    vec = pl.BlockSpec((1, d), lambda i: (0, 0))
    in_specs = [row, row, vec, row]
    out_specs = [row, vec]
    out_shape = [jax.ShapeDtypeStruct((s, d), F32), jax.ShapeDtypeStruct((1, d), F32)]
    args = [dh, x, gain, dres]
    if with_post:
        in_specs += [row, vec]
        out_specs += [row, vec]
        out_shape += [jax.ShapeDtypeStruct((s, d), BF16), jax.ShapeDtypeStruct((1, d), F32)]
        args += list(post)
    return pl.pallas_call(
        body, name=name, grid=(s // tm,), in_specs=in_specs, out_specs=out_specs, out_shape=out_shape,
        compiler_params=_params("arbitrary"),
    )(*args)


def _even_forward_tile(p_ref, halo_ref, first, a_conv_ref, b_conv_ref, bias_ref, lng_ref, lnb_ref, qbuf, ybuf, w, ts):
    def col(ref, k, rows=slice(None)):
        return ref[rows, k * w:(k + 1) * w]

    a_x, a_b, a_c, a_z = col(p_ref, 0), col(p_ref, 1), col(p_ref, 2), col(p_ref, 3)
    b_val, b_gate, b_z = col(p_ref, 4), col(p_ref, 5), col(p_ref, 6)
    keep = jnp.where(first, 0.0, 1.0)

    rows_a = slice(HALO_B - HALO_A, HALO_B)
    qbuf[0:HALO_A, :] = col(halo_ref, 2, rows_a) * col(halo_ref, 0, rows_a) * keep
    qbuf[HALO_A:HALO_A + ts, :] = a_c * a_x
    cq = jnp.zeros((ts, w), F32)
    for j in range(CONV_A):
        cq = cq + a_conv_ref[CONV_A - 1 - j:CONV_A - j, :] * qbuf[HALO_A - j:HALO_A - j + ts, :]
    ya = a_b * cq

    ybuf[0:HALO_B, :] = col(halo_ref, 4) * _sigmoid(col(halo_ref, 5)) * keep
    ybuf[HALO_B:HALO_B + ts, :] = b_val * _sigmoid(b_gate)
    yb1 = jnp.zeros((ts, w), F32) + bias_ref[...]
    for j in range(CONV_B):
        yb1 = yb1 + b_conv_ref[CONV_B - 1 - j:CONV_B - j, :] * ybuf[HALO_B - j:HALO_B - j + ts, :]
    xc = yb1 - _mean_last(yb1)
    rstd = lax.rsqrt(_mean_last(xc * xc) + EPS)
    xhat = xc * rstd
    yb2 = xhat * lng_ref[...] + lnb_ref[...]
    return dict(a_x=a_x, a_b=a_b, a_c=a_c, a_z=a_z, b_val=b_val, b_gate=b_gate, b_z=b_z,
                cq=cq, ya=ya, rstd=rstd, xhat=xhat, yb2=yb2)


def _even_specs(s, w, ts):
    tile = pl.BlockSpec((ts, 7 * w), lambda i: (i, 0))
    halo = pl.BlockSpec((HALO_B, 7 * w), lambda i: (jnp.maximum(i * (ts // HALO_B) - 1, 0), 0))
    return tile, halo


def _small_specs(shapes, index=lambda i: (0, 0)):
    return [pl.BlockSpec(sh, index) for sh in shapes]


def _even_mixer_fwd(p, a_conv, b_conv, bias, ln_g, ln_b, name):
    s = p.shape[0]
    w = p.shape[1] // 7
    ts = _row_tile(s, 256)
    assert ts % HALO_B == 0

    def body(p_ref, halo_ref, ac_ref, bc_ref, bias_ref, lng_ref, lnb_ref, u_ref, qbuf, ybuf):
        first = pl.program_id(0) == 0
        f = _even_forward_tile(p_ref, halo_ref, first, ac_ref, bc_ref, bias_ref, lng_ref, lnb_ref, qbuf, ybuf, w, ts)
        yb3 = f["yb2"] * _sigmoid(f["yb2"])
        u_ref[:, 0:w] = (f["ya"] * (f["a_z"] * _sigmoid(f["a_z"]))).astype(BF16)
        u_ref[:, w:2 * w] = (yb3 * (f["b_z"] * _sigmoid(f["b_z"]))).astype(BF16)

    tile, halo = _even_specs(s, w, ts)
    return pl.pallas_call(
        body, name=name, grid=(s // ts,),
        in_specs=[tile, halo] + _small_specs([(CONV_A, w), (CONV_B, w), (1, w), (1, w), (1, w)]),
        out_specs=pl.BlockSpec((ts, 2 * w), lambda i: (i, 0)),
        out_shape=jax.ShapeDtypeStruct((s, 2 * w), BF16),
        scratch_shapes=[pltpu.VMEM((HALO_A + ts, w), F32), pltpu.VMEM((HALO_B + ts, w), F32)],
        compiler_params=_params("arbitrary"),
    )(p, p, a_conv, b_conv, bias, ln_g, ln_b)


def _even_mixer_bwd(p, du, a_conv, b_conv, bias, ln_g, ln_b, name):
    s = p.shape[0]
    w = p.shape[1] // 7
    ts = _row_tile(s, 256)
    nt = s // ts
    assert ts % HALO_B == 0

    def body(p_ref, halo_ref, du_ref, ac_ref, bc_ref, bias_ref, lng_ref, lnb_ref,
             dp_ref, dac_ref, dbc_ref, dbias_ref, dlng_ref, dlnb_ref, qbuf, ybuf, dqbuf, dybuf, carry_dq, carry_dy):
        step = pl.program_id(0)
        first = step == nt - 1

        @pl.when(step == 0)
        def _():
            for ref in (dac_ref, dbc_ref, dbias_ref, dlng_ref, dlnb_ref, carry_dq, carry_dy):
                ref[...] = jnp.zeros_like(ref)

        f = _even_forward_tile(p_ref, halo_ref, first, ac_ref, bc_ref, bias_ref, lng_ref, lnb_ref, qbuf, ybuf, w, ts)
        a_z, b_z, yb2 = f["a_z"], f["b_z"], f["yb2"]
        s_az, s_bz, s_y2 = _sigmoid(a_z), _sigmoid(b_z), _sigmoid(yb2)
        du_a = du_ref[:, 0:w]
        du_b = du_ref[:, w:2 * w]

        d_ya = du_a * (a_z * s_az)
        dp_ref[:, 3 * w:4 * w] = (du_a * f["ya"] * _dsilu(a_z, s_az)).astype(BF16)
        dp_ref[:, 1 * w:2 * w] = (d_ya * f["cq"]).astype(BF16)
        d_cq = d_ya * f["a_b"]
        dqbuf[0:ts, :] = d_cq
        dqbuf[ts:ts + HALO_A, :] = carry_dq[...]
        d_q = jnp.zeros((ts, w), F32)
        for o in range(CONV_A):
            d_q = d_q + ac_ref[CONV_A - 1 - o:CONV_A - o, :] * dqbuf[o:o + ts, :]
        for j in range(CONV_A):
            k = CONV_A - 1 - j
            dac_ref[k:k + 1, :] += _sum_rows(d_cq * qbuf[HALO_A - j:HALO_A - j + ts, :])
        carry_dq[...] = d_cq[0:HALO_A, :]
        dp_ref[:, 2 * w:3 * w] = (d_q * f["a_x"]).astype(BF16)
        dp_ref[:, 0 * w:1 * w] = (d_q * f["a_c"]).astype(BF16)

        d_yb3 = du_b * (b_z * s_bz)
        dp_ref[:, 6 * w:7 * w] = (du_b * (yb2 * s_y2) * _dsilu(b_z, s_bz)).astype(BF16)
        d_yb2 = d_yb3 * _dsilu(yb2, s_y2)
        xhat = f["xhat"]
        dlng_ref[...] += _sum_rows(d_yb2 * xhat)
        dlnb_ref[...] += _sum_rows(d_yb2)
        d_xh = d_yb2 * lng_ref[...]
        d_yb1 = f["rstd"] * (d_xh - _mean_last(d_xh) - xhat * _mean_last(d_xh * xhat))
        dbias_ref[...] += _sum_rows(d_yb1)
        dybuf[0:ts, :] = d_yb1
        dybuf[ts:ts + HALO_B, :] = carry_dy[...]
        d_yb0 = jnp.zeros((ts, w), F32)
        for o in range(CONV_B):
            d_yb0 = d_yb0 + bc_ref[CONV_B - 1 - o:CONV_B - o, :] * dybuf[o:o + ts, :]
        for j in range(CONV_B):
            k = CONV_B - 1 - j
            dbc_ref[k:k + 1, :] += _sum_rows(d_yb1 * ybuf[HALO_B - j:HALO_B - j + ts, :])
        carry_dy[...] = d_yb1[0:HALO_B, :]
        s_g = _sigmoid(f["b_gate"])
        dp_ref[:, 4 * w:5 * w] = (d_yb0 * s_g).astype(BF16)
        dp_ref[:, 5 * w:6 * w] = (d_yb0 * f["b_val"] * s_g * (1.0 - s_g)).astype(BF16)

    rev = lambda i: (nt - 1 - i, 0)
    tile = pl.BlockSpec((ts, 7 * w), rev)
    halo = pl.BlockSpec((HALO_B, 7 * w), lambda i: (jnp.maximum((nt - 1 - i) * (ts // HALO_B) - 1, 0), 0))
    small = [(CONV_A, w), (CONV_B, w), (1, w), (1, w), (1, w)]
    return pl.pallas_call(
        body, name=name, grid=(nt,),
        in_specs=[tile, halo, pl.BlockSpec((ts, 2 * w), rev)] + _small_specs(small),
        out_specs=[pl.BlockSpec((ts, 7 * w), rev)] + _small_specs(small),
        out_shape=[jax.ShapeDtypeStruct((s, 7 * w), BF16)] + [jax.ShapeDtypeStruct(sh, F32) for sh in small],
        scratch_shapes=[pltpu.VMEM((HALO_A + ts, w), F32), pltpu.VMEM((HALO_B + ts, w), F32),
                        pltpu.VMEM((ts + HALO_A, w), F32), pltpu.VMEM((ts + HALO_B, w), F32),
                        pltpu.VMEM((HALO_A, w), F32), pltpu.VMEM((HALO_B, w), F32)],
        compiler_params=_params("arbitrary"),
    )(p, p, du, a_conv, b_conv, bias, ln_g, ln_b)


def _pool_forward_tile(p_ref, halo_ref, first, tile_index, cw_ref, cb_ref, cs_ref, vbuf, c, gc, ts):
    vbuf[0:HALO_P, :] = halo_ref[...] * jnp.where(first, 0.0, 1.0)
    vbuf[HALO_P:HALO_P + ts, :] = p_ref[:, 0:c]
    pos = tile_index * ts + lax.broadcasted_iota(jnp.int32, (ts, 1), 0) + 1
    pooled, inv, gout = [], [], []
    for g, win in enumerate(POOL_WINDOWS):
        cols = slice(g * gc, (g + 1) * gc)
        acc = jnp.zeros((ts, gc), F32)
        for j in range(win):
            acc = acc + vbuf[HALO_P - j:HALO_P - j + ts, cols]
        inv_g = 1.0 / jnp.minimum(pos, win).astype(F32)
        pooled_g = (acc * inv_g - p_ref[:, cols]).astype(BF16)
        pooled.append(pooled_g)
        inv.append(inv_g)
        gout.append(jnp.dot(pooled_g, cw_ref[g], preferred_element_type=F32) + cb_ref[:, cols])
    return pooled, inv, gout


def _odd_mixer_fwd(p, cw, cb, cs, name):
    s = p.shape[0]
    c = p.shape[1] // 2
    gc = c // N_GROUPS
    ts = _row_tile(s, 256)

    def body(p_ref, halo_ref, cw_ref, cb_ref, cs_ref, u_ref, vbuf):
        i = pl.program_id(0)
        _, _, gout = _pool_forward_tile(p_ref, halo_ref, i == 0, i, cw_ref, cb_ref, cs_ref, vbuf, c, gc, ts)
        for g in range(N_GROUPS):
            cols = slice(g * gc, (g + 1) * gc)
            z = p_ref[:, c + g * gc:c + (g + 1) * gc]
            u_ref[:, cols] = (gout[g] * cs_ref[:, cols] * (z * _sigmoid(z))).astype(BF16)

    return pl.pallas_call(
        body, name=name, grid=(s // ts,),
        in_specs=[pl.BlockSpec((ts, 2 * c), lambda i: (i, 0)),
                  pl.BlockSpec((HALO_P, c), lambda i: (jnp.maximum(i * (ts // HALO_P) - 1, 0), 0)),
                  pl.BlockSpec((N_GROUPS, gc, gc), lambda i: (0, 0, 0)),
                  pl.BlockSpec((1, c), lambda i: (0, 0)), pl.BlockSpec((1, c), lambda i: (0, 0))],
        out_specs=pl.BlockSpec((ts, c), lambda i: (i, 0)),
        out_shape=jax.ShapeDtypeStruct((s, c), BF16),
        scratch_shapes=[pltpu.VMEM((HALO_P + ts, c), F32)],
        compiler_params=_params("arbitrary"),
    )(p, p, cw, cb, cs)


def _odd_mixer_bwd(p, du, cw, cb, cs, name):
    s = p.shape[0]
    c = p.shape[1] // 2
    gc = c // N_GROUPS
    ts = _row_tile(s, 256)
    nt = s // ts

    def body(p_ref, halo_ref, du_ref, cw_ref, cb_ref, cs_ref, dp_ref, dcw_ref, dcb_ref, dcs_ref, vbuf, ebuf, carry_e):
        step = pl.program_id(0)
        tile_index = nt - 1 - step

        @pl.when(step == 0)
        def _():
            for ref in (dcw_ref, dcb_ref, dcs_ref, carry_e):
                ref[...] = jnp.zeros_like(ref)

        pooled, inv, gout = _pool_forward_tile(p_ref, halo_ref, tile_index == 0, tile_index, cw_ref, cb_ref, cs_ref,
                                               vbuf, c, gc, ts)
        ebuf[ts:ts + HALO_P, :] = carry_e[...]
        for g, win in enumerate(POOL_WINDOWS):
            cols = slice(g * gc, (g + 1) * gc)
            z = p_ref[:, c + g * gc:c + (g + 1) * gc]
            sz = _sigmoid(z)
            du_g = du_ref[:, cols]
            scale = cs_ref[:, cols]
            d_y = du_g * (z * sz)
            dp_ref[:, c + g * gc:c + (g + 1) * gc] = (du_g * (gout[g] * scale) * _dsilu(z, sz)).astype(BF16)
            dcs_ref[:, cols] += _sum_rows(d_y * gout[g])
            d_gout = d_y * scale
            dcb_ref[:, cols] += _sum_rows(d_gout)
            d_gout_b = d_gout.astype(BF16)
            dcw_ref[g] += lax.dot_general(pooled[g], d_gout_b, (((0,), (0,)), ((), ())), preferred_element_type=F32)
            d_pool = lax.dot_general(d_gout_b, cw_ref[g], (((1,), (1,)), ((), ())), preferred_element_type=F32)
            e = d_pool * inv[g]
            ebuf[0:ts, cols] = e
            d_v = -d_pool
            for o in range(win):
                d_v = d_v + ebuf[o:o + ts, cols]
            dp_ref[:, cols] = d_v.astype(BF16)
            carry_e[:, cols] = e[0:HALO_P, :]

    rev = lambda i: (nt - 1 - i, 0)
    small = [(N_GROUPS, gc, gc), (1, c), (1, c)]
    return pl.pallas_call(
        body, name=name, grid=(nt,),
        in_specs=[pl.BlockSpec((ts, 2 * c), rev),
                  pl.BlockSpec((HALO_P, c), lambda i: (jnp.maximum((nt - 1 - i) * (ts // HALO_P) - 1, 0), 0)),
                  pl.BlockSpec((ts, c), rev),
                  pl.BlockSpec((N_GROUPS, gc, gc), lambda i: (0, 0, 0)),
                  pl.BlockSpec((1, c), lambda i: (0, 0)), pl.BlockSpec((1, c), lambda i: (0, 0))],
        out_specs=[pl.BlockSpec((ts, 2 * c), rev),
                   pl.BlockSpec((N_GROUPS, gc, gc), lambda i: (0, 0, 0)),
                   pl.BlockSpec((1, c), lambda i: (0, 0)), pl.BlockSpec((1, c), lambda i: (0, 0))],
        out_shape=[jax.ShapeDtypeStruct((s, 2 * c), BF16)] + [jax.ShapeDtypeStruct(sh, F32) for sh in small],
        scratch_shapes=[pltpu.VMEM((HALO_P + ts, c), F32), pltpu.VMEM((ts + HALO_P, c), F32),
                        pltpu.VMEM((HALO_P, c), F32)],
        compiler_params=_params("arbitrary"),
    )(p, p, du, cw, cb, cs)


def _cast_into_slot(a, coords, name):
    r, cols = a.shape
    tr = _row_tile(r // 2, 256)
    per = r // 2 // tr

    def body(co_ref, a_ref, o_ref):
        o_ref[0, 0] = a_ref[...].astype(BF16)

    return pl.pallas_call(
        body, name=name,
        grid_spec=pltpu.PrefetchScalarGridSpec(
            num_scalar_prefetch=1, grid=(2, per),
            in_specs=[pl.BlockSpec((tr, cols), lambda h, i, co: (h * per + i, 0))],
            out_specs=pl.BlockSpec((1, 1, tr, cols), lambda h, i, co: (co[0], h, i, 0))),
        out_shape=jax.ShapeDtypeStruct((N_SHARDS, 2, r // 2, cols), BF16),
        compiler_params=_params("arbitrary", "arbitrary"),
    )(coords, a)


def _chip_sum(g, other, coords, name):
    n_sh, _, r2, cols = g.shape
    tr = _row_tile(r2, 256)

    def body(co_ref, g_ref, o_ref, sum_ref, mine_ref):
        v = (g_ref[0, 0].astype(F32) + o_ref[0].astype(F32)).astype(BF16)
        sum_ref[0] = v

        @pl.when(pl.program_id(1) == co_ref[0])
        def _():
            mine_ref[0] = v

    piece = pl.BlockSpec((1, tr, cols), lambda i, s, co: (s, i, 0))
    return pl.pallas_call(
        body, name=name,
        grid_spec=pltpu.PrefetchScalarGridSpec(
            num_scalar_prefetch=1, grid=(r2 // tr, n_sh),
            in_specs=[pl.BlockSpec((1, 1, tr, cols), lambda i, s, co: (s, co[1], i, 0)), piece],
            out_specs=[piece, pl.BlockSpec((1, tr, cols), lambda i, s, co: (co[0], i, 0))]),
        out_shape=[jax.ShapeDtypeStruct((n_sh, r2, cols), BF16)] * 2,
        compiler_params=_params("arbitrary", "arbitrary"),
    )(coords, g, other)


def _shard_sum(pieces, coords, name):
    n_sh, r2, cols = pieces.shape
    tr = _row_tile(r2, 256)

    def body(co_ref, p_ref, o_ref):
        acc = p_ref[0].astype(F32)
        for k in range(1, n_sh):
            acc = acc + p_ref[k].astype(F32)
        o_ref[0] = acc

    return pl.pallas_call(
        body, name=name,
        grid_spec=pltpu.PrefetchScalarGridSpec(
            num_scalar_prefetch=1, grid=(r2 // tr,),
            in_specs=[pl.BlockSpec((n_sh, tr, cols), lambda i, co: (0, i, 0))],
            out_specs=pl.BlockSpec((1, tr, cols), lambda i, co: (co[1], i, 0))),
        out_shape=jax.ShapeDtypeStruct((2, r2, cols), F32),
        compiler_params=_params("arbitrary"),
    )(coords, pieces)


def _sum_small(a, name):
    n, r, cols = a.shape

    def body(a_ref, o_ref):
        acc = a_ref[0]
        for k in range(1, n):
            acc = acc + a_ref[k]
        o_ref[...] = acc

    return pl.pallas_call(
        body, name=name,
        in_specs=[pl.BlockSpec((n, r, cols), lambda: (0, 0, 0))],
        out_specs=pl.BlockSpec((r, cols), lambda: (0, 0)),
        out_shape=jax.ShapeDtypeStruct((r, cols), F32),
        compiler_params=_params(),
    )(a)


def _adamw(w, g, m, v, name):
    r, cols = w.shape
    tr = _row_tile(r, 256) if r % SUBLANES_BF16 == 0 else r

    def body(w_ref, g_ref, m_ref, v_ref, d_ref, nm_ref, nv_ref):
        g = g_ref[...]
        m = ADAM_B1 * m_ref[...] + (1.0 - ADAM_B1) * g
        v = ADAM_B2 * v_ref[...] + (1.0 - ADAM_B2) * (g * g)
        m_hat = m / (1.0 - ADAM_B1 ** ADAM_STEP)
        v_hat = v / (1.0 - ADAM_B2 ** ADAM_STEP)
        d_ref[...] = -ADAM_LR * (m_hat / (jnp.sqrt(v_hat) + ADAM_EPS) + ADAM_WD * w_ref[...])
        nm_ref[...] = m
        nv_ref[...] = v

    blk = pl.BlockSpec((tr, cols), lambda i: (i, 0))
    return pl.pallas_call(
        body, name=name, grid=(r // tr,),
        in_specs=[blk] * 4, out_specs=[blk] * 3,
        out_shape=[jax.ShapeDtypeStruct((r, cols), F32)] * 3,
        compiler_params=_params("arbitrary"),
    )(w, g, m, v)


ANY = pl.BlockSpec(memory_space=pl.ANY)


def _place():
    x, y, c = lax.axis_index("x"), lax.axis_index("y"), lax.axis_index("c")
    other_chips = [(1 - x, y), (x, 1 - y), (1 - x, 1 - y)]
    return x, y, c, other_chips


def _chip(xy):
    return 2 * xy[0] + xy[1]


def _remote(src, dst, send_sem, recv_sem, to):
    return pltpu.make_async_remote_copy(src_ref=src, dst_ref=dst, send_sem=send_sem, recv_sem=recv_sem,
                                        device_id=to, device_id_type=MESH)


def _in_place(n):
    return {i: i for i in range(n)}


def _all_gather_weights(bufs, small):
    n = len(bufs)

    def body(*refs):
        outs, small_out = refs[n + 1:2 * n + 1], refs[2 * n + 1]
        send_sems, recv_sems = refs[2 * n + 2:]
        x, y, c, chips = _place()
        me = _chip((x, y))
        sibling = (x, y, 1 - c)

        def ici(k, j, start):
            blk = outs[k].at[me if start else _chip(chips[j]), c]
            return _remote(blk, blk, send_sems.at[6 * k + j], recv_sems.at[6 * k + j], (*chips[j], c))

        def d2d(k, j, start):
            blk = outs[k].at[_chip(chips[j]), c if start else 1 - c]
            return _remote(blk, blk, send_sems.at[6 * k + 3 + j], recv_sems.at[6 * k + 3 + j], sibling)

        def small_copy(j, start):
            blk = small_out.at[me if start else _chip(chips[j])]
            return _remote(blk, blk, send_sems.at[6 * n + j], recv_sems.at[6 * n + j], (*chips[j], c))

        for k in range(n):
            for j in range(3):
                ici(k, j, True).start()
        for j in range(3):
            small_copy(j, True).start()
        for k in range(n):
            for j in range(3):
                ici(k, j, False).wait_recv()
                d2d(k, j, True).start()
        for j in range(3):
            small_copy(j, False).wait_recv()
            small_copy(j, True).wait_send()
        for k in range(n):
            for j in range(3):
                d2d(k, j, False).wait_recv()
                ici(k, j, True).wait_send()
                d2d(k, j, True).wait_send()

    return pl.pallas_call(
        body, name="all_gather_weights", in_specs=[ANY] * (n + 1), out_specs=[ANY] * (n + 1),
        out_shape=[jax.ShapeDtypeStruct(a.shape, a.dtype) for a in (*bufs, small)],
        input_output_aliases=_in_place(n + 1),
        scratch_shapes=[pltpu.SemaphoreType.DMA((6 * n + 3,))] * 2,
    )(*bufs, small)


def _exchange_core_halves(grads):
    n = len(grads)

    def body(*refs):
        ins, outs = refs[:n], refs[n:2 * n]
        send_sems, recv_sems = refs[2 * n:]
        x, y, c, _ = _place()
        copies = [_remote(ins[k].at[s, 1 - c], outs[k].at[s], send_sems.at[N_SHARDS * k + s],
                          recv_sems.at[N_SHARDS * k + s], (x, y, 1 - c))
                  for k in range(n) for s in range(N_SHARDS)]
        for cp in copies:
            cp.start()
        for cp in copies:
            cp.wait()

    return pl.pallas_call(
        body, name="exchange_core_halves", in_specs=[ANY] * n, out_specs=[ANY] * n,
        out_shape=[jax.ShapeDtypeStruct((N_SHARDS,) + a.shape[2:], a.dtype) for a in grads],
        scratch_shapes=[pltpu.SemaphoreType.DMA((N_SHARDS * n,))] * 2,
    )(*grads)


def _scatter_chip_sums(chip_sums, landing, small):
    n = len(chip_sums)

    def body(*refs):
        ins = refs[:n]
        outs, small_out = refs[2 * n + 1:3 * n + 1], refs[3 * n + 1]
        send_sems, recv_sems = refs[3 * n + 2:]
        x, y, c, chips = _place()
        me = _chip((x, y))
        dev = 2 * me + c

        def big(k, j, start):
            dst = outs[k].at[me if start else _chip(chips[j])]
            return _remote(ins[k].at[_chip(chips[j])], dst, send_sems.at[3 * k + j], recv_sems.at[3 * k + j],
                           (*chips[j], c))

        flips = [(fx, fy, fc) for fx in (0, 1) for fy in (0, 1) for fc in (0, 1)][1:]

        def small_copy(q, start):
            fx, fy, fc = flips[q]
            px, py, pc = x ^ fx, y ^ fy, c ^ fc
            blk = small_out.at[dev if start else 4 * px + 2 * py + pc]
            return _remote(blk, blk, send_sems.at[3 * n + q], recv_sems.at[3 * n + q], (px, py, pc))

        for q in range(len(flips)):
            small_copy(q, True).start()
        for k in range(n):
            for j in range(3):
                big(k, j, True).start()
        for q in range(len(flips)):
            small_copy(q, False).wait_recv()
            small_copy(q, True).wait_send()
        for k in range(n):
            for j in range(3):
                big(k, j, False).wait_recv()
                big(k, j, True).wait_send()

    return pl.pallas_call(
        body, name="scatter_chip_sums", in_specs=[ANY] * (2 * n + 1), out_specs=[ANY] * (n + 1),
        out_shape=[jax.ShapeDtypeStruct(a.shape, a.dtype) for a in (*landing, small)],
        input_output_aliases={n + i: i for i in range(n + 1)},
        scratch_shapes=[pltpu.SemaphoreType.DMA((3 * n + 7,))] * 2,
    )(*chip_sums, *landing, small)


def _join_core_halves(halves):
    n = len(halves)

    def body(*refs):
        outs = refs[n:2 * n]
        send_sems, recv_sems = refs[2 * n:]
        x, y, c, _ = _place()
        for k in range(n):
            _remote(outs[k].at[c], outs[k].at[c], send_sems.at[k], recv_sems.at[k], (x, y, 1 - c)).start()
        for k in range(n):
            _remote(outs[k].at[c], outs[k].at[1 - c], send_sems.at[k], recv_sems.at[k], (x, y, 1 - c)).wait()

    return pl.pallas_call(
        body, name="join_core_halves", in_specs=[ANY] * n, out_specs=[ANY] * n,
        out_shape=[jax.ShapeDtypeStruct(a.shape, a.dtype) for a in halves],
        input_output_aliases=_in_place(n),
        scratch_shapes=[pltpu.SemaphoreType.DMA((n,))] * 2,
    )(*halves)


def _flat_rows(parts):
    flat = jnp.concatenate([p.reshape(-1) for p in parts])
    assert flat.shape[0] % LANES == 0
    return flat.reshape(-1, LANES)


def _unflatten(flat, shapes):
    out, at = [], 0
    for sh in shapes:
        size = 1
        for dim in sh:
            size *= dim
        out.append(flat[at:at + size].reshape(sh))
        at += size
    assert at == flat.shape[0], (at, flat.shape)
    return out


def _col_shards_to_full(a, rows):
    q = a.shape[1] // rows
    return a.reshape(N_SHARDS, rows, q).transpose(1, 0, 2).reshape(rows, N_SHARDS * q)


def _my_col_shard(full, chip):
    rows, cols = full.shape
    q = cols // N_SHARDS
    return lax.dynamic_index_in_dim(full.reshape(rows, N_SHARDS, q), chip, axis=1, keepdims=False)


def kernel(x, e_norm_pre, e_norm_post, e_w_in, e_a_conv, e_b_conv, e_b_conv_bias, e_b_ln_g, e_b_ln_b, e_w_out, o_norm_pre, o_norm_post, o_w_in, o_c_w, o_c_b, o_c_scale, o_w_out, loss_target, m_e_norm_pre, m_e_norm_post, m_e_w_in, m_e_a_conv, m_e_b_conv, m_e_b_conv_bias, m_e_b_ln_g, m_e_b_ln_b, m_e_w_out, m_o_norm_pre, m_o_norm_post, m_o_w_in, m_o_c_w, m_o_c_b, m_o_c_scale, m_o_w_out, v_e_norm_pre, v_e_norm_post, v_e_w_in, v_e_a_conv, v_e_b_conv, v_e_b_conv_bias, v_e_b_ln_g, v_e_b_ln_b, v_e_w_out, v_o_norm_pre, v_o_norm_post, v_o_w_in, v_o_c_w, v_o_c_b, v_o_c_scale, v_o_w_out):
    _, s, d = x.shape
    w = d // 2
    c = d
    gc = c // N_GROUPS
    wq, cq, gq = w // N_SHARDS, c // N_SHARDS, gc // N_SHARDS
    chip = 2 * lax.axis_index("x") + lax.axis_index("y")
    core = lax.axis_index("c")
    x2 = x.reshape(s, d)
    target = loss_target.reshape(s, d)

    big_w = [e_w_in[0], e_w_out[0], o_w_in[0], o_c_w[0].reshape(N_GROUPS * gq, gc), o_w_out[0]]
    big_m = [m_e_w_in[0], m_e_w_out[0], m_o_w_in[0], m_o_c_w[0].reshape(N_GROUPS * gq, gc), m_o_w_out[0]]
    big_v = [v_e_w_in[0], v_e_w_out[0], v_o_w_in[0], v_o_c_w[0].reshape(N_GROUPS * gq, gc), v_o_w_out[0]]
    coords = jnp.stack([chip, core]).astype(jnp.int32)
    slots = [_cast_into_slot(a, coords, "cast_%d" % k) for k, a in enumerate(big_w)]
    sharded_small = _flat_rows([e_a_conv[0], e_b_conv[0], o_norm_pre, o_norm_post, o_c_scale, o_c_b[0]])
    small_slots = lax.dynamic_update_index_in_dim(jnp.zeros((N_SHARDS,) + sharded_small.shape, F32), sharded_small,
                                                  chip, 0)
    gathered = _all_gather_weights(slots, small_slots)
    full = [g.reshape(N_SHARDS, a.shape[0], a.shape[1]) for g, a in zip(gathered[:5], big_w)]
    e_w_in_sm, o_w_in_sm = full[0], full[2]
    e_w_out_f = full[1].reshape(w + w, d)
    o_w_out_f = full[4].reshape(c, d)
    cw_f = full[3].reshape(N_SHARDS, N_GROUPS, gq, gc).transpose(1, 0, 2, 3).reshape(N_GROUPS, gc, gc)
    sm = gathered[5].reshape(N_SHARDS, -1)
    at = [0]

    def take(rows, q):
        blk = sm[:, at[0]:at[0] + rows * q]
        at[0] += rows * q
        return _col_shards_to_full(blk, rows)

    a_conv_f = take(CONV_A, wq)
    b_conv_f = take(CONV_B, wq)
    o_pre_f = take(1, cq)
    o_post_f = take(1, cq)
    cs_f = take(1, cq)
    cb_f = take(N_GROUPS, gq).reshape(1, c)

    p0, h0 = _norm_matmul(x2, e_norm_pre, e_w_in_sm, "e_in_proj")
    u0 = _even_mixer_fwd(p0, a_conv_f, b_conv_f, e_b_conv_bias, e_b_ln_g, e_b_ln_b, "e_mixer_fwd")
    x1, y0 = _matmul_post(u0, e_w_out_f, x2, e_norm_post, "e_out_proj")
    p1, h1 = _norm_matmul(x1, o_pre_f, o_w_in_sm, "o_in_proj")
    u1 = _odd_mixer_fwd(p1, cw_f, cb_f, cs_f, "o_mixer_fwd")
    d_y1, d_x2, d_o_post, loss_part = _matmul_post_loss(u1, o_w_out_f, x1, o_post_f, target, "o_out_proj_loss")

    g_o_w_out = _matmul_tn(u1, d_y1, 1, "o_w_out_grad")
    d_u1 = _matmul_nt(d_y1, o_w_out_f[None], "o_out_proj_bwd")
    d_p1, d_cw, d_cb, d_cs = _odd_mixer_bwd(p1, d_u1, cw_f, cb_f, cs_f, "o_mixer_bwd")
    g_o_w_in = _matmul_tn(h1, d_p1, N_SHARDS, "o_w_in_grad")
    d_h1 = _matmul_nt(d_p1, o_w_in_sm, "o_in_proj_bwd")
    d_x1, d_o_pre, d_y0, d_e_post = _norm_bwd(d_h1, x1, o_pre_f, d_x2, "o_pre_norm_bwd", post=(y0, e_norm_post))

    g_e_w_out = _matmul_tn(u0, d_y0, 1, "e_w_out_grad")
    d_u0 = _matmul_nt(d_y0, e_w_out_f[None], "e_out_proj_bwd")
    d_p0, d_a_conv, d_b_conv, d_bias, d_ln_g, d_ln_b = _even_mixer_bwd(
        p0, d_u0, a_conv_f, b_conv_f, e_b_conv_bias, e_b_ln_g, e_b_ln_b, "e_mixer_bwd")
    g_e_w_in = _matmul_tn(h0, d_p0, N_SHARDS, "e_w_in_grad")
    d_h0 = _matmul_nt(d_p0, e_w_in_sm, "e_in_proj_bwd")
    grad_x, d_e_pre = _norm_bwd(d_h0, x2, e_norm_pre, d_x1, "e_pre_norm_bwd")

    g_cw = d_cw.reshape(N_GROUPS, N_SHARDS, gq, gc).transpose(1, 0, 2, 3).astype(BF16)
    partial = [g_e_w_in, g_e_w_out, g_o_w_in, g_cw, g_o_w_out]
    partial = [g.reshape(N_SHARDS, 2, a.shape[0] // 2, a.shape[1]) for g, a in zip(partial, big_w)]
    from_sibling = _exchange_core_halves(partial)
    summed = [_chip_sum(g, o, coords, "chip_sum_%d" % k) for k, (g, o) in enumerate(zip(partial, from_sibling))]
    small_parts = _flat_rows([loss_part[0], d_e_pre, d_e_post, d_bias, d_ln_g, d_ln_b, d_a_conv, d_b_conv,
                              d_o_pre, d_o_post, d_cs, d_cb])
    small_rows = lax.dynamic_update_index_in_dim(jnp.zeros((N_DEVICES,) + small_parts.shape, F32), small_parts,
                                                 2 * chip + core, 0)
    scattered = _scatter_chip_sums([sm_[0] for sm_ in summed], [sm_[1] for sm_ in summed], small_rows)
    reduced = [_shard_sum(sc, coords, "shard_sum_%d" % k) for k, sc in enumerate(scattered[:5])]
    joined = _join_core_halves(reduced)
    big_g = [j.reshape(a.shape) for j, a in zip(joined, big_w)]
    small_sum = _sum_small(scattered[5], "small_sum").reshape(-1)
    (loss_row, g_e_pre, g_e_post, g_bias, g_ln_g, g_ln_b, g_a_conv_f, g_b_conv_f, g_o_pre_f, g_o_post_f, g_cs_f,
     g_cb_f) = _unflatten(small_sum, [(LANES,), (1, d), (1, d), (1, w), (1, w), (1, w), (CONV_A, w), (CONV_B, w),
                                      (1, c), (1, c), (1, c), (1, c)])
    loss = loss_row[0]
    g_a_conv = _my_col_shard(g_a_conv_f, chip)
    g_b_conv = _my_col_shard(g_b_conv_f, chip)
    g_o_pre = _my_col_shard(g_o_pre_f, chip)
    g_o_post = _my_col_shard(g_o_post_f, chip)
    g_cs = _my_col_shard(g_cs_f, chip)
    g_cb = _my_col_shard(g_cb_f.reshape(N_GROUPS, gc), chip)

    big_upd = [_adamw(wt, g, m, v, "adamw_%d" % k) for k, (wt, g, m, v) in enumerate(zip(big_w, big_g, big_m, big_v))]
    small_w = [e_norm_pre, e_norm_post, e_b_conv_bias, e_b_ln_g, e_b_ln_b, e_a_conv[0], e_b_conv[0],
               o_norm_pre, o_norm_post, o_c_b[0], o_c_scale]
    small_m = [m_e_norm_pre, m_e_norm_post, m_e_b_conv_bias, m_e_b_ln_g, m_e_b_ln_b, m_e_a_conv[0], m_e_b_conv[0],
               m_o_norm_pre, m_o_norm_post, m_o_c_b[0], m_o_c_scale]
    small_v = [v_e_norm_pre, v_e_norm_post, v_e_b_conv_bias, v_e_b_ln_g, v_e_b_ln_b, v_e_a_conv[0], v_e_b_conv[0],
               v_o_norm_pre, v_o_norm_post, v_o_c_b[0], v_o_c_scale]
    small_g = [g_e_pre, g_e_post, g_bias, g_ln_g, g_ln_b, g_a_conv, g_b_conv, g_o_pre, g_o_post, g_cb, g_cs]
    small_shapes = [a.shape for a in small_w]
    small_upd = _adamw(_flat_rows(small_w), _flat_rows(small_g), _flat_rows(small_m), _flat_rows(small_v),
                       "adamw_small")
    small_delta, small_new_m, small_new_v = [_unflatten(u.reshape(-1), small_shapes) for u in small_upd]

    def ordered(small, big, lead):
        (n_pre, n_post, bias, ln_g, ln_b, a_conv, b_conv, o_pre, o_post, cb, cs) = small
        (w_in, w_out, ow_in, cw, ow_out) = big
        out = [n_pre, n_post, w_in[None], a_conv[None], b_conv[None], bias, ln_g, ln_b, w_out[None], o_pre, o_post,
               ow_in[None], cw.reshape(1, N_GROUPS, gq, gc), cb[None], cs, ow_out[None]]
        return out

    grads = ordered(small_g, big_g, None)
    deltas = ordered(small_delta, [u[0] for u in big_upd], None)
    new_m = ordered(small_new_m, [u[1] for u in big_upd], None)
    new_v = ordered(small_new_v, [u[2] for u in big_upd], None)
    return (loss, grad_x.reshape(1, s, d), *grads, *deltas, *new_m, *new_v)
```

```python
import functools

import jax
import jax.numpy as jnp
from jax import lax
from jax.experimental import pallas as pl
from jax.experimental.pallas import tpu as pltpu

F32 = jnp.float32
BF16 = jnp.bfloat16
MESH = pl.DeviceIdType.MESH

EPS = 1e-6
CONV_A = 3
CONV_B = 31
POOL_WINDOWS = (2, 4, 8, 16)
N_GROUPS = len(POOL_WINDOWS)
N_SHARDS = 4
N_DEVICES = 8
ADAM_LR = 0.001
ADAM_B1 = 0.9
ADAM_B2 = 0.999
ADAM_EPS = 1e-08
ADAM_WD = 0.01
ADAM_STEP = 10

LANES = 128
SUBLANES_BF16 = 16
HALO_A = 8
HALO_B = 32
HALO_P = 16
VMEM_LIMIT = 56 * 1024 * 1024


def _row_tile(n, pref):
    t = min(n, pref)
    while t > SUBLANES_BF16 and (n % t or t % SUBLANES_BF16):
        t -= SUBLANES_BF16
    assert n % t == 0, (n, pref)
    return t


def _col_chunk(n, pref):
    t = (min(n, pref) // LANES) * LANES
    while t > LANES and n % t:
        t -= LANES
    assert t >= LANES and n % t == 0, (n, pref)
    return t


def _params(*sem):
    return pltpu.CompilerParams(dimension_semantics=tuple(sem) if sem else None, vmem_limit_bytes=VMEM_LIMIT)


ANY = pl.BlockSpec(memory_space=pl.ANY)


class _Comm:
    def __init__(self, srcs, bufs, n_sems, phases, finish):
        self.srcs, self.bufs, self.n_sems, self.phases, self.finish = list(srcs), list(bufs), n_sems, phases, finish


def _call(body, *, name, grid, in_specs, out_specs, out_shape, args, scratch_shapes=(), comm=None):
    params = _params(*(("arbitrary",) * len(grid)))
    if comm is None:
        out = pl.pallas_call(body, name=name, grid=grid, in_specs=in_specs, out_specs=out_specs, out_shape=out_shape,
                             scratch_shapes=scratch_shapes, compiler_params=params)(*args)
        return list(out), []
    n_in, n_out, n_scr = len(in_specs), len(out_specs), len(scratch_shapes)
    ns, nb = len(comm.srcs), len(comm.bufs)
    total = 1
    for size in grid:
        total *= size

    def fused(*refs):
        ins, srcs = refs[:n_in], refs[n_in:n_in + ns]
        at = n_in + ns + nb
        outs, bufs = refs[at:at + n_out], refs[at + n_out:at + n_out + nb]
        scratch = refs[at + n_out + nb:at + n_out + nb + n_scr]
        send_sems, recv_sems = refs[-2:]
        step = 0
        for axis, size in enumerate(grid):
            step = step * size + pl.program_id(axis)
        for when, fn in comm.phases:
            pl.when(step == when)(functools.partial(fn, srcs, bufs, send_sems, recv_sems))
        body(*ins, *outs, *scratch)
        pl.when(step == total - 1)(functools.partial(comm.finish, srcs, bufs, send_sems, recv_sems))

    out = pl.pallas_call(
        fused, name=name, grid=grid,
        in_specs=list(in_specs) + [ANY] * (ns + nb), out_specs=list(out_specs) + [ANY] * nb,
        out_shape=list(out_shape) + [jax.ShapeDtypeStruct(b.shape, b.dtype) for b in comm.bufs],
        input_output_aliases={n_in + ns + i: n_out + i for i in range(nb)},
        scratch_shapes=list(scratch_shapes) + [pltpu.SemaphoreType.DMA((comm.n_sems,))] * 2,
        compiler_params=params,
    )(*args, *comm.srcs, *comm.bufs)
    return list(out[:n_out]), list(out[n_out:])


def _comm_only(comm, name):
    ns, nb = len(comm.srcs), len(comm.bufs)

    def body(*refs):
        srcs, bufs = refs[:ns], refs[ns + nb:ns + 2 * nb]
        send_sems, recv_sems = refs[-2:]
        for _, fn in comm.phases:
            fn(srcs, bufs, send_sems, recv_sems)
        comm.finish(srcs, bufs, send_sems, recv_sems)

    return pl.pallas_call(
        body, name=name, in_specs=[ANY] * (ns + nb), out_specs=[ANY] * nb,
        out_shape=[jax.ShapeDtypeStruct(b.shape, b.dtype) for b in comm.bufs],
        input_output_aliases={ns + i: i for i in range(nb)},
        scratch_shapes=[pltpu.SemaphoreType.DMA((comm.n_sems,))] * 2,
    )(*comm.srcs, *comm.bufs)


def _sigmoid(v):
    return jax.nn.sigmoid(v)


def _dsilu(v, s):
    return s * (1.0 + v * (1.0 - s))


def _mean_last(v):
    return jnp.mean(v, axis=-1, keepdims=True)


def _sum_rows(v):
    return jnp.sum(v, axis=0, keepdims=True)


def _norm_matmul(x, gain, w_sm, name, comm=None):
    s, d = x.shape
    n_sh, _, ns = w_sm.shape
    tm = _row_tile(s, 512)

    def body(x_ref, g_ref, w_ref, p_ref, h_ref):
        @pl.when(pl.program_id(1) == 0)
        def _():
            xv = x_ref[...]
            r = lax.rsqrt(_mean_last(xv * xv) + EPS)
            h_ref[...] = (xv * r * g_ref[...]).astype(BF16)

        p_ref[...] = jnp.dot(h_ref[...], w_ref[0], preferred_element_type=F32)

    return _call(
        body, name=name, grid=(s // tm, n_sh),
        in_specs=[pl.BlockSpec((tm, d), lambda i, j: (i, 0)),
                  pl.BlockSpec((1, d), lambda i, j: (0, 0)),
                  pl.BlockSpec((1, d, ns), lambda i, j: (j, 0, 0))],
        out_specs=[pl.BlockSpec((tm, ns), lambda i, j: (i, j)),
                   pl.BlockSpec((tm, d), lambda i, j: (i, 0))],
        out_shape=[jax.ShapeDtypeStruct((s, n_sh * ns), F32), jax.ShapeDtypeStruct((s, d), BF16)],
        args=(x, gain, w_sm), comm=comm)


def _matmul_post(u, w, x_res, gain, name):
    s, k = u.shape
    d = w.shape[1]
    tm = _row_tile(s, 256)

    def body(u_ref, w_ref, x_ref, g_ref, xn_ref, y_ref):
        y = jnp.dot(u_ref[...], w_ref[...], preferred_element_type=F32)
        r = lax.rsqrt(_mean_last(y * y) + EPS)
        y_ref[...] = y
        xn_ref[...] = x_ref[...] + (y * r) * g_ref[...]

    return pl.pallas_call(
        body, name=name, grid=(s // tm,),
        in_specs=[pl.BlockSpec((tm, k), lambda i: (i, 0)),
                  pl.BlockSpec((k, d), lambda i: (0, 0)),
                  pl.BlockSpec((tm, d), lambda i: (i, 0)),
                  pl.BlockSpec((1, d), lambda i: (0, 0))],
        out_specs=[pl.BlockSpec((tm, d), lambda i: (i, 0)), pl.BlockSpec((tm, d), lambda i: (i, 0))],
        out_shape=[jax.ShapeDtypeStruct((s, d), F32), jax.ShapeDtypeStruct((s, d), F32)],
        compiler_params=_params("arbitrary"),
    )(u, w, x_res, gain)


def _matmul_post_loss(u, w, x_res, gain, target, name):
    s, k = u.shape
    d = w.shape[1]
    tm = _row_tile(s, 256)

    def body(u_ref, w_ref, x_ref, g_ref, t_ref, dy_ref, dout_ref, dg_ref, loss_ref):
        @pl.when(pl.program_id(0) == 0)
        def _():
            dg_ref[...] = jnp.zeros_like(dg_ref)
            loss_ref[...] = jnp.zeros_like(loss_ref)

        y = jnp.dot(u_ref[...], w_ref[...], preferred_element_type=F32)
        r = lax.rsqrt(_mean_last(y * y) + EPS)
        n = y * r
        g = g_ref[...]
        err = x_ref[...] + n * g - t_ref[...]
        loss_ref[...] += 0.5 * jnp.sum(_mean_last(err * err))
        dout = err * (1.0 / d)
        dout_ref[...] = dout
        dg_ref[...] += _sum_rows(dout * n)
        dn = dout * g
        dy_ref[...] = (r * (dn - n * _mean_last(dn * n))).astype(BF16)

    return pl.pallas_call(
        body, name=name, grid=(s // tm,),
        in_specs=[pl.BlockSpec((tm, k), lambda i: (i, 0)),
                  pl.BlockSpec((k, d), lambda i: (0, 0)),
                  pl.BlockSpec((tm, d), lambda i: (i, 0)),
                  pl.BlockSpec((1, d), lambda i: (0, 0)),
                  pl.BlockSpec((tm, d), lambda i: (i, 0))],
        out_specs=[pl.BlockSpec((tm, d), lambda i: (i, 0)),
                   pl.BlockSpec((tm, d), lambda i: (i, 0)),
                   pl.BlockSpec((1, d), lambda i: (0, 0)),
                   pl.BlockSpec((8, LANES), lambda i: (0, 0))],
        out_shape=[jax.ShapeDtypeStruct((s, d), BF16), jax.ShapeDtypeStruct((s, d), F32),
                   jax.ShapeDtypeStruct((1, d), F32), jax.ShapeDtypeStruct((8, LANES), F32)],
        compiler_params=_params("arbitrary"),
    )(u, w, x_res, gain, target)


def _matmul_nt(a, w_sm, name, comm=None):
    s, ncols = a.shape
    n_sh, r, ns = w_sm.shape
    assert ncols == n_sh * ns
    tm = _row_tile(s, 512)
    nc = _col_chunk(ns, 1024)
    per = ns // nc
    steps = n_sh * per

    def body(a_ref, w_ref, o_ref):
        part = lax.dot_general(a_ref[...], w_ref[0], (((1,), (1,)), ((), ())), preferred_element_type=F32)

        @pl.when(pl.program_id(1) == 0)
        def _():
            o_ref[...] = part

        @pl.when(pl.program_id(1) > 0)
        def _():
            o_ref[...] += part

    out, bufs = _call(
        body, name=name, grid=(s // tm, steps),
        in_specs=[pl.BlockSpec((tm, nc), lambda i, j: (i, j)),
                  pl.BlockSpec((1, r, nc), lambda i, j: (j // per, 0, j % per))],
        out_specs=[pl.BlockSpec((tm, r), lambda i, j: (i, 0))],
        out_shape=[jax.ShapeDtypeStruct((s, r), F32)],
        args=(a, w_sm), comm=comm)
    return out[0], bufs


def _matmul_tn(a, b, n_sh, name):
    s, k = a.shape
    n = b.shape[1]
    ns = n // n_sh
    tk = _row_tile(k, 1024)
    ts = _row_tile(s, 512)
    n_s = s // ts

    def body(a_ref, b_ref, o_ref, acc_ref):
        part = lax.dot_general(a_ref[...], b_ref[...], (((0,), (0,)), ((), ())), preferred_element_type=F32)

        @pl.when(pl.program_id(2) == 0)
        def _():
            acc_ref[...] = part

        @pl.when(pl.program_id(2) > 0)
        def _():
            acc_ref[...] += part

        @pl.when(pl.program_id(2) == n_s - 1)
        def _():
            o_ref[0] = acc_ref[...].astype(BF16)

    return pl.pallas_call(
        body, name=name, grid=(n_sh, k // tk, n_s),
        in_specs=[pl.BlockSpec((ts, tk), lambda j, i, t: (t, i)),
                  pl.BlockSpec((ts, ns), lambda j, i, t: (t, j))],
        out_specs=pl.BlockSpec((1, tk, ns), lambda j, i, t: (j, i, 0)),
        out_shape=jax.ShapeDtypeStruct((n_sh, k, ns), BF16),
        scratch_shapes=[pltpu.VMEM((tk, ns), F32)],
        compiler_params=_params("arbitrary", "arbitrary", "arbitrary"),
    )(a, b)


def _norm_bwd(dh, x, gain, dres, name, post=None):
    s, d = x.shape
    tm = _row_tile(s, 256)
    with_post = post is not None

    def rms_bwd(dout, v, g):
        r = lax.rsqrt(_mean_last(v * v) + EPS)
        n = v * r
        dn = dout * g
        return r * (dn - n * _mean_last(dn * n)), _sum_rows(dout * n)

    def body(*refs):
        if with_post:
            dh_ref, x_ref, g_ref, dres_ref, y_ref, gp_ref, dx_ref, dg_ref, dy_ref, dgp_ref = refs
        else:
            dh_ref, x_ref, g_ref, dres_ref, dx_ref, dg_ref = refs

        @pl.when(pl.program_id(0) == 0)
        def _():
            dg_ref[...] = jnp.zeros_like(dg_ref)
            if with_post:
                dgp_ref[...] = jnp.zeros_like(dgp_ref)

        dv, dg = rms_bwd(dh_ref[...], x_ref[...], g_ref[...])
        dx = dres_ref[...] + dv
        dx_ref[...] = dx
        dg_ref[...] += dg
        if with_post:
            dy, dgp = rms_bwd(dx, y_ref[...], gp_ref[...])
            dy_ref[...] = dy.astype(BF16)
            dgp_ref[...] += dgp

    row = pl.BlockSpec((tm, d), lambda i: (i, 0))
    vec = pl.BlockSpec((1, d), lambda i: (0, 0))
    in_specs = [row, row, vec, row]
    out_specs = [row, vec]
    out_shape = [jax.ShapeDtypeStruct((s, d), F32), jax.ShapeDtypeStruct((1, d), F32)]
    args = [dh, x, gain, dres]
    if with_post:
        in_specs += [row, vec]
        out_specs += [row, vec]
        out_shape += [jax.ShapeDtypeStruct((s, d), BF16), jax.ShapeDtypeStruct((1, d), F32)]
        args += list(post)
    return pl.pallas_call(
        body, name=name, grid=(s // tm,), in_specs=in_specs, out_specs=out_specs, out_shape=out_shape,
        compiler_params=_params("arbitrary"),
    )(*args)


def _even_forward_tile(p_ref, halo_ref, first, a_conv_ref, b_conv_ref, bias_ref, lng_ref, lnb_ref, qbuf, ybuf, w, ts):
    def col(ref, k, rows=slice(None)):
        return ref[rows, k * w:(k + 1) * w]

    a_x, a_b, a_c, a_z = col(p_ref, 0), col(p_ref, 1), col(p_ref, 2), col(p_ref, 3)
    b_val, b_gate, b_z = col(p_ref, 4), col(p_ref, 5), col(p_ref, 6)
    keep = jnp.where(first, 0.0, 1.0)

    rows_a = slice(HALO_B - HALO_A, HALO_B)
    qbuf[0:HALO_A, :] = col(halo_ref, 2, rows_a) * col(halo_ref, 0, rows_a) * keep
    qbuf[HALO_A:HALO_A + ts, :] = a_c * a_x
    cq = jnp.zeros((ts, w), F32)
    for j in range(CONV_A):
        cq = cq + a_conv_ref[CONV_A - 1 - j:CONV_A - j, :] * qbuf[HALO_A - j:HALO_A - j + ts, :]
    ya = a_b * cq

    ybuf[0:HALO_B, :] = col(halo_ref, 4) * _sigmoid(col(halo_ref, 5)) * keep
    ybuf[HALO_B:HALO_B + ts, :] = b_val * _sigmoid(b_gate)
    yb1 = jnp.zeros((ts, w), F32) + bias_ref[...]
    for j in range(CONV_B):
        yb1 = yb1 + b_conv_ref[CONV_B - 1 - j:CONV_B - j, :] * ybuf[HALO_B - j:HALO_B - j + ts, :]
    xc = yb1 - _mean_last(yb1)
    rstd = lax.rsqrt(_mean_last(xc * xc) + EPS)
    xhat = xc * rstd
    yb2 = xhat * lng_ref[...] + lnb_ref[...]
    return dict(a_x=a_x, a_b=a_b, a_c=a_c, a_z=a_z, b_val=b_val, b_gate=b_gate, b_z=b_z,
                cq=cq, ya=ya, rstd=rstd, xhat=xhat, yb2=yb2)


def _even_specs(s, w, ts):
    tile = pl.BlockSpec((ts, 7 * w), lambda i: (i, 0))
    halo = pl.BlockSpec((HALO_B, 7 * w), lambda i: (jnp.maximum(i * (ts // HALO_B) - 1, 0), 0))
    return tile, halo


def _small_specs(shapes, index=lambda i: (0, 0)):
    return [pl.BlockSpec(sh, index) for sh in shapes]


def _even_mixer_fwd(p, a_conv, b_conv, bias, ln_g, ln_b, name):
    s = p.shape[0]
    w = p.shape[1] // 7
    ts = _row_tile(s, 256)
    assert ts % HALO_B == 0

    def body(p_ref, halo_ref, ac_ref, bc_ref, bias_ref, lng_ref, lnb_ref, u_ref, qbuf, ybuf):
        first = pl.program_id(0) == 0
        f = _even_forward_tile(p_ref, halo_ref, first, ac_ref, bc_ref, bias_ref, lng_ref, lnb_ref, qbuf, ybuf, w, ts)
        yb3 = f["yb2"] * _sigmoid(f["yb2"])
        u_ref[:, 0:w] = (f["ya"] * (f["a_z"] * _sigmoid(f["a_z"]))).astype(BF16)
        u_ref[:, w:2 * w] = (yb3 * (f["b_z"] * _sigmoid(f["b_z"]))).astype(BF16)

    tile, halo = _even_specs(s, w, ts)
    return pl.pallas_call(
        body, name=name, grid=(s // ts,),
        in_specs=[tile, halo] + _small_specs([(CONV_A, w), (CONV_B, w), (1, w), (1, w), (1, w)]),
        out_specs=pl.BlockSpec((ts, 2 * w), lambda i: (i, 0)),
        out_shape=jax.ShapeDtypeStruct((s, 2 * w), BF16),
        scratch_shapes=[pltpu.VMEM((HALO_A + ts, w), F32), pltpu.VMEM((HALO_B + ts, w), F32)],
        compiler_params=_params("arbitrary"),
    )(p, p, a_conv, b_conv, bias, ln_g, ln_b)


def _even_mixer_bwd(p, du, a_conv, b_conv, bias, ln_g, ln_b, name, comm=None):
    s = p.shape[0]
    w = p.shape[1] // 7
    ts = _row_tile(s, 256)
    nt = s // ts
    assert ts % HALO_B == 0

    def body(p_ref, halo_ref, du_ref, ac_ref, bc_ref, bias_ref, lng_ref, lnb_ref,
             dp_ref, dac_ref, dbc_ref, dbias_ref, dlng_ref, dlnb_ref, qbuf, ybuf, dqbuf, dybuf, carry_dq, carry_dy):
        step = pl.program_id(0)
        first = step == nt - 1

        @pl.when(step == 0)
        def _():
            for ref in (dac_ref, dbc_ref, dbias_ref, dlng_ref, dlnb_ref, carry_dq, carry_dy):
                ref[...] = jnp.zeros_like(ref)

        f = _even_forward_tile(p_ref, halo_ref, first, ac_ref, bc_ref, bias_ref, lng_ref, lnb_ref, qbuf, ybuf, w, ts)
        a_z, b_z, yb2 = f["a_z"], f["b_z"], f["yb2"]
        s_az, s_bz, s_y2 = _sigmoid(a_z), _sigmoid(b_z), _sigmoid(yb2)
        du_a = du_ref[:, 0:w]
        du_b = du_ref[:, w:2 * w]

        d_ya = du_a * (a_z * s_az)
        dp_ref[:, 3 * w:4 * w] = (du_a * f["ya"] * _dsilu(a_z, s_az)).astype(BF16)
        dp_ref[:, 1 * w:2 * w] = (d_ya * f["cq"]).astype(BF16)
        d_cq = d_ya * f["a_b"]
        dqbuf[0:ts, :] = d_cq
        dqbuf[ts:ts + HALO_A, :] = carry_dq[...]
        d_q = jnp.zeros((ts, w), F32)
        for o in range(CONV_A):
            d_q = d_q + ac_ref[CONV_A - 1 - o:CONV_A - o, :] * dqbuf[o:o + ts, :]
        for j in range(CONV_A):
            k = CONV_A - 1 - j
            dac_ref[k:k + 1, :] += _sum_rows(d_cq * qbuf[HALO_A - j:HALO_A - j + ts, :])
        carry_dq[...] = d_cq[0:HALO_A, :]
        dp_ref[:, 2 * w:3 * w] = (d_q * f["a_x"]).astype(BF16)
        dp_ref[:, 0 * w:1 * w] = (d_q * f["a_c"]).astype(BF16)

        d_yb3 = du_b * (b_z * s_bz)
        dp_ref[:, 6 * w:7 * w] = (du_b * (yb2 * s_y2) * _dsilu(b_z, s_bz)).astype(BF16)
        d_yb2 = d_yb3 * _dsilu(yb2, s_y2)
        xhat = f["xhat"]
        dlng_ref[...] += _sum_rows(d_yb2 * xhat)
        dlnb_ref[...] += _sum_rows(d_yb2)
        d_xh = d_yb2 * lng_ref[...]
        d_yb1 = f["rstd"] * (d_xh - _mean_last(d_xh) - xhat * _mean_last(d_xh * xhat))
        dbias_ref[...] += _sum_rows(d_yb1)
        dybuf[0:ts, :] = d_yb1
        dybuf[ts:ts + HALO_B, :] = carry_dy[...]
        d_yb0 = jnp.zeros((ts, w), F32)
        for o in range(CONV_B):
            d_yb0 = d_yb0 + bc_ref[CONV_B - 1 - o:CONV_B - o, :] * dybuf[o:o + ts, :]
        for j in range(CONV_B):
            k = CONV_B - 1 - j
            dbc_ref[k:k + 1, :] += _sum_rows(d_yb1 * ybuf[HALO_B - j:HALO_B - j + ts, :])
        carry_dy[...] = d_yb1[0:HALO_B, :]
        s_g = _sigmoid(f["b_gate"])
        dp_ref[:, 4 * w:5 * w] = (d_yb0 * s_g).astype(BF16)
        dp_ref[:, 5 * w:6 * w] = (d_yb0 * f["b_val"] * s_g * (1.0 - s_g)).astype(BF16)

    rev = lambda i: (nt - 1 - i, 0)
    tile = pl.BlockSpec((ts, 7 * w), rev)
    halo = pl.BlockSpec((HALO_B, 7 * w), lambda i: (jnp.maximum((nt - 1 - i) * (ts // HALO_B) - 1, 0), 0))
    small = [(CONV_A, w), (CONV_B, w), (1, w), (1, w), (1, w)]
    return _call(
        body, name=name, grid=(nt,),
        in_specs=[tile, halo, pl.BlockSpec((ts, 2 * w), rev)] + _small_specs(small),
        out_specs=[pl.BlockSpec((ts, 7 * w), rev)] + _small_specs(small),
        out_shape=[jax.ShapeDtypeStruct((s, 7 * w), BF16)] + [jax.ShapeDtypeStruct(sh, F32) for sh in small],
        scratch_shapes=[pltpu.VMEM((HALO_A + ts, w), F32), pltpu.VMEM((HALO_B + ts, w), F32),
                        pltpu.VMEM((ts + HALO_A, w), F32), pltpu.VMEM((ts + HALO_B, w), F32),
                        pltpu.VMEM((HALO_A, w), F32), pltpu.VMEM((HALO_B, w), F32)],
        args=(p, p, du, a_conv, b_conv, bias, ln_g, ln_b), comm=comm)


def _pool_forward_tile(p_ref, halo_ref, first, tile_index, cw_ref, cb_ref, cs_ref, vbuf, c, gc, ts):
    vbuf[0:HALO_P, :] = halo_ref[...] * jnp.where(first, 0.0, 1.0)
    vbuf[HALO_P:HALO_P + ts, :] = p_ref[:, 0:c]
    pos = tile_index * ts + lax.broadcasted_iota(jnp.int32, (ts, 1), 0) + 1
    pooled, inv, gout = [], [], []
    for g, win in enumerate(POOL_WINDOWS):
        cols = slice(g * gc, (g + 1) * gc)
        acc = jnp.zeros((ts, gc), F32)
        for j in range(win):
            acc = acc + vbuf[HALO_P - j:HALO_P - j + ts, cols]
        inv_g = 1.0 / jnp.minimum(pos, win).astype(F32)
        pooled_g = (acc * inv_g - p_ref[:, cols]).astype(BF16)
        pooled.append(pooled_g)
        inv.append(inv_g)
        gout.append(jnp.dot(pooled_g, cw_ref[g], preferred_element_type=F32) + cb_ref[:, cols])
    return pooled, inv, gout


def _odd_mixer_fwd(p, cw, cb, cs, name):
    s = p.shape[0]
    c = p.shape[1] // 2
    gc = c // N_GROUPS
    ts = _row_tile(s, 256)

    def body(p_ref, halo_ref, cw_ref, cb_ref, cs_ref, u_ref, vbuf):
        i = pl.program_id(0)
        _, _, gout = _pool_forward_tile(p_ref, halo_ref, i == 0, i, cw_ref, cb_ref, cs_ref, vbuf, c, gc, ts)
        for g in range(N_GROUPS):
            cols = slice(g * gc, (g + 1) * gc)
            z = p_ref[:, c + g * gc:c + (g + 1) * gc]
            u_ref[:, cols] = (gout[g] * cs_ref[:, cols] * (z * _sigmoid(z))).astype(BF16)

    return pl.pallas_call(
        body, name=name, grid=(s // ts,),
        in_specs=[pl.BlockSpec((ts, 2 * c), lambda i: (i, 0)),
                  pl.BlockSpec((HALO_P, c), lambda i: (jnp.maximum(i * (ts // HALO_P) - 1, 0), 0)),
                  pl.BlockSpec((N_GROUPS, gc, gc), lambda i: (0, 0, 0)),
                  pl.BlockSpec((1, c), lambda i: (0, 0)), pl.BlockSpec((1, c), lambda i: (0, 0))],
        out_specs=pl.BlockSpec((ts, c), lambda i: (i, 0)),
        out_shape=jax.ShapeDtypeStruct((s, c), BF16),
        scratch_shapes=[pltpu.VMEM((HALO_P + ts, c), F32)],
        compiler_params=_params("arbitrary"),
    )(p, p, cw, cb, cs)


def _odd_mixer_bwd(p, du, cw, cb, cs, name):
    s = p.shape[0]
    c = p.shape[1] // 2
    gc = c // N_GROUPS
    ts = _row_tile(s, 256)
    nt = s // ts

    def body(p_ref, halo_ref, du_ref, cw_ref, cb_ref, cs_ref, dp_ref, dcw_ref, dcb_ref, dcs_ref, vbuf, ebuf, carry_e):
        step = pl.program_id(0)
        tile_index = nt - 1 - step

        @pl.when(step == 0)
        def _():
            for ref in (dcw_ref, dcb_ref, dcs_ref, carry_e):
                ref[...] = jnp.zeros_like(ref)

        pooled, inv, gout = _pool_forward_tile(p_ref, halo_ref, tile_index == 0, tile_index, cw_ref, cb_ref, cs_ref,
                                               vbuf, c, gc, ts)
        ebuf[ts:ts + HALO_P, :] = carry_e[...]
        for g, win in enumerate(POOL_WINDOWS):
            cols = slice(g * gc, (g + 1) * gc)
            z = p_ref[:, c + g * gc:c + (g + 1) * gc]
            sz = _sigmoid(z)
            du_g = du_ref[:, cols]
            scale = cs_ref[:, cols]
            d_y = du_g * (z * sz)
            dp_ref[:, c + g * gc:c + (g + 1) * gc] = (du_g * (gout[g] * scale) * _dsilu(z, sz)).astype(BF16)
            dcs_ref[:, cols] += _sum_rows(d_y * gout[g])
            d_gout = d_y * scale
            dcb_ref[:, cols] += _sum_rows(d_gout)
            d_gout_b = d_gout.astype(BF16)
            dcw_ref[g] += lax.dot_general(pooled[g], d_gout_b, (((0,), (0,)), ((), ())), preferred_element_type=F32)
            d_pool = lax.dot_general(d_gout_b, cw_ref[g], (((1,), (1,)), ((), ())), preferred_element_type=F32)
            e = d_pool * inv[g]
            ebuf[0:ts, cols] = e
            d_v = -d_pool
            for o in range(win):
                d_v = d_v + ebuf[o:o + ts, cols]
            dp_ref[:, cols] = d_v.astype(BF16)
            carry_e[:, cols] = e[0:HALO_P, :]

    rev = lambda i: (nt - 1 - i, 0)
    small = [(N_GROUPS, gc, gc), (1, c), (1, c)]
    return pl.pallas_call(
        body, name=name, grid=(nt,),
        in_specs=[pl.BlockSpec((ts, 2 * c), rev),
                  pl.BlockSpec((HALO_P, c), lambda i: (jnp.maximum((nt - 1 - i) * (ts // HALO_P) - 1, 0), 0)),
                  pl.BlockSpec((ts, c), rev),
                  pl.BlockSpec((N_GROUPS, gc, gc), lambda i: (0, 0, 0)),
                  pl.BlockSpec((1, c), lambda i: (0, 0)), pl.BlockSpec((1, c), lambda i: (0, 0))],
        out_specs=[pl.BlockSpec((ts, 2 * c), rev),
                   pl.BlockSpec((N_GROUPS, gc, gc), lambda i: (0, 0, 0)),
                   pl.BlockSpec((1, c), lambda i: (0, 0)), pl.BlockSpec((1, c), lambda i: (0, 0))],
        out_shape=[jax.ShapeDtypeStruct((s, 2 * c), BF16)] + [jax.ShapeDtypeStruct(sh, F32) for sh in small],
        scratch_shapes=[pltpu.VMEM((HALO_P + ts, c), F32), pltpu.VMEM((ts + HALO_P, c), F32),
                        pltpu.VMEM((HALO_P, c), F32)],
        compiler_params=_params("arbitrary"),
    )(p, p, du, cw, cb, cs)


def _cast_into_slot(a, coords, name):
    r, cols = a.shape
    tr = _row_tile(r // 2, 256)
    per = r // 2 // tr

    def body(co_ref, a_ref, o_ref):
        o_ref[0, 0] = a_ref[...].astype(BF16)

    return pl.pallas_call(
        body, name=name,
        grid_spec=pltpu.PrefetchScalarGridSpec(
            num_scalar_prefetch=1, grid=(2, per),
            in_specs=[pl.BlockSpec((tr, cols), lambda h, i, co: (h * per + i, 0))],
            out_specs=pl.BlockSpec((1, 1, tr, cols), lambda h, i, co: (co[0], h, i, 0))),
        out_shape=jax.ShapeDtypeStruct((N_SHARDS, 2, r // 2, cols), BF16),
        compiler_params=_params("arbitrary", "arbitrary"),
    )(coords, a)


def _chip_sum(g, other, coords, name):
    n_sh, _, r2, cols = g.shape
    tr = _row_tile(r2, 256)

    def body(co_ref, g_ref, o_ref, sum_ref, mine_ref):
        v = (g_ref[0, 0].astype(F32) + o_ref[0].astype(F32)).astype(BF16)
        sum_ref[0] = v

        @pl.when(pl.program_id(1) == co_ref[0])
        def _():
            mine_ref[0] = v

    piece = pl.BlockSpec((1, tr, cols), lambda i, s, co: (s, i, 0))
    return pl.pallas_call(
        body, name=name,
        grid_spec=pltpu.PrefetchScalarGridSpec(
            num_scalar_prefetch=1, grid=(r2 // tr, n_sh),
            in_specs=[pl.BlockSpec((1, 1, tr, cols), lambda i, s, co: (s, co[1], i, 0)), piece],
            out_specs=[piece, pl.BlockSpec((1, tr, cols), lambda i, s, co: (co[0], i, 0))]),
        out_shape=[jax.ShapeDtypeStruct((n_sh, r2, cols), BF16)] * 2,
        compiler_params=_params("arbitrary", "arbitrary"),
    )(coords, g, other)


def _shard_sum(pieces, coords, name):
    n_sh, r2, cols = pieces.shape
    tr = _row_tile(r2, 256)

    def body(co_ref, p_ref, o_ref):
        acc = p_ref[0].astype(F32)
        for k in range(1, n_sh):
            acc = acc + p_ref[k].astype(F32)
        o_ref[0] = acc

    return pl.pallas_call(
        body, name=name,
        grid_spec=pltpu.PrefetchScalarGridSpec(
            num_scalar_prefetch=1, grid=(r2 // tr,),
            in_specs=[pl.BlockSpec((n_sh, tr, cols), lambda i, co: (0, i, 0))],
            out_specs=pl.BlockSpec((1, tr, cols), lambda i, co: (co[1], i, 0))),
        out_shape=jax.ShapeDtypeStruct((2, r2, cols), F32),
        compiler_params=_params("arbitrary"),
    )(coords, pieces)


def _sum_small(a, name):
    n, r, cols = a.shape

    def body(a_ref, o_ref):
        acc = a_ref[0]
        for k in range(1, n):
            acc = acc + a_ref[k]
        o_ref[...] = acc

    return pl.pallas_call(
        body, name=name,
        in_specs=[pl.BlockSpec((n, r, cols), lambda: (0, 0, 0))],
        out_specs=pl.BlockSpec((r, cols), lambda: (0, 0)),
        out_shape=jax.ShapeDtypeStruct((r, cols), F32),
        compiler_params=_params(),
    )(a)


def _adamw(w, g, m, v, name):
    r, cols = w.shape
    tr = _row_tile(r, 256) if r % SUBLANES_BF16 == 0 else r

    def body(w_ref, g_ref, m_ref, v_ref, d_ref, nm_ref, nv_ref):
        g = g_ref[...]
        m = ADAM_B1 * m_ref[...] + (1.0 - ADAM_B1) * g
        v = ADAM_B2 * v_ref[...] + (1.0 - ADAM_B2) * (g * g)
        m_hat = m / (1.0 - ADAM_B1 ** ADAM_STEP)
        v_hat = v / (1.0 - ADAM_B2 ** ADAM_STEP)
        d_ref[...] = -ADAM_LR * (m_hat / (jnp.sqrt(v_hat) + ADAM_EPS) + ADAM_WD * w_ref[...])
        nm_ref[...] = m
        nv_ref[...] = v

    blk = pl.BlockSpec((tr, cols), lambda i: (i, 0))
    return pl.pallas_call(
        body, name=name, grid=(r // tr,),
        in_specs=[blk] * 4, out_specs=[blk] * 3,
        out_shape=[jax.ShapeDtypeStruct((r, cols), F32)] * 3,
        compiler_params=_params("arbitrary"),
    )(w, g, m, v)


def _place():
    x, y, c = lax.axis_index("x"), lax.axis_index("y"), lax.axis_index("c")
    other_chips = [(1 - x, y), (x, 1 - y), (1 - x, 1 - y)]
    return x, y, c, other_chips


def _chip(xy):
    return 2 * xy[0] + xy[1]


def _remote(src, dst, send_sem, recv_sem, to):
    return pltpu.make_async_remote_copy(src_ref=src, dst_ref=dst, send_sem=send_sem, recv_sem=recv_sem,
                                        device_id=to, device_id_type=MESH)


def _gather_comm(bufs, small, forward_at):
    n = len(bufs)

    def ici(ctx, k, j, start):
        (x, y, c, chips), b, send, recv = ctx
        blk = b[k].at[_chip((x, y)) if start else _chip(chips[j]), c]
        return _remote(blk, blk, send.at[6 * k + j], recv.at[6 * k + j], (*chips[j], c))

    def d2d(ctx, k, j, start):
        (x, y, c, chips), b, send, recv = ctx
        blk = b[k].at[_chip(chips[j]), c if start else 1 - c]
        return _remote(blk, blk, send.at[6 * k + 3 + j], recv.at[6 * k + 3 + j], (x, y, 1 - c))

    def small_copy(ctx, j, start):
        (x, y, c, chips), b, send, recv = ctx
        blk = b[n].at[_chip((x, y)) if start else _chip(chips[j])]
        return _remote(blk, blk, send.at[6 * n + j], recv.at[6 * n + j], (*chips[j], c))

    def start(srcs, b, send, recv):
        ctx = (_place(), b, send, recv)
        for k in range(n):
            for j in range(3):
                ici(ctx, k, j, True).start()
        if small is not None:
            for j in range(3):
                small_copy(ctx, j, True).start()

    def forward(srcs, b, send, recv):
        ctx = (_place(), b, send, recv)
        for k in range(n):
            for j in range(3):
                ici(ctx, k, j, False).wait_recv()
                d2d(ctx, k, j, True).start()

    def finish(srcs, b, send, recv):
        ctx = (_place(), b, send, recv)
        if small is not None:
            for j in range(3):
                small_copy(ctx, j, False).wait_recv()
                small_copy(ctx, j, True).wait_send()
        for k in range(n):
            for j in range(3):
                d2d(ctx, k, j, False).wait_recv()
                ici(ctx, k, j, True).wait_send()
                d2d(ctx, k, j, True).wait_send()

    all_bufs = list(bufs) + ([small] if small is not None else [])
    return _Comm([], all_bufs, 6 * n + 3, [(0, start), (forward_at, forward)], finish)


def _exchange_core_halves(grads, name):
    n = len(grads)

    def body(*refs):
        ins, outs = refs[:n], refs[n:2 * n]
        send_sems, recv_sems = refs[2 * n:]
        x, y, c, _ = _place()
        copies = [_remote(ins[k].at[s, 1 - c], outs[k].at[s], send_sems.at[N_SHARDS * k + s],
                          recv_sems.at[N_SHARDS * k + s], (x, y, 1 - c))
                  for k in range(n) for s in range(N_SHARDS)]
        for cp in copies:
            cp.start()
        for cp in copies:
            cp.wait()

    return pl.pallas_call(
        body, name=name, in_specs=[ANY] * n, out_specs=[ANY] * n,
        out_shape=[jax.ShapeDtypeStruct((N_SHARDS,) + a.shape[2:], a.dtype) for a in grads],
        scratch_shapes=[pltpu.SemaphoreType.DMA((N_SHARDS * n,))] * 2,
    )(*grads)


def _scatter_comm(chip_sums, landing):
    n = len(chip_sums)

    def big(srcs, b, send, recv, k, j, start):
        x, y, c, chips = _place()
        dst = b[k].at[_chip((x, y)) if start else _chip(chips[j])]
        return _remote(srcs[k].at[_chip(chips[j])], dst, send.at[3 * k + j], recv.at[3 * k + j], (*chips[j], c))

    def start(srcs, b, send, recv):
        for k in range(n):
            for j in range(3):
                big(srcs, b, send, recv, k, j, True).start()

    def finish(srcs, b, send, recv):
        for k in range(n):
            for j in range(3):
                big(srcs, b, send, recv, k, j, False).wait_recv()
                big(srcs, b, send, recv, k, j, True).wait_send()

    return _Comm(chip_sums, landing, 3 * n, [(0, start)], finish)


def _join_comm(halves, small):
    n = len(halves)
    flips = [(fx, fy, fc) for fx in (0, 1) for fy in (0, 1) for fc in (0, 1)][1:]

    def half(b, send, recv, k, start):
        x, y, c, _ = _place()
        return _remote(b[k].at[c], b[k].at[c if start else 1 - c], send.at[k], recv.at[k], (x, y, 1 - c))

    def small_copy(b, send, recv, q, start):
        x, y, c, _ = _place()
        px, py, pc = x ^ flips[q][0], y ^ flips[q][1], c ^ flips[q][2]
        blk = b[n].at[4 * x + 2 * y + c if start else 4 * px + 2 * py + pc]
        return _remote(blk, blk, send.at[n + q], recv.at[n + q], (px, py, pc))

    def start(srcs, b, send, recv):
        for q in range(len(flips)):
            small_copy(b, send, recv, q, True).start()
        for k in range(n):
            half(b, send, recv, k, True).start()

    def finish(srcs, b, send, recv):
        for q in range(len(flips)):
            small_copy(b, send, recv, q, False).wait()
        for k in range(n):
            half(b, send, recv, k, False).wait()

    return _Comm([], list(halves) + [small], n + len(flips), [(0, start)], finish)


def _flat_rows(parts):
    flat = jnp.concatenate([p.reshape(-1) for p in parts])
    assert flat.shape[0] % LANES == 0
    return flat.reshape(-1, LANES)


def _unflatten(flat, shapes):
    out, at = [], 0
    for sh in shapes:
        size = 1
        for dim in sh:
            size *= dim
        out.append(flat[at:at + size].reshape(sh))
        at += size
    assert at == flat.shape[0], (at, flat.shape)
    return out


def _col_shards_to_full(a, rows):
    q = a.shape[1] // rows
    return a.reshape(N_SHARDS, rows, q).transpose(1, 0, 2).reshape(rows, N_SHARDS * q)


def _my_col_shard(full, chip):
    rows, cols = full.shape
    q = cols // N_SHARDS
    return lax.dynamic_index_in_dim(full.reshape(rows, N_SHARDS, q), chip, axis=1, keepdims=False)


def kernel(x, e_norm_pre, e_norm_post, e_w_in, e_a_conv, e_b_conv, e_b_conv_bias, e_b_ln_g, e_b_ln_b, e_w_out, o_norm_pre, o_norm_post, o_w_in, o_c_w, o_c_b, o_c_scale, o_w_out, loss_target, m_e_norm_pre, m_e_norm_post, m_e_w_in, m_e_a_conv, m_e_b_conv, m_e_b_conv_bias, m_e_b_ln_g, m_e_b_ln_b, m_e_w_out, m_o_norm_pre, m_o_norm_post, m_o_w_in, m_o_c_w, m_o_c_b, m_o_c_scale, m_o_w_out, v_e_norm_pre, v_e_norm_post, v_e_w_in, v_e_a_conv, v_e_b_conv, v_e_b_conv_bias, v_e_b_ln_g, v_e_b_ln_b, v_e_w_out, v_o_norm_pre, v_o_norm_post, v_o_w_in, v_o_c_w, v_o_c_b, v_o_c_scale, v_o_w_out):
    _, s, d = x.shape
    w = d // 2
    c = d
    gc = c // N_GROUPS
    wq, cq, gq = w // N_SHARDS, c // N_SHARDS, gc // N_SHARDS
    chip = 2 * lax.axis_index("x") + lax.axis_index("y")
    core = lax.axis_index("c")
    x2 = x.reshape(s, d)
    target = loss_target.reshape(s, d)

    big_w = [e_w_in[0], e_w_out[0], o_w_in[0], o_c_w[0].reshape(N_GROUPS * gq, gc), o_w_out[0]]
    big_m = [m_e_w_in[0], m_e_w_out[0], m_o_w_in[0], m_o_c_w[0].reshape(N_GROUPS * gq, gc), m_o_w_out[0]]
    big_v = [v_e_w_in[0], v_e_w_out[0], v_o_w_in[0], v_o_c_w[0].reshape(N_GROUPS * gq, gc), v_o_w_out[0]]
    coords = jnp.stack([chip, core]).astype(jnp.int32)
    slots = [_cast_into_slot(a, coords, "cast_%d" % k) for k, a in enumerate(big_w)]
    sharded_small = _flat_rows([e_a_conv[0], e_b_conv[0], o_norm_pre, o_norm_post, o_c_scale, o_c_b[0]])
    small_slots = lax.dynamic_update_index_in_dim(jnp.zeros((N_SHARDS,) + sharded_small.shape, F32), sharded_small,
                                                  chip, 0)
    e_w_in_g, e_w_out_g, small_g4 = _comm_only(_gather_comm(slots[:2], small_slots, 0), "gather_even_weights")
    e_w_in_sm = e_w_in_g.reshape((N_SHARDS,) + big_w[0].shape)
    e_w_out_f = e_w_out_g.reshape(w + w, d)
    sm = small_g4.reshape(N_SHARDS, -1)
    at = [0]

    def take(rows, q):
        blk = sm[:, at[0]:at[0] + rows * q]
        at[0] += rows * q
        return _col_shards_to_full(blk, rows)

    a_conv_f = take(CONV_A, wq)
    b_conv_f = take(CONV_B, wq)
    o_pre_f = take(1, cq)
    o_post_f = take(1, cq)
    cs_f = take(1, cq)
    cb_f = take(N_GROUPS, gq).reshape(1, c)

    in_proj_steps = (s // _row_tile(s, 512)) * N_SHARDS
    (p0, h0), odd_g = _norm_matmul(x2, e_norm_pre, e_w_in_sm, "e_in_proj",
                                   comm=_gather_comm(slots[2:], None, (2 * in_proj_steps) // 3))
    o_w_in_sm = odd_g[0].reshape((N_SHARDS,) + big_w[2].shape)
    cw_f = odd_g[1].reshape(N_SHARDS, N_GROUPS, gq, gc).transpose(1, 0, 2, 3).reshape(N_GROUPS, gc, gc)
    o_w_out_f = odd_g[2].reshape(c, d)
    u0 = _even_mixer_fwd(p0, a_conv_f, b_conv_f, e_b_conv_bias, e_b_ln_g, e_b_ln_b, "e_mixer_fwd")
    x1, y0 = _matmul_post(u0, e_w_out_f, x2, e_norm_post, "e_out_proj")
    (p1, h1), _ = _norm_matmul(x1, o_pre_f, o_w_in_sm, "o_in_proj")
    u1 = _odd_mixer_fwd(p1, cw_f, cb_f, cs_f, "o_mixer_fwd")
    d_y1, d_x2, d_o_post, loss_part = _matmul_post_loss(u1, o_w_out_f, x1, o_post_f, target, "o_out_proj_loss")

    def as_pieces(g, k):
        return g.reshape(N_SHARDS, 2, big_w[k].shape[0] // 2, big_w[k].shape[1])

    def chip_sums(ks, grads, name):
        pieces = [as_pieces(g, k) for k, g in zip(ks, grads)]
        from_sibling = _exchange_core_halves(pieces, name)
        both = [_chip_sum(g, o, coords, "chip_sum_%d" % k) for k, g, o in zip(ks, pieces, from_sibling)]
        return [b[0] for b in both], [b[1] for b in both]

    g_o_w_out = _matmul_tn(u1, d_y1, 1, "o_w_out_grad")
    d_u1, _ = _matmul_nt(d_y1, o_w_out_f[None], "o_out_proj_bwd")
    d_p1, d_cw, d_cb, d_cs = _odd_mixer_bwd(p1, d_u1, cw_f, cb_f, cs_f, "o_mixer_bwd")
    g_o_w_in = _matmul_tn(h1, d_p1, N_SHARDS, "o_w_in_grad")
    d_h1, _ = _matmul_nt(d_p1, o_w_in_sm, "o_in_proj_bwd")
    d_x1, d_o_pre, d_y0, d_e_post = _norm_bwd(d_h1, x1, o_pre_f, d_x2, "o_pre_norm_bwd", post=(y0, e_norm_post))

    g_e_w_out = _matmul_tn(u0, d_y0, 1, "e_w_out_grad")
    g_cw = d_cw.reshape(N_GROUPS, N_SHARDS, gq, gc).transpose(1, 0, 2, 3).astype(BF16)
    sums_a, landing_a = chip_sums([1, 2, 3, 4], [g_e_w_out, g_o_w_in, g_cw, g_o_w_out], "exchange_core_halves_a")
    d_u0, _ = _matmul_nt(d_y0, e_w_out_f[None], "e_out_proj_bwd")
    (d_p0, d_a_conv, d_b_conv, d_bias, d_ln_g, d_ln_b), landed_a = _even_mixer_bwd(
        p0, d_u0, a_conv_f, b_conv_f, e_b_conv_bias, e_b_ln_g, e_b_ln_b, "e_mixer_bwd",
        comm=_scatter_comm(sums_a, landing_a))
    g_e_w_in = _matmul_tn(h0, d_p0, N_SHARDS, "e_w_in_grad")
    sums_b, landing_b = chip_sums([0], [g_e_w_in], "exchange_core_halves_b")
    d_h0, landed_b = _matmul_nt(d_p0, e_w_in_sm, "e_in_proj_bwd", comm=_scatter_comm(sums_b, landing_b))
    grad_x, d_e_pre = _norm_bwd(d_h0, x2, e_norm_pre, d_x1, "e_pre_norm_bwd")

    landed = landed_b + landed_a
    reduced = [_shard_sum(sc, coords, "shard_sum_%d" % k) for k, sc in enumerate(landed)]
    small_parts = _flat_rows([loss_part[0], d_e_pre, d_e_post, d_bias, d_ln_g, d_ln_b, d_a_conv, d_b_conv,
                              d_o_pre, d_o_post, d_cs, d_cb])
    small_rows = lax.dynamic_update_index_in_dim(jnp.zeros((N_DEVICES,) + small_parts.shape, F32), small_parts,
                                                 2 * chip + core, 0)
    joined = _comm_only(_join_comm(reduced, small_rows), "join_core_halves")
    big_g = [j.reshape(a.shape) for j, a in zip(joined[:5], big_w)]
    small_sum = _sum_small(joined[5], "small_sum").reshape(-1)
    (loss_row, g_e_pre, g_e_post, g_bias, g_ln_g, g_ln_b, g_a_conv_f, g_b_conv_f, g_o_pre_f, g_o_post_f, g_cs_f,
     g_cb_f) = _unflatten(small_sum, [(LANES,), (1, d), (1, d), (1, w), (1, w), (1, w), (CONV_A, w), (CONV_B, w),
                                      (1, c), (1, c), (1, c), (1, c)])
    loss = loss_row[0]
    g_a_conv = _my_col_shard(g_a_conv_f, chip)
    g_b_conv = _my_col_shard(g_b_conv_f, chip)
    g_o_pre = _my_col_shard(g_o_pre_f, chip)
    g_o_post = _my_col_shard(g_o_post_f, chip)
    g_cs = _my_col_shard(g_cs_f, chip)
    g_cb = _my_col_shard(g_cb_f.reshape(N_GROUPS, gc), chip)

    big_upd = [_adamw(wt, g, m, v, "adamw_%d" % k) for k, (wt, g, m, v) in enumerate(zip(big_w, big_g, big_m, big_v))]
    small_w = [e_norm_pre, e_norm_post, e_b_conv_bias, e_b_ln_g, e_b_ln_b, e_a_conv[0], e_b_conv[0],
               o_norm_pre, o_norm_post, o_c_b[0], o_c_scale]
    small_m = [m_e_norm_pre, m_e_norm_post, m_e_b_conv_bias, m_e_b_ln_g, m_e_b_ln_b, m_e_a_conv[0], m_e_b_conv[0],
               m_o_norm_pre, m_o_norm_post, m_o_c_b[0], m_o_c_scale]
    small_v = [v_e_norm_pre, v_e_norm_post, v_e_b_conv_bias, v_e_b_ln_g, v_e_b_ln_b, v_e_a_conv[0], v_e_b_conv[0],
               v_o_norm_pre, v_o_norm_post, v_o_c_b[0], v_o_c_scale]
    small_g = [g_e_pre, g_e_post, g_bias, g_ln_g, g_ln_b, g_a_conv, g_b_conv, g_o_pre, g_o_post, g_cb, g_cs]
    small_shapes = [a.shape for a in small_w]
    small_upd = _adamw(_flat_rows(small_w), _flat_rows(small_g), _flat_rows(small_m), _flat_rows(small_v),
                       "adamw_small")
    small_delta, small_new_m, small_new_v = [_unflatten(u.reshape(-1), small_shapes) for u in small_upd]

    def ordered(small, big, lead):
        (n_pre, n_post, bias, ln_g, ln_b, a_conv, b_conv, o_pre, o_post, cb, cs) = small
        (w_in, w_out, ow_in, cw, ow_out) = big
        out = [n_pre, n_post, w_in[None], a_conv[None], b_conv[None], bias, ln_g, ln_b, w_out[None], o_pre, o_post,
               ow_in[None], cw.reshape(1, N_GROUPS, gq, gc), cb[None], cs, ow_out[None]]
        return out

    grads = ordered(small_g, big_g, None)
    deltas = ordered(small_delta, [u[0] for u in big_upd], None)
    new_m = ordered(small_new_m, [u[1] for u in big_upd], None)
    new_v = ordered(small_new_v, [u[2] for u in big_upd], None)
    return (loss, grad_x.reshape(1, s, d), *grads, *deltas, *new_m, *new_v)
```

```python
import functools

import jax
import jax.numpy as jnp
from jax import lax
from jax.experimental import pallas as pl
from jax.experimental.pallas import tpu as pltpu

F32 = jnp.float32
BF16 = jnp.bfloat16
MESH = pl.DeviceIdType.MESH

EPS = 1e-6
CONV_A = 3
CONV_B = 31
POOL_WINDOWS = (2, 4, 8, 16)
N_GROUPS = len(POOL_WINDOWS)
N_SHARDS = 4
N_DEVICES = 8
ADAM_LR = 0.001
ADAM_B1 = 0.9
ADAM_B2 = 0.999
ADAM_EPS = 1e-08
ADAM_WD = 0.01
ADAM_STEP = 10

LANES = 128
SUBLANES_BF16 = 16
HALO_A = 8
HALO_B = 32
HALO_P = 16
SHIFTS = 8
ROW_BLOCK = 32
LANE_BLOCK = 256
VMEM_LIMIT = 56 * 1024 * 1024


def _row_tile(n, pref):
    t = min(n, pref)
    while t > SUBLANES_BF16 and (n % t or t % SUBLANES_BF16):
        t -= SUBLANES_BF16
    assert n % t == 0, (n, pref)
    return t


def _col_chunk(n, pref):
    t = (min(n, pref) // LANES) * LANES
    while t > LANES and n % t:
        t -= LANES
    assert t >= LANES and n % t == 0, (n, pref)
    return t


def _params(*sem):
    return pltpu.CompilerParams(dimension_semantics=tuple(sem) if sem else None, vmem_limit_bytes=VMEM_LIMIT)


ANY = pl.BlockSpec(memory_space=pl.ANY)


class _Comm:
    def __init__(self, srcs, bufs, n_sems, phases, finish):
        self.srcs, self.bufs, self.n_sems, self.phases, self.finish = list(srcs), list(bufs), n_sems, phases, finish


def _call(body, *, name, grid, in_specs, out_specs, out_shape, args, scratch_shapes=(), comm=None):
    params = _params(*(("arbitrary",) * len(grid)))
    if comm is None:
        out = pl.pallas_call(body, name=name, grid=grid, in_specs=in_specs, out_specs=out_specs, out_shape=out_shape,
                             scratch_shapes=scratch_shapes, compiler_params=params)(*args)
        return list(out), []
    n_in, n_out, n_scr = len(in_specs), len(out_specs), len(scratch_shapes)
    ns, nb = len(comm.srcs), len(comm.bufs)
    total = 1
    for size in grid:
        total *= size

    def fused(*refs):
        ins, srcs = refs[:n_in], refs[n_in:n_in + ns]
        at = n_in + ns + nb
        outs, bufs = refs[at:at + n_out], refs[at + n_out:at + n_out + nb]
        scratch = refs[at + n_out + nb:at + n_out + nb + n_scr]
        send_sems, recv_sems = refs[-2:]
        step = 0
        for axis, size in enumerate(grid):
            step = step * size + pl.program_id(axis)
        for when, fn in comm.phases:
            pl.when(step == when)(functools.partial(fn, srcs, bufs, send_sems, recv_sems))
        body(*ins, *outs, *scratch)
        pl.when(step == total - 1)(functools.partial(comm.finish, srcs, bufs, send_sems, recv_sems))

    out = pl.pallas_call(
        fused, name=name, grid=grid,
        in_specs=list(in_specs) + [ANY] * (ns + nb), out_specs=list(out_specs) + [ANY] * nb,
        out_shape=list(out_shape) + [jax.ShapeDtypeStruct(b.shape, b.dtype) for b in comm.bufs],
        input_output_aliases={n_in + ns + i: n_out + i for i in range(nb)},
        scratch_shapes=list(scratch_shapes) + [pltpu.SemaphoreType.DMA((comm.n_sems,))] * 2,
        compiler_params=params,
    )(*args, *comm.srcs, *comm.bufs)
    return list(out[:n_out]), list(out[n_out:])


def _comm_only(comm, name):
    ns, nb = len(comm.srcs), len(comm.bufs)

    def body(*refs):
        srcs, bufs = refs[:ns], refs[ns + nb:ns + 2 * nb]
        send_sems, recv_sems = refs[-2:]
        for _, fn in comm.phases:
            fn(srcs, bufs, send_sems, recv_sems)
        comm.finish(srcs, bufs, send_sems, recv_sems)

    return pl.pallas_call(
        body, name=name, in_specs=[ANY] * (ns + nb), out_specs=[ANY] * nb,
        out_shape=[jax.ShapeDtypeStruct(b.shape, b.dtype) for b in comm.bufs],
        input_output_aliases={ns + i: i for i in range(nb)},
        scratch_shapes=[pltpu.SemaphoreType.DMA((comm.n_sems,))] * 2,
    )(*comm.srcs, *comm.bufs)


def _sigmoid(v):
    return jax.nn.sigmoid(v)


def _dsilu(v, s):
    return s * (1.0 + v * (1.0 - s))


def _mean_last(v):
    return jnp.mean(v, axis=-1, keepdims=True)


def _sum_rows(v):
    return jnp.sum(v, axis=0, keepdims=True)


def _norm_matmul(x, gain, w_sm, name, comm=None):
    s, d = x.shape
    n_sh, _, ns = w_sm.shape
    tm = _row_tile(s, 512)

    def body(x_ref, g_ref, w_ref, p_ref, h_ref):
        @pl.when(pl.program_id(1) == 0)
        def _():
            xv = x_ref[...]
            r = lax.rsqrt(_mean_last(xv * xv) + EPS)
            h_ref[...] = (xv * r * g_ref[...]).astype(BF16)

        p_ref[...] = jnp.dot(h_ref[...], w_ref[0], preferred_element_type=F32)

    return _call(
        body, name=name, grid=(s // tm, n_sh),
        in_specs=[pl.BlockSpec((tm, d), lambda i, j: (i, 0)),
                  pl.BlockSpec((1, d), lambda i, j: (0, 0)),
                  pl.BlockSpec((1, d, ns), lambda i, j: (j, 0, 0))],
        out_specs=[pl.BlockSpec((tm, ns), lambda i, j: (i, j)),
                   pl.BlockSpec((tm, d), lambda i, j: (i, 0))],
        out_shape=[jax.ShapeDtypeStruct((s, n_sh * ns), F32), jax.ShapeDtypeStruct((s, d), BF16)],
        args=(x, gain, w_sm), comm=comm)


def _matmul_post(u, w, x_res, gain, name):
    s, k = u.shape
    d = w.shape[1]
    tm = _row_tile(s, 256)

    def body(u_ref, w_ref, x_ref, g_ref, xn_ref, y_ref):
        y = jnp.dot(u_ref[...], w_ref[...], preferred_element_type=F32)
        r = lax.rsqrt(_mean_last(y * y) + EPS)
        y_ref[...] = y
        xn_ref[...] = x_ref[...] + (y * r) * g_ref[...]

    return pl.pallas_call(
        body, name=name, grid=(s // tm,),
        in_specs=[pl.BlockSpec((tm, k), lambda i: (i, 0)),
                  pl.BlockSpec((k, d), lambda i: (0, 0)),
                  pl.BlockSpec((tm, d), lambda i: (i, 0)),
                  pl.BlockSpec((1, d), lambda i: (0, 0))],
        out_specs=[pl.BlockSpec((tm, d), lambda i: (i, 0)), pl.BlockSpec((tm, d), lambda i: (i, 0))],
        out_shape=[jax.ShapeDtypeStruct((s, d), F32), jax.ShapeDtypeStruct((s, d), F32)],
        compiler_params=_params("arbitrary"),
    )(u, w, x_res, gain)


def _matmul_post_loss(u, w, x_res, gain, target, name):
    s, k = u.shape
    d = w.shape[1]
    tm = _row_tile(s, 256)

    def body(u_ref, w_ref, x_ref, g_ref, t_ref, dy_ref, dout_ref, dg_ref, loss_ref):
        @pl.when(pl.program_id(0) == 0)
        def _():
            dg_ref[...] = jnp.zeros_like(dg_ref)
            loss_ref[...] = jnp.zeros_like(loss_ref)

        y = jnp.dot(u_ref[...], w_ref[...], preferred_element_type=F32)
        r = lax.rsqrt(_mean_last(y * y) + EPS)
        n = y * r
        g = g_ref[...]
        err = x_ref[...] + n * g - t_ref[...]
        loss_ref[...] += 0.5 * jnp.sum(_mean_last(err * err))
        dout = err * (1.0 / d)
        dout_ref[...] = dout
        dg_ref[...] += _sum_rows(dout * n)
        dn = dout * g
        dy_ref[...] = (r * (dn - n * _mean_last(dn * n))).astype(BF16)

    return pl.pallas_call(
        body, name=name, grid=(s // tm,),
        in_specs=[pl.BlockSpec((tm, k), lambda i: (i, 0)),
                  pl.BlockSpec((k, d), lambda i: (0, 0)),
                  pl.BlockSpec((tm, d), lambda i: (i, 0)),
                  pl.BlockSpec((1, d), lambda i: (0, 0)),
                  pl.BlockSpec((tm, d), lambda i: (i, 0))],
        out_specs=[pl.BlockSpec((tm, d), lambda i: (i, 0)),
                   pl.BlockSpec((tm, d), lambda i: (i, 0)),
                   pl.BlockSpec((1, d), lambda i: (0, 0)),
                   pl.BlockSpec((8, LANES), lambda i: (0, 0))],
        out_shape=[jax.ShapeDtypeStruct((s, d), BF16), jax.ShapeDtypeStruct((s, d), F32),
                   jax.ShapeDtypeStruct((1, d), F32), jax.ShapeDtypeStruct((8, LANES), F32)],
        compiler_params=_params("arbitrary"),
    )(u, w, x_res, gain, target)


def _matmul_nt(a, w_sm, name, comm=None):
    s, ncols = a.shape
    n_sh, r, ns = w_sm.shape
    assert ncols == n_sh * ns
    tm = _row_tile(s, 512)
    nc = _col_chunk(ns, 1792)
    per = ns // nc
    steps = n_sh * per

    def body(a_ref, w_ref, o_ref):
        part = lax.dot_general(a_ref[...], w_ref[0], (((1,), (1,)), ((), ())), preferred_element_type=F32)

        @pl.when(pl.program_id(1) == 0)
        def _():
            o_ref[...] = part

        @pl.when(pl.program_id(1) > 0)
        def _():
            o_ref[...] += part

    out, bufs = _call(
        body, name=name, grid=(s // tm, steps),
        in_specs=[pl.BlockSpec((tm, nc), lambda i, j: (i, j)),
                  pl.BlockSpec((1, r, nc), lambda i, j: (j // per, 0, j % per))],
        out_specs=[pl.BlockSpec((tm, r), lambda i, j: (i, 0))],
        out_shape=[jax.ShapeDtypeStruct((s, r), F32)],
        args=(a, w_sm), comm=comm)
    return out[0], bufs


def _matmul_tn(a, b, n_sh, name):
    s, k = a.shape
    n = b.shape[1]
    ns = n // n_sh
    tk = _row_tile(k, 1024)
    ts = _row_tile(s, 512)
    n_s = s // ts

    def body(a_ref, b_ref, o_ref, acc_ref):
        part = lax.dot_general(a_ref[...], b_ref[...], (((0,), (0,)), ((), ())), preferred_element_type=F32)

        @pl.when(pl.program_id(2) == 0)
        def _():
            acc_ref[...] = part

        @pl.when(pl.program_id(2) > 0)
        def _():
            acc_ref[...] += part

        @pl.when(pl.program_id(2) == n_s - 1)
        def _():
            o_ref[0] = acc_ref[...].astype(BF16)

    return pl.pallas_call(
        body, name=name, grid=(n_sh, k // tk, n_s),
        in_specs=[pl.BlockSpec((ts, tk), lambda j, i, t: (t, i)),
                  pl.BlockSpec((ts, ns), lambda j, i, t: (t, j))],
        out_specs=pl.BlockSpec((1, tk, ns), lambda j, i, t: (j, i, 0)),
        out_shape=jax.ShapeDtypeStruct((n_sh, k, ns), BF16),
        scratch_shapes=[pltpu.VMEM((tk, ns), F32)],
        compiler_params=_params("arbitrary", "arbitrary", "arbitrary"),
    )(a, b)


def _norm_bwd(dh, x, gain, dres, name, post=None):
    s, d = x.shape
    tm = _row_tile(s, 256)
    with_post = post is not None

    def rms_bwd(dout, v, g):
        r = lax.rsqrt(_mean_last(v * v) + EPS)
        n = v * r
        dn = dout * g
        return r * (dn - n * _mean_last(dn * n)), _sum_rows(dout * n)

    def body(*refs):
        if with_post:
            dh_ref, x_ref, g_ref, dres_ref, y_ref, gp_ref, dx_ref, dg_ref, dy_ref, dgp_ref = refs
        else:
            dh_ref, x_ref, g_ref, dres_ref, dx_ref, dg_ref = refs

        @pl.when(pl.program_id(0) == 0)
        def _():
            dg_ref[...] = jnp.zeros_like(dg_ref)
            if with_post:
                dgp_ref[...] = jnp.zeros_like(dgp_ref)

        dv, dg = rms_bwd(dh_ref[...], x_ref[...], g_ref[...])
        dx = dres_ref[...] + dv
        dx_ref[...] = dx
        dg_ref[...] += dg
        if with_post:
            dy, dgp = rms_bwd(dx, y_ref[...], gp_ref[...])
            dy_ref[...] = dy.astype(BF16)
            dgp_ref[...] += dgp

    row = pl.BlockSpec((tm, d), lambda i: (i, 0))
    vec = pl.BlockSpec((1, d), lambda i: (0, 0))
    in_specs = [row, row, vec, row]
    out_specs = [row, vec]
    out_shape = [jax.ShapeDtypeStruct((s, d), F32), jax.ShapeDtypeStruct((1, d), F32)]
    args = [dh, x, gain, dres]
    if with_post:
        in_specs += [row, vec]
        out_specs += [row, vec]
        out_shape += [jax.ShapeDtypeStruct((s, d), BF16), jax.ShapeDtypeStruct((1, d), F32)]
        args += list(post)
    return pl.pallas_call(
        body, name=name, grid=(s // tm,), in_specs=in_specs, out_specs=out_specs, out_shape=out_shape,
        compiler_params=_params("arbitrary"),
    )(*args)


def _fill_shifted_down(sh, rows):
    for b in range(1, SHIFTS):
        sh[b, SHIFTS:rows, :] = sh[0, SHIFTS - b:rows - b, :]


def _fill_shifted_up(sh, rows):
    for b in range(1, SHIFTS):
        sh[b, 0:rows - SHIFTS, :] = sh[0, b:rows - SHIFTS + b, :]


def _for_blocks(ts, w, fn):
    lb = min(LANE_BLOCK, w)
    for l0 in range(0, w, lb):
        def rows(rb, carry, l0=l0):
            fn(pl.multiple_of(rb * ROW_BLOCK, ROW_BLOCK), slice(l0, l0 + lb))
            return carry

        lax.fori_loop(0, ts // ROW_BLOCK, rows, 0)


def _conv31(sh, base, step, wt_ref, bias_ref, out_ref, ts, w):
    def block(r0, lanes):
        acc = jnp.zeros((ROW_BLOCK, lanes.stop - lanes.start), F32)
        if bias_ref is not None:
            acc = acc + bias_ref[:, lanes]
        for j in range(CONV_B):
            a, b = divmod(j, SHIFTS)
            start = pl.multiple_of(r0 + (base + step * a), SHIFTS)
            acc = acc + wt_ref[CONV_B - 1 - j:CONV_B - j, lanes] * sh[b, pl.ds(start, ROW_BLOCK), lanes]
        out_ref[pl.ds(r0, ROW_BLOCK), lanes] = acc

    _for_blocks(ts, w, block)


def _conv31_weight_grad(d_sh, x_sh, wacc, ts, w):
    def block(r0, lanes):
        d = d_sh[0, pl.ds(r0, ROW_BLOCK), lanes]
        for j in range(CONV_B):
            a, b = divmod(j, SHIFTS)
            start = pl.multiple_of(r0 + (HALO_B - SHIFTS * a), SHIFTS)
            prod = d * x_sh[b, pl.ds(start, ROW_BLOCK), lanes]
            part = prod[0:SHIFTS, :]
            for q in range(1, ROW_BLOCK // SHIFTS):
                part = part + prod[q * SHIFTS:(q + 1) * SHIFTS, :]
            wacc[CONV_B - 1 - j, :, lanes] += part

    _for_blocks(ts, w, block)


def _even_forward_tile(p_ref, halo_ref, first, a_conv_ref, b_conv_ref, bias_ref, lng_ref, lnb_ref, qbuf, ysh, y1buf,
                       w, ts):
    def col(ref, k, rows=slice(None)):
        return ref[rows, k * w:(k + 1) * w]

    a_x, a_b, a_c, a_z = col(p_ref, 0), col(p_ref, 1), col(p_ref, 2), col(p_ref, 3)
    b_val, b_gate, b_z = col(p_ref, 4), col(p_ref, 5), col(p_ref, 6)
    keep = jnp.where(first, 0.0, 1.0)

    rows_a = slice(HALO_B - HALO_A, HALO_B)
    qbuf[0:HALO_A, :] = col(halo_ref, 2, rows_a) * col(halo_ref, 0, rows_a) * keep
    qbuf[HALO_A:HALO_A + ts, :] = a_c * a_x
    cq = jnp.zeros((ts, w), F32)
    for j in range(CONV_A):
        cq = cq + a_conv_ref[CONV_A - 1 - j:CONV_A - j, :] * qbuf[HALO_A - j:HALO_A - j + ts, :]
    ya = a_b * cq

    ysh[0, 0:HALO_B, :] = col(halo_ref, 4) * _sigmoid(col(halo_ref, 5)) * keep
    ysh[0, HALO_B:HALO_B + ts, :] = b_val * _sigmoid(b_gate)
    _fill_shifted_down(ysh, HALO_B + ts)
    _conv31(ysh, HALO_B, -SHIFTS, b_conv_ref, bias_ref, y1buf, ts, w)
    yb1 = y1buf[...]
    xc = yb1 - _mean_last(yb1)
    rstd = lax.rsqrt(_mean_last(xc * xc) + EPS)
    xhat = xc * rstd
    yb2 = xhat * lng_ref[...] + lnb_ref[...]
    return dict(a_x=a_x, a_b=a_b, a_c=a_c, a_z=a_z, b_val=b_val, b_gate=b_gate, b_z=b_z,
                cq=cq, ya=ya, rstd=rstd, xhat=xhat, yb2=yb2)


def _even_specs(s, w, ts):
    tile = pl.BlockSpec((ts, 7 * w), lambda i: (i, 0))
    halo = pl.BlockSpec((HALO_B, 7 * w), lambda i: (jnp.maximum(i * (ts // HALO_B) - 1, 0), 0))
    return tile, halo


def _small_specs(shapes, index=lambda i: (0, 0)):
    return [pl.BlockSpec(sh, index) for sh in shapes]


def _even_mixer_fwd(p, a_conv, b_conv, bias, ln_g, ln_b, name):
    s = p.shape[0]
    w = p.shape[1] // 7
    ts = _row_tile(s, 256)
    assert ts % HALO_B == 0

    def body(p_ref, halo_ref, ac_ref, bc_ref, bias_ref, lng_ref, lnb_ref, u_ref, qbuf, ysh, y1buf):
        first = pl.program_id(0) == 0
        f = _even_forward_tile(p_ref, halo_ref, first, ac_ref, bc_ref, bias_ref, lng_ref, lnb_ref, qbuf, ysh, y1buf,
                               w, ts)
        yb3 = f["yb2"] * _sigmoid(f["yb2"])
        u_ref[:, 0:w] = (f["ya"] * (f["a_z"] * _sigmoid(f["a_z"]))).astype(BF16)
        u_ref[:, w:2 * w] = (yb3 * (f["b_z"] * _sigmoid(f["b_z"]))).astype(BF16)

    tile, halo = _even_specs(s, w, ts)
    return pl.pallas_call(
        body, name=name, grid=(s // ts,),
        in_specs=[tile, halo] + _small_specs([(CONV_A, w), (CONV_B, w), (1, w), (1, w), (1, w)]),
        out_specs=pl.BlockSpec((ts, 2 * w), lambda i: (i, 0)),
        out_shape=jax.ShapeDtypeStruct((s, 2 * w), BF16),
        scratch_shapes=[pltpu.VMEM((HALO_A + ts, w), F32), pltpu.VMEM((SHIFTS, HALO_B + ts, w), F32),
                        pltpu.VMEM((ts, w), F32)],
        compiler_params=_params("arbitrary"),
    )(p, p, a_conv, b_conv, bias, ln_g, ln_b)


def _even_mixer_bwd(p, du, a_conv, b_conv, bias, ln_g, ln_b, name, comm=None):
    s = p.shape[0]
    w = p.shape[1] // 7
    ts = _row_tile(s, 128)
    nt = s // ts
    assert ts % HALO_B == 0

    def body(p_ref, halo_ref, du_ref, ac_ref, bc_ref, bias_ref, lng_ref, lnb_ref,
             dp_ref, dac_ref, dbc_ref, dbias_ref, dlng_ref, dlnb_ref,
             qbuf, ysh, y1buf, dqbuf, dsh, dy0buf, wacc, carry_dq, carry_dy):
        step = pl.program_id(0)
        first = step == nt - 1

        @pl.when(step == 0)
        def _():
            for ref in (dac_ref, dbias_ref, dlng_ref, dlnb_ref, wacc, carry_dq, carry_dy):
                ref[...] = jnp.zeros_like(ref)

        f = _even_forward_tile(p_ref, halo_ref, first, ac_ref, bc_ref, bias_ref, lng_ref, lnb_ref, qbuf, ysh, y1buf,
                               w, ts)
        a_z, b_z, yb2 = f["a_z"], f["b_z"], f["yb2"]
        s_az, s_bz, s_y2 = _sigmoid(a_z), _sigmoid(b_z), _sigmoid(yb2)
        du_a = du_ref[:, 0:w]
        du_b = du_ref[:, w:2 * w]

        d_ya = du_a * (a_z * s_az)
        dp_ref[:, 3 * w:4 * w] = (du_a * f["ya"] * _dsilu(a_z, s_az)).astype(BF16)
        dp_ref[:, 1 * w:2 * w] = (d_ya * f["cq"]).astype(BF16)
        d_cq = d_ya * f["a_b"]
        dqbuf[0:ts, :] = d_cq
        dqbuf[ts:ts + HALO_A, :] = carry_dq[...]
        d_q = jnp.zeros((ts, w), F32)
        for o in range(CONV_A):
            d_q = d_q + ac_ref[CONV_A - 1 - o:CONV_A - o, :] * dqbuf[o:o + ts, :]
        for j in range(CONV_A):
            k = CONV_A - 1 - j
            dac_ref[k:k + 1, :] += _sum_rows(d_cq * qbuf[HALO_A - j:HALO_A - j + ts, :])
        carry_dq[...] = d_cq[0:HALO_A, :]
        dp_ref[:, 2 * w:3 * w] = (d_q * f["a_x"]).astype(BF16)
        dp_ref[:, 0 * w:1 * w] = (d_q * f["a_c"]).astype(BF16)

        d_yb3 = du_b * (b_z * s_bz)
        dp_ref[:, 6 * w:7 * w] = (du_b * (yb2 * s_y2) * _dsilu(b_z, s_bz)).astype(BF16)
        d_yb2 = d_yb3 * _dsilu(yb2, s_y2)
        xhat = f["xhat"]
        dlng_ref[...] += _sum_rows(d_yb2 * xhat)
        dlnb_ref[...] += _sum_rows(d_yb2)
        d_xh = d_yb2 * lng_ref[...]
        d_yb1 = f["rstd"] * (d_xh - _mean_last(d_xh) - xhat * _mean_last(d_xh * xhat))
        dbias_ref[...] += _sum_rows(d_yb1)
        dsh[0, 0:ts, :] = d_yb1
        dsh[0, ts:ts + HALO_B, :] = carry_dy[...]
        carry_dy[...] = d_yb1[0:HALO_B, :]
        _fill_shifted_up(dsh, ts + HALO_B)
        _conv31(dsh, 0, SHIFTS, bc_ref, None, dy0buf, ts, w)
        _conv31_weight_grad(dsh, ysh, wacc, ts, w)

        @pl.when(step == nt - 1)
        def _():
            for k in range(CONV_B):
                dbc_ref[k:k + 1, :] = _sum_rows(wacc[k])

        d_yb0 = dy0buf[...]
        s_g = _sigmoid(f["b_gate"])
        dp_ref[:, 4 * w:5 * w] = (d_yb0 * s_g).astype(BF16)
        dp_ref[:, 5 * w:6 * w] = (d_yb0 * f["b_val"] * s_g * (1.0 - s_g)).astype(BF16)

    rev = lambda i: (nt - 1 - i, 0)
    tile = pl.BlockSpec((ts, 7 * w), rev)
    halo = pl.BlockSpec((HALO_B, 7 * w), lambda i: (jnp.maximum((nt - 1 - i) * (ts // HALO_B) - 1, 0), 0))
    small = [(CONV_A, w), (CONV_B, w), (1, w), (1, w), (1, w)]
    return _call(
        body, name=name, grid=(nt,),
        in_specs=[tile, halo, pl.BlockSpec((ts, 2 * w), rev)] + _small_specs(small),
        out_specs=[pl.BlockSpec((ts, 7 * w), rev)] + _small_specs(small),
        out_shape=[jax.ShapeDtypeStruct((s, 7 * w), BF16)] + [jax.ShapeDtypeStruct(sh, F32) for sh in small],
        scratch_shapes=[pltpu.VMEM((HALO_A + ts, w), F32), pltpu.VMEM((SHIFTS, HALO_B + ts, w), F32),
                        pltpu.VMEM((ts, w), F32),
                        pltpu.VMEM((ts + HALO_A, w), F32), pltpu.VMEM((SHIFTS, ts + HALO_B, w), F32),
                        pltpu.VMEM((ts, w), F32), pltpu.VMEM((CONV_B, SHIFTS, w), F32),
                        pltpu.VMEM((HALO_A, w), F32), pltpu.VMEM((HALO_B, w), F32)],
        args=(p, p, du, a_conv, b_conv, bias, ln_g, ln_b), comm=comm)


def _pool_forward_tile(p_ref, halo_ref, first, tile_index, cw_ref, cb_ref, cs_ref, vbuf, c, gc, ts):
    vbuf[0:HALO_P, :] = halo_ref[...] * jnp.where(first, 0.0, 1.0)
    vbuf[HALO_P:HALO_P + ts, :] = p_ref[:, 0:c]
    pos = tile_index * ts + lax.broadcasted_iota(jnp.int32, (ts, 1), 0) + 1
    pooled, inv, gout = [], [], []
    for g, win in enumerate(POOL_WINDOWS):
        cols = slice(g * gc, (g + 1) * gc)
        acc = jnp.zeros((ts, gc), F32)
        for j in range(win):
            acc = acc + vbuf[HALO_P - j:HALO_P - j + ts, cols]
        inv_g = 1.0 / jnp.minimum(pos, win).astype(F32)
        pooled_g = (acc * inv_g - p_ref[:, cols]).astype(BF16)
        pooled.append(pooled_g)
        inv.append(inv_g)
        gout.append(jnp.dot(pooled_g, cw_ref[g], preferred_element_type=F32) + cb_ref[:, cols])
    return pooled, inv, gout


def _odd_mixer_fwd(p, cw, cb, cs, name):
    s = p.shape[0]
    c = p.shape[1] // 2
    gc = c // N_GROUPS
    ts = _row_tile(s, 256)

    def body(p_ref, halo_ref, cw_ref, cb_ref, cs_ref, u_ref, vbuf):
        i = pl.program_id(0)
        _, _, gout = _pool_forward_tile(p_ref, halo_ref, i == 0, i, cw_ref, cb_ref, cs_ref, vbuf, c, gc, ts)
        for g in range(N_GROUPS):
            cols = slice(g * gc, (g + 1) * gc)
            z = p_ref[:, c + g * gc:c + (g + 1) * gc]
            u_ref[:, cols] = (gout[g] * cs_ref[:, cols] * (z * _sigmoid(z))).astype(BF16)

    return pl.pallas_call(
        body, name=name, grid=(s // ts,),
        in_specs=[pl.BlockSpec((ts, 2 * c), lambda i: (i, 0)),
                  pl.BlockSpec((HALO_P, c), lambda i: (jnp.maximum(i * (ts // HALO_P) - 1, 0), 0)),
                  pl.BlockSpec((N_GROUPS, gc, gc), lambda i: (0, 0, 0)),
                  pl.BlockSpec((1, c), lambda i: (0, 0)), pl.BlockSpec((1, c), lambda i: (0, 0))],
        out_specs=pl.BlockSpec((ts, c), lambda i: (i, 0)),
        out_shape=jax.ShapeDtypeStruct((s, c), BF16),
        scratch_shapes=[pltpu.VMEM((HALO_P + ts, c), F32)],
        compiler_params=_params("arbitrary"),
    )(p, p, cw, cb, cs)


def _odd_mixer_bwd(p, du, cw, cb, cs, name):
    s = p.shape[0]
    c = p.shape[1] // 2
    gc = c // N_GROUPS
    ts = _row_tile(s, 256)
    nt = s // ts

    def body(p_ref, halo_ref, du_ref, cw_ref, cb_ref, cs_ref, dp_ref, dcw_ref, dcb_ref, dcs_ref, vbuf, ebuf, carry_e):
        step = pl.program_id(0)
        tile_index = nt - 1 - step

        @pl.when(step == 0)
        def _():
            for ref in (dcw_ref, dcb_ref, dcs_ref, carry_e):
                ref[...] = jnp.zeros_like(ref)

        pooled, inv, gout = _pool_forward_tile(p_ref, halo_ref, tile_index == 0, tile_index, cw_ref, cb_ref, cs_ref,
                                               vbuf, c, gc, ts)
        ebuf[ts:ts + HALO_P, :] = carry_e[...]
        for g, win in enumerate(POOL_WINDOWS):
            cols = slice(g * gc, (g + 1) * gc)
            z = p_ref[:, c + g * gc:c + (g + 1) * gc]
            sz = _sigmoid(z)
            du_g = du_ref[:, cols]
            scale = cs_ref[:, cols]
            d_y = du_g * (z * sz)
            dp_ref[:, c + g * gc:c + (g + 1) * gc] = (du_g * (gout[g] * scale) * _dsilu(z, sz)).astype(BF16)
            dcs_ref[:, cols] += _sum_rows(d_y * gout[g])
            d_gout = d_y * scale
            dcb_ref[:, cols] += _sum_rows(d_gout)
            d_gout_b = d_gout.astype(BF16)
            dcw_ref[g] += lax.dot_general(pooled[g], d_gout_b, (((0,), (0,)), ((), ())), preferred_element_type=F32)
            d_pool = lax.dot_general(d_gout_b, cw_ref[g], (((1,), (1,)), ((), ())), preferred_element_type=F32)
            e = d_pool * inv[g]
            ebuf[0:ts, cols] = e
            d_v = -d_pool
            for o in range(win):
                d_v = d_v + ebuf[o:o + ts, cols]
            dp_ref[:, cols] = d_v.astype(BF16)
            carry_e[:, cols] = e[0:HALO_P, :]

    rev = lambda i: (nt - 1 - i, 0)
    small = [(N_GROUPS, gc, gc), (1, c), (1, c)]
    return pl.pallas_call(
        body, name=name, grid=(nt,),
        in_specs=[pl.BlockSpec((ts, 2 * c), rev),
                  pl.BlockSpec((HALO_P, c), lambda i: (jnp.maximum((nt - 1 - i) * (ts // HALO_P) - 1, 0), 0)),
                  pl.BlockSpec((ts, c), rev),
                  pl.BlockSpec((N_GROUPS, gc, gc), lambda i: (0, 0, 0)),
                  pl.BlockSpec((1, c), lambda i: (0, 0)), pl.BlockSpec((1, c), lambda i: (0, 0))],
        out_specs=[pl.BlockSpec((ts, 2 * c), rev),
                   pl.BlockSpec((N_GROUPS, gc, gc), lambda i: (0, 0, 0)),
                   pl.BlockSpec((1, c), lambda i: (0, 0)), pl.BlockSpec((1, c), lambda i: (0, 0))],
        out_shape=[jax.ShapeDtypeStruct((s, 2 * c), BF16)] + [jax.ShapeDtypeStruct(sh, F32) for sh in small],
        scratch_shapes=[pltpu.VMEM((HALO_P + ts, c), F32), pltpu.VMEM((ts + HALO_P, c), F32),
                        pltpu.VMEM((HALO_P, c), F32)],
        compiler_params=_params("arbitrary"),
    )(p, p, du, cw, cb, cs)


def _cast_into_slot(a, coords, name):
    r, cols = a.shape
    tr = _row_tile(r // 2, 256)
    per = r // 2 // tr

    def body(co_ref, a_ref, o_ref):
        o_ref[0, 0] = a_ref[...].astype(BF16)

    return pl.pallas_call(
        body, name=name,
        grid_spec=pltpu.PrefetchScalarGridSpec(
            num_scalar_prefetch=1, grid=(2, per),
            in_specs=[pl.BlockSpec((tr, cols), lambda h, i, co: (h * per + i, 0))],
            out_specs=pl.BlockSpec((1, 1, tr, cols), lambda h, i, co: (co[0], h, i, 0))),
        out_shape=jax.ShapeDtypeStruct((N_SHARDS, 2, r // 2, cols), BF16),
        compiler_params=_params("arbitrary", "arbitrary"),
    )(coords, a)


def _chip_sum(g, other, coords, name):
    n_sh, _, r2, cols = g.shape
    tr = _row_tile(r2, 256)

    def body(co_ref, g_ref, o_ref, sum_ref, mine_ref):
        v = (g_ref[0, 0].astype(F32) + o_ref[0].astype(F32)).astype(BF16)
        sum_ref[0] = v

        @pl.when(pl.program_id(1) == co_ref[0])
        def _():
            mine_ref[0] = v

    piece = pl.BlockSpec((1, tr, cols), lambda i, s, co: (s, i, 0))
    return pl.pallas_call(
        body, name=name,
        grid_spec=pltpu.PrefetchScalarGridSpec(
            num_scalar_prefetch=1, grid=(r2 // tr, n_sh),
            in_specs=[pl.BlockSpec((1, 1, tr, cols), lambda i, s, co: (s, co[1], i, 0)), piece],
            out_specs=[piece, pl.BlockSpec((1, tr, cols), lambda i, s, co: (co[0], i, 0))]),
        out_shape=[jax.ShapeDtypeStruct((n_sh, r2, cols), BF16)] * 2,
        compiler_params=_params("arbitrary", "arbitrary"),
    )(coords, g, other)


def _shard_sum(pieces, coords, name):
    n_sh, r2, cols = pieces.shape
    tr = _row_tile(r2, 256)

    def body(co_ref, p_ref, o_ref):
        acc = p_ref[0].astype(F32)
        for k in range(1, n_sh):
            acc = acc + p_ref[k].astype(F32)
        o_ref[0] = acc

    return pl.pallas_call(
        body, name=name,
        grid_spec=pltpu.PrefetchScalarGridSpec(
            num_scalar_prefetch=1, grid=(r2 // tr,),
            in_specs=[pl.BlockSpec((n_sh, tr, cols), lambda i, co: (0, i, 0))],
            out_specs=pl.BlockSpec((1, tr, cols), lambda i, co: (co[1], i, 0))),
        out_shape=jax.ShapeDtypeStruct((2, r2, cols), F32),
        compiler_params=_params("arbitrary"),
    )(coords, pieces)


def _sum_small(a, name):
    n, r, cols = a.shape

    def body(a_ref, o_ref):
        acc = a_ref[0]
        for k in range(1, n):
            acc = acc + a_ref[k]
        o_ref[...] = acc

    return pl.pallas_call(
        body, name=name,
        in_specs=[pl.BlockSpec((n, r, cols), lambda: (0, 0, 0))],
        out_specs=pl.BlockSpec((r, cols), lambda: (0, 0)),
        out_shape=jax.ShapeDtypeStruct((r, cols), F32),
        compiler_params=_params(),
    )(a)


def _adamw(w, g, m, v, name):
    r, cols = w.shape
    tr = _row_tile(r, 256) if r % SUBLANES_BF16 == 0 else r

    def body(w_ref, g_ref, m_ref, v_ref, d_ref, nm_ref, nv_ref):
        g = g_ref[...]
        m = ADAM_B1 * m_ref[...] + (1.0 - ADAM_B1) * g
        v = ADAM_B2 * v_ref[...] + (1.0 - ADAM_B2) * (g * g)
        m_hat = m / (1.0 - ADAM_B1 ** ADAM_STEP)
        v_hat = v / (1.0 - ADAM_B2 ** ADAM_STEP)
        d_ref[...] = -ADAM_LR * (m_hat / (jnp.sqrt(v_hat) + ADAM_EPS) + ADAM_WD * w_ref[...])
        nm_ref[...] = m
        nv_ref[...] = v

    blk = pl.BlockSpec((tr, cols), lambda i: (i, 0))
    return pl.pallas_call(
        body, name=name, grid=(r // tr,),
        in_specs=[blk] * 4, out_specs=[blk] * 3,
        out_shape=[jax.ShapeDtypeStruct((r, cols), F32)] * 3,
        compiler_params=_params("arbitrary"),
    )(w, g, m, v)


def _place():
    x, y, c = lax.axis_index("x"), lax.axis_index("y"), lax.axis_index("c")
    other_chips = [(1 - x, y), (x, 1 - y), (1 - x, 1 - y)]
    return x, y, c, other_chips


def _chip(xy):
    return 2 * xy[0] + xy[1]


def _remote(src, dst, send_sem, recv_sem, to):
    return pltpu.make_async_remote_copy(src_ref=src, dst_ref=dst, send_sem=send_sem, recv_sem=recv_sem,
                                        device_id=to, device_id_type=MESH)


def _gather_comm(bufs, small, forward_at):
    n = len(bufs)

    def ici(ctx, k, j, start):
        (x, y, c, chips), b, send, recv = ctx
        blk = b[k].at[_chip((x, y)) if start else _chip(chips[j]), c]
        return _remote(blk, blk, send.at[6 * k + j], recv.at[6 * k + j], (*chips[j], c))

    def d2d(ctx, k, j, start):
        (x, y, c, chips), b, send, recv = ctx
        blk = b[k].at[_chip(chips[j]), c if start else 1 - c]
        return _remote(blk, blk, send.at[6 * k + 3 + j], recv.at[6 * k + 3 + j], (x, y, 1 - c))

    def small_copy(ctx, j, start):
        (x, y, c, chips), b, send, recv = ctx
        blk = b[n].at[_chip((x, y)) if start else _chip(chips[j])]
        return _remote(blk, blk, send.at[6 * n + j], recv.at[6 * n + j], (*chips[j], c))

    def start(srcs, b, send, recv):
        ctx = (_place(), b, send, recv)
        for k in range(n):
            for j in range(3):
                ici(ctx, k, j, True).start()
        if small is not None:
            for j in range(3):
                small_copy(ctx, j, True).start()

    def forward(srcs, b, send, recv):
        ctx = (_place(), b, send, recv)
        for k in range(n):
            for j in range(3):
                ici(ctx, k, j, False).wait_recv()
                d2d(ctx, k, j, True).start()

    def finish(srcs, b, send, recv):
        ctx = (_place(), b, send, recv)
        if small is not None:
            for j in range(3):
                small_copy(ctx, j, False).wait_recv()
                small_copy(ctx, j, True).wait_send()
        for k in range(n):
            for j in range(3):
                d2d(ctx, k, j, False).wait_recv()
                ici(ctx, k, j, True).wait_send()
                d2d(ctx, k, j, True).wait_send()

    all_bufs = list(bufs) + ([small] if small is not None else [])
    return _Comm([], all_bufs, 6 * n + 3, [(0, start), (forward_at, forward)], finish)


def _exchange_core_halves(grads, name):
    n = len(grads)

    def body(*refs):
        ins, outs = refs[:n], refs[n:2 * n]
        send_sems, recv_sems = refs[2 * n:]
        x, y, c, _ = _place()
        copies = [_remote(ins[k].at[s, 1 - c], outs[k].at[s], send_sems.at[N_SHARDS * k + s],
                          recv_sems.at[N_SHARDS * k + s], (x, y, 1 - c))
                  for k in range(n) for s in range(N_SHARDS)]
        for cp in copies:
            cp.start()
        for cp in copies:
            cp.wait()

    return pl.pallas_call(
        body, name=name, in_specs=[ANY] * n, out_specs=[ANY] * n,
        out_shape=[jax.ShapeDtypeStruct((N_SHARDS,) + a.shape[2:], a.dtype) for a in grads],
        scratch_shapes=[pltpu.SemaphoreType.DMA((N_SHARDS * n,))] * 2,
    )(*grads)


def _scatter_comm(chip_sums, landing):
    n = len(chip_sums)

    def big(srcs, b, send, recv, k, j, start):
        x, y, c, chips = _place()
        dst = b[k].at[_chip((x, y)) if start else _chip(chips[j])]
        return _remote(srcs[k].at[_chip(chips[j])], dst, send.at[3 * k + j], recv.at[3 * k + j], (*chips[j], c))

    def start(srcs, b, send, recv):
        for k in range(n):
            for j in range(3):
                big(srcs, b, send, recv, k, j, True).start()

    def finish(srcs, b, send, recv):
        for k in range(n):
            for j in range(3):
                big(srcs, b, send, recv, k, j, False).wait_recv()
                big(srcs, b, send, recv, k, j, True).wait_send()

    return _Comm(chip_sums, landing, 3 * n, [(0, start)], finish)


def _join_comm(halves, small):
    n = len(halves)
    flips = [(fx, fy, fc) for fx in (0, 1) for fy in (0, 1) for fc in (0, 1)][1:]

    def half(b, send, recv, k, start):
        x, y, c, _ = _place()
        return _remote(b[k].at[c], b[k].at[c if start else 1 - c], send.at[k], recv.at[k], (x, y, 1 - c))

    def small_copy(b, send, recv, q, start):
        x, y, c, _ = _place()
        px, py, pc = x ^ flips[q][0], y ^ flips[q][1], c ^ flips[q][2]
        blk = b[n].at[4 * x + 2 * y + c if start else 4 * px + 2 * py + pc]
        return _remote(blk, blk, send.at[n + q], recv.at[n + q], (px, py, pc))

    def start(srcs, b, send, recv):
        for q in range(len(flips)):
            small_copy(b, send, recv, q, True).start()
        for k in range(n):
            half(b, send, recv, k, True).start()

    def finish(srcs, b, send, recv):
        for q in range(len(flips)):
            small_copy(b, send, recv, q, False).wait()
        for k in range(n):
            half(b, send, recv, k, False).wait()

    return _Comm([], list(halves) + [small], n + len(flips), [(0, start)], finish)


def _flat_rows(parts):
    flat = jnp.concatenate([p.reshape(-1) for p in parts])
    assert flat.shape[0] % LANES == 0
    return flat.reshape(-1, LANES)


def _unflatten(flat, shapes):
    out, at = [], 0
    for sh in shapes:
        size = 1
        for dim in sh:
            size *= dim
        out.append(flat[at:at + size].reshape(sh))
        at += size
    assert at == flat.shape[0], (at, flat.shape)
    return out


def _col_shards_to_full(a, rows):
    q = a.shape[1] // rows
    return a.reshape(N_SHARDS, rows, q).transpose(1, 0, 2).reshape(rows, N_SHARDS * q)


def _my_col_shard(full, chip):
    rows, cols = full.shape
    q = cols // N_SHARDS
    return lax.dynamic_index_in_dim(full.reshape(rows, N_SHARDS, q), chip, axis=1, keepdims=False)


def kernel(x, e_norm_pre, e_norm_post, e_w_in, e_a_conv, e_b_conv, e_b_conv_bias, e_b_ln_g, e_b_ln_b, e_w_out, o_norm_pre, o_norm_post, o_w_in, o_c_w, o_c_b, o_c_scale, o_w_out, loss_target, m_e_norm_pre, m_e_norm_post, m_e_w_in, m_e_a_conv, m_e_b_conv, m_e_b_conv_bias, m_e_b_ln_g, m_e_b_ln_b, m_e_w_out, m_o_norm_pre, m_o_norm_post, m_o_w_in, m_o_c_w, m_o_c_b, m_o_c_scale, m_o_w_out, v_e_norm_pre, v_e_norm_post, v_e_w_in, v_e_a_conv, v_e_b_conv, v_e_b_conv_bias, v_e_b_ln_g, v_e_b_ln_b, v_e_w_out, v_o_norm_pre, v_o_norm_post, v_o_w_in, v_o_c_w, v_o_c_b, v_o_c_scale, v_o_w_out):
    _, s, d = x.shape
    w = d // 2
    c = d
    gc = c // N_GROUPS
    wq, cq, gq = w // N_SHARDS, c // N_SHARDS, gc // N_SHARDS
    chip = 2 * lax.axis_index("x") + lax.axis_index("y")
    core = lax.axis_index("c")
    x2 = x.reshape(s, d)
    target = loss_target.reshape(s, d)

    big_w = [e_w_in[0], e_w_out[0], o_w_in[0], o_c_w[0].reshape(N_GROUPS * gq, gc), o_w_out[0]]
    big_m = [m_e_w_in[0], m_e_w_out[0], m_o_w_in[0], m_o_c_w[0].reshape(N_GROUPS * gq, gc), m_o_w_out[0]]
    big_v = [v_e_w_in[0], v_e_w_out[0], v_o_w_in[0], v_o_c_w[0].reshape(N_GROUPS * gq, gc), v_o_w_out[0]]
    coords = jnp.stack([chip, core]).astype(jnp.int32)
    slots = [_cast_into_slot(a, coords, "cast_%d" % k) for k, a in enumerate(big_w)]
    sharded_small = _flat_rows([e_a_conv[0], e_b_conv[0], o_norm_pre, o_norm_post, o_c_scale, o_c_b[0]])
    small_slots = lax.dynamic_update_index_in_dim(jnp.zeros((N_SHARDS,) + sharded_small.shape, F32), sharded_small,
                                                  chip, 0)
    e_w_in_g, e_w_out_g, small_g4 = _comm_only(_gather_comm(slots[:2], small_slots, 0), "gather_even_weights")
    e_w_in_sm = e_w_in_g.reshape((N_SHARDS,) + big_w[0].shape)
    e_w_out_f = e_w_out_g.reshape(w + w, d)
    sm = small_g4.reshape(N_SHARDS, -1)
    at = [0]

    def take(rows, q):
        blk = sm[:, at[0]:at[0] + rows * q]
        at[0] += rows * q
        return _col_shards_to_full(blk, rows)

    a_conv_f = take(CONV_A, wq)
    b_conv_f = take(CONV_B, wq)
    o_pre_f = take(1, cq)
    o_post_f = take(1, cq)
    cs_f = take(1, cq)
    cb_f = take(N_GROUPS, gq).reshape(1, c)

    in_proj_steps = (s // _row_tile(s, 512)) * N_SHARDS
    (p0, h0), odd_g = _norm_matmul(x2, e_norm_pre, e_w_in_sm, "e_in_proj",
                                   comm=_gather_comm(slots[2:], None, (2 * in_proj_steps) // 3))
    o_w_in_sm = odd_g[0].reshape((N_SHARDS,) + big_w[2].shape)
    cw_f = odd_g[1].reshape(N_SHARDS, N_GROUPS, gq, gc).transpose(1, 0, 2, 3).reshape(N_GROUPS, gc, gc)
    o_w_out_f = odd_g[2].reshape(c, d)
    u0 = _even_mixer_fwd(p0, a_conv_f, b_conv_f, e_b_conv_bias, e_b_ln_g, e_b_ln_b, "e_mixer_fwd")
    x1, y0 = _matmul_post(u0, e_w_out_f, x2, e_norm_post, "e_out_proj")
    (p1, h1), _ = _norm_matmul(x1, o_pre_f, o_w_in_sm, "o_in_proj")
    u1 = _odd_mixer_fwd(p1, cw_f, cb_f, cs_f, "o_mixer_fwd")
    d_y1, d_x2, d_o_post, loss_part = _matmul_post_loss(u1, o_w_out_f, x1, o_post_f, target, "o_out_proj_loss")

    def as_pieces(g, k):
        return g.reshape(N_SHARDS, 2, big_w[k].shape[0] // 2, big_w[k].shape[1])

    def chip_sums(ks, grads, name):
        pieces = [as_pieces(g, k) for k, g in zip(ks, grads)]
        from_sibling = _exchange_core_halves(pieces, name)
        both = [_chip_sum(g, o, coords, "chip_sum_%d" % k) for k, g, o in zip(ks, pieces, from_sibling)]
        return [b[0] for b in both], [b[1] for b in both]

    g_o_w_out = _matmul_tn(u1, d_y1, 1, "o_w_out_grad")
    d_u1, _ = _matmul_nt(d_y1, o_w_out_f[None], "o_out_proj_bwd")
    d_p1, d_cw, d_cb, d_cs = _odd_mixer_bwd(p1, d_u1, cw_f, cb_f, cs_f, "o_mixer_bwd")
    g_o_w_in = _matmul_tn(h1, d_p1, N_SHARDS, "o_w_in_grad")
    d_h1, _ = _matmul_nt(d_p1, o_w_in_sm, "o_in_proj_bwd")
    d_x1, d_o_pre, d_y0, d_e_post = _norm_bwd(d_h1, x1, o_pre_f, d_x2, "o_pre_norm_bwd", post=(y0, e_norm_post))

    g_e_w_out = _matmul_tn(u0, d_y0, 1, "e_w_out_grad")
    g_cw = d_cw.reshape(N_GROUPS, N_SHARDS, gq, gc).transpose(1, 0, 2, 3).astype(BF16)
    sums_a, landing_a = chip_sums([1, 2, 3, 4], [g_e_w_out, g_o_w_in, g_cw, g_o_w_out], "exchange_core_halves_a")
    d_u0, _ = _matmul_nt(d_y0, e_w_out_f[None], "e_out_proj_bwd")
    (d_p0, d_a_conv, d_b_conv, d_bias, d_ln_g, d_ln_b), landed_a = _even_mixer_bwd(
        p0, d_u0, a_conv_f, b_conv_f, e_b_conv_bias, e_b_ln_g, e_b_ln_b, "e_mixer_bwd",
        comm=_scatter_comm(sums_a, landing_a))
    g_e_w_in = _matmul_tn(h0, d_p0, N_SHARDS, "e_w_in_grad")
    sums_b, landing_b = chip_sums([0], [g_e_w_in], "exchange_core_halves_b")
    d_h0, landed_b = _matmul_nt(d_p0, e_w_in_sm, "e_in_proj_bwd", comm=_scatter_comm(sums_b, landing_b))
    grad_x, d_e_pre = _norm_bwd(d_h0, x2, e_norm_pre, d_x1, "e_pre_norm_bwd")

    landed = landed_b + landed_a
    reduced = [_shard_sum(sc, coords, "shard_sum_%d" % k) for k, sc in enumerate(landed)]
    small_parts = _flat_rows([loss_part[0], d_e_pre, d_e_post, d_bias, d_ln_g, d_ln_b, d_a_conv, d_b_conv,
                              d_o_pre, d_o_post, d_cs, d_cb])
    small_rows = lax.dynamic_update_index_in_dim(jnp.zeros((N_DEVICES,) + small_parts.shape, F32), small_parts,
                                                 2 * chip + core, 0)
    joined = _comm_only(_join_comm(reduced, small_rows), "join_core_halves")
    big_g = [j.reshape(a.shape) for j, a in zip(joined[:5], big_w)]
    small_sum = _sum_small(joined[5], "small_sum").reshape(-1)
    (loss_row, g_e_pre, g_e_post, g_bias, g_ln_g, g_ln_b, g_a_conv_f, g_b_conv_f, g_o_pre_f, g_o_post_f, g_cs_f,
     g_cb_f) = _unflatten(small_sum, [(LANES,), (1, d), (1, d), (1, w), (1, w), (1, w), (CONV_A, w), (CONV_B, w),
                                      (1, c), (1, c), (1, c), (1, c)])
    loss = loss_row[0]
    g_a_conv = _my_col_shard(g_a_conv_f, chip)
    g_b_conv = _my_col_shard(g_b_conv_f, chip)
    g_o_pre = _my_col_shard(g_o_pre_f, chip)
    g_o_post = _my_col_shard(g_o_post_f, chip)
    g_cs = _my_col_shard(g_cs_f, chip)
    g_cb = _my_col_shard(g_cb_f.reshape(N_GROUPS, gc), chip)

    big_upd = [_adamw(wt, g, m, v, "adamw_%d" % k) for k, (wt, g, m, v) in enumerate(zip(big_w, big_g, big_m, big_v))]
    small_w = [e_norm_pre, e_norm_post, e_b_conv_bias, e_b_ln_g, e_b_ln_b, e_a_conv[0], e_b_conv[0],
               o_norm_pre, o_norm_post, o_c_b[0], o_c_scale]
    small_m = [m_e_norm_pre, m_e_norm_post, m_e_b_conv_bias, m_e_b_ln_g, m_e_b_ln_b, m_e_a_conv[0], m_e_b_conv[0],
               m_o_norm_pre, m_o_norm_post, m_o_c_b[0], m_o_c_scale]
    small_v = [v_e_norm_pre, v_e_norm_post, v_e_b_conv_bias, v_e_b_ln_g, v_e_b_ln_b, v_e_a_conv[0], v_e_b_conv[0],
               v_o_norm_pre, v_o_norm_post, v_o_c_b[0], v_o_c_scale]
    small_g = [g_e_pre, g_e_post, g_bias, g_ln_g, g_ln_b, g_a_conv, g_b_conv, g_o_pre, g_o_post, g_cb, g_cs]
    small_shapes = [a.shape for a in small_w]
    small_upd = _adamw(_flat_rows(small_w), _flat_rows(small_g), _flat_rows(small_m), _flat_rows(small_v),
                       "adamw_small")
    small_delta, small_new_m, small_new_v = [_unflatten(u.reshape(-1), small_shapes) for u in small_upd]

    def ordered(small, big, lead):
        (n_pre, n_post, bias, ln_g, ln_b, a_conv, b_conv, o_pre, o_post, cb, cs) = small
        (w_in, w_out, ow_in, cw, ow_out) = big
        out = [n_pre, n_post, w_in[None], a_conv[None], b_conv[None], bias, ln_g, ln_b, w_out[None], o_pre, o_post,
               ow_in[None], cw.reshape(1, N_GROUPS, gq, gc), cb[None], cs, ow_out[None]]
        return out

    grads = ordered(small_g, big_g, None)
    deltas = ordered(small_delta, [u[0] for u in big_upd], None)
    new_m = ordered(small_new_m, [u[1] for u in big_upd], None)
    new_v = ordered(small_new_v, [u[2] for u in big_upd], None)
    return (loss, grad_x.reshape(1, s, d), *grads, *deltas, *new_m, *new_v)
```

```python
import functools

import jax
import jax.numpy as jnp
from jax import lax
from jax.experimental import pallas as pl
from jax.experimental.pallas import tpu as pltpu

F32 = jnp.float32
BF16 = jnp.bfloat16
MESH = pl.DeviceIdType.MESH

EPS = 1e-6
CONV_A = 3
CONV_B = 31
POOL_WINDOWS = (2, 4, 8, 16)
N_GROUPS = len(POOL_WINDOWS)
N_SHARDS = 4
N_DEVICES = 8
ADAM_LR = 0.001
ADAM_B1 = 0.9
ADAM_B2 = 0.999
ADAM_EPS = 1e-08
ADAM_WD = 0.01
ADAM_STEP = 10

LANES = 128
SUBLANES_BF16 = 16
HALO_A = 8
HALO_B = 32
HALO_P = 16
SHIFTS = 8
ROW_BLOCK = 32
LANE_BLOCK = 256
VMEM_LIMIT = 56 * 1024 * 1024


def _row_tile(n, pref):
    t = min(n, pref)
    while t > SUBLANES_BF16 and (n % t or t % SUBLANES_BF16):
        t -= SUBLANES_BF16
    assert n % t == 0, (n, pref)
    return t


def _col_chunk(n, pref):
    t = (min(n, pref) // LANES) * LANES
    while t > LANES and n % t:
        t -= LANES
    assert t >= LANES and n % t == 0, (n, pref)
    return t


def _params(*sem):
    return pltpu.CompilerParams(dimension_semantics=tuple(sem) if sem else None, vmem_limit_bytes=VMEM_LIMIT)


ANY = pl.BlockSpec(memory_space=pl.ANY)


class _Comm:
    def __init__(self, srcs, bufs, n_sems, phases, finish):
        self.srcs, self.bufs, self.n_sems, self.phases, self.finish = list(srcs), list(bufs), n_sems, phases, finish


def _call(body, *, name, grid, in_specs, out_specs, out_shape, args, scratch_shapes=(), comm=None):
    params = _params(*(("arbitrary",) * len(grid)))
    if comm is None:
        out = pl.pallas_call(body, name=name, grid=grid, in_specs=in_specs, out_specs=out_specs, out_shape=out_shape,
                             scratch_shapes=scratch_shapes, compiler_params=params)(*args)
        return list(out), []
    n_in, n_out, n_scr = len(in_specs), len(out_specs), len(scratch_shapes)
    ns, nb = len(comm.srcs), len(comm.bufs)
    total = 1
    for size in grid:
        total *= size

    def fused(*refs):
        ins, srcs = refs[:n_in], refs[n_in:n_in + ns]
        at = n_in + ns + nb
        outs, bufs = refs[at:at + n_out], refs[at + n_out:at + n_out + nb]
        scratch = refs[at + n_out + nb:at + n_out + nb + n_scr]
        send_sems, recv_sems = refs[-2:]
        step = 0
        for axis, size in enumerate(grid):
            step = step * size + pl.program_id(axis)
        for when, fn in comm.phases:
            pl.when(step == when)(functools.partial(fn, srcs, bufs, send_sems, recv_sems))
        body(*ins, *outs, *scratch)
        pl.when(step == total - 1)(functools.partial(comm.finish, srcs, bufs, send_sems, recv_sems))

    out = pl.pallas_call(
        fused, name=name, grid=grid,
        in_specs=list(in_specs) + [ANY] * (ns + nb), out_specs=list(out_specs) + [ANY] * nb,
        out_shape=list(out_shape) + [jax.ShapeDtypeStruct(b.shape, b.dtype) for b in comm.bufs],
        input_output_aliases={n_in + ns + i: n_out + i for i in range(nb)},
        scratch_shapes=list(scratch_shapes) + [pltpu.SemaphoreType.DMA((comm.n_sems,))] * 2,
        compiler_params=params,
    )(*args, *comm.srcs, *comm.bufs)
    return list(out[:n_out]), list(out[n_out:])


def _comm_only(comm, name):
    ns, nb = len(comm.srcs), len(comm.bufs)

    def body(*refs):
        srcs, bufs = refs[:ns], refs[ns + nb:ns + 2 * nb]
        send_sems, recv_sems = refs[-2:]
        for _, fn in comm.phases:
            fn(srcs, bufs, send_sems, recv_sems)
        comm.finish(srcs, bufs, send_sems, recv_sems)

    return pl.pallas_call(
        body, name=name, in_specs=[ANY] * (ns + nb), out_specs=[ANY] * nb,
        out_shape=[jax.ShapeDtypeStruct(b.shape, b.dtype) for b in comm.bufs],
        input_output_aliases={ns + i: i for i in range(nb)},
        scratch_shapes=[pltpu.SemaphoreType.DMA((comm.n_sems,))] * 2,
    )(*comm.srcs, *comm.bufs)


def _sigmoid(v):
    return jax.nn.sigmoid(v)


def _dsilu(v, s):
    return s * (1.0 + v * (1.0 - s))


def _mean_last(v):
    return jnp.mean(v, axis=-1, keepdims=True)


def _sum_rows(v):
    return jnp.sum(v, axis=0, keepdims=True)


def _norm_matmul(x, gain, w_sm, name, comm=None):
    s, d = x.shape
    n_sh, _, ns = w_sm.shape
    tm = _row_tile(s, 512)

    def body(x_ref, g_ref, w_ref, p_ref, h_ref):
        @pl.when(pl.program_id(1) == 0)
        def _():
            xv = x_ref[...]
            r = lax.rsqrt(_mean_last(xv * xv) + EPS)
            h_ref[...] = (xv * r * g_ref[...]).astype(BF16)

        p_ref[...] = jnp.dot(h_ref[...], w_ref[0], preferred_element_type=F32)

    return _call(
        body, name=name, grid=(s // tm, n_sh),
        in_specs=[pl.BlockSpec((tm, d), lambda i, j: (i, 0)),
                  pl.BlockSpec((1, d), lambda i, j: (0, 0)),
                  pl.BlockSpec((1, d, ns), lambda i, j: (j, 0, 0))],
        out_specs=[pl.BlockSpec((tm, ns), lambda i, j: (i, j)),
                   pl.BlockSpec((tm, d), lambda i, j: (i, 0))],
        out_shape=[jax.ShapeDtypeStruct((s, n_sh * ns), F32), jax.ShapeDtypeStruct((s, d), BF16)],
        args=(x, gain, w_sm), comm=comm)


def _matmul_post_loss(u, w, x_res, gain, target, name):
    s, k = u.shape
    d = w.shape[1]
    tm = _row_tile(s, 256)

    def body(u_ref, w_ref, x_ref, g_ref, t_ref, dy_ref, dout_ref, dg_ref, loss_ref):
        @pl.when(pl.program_id(0) == 0)
        def _():
            dg_ref[...] = jnp.zeros_like(dg_ref)
            loss_ref[...] = jnp.zeros_like(loss_ref)

        y = jnp.dot(u_ref[...], w_ref[...], preferred_element_type=F32)
        r = lax.rsqrt(_mean_last(y * y) + EPS)
        n = y * r
        g = g_ref[...]
        err = x_ref[...] + n * g - t_ref[...]
        loss_ref[...] += 0.5 * jnp.sum(_mean_last(err * err))
        dout = err * (1.0 / d)
        dout_ref[...] = dout
        dg_ref[...] += _sum_rows(dout * n)
        dn = dout * g
        dy_ref[...] = (r * (dn - n * _mean_last(dn * n))).astype(BF16)

    return pl.pallas_call(
        body, name=name, grid=(s // tm,),
        in_specs=[pl.BlockSpec((tm, k), lambda i: (i, 0)),
                  pl.BlockSpec((k, d), lambda i: (0, 0)),
                  pl.BlockSpec((tm, d), lambda i: (i, 0)),
                  pl.BlockSpec((1, d), lambda i: (0, 0)),
                  pl.BlockSpec((tm, d), lambda i: (i, 0))],
        out_specs=[pl.BlockSpec((tm, d), lambda i: (i, 0)),
                   pl.BlockSpec((tm, d), lambda i: (i, 0)),
                   pl.BlockSpec((1, d), lambda i: (0, 0)),
                   pl.BlockSpec((8, LANES), lambda i: (0, 0))],
        out_shape=[jax.ShapeDtypeStruct((s, d), BF16), jax.ShapeDtypeStruct((s, d), F32),
                   jax.ShapeDtypeStruct((1, d), F32), jax.ShapeDtypeStruct((8, LANES), F32)],
        compiler_params=_params("arbitrary"),
    )(u, w, x_res, gain, target)


def _matmul_nt(a, w_sm, name, comm=None):
    s, ncols = a.shape
    n_sh, r, ns = w_sm.shape
    assert ncols == n_sh * ns
    tm = _row_tile(s, 512)
    nc = _col_chunk(ns, 1792)
    per = ns // nc
    steps = n_sh * per

    def body(a_ref, w_ref, o_ref):
        part = lax.dot_general(a_ref[...], w_ref[0], (((1,), (1,)), ((), ())), preferred_element_type=F32)

        @pl.when(pl.program_id(1) == 0)
        def _():
            o_ref[...] = part

        @pl.when(pl.program_id(1) > 0)
        def _():
            o_ref[...] += part

    out, bufs = _call(
        body, name=name, grid=(s // tm, steps),
        in_specs=[pl.BlockSpec((tm, nc), lambda i, j: (i, j)),
                  pl.BlockSpec((1, r, nc), lambda i, j: (j // per, 0, j % per))],
        out_specs=[pl.BlockSpec((tm, r), lambda i, j: (i, 0))],
        out_shape=[jax.ShapeDtypeStruct((s, r), F32)],
        args=(a, w_sm), comm=comm)
    return out[0], bufs


def _matmul_tn(a, b, n_sh, name):
    s, k = a.shape
    n = b.shape[1]
    ns = n // n_sh
    tk = _row_tile(k, 1024)
    ts = _row_tile(s, 1024)
    n_s = s // ts

    def body(a_ref, b_ref, o_ref, acc_ref):
        part = lax.dot_general(a_ref[...], b_ref[...], (((0,), (0,)), ((), ())), preferred_element_type=F32)

        @pl.when(pl.program_id(2) == 0)
        def _():
            acc_ref[...] = part

        @pl.when(pl.program_id(2) > 0)
        def _():
            acc_ref[...] += part

        @pl.when(pl.program_id(2) == n_s - 1)
        def _():
            o_ref[0] = acc_ref[...].astype(BF16)

    return pl.pallas_call(
        body, name=name, grid=(n_sh, k // tk, n_s),
        in_specs=[pl.BlockSpec((ts, tk), lambda j, i, t: (t, i)),
                  pl.BlockSpec((ts, ns), lambda j, i, t: (t, j))],
        out_specs=pl.BlockSpec((1, tk, ns), lambda j, i, t: (j, i, 0)),
        out_shape=jax.ShapeDtypeStruct((n_sh, k, ns), BF16),
        scratch_shapes=[pltpu.VMEM((tk, ns), F32)],
        compiler_params=_params("arbitrary", "arbitrary", "arbitrary"),
    )(a, b)


def _norm_bwd(dh, x, gain, dres, name, post=None):
    s, d = x.shape
    tm = _row_tile(s, 256)
    with_post = post is not None

    def rms_bwd(dout, v, g):
        r = lax.rsqrt(_mean_last(v * v) + EPS)
        n = v * r
        dn = dout * g
        return r * (dn - n * _mean_last(dn * n)), _sum_rows(dout * n)

    def body(*refs):
        if with_post:
            dh_ref, x_ref, g_ref, dres_ref, y_ref, gp_ref, dx_ref, dg_ref, dy_ref, dgp_ref = refs
        else:
            dh_ref, x_ref, g_ref, dres_ref, dx_ref, dg_ref = refs

        @pl.when(pl.program_id(0) == 0)
        def _():
            dg_ref[...] = jnp.zeros_like(dg_ref)
            if with_post:
                dgp_ref[...] = jnp.zeros_like(dgp_ref)

        dv, dg = rms_bwd(dh_ref[...], x_ref[...], g_ref[...])
        dx = dres_ref[...] + dv
        dx_ref[...] = dx
        dg_ref[...] += dg
        if with_post:
            dy, dgp = rms_bwd(dx, y_ref[...], gp_ref[...])
            dy_ref[...] = dy.astype(BF16)
            dgp_ref[...] += dgp

    row = pl.BlockSpec((tm, d), lambda i: (i, 0))
    vec = pl.BlockSpec((1, d), lambda i: (0, 0))
    in_specs = [row, row, vec, row]
    out_specs = [row, vec]
    out_shape = [jax.ShapeDtypeStruct((s, d), F32), jax.ShapeDtypeStruct((1, d), F32)]
    args = [dh, x, gain, dres]
    if with_post:
        in_specs += [row, vec]
        out_specs += [row, vec]
        out_shape += [jax.ShapeDtypeStruct((s, d), BF16), jax.ShapeDtypeStruct((1, d), F32)]
        args += list(post)
    return pl.pallas_call(
        body, name=name, grid=(s // tm,), in_specs=in_specs, out_specs=out_specs, out_shape=out_shape,
        compiler_params=_params("arbitrary"),
    )(*args)


def _fill_shifted_down(sh, rows):
    for b in range(1, SHIFTS):
        sh[b, SHIFTS:rows, :] = sh[0, SHIFTS - b:rows - b, :]


def _fill_shifted_up(sh, rows):
    for b in range(1, SHIFTS):
        sh[b, 0:rows - SHIFTS, :] = sh[0, b:rows - SHIFTS + b, :]


def _for_blocks(ts, w, fn):
    lb = min(LANE_BLOCK, w)
    for l0 in range(0, w, lb):
        def rows(rb, carry, l0=l0):
            fn(pl.multiple_of(rb * ROW_BLOCK, ROW_BLOCK), slice(l0, l0 + lb))
            return carry

        lax.fori_loop(0, ts // ROW_BLOCK, rows, 0)


TAP_SPAN = SHIFTS * ((CONV_B - 1) // SHIFTS)
WINDOW = ROW_BLOCK + TAP_SPAN


def _taps_of(b):
    return [(a, SHIFTS * a + b) for a in range((CONV_B - 1 - b) // SHIFTS + 1)]


def _conv31(sh, base, step, wt_ref, bias_ref, out_ref, ts, w):
    low = min(0, step * (TAP_SPAN // SHIFTS))

    def block(r0, lanes):
        acc = jnp.zeros((ROW_BLOCK, lanes.stop - lanes.start), F32)
        if bias_ref is not None:
            acc = acc + bias_ref[:, lanes]
        for b in range(SHIFTS):
            window = sh[b, pl.ds(pl.multiple_of(r0 + (base + low), SHIFTS), WINDOW), lanes]
            for a, j in _taps_of(b):
                at = step * a - low
                acc = acc + wt_ref[CONV_B - 1 - j:CONV_B - j, lanes] * window[at:at + ROW_BLOCK, :]
        out_ref[pl.ds(r0, ROW_BLOCK), lanes] = acc

    _for_blocks(ts, w, block)


def _conv31_weight_grad(d_sh, x_sh, wacc, ts, w):
    def block(r0, lanes):
        d = d_sh[0, pl.ds(r0, ROW_BLOCK), lanes]
        for b in range(SHIFTS):
            window = x_sh[b, pl.ds(pl.multiple_of(r0 + (HALO_B - TAP_SPAN), SHIFTS), WINDOW), lanes]
            for a, j in _taps_of(b):
                at = TAP_SPAN - SHIFTS * a
                prod = d * window[at:at + ROW_BLOCK, :]
                part = prod[0:SHIFTS, :]
                for q in range(1, ROW_BLOCK // SHIFTS):
                    part = part + prod[q * SHIFTS:(q + 1) * SHIFTS, :]
                wacc[CONV_B - 1 - j, :, lanes] += part

    _for_blocks(ts, w, block)


def _even_forward_tile(p_ref, halo_ref, first, a_conv_ref, b_conv_ref, bias_ref, lng_ref, lnb_ref, qbuf, ysh, y1buf,
                       w, ts):
    def col(ref, k, rows=slice(None)):
        return ref[rows, k * w:(k + 1) * w]

    a_x, a_b, a_c, a_z = col(p_ref, 0), col(p_ref, 1), col(p_ref, 2), col(p_ref, 3)
    b_val, b_gate, b_z = col(p_ref, 4), col(p_ref, 5), col(p_ref, 6)
    keep = jnp.where(first, 0.0, 1.0)

    rows_a = slice(HALO_B - HALO_A, HALO_B)
    qbuf[0:HALO_A, :] = col(halo_ref, 2, rows_a) * col(halo_ref, 0, rows_a) * keep
    qbuf[HALO_A:HALO_A + ts, :] = a_c * a_x
    cq = jnp.zeros((ts, w), F32)
    for j in range(CONV_A):
        cq = cq + a_conv_ref[CONV_A - 1 - j:CONV_A - j, :] * qbuf[HALO_A - j:HALO_A - j + ts, :]
    ya = a_b * cq

    ysh[0, 0:HALO_B, :] = col(halo_ref, 4) * _sigmoid(col(halo_ref, 5)) * keep
    ysh[0, HALO_B:HALO_B + ts, :] = b_val * _sigmoid(b_gate)
    _fill_shifted_down(ysh, HALO_B + ts)
    _conv31(ysh, HALO_B, -SHIFTS, b_conv_ref, bias_ref, y1buf, ts, w)
    yb1 = y1buf[...]
    xc = yb1 - _mean_last(yb1)
    rstd = lax.rsqrt(_mean_last(xc * xc) + EPS)
    xhat = xc * rstd
    yb2 = xhat * lng_ref[...] + lnb_ref[...]
    return dict(a_x=a_x, a_b=a_b, a_c=a_c, a_z=a_z, b_val=b_val, b_gate=b_gate, b_z=b_z,
                cq=cq, ya=ya, rstd=rstd, xhat=xhat, yb2=yb2)


def _even_specs(s, w, ts):
    tile = pl.BlockSpec((ts, 7 * w), lambda i: (i, 0))
    halo = pl.BlockSpec((HALO_B, 7 * w), lambda i: (jnp.maximum(i * (ts // HALO_B) - 1, 0), 0))
    return tile, halo


def _small_specs(shapes, index=lambda i: (0, 0)):
    return [pl.BlockSpec(sh, index) for sh in shapes]


def _even_mixer_fwd(p, w_out, x_res, gain, a_conv, b_conv, bias, ln_g, ln_b, name):
    s, d = x_res.shape
    w = p.shape[1] // 7
    ts = _row_tile(s, 128)
    assert ts % HALO_B == 0

    def body(p_ref, halo_ref, wout_ref, x_ref, g_ref, ac_ref, bc_ref, bias_ref, lng_ref, lnb_ref,
             u_ref, xn_ref, y_ref, qbuf, ysh, y1buf):
        first = pl.program_id(0) == 0
        f = _even_forward_tile(p_ref, halo_ref, first, ac_ref, bc_ref, bias_ref, lng_ref, lnb_ref, qbuf, ysh, y1buf,
                               w, ts)
        yb3 = f["yb2"] * _sigmoid(f["yb2"])
        u_a = (f["ya"] * (f["a_z"] * _sigmoid(f["a_z"]))).astype(BF16)
        u_b = (yb3 * (f["b_z"] * _sigmoid(f["b_z"]))).astype(BF16)
        u_ref[:, 0:w] = u_a
        u_ref[:, w:2 * w] = u_b
        y = (jnp.dot(u_a, wout_ref[0:w, :], preferred_element_type=F32)
             + jnp.dot(u_b, wout_ref[w:2 * w, :], preferred_element_type=F32))
        r = lax.rsqrt(_mean_last(y * y) + EPS)
        y_ref[...] = y
        xn_ref[...] = x_ref[...] + (y * r) * g_ref[...]

    tile, halo = _even_specs(s, w, ts)
    row = pl.BlockSpec((ts, d), lambda i: (i, 0))
    return pl.pallas_call(
        body, name=name, grid=(s // ts,),
        in_specs=[tile, halo, pl.BlockSpec((2 * w, d), lambda i: (0, 0)), row, pl.BlockSpec((1, d), lambda i: (0, 0))]
        + _small_specs([(CONV_A, w), (CONV_B, w), (1, w), (1, w), (1, w)]),
        out_specs=[pl.BlockSpec((ts, 2 * w), lambda i: (i, 0)), row, row],
        out_shape=[jax.ShapeDtypeStruct((s, 2 * w), BF16), jax.ShapeDtypeStruct((s, d), F32),
                   jax.ShapeDtypeStruct((s, d), F32)],
        scratch_shapes=[pltpu.VMEM((HALO_A + ts, w), F32), pltpu.VMEM((SHIFTS, HALO_B + ts, w), F32),
                        pltpu.VMEM((ts, w), F32)],
        compiler_params=_params("arbitrary"),
    )(p, p, w_out, x_res, gain, a_conv, b_conv, bias, ln_g, ln_b)


def _even_mixer_bwd(p, dy, w_out, a_conv, b_conv, bias, ln_g, ln_b, name, comm=None):
    s, d = dy.shape
    w = p.shape[1] // 7
    ts = _row_tile(s, 128)
    nt = s // ts
    assert ts % HALO_B == 0

    def body(p_ref, halo_ref, dy_ref, wout_ref, ac_ref, bc_ref, bias_ref, lng_ref, lnb_ref,
             dp_ref, dac_ref, dbc_ref, dbias_ref, dlng_ref, dlnb_ref,
             qbuf, ysh, y1buf, dqbuf, dsh, dy0buf, wacc, carry_dq, carry_dy):
        step = pl.program_id(0)
        first = step == nt - 1

        @pl.when(step == 0)
        def _():
            for ref in (dac_ref, dbias_ref, dlng_ref, dlnb_ref, wacc, carry_dq, carry_dy):
                ref[...] = jnp.zeros_like(ref)

        f = _even_forward_tile(p_ref, halo_ref, first, ac_ref, bc_ref, bias_ref, lng_ref, lnb_ref, qbuf, ysh, y1buf,
                               w, ts)
        a_z, b_z, yb2 = f["a_z"], f["b_z"], f["yb2"]
        s_az, s_bz, s_y2 = _sigmoid(a_z), _sigmoid(b_z), _sigmoid(yb2)
        nt_dims = (((1,), (1,)), ((), ()))
        du_a = lax.dot_general(dy_ref[...], wout_ref[0:w, :], nt_dims, preferred_element_type=F32)
        du_b = lax.dot_general(dy_ref[...], wout_ref[w:2 * w, :], nt_dims, preferred_element_type=F32)

        d_ya = du_a * (a_z * s_az)
        dp_ref[:, 3 * w:4 * w] = (du_a * f["ya"] * _dsilu(a_z, s_az)).astype(BF16)
        dp_ref[:, 1 * w:2 * w] = (d_ya * f["cq"]).astype(BF16)
        d_cq = d_ya * f["a_b"]
        dqbuf[0:ts, :] = d_cq
        dqbuf[ts:ts + HALO_A, :] = carry_dq[...]
        d_q = jnp.zeros((ts, w), F32)
        for o in range(CONV_A):
            d_q = d_q + ac_ref[CONV_A - 1 - o:CONV_A - o, :] * dqbuf[o:o + ts, :]
        for j in range(CONV_A):
            k = CONV_A - 1 - j
            dac_ref[k:k + 1, :] += _sum_rows(d_cq * qbuf[HALO_A - j:HALO_A - j + ts, :])
        carry_dq[...] = d_cq[0:HALO_A, :]
        dp_ref[:, 2 * w:3 * w] = (d_q * f["a_x"]).astype(BF16)
        dp_ref[:, 0 * w:1 * w] = (d_q * f["a_c"]).astype(BF16)

        d_yb3 = du_b * (b_z * s_bz)
        dp_ref[:, 6 * w:7 * w] = (du_b * (yb2 * s_y2) * _dsilu(b_z, s_bz)).astype(BF16)
        d_yb2 = d_yb3 * _dsilu(yb2, s_y2)
        xhat = f["xhat"]
        dlng_ref[...] += _sum_rows(d_yb2 * xhat)
        dlnb_ref[...] += _sum_rows(d_yb2)
        d_xh = d_yb2 * lng_ref[...]
        d_yb1 = f["rstd"] * (d_xh - _mean_last(d_xh) - xhat * _mean_last(d_xh * xhat))
        dbias_ref[...] += _sum_rows(d_yb1)
        dsh[0, 0:ts, :] = d_yb1
        dsh[0, ts:ts + HALO_B, :] = carry_dy[...]
        carry_dy[...] = d_yb1[0:HALO_B, :]
        _fill_shifted_up(dsh, ts + HALO_B)
        _conv31(dsh, 0, SHIFTS, bc_ref, None, dy0buf, ts, w)
        _conv31_weight_grad(dsh, ysh, wacc, ts, w)

        @pl.when(step == nt - 1)
        def _():
            for k in range(CONV_B):
                dbc_ref[k:k + 1, :] = _sum_rows(wacc[k])

        d_yb0 = dy0buf[...]
        s_g = _sigmoid(f["b_gate"])
        dp_ref[:, 4 * w:5 * w] = (d_yb0 * s_g).astype(BF16)
        dp_ref[:, 5 * w:6 * w] = (d_yb0 * f["b_val"] * s_g * (1.0 - s_g)).astype(BF16)

    rev = lambda i: (nt - 1 - i, 0)
    tile = pl.BlockSpec((ts, 7 * w), rev)
    halo = pl.BlockSpec((HALO_B, 7 * w), lambda i: (jnp.maximum((nt - 1 - i) * (ts // HALO_B) - 1, 0), 0))
    small = [(CONV_A, w), (CONV_B, w), (1, w), (1, w), (1, w)]
    return _call(
        body, name=name, grid=(nt,),
        in_specs=[tile, halo, pl.BlockSpec((ts, d), rev), pl.BlockSpec((2 * w, d), lambda i: (0, 0))]
        + _small_specs(small),
        out_specs=[pl.BlockSpec((ts, 7 * w), rev)] + _small_specs(small),
        out_shape=[jax.ShapeDtypeStruct((s, 7 * w), BF16)] + [jax.ShapeDtypeStruct(sh, F32) for sh in small],
        scratch_shapes=[pltpu.VMEM((HALO_A + ts, w), F32), pltpu.VMEM((SHIFTS, HALO_B + ts, w), F32),
                        pltpu.VMEM((ts, w), F32),
                        pltpu.VMEM((ts + HALO_A, w), F32), pltpu.VMEM((SHIFTS, ts + HALO_B, w), F32),
                        pltpu.VMEM((ts, w), F32), pltpu.VMEM((CONV_B, SHIFTS, w), F32),
                        pltpu.VMEM((HALO_A, w), F32), pltpu.VMEM((HALO_B, w), F32)],
        args=(p, p, dy, w_out, a_conv, b_conv, bias, ln_g, ln_b), comm=comm)


def _pool_forward_tile(p_ref, halo_ref, first, tile_index, cw_ref, cb_ref, cs_ref, vbuf, c, gc, ts):
    vbuf[0:HALO_P, :] = halo_ref[...] * jnp.where(first, 0.0, 1.0)
    vbuf[HALO_P:HALO_P + ts, :] = p_ref[:, 0:c]
    pos = tile_index * ts + lax.broadcasted_iota(jnp.int32, (ts, 1), 0) + 1
    pooled, inv, gout = [], [], []
    for g, win in enumerate(POOL_WINDOWS):
        cols = slice(g * gc, (g + 1) * gc)
        acc = jnp.zeros((ts, gc), F32)
        for j in range(win):
            acc = acc + vbuf[HALO_P - j:HALO_P - j + ts, cols]
        inv_g = 1.0 / jnp.minimum(pos, win).astype(F32)
        pooled_g = (acc * inv_g - p_ref[:, cols]).astype(BF16)
        pooled.append(pooled_g)
        inv.append(inv_g)
        gout.append(jnp.dot(pooled_g, cw_ref[g], preferred_element_type=F32) + cb_ref[:, cols])
    return pooled, inv, gout


def _odd_mixer_fwd(p, cw, cb, cs, name):
    s = p.shape[0]
    c = p.shape[1] // 2
    gc = c // N_GROUPS
    ts = _row_tile(s, 256)

    def body(p_ref, halo_ref, cw_ref, cb_ref, cs_ref, u_ref, vbuf):
        i = pl.program_id(0)
        _, _, gout = _pool_forward_tile(p_ref, halo_ref, i == 0, i, cw_ref, cb_ref, cs_ref, vbuf, c, gc, ts)
        for g in range(N_GROUPS):
            cols = slice(g * gc, (g + 1) * gc)
            z = p_ref[:, c + g * gc:c + (g + 1) * gc]
            u_ref[:, cols] = (gout[g] * cs_ref[:, cols] * (z * _sigmoid(z))).astype(BF16)

    return pl.pallas_call(
        body, name=name, grid=(s // ts,),
        in_specs=[pl.BlockSpec((ts, 2 * c), lambda i: (i, 0)),
                  pl.BlockSpec((HALO_P, c), lambda i: (jnp.maximum(i * (ts // HALO_P) - 1, 0), 0)),
                  pl.BlockSpec((N_GROUPS, gc, gc), lambda i: (0, 0, 0)),
                  pl.BlockSpec((1, c), lambda i: (0, 0)), pl.BlockSpec((1, c), lambda i: (0, 0))],
        out_specs=pl.BlockSpec((ts, c), lambda i: (i, 0)),
        out_shape=jax.ShapeDtypeStruct((s, c), BF16),
        scratch_shapes=[pltpu.VMEM((HALO_P + ts, c), F32)],
        compiler_params=_params("arbitrary"),
    )(p, p, cw, cb, cs)


def _odd_mixer_bwd(p, du, cw, cb, cs, name):
    s = p.shape[0]
    c = p.shape[1] // 2
    gc = c // N_GROUPS
    ts = _row_tile(s, 256)
    nt = s // ts

    def body(p_ref, halo_ref, du_ref, cw_ref, cb_ref, cs_ref, dp_ref, dcw_ref, dcb_ref, dcs_ref, vbuf, ebuf, carry_e):
        step = pl.program_id(0)
        tile_index = nt - 1 - step

        @pl.when(step == 0)
        def _():
            for ref in (dcw_ref, dcb_ref, dcs_ref, carry_e):
                ref[...] = jnp.zeros_like(ref)

        pooled, inv, gout = _pool_forward_tile(p_ref, halo_ref, tile_index == 0, tile_index, cw_ref, cb_ref, cs_ref,
                                               vbuf, c, gc, ts)
        ebuf[ts:ts + HALO_P, :] = carry_e[...]
        for g, win in enumerate(POOL_WINDOWS):
            cols = slice(g * gc, (g + 1) * gc)
            z = p_ref[:, c + g * gc:c + (g + 1) * gc]
            sz = _sigmoid(z)
            du_g = du_ref[:, cols]
            scale = cs_ref[:, cols]
            d_y = du_g * (z * sz)
            dp_ref[:, c + g * gc:c + (g + 1) * gc] = (du_g * (gout[g] * scale) * _dsilu(z, sz)).astype(BF16)
            dcs_ref[:, cols] += _sum_rows(d_y * gout[g])
            d_gout = d_y * scale
            dcb_ref[:, cols] += _sum_rows(d_gout)
            d_gout_b = d_gout.astype(BF16)
            dcw_ref[g] += lax.dot_general(pooled[g], d_gout_b, (((0,), (0,)), ((), ())), preferred_element_type=F32)
            d_pool = lax.dot_general(d_gout_b, cw_ref[g], (((1,), (1,)), ((), ())), preferred_element_type=F32)
            e = d_pool * inv[g]
            ebuf[0:ts, cols] = e
            d_v = -d_pool
            for o in range(win):
                d_v = d_v + ebuf[o:o + ts, cols]
            dp_ref[:, cols] = d_v.astype(BF16)
            carry_e[:, cols] = e[0:HALO_P, :]

    rev = lambda i: (nt - 1 - i, 0)
    small = [(N_GROUPS, gc, gc), (1, c), (1, c)]
    return pl.pallas_call(
        body, name=name, grid=(nt,),
        in_specs=[pl.BlockSpec((ts, 2 * c), rev),
                  pl.BlockSpec((HALO_P, c), lambda i: (jnp.maximum((nt - 1 - i) * (ts // HALO_P) - 1, 0), 0)),
                  pl.BlockSpec((ts, c), rev),
                  pl.BlockSpec((N_GROUPS, gc, gc), lambda i: (0, 0, 0)),
                  pl.BlockSpec((1, c), lambda i: (0, 0)), pl.BlockSpec((1, c), lambda i: (0, 0))],
        out_specs=[pl.BlockSpec((ts, 2 * c), rev),
                   pl.BlockSpec((N_GROUPS, gc, gc), lambda i: (0, 0, 0)),
                   pl.BlockSpec((1, c), lambda i: (0, 0)), pl.BlockSpec((1, c), lambda i: (0, 0))],
        out_shape=[jax.ShapeDtypeStruct((s, 2 * c), BF16)] + [jax.ShapeDtypeStruct(sh, F32) for sh in small],
        scratch_shapes=[pltpu.VMEM((HALO_P + ts, c), F32), pltpu.VMEM((ts + HALO_P, c), F32),
                        pltpu.VMEM((HALO_P, c), F32)],
        compiler_params=_params("arbitrary"),
    )(p, p, du, cw, cb, cs)


def _cast_into_slot(a, coords, name):
    r, cols = a.shape
    tr = _row_tile(r // 2, 256)
    per = r // 2 // tr

    def body(co_ref, a_ref, o_ref):
        o_ref[0, 0] = a_ref[...].astype(BF16)

    return pl.pallas_call(
        body, name=name,
        grid_spec=pltpu.PrefetchScalarGridSpec(
            num_scalar_prefetch=1, grid=(2, per),
            in_specs=[pl.BlockSpec((tr, cols), lambda h, i, co: (h * per + i, 0))],
            out_specs=pl.BlockSpec((1, 1, tr, cols), lambda h, i, co: (co[0], h, i, 0))),
        out_shape=jax.ShapeDtypeStruct((N_SHARDS, 2, r // 2, cols), BF16),
        compiler_params=_params("arbitrary", "arbitrary"),
    )(coords, a)


def _chip_sum(g, other, coords, name):
    n_sh, _, r2, cols = g.shape
    tr = _row_tile(r2, 256)

    def body(co_ref, g_ref, o_ref, sum_ref, mine_ref):
        v = (g_ref[0, 0].astype(F32) + o_ref[0].astype(F32)).astype(BF16)
        sum_ref[0] = v

        @pl.when(pl.program_id(1) == co_ref[0])
        def _():
            mine_ref[0] = v

    piece = pl.BlockSpec((1, tr, cols), lambda i, s, co: (s, i, 0))
    return pl.pallas_call(
        body, name=name,
        grid_spec=pltpu.PrefetchScalarGridSpec(
            num_scalar_prefetch=1, grid=(r2 // tr, n_sh),
            in_specs=[pl.BlockSpec((1, 1, tr, cols), lambda i, s, co: (s, co[1], i, 0)), piece],
            out_specs=[piece, pl.BlockSpec((1, tr, cols), lambda i, s, co: (co[0], i, 0))]),
        out_shape=[jax.ShapeDtypeStruct((n_sh, r2, cols), BF16)] * 2,
        compiler_params=_params("arbitrary", "arbitrary"),
    )(coords, g, other)


def _shard_sum(pieces, coords, name):
    n_sh, r2, cols = pieces.shape
    tr = _row_tile(r2, 256)

    def body(co_ref, p_ref, o_ref):
        acc = p_ref[0].astype(F32)
        for k in range(1, n_sh):
            acc = acc + p_ref[k].astype(F32)
        o_ref[0] = acc

    return pl.pallas_call(
        body, name=name,
        grid_spec=pltpu.PrefetchScalarGridSpec(
            num_scalar_prefetch=1, grid=(r2 // tr,),
            in_specs=[pl.BlockSpec((n_sh, tr, cols), lambda i, co: (0, i, 0))],
            out_specs=pl.BlockSpec((1, tr, cols), lambda i, co: (co[1], i, 0))),
        out_shape=jax.ShapeDtypeStruct((2, r2, cols), F32),
        compiler_params=_params("arbitrary"),
    )(coords, pieces)


def _sum_small(a, name):
    n, r, cols = a.shape

    def body(a_ref, o_ref):
        acc = a_ref[0]
        for k in range(1, n):
            acc = acc + a_ref[k]
        o_ref[...] = acc

    return pl.pallas_call(
        body, name=name,
        in_specs=[pl.BlockSpec((n, r, cols), lambda: (0, 0, 0))],
        out_specs=pl.BlockSpec((r, cols), lambda: (0, 0)),
        out_shape=jax.ShapeDtypeStruct((r, cols), F32),
        compiler_params=_params(),
    )(a)


def _adamw(w, g, m, v, name):
    r, cols = w.shape
    tr = _row_tile(r, 256) if r % SUBLANES_BF16 == 0 else r

    def body(w_ref, g_ref, m_ref, v_ref, d_ref, nm_ref, nv_ref):
        g = g_ref[...]
        m = ADAM_B1 * m_ref[...] + (1.0 - ADAM_B1) * g
        v = ADAM_B2 * v_ref[...] + (1.0 - ADAM_B2) * (g * g)
        m_hat = m / (1.0 - ADAM_B1 ** ADAM_STEP)
        v_hat = v / (1.0 - ADAM_B2 ** ADAM_STEP)
        d_ref[...] = -ADAM_LR * (m_hat / (jnp.sqrt(v_hat) + ADAM_EPS) + ADAM_WD * w_ref[...])
        nm_ref[...] = m
        nv_ref[...] = v

    blk = pl.BlockSpec((tr, cols), lambda i: (i, 0))
    return pl.pallas_call(
        body, name=name, grid=(r // tr,),
        in_specs=[blk] * 4, out_specs=[blk] * 3,
        out_shape=[jax.ShapeDtypeStruct((r, cols), F32)] * 3,
        compiler_params=_params("arbitrary"),
    )(w, g, m, v)


def _place():
    x, y, c = lax.axis_index("x"), lax.axis_index("y"), lax.axis_index("c")
    other_chips = [(1 - x, y), (x, 1 - y), (1 - x, 1 - y)]
    return x, y, c, other_chips


def _chip(xy):
    return 2 * xy[0] + xy[1]


def _remote(src, dst, send_sem, recv_sem, to):
    return pltpu.make_async_remote_copy(src_ref=src, dst_ref=dst, send_sem=send_sem, recv_sem=recv_sem,
                                        device_id=to, device_id_type=MESH)


def _gather_comm(bufs, small, forward_at):
    n = len(bufs)

    def ici(ctx, k, j, start):
        (x, y, c, chips), b, send, recv = ctx
        blk = b[k].at[_chip((x, y)) if start else _chip(chips[j]), c]
        return _remote(blk, blk, send.at[6 * k + j], recv.at[6 * k + j], (*chips[j], c))

    def d2d(ctx, k, j, start):
        (x, y, c, chips), b, send, recv = ctx
        blk = b[k].at[_chip(chips[j]), c if start else 1 - c]
        return _remote(blk, blk, send.at[6 * k + 3 + j], recv.at[6 * k + 3 + j], (x, y, 1 - c))

    def small_copy(ctx, j, start):
        (x, y, c, chips), b, send, recv = ctx
        blk = b[n].at[_chip((x, y)) if start else _chip(chips[j])]
        return _remote(blk, blk, send.at[6 * n + j], recv.at[6 * n + j], (*chips[j], c))

    def start(srcs, b, send, recv):
        ctx = (_place(), b, send, recv)
        for k in range(n):
            for j in range(3):
                ici(ctx, k, j, True).start()
        if small is not None:
            for j in range(3):
                small_copy(ctx, j, True).start()

    def forward(srcs, b, send, recv):
        ctx = (_place(), b, send, recv)
        for k in range(n):
            for j in range(3):
                ici(ctx, k, j, False).wait_recv()
                d2d(ctx, k, j, True).start()

    def finish(srcs, b, send, recv):
        ctx = (_place(), b, send, recv)
        if small is not None:
            for j in range(3):
                small_copy(ctx, j, False).wait_recv()
                small_copy(ctx, j, True).wait_send()
        for k in range(n):
            for j in range(3):
                d2d(ctx, k, j, False).wait_recv()
                ici(ctx, k, j, True).wait_send()
                d2d(ctx, k, j, True).wait_send()

    all_bufs = list(bufs) + ([small] if small is not None else [])
    return _Comm([], all_bufs, 6 * n + 3, [(0, start), (forward_at, forward)], finish)


def _exchange_core_halves(grads, name):
    n = len(grads)

    def body(*refs):
        ins, outs = refs[:n], refs[n:2 * n]
        send_sems, recv_sems = refs[2 * n:]
        x, y, c, _ = _place()
        copies = [_remote(ins[k].at[s, 1 - c], outs[k].at[s], send_sems.at[N_SHARDS * k + s],
                          recv_sems.at[N_SHARDS * k + s], (x, y, 1 - c))
                  for k in range(n) for s in range(N_SHARDS)]
        for cp in copies:
            cp.start()
        for cp in copies:
            cp.wait()

    return pl.pallas_call(
        body, name=name, in_specs=[ANY] * n, out_specs=[ANY] * n,
        out_shape=[jax.ShapeDtypeStruct((N_SHARDS,) + a.shape[2:], a.dtype) for a in grads],
        scratch_shapes=[pltpu.SemaphoreType.DMA((N_SHARDS * n,))] * 2,
    )(*grads)


def _scatter_comm(chip_sums, landing):
    n = len(chip_sums)

    def big(srcs, b, send, recv, k, j, start):
        x, y, c, chips = _place()
        dst = b[k].at[_chip((x, y)) if start else _chip(chips[j])]
        return _remote(srcs[k].at[_chip(chips[j])], dst, send.at[3 * k + j], recv.at[3 * k + j], (*chips[j], c))

    def start(srcs, b, send, recv):
        for k in range(n):
            for j in range(3):
                big(srcs, b, send, recv, k, j, True).start()

    def finish(srcs, b, send, recv):
        for k in range(n):
            for j in range(3):
                big(srcs, b, send, recv, k, j, False).wait_recv()
                big(srcs, b, send, recv, k, j, True).wait_send()

    return _Comm(chip_sums, landing, 3 * n, [(0, start)], finish)


def _join_comm(halves, small):
    n = len(halves)
    flips = [(fx, fy, fc) for fx in (0, 1) for fy in (0, 1) for fc in (0, 1)][1:]

    def half(b, send, recv, k, start):
        x, y, c, _ = _place()
        return _remote(b[k].at[c], b[k].at[c if start else 1 - c], send.at[k], recv.at[k], (x, y, 1 - c))

    def small_copy(b, send, recv, q, start):
        x, y, c, _ = _place()
        px, py, pc = x ^ flips[q][0], y ^ flips[q][1], c ^ flips[q][2]
        blk = b[n].at[4 * x + 2 * y + c if start else 4 * px + 2 * py + pc]
        return _remote(blk, blk, send.at[n + q], recv.at[n + q], (px, py, pc))

    def start(srcs, b, send, recv):
        for q in range(len(flips)):
            small_copy(b, send, recv, q, True).start()
        for k in range(n):
            half(b, send, recv, k, True).start()

    def finish(srcs, b, send, recv):
        for q in range(len(flips)):
            small_copy(b, send, recv, q, False).wait()
        for k in range(n):
            half(b, send, recv, k, False).wait()

    return _Comm([], list(halves) + [small], n + len(flips), [(0, start)], finish)


def _flat_rows(parts):
    flat = jnp.concatenate([p.reshape(-1) for p in parts])
    assert flat.shape[0] % LANES == 0
    return flat.reshape(-1, LANES)


def _unflatten(flat, shapes):
    out, at = [], 0
    for sh in shapes:
        size = 1
        for dim in sh:
            size *= dim
        out.append(flat[at:at + size].reshape(sh))
        at += size
    assert at == flat.shape[0], (at, flat.shape)
    return out


def _col_shards_to_full(a, rows):
    q = a.shape[1] // rows
    return a.reshape(N_SHARDS, rows, q).transpose(1, 0, 2).reshape(rows, N_SHARDS * q)


def _my_col_shard(full, chip):
    rows, cols = full.shape
    q = cols // N_SHARDS
    return lax.dynamic_index_in_dim(full.reshape(rows, N_SHARDS, q), chip, axis=1, keepdims=False)


def kernel(x, e_norm_pre, e_norm_post, e_w_in, e_a_conv, e_b_conv, e_b_conv_bias, e_b_ln_g, e_b_ln_b, e_w_out, o_norm_pre, o_norm_post, o_w_in, o_c_w, o_c_b, o_c_scale, o_w_out, loss_target, m_e_norm_pre, m_e_norm_post, m_e_w_in, m_e_a_conv, m_e_b_conv, m_e_b_conv_bias, m_e_b_ln_g, m_e_b_ln_b, m_e_w_out, m_o_norm_pre, m_o_norm_post, m_o_w_in, m_o_c_w, m_o_c_b, m_o_c_scale, m_o_w_out, v_e_norm_pre, v_e_norm_post, v_e_w_in, v_e_a_conv, v_e_b_conv, v_e_b_conv_bias, v_e_b_ln_g, v_e_b_ln_b, v_e_w_out, v_o_norm_pre, v_o_norm_post, v_o_w_in, v_o_c_w, v_o_c_b, v_o_c_scale, v_o_w_out):
    _, s, d = x.shape
    w = d // 2
    c = d
    gc = c // N_GROUPS
    wq, cq, gq = w // N_SHARDS, c // N_SHARDS, gc // N_SHARDS
    chip = 2 * lax.axis_index("x") + lax.axis_index("y")
    core = lax.axis_index("c")
    x2 = x.reshape(s, d)
    target = loss_target.reshape(s, d)

    big_w = [e_w_in[0], e_w_out[0], o_w_in[0], o_c_w[0].reshape(N_GROUPS * gq, gc), o_w_out[0]]
    big_m = [m_e_w_in[0], m_e_w_out[0], m_o_w_in[0], m_o_c_w[0].reshape(N_GROUPS * gq, gc), m_o_w_out[0]]
    big_v = [v_e_w_in[0], v_e_w_out[0], v_o_w_in[0], v_o_c_w[0].reshape(N_GROUPS * gq, gc), v_o_w_out[0]]
    coords = jnp.stack([chip, core]).astype(jnp.int32)
    slots = [_cast_into_slot(a, coords, "cast_%d" % k) for k, a in enumerate(big_w)]
    sharded_small = _flat_rows([e_a_conv[0], e_b_conv[0], o_norm_pre, o_norm_post, o_c_scale, o_c_b[0]])
    small_slots = lax.dynamic_update_index_in_dim(jnp.zeros((N_SHARDS,) + sharded_small.shape, F32), sharded_small,
                                                  chip, 0)
    e_w_in_g, e_w_out_g, small_g4 = _comm_only(_gather_comm(slots[:2], small_slots, 0), "gather_even_weights")
    e_w_in_sm = e_w_in_g.reshape((N_SHARDS,) + big_w[0].shape)
    e_w_out_f = e_w_out_g.reshape(w + w, d)
    sm = small_g4.reshape(N_SHARDS, -1)
    at = [0]

    def take(rows, q):
        blk = sm[:, at[0]:at[0] + rows * q]
        at[0] += rows * q
        return _col_shards_to_full(blk, rows)

    a_conv_f = take(CONV_A, wq)
    b_conv_f = take(CONV_B, wq)
    o_pre_f = take(1, cq)
    o_post_f = take(1, cq)
    cs_f = take(1, cq)
    cb_f = take(N_GROUPS, gq).reshape(1, c)

    in_proj_steps = (s // _row_tile(s, 512)) * N_SHARDS
    (p0, h0), odd_g = _norm_matmul(x2, e_norm_pre, e_w_in_sm, "e_in_proj",
                                   comm=_gather_comm(slots[2:], None, (2 * in_proj_steps) // 3))
    o_w_in_sm = odd_g[0].reshape((N_SHARDS,) + big_w[2].shape)
    cw_f = odd_g[1].reshape(N_SHARDS, N_GROUPS, gq, gc).transpose(1, 0, 2, 3).reshape(N_GROUPS, gc, gc)
    o_w_out_f = odd_g[2].reshape(c, d)
    u0, x1, y0 = _even_mixer_fwd(p0, e_w_out_f, x2, e_norm_post, a_conv_f, b_conv_f, e_b_conv_bias, e_b_ln_g,
                                 e_b_ln_b, "e_mixer_out_proj")
    (p1, h1), _ = _norm_matmul(x1, o_pre_f, o_w_in_sm, "o_in_proj")
    u1 = _odd_mixer_fwd(p1, cw_f, cb_f, cs_f, "o_mixer_fwd")
    d_y1, d_x2, d_o_post, loss_part = _matmul_post_loss(u1, o_w_out_f, x1, o_post_f, target, "o_out_proj_loss")

    def as_pieces(g, k):
        return g.reshape(N_SHARDS, 2, big_w[k].shape[0] // 2, big_w[k].shape[1])

    def chip_sums(ks, grads, name):
        pieces = [as_pieces(g, k) for k, g in zip(ks, grads)]
        from_sibling = _exchange_core_halves(pieces, name)
        both = [_chip_sum(g, o, coords, "chip_sum_%d" % k) for k, g, o in zip(ks, pieces, from_sibling)]
        return [b[0] for b in both], [b[1] for b in both]

    g_o_w_out = _matmul_tn(u1, d_y1, 1, "o_w_out_grad")
    d_u1, _ = _matmul_nt(d_y1, o_w_out_f[None], "o_out_proj_bwd")
    d_p1, d_cw, d_cb, d_cs = _odd_mixer_bwd(p1, d_u1, cw_f, cb_f, cs_f, "o_mixer_bwd")
    g_o_w_in = _matmul_tn(h1, d_p1, N_SHARDS, "o_w_in_grad")
    d_h1, _ = _matmul_nt(d_p1, o_w_in_sm, "o_in_proj_bwd")
    d_x1, d_o_pre, d_y0, d_e_post = _norm_bwd(d_h1, x1, o_pre_f, d_x2, "o_pre_norm_bwd", post=(y0, e_norm_post))

    g_e_w_out = _matmul_tn(u0, d_y0, 1, "e_w_out_grad")
    g_cw = d_cw.reshape(N_GROUPS, N_SHARDS, gq, gc).transpose(1, 0, 2, 3).astype(BF16)
    sums_a, landing_a = chip_sums([1, 2, 3, 4], [g_e_w_out, g_o_w_in, g_cw, g_o_w_out], "exchange_core_halves_a")
    (d_p0, d_a_conv, d_b_conv, d_bias, d_ln_g, d_ln_b), landed_a = _even_mixer_bwd(
        p0, d_y0, e_w_out_f, a_conv_f, b_conv_f, e_b_conv_bias, e_b_ln_g, e_b_ln_b, "e_mixer_bwd",
        comm=_scatter_comm(sums_a, landing_a))
    g_e_w_in = _matmul_tn(h0, d_p0, N_SHARDS, "e_w_in_grad")
    sums_b, landing_b = chip_sums([0], [g_e_w_in], "exchange_core_halves_b")
    d_h0, landed_b = _matmul_nt(d_p0, e_w_in_sm, "e_in_proj_bwd", comm=_scatter_comm(sums_b, landing_b))
    grad_x, d_e_pre = _norm_bwd(d_h0, x2, e_norm_pre, d_x1, "e_pre_norm_bwd")

    landed = landed_b + landed_a
    reduced = [_shard_sum(sc, coords, "shard_sum_%d" % k) for k, sc in enumerate(landed)]
    small_parts = _flat_rows([loss_part[0], d_e_pre, d_e_post, d_bias, d_ln_g, d_ln_b, d_a_conv, d_b_conv,
                              d_o_pre, d_o_post, d_cs, d_cb])
    small_rows = lax.dynamic_update_index_in_dim(jnp.zeros((N_DEVICES,) + small_parts.shape, F32), small_parts,
                                                 2 * chip + core, 0)
    joined = _comm_only(_join_comm(reduced, small_rows), "join_core_halves")
    big_g = [j.reshape(a.shape) for j, a in zip(joined[:5], big_w)]
    small_sum = _sum_small(joined[5], "small_sum").reshape(-1)
    (loss_row, g_e_pre, g_e_post, g_bias, g_ln_g, g_ln_b, g_a_conv_f, g_b_conv_f, g_o_pre_f, g_o_post_f, g_cs_f,
     g_cb_f) = _unflatten(small_sum, [(LANES,), (1, d), (1, d), (1, w), (1, w), (1, w), (CONV_A, w), (CONV_B, w),
                                      (1, c), (1, c), (1, c), (1, c)])
    loss = loss_row[0]
    g_a_conv = _my_col_shard(g_a_conv_f, chip)
    g_b_conv = _my_col_shard(g_b_conv_f, chip)
    g_o_pre = _my_col_shard(g_o_pre_f, chip)
    g_o_post = _my_col_shard(g_o_post_f, chip)
    g_cs = _my_col_shard(g_cs_f, chip)
    g_cb = _my_col_shard(g_cb_f.reshape(N_GROUPS, gc), chip)

    big_upd = [_adamw(wt, g, m, v, "adamw_%d" % k) for k, (wt, g, m, v) in enumerate(zip(big_w, big_g, big_m, big_v))]
    small_w = [e_norm_pre, e_norm_post, e_b_conv_bias, e_b_ln_g, e_b_ln_b, e_a_conv[0], e_b_conv[0],
               o_norm_pre, o_norm_post, o_c_b[0], o_c_scale]
    small_m = [m_e_norm_pre, m_e_norm_post, m_e_b_conv_bias, m_e_b_ln_g, m_e_b_ln_b, m_e_a_conv[0], m_e_b_conv[0],
               m_o_norm_pre, m_o_norm_post, m_o_c_b[0], m_o_c_scale]
    small_v = [v_e_norm_pre, v_e_norm_post, v_e_b_conv_bias, v_e_b_ln_g, v_e_b_ln_b, v_e_a_conv[0], v_e_b_conv[0],
               v_o_norm_pre, v_o_norm_post, v_o_c_b[0], v_o_c_scale]
    small_g = [g_e_pre, g_e_post, g_bias, g_ln_g, g_ln_b, g_a_conv, g_b_conv, g_o_pre, g_o_post, g_cb, g_cs]
    small_shapes = [a.shape for a in small_w]
    small_upd = _adamw(_flat_rows(small_w), _flat_rows(small_g), _flat_rows(small_m), _flat_rows(small_v),
                       "adamw_small")
    small_delta, small_new_m, small_new_v = [_unflatten(u.reshape(-1), small_shapes) for u in small_upd]

    def ordered(small, big, lead):
        (n_pre, n_post, bias, ln_g, ln_b, a_conv, b_conv, o_pre, o_post, cb, cs) = small
        (w_in, w_out, ow_in, cw, ow_out) = big
        out = [n_pre, n_post, w_in[None], a_conv[None], b_conv[None], bias, ln_g, ln_b, w_out[None], o_pre, o_post,
               ow_in[None], cw.reshape(1, N_GROUPS, gq, gc), cb[None], cs, ow_out[None]]
        return out

    grads = ordered(small_g, big_g, None)
    deltas = ordered(small_delta, [u[0] for u in big_upd], None)
    new_m = ordered(small_new_m, [u[1] for u in big_upd], None)
    new_v = ordered(small_new_v, [u[2] for u in big_upd], None)
    return (loss, grad_x.reshape(1, s, d), *grads, *deltas, *new_m, *new_v)
```

```python
import functools

import jax
import jax.numpy as jnp
from jax import lax
from jax.experimental import pallas as pl
from jax.experimental.pallas import tpu as pltpu

F32 = jnp.float32
BF16 = jnp.bfloat16
MESH = pl.DeviceIdType.MESH

EPS = 1e-6
CONV_A = 3
CONV_B = 31
POOL_WINDOWS = (2, 4, 8, 16)
N_GROUPS = len(POOL_WINDOWS)
N_SHARDS = 4
N_DEVICES = 8
ADAM_LR = 0.001
ADAM_B1 = 0.9
ADAM_B2 = 0.999
ADAM_EPS = 1e-08
ADAM_WD = 0.01
ADAM_STEP = 10

LANES = 128
SUBLANES_BF16 = 16
HALO_A = 8
HALO_B = 32
HALO_P = 16
SHIFTS = 8
ROW_BLOCK = 32
LANE_BLOCK = 256
VMEM_LIMIT = 56 * 1024 * 1024


def _row_tile(n, pref):
    t = min(n, pref)
    while t > SUBLANES_BF16 and (n % t or t % SUBLANES_BF16):
        t -= SUBLANES_BF16
    assert n % t == 0, (n, pref)
    return t


def _col_chunk(n, pref):
    t = (min(n, pref) // LANES) * LANES
    while t > LANES and n % t:
        t -= LANES
    assert t >= LANES and n % t == 0, (n, pref)
    return t


def _params(*sem):
    return pltpu.CompilerParams(dimension_semantics=tuple(sem) if sem else None, vmem_limit_bytes=VMEM_LIMIT)


ANY = pl.BlockSpec(memory_space=pl.ANY)


class _Comm:
    def __init__(self, srcs, bufs, n_sems, phases, finish):
        self.srcs, self.bufs, self.n_sems, self.phases, self.finish = list(srcs), list(bufs), n_sems, phases, finish


def _call(body, *, name, grid, in_specs, out_specs, out_shape, args, scratch_shapes=(), comm=None):
    params = _params(*(("arbitrary",) * len(grid)))
    if comm is None:
        out = pl.pallas_call(body, name=name, grid=grid, in_specs=in_specs, out_specs=out_specs, out_shape=out_shape,
                             scratch_shapes=scratch_shapes, compiler_params=params)(*args)
        return list(out), []
    n_in, n_out, n_scr = len(in_specs), len(out_specs), len(scratch_shapes)
    ns, nb = len(comm.srcs), len(comm.bufs)
    total = 1
    for size in grid:
        total *= size

    def fused(*refs):
        ins, srcs = refs[:n_in], refs[n_in:n_in + ns]
        at = n_in + ns + nb
        outs, bufs = refs[at:at + n_out], refs[at + n_out:at + n_out + nb]
        scratch = refs[at + n_out + nb:at + n_out + nb + n_scr]
        send_sems, recv_sems = refs[-2:]
        step = 0
        for axis, size in enumerate(grid):
            step = step * size + pl.program_id(axis)
        for when, fn in comm.phases:
            pl.when(step == when)(functools.partial(fn, srcs, bufs, send_sems, recv_sems))
        body(*ins, *outs, *scratch)
        pl.when(step == total - 1)(functools.partial(comm.finish, srcs, bufs, send_sems, recv_sems))

    out = pl.pallas_call(
        fused, name=name, grid=grid,
        in_specs=list(in_specs) + [ANY] * (ns + nb), out_specs=list(out_specs) + [ANY] * nb,
        out_shape=list(out_shape) + [jax.ShapeDtypeStruct(b.shape, b.dtype) for b in comm.bufs],
        input_output_aliases={n_in + ns + i: n_out + i for i in range(nb)},
        scratch_shapes=list(scratch_shapes) + [pltpu.SemaphoreType.DMA((comm.n_sems,))] * 2,
        compiler_params=params,
    )(*args, *comm.srcs, *comm.bufs)
    return list(out[:n_out]), list(out[n_out:])


def _comm_only(comm, name):
    ns, nb = len(comm.srcs), len(comm.bufs)

    def body(*refs):
        srcs, bufs = refs[:ns], refs[ns + nb:ns + 2 * nb]
        send_sems, recv_sems = refs[-2:]
        for _, fn in comm.phases:
            fn(srcs, bufs, send_sems, recv_sems)
        comm.finish(srcs, bufs, send_sems, recv_sems)

    return pl.pallas_call(
        body, name=name, in_specs=[ANY] * (ns + nb), out_specs=[ANY] * nb,
        out_shape=[jax.ShapeDtypeStruct(b.shape, b.dtype) for b in comm.bufs],
        input_output_aliases={ns + i: i for i in range(nb)},
        scratch_shapes=[pltpu.SemaphoreType.DMA((comm.n_sems,))] * 2,
    )(*comm.srcs, *comm.bufs)


def _sigmoid(v):
    return jax.nn.sigmoid(v)


def _dsilu(v, s):
    return s * (1.0 + v * (1.0 - s))


def _mean_last(v):
    return jnp.mean(v, axis=-1, keepdims=True)


def _sum_rows(v):
    return jnp.sum(v, axis=0, keepdims=True)


def _norm_matmul(x, gain, w_sm, name, comm=None):
    s, d = x.shape
    n_sh, _, ns = w_sm.shape
    tm = _row_tile(s, 512)

    def body(x_ref, g_ref, w_ref, p_ref, h_ref):
        @pl.when(pl.program_id(1) == 0)
        def _():
            xv = x_ref[...]
            r = lax.rsqrt(_mean_last(xv * xv) + EPS)
            h_ref[...] = (xv * r * g_ref[...]).astype(BF16)

        p_ref[...] = jnp.dot(h_ref[...], w_ref[0], preferred_element_type=F32)

    return _call(
        body, name=name, grid=(s // tm, n_sh),
        in_specs=[pl.BlockSpec((tm, d), lambda i, j: (i, 0)),
                  pl.BlockSpec((1, d), lambda i, j: (0, 0)),
                  pl.BlockSpec((1, d, ns), lambda i, j: (j, 0, 0))],
        out_specs=[pl.BlockSpec((tm, ns), lambda i, j: (i, j)),
                   pl.BlockSpec((tm, d), lambda i, j: (i, 0))],
        out_shape=[jax.ShapeDtypeStruct((s, n_sh * ns), F32), jax.ShapeDtypeStruct((s, d), BF16)],
        args=(x, gain, w_sm), comm=comm)


def _matmul_post_loss(u, w, x_res, gain, target, name):
    s, k = u.shape
    d = w.shape[1]
    tm = _row_tile(s, 256)

    def body(u_ref, w_ref, x_ref, g_ref, t_ref, dy_ref, dout_ref, dg_ref, loss_ref):
        @pl.when(pl.program_id(0) == 0)
        def _():
            dg_ref[...] = jnp.zeros_like(dg_ref)
            loss_ref[...] = jnp.zeros_like(loss_ref)

        y = jnp.dot(u_ref[...], w_ref[...], preferred_element_type=F32)
        r = lax.rsqrt(_mean_last(y * y) + EPS)
        n = y * r
        g = g_ref[...]
        err = x_ref[...] + n * g - t_ref[...]
        loss_ref[...] += 0.5 * jnp.sum(_mean_last(err * err))
        dout = err * (1.0 / d)
        dout_ref[...] = dout
        dg_ref[...] += _sum_rows(dout * n)
        dn = dout * g
        dy_ref[...] = (r * (dn - n * _mean_last(dn * n))).astype(BF16)

    return pl.pallas_call(
        body, name=name, grid=(s // tm,),
        in_specs=[pl.BlockSpec((tm, k), lambda i: (i, 0)),
                  pl.BlockSpec((k, d), lambda i: (0, 0)),
                  pl.BlockSpec((tm, d), lambda i: (i, 0)),
                  pl.BlockSpec((1, d), lambda i: (0, 0)),
                  pl.BlockSpec((tm, d), lambda i: (i, 0))],
        out_specs=[pl.BlockSpec((tm, d), lambda i: (i, 0)),
                   pl.BlockSpec((tm, d), lambda i: (i, 0)),
                   pl.BlockSpec((1, d), lambda i: (0, 0)),
                   pl.BlockSpec((8, LANES), lambda i: (0, 0))],
        out_shape=[jax.ShapeDtypeStruct((s, d), BF16), jax.ShapeDtypeStruct((s, d), F32),
                   jax.ShapeDtypeStruct((1, d), F32), jax.ShapeDtypeStruct((8, LANES), F32)],
        compiler_params=_params("arbitrary"),
    )(u, w, x_res, gain, target)


def _matmul_nt(a, w_sm, name, comm=None):
    s, ncols = a.shape
    n_sh, r, ns = w_sm.shape
    assert ncols == n_sh * ns
    tm = _row_tile(s, 512)
    nc = _col_chunk(ns, 1792)
    per = ns // nc
    steps = n_sh * per

    def body(a_ref, w_ref, o_ref):
        part = lax.dot_general(a_ref[...], w_ref[0], (((1,), (1,)), ((), ())), preferred_element_type=F32)

        @pl.when(pl.program_id(1) == 0)
        def _():
            o_ref[...] = part

        @pl.when(pl.program_id(1) > 0)
        def _():
            o_ref[...] += part

    out, bufs = _call(
        body, name=name, grid=(s // tm, steps),
        in_specs=[pl.BlockSpec((tm, nc), lambda i, j: (i, j)),
                  pl.BlockSpec((1, r, nc), lambda i, j: (j // per, 0, j % per))],
        out_specs=[pl.BlockSpec((tm, r), lambda i, j: (i, 0))],
        out_shape=[jax.ShapeDtypeStruct((s, r), F32)],
        args=(a, w_sm), comm=comm)
    return out[0], bufs


def _matmul_tn(a, b, n_sh, name):
    s, k = a.shape
    n = b.shape[1]
    ns = n // n_sh
    tk = _row_tile(k, 1024)
    ts = _row_tile(s, 1024)
    n_s = s // ts

    def body(a_ref, b_ref, o_ref, acc_ref):
        part = lax.dot_general(a_ref[...], b_ref[...], (((0,), (0,)), ((), ())), preferred_element_type=F32)

        @pl.when(pl.program_id(2) == 0)
        def _():
            acc_ref[...] = part

        @pl.when(pl.program_id(2) > 0)
        def _():
            acc_ref[...] += part

        @pl.when(pl.program_id(2) == n_s - 1)
        def _():
            o_ref[0] = acc_ref[...].astype(BF16)

    return pl.pallas_call(
        body, name=name, grid=(n_sh, k // tk, n_s),
        in_specs=[pl.BlockSpec((ts, tk), lambda j, i, t: (t, i)),
                  pl.BlockSpec((ts, ns), lambda j, i, t: (t, j))],
        out_specs=pl.BlockSpec((1, tk, ns), lambda j, i, t: (j, i, 0)),
        out_shape=jax.ShapeDtypeStruct((n_sh, k, ns), BF16),
        scratch_shapes=[pltpu.VMEM((tk, ns), F32)],
        compiler_params=_params("arbitrary", "arbitrary", "arbitrary"),
    )(a, b)


def _norm_bwd(dh, x, gain, dres, name, post=None):
    s, d = x.shape
    tm = _row_tile(s, 256)
    with_post = post is not None

    def rms_bwd(dout, v, g):
        r = lax.rsqrt(_mean_last(v * v) + EPS)
        n = v * r
        dn = dout * g
        return r * (dn - n * _mean_last(dn * n)), _sum_rows(dout * n)

    def body(*refs):
        if with_post:
            dh_ref, x_ref, g_ref, dres_ref, y_ref, gp_ref, dx_ref, dg_ref, dy_ref, dgp_ref = refs
        else:
            dh_ref, x_ref, g_ref, dres_ref, dx_ref, dg_ref = refs

        @pl.when(pl.program_id(0) == 0)
        def _():
            dg_ref[...] = jnp.zeros_like(dg_ref)
            if with_post:
                dgp_ref[...] = jnp.zeros_like(dgp_ref)

        dv, dg = rms_bwd(dh_ref[...], x_ref[...], g_ref[...])
        dx = dres_ref[...] + dv
        dx_ref[...] = dx
        dg_ref[...] += dg
        if with_post:
            dy, dgp = rms_bwd(dx, y_ref[...], gp_ref[...])
            dy_ref[...] = dy.astype(BF16)
            dgp_ref[...] += dgp

    row = pl.BlockSpec((tm, d), lambda i: (i, 0))
    vec = pl.BlockSpec((1, d), lambda i: (0, 0))
    in_specs = [row, row, vec, row]
    out_specs = [row, vec]
    out_shape = [jax.ShapeDtypeStruct((s, d), F32), jax.ShapeDtypeStruct((1, d), F32)]
    args = [dh, x, gain, dres]
    if with_post:
        in_specs += [row, vec]
        out_specs += [row, vec]
        out_shape += [jax.ShapeDtypeStruct((s, d), BF16), jax.ShapeDtypeStruct((1, d), F32)]
        args += list(post)
    return pl.pallas_call(
        body, name=name, grid=(s // tm,), in_specs=in_specs, out_specs=out_specs, out_shape=out_shape,
        compiler_params=_params("arbitrary"),
    )(*args)


def _fill_shifted_down(sh, rows):
    for b in range(1, SHIFTS):
        sh[b, SHIFTS:rows, :] = sh[0, SHIFTS - b:rows - b, :]


def _fill_shifted_up(sh, rows):
    for b in range(1, SHIFTS):
        sh[b, 0:rows - SHIFTS, :] = sh[0, b:rows - SHIFTS + b, :]


def _for_blocks(ts, w, fn):
    lb = min(LANE_BLOCK, w)
    for l0 in range(0, w, lb):
        def rows(rb, carry, l0=l0):
            fn(pl.multiple_of(rb * ROW_BLOCK, ROW_BLOCK), slice(l0, l0 + lb))
            return carry

        lax.fori_loop(0, ts // ROW_BLOCK, rows, 0)


TAP_SPAN = SHIFTS * ((CONV_B - 1) // SHIFTS)
WINDOW = ROW_BLOCK + TAP_SPAN


def _taps_of(b):
    return [(a, SHIFTS * a + b) for a in range((CONV_B - 1 - b) // SHIFTS + 1)]


def _conv31(sh, base, step, wt_ref, bias_ref, out_ref, ts, w):
    low = min(0, step * (TAP_SPAN // SHIFTS))

    def block(r0, lanes):
        acc = jnp.zeros((ROW_BLOCK, lanes.stop - lanes.start), F32)
        if bias_ref is not None:
            acc = acc + bias_ref[:, lanes]
        for b in range(SHIFTS):
            window = sh[b, pl.ds(pl.multiple_of(r0 + (base + low), SHIFTS), WINDOW), lanes]
            for a, j in _taps_of(b):
                at = step * a - low
                acc = acc + wt_ref[CONV_B - 1 - j:CONV_B - j, lanes] * window[at:at + ROW_BLOCK, :]
        out_ref[pl.ds(r0, ROW_BLOCK), lanes] = acc

    _for_blocks(ts, w, block)


def _conv31_weight_grad(d_sh, x_sh, wacc, ts, w):
    def block(r0, lanes):
        d = d_sh[0, pl.ds(r0, ROW_BLOCK), lanes]
        for b in range(SHIFTS):
            window = x_sh[b, pl.ds(pl.multiple_of(r0 + (HALO_B - TAP_SPAN), SHIFTS), WINDOW), lanes]
            for a, j in _taps_of(b):
                at = TAP_SPAN - SHIFTS * a
                prod = d * window[at:at + ROW_BLOCK, :]
                part = prod[0:SHIFTS, :]
                for q in range(1, ROW_BLOCK // SHIFTS):
                    part = part + prod[q * SHIFTS:(q + 1) * SHIFTS, :]
                wacc[CONV_B - 1 - j, :, lanes] += part

    _for_blocks(ts, w, block)


def _even_forward_tile(p_ref, halo_ref, first, a_conv_ref, b_conv_ref, bias_ref, lng_ref, lnb_ref, qbuf, ysh, y1buf,
                       w, ts):
    def col(ref, k, rows=slice(None)):
        return ref[rows, k * w:(k + 1) * w]

    a_x, a_b, a_c, a_z = col(p_ref, 0), col(p_ref, 1), col(p_ref, 2), col(p_ref, 3)
    b_val, b_gate, b_z = col(p_ref, 4), col(p_ref, 5), col(p_ref, 6)
    keep = jnp.where(first, 0.0, 1.0)

    rows_a = slice(HALO_B - HALO_A, HALO_B)
    qbuf[0:HALO_A, :] = col(halo_ref, 2, rows_a) * col(halo_ref, 0, rows_a) * keep
    qbuf[HALO_A:HALO_A + ts, :] = a_c * a_x
    cq = jnp.zeros((ts, w), F32)
    for j in range(CONV_A):
        cq = cq + a_conv_ref[CONV_A - 1 - j:CONV_A - j, :] * qbuf[HALO_A - j:HALO_A - j + ts, :]
    ya = a_b * cq

    ysh[0, 0:HALO_B, :] = col(halo_ref, 4) * _sigmoid(col(halo_ref, 5)) * keep
    ysh[0, HALO_B:HALO_B + ts, :] = b_val * _sigmoid(b_gate)
    _fill_shifted_down(ysh, HALO_B + ts)
    _conv31(ysh, HALO_B, -SHIFTS, b_conv_ref, bias_ref, y1buf, ts, w)
    yb1 = y1buf[...]
    xc = yb1 - _mean_last(yb1)
    rstd = lax.rsqrt(_mean_last(xc * xc) + EPS)
    xhat = xc * rstd
    yb2 = xhat * lng_ref[...] + lnb_ref[...]
    return dict(a_x=a_x, a_b=a_b, a_c=a_c, a_z=a_z, b_val=b_val, b_gate=b_gate, b_z=b_z,
                cq=cq, ya=ya, rstd=rstd, xhat=xhat, yb2=yb2)


def _even_specs(s, w, ts):
    tile = pl.BlockSpec((ts, 7 * w), lambda i: (i, 0))
    halo = pl.BlockSpec((HALO_B, 7 * w), lambda i: (jnp.maximum(i * (ts // HALO_B) - 1, 0), 0))
    return tile, halo


def _small_specs(shapes, index=lambda i: (0, 0)):
    return [pl.BlockSpec(sh, index) for sh in shapes]


def _even_mixer_fwd(p, w_out, x_res, gain, a_conv, b_conv, bias, ln_g, ln_b, name, comm=None):
    s, d = x_res.shape
    w = p.shape[1] // 7
    ts = _row_tile(s, 128)
    assert ts % HALO_B == 0

    def body(p_ref, halo_ref, wout_ref, x_ref, g_ref, ac_ref, bc_ref, bias_ref, lng_ref, lnb_ref,
             u_ref, xn_ref, y_ref, qbuf, ysh, y1buf):
        first = pl.program_id(0) == 0
        f = _even_forward_tile(p_ref, halo_ref, first, ac_ref, bc_ref, bias_ref, lng_ref, lnb_ref, qbuf, ysh, y1buf,
                               w, ts)
        yb3 = f["yb2"] * _sigmoid(f["yb2"])
        u_a = (f["ya"] * (f["a_z"] * _sigmoid(f["a_z"]))).astype(BF16)
        u_b = (yb3 * (f["b_z"] * _sigmoid(f["b_z"]))).astype(BF16)
        u_ref[:, 0:w] = u_a
        u_ref[:, w:2 * w] = u_b
        y = (jnp.dot(u_a, wout_ref[0:w, :], preferred_element_type=F32)
             + jnp.dot(u_b, wout_ref[w:2 * w, :], preferred_element_type=F32))
        r = lax.rsqrt(_mean_last(y * y) + EPS)
        y_ref[...] = y
        xn_ref[...] = x_ref[...] + (y * r) * g_ref[...]

    tile, halo = _even_specs(s, w, ts)
    row = pl.BlockSpec((ts, d), lambda i: (i, 0))
    return _call(
        body, name=name, grid=(s // ts,),
        in_specs=[tile, halo, pl.BlockSpec((2 * w, d), lambda i: (0, 0)), row, pl.BlockSpec((1, d), lambda i: (0, 0))]
        + _small_specs([(CONV_A, w), (CONV_B, w), (1, w), (1, w), (1, w)]),
        out_specs=[pl.BlockSpec((ts, 2 * w), lambda i: (i, 0)), row, row],
        out_shape=[jax.ShapeDtypeStruct((s, 2 * w), BF16), jax.ShapeDtypeStruct((s, d), F32),
                   jax.ShapeDtypeStruct((s, d), F32)],
        scratch_shapes=[pltpu.VMEM((HALO_A + ts, w), F32), pltpu.VMEM((SHIFTS, HALO_B + ts, w), F32),
                        pltpu.VMEM((ts, w), F32)],
        args=(p, p, w_out, x_res, gain, a_conv, b_conv, bias, ln_g, ln_b), comm=comm)


def _even_mixer_bwd(p, dy, w_out, a_conv, b_conv, bias, ln_g, ln_b, name, comm=None):
    s, d = dy.shape
    w = p.shape[1] // 7
    ts = _row_tile(s, 128)
    nt = s // ts
    assert ts % HALO_B == 0

    def body(p_ref, halo_ref, dy_ref, wout_ref, ac_ref, bc_ref, bias_ref, lng_ref, lnb_ref,
             dp_ref, dac_ref, dbc_ref, dbias_ref, dlng_ref, dlnb_ref,
             qbuf, ysh, y1buf, dqbuf, dsh, dy0buf, wacc, carry_dq, carry_dy):
        step = pl.program_id(0)
        first = step == nt - 1

        @pl.when(step == 0)
        def _():
            for ref in (dac_ref, dbias_ref, dlng_ref, dlnb_ref, wacc, carry_dq, carry_dy):
                ref[...] = jnp.zeros_like(ref)

        f = _even_forward_tile(p_ref, halo_ref, first, ac_ref, bc_ref, bias_ref, lng_ref, lnb_ref, qbuf, ysh, y1buf,
                               w, ts)
        a_z, b_z, yb2 = f["a_z"], f["b_z"], f["yb2"]
        s_az, s_bz, s_y2 = _sigmoid(a_z), _sigmoid(b_z), _sigmoid(yb2)
        nt_dims = (((1,), (1,)), ((), ()))
        du_a = lax.dot_general(dy_ref[...], wout_ref[0:w, :], nt_dims, preferred_element_type=F32)
        du_b = lax.dot_general(dy_ref[...], wout_ref[w:2 * w, :], nt_dims, preferred_element_type=F32)

        d_ya = du_a * (a_z * s_az)
        dp_ref[:, 3 * w:4 * w] = (du_a * f["ya"] * _dsilu(a_z, s_az)).astype(BF16)
        dp_ref[:, 1 * w:2 * w] = (d_ya * f["cq"]).astype(BF16)
        d_cq = d_ya * f["a_b"]
        dqbuf[0:ts, :] = d_cq
        dqbuf[ts:ts + HALO_A, :] = carry_dq[...]
        d_q = jnp.zeros((ts, w), F32)
        for o in range(CONV_A):
            d_q = d_q + ac_ref[CONV_A - 1 - o:CONV_A - o, :] * dqbuf[o:o + ts, :]
        for j in range(CONV_A):
            k = CONV_A - 1 - j
            dac_ref[k:k + 1, :] += _sum_rows(d_cq * qbuf[HALO_A - j:HALO_A - j + ts, :])
        carry_dq[...] = d_cq[0:HALO_A, :]
        dp_ref[:, 2 * w:3 * w] = (d_q * f["a_x"]).astype(BF16)
        dp_ref[:, 0 * w:1 * w] = (d_q * f["a_c"]).astype(BF16)

        d_yb3 = du_b * (b_z * s_bz)
        dp_ref[:, 6 * w:7 * w] = (du_b * (yb2 * s_y2) * _dsilu(b_z, s_bz)).astype(BF16)
        d_yb2 = d_yb3 * _dsilu(yb2, s_y2)
        xhat = f["xhat"]
        dlng_ref[...] += _sum_rows(d_yb2 * xhat)
        dlnb_ref[...] += _sum_rows(d_yb2)
        d_xh = d_yb2 * lng_ref[...]
        d_yb1 = f["rstd"] * (d_xh - _mean_last(d_xh) - xhat * _mean_last(d_xh * xhat))
        dbias_ref[...] += _sum_rows(d_yb1)
        dsh[0, 0:ts, :] = d_yb1
        dsh[0, ts:ts + HALO_B, :] = carry_dy[...]
        carry_dy[...] = d_yb1[0:HALO_B, :]
        _fill_shifted_up(dsh, ts + HALO_B)
        _conv31(dsh, 0, SHIFTS, bc_ref, None, dy0buf, ts, w)
        _conv31_weight_grad(dsh, ysh, wacc, ts, w)

        @pl.when(step == nt - 1)
        def _():
            for k in range(CONV_B):
                dbc_ref[k:k + 1, :] = _sum_rows(wacc[k])

        d_yb0 = dy0buf[...]
        s_g = _sigmoid(f["b_gate"])
        dp_ref[:, 4 * w:5 * w] = (d_yb0 * s_g).astype(BF16)
        dp_ref[:, 5 * w:6 * w] = (d_yb0 * f["b_val"] * s_g * (1.0 - s_g)).astype(BF16)

    rev = lambda i: (nt - 1 - i, 0)
    tile = pl.BlockSpec((ts, 7 * w), rev)
    halo = pl.BlockSpec((HALO_B, 7 * w), lambda i: (jnp.maximum((nt - 1 - i) * (ts // HALO_B) - 1, 0), 0))
    small = [(CONV_A, w), (CONV_B, w), (1, w), (1, w), (1, w)]
    return _call(
        body, name=name, grid=(nt,),
        in_specs=[tile, halo, pl.BlockSpec((ts, d), rev), pl.BlockSpec((2 * w, d), lambda i: (0, 0))]
        + _small_specs(small),
        out_specs=[pl.BlockSpec((ts, 7 * w), rev)] + _small_specs(small),
        out_shape=[jax.ShapeDtypeStruct((s, 7 * w), BF16)] + [jax.ShapeDtypeStruct(sh, F32) for sh in small],
        scratch_shapes=[pltpu.VMEM((HALO_A + ts, w), F32), pltpu.VMEM((SHIFTS, HALO_B + ts, w), F32),
                        pltpu.VMEM((ts, w), F32),
                        pltpu.VMEM((ts + HALO_A, w), F32), pltpu.VMEM((SHIFTS, ts + HALO_B, w), F32),
                        pltpu.VMEM((ts, w), F32), pltpu.VMEM((CONV_B, SHIFTS, w), F32),
                        pltpu.VMEM((HALO_A, w), F32), pltpu.VMEM((HALO_B, w), F32)],
        args=(p, p, dy, w_out, a_conv, b_conv, bias, ln_g, ln_b), comm=comm)


def _pool_forward_tile(p_ref, halo_ref, first, tile_index, cw_ref, cb_ref, cs_ref, vbuf, c, gc, ts):
    vbuf[0:HALO_P, :] = halo_ref[...] * jnp.where(first, 0.0, 1.0)
    vbuf[HALO_P:HALO_P + ts, :] = p_ref[:, 0:c]
    pos = tile_index * ts + lax.broadcasted_iota(jnp.int32, (ts, 1), 0) + 1
    pooled, inv, gout = [], [], []
    for g, win in enumerate(POOL_WINDOWS):
        cols = slice(g * gc, (g + 1) * gc)
        acc = jnp.zeros((ts, gc), F32)
        for j in range(win):
            acc = acc + vbuf[HALO_P - j:HALO_P - j + ts, cols]
        inv_g = 1.0 / jnp.minimum(pos, win).astype(F32)
        pooled_g = (acc * inv_g - p_ref[:, cols]).astype(BF16)
        pooled.append(pooled_g)
        inv.append(inv_g)
        gout.append(jnp.dot(pooled_g, cw_ref[g], preferred_element_type=F32) + cb_ref[:, cols])
    return pooled, inv, gout


def _odd_mixer_fwd(p, cw, cb, cs, name):
    s = p.shape[0]
    c = p.shape[1] // 2
    gc = c // N_GROUPS
    ts = _row_tile(s, 256)

    def body(p_ref, halo_ref, cw_ref, cb_ref, cs_ref, u_ref, vbuf):
        i = pl.program_id(0)
        _, _, gout = _pool_forward_tile(p_ref, halo_ref, i == 0, i, cw_ref, cb_ref, cs_ref, vbuf, c, gc, ts)
        for g in range(N_GROUPS):
            cols = slice(g * gc, (g + 1) * gc)
            z = p_ref[:, c + g * gc:c + (g + 1) * gc]
            u_ref[:, cols] = (gout[g] * cs_ref[:, cols] * (z * _sigmoid(z))).astype(BF16)

    return pl.pallas_call(
        body, name=name, grid=(s // ts,),
        in_specs=[pl.BlockSpec((ts, 2 * c), lambda i: (i, 0)),
                  pl.BlockSpec((HALO_P, c), lambda i: (jnp.maximum(i * (ts // HALO_P) - 1, 0), 0)),
                  pl.BlockSpec((N_GROUPS, gc, gc), lambda i: (0, 0, 0)),
                  pl.BlockSpec((1, c), lambda i: (0, 0)), pl.BlockSpec((1, c), lambda i: (0, 0))],
        out_specs=pl.BlockSpec((ts, c), lambda i: (i, 0)),
        out_shape=jax.ShapeDtypeStruct((s, c), BF16),
        scratch_shapes=[pltpu.VMEM((HALO_P + ts, c), F32)],
        compiler_params=_params("arbitrary"),
    )(p, p, cw, cb, cs)


def _odd_mixer_bwd(p, du, cw, cb, cs, name):
    s = p.shape[0]
    c = p.shape[1] // 2
    gc = c // N_GROUPS
    ts = _row_tile(s, 256)
    nt = s // ts

    def body(p_ref, halo_ref, du_ref, cw_ref, cb_ref, cs_ref, dp_ref, dcw_ref, dcb_ref, dcs_ref, vbuf, ebuf, carry_e):
        step = pl.program_id(0)
        tile_index = nt - 1 - step

        @pl.when(step == 0)
        def _():
            for ref in (dcw_ref, dcb_ref, dcs_ref, carry_e):
                ref[...] = jnp.zeros_like(ref)

        pooled, inv, gout = _pool_forward_tile(p_ref, halo_ref, tile_index == 0, tile_index, cw_ref, cb_ref, cs_ref,
                                               vbuf, c, gc, ts)
        ebuf[ts:ts + HALO_P, :] = carry_e[...]
        for g, win in enumerate(POOL_WINDOWS):
            cols = slice(g * gc, (g + 1) * gc)
            z = p_ref[:, c + g * gc:c + (g + 1) * gc]
            sz = _sigmoid(z)
            du_g = du_ref[:, cols]
            scale = cs_ref[:, cols]
            d_y = du_g * (z * sz)
            dp_ref[:, c + g * gc:c + (g + 1) * gc] = (du_g * (gout[g] * scale) * _dsilu(z, sz)).astype(BF16)
            dcs_ref[:, cols] += _sum_rows(d_y * gout[g])
            d_gout = d_y * scale
            dcb_ref[:, cols] += _sum_rows(d_gout)
            d_gout_b = d_gout.astype(BF16)
            dcw_ref[g] += lax.dot_general(pooled[g], d_gout_b, (((0,), (0,)), ((), ())), preferred_element_type=F32)
            d_pool = lax.dot_general(d_gout_b, cw_ref[g], (((1,), (1,)), ((), ())), preferred_element_type=F32)
            e = d_pool * inv[g]
            ebuf[0:ts, cols] = e
            d_v = -d_pool
            for o in range(win):
                d_v = d_v + ebuf[o:o + ts, cols]
            dp_ref[:, cols] = d_v.astype(BF16)
            carry_e[:, cols] = e[0:HALO_P, :]

    rev = lambda i: (nt - 1 - i, 0)
    small = [(N_GROUPS, gc, gc), (1, c), (1, c)]
    return pl.pallas_call(
        body, name=name, grid=(nt,),
        in_specs=[pl.BlockSpec((ts, 2 * c), rev),
                  pl.BlockSpec((HALO_P, c), lambda i: (jnp.maximum((nt - 1 - i) * (ts // HALO_P) - 1, 0), 0)),
                  pl.BlockSpec((ts, c), rev),
                  pl.BlockSpec((N_GROUPS, gc, gc), lambda i: (0, 0, 0)),
                  pl.BlockSpec((1, c), lambda i: (0, 0)), pl.BlockSpec((1, c), lambda i: (0, 0))],
        out_specs=[pl.BlockSpec((ts, 2 * c), rev),
                   pl.BlockSpec((N_GROUPS, gc, gc), lambda i: (0, 0, 0)),
                   pl.BlockSpec((1, c), lambda i: (0, 0)), pl.BlockSpec((1, c), lambda i: (0, 0))],
        out_shape=[jax.ShapeDtypeStruct((s, 2 * c), BF16)] + [jax.ShapeDtypeStruct(sh, F32) for sh in small],
        scratch_shapes=[pltpu.VMEM((HALO_P + ts, c), F32), pltpu.VMEM((ts + HALO_P, c), F32),
                        pltpu.VMEM((HALO_P, c), F32)],
        compiler_params=_params("arbitrary"),
    )(p, p, du, cw, cb, cs)


def _cast_into_slot(a, coords, name):
    r, cols = a.shape
    tr = _row_tile(r // 2, 256)
    per = r // 2 // tr

    def body(co_ref, a_ref, o_ref):
        o_ref[0, 0] = a_ref[...].astype(BF16)

    return pl.pallas_call(
        body, name=name,
        grid_spec=pltpu.PrefetchScalarGridSpec(
            num_scalar_prefetch=1, grid=(2, per),
            in_specs=[pl.BlockSpec((tr, cols), lambda h, i, co: (h * per + i, 0))],
            out_specs=pl.BlockSpec((1, 1, tr, cols), lambda h, i, co: (co[0], h, i, 0))),
        out_shape=jax.ShapeDtypeStruct((N_SHARDS, 2, r // 2, cols), BF16),
        compiler_params=_params("arbitrary", "arbitrary"),
    )(coords, a)


def _chip_sum(g, other, coords, name):
    n_sh, _, r2, cols = g.shape
    tr = _row_tile(r2, 256)

    def body(co_ref, g_ref, o_ref, sum_ref, mine_ref):
        v = (g_ref[0, 0].astype(F32) + o_ref[0].astype(F32)).astype(BF16)
        sum_ref[0] = v

        @pl.when(pl.program_id(1) == co_ref[0])
        def _():
            mine_ref[0] = v

    piece = pl.BlockSpec((1, tr, cols), lambda i, s, co: (s, i, 0))
    return pl.pallas_call(
        body, name=name,
        grid_spec=pltpu.PrefetchScalarGridSpec(
            num_scalar_prefetch=1, grid=(r2 // tr, n_sh),
            in_specs=[pl.BlockSpec((1, 1, tr, cols), lambda i, s, co: (s, co[1], i, 0)), piece],
            out_specs=[piece, pl.BlockSpec((1, tr, cols), lambda i, s, co: (co[0], i, 0))]),
        out_shape=[jax.ShapeDtypeStruct((n_sh, r2, cols), BF16)] * 2,
        compiler_params=_params("arbitrary", "arbitrary"),
    )(coords, g, other)


def _shard_sum(pieces, coords, name):
    n_sh, r2, cols = pieces.shape
    tr = _row_tile(r2, 256)

    def body(co_ref, p_ref, o_ref):
        acc = p_ref[0].astype(F32)
        for k in range(1, n_sh):
            acc = acc + p_ref[k].astype(F32)
        o_ref[0] = acc

    return pl.pallas_call(
        body, name=name,
        grid_spec=pltpu.PrefetchScalarGridSpec(
            num_scalar_prefetch=1, grid=(r2 // tr,),
            in_specs=[pl.BlockSpec((n_sh, tr, cols), lambda i, co: (0, i, 0))],
            out_specs=pl.BlockSpec((1, tr, cols), lambda i, co: (co[1], i, 0))),
        out_shape=jax.ShapeDtypeStruct((2, r2, cols), F32),
        compiler_params=_params("arbitrary"),
    )(coords, pieces)


def _sum_small(a, name):
    n, r, cols = a.shape

    def body(a_ref, o_ref):
        acc = a_ref[0]
        for k in range(1, n):
            acc = acc + a_ref[k]
        o_ref[...] = acc

    return pl.pallas_call(
        body, name=name,
        in_specs=[pl.BlockSpec((n, r, cols), lambda: (0, 0, 0))],
        out_specs=pl.BlockSpec((r, cols), lambda: (0, 0)),
        out_shape=jax.ShapeDtypeStruct((r, cols), F32),
        compiler_params=_params(),
    )(a)


def _adamw(w, g, m, v, name):
    r, cols = w.shape
    tr = _row_tile(r, 256) if r % SUBLANES_BF16 == 0 else r

    def body(w_ref, g_ref, m_ref, v_ref, d_ref, nm_ref, nv_ref):
        g = g_ref[...]
        m = ADAM_B1 * m_ref[...] + (1.0 - ADAM_B1) * g
        v = ADAM_B2 * v_ref[...] + (1.0 - ADAM_B2) * (g * g)
        m_hat = m / (1.0 - ADAM_B1 ** ADAM_STEP)
        v_hat = v / (1.0 - ADAM_B2 ** ADAM_STEP)
        d_ref[...] = -ADAM_LR * (m_hat / (jnp.sqrt(v_hat) + ADAM_EPS) + ADAM_WD * w_ref[...])
        nm_ref[...] = m
        nv_ref[...] = v

    blk = pl.BlockSpec((tr, cols), lambda i: (i, 0))
    return pl.pallas_call(
        body, name=name, grid=(r // tr,),
        in_specs=[blk] * 4, out_specs=[blk] * 3,
        out_shape=[jax.ShapeDtypeStruct((r, cols), F32)] * 3,
        compiler_params=_params("arbitrary"),
    )(w, g, m, v)


def _place():
    x, y, c = lax.axis_index("x"), lax.axis_index("y"), lax.axis_index("c")
    other_chips = [(1 - x, y), (x, 1 - y), (1 - x, 1 - y)]
    return x, y, c, other_chips


def _chip(xy):
    return 2 * xy[0] + xy[1]


def _remote(src, dst, send_sem, recv_sem, to):
    return pltpu.make_async_remote_copy(src_ref=src, dst_ref=dst, send_sem=send_sem, recv_sem=recv_sem,
                                        device_id=to, device_id_type=MESH)


def _gather_ici(ctx, k, j, start):
    (x, y, c, chips), b, send, recv = ctx
    blk = b[k].at[_chip((x, y)) if start else _chip(chips[j]), c]
    return _remote(blk, blk, send.at[6 * k + j], recv.at[6 * k + j], (*chips[j], c))


def _gather_d2d(ctx, k, j, start):
    (x, y, c, chips), b, send, recv = ctx
    blk = b[k].at[_chip(chips[j]), c if start else 1 - c]
    return _remote(blk, blk, send.at[6 * k + 3 + j], recv.at[6 * k + 3 + j], (x, y, 1 - c))


def _gather_small(ctx, n, j, start):
    (x, y, c, chips), b, send, recv = ctx
    blk = b[n].at[_chip((x, y)) if start else _chip(chips[j])]
    return _remote(blk, blk, send.at[6 * n + j], recv.at[6 * n + j], (*chips[j], c))


def _gather_comm(bufs, small, forward_at):
    n = len(bufs)
    ici, d2d = _gather_ici, _gather_d2d

    def small_copy(ctx, j, start):
        return _gather_small(ctx, n, j, start)

    def start(srcs, b, send, recv):
        ctx = (_place(), b, send, recv)
        for k in range(n):
            for j in range(3):
                ici(ctx, k, j, True).start()
        if small is not None:
            for j in range(3):
                small_copy(ctx, j, True).start()

    def forward(srcs, b, send, recv):
        ctx = (_place(), b, send, recv)
        for k in range(n):
            for j in range(3):
                ici(ctx, k, j, False).wait_recv()
                d2d(ctx, k, j, True).start()

    def finish(srcs, b, send, recv):
        ctx = (_place(), b, send, recv)
        if small is not None:
            for j in range(3):
                small_copy(ctx, j, False).wait_recv()
                small_copy(ctx, j, True).wait_send()
        for k in range(n):
            for j in range(3):
                d2d(ctx, k, j, False).wait_recv()
                ici(ctx, k, j, True).wait_send()
                d2d(ctx, k, j, True).wait_send()

    all_bufs = list(bufs) + ([small] if small is not None else [])
    return _Comm([], all_bufs, 6 * n + 3, [(0, start), (forward_at, forward)], finish)


def _rmsnorm(x, gain, name):
    s, d = x.shape
    tm = _row_tile(s, 512)

    def body(x_ref, g_ref, h_ref):
        xv = x_ref[...]
        r = lax.rsqrt(_mean_last(xv * xv) + EPS)
        h_ref[...] = (xv * r * g_ref[...]).astype(BF16)

    return pl.pallas_call(
        body, name=name, grid=(s // tm,),
        in_specs=[pl.BlockSpec((tm, d), lambda i: (i, 0)), pl.BlockSpec((1, d), lambda i: (0, 0))],
        out_specs=pl.BlockSpec((tm, d), lambda i: (i, 0)),
        out_shape=jax.ShapeDtypeStruct((s, d), BF16),
        compiler_params=_params("arbitrary"),
    )(x, gain)


def _gathered_in_proj(h, bufs, small, order, name):
    s, d = h.shape
    n_sh, _, r2, ns = bufs[0].shape
    assert d == 2 * r2
    n = len(bufs)
    tm = _row_tile(s, 512)
    n_i = s // tm
    hook_i = max(n_i - 2, 0)
    n_sems = 6 * n + 3

    def body(order_ref, h_ref, *rest):
        p_ref = rest[n + 1]
        b = rest[n + 2:2 * n + 3]
        w_vmem, w_sems, send, recv = rest[2 * n + 3:]
        j, i = pl.program_id(0), pl.program_id(1)
        ctx = (_place(), b, send, recv)

        def fetch(q):
            return pltpu.make_async_copy(b[0].at[order_ref[q]], w_vmem.at[q % 2], w_sems.at[q % 2])

        @pl.when((j == 0) & (i == 0))
        def _():
            for k in range(n):
                for peer in range(3):
                    _gather_ici(ctx, k, peer, True).start()
            for peer in range(3):
                _gather_small(ctx, n, peer, True).start()
            fetch(0).start()
            fetch(0).wait()

        for q in range(1, n_sh):
            @pl.when((j == q - 1) & (i == hook_i))
            def _(q=q):
                _gather_ici(ctx, 0, q - 1, False).wait_recv()
                _gather_d2d(ctx, 0, q - 1, True).start()
                _gather_d2d(ctx, 0, q - 1, False).wait_recv()
                fetch(q).start()
                if q == n_sh - 1:
                    for k in range(1, n):
                        for peer in range(3):
                            _gather_ici(ctx, k, peer, False).wait_recv()
                            _gather_d2d(ctx, k, peer, True).start()

            @pl.when((j == q) & (i == 0))
            def _(q=q):
                fetch(q).wait()

        wv = w_vmem.at[j % 2]
        p_ref[...] = (jnp.dot(h_ref[:, 0:r2], wv[0], preferred_element_type=F32)
                      + jnp.dot(h_ref[:, r2:d], wv[1], preferred_element_type=F32))

        @pl.when((j == n_sh - 1) & (i == n_i - 1))
        def _():
            for peer in range(3):
                _gather_small(ctx, n, peer, False).wait_recv()
                _gather_small(ctx, n, peer, True).wait_send()
            for k in range(n):
                for peer in range(3):
                    if k > 0:
                        _gather_d2d(ctx, k, peer, False).wait_recv()
                    _gather_ici(ctx, k, peer, True).wait_send()
                    _gather_d2d(ctx, k, peer, True).wait_send()

    all_bufs = list(bufs) + [small]
    out = pl.pallas_call(
        body, name=name,
        grid_spec=pltpu.PrefetchScalarGridSpec(
            num_scalar_prefetch=1, grid=(n_sh, n_i),
            in_specs=[pl.BlockSpec((tm, d), lambda j, i, o: (i, 0))] + [ANY] * (n + 1),
            out_specs=[pl.BlockSpec((tm, ns), lambda j, i, o: (i, o[j]))] + [ANY] * (n + 1),
            scratch_shapes=[pltpu.VMEM((2, 2, r2, ns), BF16), pltpu.SemaphoreType.DMA((2,)),
                            pltpu.SemaphoreType.DMA((n_sems,)), pltpu.SemaphoreType.DMA((n_sems,))]),
        out_shape=[jax.ShapeDtypeStruct((s, n_sh * ns), F32)]
        + [jax.ShapeDtypeStruct(a.shape, a.dtype) for a in all_bufs],
        input_output_aliases={2 + t: 1 + t for t in range(n + 1)},
        compiler_params=_params("arbitrary", "arbitrary"),
    )(order, h, *all_bufs)
    return out[0], list(out[1:])


def _exchange_core_halves(grads, name):
    n = len(grads)

    def body(*refs):
        ins, outs = refs[:n], refs[n:2 * n]
        send_sems, recv_sems = refs[2 * n:]
        x, y, c, _ = _place()
        copies = [_remote(ins[k].at[s, 1 - c], outs[k].at[s], send_sems.at[N_SHARDS * k + s],
                          recv_sems.at[N_SHARDS * k + s], (x, y, 1 - c))
                  for k in range(n) for s in range(N_SHARDS)]
        for cp in copies:
            cp.start()
        for cp in copies:
            cp.wait()

    return pl.pallas_call(
        body, name=name, in_specs=[ANY] * n, out_specs=[ANY] * n,
        out_shape=[jax.ShapeDtypeStruct((N_SHARDS,) + a.shape[2:], a.dtype) for a in grads],
        scratch_shapes=[pltpu.SemaphoreType.DMA((N_SHARDS * n,))] * 2,
    )(*grads)


def _scatter_comm(chip_sums, landing):
    n = len(chip_sums)

    def big(srcs, b, send, recv, k, j, start):
        x, y, c, chips = _place()
        dst = b[k].at[_chip((x, y)) if start else _chip(chips[j])]
        return _remote(srcs[k].at[_chip(chips[j])], dst, send.at[3 * k + j], recv.at[3 * k + j], (*chips[j], c))

    def start(srcs, b, send, recv):
        for k in range(n):
            for j in range(3):
                big(srcs, b, send, recv, k, j, True).start()

    def finish(srcs, b, send, recv):
        for k in range(n):
            for j in range(3):
                big(srcs, b, send, recv, k, j, False).wait_recv()
                big(srcs, b, send, recv, k, j, True).wait_send()

    return _Comm(chip_sums, landing, 3 * n, [(0, start)], finish)


def _join_comm(halves, small):
    n = len(halves)
    flips = [(fx, fy, fc) for fx in (0, 1) for fy in (0, 1) for fc in (0, 1)][1:]

    def half(b, send, recv, k, start):
        x, y, c, _ = _place()
        return _remote(b[k].at[c], b[k].at[c if start else 1 - c], send.at[k], recv.at[k], (x, y, 1 - c))

    def small_copy(b, send, recv, q, start):
        x, y, c, _ = _place()
        px, py, pc = x ^ flips[q][0], y ^ flips[q][1], c ^ flips[q][2]
        blk = b[n].at[4 * x + 2 * y + c if start else 4 * px + 2 * py + pc]
        return _remote(blk, blk, send.at[n + q], recv.at[n + q], (px, py, pc))

    def start(srcs, b, send, recv):
        for q in range(len(flips)):
            small_copy(b, send, recv, q, True).start()
        for k in range(n):
            half(b, send, recv, k, True).start()

    def finish(srcs, b, send, recv):
        for q in range(len(flips)):
            small_copy(b, send, recv, q, False).wait()
        for k in range(n):
            half(b, send, recv, k, False).wait()

    return _Comm([], list(halves) + [small], n + len(flips), [(0, start)], finish)


def _flat_rows(parts):
    flat = jnp.concatenate([p.reshape(-1) for p in parts])
    assert flat.shape[0] % LANES == 0
    return flat.reshape(-1, LANES)


def _unflatten(flat, shapes):
    out, at = [], 0
    for sh in shapes:
        size = 1
        for dim in sh:
            size *= dim
        out.append(flat[at:at + size].reshape(sh))
        at += size
    assert at == flat.shape[0], (at, flat.shape)
    return out


def _col_shards_to_full(a, rows):
    q = a.shape[1] // rows
    return a.reshape(N_SHARDS, rows, q).transpose(1, 0, 2).reshape(rows, N_SHARDS * q)


def _my_col_shard(full, chip):
    rows, cols = full.shape
    q = cols // N_SHARDS
    return lax.dynamic_index_in_dim(full.reshape(rows, N_SHARDS, q), chip, axis=1, keepdims=False)


def kernel(x, e_norm_pre, e_norm_post, e_w_in, e_a_conv, e_b_conv, e_b_conv_bias, e_b_ln_g, e_b_ln_b, e_w_out, o_norm_pre, o_norm_post, o_w_in, o_c_w, o_c_b, o_c_scale, o_w_out, loss_target, m_e_norm_pre, m_e_norm_post, m_e_w_in, m_e_a_conv, m_e_b_conv, m_e_b_conv_bias, m_e_b_ln_g, m_e_b_ln_b, m_e_w_out, m_o_norm_pre, m_o_norm_post, m_o_w_in, m_o_c_w, m_o_c_b, m_o_c_scale, m_o_w_out, v_e_norm_pre, v_e_norm_post, v_e_w_in, v_e_a_conv, v_e_b_conv, v_e_b_conv_bias, v_e_b_ln_g, v_e_b_ln_b, v_e_w_out, v_o_norm_pre, v_o_norm_post, v_o_w_in, v_o_c_w, v_o_c_b, v_o_c_scale, v_o_w_out):
    _, s, d = x.shape
    w = d // 2
    c = d
    gc = c // N_GROUPS
    wq, cq, gq = w // N_SHARDS, c // N_SHARDS, gc // N_SHARDS
    chip = 2 * lax.axis_index("x") + lax.axis_index("y")
    core = lax.axis_index("c")
    x2 = x.reshape(s, d)
    target = loss_target.reshape(s, d)

    big_w = [e_w_in[0], e_w_out[0], o_w_in[0], o_c_w[0].reshape(N_GROUPS * gq, gc), o_w_out[0]]
    big_m = [m_e_w_in[0], m_e_w_out[0], m_o_w_in[0], m_o_c_w[0].reshape(N_GROUPS * gq, gc), m_o_w_out[0]]
    big_v = [v_e_w_in[0], v_e_w_out[0], v_o_w_in[0], v_o_c_w[0].reshape(N_GROUPS * gq, gc), v_o_w_out[0]]
    coords = jnp.stack([chip, core]).astype(jnp.int32)
    slots = [_cast_into_slot(a, coords, "cast_%d" % k) for k, a in enumerate(big_w)]
    sharded_small = _flat_rows([e_a_conv[0], e_b_conv[0], o_norm_pre, o_norm_post, o_c_scale, o_c_b[0]])
    small_slots = lax.dynamic_update_index_in_dim(jnp.zeros((N_SHARDS,) + sharded_small.shape, F32), sharded_small,
                                                  chip, 0)
    xi, yi = lax.axis_index("x"), lax.axis_index("y")
    order = jnp.stack([chip, 2 * (1 - xi) + yi, 2 * xi + (1 - yi), 2 * (1 - xi) + (1 - yi)]).astype(jnp.int32)
    h0 = _rmsnorm(x2, e_norm_pre, "e_pre_norm")
    p0, (e_w_in_g, e_w_out_g, small_g4) = _gathered_in_proj(h0, slots[:2], small_slots, order, "e_in_proj")
    e_w_in_sm = e_w_in_g.reshape((N_SHARDS,) + big_w[0].shape)
    e_w_out_f = e_w_out_g.reshape(w + w, d)
    sm = small_g4.reshape(N_SHARDS, -1)
    at = [0]

    def take(rows, q):
        blk = sm[:, at[0]:at[0] + rows * q]
        at[0] += rows * q
        return _col_shards_to_full(blk, rows)

    a_conv_f = take(CONV_A, wq)
    b_conv_f = take(CONV_B, wq)
    o_pre_f = take(1, cq)
    o_post_f = take(1, cq)
    cs_f = take(1, cq)
    cb_f = take(N_GROUPS, gq).reshape(1, c)

    mixer_steps = s // _row_tile(s, 128)
    (u0, x1, y0), odd_g = _even_mixer_fwd(p0, e_w_out_f, x2, e_norm_post, a_conv_f, b_conv_f, e_b_conv_bias, e_b_ln_g,
                                          e_b_ln_b, "e_mixer_out_proj",
                                          comm=_gather_comm(slots[2:], None, (2 * mixer_steps) // 3))
    o_w_in_sm = odd_g[0].reshape((N_SHARDS,) + big_w[2].shape)
    cw_f = odd_g[1].reshape(N_SHARDS, N_GROUPS, gq, gc).transpose(1, 0, 2, 3).reshape(N_GROUPS, gc, gc)
    o_w_out_f = odd_g[2].reshape(c, d)
    (p1, h1), _ = _norm_matmul(x1, o_pre_f, o_w_in_sm, "o_in_proj")
    u1 = _odd_mixer_fwd(p1, cw_f, cb_f, cs_f, "o_mixer_fwd")
    d_y1, d_x2, d_o_post, loss_part = _matmul_post_loss(u1, o_w_out_f, x1, o_post_f, target, "o_out_proj_loss")

    def as_pieces(g, k):
        return g.reshape(N_SHARDS, 2, big_w[k].shape[0] // 2, big_w[k].shape[1])

    def chip_sums(ks, grads, name):
        pieces = [as_pieces(g, k) for k, g in zip(ks, grads)]
        from_sibling = _exchange_core_halves(pieces, name)
        both = [_chip_sum(g, o, coords, "chip_sum_%d" % k) for k, g, o in zip(ks, pieces, from_sibling)]
        return [b[0] for b in both], [b[1] for b in both]

    g_o_w_out = _matmul_tn(u1, d_y1, 1, "o_w_out_grad")
    d_u1, _ = _matmul_nt(d_y1, o_w_out_f[None], "o_out_proj_bwd")
    d_p1, d_cw, d_cb, d_cs = _odd_mixer_bwd(p1, d_u1, cw_f, cb_f, cs_f, "o_mixer_bwd")
    g_o_w_in = _matmul_tn(h1, d_p1, N_SHARDS, "o_w_in_grad")
    d_h1, _ = _matmul_nt(d_p1, o_w_in_sm, "o_in_proj_bwd")
    d_x1, d_o_pre, d_y0, d_e_post = _norm_bwd(d_h1, x1, o_pre_f, d_x2, "o_pre_norm_bwd", post=(y0, e_norm_post))

    g_e_w_out = _matmul_tn(u0, d_y0, 1, "e_w_out_grad")
    g_cw = d_cw.reshape(N_GROUPS, N_SHARDS, gq, gc).transpose(1, 0, 2, 3).astype(BF16)
    sums_a, landing_a = chip_sums([1, 2, 3, 4], [g_e_w_out, g_o_w_in, g_cw, g_o_w_out], "exchange_core_halves_a")
    (d_p0, d_a_conv, d_b_conv, d_bias, d_ln_g, d_ln_b), landed_a = _even_mixer_bwd(
        p0, d_y0, e_w_out_f, a_conv_f, b_conv_f, e_b_conv_bias, e_b_ln_g, e_b_ln_b, "e_mixer_bwd",
        comm=_scatter_comm(sums_a, landing_a))
    g_e_w_in = _matmul_tn(h0, d_p0, N_SHARDS, "e_w_in_grad")
    sums_b, landing_b = chip_sums([0], [g_e_w_in], "exchange_core_halves_b")
    d_h0, landed_b = _matmul_nt(d_p0, e_w_in_sm, "e_in_proj_bwd", comm=_scatter_comm(sums_b, landing_b))
    grad_x, d_e_pre = _norm_bwd(d_h0, x2, e_norm_pre, d_x1, "e_pre_norm_bwd")

    landed = landed_b + landed_a
    reduced = [_shard_sum(sc, coords, "shard_sum_%d" % k) for k, sc in enumerate(landed)]
    small_parts = _flat_rows([loss_part[0], d_e_pre, d_e_post, d_bias, d_ln_g, d_ln_b, d_a_conv, d_b_conv,
                              d_o_pre, d_o_post, d_cs, d_cb])
    small_rows = lax.dynamic_update_index_in_dim(jnp.zeros((N_DEVICES,) + small_parts.shape, F32), small_parts,
                                                 2 * chip + core, 0)
    joined = _comm_only(_join_comm(reduced, small_rows), "join_core_halves")
    big_g = [j.reshape(a.shape) for j, a in zip(joined[:5], big_w)]
    small_sum = _sum_small(joined[5], "small_sum").reshape(-1)
    (loss_row, g_e_pre, g_e_post, g_bias, g_ln_g, g_ln_b, g_a_conv_f, g_b_conv_f, g_o_pre_f, g_o_post_f, g_cs_f,
     g_cb_f) = _unflatten(small_sum, [(LANES,), (1, d), (1, d), (1, w), (1, w), (1, w), (CONV_A, w), (CONV_B, w),
                                      (1, c), (1, c), (1, c), (1, c)])
    loss = loss_row[0]
    g_a_conv = _my_col_shard(g_a_conv_f, chip)
    g_b_conv = _my_col_shard(g_b_conv_f, chip)
    g_o_pre = _my_col_shard(g_o_pre_f, chip)
    g_o_post = _my_col_shard(g_o_post_f, chip)
    g_cs = _my_col_shard(g_cs_f, chip)
    g_cb = _my_col_shard(g_cb_f.reshape(N_GROUPS, gc), chip)

    big_upd = [_adamw(wt, g, m, v, "adamw_%d" % k) for k, (wt, g, m, v) in enumerate(zip(big_w, big_g, big_m, big_v))]
    small_w = [e_norm_pre, e_norm_post, e_b_conv_bias, e_b_ln_g, e_b_ln_b, e_a_conv[0], e_b_conv[0],
               o_norm_pre, o_norm_post, o_c_b[0], o_c_scale]
    small_m = [m_e_norm_pre, m_e_norm_post, m_e_b_conv_bias, m_e_b_ln_g, m_e_b_ln_b, m_e_a_conv[0], m_e_b_conv[0],
               m_o_norm_pre, m_o_norm_post, m_o_c_b[0], m_o_c_scale]
    small_v = [v_e_norm_pre, v_e_norm_post, v_e_b_conv_bias, v_e_b_ln_g, v_e_b_ln_b, v_e_a_conv[0], v_e_b_conv[0],
               v_o_norm_pre, v_o_norm_post, v_o_c_b[0], v_o_c_scale]
    small_g = [g_e_pre, g_e_post, g_bias, g_ln_g, g_ln_b, g_a_conv, g_b_conv, g_o_pre, g_o_post, g_cb, g_cs]
    small_shapes = [a.shape for a in small_w]
    small_upd = _adamw(_flat_rows(small_w), _flat_rows(small_g), _flat_rows(small_m), _flat_rows(small_v),
                       "adamw_small")
    small_delta, small_new_m, small_new_v = [_unflatten(u.reshape(-1), small_shapes) for u in small_upd]

    def ordered(small, big, lead):
        (n_pre, n_post, bias, ln_g, ln_b, a_conv, b_conv, o_pre, o_post, cb, cs) = small
        (w_in, w_out, ow_in, cw, ow_out) = big
        out = [n_pre, n_post, w_in[None], a_conv[None], b_conv[None], bias, ln_g, ln_b, w_out[None], o_pre, o_post,
               ow_in[None], cw.reshape(1, N_GROUPS, gq, gc), cb[None], cs, ow_out[None]]
        return out

    grads = ordered(small_g, big_g, None)
    deltas = ordered(small_delta, [u[0] for u in big_upd], None)
    new_m = ordered(small_new_m, [u[1] for u in big_upd], None)
    new_v = ordered(small_new_v, [u[2] for u in big_upd], None)
    return (loss, grad_x.reshape(1, s, d), *grads, *deltas, *new_m, *new_v)
```

```python
import functools

import jax
import jax.numpy as jnp
from jax import lax
from jax.experimental import pallas as pl
from jax.experimental.pallas import tpu as pltpu

F32 = jnp.float32
BF16 = jnp.bfloat16
MESH = pl.DeviceIdType.MESH

EPS = 1e-6
CONV_A = 3
CONV_B = 31
POOL_WINDOWS = (2, 4, 8, 16)
N_GROUPS = len(POOL_WINDOWS)
N_SHARDS = 4
N_DEVICES = 8
ADAM_LR = 0.001
ADAM_B1 = 0.9
ADAM_B2 = 0.999
ADAM_EPS = 1e-08
ADAM_WD = 0.01
ADAM_STEP = 10

LANES = 128
SUBLANES_BF16 = 16
HALO_A = 8
HALO_B = 32
HALO_P = 16
SHIFTS = 8
ROW_BLOCK = 32
LANE_BLOCK = 256
VMEM_LIMIT = 56 * 1024 * 1024
TN_ACC_BYTES = 8 * 1024 * 1024


def _row_tile(n, pref):
    t = max(min(n, pref) // SUBLANES_BF16, 1) * SUBLANES_BF16
    while t > SUBLANES_BF16 and (n % t or t % SUBLANES_BF16):
        t -= SUBLANES_BF16
    assert n % t == 0, (n, pref)
    return t


def _col_chunk(n, pref):
    t = (min(n, pref) // LANES) * LANES
    while t > LANES and n % t:
        t -= LANES
    assert t >= LANES and n % t == 0, (n, pref)
    return t


def _params(*sem):
    return pltpu.CompilerParams(dimension_semantics=tuple(sem) if sem else None, vmem_limit_bytes=VMEM_LIMIT)


ANY = pl.BlockSpec(memory_space=pl.ANY)


class _Comm:
    def __init__(self, srcs, bufs, n_sems, phases, finish):
        self.srcs, self.bufs, self.n_sems, self.phases, self.finish = list(srcs), list(bufs), n_sems, phases, finish


def _call(body, *, name, grid, in_specs, out_specs, out_shape, args, scratch_shapes=(), comm=None):
    params = _params(*(("arbitrary",) * len(grid)))
    if comm is None:
        out = pl.pallas_call(body, name=name, grid=grid, in_specs=in_specs, out_specs=out_specs, out_shape=out_shape,
                             scratch_shapes=scratch_shapes, compiler_params=params)(*args)
        return list(out), []
    n_in, n_out, n_scr = len(in_specs), len(out_specs), len(scratch_shapes)
    ns, nb = len(comm.srcs), len(comm.bufs)
    total = 1
    for size in grid:
        total *= size

    def fused(*refs):
        ins, srcs = refs[:n_in], refs[n_in:n_in + ns]
        at = n_in + ns + nb
        outs, bufs = refs[at:at + n_out], refs[at + n_out:at + n_out + nb]
        scratch = refs[at + n_out + nb:at + n_out + nb + n_scr]
        send_sems, recv_sems = refs[-2:]
        step = 0
        for axis, size in enumerate(grid):
            step = step * size + pl.program_id(axis)
        for when, fn in comm.phases:
            pl.when(step == when)(functools.partial(fn, srcs, bufs, send_sems, recv_sems))
        body(*ins, *outs, *scratch)
        pl.when(step == total - 1)(functools.partial(comm.finish, srcs, bufs, send_sems, recv_sems))

    out = pl.pallas_call(
        fused, name=name, grid=grid,
        in_specs=list(in_specs) + [ANY] * (ns + nb), out_specs=list(out_specs) + [ANY] * nb,
        out_shape=list(out_shape) + [jax.ShapeDtypeStruct(b.shape, b.dtype) for b in comm.bufs],
        input_output_aliases={n_in + ns + i: n_out + i for i in range(nb)},
        scratch_shapes=list(scratch_shapes) + [pltpu.SemaphoreType.DMA((comm.n_sems,))] * 2,
        compiler_params=params,
    )(*args, *comm.srcs, *comm.bufs)
    return list(out[:n_out]), list(out[n_out:])


def _comm_only(comm, name):
    ns, nb = len(comm.srcs), len(comm.bufs)

    def body(*refs):
        srcs, bufs = refs[:ns], refs[ns + nb:ns + 2 * nb]
        send_sems, recv_sems = refs[-2:]
        for _, fn in comm.phases:
            fn(srcs, bufs, send_sems, recv_sems)
        comm.finish(srcs, bufs, send_sems, recv_sems)

    return pl.pallas_call(
        body, name=name, in_specs=[ANY] * (ns + nb), out_specs=[ANY] * nb,
        out_shape=[jax.ShapeDtypeStruct(b.shape, b.dtype) for b in comm.bufs],
        input_output_aliases={ns + i: i for i in range(nb)},
        scratch_shapes=[pltpu.SemaphoreType.DMA((comm.n_sems,))] * 2,
    )(*comm.srcs, *comm.bufs)


def _sigmoid(v):
    return jax.nn.sigmoid(v)


def _dsilu(v, s):
    return s * (1.0 + v * (1.0 - s))


def _mean_last(v):
    return jnp.mean(v, axis=-1, keepdims=True)


def _sum_rows(v):
    return jnp.sum(v, axis=0, keepdims=True)


def _norm_matmul(x, gain, w_sm, name, comm=None):
    s, d = x.shape
    n_sh, _, ns = w_sm.shape
    tm = _row_tile(s, 1024)

    def body(x_ref, g_ref, w_ref, p_ref, h_ref):
        @pl.when(pl.program_id(1) == 0)
        def _():
            xv = x_ref[...]
            r = lax.rsqrt(_mean_last(xv * xv) + EPS)
            h_ref[...] = (xv * r * g_ref[...]).astype(BF16)

        p_ref[...] = jnp.dot(h_ref[...], w_ref[0], preferred_element_type=F32)

    return _call(
        body, name=name, grid=(s // tm, n_sh),
        in_specs=[pl.BlockSpec((tm, d), lambda i, j: (i, 0)),
                  pl.BlockSpec((1, d), lambda i, j: (0, 0)),
                  pl.BlockSpec((1, d, ns), lambda i, j: (j, 0, 0))],
        out_specs=[pl.BlockSpec((tm, ns), lambda i, j: (i, j)),
                   pl.BlockSpec((tm, d), lambda i, j: (i, 0))],
        out_shape=[jax.ShapeDtypeStruct((s, n_sh * ns), F32), jax.ShapeDtypeStruct((s, d), BF16)],
        args=(x, gain, w_sm), comm=comm)


def _matmul_post_loss(u, w, x_res, gain, target, name):
    s, k = u.shape
    d = w.shape[1]
    tm = _row_tile(s, 256)

    def body(u_ref, w_ref, x_ref, g_ref, t_ref, dy_ref, dout_ref, dg_ref, loss_ref):
        @pl.when(pl.program_id(0) == 0)
        def _():
            dg_ref[...] = jnp.zeros_like(dg_ref)
            loss_ref[...] = jnp.zeros_like(loss_ref)

        y = jnp.dot(u_ref[...], w_ref[...], preferred_element_type=F32)
        r = lax.rsqrt(_mean_last(y * y) + EPS)
        n = y * r
        g = g_ref[...]
        err = x_ref[...] + n * g - t_ref[...]
        loss_ref[...] += 0.5 * jnp.sum(_mean_last(err * err))
        dout = err * (1.0 / d)
        dout_ref[...] = dout
        dg_ref[...] += _sum_rows(dout * n)
        dn = dout * g
        dy_ref[...] = (r * (dn - n * _mean_last(dn * n))).astype(BF16)

    return pl.pallas_call(
        body, name=name, grid=(s // tm,),
        in_specs=[pl.BlockSpec((tm, k), lambda i: (i, 0)),
                  pl.BlockSpec((k, d), lambda i: (0, 0)),
                  pl.BlockSpec((tm, d), lambda i: (i, 0)),
                  pl.BlockSpec((1, d), lambda i: (0, 0)),
                  pl.BlockSpec((tm, d), lambda i: (i, 0))],
        out_specs=[pl.BlockSpec((tm, d), lambda i: (i, 0)),
                   pl.BlockSpec((tm, d), lambda i: (i, 0)),
                   pl.BlockSpec((1, d), lambda i: (0, 0)),
                   pl.BlockSpec((8, LANES), lambda i: (0, 0))],
        out_shape=[jax.ShapeDtypeStruct((s, d), BF16), jax.ShapeDtypeStruct((s, d), F32),
                   jax.ShapeDtypeStruct((1, d), F32), jax.ShapeDtypeStruct((8, LANES), F32)],
        compiler_params=_params("arbitrary"),
    )(u, w, x_res, gain, target)


def _matmul_nt(a, w_sm, name, comm=None):
    s, ncols = a.shape
    n_sh, r, ns = w_sm.shape
    assert ncols == n_sh * ns
    tm = _row_tile(s, 1024)
    nc = _col_chunk(ns, 1792)
    per = ns // nc
    steps = n_sh * per

    def body(a_ref, w_ref, o_ref):
        part = lax.dot_general(a_ref[...], w_ref[0], (((1,), (1,)), ((), ())), preferred_element_type=F32)

        @pl.when(pl.program_id(1) == 0)
        def _():
            o_ref[...] = part

        @pl.when(pl.program_id(1) > 0)
        def _():
            o_ref[...] += part

    out, bufs = _call(
        body, name=name, grid=(s // tm, steps),
        in_specs=[pl.BlockSpec((tm, nc), lambda i, j: (i, j)),
                  pl.BlockSpec((1, r, nc), lambda i, j: (j // per, 0, j % per))],
        out_specs=[pl.BlockSpec((tm, r), lambda i, j: (i, 0))],
        out_shape=[jax.ShapeDtypeStruct((s, r), F32)],
        args=(a, w_sm), comm=comm)
    return out[0], bufs


def _matmul_tn(a, b, n_sh, name):
    s, k = a.shape
    n = b.shape[1]
    ns = n // n_sh
    tk = _col_chunk(k, TN_ACC_BYTES // (4 * ns))
    ts = _row_tile(s, 1024)
    n_s = s // ts

    def body(a_ref, b_ref, o_ref, acc_ref):
        part = lax.dot_general(a_ref[...], b_ref[...], (((0,), (0,)), ((), ())), preferred_element_type=F32)

        @pl.when(pl.program_id(2) == 0)
        def _():
            acc_ref[...] = part

        @pl.when(pl.program_id(2) > 0)
        def _():
            acc_ref[...] += part

        @pl.when(pl.program_id(2) == n_s - 1)
        def _():
            o_ref[0] = acc_ref[...].astype(BF16)

    return pl.pallas_call(
        body, name=name, grid=(n_sh, k // tk, n_s),
        in_specs=[pl.BlockSpec((ts, tk), lambda j, i, t: (t, i)),
                  pl.BlockSpec((ts, ns), lambda j, i, t: (t, j))],
        out_specs=pl.BlockSpec((1, tk, ns), lambda j, i, t: (j, i, 0)),
        out_shape=jax.ShapeDtypeStruct((n_sh, k, ns), BF16),
        scratch_shapes=[pltpu.VMEM((tk, ns), F32)],
        compiler_params=_params("arbitrary", "arbitrary", "arbitrary"),
    )(a, b)


def _norm_bwd(dh, x, gain, dres, name, post=None):
    s, d = x.shape
    tm = _row_tile(s, 256)
    with_post = post is not None

    def rms_bwd(dout, v, g):
        r = lax.rsqrt(_mean_last(v * v) + EPS)
        n = v * r
        dn = dout * g
        return r * (dn - n * _mean_last(dn * n)), _sum_rows(dout * n)

    def body(*refs):
        if with_post:
            dh_ref, x_ref, g_ref, dres_ref, y_ref, gp_ref, dx_ref, dg_ref, dy_ref, dgp_ref = refs
        else:
            dh_ref, x_ref, g_ref, dres_ref, dx_ref, dg_ref = refs

        @pl.when(pl.program_id(0) == 0)
        def _():
            dg_ref[...] = jnp.zeros_like(dg_ref)
            if with_post:
                dgp_ref[...] = jnp.zeros_like(dgp_ref)

        dv, dg = rms_bwd(dh_ref[...], x_ref[...], g_ref[...])
        dx = dres_ref[...] + dv
        dx_ref[...] = dx
        dg_ref[...] += dg
        if with_post:
            dy, dgp = rms_bwd(dx, y_ref[...], gp_ref[...])
            dy_ref[...] = dy.astype(BF16)
            dgp_ref[...] += dgp

    row = pl.BlockSpec((tm, d), lambda i: (i, 0))
    vec = pl.BlockSpec((1, d), lambda i: (0, 0))
    in_specs = [row, row, vec, row]
    out_specs = [row, vec]
    out_shape = [jax.ShapeDtypeStruct((s, d), F32), jax.ShapeDtypeStruct((1, d), F32)]
    args = [dh, x, gain, dres]
    if with_post:
        in_specs += [row, vec]
        out_specs += [row, vec]
        out_shape += [jax.ShapeDtypeStruct((s, d), BF16), jax.ShapeDtypeStruct((1, d), F32)]
        args += list(post)
    return pl.pallas_call(
        body, name=name, grid=(s // tm,), in_specs=in_specs, out_specs=out_specs, out_shape=out_shape,
        compiler_params=_params("arbitrary"),
    )(*args)


def _fill_shifted_down(sh, rows):
    for b in range(1, SHIFTS):
        sh[b, SHIFTS:rows, :] = sh[0, SHIFTS - b:rows - b, :]


def _fill_shifted_up(sh, rows):
    for b in range(1, SHIFTS):
        sh[b, 0:rows - SHIFTS, :] = sh[0, b:rows - SHIFTS + b, :]


def _for_blocks(ts, w, fn):
    lb = min(LANE_BLOCK, w)
    for l0 in range(0, w, lb):
        def rows(rb, carry, l0=l0):
            fn(pl.multiple_of(rb * ROW_BLOCK, ROW_BLOCK), slice(l0, l0 + lb))
            return carry

        lax.fori_loop(0, ts // ROW_BLOCK, rows, 0)


TAP_SPAN = SHIFTS * ((CONV_B - 1) // SHIFTS)
WINDOW = ROW_BLOCK + TAP_SPAN


def _taps_of(b):
    return [(a, SHIFTS * a + b) for a in range((CONV_B - 1 - b) // SHIFTS + 1)]


def _conv31(sh, base, step, wt_ref, bias_ref, out_ref, ts, w):
    low = min(0, step * (TAP_SPAN // SHIFTS))

    def block(r0, lanes):
        acc = jnp.zeros((ROW_BLOCK, lanes.stop - lanes.start), F32)
        if bias_ref is not None:
            acc = acc + bias_ref[:, lanes]
        for b in range(SHIFTS):
            window = sh[b, pl.ds(pl.multiple_of(r0 + (base + low), SHIFTS), WINDOW), lanes]
            for a, j in _taps_of(b):
                at = step * a - low
                acc = acc + wt_ref[CONV_B - 1 - j:CONV_B - j, lanes] * window[at:at + ROW_BLOCK, :]
        out_ref[pl.ds(r0, ROW_BLOCK), lanes] = acc

    _for_blocks(ts, w, block)


def _conv31_weight_grad(d_sh, x_sh, wacc, ts, w):
    def block(r0, lanes):
        d = d_sh[0, pl.ds(r0, ROW_BLOCK), lanes]
        for b in range(SHIFTS):
            window = x_sh[b, pl.ds(pl.multiple_of(r0 + (HALO_B - TAP_SPAN), SHIFTS), WINDOW), lanes]
            for a, j in _taps_of(b):
                at = TAP_SPAN - SHIFTS * a
                prod = d * window[at:at + ROW_BLOCK, :]
                part = prod[0:SHIFTS, :]
                for q in range(1, ROW_BLOCK // SHIFTS):
                    part = part + prod[q * SHIFTS:(q + 1) * SHIFTS, :]
                wacc[CONV_B - 1 - j, :, lanes] += part

    _for_blocks(ts, w, block)


def _even_forward_tile(p_ref, halo_ref, first, a_conv_ref, b_conv_ref, bias_ref, lng_ref, lnb_ref, qbuf, ysh, y1buf,
                       w, ts):
    def col(ref, k, rows=slice(None)):
        return ref[rows, k * w:(k + 1) * w]

    a_x, a_b, a_c, a_z = col(p_ref, 0), col(p_ref, 1), col(p_ref, 2), col(p_ref, 3)
    b_val, b_gate, b_z = col(p_ref, 4), col(p_ref, 5), col(p_ref, 6)
    keep = jnp.where(first, 0.0, 1.0)

    rows_a = slice(HALO_B - HALO_A, HALO_B)
    qbuf[0:HALO_A, :] = col(halo_ref, 2, rows_a) * col(halo_ref, 0, rows_a) * keep
    qbuf[HALO_A:HALO_A + ts, :] = a_c * a_x
    cq = jnp.zeros((ts, w), F32)
    for j in range(CONV_A):
        cq = cq + a_conv_ref[CONV_A - 1 - j:CONV_A - j, :] * qbuf[HALO_A - j:HALO_A - j + ts, :]
    ya = a_b * cq

    ysh[0, 0:HALO_B, :] = col(halo_ref, 4) * _sigmoid(col(halo_ref, 5)) * keep
    ysh[0, HALO_B:HALO_B + ts, :] = b_val * _sigmoid(b_gate)
    _fill_shifted_down(ysh, HALO_B + ts)
    _conv31(ysh, HALO_B, -SHIFTS, b_conv_ref, bias_ref, y1buf, ts, w)
    yb1 = y1buf[...]
    xc = yb1 - _mean_last(yb1)
    rstd = lax.rsqrt(_mean_last(xc * xc) + EPS)
    xhat = xc * rstd
    yb2 = xhat * lng_ref[...] + lnb_ref[...]
    return dict(a_x=a_x, a_b=a_b, a_c=a_c, a_z=a_z, b_val=b_val, b_gate=b_gate, b_z=b_z,
                cq=cq, ya=ya, rstd=rstd, xhat=xhat, yb2=yb2)


def _even_specs(s, w, ts):
    tile = pl.BlockSpec((ts, 7 * w), lambda i: (i, 0))
    halo = pl.BlockSpec((HALO_B, 7 * w), lambda i: (jnp.maximum(i * (ts // HALO_B) - 1, 0), 0))
    return tile, halo


def _small_specs(shapes, index=lambda i: (0, 0)):
    return [pl.BlockSpec(sh, index) for sh in shapes]


def _even_mixer_fwd(p, w_out, x_res, gain, a_conv, b_conv, bias, ln_g, ln_b, name, comm=None):
    s, d = x_res.shape
    w = p.shape[1] // 7
    ts = _row_tile(s, 128)
    assert ts % HALO_B == 0

    def body(p_ref, halo_ref, wout_ref, x_ref, g_ref, ac_ref, bc_ref, bias_ref, lng_ref, lnb_ref,
             u_ref, xn_ref, y_ref, qbuf, ysh, y1buf):
        first = pl.program_id(0) == 0
        f = _even_forward_tile(p_ref, halo_ref, first, ac_ref, bc_ref, bias_ref, lng_ref, lnb_ref, qbuf, ysh, y1buf,
                               w, ts)
        yb3 = f["yb2"] * _sigmoid(f["yb2"])
        u_a = (f["ya"] * (f["a_z"] * _sigmoid(f["a_z"]))).astype(BF16)
        u_b = (yb3 * (f["b_z"] * _sigmoid(f["b_z"]))).astype(BF16)
        u_ref[:, 0:w] = u_a
        u_ref[:, w:2 * w] = u_b
        y = (jnp.dot(u_a, wout_ref[0:w, :], preferred_element_type=F32)
             + jnp.dot(u_b, wout_ref[w:2 * w, :], preferred_element_type=F32))
        r = lax.rsqrt(_mean_last(y * y) + EPS)
        y_ref[...] = y
        xn_ref[...] = x_ref[...] + (y * r) * g_ref[...]

    tile, halo = _even_specs(s, w, ts)
    row = pl.BlockSpec((ts, d), lambda i: (i, 0))
    return _call(
        body, name=name, grid=(s // ts,),
        in_specs=[tile, halo, pl.BlockSpec((2 * w, d), lambda i: (0, 0)), row, pl.BlockSpec((1, d), lambda i: (0, 0))]
        + _small_specs([(CONV_A, w), (CONV_B, w), (1, w), (1, w), (1, w)]),
        out_specs=[pl.BlockSpec((ts, 2 * w), lambda i: (i, 0)), row, row],
        out_shape=[jax.ShapeDtypeStruct((s, 2 * w), BF16), jax.ShapeDtypeStruct((s, d), F32),
                   jax.ShapeDtypeStruct((s, d), F32)],
        scratch_shapes=[pltpu.VMEM((HALO_A + ts, w), F32), pltpu.VMEM((SHIFTS, HALO_B + ts, w), F32),
                        pltpu.VMEM((ts, w), F32)],
        args=(p, p, w_out, x_res, gain, a_conv, b_conv, bias, ln_g, ln_b), comm=comm)


def _even_mixer_bwd(p, dy, w_out, a_conv, b_conv, bias, ln_g, ln_b, name, comm=None):
    s, d = dy.shape
    w = p.shape[1] // 7
    ts = _row_tile(s, 128)
    nt = s // ts
    assert ts % HALO_B == 0

    def body(p_ref, halo_ref, dy_ref, wout_ref, ac_ref, bc_ref, bias_ref, lng_ref, lnb_ref,
             dp_ref, dac_ref, dbc_ref, dbias_ref, dlng_ref, dlnb_ref,
             qbuf, ysh, y1buf, dqbuf, dsh, dy0buf, wacc, carry_dq, carry_dy):
        step = pl.program_id(0)
        first = step == nt - 1

        @pl.when(step == 0)
        def _():
            for ref in (dac_ref, dbias_ref, dlng_ref, dlnb_ref, wacc, carry_dq, carry_dy):
                ref[...] = jnp.zeros_like(ref)

        f = _even_forward_tile(p_ref, halo_ref, first, ac_ref, bc_ref, bias_ref, lng_ref, lnb_ref, qbuf, ysh, y1buf,
                               w, ts)
        a_z, b_z, yb2 = f["a_z"], f["b_z"], f["yb2"]
        s_az, s_bz, s_y2 = _sigmoid(a_z), _sigmoid(b_z), _sigmoid(yb2)
        nt_dims = (((1,), (1,)), ((), ()))
        du_a = lax.dot_general(dy_ref[...], wout_ref[0:w, :], nt_dims, preferred_element_type=F32)
        du_b = lax.dot_general(dy_ref[...], wout_ref[w:2 * w, :], nt_dims, preferred_element_type=F32)

        d_ya = du_a * (a_z * s_az)
        dp_ref[:, 3 * w:4 * w] = (du_a * f["ya"] * _dsilu(a_z, s_az)).astype(BF16)
        dp_ref[:, 1 * w:2 * w] = (d_ya * f["cq"]).astype(BF16)
        d_cq = d_ya * f["a_b"]
        dqbuf[0:ts, :] = d_cq
        dqbuf[ts:ts + HALO_A, :] = carry_dq[...]
        d_q = jnp.zeros((ts, w), F32)
        for o in range(CONV_A):
            d_q = d_q + ac_ref[CONV_A - 1 - o:CONV_A - o, :] * dqbuf[o:o + ts, :]
        for j in range(CONV_A):
            k = CONV_A - 1 - j
            dac_ref[k:k + 1, :] += _sum_rows(d_cq * qbuf[HALO_A - j:HALO_A - j + ts, :])
        carry_dq[...] = d_cq[0:HALO_A, :]
        dp_ref[:, 2 * w:3 * w] = (d_q * f["a_x"]).astype(BF16)
        dp_ref[:, 0 * w:1 * w] = (d_q * f["a_c"]).astype(BF16)

        d_yb3 = du_b * (b_z * s_bz)
        dp_ref[:, 6 * w:7 * w] = (du_b * (yb2 * s_y2) * _dsilu(b_z, s_bz)).astype(BF16)
        d_yb2 = d_yb3 * _dsilu(yb2, s_y2)
        xhat = f["xhat"]
        dlng_ref[...] += _sum_rows(d_yb2 * xhat)
        dlnb_ref[...] += _sum_rows(d_yb2)
        d_xh = d_yb2 * lng_ref[...]
        d_yb1 = f["rstd"] * (d_xh - _mean_last(d_xh) - xhat * _mean_last(d_xh * xhat))
        dbias_ref[...] += _sum_rows(d_yb1)
        dsh[0, 0:ts, :] = d_yb1
        dsh[0, ts:ts + HALO_B, :] = carry_dy[...]
        carry_dy[...] = d_yb1[0:HALO_B, :]
        _fill_shifted_up(dsh, ts + HALO_B)
        _conv31(dsh, 0, SHIFTS, bc_ref, None, dy0buf, ts, w)
        _conv31_weight_grad(dsh, ysh, wacc, ts, w)

        @pl.when(step == nt - 1)
        def _():
            for k in range(CONV_B):
                dbc_ref[k:k + 1, :] = _sum_rows(wacc[k])

        d_yb0 = dy0buf[...]
        s_g = _sigmoid(f["b_gate"])
        dp_ref[:, 4 * w:5 * w] = (d_yb0 * s_g).astype(BF16)
        dp_ref[:, 5 * w:6 * w] = (d_yb0 * f["b_val"] * s_g * (1.0 - s_g)).astype(BF16)

    rev = lambda i: (nt - 1 - i, 0)
    tile = pl.BlockSpec((ts, 7 * w), rev)
    halo = pl.BlockSpec((HALO_B, 7 * w), lambda i: (jnp.maximum((nt - 1 - i) * (ts // HALO_B) - 1, 0), 0))
    small = [(CONV_A, w), (CONV_B, w), (1, w), (1, w), (1, w)]
    return _call(
        body, name=name, grid=(nt,),
        in_specs=[tile, halo, pl.BlockSpec((ts, d), rev), pl.BlockSpec((2 * w, d), lambda i: (0, 0))]
        + _small_specs(small),
        out_specs=[pl.BlockSpec((ts, 7 * w), rev)] + _small_specs(small),
        out_shape=[jax.ShapeDtypeStruct((s, 7 * w), BF16)] + [jax.ShapeDtypeStruct(sh, F32) for sh in small],
        scratch_shapes=[pltpu.VMEM((HALO_A + ts, w), F32), pltpu.VMEM((SHIFTS, HALO_B + ts, w), F32),
                        pltpu.VMEM((ts, w), F32),
                        pltpu.VMEM((ts + HALO_A, w), F32), pltpu.VMEM((SHIFTS, ts + HALO_B, w), F32),
                        pltpu.VMEM((ts, w), F32), pltpu.VMEM((CONV_B, SHIFTS, w), F32),
                        pltpu.VMEM((HALO_A, w), F32), pltpu.VMEM((HALO_B, w), F32)],
        args=(p, p, dy, w_out, a_conv, b_conv, bias, ln_g, ln_b), comm=comm)


def _pool_forward_tile(p_ref, halo_ref, first, tile_index, cw_ref, cb_ref, cs_ref, vbuf, c, gc, ts):
    vbuf[0:HALO_P, :] = halo_ref[...] * jnp.where(first, 0.0, 1.0)
    vbuf[HALO_P:HALO_P + ts, :] = p_ref[:, 0:c]
    pos = tile_index * ts + lax.broadcasted_iota(jnp.int32, (ts, 1), 0) + 1
    pooled, inv, gout = [], [], []
    for g, win in enumerate(POOL_WINDOWS):
        cols = slice(g * gc, (g + 1) * gc)
        acc = jnp.zeros((ts, gc), F32)
        for j in range(win):
            acc = acc + vbuf[HALO_P - j:HALO_P - j + ts, cols]
        inv_g = 1.0 / jnp.minimum(pos, win).astype(F32)
        pooled_g = (acc * inv_g - p_ref[:, cols]).astype(BF16)
        pooled.append(pooled_g)
        inv.append(inv_g)
        gout.append(jnp.dot(pooled_g, cw_ref[g], preferred_element_type=F32) + cb_ref[:, cols])
    return pooled, inv, gout


def _odd_mixer_fwd(p, cw, cb, cs, name):
    s = p.shape[0]
    c = p.shape[1] // 2
    gc = c // N_GROUPS
    ts = _row_tile(s, 256)

    def body(p_ref, halo_ref, cw_ref, cb_ref, cs_ref, u_ref, vbuf):
        i = pl.program_id(0)
        _, _, gout = _pool_forward_tile(p_ref, halo_ref, i == 0, i, cw_ref, cb_ref, cs_ref, vbuf, c, gc, ts)
        for g in range(N_GROUPS):
            cols = slice(g * gc, (g + 1) * gc)
            z = p_ref[:, c + g * gc:c + (g + 1) * gc]
            u_ref[:, cols] = (gout[g] * cs_ref[:, cols] * (z * _sigmoid(z))).astype(BF16)

    return pl.pallas_call(
        body, name=name, grid=(s // ts,),
        in_specs=[pl.BlockSpec((ts, 2 * c), lambda i: (i, 0)),
                  pl.BlockSpec((HALO_P, c), lambda i: (jnp.maximum(i * (ts // HALO_P) - 1, 0), 0)),
                  pl.BlockSpec((N_GROUPS, gc, gc), lambda i: (0, 0, 0)),
                  pl.BlockSpec((1, c), lambda i: (0, 0)), pl.BlockSpec((1, c), lambda i: (0, 0))],
        out_specs=pl.BlockSpec((ts, c), lambda i: (i, 0)),
        out_shape=jax.ShapeDtypeStruct((s, c), BF16),
        scratch_shapes=[pltpu.VMEM((HALO_P + ts, c), F32)],
        compiler_params=_params("arbitrary"),
    )(p, p, cw, cb, cs)


def _odd_mixer_bwd(p, du, cw, cb, cs, name):
    s = p.shape[0]
    c = p.shape[1] // 2
    gc = c // N_GROUPS
    ts = _row_tile(s, 256)
    nt = s // ts

    def body(p_ref, halo_ref, du_ref, cw_ref, cb_ref, cs_ref, dp_ref, dcw_ref, dcb_ref, dcs_ref, vbuf, ebuf, carry_e):
        step = pl.program_id(0)
        tile_index = nt - 1 - step

        @pl.when(step == 0)
        def _():
            for ref in (dcw_ref, dcb_ref, dcs_ref, carry_e):
                ref[...] = jnp.zeros_like(ref)

        pooled, inv, gout = _pool_forward_tile(p_ref, halo_ref, tile_index == 0, tile_index, cw_ref, cb_ref, cs_ref,
                                               vbuf, c, gc, ts)
        ebuf[ts:ts + HALO_P, :] = carry_e[...]
        for g, win in enumerate(POOL_WINDOWS):
            cols = slice(g * gc, (g + 1) * gc)
            z = p_ref[:, c + g * gc:c + (g + 1) * gc]
            sz = _sigmoid(z)
            du_g = du_ref[:, cols]
            scale = cs_ref[:, cols]
            d_y = du_g * (z * sz)
            dp_ref[:, c + g * gc:c + (g + 1) * gc] = (du_g * (gout[g] * scale) * _dsilu(z, sz)).astype(BF16)
            dcs_ref[:, cols] += _sum_rows(d_y * gout[g])
            d_gout = d_y * scale
            dcb_ref[:, cols] += _sum_rows(d_gout)
            d_gout_b = d_gout.astype(BF16)
            dcw_ref[g] += lax.dot_general(pooled[g], d_gout_b, (((0,), (0,)), ((), ())), preferred_element_type=F32)
            d_pool = lax.dot_general(d_gout_b, cw_ref[g], (((1,), (1,)), ((), ())), preferred_element_type=F32)
            e = d_pool * inv[g]
            ebuf[0:ts, cols] = e
            d_v = -d_pool
            for o in range(win):
                d_v = d_v + ebuf[o:o + ts, cols]
            dp_ref[:, cols] = d_v.astype(BF16)
            carry_e[:, cols] = e[0:HALO_P, :]

    rev = lambda i: (nt - 1 - i, 0)
    small = [(N_GROUPS, gc, gc), (1, c), (1, c)]
    return pl.pallas_call(
        body, name=name, grid=(nt,),
        in_specs=[pl.BlockSpec((ts, 2 * c), rev),
                  pl.BlockSpec((HALO_P, c), lambda i: (jnp.maximum((nt - 1 - i) * (ts // HALO_P) - 1, 0), 0)),
                  pl.BlockSpec((ts, c), rev),
                  pl.BlockSpec((N_GROUPS, gc, gc), lambda i: (0, 0, 0)),
                  pl.BlockSpec((1, c), lambda i: (0, 0)), pl.BlockSpec((1, c), lambda i: (0, 0))],
        out_specs=[pl.BlockSpec((ts, 2 * c), rev),
                   pl.BlockSpec((N_GROUPS, gc, gc), lambda i: (0, 0, 0)),
                   pl.BlockSpec((1, c), lambda i: (0, 0)), pl.BlockSpec((1, c), lambda i: (0, 0))],
        out_shape=[jax.ShapeDtypeStruct((s, 2 * c), BF16)] + [jax.ShapeDtypeStruct(sh, F32) for sh in small],
        scratch_shapes=[pltpu.VMEM((HALO_P + ts, c), F32), pltpu.VMEM((ts + HALO_P, c), F32),
                        pltpu.VMEM((HALO_P, c), F32)],
        compiler_params=_params("arbitrary"),
    )(p, p, du, cw, cb, cs)


def _cast_into_slot(a, coords, name):
    r, cols = a.shape
    tr = _row_tile(r // 2, 256)
    per = r // 2 // tr

    def body(co_ref, a_ref, o_ref):
        o_ref[0, 0] = a_ref[...].astype(BF16)

    return pl.pallas_call(
        body, name=name,
        grid_spec=pltpu.PrefetchScalarGridSpec(
            num_scalar_prefetch=1, grid=(2, per),
            in_specs=[pl.BlockSpec((tr, cols), lambda h, i, co: (h * per + i, 0))],
            out_specs=pl.BlockSpec((1, 1, tr, cols), lambda h, i, co: (co[0], h, i, 0))),
        out_shape=jax.ShapeDtypeStruct((N_SHARDS, 2, r // 2, cols), BF16),
        compiler_params=_params("arbitrary", "arbitrary"),
    )(coords, a)


def _chip_sum(g, other, coords, name):
    n_sh, _, r2, cols = g.shape
    tr = _row_tile(r2, 256)

    def body(co_ref, g_ref, o_ref, sum_ref, mine_ref):
        v = (g_ref[0, 0].astype(F32) + o_ref[0].astype(F32)).astype(BF16)
        sum_ref[0] = v

        @pl.when(pl.program_id(1) == co_ref[0])
        def _():
            mine_ref[0] = v

    piece = pl.BlockSpec((1, tr, cols), lambda i, s, co: (s, i, 0))
    return pl.pallas_call(
        body, name=name,
        grid_spec=pltpu.PrefetchScalarGridSpec(
            num_scalar_prefetch=1, grid=(r2 // tr, n_sh),
            in_specs=[pl.BlockSpec((1, 1, tr, cols), lambda i, s, co: (s, co[1], i, 0)), piece],
            out_specs=[piece, pl.BlockSpec((1, tr, cols), lambda i, s, co: (co[0], i, 0))]),
        out_shape=[jax.ShapeDtypeStruct((n_sh, r2, cols), BF16)] * 2,
        compiler_params=_params("arbitrary", "arbitrary"),
    )(coords, g, other)


def _shard_sum(pieces, coords, name):
    n_sh, r2, cols = pieces.shape
    tr = _row_tile(r2, 256)

    def body(co_ref, p_ref, o_ref):
        acc = p_ref[0].astype(F32)
        for k in range(1, n_sh):
            acc = acc + p_ref[k].astype(F32)
        o_ref[0] = acc

    return pl.pallas_call(
        body, name=name,
        grid_spec=pltpu.PrefetchScalarGridSpec(
            num_scalar_prefetch=1, grid=(r2 // tr,),
            in_specs=[pl.BlockSpec((n_sh, tr, cols), lambda i, co: (0, i, 0))],
            out_specs=pl.BlockSpec((1, tr, cols), lambda i, co: (co[1], i, 0))),
        out_shape=jax.ShapeDtypeStruct((2, r2, cols), F32),
        compiler_params=_params("arbitrary"),
    )(coords, pieces)


def _sum_small(a, name):
    n, r, cols = a.shape

    def body(a_ref, o_ref):
        acc = a_ref[0]
        for k in range(1, n):
            acc = acc + a_ref[k]
        o_ref[...] = acc

    return pl.pallas_call(
        body, name=name,
        in_specs=[pl.BlockSpec((n, r, cols), lambda: (0, 0, 0))],
        out_specs=pl.BlockSpec((r, cols), lambda: (0, 0)),
        out_shape=jax.ShapeDtypeStruct((r, cols), F32),
        compiler_params=_params(),
    )(a)


def _adamw(w, g, m, v, name):
    r, cols = w.shape
    tr = _row_tile(r, 256) if r % SUBLANES_BF16 == 0 else r

    def body(w_ref, g_ref, m_ref, v_ref, d_ref, nm_ref, nv_ref):
        g = g_ref[...]
        m = ADAM_B1 * m_ref[...] + (1.0 - ADAM_B1) * g
        v = ADAM_B2 * v_ref[...] + (1.0 - ADAM_B2) * (g * g)
        m_hat = m / (1.0 - ADAM_B1 ** ADAM_STEP)
        v_hat = v / (1.0 - ADAM_B2 ** ADAM_STEP)
        d_ref[...] = -ADAM_LR * (m_hat / (jnp.sqrt(v_hat) + ADAM_EPS) + ADAM_WD * w_ref[...])
        nm_ref[...] = m
        nv_ref[...] = v

    blk = pl.BlockSpec((tr, cols), lambda i: (i, 0))
    return pl.pallas_call(
        body, name=name, grid=(r // tr,),
        in_specs=[blk] * 4, out_specs=[blk] * 3,
        out_shape=[jax.ShapeDtypeStruct((r, cols), F32)] * 3,
        compiler_params=_params("arbitrary"),
    )(w, g, m, v)


def _place():
    x, y, c = lax.axis_index("x"), lax.axis_index("y"), lax.axis_index("c")
    other_chips = [(1 - x, y), (x, 1 - y), (1 - x, 1 - y)]
    return x, y, c, other_chips


def _chip(xy):
    return 2 * xy[0] + xy[1]


def _remote(src, dst, send_sem, recv_sem, to):
    return pltpu.make_async_remote_copy(src_ref=src, dst_ref=dst, send_sem=send_sem, recv_sem=recv_sem,
                                        device_id=to, device_id_type=MESH)


def _gather_ici(ctx, k, j, start):
    (x, y, c, chips), b, send, recv = ctx
    blk = b[k].at[_chip((x, y)) if start else _chip(chips[j]), c]
    return _remote(blk, blk, send.at[6 * k + j], recv.at[6 * k + j], (*chips[j], c))


def _gather_d2d(ctx, k, j, start):
    (x, y, c, chips), b, send, recv = ctx
    blk = b[k].at[_chip(chips[j]), c if start else 1 - c]
    return _remote(blk, blk, send.at[6 * k + 3 + j], recv.at[6 * k + 3 + j], (x, y, 1 - c))


def _gather_small(ctx, n, j, start):
    (x, y, c, chips), b, send, recv = ctx
    blk = b[n].at[_chip((x, y)) if start else _chip(chips[j])]
    return _remote(blk, blk, send.at[6 * n + j], recv.at[6 * n + j], (*chips[j], c))


def _gather_comm(bufs, small, forward_at):
    n = len(bufs)
    ici, d2d = _gather_ici, _gather_d2d

    def small_copy(ctx, j, start):
        return _gather_small(ctx, n, j, start)

    def start(srcs, b, send, recv):
        ctx = (_place(), b, send, recv)
        for k in range(n):
            for j in range(3):
                ici(ctx, k, j, True).start()
        if small is not None:
            for j in range(3):
                small_copy(ctx, j, True).start()

    def forward(srcs, b, send, recv):
        ctx = (_place(), b, send, recv)
        for k in range(n):
            for j in range(3):
                ici(ctx, k, j, False).wait_recv()
                d2d(ctx, k, j, True).start()

    def finish(srcs, b, send, recv):
        ctx = (_place(), b, send, recv)
        if small is not None:
            for j in range(3):
                small_copy(ctx, j, False).wait_recv()
                small_copy(ctx, j, True).wait_send()
        for k in range(n):
            for j in range(3):
                d2d(ctx, k, j, False).wait_recv()
                ici(ctx, k, j, True).wait_send()
                d2d(ctx, k, j, True).wait_send()

    all_bufs = list(bufs) + ([small] if small is not None else [])
    return _Comm([], all_bufs, 6 * n + 3, [(0, start), (forward_at, forward)], finish)


def _rmsnorm(x, gain, name):
    s, d = x.shape
    tm = _row_tile(s, 512)

    def body(x_ref, g_ref, h_ref):
        xv = x_ref[...]
        r = lax.rsqrt(_mean_last(xv * xv) + EPS)
        h_ref[...] = (xv * r * g_ref[...]).astype(BF16)

    return pl.pallas_call(
        body, name=name, grid=(s // tm,),
        in_specs=[pl.BlockSpec((tm, d), lambda i: (i, 0)), pl.BlockSpec((1, d), lambda i: (0, 0))],
        out_specs=pl.BlockSpec((tm, d), lambda i: (i, 0)),
        out_shape=jax.ShapeDtypeStruct((s, d), BF16),
        compiler_params=_params("arbitrary"),
    )(x, gain)


def _gathered_in_proj(h, bufs, small, order, name):
    s, d = h.shape
    n_sh, _, r2, ns = bufs[0].shape
    assert d == 2 * r2
    n = len(bufs)
    tm = _row_tile(s, 512)
    n_i = s // tm
    hook_i = max(n_i - 2, 0)
    n_sems = 6 * n + 3

    def body(order_ref, h_ref, *rest):
        p_ref = rest[n + 1]
        b = rest[n + 2:2 * n + 3]
        w_vmem, w_sems, send, recv = rest[2 * n + 3:]
        j, i = pl.program_id(0), pl.program_id(1)
        ctx = (_place(), b, send, recv)

        def fetch(q):
            return pltpu.make_async_copy(b[0].at[order_ref[q]], w_vmem.at[q % 2], w_sems.at[q % 2])

        @pl.when((j == 0) & (i == 0))
        def _():
            for k in range(n):
                for peer in range(3):
                    _gather_ici(ctx, k, peer, True).start()
            for peer in range(3):
                _gather_small(ctx, n, peer, True).start()
            fetch(0).start()
            fetch(0).wait()

        for q in range(1, n_sh):
            @pl.when((j == q - 1) & (i == hook_i))
            def _(q=q):
                _gather_ici(ctx, 0, q - 1, False).wait_recv()
                _gather_d2d(ctx, 0, q - 1, True).start()
                _gather_d2d(ctx, 0, q - 1, False).wait_recv()
                fetch(q).start()
                if q == n_sh - 1:
                    for k in range(1, n):
                        for peer in range(3):
                            _gather_ici(ctx, k, peer, False).wait_recv()
                            _gather_d2d(ctx, k, peer, True).start()

            @pl.when((j == q) & (i == 0))
            def _(q=q):
                fetch(q).wait()

        wv = w_vmem.at[j % 2]
        p_ref[...] = (jnp.dot(h_ref[:, 0:r2], wv[0], preferred_element_type=F32)
                      + jnp.dot(h_ref[:, r2:d], wv[1], preferred_element_type=F32))

        @pl.when((j == n_sh - 1) & (i == n_i - 1))
        def _():
            for peer in range(3):
                _gather_small(ctx, n, peer, False).wait_recv()
                _gather_small(ctx, n, peer, True).wait_send()
            for k in range(n):
                for peer in range(3):
                    if k > 0:
                        _gather_d2d(ctx, k, peer, False).wait_recv()
                    _gather_ici(ctx, k, peer, True).wait_send()
                    _gather_d2d(ctx, k, peer, True).wait_send()

    all_bufs = list(bufs) + [small]
    out = pl.pallas_call(
        body, name=name,
        grid_spec=pltpu.PrefetchScalarGridSpec(
            num_scalar_prefetch=1, grid=(n_sh, n_i),
            in_specs=[pl.BlockSpec((tm, d), lambda j, i, o: (i, 0))] + [ANY] * (n + 1),
            out_specs=[pl.BlockSpec((tm, ns), lambda j, i, o: (i, o[j]))] + [ANY] * (n + 1),
            scratch_shapes=[pltpu.VMEM((2, 2, r2, ns), BF16), pltpu.SemaphoreType.DMA((2,)),
                            pltpu.SemaphoreType.DMA((n_sems,)), pltpu.SemaphoreType.DMA((n_sems,))]),
        out_shape=[jax.ShapeDtypeStruct((s, n_sh * ns), F32)]
        + [jax.ShapeDtypeStruct(a.shape, a.dtype) for a in all_bufs],
        input_output_aliases={2 + t: 1 + t for t in range(n + 1)},
        compiler_params=_params("arbitrary", "arbitrary"),
    )(order, h, *all_bufs)
    return out[0], list(out[1:])


def _exchange_core_halves(grads, name):
    n = len(grads)

    def body(*refs):
        ins, outs = refs[:n], refs[n:2 * n]
        send_sems, recv_sems = refs[2 * n:]
        x, y, c, _ = _place()
        copies = [_remote(ins[k].at[s, 1 - c], outs[k].at[s], send_sems.at[N_SHARDS * k + s],
                          recv_sems.at[N_SHARDS * k + s], (x, y, 1 - c))
                  for k in range(n) for s in range(N_SHARDS)]
        for cp in copies:
            cp.start()
        for cp in copies:
            cp.wait()

    return pl.pallas_call(
        body, name=name, in_specs=[ANY] * n, out_specs=[ANY] * n,
        out_shape=[jax.ShapeDtypeStruct((N_SHARDS,) + a.shape[2:], a.dtype) for a in grads],
        scratch_shapes=[pltpu.SemaphoreType.DMA((N_SHARDS * n,))] * 2,
    )(*grads)


def _scatter_comm(chip_sums, landing):
    n = len(chip_sums)

    def big(srcs, b, send, recv, k, j, start):
        x, y, c, chips = _place()
        dst = b[k].at[_chip((x, y)) if start else _chip(chips[j])]
        return _remote(srcs[k].at[_chip(chips[j])], dst, send.at[3 * k + j], recv.at[3 * k + j], (*chips[j], c))

    def start(srcs, b, send, recv):
        for k in range(n):
            for j in range(3):
                big(srcs, b, send, recv, k, j, True).start()

    def finish(srcs, b, send, recv):
        for k in range(n):
            for j in range(3):
                big(srcs, b, send, recv, k, j, False).wait_recv()
                big(srcs, b, send, recv, k, j, True).wait_send()

    return _Comm(chip_sums, landing, 3 * n, [(0, start)], finish)


def _join_comm(halves, small):
    n = len(halves)
    flips = [(fx, fy, fc) for fx in (0, 1) for fy in (0, 1) for fc in (0, 1)][1:]

    def half(b, send, recv, k, start):
        x, y, c, _ = _place()
        return _remote(b[k].at[c], b[k].at[c if start else 1 - c], send.at[k], recv.at[k], (x, y, 1 - c))

    def small_copy(b, send, recv, q, start):
        x, y, c, _ = _place()
        px, py, pc = x ^ flips[q][0], y ^ flips[q][1], c ^ flips[q][2]
        blk = b[n].at[4 * x + 2 * y + c if start else 4 * px + 2 * py + pc]
        return _remote(blk, blk, send.at[n + q], recv.at[n + q], (px, py, pc))

    def start(srcs, b, send, recv):
        for q in range(len(flips)):
            small_copy(b, send, recv, q, True).start()
        for k in range(n):
            half(b, send, recv, k, True).start()

    def finish(srcs, b, send, recv):
        for q in range(len(flips)):
            small_copy(b, send, recv, q, False).wait()
        for k in range(n):
            half(b, send, recv, k, False).wait()

    return _Comm([], list(halves) + [small], n + len(flips), [(0, start)], finish)


def _flat_rows(parts):
    flat = jnp.concatenate([p.reshape(-1) for p in parts])
    assert flat.shape[0] % LANES == 0
    return flat.reshape(-1, LANES)


def _unflatten(flat, shapes):
    out, at = [], 0
    for sh in shapes:
        size = 1
        for dim in sh:
            size *= dim
        out.append(flat[at:at + size].reshape(sh))
        at += size
    assert at == flat.shape[0], (at, flat.shape)
    return out


def _col_shards_to_full(a, rows):
    q = a.shape[1] // rows
    return a.reshape(N_SHARDS, rows, q).transpose(1, 0, 2).reshape(rows, N_SHARDS * q)


def _my_col_shard(full, chip):
    rows, cols = full.shape
    q = cols // N_SHARDS
    return lax.dynamic_index_in_dim(full.reshape(rows, N_SHARDS, q), chip, axis=1, keepdims=False)


def kernel(x, e_norm_pre, e_norm_post, e_w_in, e_a_conv, e_b_conv, e_b_conv_bias, e_b_ln_g, e_b_ln_b, e_w_out, o_norm_pre, o_norm_post, o_w_in, o_c_w, o_c_b, o_c_scale, o_w_out, loss_target, m_e_norm_pre, m_e_norm_post, m_e_w_in, m_e_a_conv, m_e_b_conv, m_e_b_conv_bias, m_e_b_ln_g, m_e_b_ln_b, m_e_w_out, m_o_norm_pre, m_o_norm_post, m_o_w_in, m_o_c_w, m_o_c_b, m_o_c_scale, m_o_w_out, v_e_norm_pre, v_e_norm_post, v_e_w_in, v_e_a_conv, v_e_b_conv, v_e_b_conv_bias, v_e_b_ln_g, v_e_b_ln_b, v_e_w_out, v_o_norm_pre, v_o_norm_post, v_o_w_in, v_o_c_w, v_o_c_b, v_o_c_scale, v_o_w_out):
    _, s, d = x.shape
    w = d // 2
    c = d
    gc = c // N_GROUPS
    wq, cq, gq = w // N_SHARDS, c // N_SHARDS, gc // N_SHARDS
    chip = 2 * lax.axis_index("x") + lax.axis_index("y")
    core = lax.axis_index("c")
    x2 = x.reshape(s, d)
    target = loss_target.reshape(s, d)

    big_w = [e_w_in[0], e_w_out[0], o_w_in[0], o_c_w[0].reshape(N_GROUPS * gq, gc), o_w_out[0]]
    big_m = [m_e_w_in[0], m_e_w_out[0], m_o_w_in[0], m_o_c_w[0].reshape(N_GROUPS * gq, gc), m_o_w_out[0]]
    big_v = [v_e_w_in[0], v_e_w_out[0], v_o_w_in[0], v_o_c_w[0].reshape(N_GROUPS * gq, gc), v_o_w_out[0]]
    coords = jnp.stack([chip, core]).astype(jnp.int32)
    slots = [_cast_into_slot(a, coords, "cast_%d" % k) for k, a in enumerate(big_w)]
    sharded_small = _flat_rows([e_a_conv[0], e_b_conv[0], o_norm_pre, o_norm_post, o_c_scale, o_c_b[0]])
    small_slots = lax.dynamic_update_index_in_dim(jnp.zeros((N_SHARDS,) + sharded_small.shape, F32), sharded_small,
                                                  chip, 0)
    xi, yi = lax.axis_index("x"), lax.axis_index("y")
    order = jnp.stack([chip, 2 * (1 - xi) + yi, 2 * xi + (1 - yi), 2 * (1 - xi) + (1 - yi)]).astype(jnp.int32)
    h0 = _rmsnorm(x2, e_norm_pre, "e_pre_norm")
    p0, (e_w_in_g, e_w_out_g, small_g4) = _gathered_in_proj(h0, slots[:2], small_slots, order, "e_in_proj")
    e_w_in_sm = e_w_in_g.reshape((N_SHARDS,) + big_w[0].shape)
    e_w_out_f = e_w_out_g.reshape(w + w, d)
    sm = small_g4.reshape(N_SHARDS, -1)
    at = [0]

    def take(rows, q):
        blk = sm[:, at[0]:at[0] + rows * q]
        at[0] += rows * q
        return _col_shards_to_full(blk, rows)

    a_conv_f = take(CONV_A, wq)
    b_conv_f = take(CONV_B, wq)
    o_pre_f = take(1, cq)
    o_post_f = take(1, cq)
    cs_f = take(1, cq)
    cb_f = take(N_GROUPS, gq).reshape(1, c)

    mixer_steps = s // _row_tile(s, 128)
    (u0, x1, y0), odd_g = _even_mixer_fwd(p0, e_w_out_f, x2, e_norm_post, a_conv_f, b_conv_f, e_b_conv_bias, e_b_ln_g,
                                          e_b_ln_b, "e_mixer_out_proj",
                                          comm=_gather_comm(slots[2:], None, max(mixer_steps - 4, 0)))
    o_w_in_sm = odd_g[0].reshape((N_SHARDS,) + big_w[2].shape)
    cw_f = odd_g[1].reshape(N_SHARDS, N_GROUPS, gq, gc).transpose(1, 0, 2, 3).reshape(N_GROUPS, gc, gc)
    o_w_out_f = odd_g[2].reshape(c, d)
    (p1, h1), _ = _norm_matmul(x1, o_pre_f, o_w_in_sm, "o_in_proj")
    u1 = _odd_mixer_fwd(p1, cw_f, cb_f, cs_f, "o_mixer_fwd")
    d_y1, d_x2, d_o_post, loss_part = _matmul_post_loss(u1, o_w_out_f, x1, o_post_f, target, "o_out_proj_loss")

    def as_pieces(g, k):
        return g.reshape(N_SHARDS, 2, big_w[k].shape[0] // 2, big_w[k].shape[1])

    def chip_sums(ks, grads, name):
        pieces = [as_pieces(g, k) for k, g in zip(ks, grads)]
        from_sibling = _exchange_core_halves(pieces, name)
        both = [_chip_sum(g, o, coords, "chip_sum_%d" % k) for k, g, o in zip(ks, pieces, from_sibling)]
        return [b[0] for b in both], [b[1] for b in both]

    g_o_w_out = _matmul_tn(u1, d_y1, 1, "o_w_out_grad")
    d_u1, _ = _matmul_nt(d_y1, o_w_out_f[None], "o_out_proj_bwd")
    d_p1, d_cw, d_cb, d_cs = _odd_mixer_bwd(p1, d_u1, cw_f, cb_f, cs_f, "o_mixer_bwd")
    g_o_w_in = _matmul_tn(h1, d_p1, N_SHARDS, "o_w_in_grad")
    d_h1, _ = _matmul_nt(d_p1, o_w_in_sm, "o_in_proj_bwd")
    d_x1, d_o_pre, d_y0, d_e_post = _norm_bwd(d_h1, x1, o_pre_f, d_x2, "o_pre_norm_bwd", post=(y0, e_norm_post))

    g_e_w_out = _matmul_tn(u0, d_y0, 1, "e_w_out_grad")
    g_cw = d_cw.reshape(N_GROUPS, N_SHARDS, gq, gc).transpose(1, 0, 2, 3).astype(BF16)
    sums_a, landing_a = chip_sums([1, 2, 3, 4], [g_e_w_out, g_o_w_in, g_cw, g_o_w_out], "exchange_core_halves_a")
    (d_p0, d_a_conv, d_b_conv, d_bias, d_ln_g, d_ln_b), landed_a = _even_mixer_bwd(
        p0, d_y0, e_w_out_f, a_conv_f, b_conv_f, e_b_conv_bias, e_b_ln_g, e_b_ln_b, "e_mixer_bwd",
        comm=_scatter_comm(sums_a, landing_a))
    g_e_w_in = _matmul_tn(h0, d_p0, N_SHARDS, "e_w_in_grad")
    sums_b, landing_b = chip_sums([0], [g_e_w_in], "exchange_core_halves_b")
    d_h0, landed_b = _matmul_nt(d_p0, e_w_in_sm, "e_in_proj_bwd", comm=_scatter_comm(sums_b, landing_b))
    grad_x, d_e_pre = _norm_bwd(d_h0, x2, e_norm_pre, d_x1, "e_pre_norm_bwd")

    landed = landed_b + landed_a
    reduced = [_shard_sum(sc, coords, "shard_sum_%d" % k) for k, sc in enumerate(landed)]
    small_parts = _flat_rows([loss_part[0], d_e_pre, d_e_post, d_bias, d_ln_g, d_ln_b, d_a_conv, d_b_conv,
                              d_o_pre, d_o_post, d_cs, d_cb])
    small_rows = lax.dynamic_update_index_in_dim(jnp.zeros((N_DEVICES,) + small_parts.shape, F32), small_parts,
                                                 2 * chip + core, 0)
    joined = _comm_only(_join_comm(reduced, small_rows), "join_core_halves")
    big_g = [j.reshape(a.shape) for j, a in zip(joined[:5], big_w)]
    small_sum = _sum_small(joined[5], "small_sum").reshape(-1)
    (loss_row, g_e_pre, g_e_post, g_bias, g_ln_g, g_ln_b, g_a_conv_f, g_b_conv_f, g_o_pre_f, g_o_post_f, g_cs_f,
     g_cb_f) = _unflatten(small_sum, [(LANES,), (1, d), (1, d), (1, w), (1, w), (1, w), (CONV_A, w), (CONV_B, w),
                                      (1, c), (1, c), (1, c), (1, c)])
    loss = loss_row[0]
    g_a_conv = _my_col_shard(g_a_conv_f, chip)
    g_b_conv = _my_col_shard(g_b_conv_f, chip)
    g_o_pre = _my_col_shard(g_o_pre_f, chip)
    g_o_post = _my_col_shard(g_o_post_f, chip)
    g_cs = _my_col_shard(g_cs_f, chip)
    g_cb = _my_col_shard(g_cb_f.reshape(N_GROUPS, gc), chip)

    big_upd = [_adamw(wt, g, m, v, "adamw_%d" % k) for k, (wt, g, m, v) in enumerate(zip(big_w, big_g, big_m, big_v))]
    small_w = [e_norm_pre, e_norm_post, e_b_conv_bias, e_b_ln_g, e_b_ln_b, e_a_conv[0], e_b_conv[0],
               o_norm_pre, o_norm_post, o_c_b[0], o_c_scale]
    small_m = [m_e_norm_pre, m_e_norm_post, m_e_b_conv_bias, m_e_b_ln_g, m_e_b_ln_b, m_e_a_conv[0], m_e_b_conv[0],
               m_o_norm_pre, m_o_norm_post, m_o_c_b[0], m_o_c_scale]
    small_v = [v_e_norm_pre, v_e_norm_post, v_e_b_conv_bias, v_e_b_ln_g, v_e_b_ln_b, v_e_a_conv[0], v_e_b_conv[0],
               v_o_norm_pre, v_o_norm_post, v_o_c_b[0], v_o_c_scale]
    small_g = [g_e_pre, g_e_post, g_bias, g_ln_g, g_ln_b, g_a_conv, g_b_conv, g_o_pre, g_o_post, g_cb, g_cs]
    small_shapes = [a.shape for a in small_w]
    small_upd = _adamw(_flat_rows(small_w), _flat_rows(small_g), _flat_rows(small_m), _flat_rows(small_v),
                       "adamw_small")
    small_delta, small_new_m, small_new_v = [_unflatten(u.reshape(-1), small_shapes) for u in small_upd]

    def ordered(small, big, lead):
        (n_pre, n_post, bias, ln_g, ln_b, a_conv, b_conv, o_pre, o_post, cb, cs) = small
        (w_in, w_out, ow_in, cw, ow_out) = big
        out = [n_pre, n_post, w_in[None], a_conv[None], b_conv[None], bias, ln_g, ln_b, w_out[None], o_pre, o_post,
               ow_in[None], cw.reshape(1, N_GROUPS, gq, gc), cb[None], cs, ow_out[None]]
        return out

    grads = ordered(small_g, big_g, None)
    deltas = ordered(small_delta, [u[0] for u in big_upd], None)
    new_m = ordered(small_new_m, [u[1] for u in big_upd], None)
    new_v = ordered(small_new_v, [u[2] for u in big_upd], None)
    return (loss, grad_x.reshape(1, s, d), *grads, *deltas, *new_m, *new_v)
```

```python
import functools

import jax
import jax.numpy as jnp
from jax import lax
from jax.experimental import pallas as pl
from jax.experimental.pallas import tpu as pltpu

F32 = jnp.float32
BF16 = jnp.bfloat16
MESH = pl.DeviceIdType.MESH

EPS = 1e-6
CONV_A = 3
CONV_B = 31
POOL_WINDOWS = (2, 4, 8, 16)
N_GROUPS = len(POOL_WINDOWS)
N_SHARDS = 4
N_DEVICES = 8
ADAM_LR = 0.001
ADAM_B1 = 0.9
ADAM_B2 = 0.999
ADAM_EPS = 1e-08
ADAM_WD = 0.01
ADAM_STEP = 10

LANES = 128
SUBLANES_BF16 = 16
HALO_A = 8
HALO_B = 32
HALO_P = 16
SHIFTS = 8
ROW_BLOCK = 32
LANE_BLOCK = 256
VMEM_LIMIT = 56 * 1024 * 1024
TN_ACC_BYTES = 8 * 1024 * 1024
EVEN_BWD_ROWS = 128


def _row_tile(n, pref):
    t = max(min(n, pref) // SUBLANES_BF16, 1) * SUBLANES_BF16
    while t > SUBLANES_BF16 and (n % t or t % SUBLANES_BF16):
        t -= SUBLANES_BF16
    assert n % t == 0, (n, pref)
    return t


def _col_chunk(n, pref):
    t = (min(n, pref) // LANES) * LANES
    while t > LANES and n % t:
        t -= LANES
    assert t >= LANES and n % t == 0, (n, pref)
    return t


def _params(*sem):
    return pltpu.CompilerParams(dimension_semantics=tuple(sem) if sem else None, vmem_limit_bytes=VMEM_LIMIT)


ANY = pl.BlockSpec(memory_space=pl.ANY)


class _Comm:
    def __init__(self, srcs, bufs, n_sems, phases, finish):
        self.srcs, self.bufs, self.n_sems, self.phases, self.finish = list(srcs), list(bufs), n_sems, phases, finish


def _call(body, *, name, grid, in_specs, out_specs, out_shape, args, scratch_shapes=(), comm=None):
    params = _params(*(("arbitrary",) * len(grid)))
    if comm is None:
        out = pl.pallas_call(body, name=name, grid=grid, in_specs=in_specs, out_specs=out_specs, out_shape=out_shape,
                             scratch_shapes=scratch_shapes, compiler_params=params)(*args)
        return list(out), []
    n_in, n_out, n_scr = len(in_specs), len(out_specs), len(scratch_shapes)
    ns, nb = len(comm.srcs), len(comm.bufs)
    total = 1
    for size in grid:
        total *= size

    def fused(*refs):
        ins, srcs = refs[:n_in], refs[n_in:n_in + ns]
        at = n_in + ns + nb
        outs, bufs = refs[at:at + n_out], refs[at + n_out:at + n_out + nb]
        scratch = refs[at + n_out + nb:at + n_out + nb + n_scr]
        send_sems, recv_sems = refs[-2:]
        step = 0
        for axis, size in enumerate(grid):
            step = step * size + pl.program_id(axis)
        for when, fn in comm.phases:
            pl.when(step == when)(functools.partial(fn, srcs, bufs, send_sems, recv_sems))
        body(*ins, *outs, *scratch)
        pl.when(step == total - 1)(functools.partial(comm.finish, srcs, bufs, send_sems, recv_sems))

    out = pl.pallas_call(
        fused, name=name, grid=grid,
        in_specs=list(in_specs) + [ANY] * (ns + nb), out_specs=list(out_specs) + [ANY] * nb,
        out_shape=list(out_shape) + [jax.ShapeDtypeStruct(b.shape, b.dtype) for b in comm.bufs],
        input_output_aliases={n_in + ns + i: n_out + i for i in range(nb)},
        scratch_shapes=list(scratch_shapes) + [pltpu.SemaphoreType.DMA((comm.n_sems,))] * 2,
        compiler_params=params,
    )(*args, *comm.srcs, *comm.bufs)
    return list(out[:n_out]), list(out[n_out:])


def _comm_only(comm, name):
    ns, nb = len(comm.srcs), len(comm.bufs)

    def body(*refs):
        srcs, bufs = refs[:ns], refs[ns + nb:ns + 2 * nb]
        send_sems, recv_sems = refs[-2:]
        for _, fn in comm.phases:
            fn(srcs, bufs, send_sems, recv_sems)
        comm.finish(srcs, bufs, send_sems, recv_sems)

    return pl.pallas_call(
        body, name=name, in_specs=[ANY] * (ns + nb), out_specs=[ANY] * nb,
        out_shape=[jax.ShapeDtypeStruct(b.shape, b.dtype) for b in comm.bufs],
        input_output_aliases={ns + i: i for i in range(nb)},
        scratch_shapes=[pltpu.SemaphoreType.DMA((comm.n_sems,))] * 2,
    )(*comm.srcs, *comm.bufs)


def _sigmoid(v):
    return jax.nn.sigmoid(v)


def _dsilu(v, s):
    return s * (1.0 + v * (1.0 - s))


def _mean_last(v):
    return jnp.mean(v, axis=-1, keepdims=True)


def _sum_rows(v):
    return jnp.sum(v, axis=0, keepdims=True)


def _norm_matmul(x, gain, w_sm, name, comm=None):
    s, d = x.shape
    n_sh, _, ns = w_sm.shape
    tm = _row_tile(s, 1024)

    def body(x_ref, g_ref, w_ref, p_ref, h_ref):
        @pl.when(pl.program_id(1) == 0)
        def _():
            xv = x_ref[...]
            r = lax.rsqrt(_mean_last(xv * xv) + EPS)
            h_ref[...] = (xv * r * g_ref[...]).astype(BF16)

        p_ref[...] = jnp.dot(h_ref[...], w_ref[0], preferred_element_type=F32)

    return _call(
        body, name=name, grid=(s // tm, n_sh),
        in_specs=[pl.BlockSpec((tm, d), lambda i, j: (i, 0)),
                  pl.BlockSpec((1, d), lambda i, j: (0, 0)),
                  pl.BlockSpec((1, d, ns), lambda i, j: (j, 0, 0))],
        out_specs=[pl.BlockSpec((tm, ns), lambda i, j: (i, j)),
                   pl.BlockSpec((tm, d), lambda i, j: (i, 0))],
        out_shape=[jax.ShapeDtypeStruct((s, n_sh * ns), F32), jax.ShapeDtypeStruct((s, d), BF16)],
        args=(x, gain, w_sm), comm=comm)


def _matmul_post_loss(u, w, x_res, gain, target, name):
    s, k = u.shape
    d = w.shape[1]
    tm = _row_tile(s, 256)

    def body(u_ref, w_ref, x_ref, g_ref, t_ref, dy_ref, dout_ref, dg_ref, loss_ref):
        @pl.when(pl.program_id(0) == 0)
        def _():
            dg_ref[...] = jnp.zeros_like(dg_ref)
            loss_ref[...] = jnp.zeros_like(loss_ref)

        y = jnp.dot(u_ref[...], w_ref[...], preferred_element_type=F32)
        r = lax.rsqrt(_mean_last(y * y) + EPS)
        n = y * r
        g = g_ref[...]
        err = x_ref[...] + n * g - t_ref[...]
        loss_ref[...] += 0.5 * jnp.sum(_mean_last(err * err))
        dout = err * (1.0 / d)
        dout_ref[...] = dout
        dg_ref[...] += _sum_rows(dout * n)
        dn = dout * g
        dy_ref[...] = (r * (dn - n * _mean_last(dn * n))).astype(BF16)

    return pl.pallas_call(
        body, name=name, grid=(s // tm,),
        in_specs=[pl.BlockSpec((tm, k), lambda i: (i, 0)),
                  pl.BlockSpec((k, d), lambda i: (0, 0)),
                  pl.BlockSpec((tm, d), lambda i: (i, 0)),
                  pl.BlockSpec((1, d), lambda i: (0, 0)),
                  pl.BlockSpec((tm, d), lambda i: (i, 0))],
        out_specs=[pl.BlockSpec((tm, d), lambda i: (i, 0)),
                   pl.BlockSpec((tm, d), lambda i: (i, 0)),
                   pl.BlockSpec((1, d), lambda i: (0, 0)),
                   pl.BlockSpec((8, LANES), lambda i: (0, 0))],
        out_shape=[jax.ShapeDtypeStruct((s, d), BF16), jax.ShapeDtypeStruct((s, d), F32),
                   jax.ShapeDtypeStruct((1, d), F32), jax.ShapeDtypeStruct((8, LANES), F32)],
        compiler_params=_params("arbitrary"),
    )(u, w, x_res, gain, target)


def _matmul_nt(a, w_sm, name, comm=None):
    s, ncols = a.shape
    n_sh, r, ns = w_sm.shape
    assert ncols == n_sh * ns
    tm = _row_tile(s, 1024)
    nc = _col_chunk(ns, 1792)
    per = ns // nc
    steps = n_sh * per

    def body(a_ref, w_ref, o_ref):
        part = lax.dot_general(a_ref[...], w_ref[0], (((1,), (1,)), ((), ())), preferred_element_type=F32)

        @pl.when(pl.program_id(1) == 0)
        def _():
            o_ref[...] = part

        @pl.when(pl.program_id(1) > 0)
        def _():
            o_ref[...] += part

    out, bufs = _call(
        body, name=name, grid=(s // tm, steps),
        in_specs=[pl.BlockSpec((tm, nc), lambda i, j: (i, j)),
                  pl.BlockSpec((1, r, nc), lambda i, j: (j // per, 0, j % per))],
        out_specs=[pl.BlockSpec((tm, r), lambda i, j: (i, 0))],
        out_shape=[jax.ShapeDtypeStruct((s, r), F32)],
        args=(a, w_sm), comm=comm)
    return out[0], bufs


def _matmul_tn(a, b, n_sh, name):
    s, k = a.shape
    n = b.shape[1]
    ns = n // n_sh
    tk = _col_chunk(k, TN_ACC_BYTES // (4 * ns))
    ts = _row_tile(s, 1024)
    n_s = s // ts

    def body(a_ref, b_ref, o_ref, acc_ref):
        part = lax.dot_general(a_ref[...], b_ref[...], (((0,), (0,)), ((), ())), preferred_element_type=F32)

        @pl.when(pl.program_id(2) == 0)
        def _():
            acc_ref[...] = part

        @pl.when(pl.program_id(2) > 0)
        def _():
            acc_ref[...] += part

        @pl.when(pl.program_id(2) == n_s - 1)
        def _():
            o_ref[0] = acc_ref[...].astype(BF16)

    return pl.pallas_call(
        body, name=name, grid=(n_sh, k // tk, n_s),
        in_specs=[pl.BlockSpec((ts, tk), lambda j, i, t: (t, i)),
                  pl.BlockSpec((ts, ns), lambda j, i, t: (t, j))],
        out_specs=pl.BlockSpec((1, tk, ns), lambda j, i, t: (j, i, 0)),
        out_shape=jax.ShapeDtypeStruct((n_sh, k, ns), BF16),
        scratch_shapes=[pltpu.VMEM((tk, ns), F32)],
        compiler_params=_params("arbitrary", "arbitrary", "arbitrary"),
    )(a, b)


def _norm_bwd(dh, x, gain, dres, name, post=None, comm=None):
    s, d = x.shape
    tm = _row_tile(s, 256)
    with_post = post is not None

    def rms_bwd(dout, v, g):
        r = lax.rsqrt(_mean_last(v * v) + EPS)
        n = v * r
        dn = dout * g
        return r * (dn - n * _mean_last(dn * n)), _sum_rows(dout * n)

    def body(*refs):
        if with_post:
            dh_ref, x_ref, g_ref, dres_ref, y_ref, gp_ref, dx_ref, dg_ref, dy_ref, dgp_ref = refs
        else:
            dh_ref, x_ref, g_ref, dres_ref, dx_ref, dg_ref = refs

        @pl.when(pl.program_id(0) == 0)
        def _():
            dg_ref[...] = jnp.zeros_like(dg_ref)
            if with_post:
                dgp_ref[...] = jnp.zeros_like(dgp_ref)

        dv, dg = rms_bwd(dh_ref[...], x_ref[...], g_ref[...])
        dx = dres_ref[...] + dv
        dx_ref[...] = dx
        dg_ref[...] += dg
        if with_post:
            dy, dgp = rms_bwd(dx, y_ref[...], gp_ref[...])
            dy_ref[...] = dy.astype(BF16)
            dgp_ref[...] += dgp

    row = pl.BlockSpec((tm, d), lambda i: (i, 0))
    vec = pl.BlockSpec((1, d), lambda i: (0, 0))
    in_specs = [row, row, vec, row]
    out_specs = [row, vec]
    out_shape = [jax.ShapeDtypeStruct((s, d), F32), jax.ShapeDtypeStruct((1, d), F32)]
    args = [dh, x, gain, dres]
    if with_post:
        in_specs += [row, vec]
        out_specs += [row, vec]
        out_shape += [jax.ShapeDtypeStruct((s, d), BF16), jax.ShapeDtypeStruct((1, d), F32)]
        args += list(post)
    out, bufs = _call(body, name=name, grid=(s // tm,), in_specs=in_specs, out_specs=out_specs, out_shape=out_shape,
                      args=args, comm=comm)
    return out + [bufs] if comm is not None else out


def _fill_shifted_down(sh, rows):
    for b in range(1, SHIFTS):
        sh[b, SHIFTS:rows, :] = sh[0, SHIFTS - b:rows - b, :]


def _fill_shifted_up(sh, rows):
    for b in range(1, SHIFTS):
        sh[b, 0:rows - SHIFTS, :] = sh[0, b:rows - SHIFTS + b, :]


def _for_blocks(ts, w, fn):
    lb = min(LANE_BLOCK, w)
    for l0 in range(0, w, lb):
        def rows(rb, carry, l0=l0):
            fn(pl.multiple_of(rb * ROW_BLOCK, ROW_BLOCK), slice(l0, l0 + lb))
            return carry

        lax.fori_loop(0, ts // ROW_BLOCK, rows, 0)


TAP_SPAN = SHIFTS * ((CONV_B - 1) // SHIFTS)
WINDOW = ROW_BLOCK + TAP_SPAN


def _taps_of(b):
    return [(a, SHIFTS * a + b) for a in range((CONV_B - 1 - b) // SHIFTS + 1)]


def _conv31(sh, base, step, wt_ref, bias_ref, out_ref, ts, w):
    low = min(0, step * (TAP_SPAN // SHIFTS))

    def block(r0, lanes):
        acc = jnp.zeros((ROW_BLOCK, lanes.stop - lanes.start), F32)
        if bias_ref is not None:
            acc = acc + bias_ref[:, lanes]
        for b in range(SHIFTS):
            window = sh[b, pl.ds(pl.multiple_of(r0 + (base + low), SHIFTS), WINDOW), lanes]
            for a, j in _taps_of(b):
                at = step * a - low
                acc = acc + wt_ref[CONV_B - 1 - j:CONV_B - j, lanes] * window[at:at + ROW_BLOCK, :]
        out_ref[pl.ds(r0, ROW_BLOCK), lanes] = acc

    _for_blocks(ts, w, block)


def _conv31_weight_grad(d_sh, x_sh, wacc, ts, w):
    def block(r0, lanes):
        d = d_sh[0, pl.ds(r0, ROW_BLOCK), lanes]
        for b in range(SHIFTS):
            window = x_sh[b, pl.ds(pl.multiple_of(r0 + (HALO_B - TAP_SPAN), SHIFTS), WINDOW), lanes]
            for a, j in _taps_of(b):
                at = TAP_SPAN - SHIFTS * a
                prod = d * window[at:at + ROW_BLOCK, :]
                part = prod[0:SHIFTS, :]
                for q in range(1, ROW_BLOCK // SHIFTS):
                    part = part + prod[q * SHIFTS:(q + 1) * SHIFTS, :]
                wacc[CONV_B - 1 - j, :, lanes] += part

    _for_blocks(ts, w, block)


def _even_forward_tile(p_ref, halo_ref, first, a_conv_ref, b_conv_ref, bias_ref, lng_ref, lnb_ref, qbuf, ysh, y1buf,
                       w, ts):
    def col(ref, k, rows=slice(None)):
        return ref[rows, k * w:(k + 1) * w]

    a_x, a_b, a_c, a_z = col(p_ref, 0), col(p_ref, 1), col(p_ref, 2), col(p_ref, 3)
    b_val, b_gate, b_z = col(p_ref, 4), col(p_ref, 5), col(p_ref, 6)
    keep = jnp.where(first, 0.0, 1.0)

    rows_a = slice(HALO_B - HALO_A, HALO_B)
    qbuf[0:HALO_A, :] = col(halo_ref, 2, rows_a) * col(halo_ref, 0, rows_a) * keep
    qbuf[HALO_A:HALO_A + ts, :] = a_c * a_x
    cq = jnp.zeros((ts, w), F32)
    for j in range(CONV_A):
        cq = cq + a_conv_ref[CONV_A - 1 - j:CONV_A - j, :] * qbuf[HALO_A - j:HALO_A - j + ts, :]
    ya = a_b * cq

    ysh[0, 0:HALO_B, :] = col(halo_ref, 4) * _sigmoid(col(halo_ref, 5)) * keep
    ysh[0, HALO_B:HALO_B + ts, :] = b_val * _sigmoid(b_gate)
    _fill_shifted_down(ysh, HALO_B + ts)
    _conv31(ysh, HALO_B, -SHIFTS, b_conv_ref, bias_ref, y1buf, ts, w)
    yb1 = y1buf[...]
    xc = yb1 - _mean_last(yb1)
    rstd = lax.rsqrt(_mean_last(xc * xc) + EPS)
    xhat = xc * rstd
    yb2 = xhat * lng_ref[...] + lnb_ref[...]
    return dict(a_x=a_x, a_b=a_b, a_c=a_c, a_z=a_z, b_val=b_val, b_gate=b_gate, b_z=b_z,
                cq=cq, ya=ya, rstd=rstd, xhat=xhat, yb2=yb2)


def _even_specs(s, w, ts):
    tile = pl.BlockSpec((ts, 7 * w), lambda i: (i, 0))
    halo = pl.BlockSpec((HALO_B, 7 * w), lambda i: (jnp.maximum(i * (ts // HALO_B) - 1, 0), 0))
    return tile, halo


def _small_specs(shapes, index=lambda i: (0, 0)):
    return [pl.BlockSpec(sh, index) for sh in shapes]


def _even_mixer_fwd(p, w_out, x_res, gain, a_conv, b_conv, bias, ln_g, ln_b, name, comm=None):
    s, d = x_res.shape
    w = p.shape[1] // 7
    ts = _row_tile(s, 128)
    assert ts % HALO_B == 0

    def body(p_ref, halo_ref, wout_ref, x_ref, g_ref, ac_ref, bc_ref, bias_ref, lng_ref, lnb_ref,
             u_ref, xn_ref, y_ref, qbuf, ysh, y1buf):
        first = pl.program_id(0) == 0
        f = _even_forward_tile(p_ref, halo_ref, first, ac_ref, bc_ref, bias_ref, lng_ref, lnb_ref, qbuf, ysh, y1buf,
                               w, ts)
        yb3 = f["yb2"] * _sigmoid(f["yb2"])
        u_a = (f["ya"] * (f["a_z"] * _sigmoid(f["a_z"]))).astype(BF16)
        u_b = (yb3 * (f["b_z"] * _sigmoid(f["b_z"]))).astype(BF16)
        u_ref[:, 0:w] = u_a
        u_ref[:, w:2 * w] = u_b
        y = (jnp.dot(u_a, wout_ref[0:w, :], preferred_element_type=F32)
             + jnp.dot(u_b, wout_ref[w:2 * w, :], preferred_element_type=F32))
        r = lax.rsqrt(_mean_last(y * y) + EPS)
        y_ref[...] = y
        xn_ref[...] = x_ref[...] + (y * r) * g_ref[...]

    tile, halo = _even_specs(s, w, ts)
    row = pl.BlockSpec((ts, d), lambda i: (i, 0))
    return _call(
        body, name=name, grid=(s // ts,),
        in_specs=[tile, halo, pl.BlockSpec((2 * w, d), lambda i: (0, 0)), row, pl.BlockSpec((1, d), lambda i: (0, 0))]
        + _small_specs([(CONV_A, w), (CONV_B, w), (1, w), (1, w), (1, w)]),
        out_specs=[pl.BlockSpec((ts, 2 * w), lambda i: (i, 0)), row, row],
        out_shape=[jax.ShapeDtypeStruct((s, 2 * w), BF16), jax.ShapeDtypeStruct((s, d), F32),
                   jax.ShapeDtypeStruct((s, d), F32)],
        scratch_shapes=[pltpu.VMEM((HALO_A + ts, w), F32), pltpu.VMEM((SHIFTS, HALO_B + ts, w), F32),
                        pltpu.VMEM((ts, w), F32)],
        args=(p, p, w_out, x_res, gain, a_conv, b_conv, bias, ln_g, ln_b), comm=comm)


def _even_mixer_bwd(p, du, a_conv, b_conv, bias, ln_g, ln_b, name, comm=None):
    s = p.shape[0]
    w = p.shape[1] // 7
    ts = _row_tile(s, EVEN_BWD_ROWS)
    nt = s // ts
    assert ts % HALO_B == 0

    def body(p_ref, halo_ref, du_ref, ac_ref, bc_ref, bias_ref, lng_ref, lnb_ref,
             dp_ref, dac_ref, dbc_ref, dbias_ref, dlng_ref, dlnb_ref,
             qbuf, ysh, y1buf, dqbuf, dsh, dy0buf, wacc, carry_dq, carry_dy):
        step = pl.program_id(0)
        first = step == nt - 1

        @pl.when(step == 0)
        def _():
            for ref in (dac_ref, dbias_ref, dlng_ref, dlnb_ref, wacc, carry_dq, carry_dy):
                ref[...] = jnp.zeros_like(ref)

        f = _even_forward_tile(p_ref, halo_ref, first, ac_ref, bc_ref, bias_ref, lng_ref, lnb_ref, qbuf, ysh, y1buf,
                               w, ts)
        a_z, b_z, yb2 = f["a_z"], f["b_z"], f["yb2"]
        s_az, s_bz, s_y2 = _sigmoid(a_z), _sigmoid(b_z), _sigmoid(yb2)
        du_a = du_ref[:, 0:w]
        du_b = du_ref[:, w:2 * w]

        d_ya = du_a * (a_z * s_az)
        dp_ref[:, 3 * w:4 * w] = (du_a * f["ya"] * _dsilu(a_z, s_az)).astype(BF16)
        dp_ref[:, 1 * w:2 * w] = (d_ya * f["cq"]).astype(BF16)
        d_cq = d_ya * f["a_b"]
        dqbuf[0:ts, :] = d_cq
        dqbuf[ts:ts + HALO_A, :] = carry_dq[...]
        d_q = jnp.zeros((ts, w), F32)
        for o in range(CONV_A):
            d_q = d_q + ac_ref[CONV_A - 1 - o:CONV_A - o, :] * dqbuf[o:o + ts, :]
        for j in range(CONV_A):
            k = CONV_A - 1 - j
            dac_ref[k:k + 1, :] += _sum_rows(d_cq * qbuf[HALO_A - j:HALO_A - j + ts, :])
        carry_dq[...] = d_cq[0:HALO_A, :]
        dp_ref[:, 2 * w:3 * w] = (d_q * f["a_x"]).astype(BF16)
        dp_ref[:, 0 * w:1 * w] = (d_q * f["a_c"]).astype(BF16)

        d_yb3 = du_b * (b_z * s_bz)
        dp_ref[:, 6 * w:7 * w] = (du_b * (yb2 * s_y2) * _dsilu(b_z, s_bz)).astype(BF16)
        d_yb2 = d_yb3 * _dsilu(yb2, s_y2)
        xhat = f["xhat"]
        dlng_ref[...] += _sum_rows(d_yb2 * xhat)
        dlnb_ref[...] += _sum_rows(d_yb2)
        d_xh = d_yb2 * lng_ref[...]
        d_yb1 = f["rstd"] * (d_xh - _mean_last(d_xh) - xhat * _mean_last(d_xh * xhat))
        dbias_ref[...] += _sum_rows(d_yb1)
        dsh[0, 0:ts, :] = d_yb1
        dsh[0, ts:ts + HALO_B, :] = carry_dy[...]
        carry_dy[...] = d_yb1[0:HALO_B, :]
        _fill_shifted_up(dsh, ts + HALO_B)
        _conv31(dsh, 0, SHIFTS, bc_ref, None, dy0buf, ts, w)
        _conv31_weight_grad(dsh, ysh, wacc, ts, w)

        @pl.when(step == nt - 1)
        def _():
            for k in range(CONV_B):
                dbc_ref[k:k + 1, :] = _sum_rows(wacc[k])

        d_yb0 = dy0buf[...]
        s_g = _sigmoid(f["b_gate"])
        dp_ref[:, 4 * w:5 * w] = (d_yb0 * s_g).astype(BF16)
        dp_ref[:, 5 * w:6 * w] = (d_yb0 * f["b_val"] * s_g * (1.0 - s_g)).astype(BF16)

    rev = lambda i: (nt - 1 - i, 0)
    tile = pl.BlockSpec((ts, 7 * w), rev)
    halo = pl.BlockSpec((HALO_B, 7 * w), lambda i: (jnp.maximum((nt - 1 - i) * (ts // HALO_B) - 1, 0), 0))
    small = [(CONV_A, w), (CONV_B, w), (1, w), (1, w), (1, w)]
    return _call(
        body, name=name, grid=(nt,),
        in_specs=[tile, halo, pl.BlockSpec((ts, 2 * w), rev)] + _small_specs(small),
        out_specs=[pl.BlockSpec((ts, 7 * w), rev)] + _small_specs(small),
        out_shape=[jax.ShapeDtypeStruct((s, 7 * w), BF16)] + [jax.ShapeDtypeStruct(sh, F32) for sh in small],
        scratch_shapes=[pltpu.VMEM((HALO_A + ts, w), F32), pltpu.VMEM((SHIFTS, HALO_B + ts, w), F32),
                        pltpu.VMEM((ts, w), F32),
                        pltpu.VMEM((ts + HALO_A, w), F32), pltpu.VMEM((SHIFTS, ts + HALO_B, w), F32),
                        pltpu.VMEM((ts, w), F32), pltpu.VMEM((CONV_B, SHIFTS, w), F32),
                        pltpu.VMEM((HALO_A, w), F32), pltpu.VMEM((HALO_B, w), F32)],
        args=(p, p, du, a_conv, b_conv, bias, ln_g, ln_b), comm=comm)


def _pool_forward_tile(p_ref, halo_ref, first, tile_index, cw_ref, cb_ref, cs_ref, vbuf, c, gc, ts):
    vbuf[0:HALO_P, :] = halo_ref[...] * jnp.where(first, 0.0, 1.0)
    vbuf[HALO_P:HALO_P + ts, :] = p_ref[:, 0:c]
    pos = tile_index * ts + lax.broadcasted_iota(jnp.int32, (ts, 1), 0) + 1
    pooled, inv, gout = [], [], []
    for g, win in enumerate(POOL_WINDOWS):
        cols = slice(g * gc, (g + 1) * gc)
        acc = jnp.zeros((ts, gc), F32)
        for j in range(win):
            acc = acc + vbuf[HALO_P - j:HALO_P - j + ts, cols]
        inv_g = 1.0 / jnp.minimum(pos, win).astype(F32)
        pooled_g = (acc * inv_g - p_ref[:, cols]).astype(BF16)
        pooled.append(pooled_g)
        inv.append(inv_g)
        gout.append(jnp.dot(pooled_g, cw_ref[g], preferred_element_type=F32) + cb_ref[:, cols])
    return pooled, inv, gout


def _odd_mixer_fwd(p, cw, cb, cs, name):
    s = p.shape[0]
    c = p.shape[1] // 2
    gc = c // N_GROUPS
    ts = _row_tile(s, 256)

    def body(p_ref, halo_ref, cw_ref, cb_ref, cs_ref, u_ref, vbuf):
        i = pl.program_id(0)
        _, _, gout = _pool_forward_tile(p_ref, halo_ref, i == 0, i, cw_ref, cb_ref, cs_ref, vbuf, c, gc, ts)
        for g in range(N_GROUPS):
            cols = slice(g * gc, (g + 1) * gc)
            z = p_ref[:, c + g * gc:c + (g + 1) * gc]
            u_ref[:, cols] = (gout[g] * cs_ref[:, cols] * (z * _sigmoid(z))).astype(BF16)

    return pl.pallas_call(
        body, name=name, grid=(s // ts,),
        in_specs=[pl.BlockSpec((ts, 2 * c), lambda i: (i, 0)),
                  pl.BlockSpec((HALO_P, c), lambda i: (jnp.maximum(i * (ts // HALO_P) - 1, 0), 0)),
                  pl.BlockSpec((N_GROUPS, gc, gc), lambda i: (0, 0, 0)),
                  pl.BlockSpec((1, c), lambda i: (0, 0)), pl.BlockSpec((1, c), lambda i: (0, 0))],
        out_specs=pl.BlockSpec((ts, c), lambda i: (i, 0)),
        out_shape=jax.ShapeDtypeStruct((s, c), BF16),
        scratch_shapes=[pltpu.VMEM((HALO_P + ts, c), F32)],
        compiler_params=_params("arbitrary"),
    )(p, p, cw, cb, cs)


def _odd_mixer_bwd(p, du, cw, cb, cs, name):
    s = p.shape[0]
    c = p.shape[1] // 2
    gc = c // N_GROUPS
    ts = _row_tile(s, 256)
    nt = s // ts

    def body(p_ref, halo_ref, du_ref, cw_ref, cb_ref, cs_ref, dp_ref, dcw_ref, dcb_ref, dcs_ref, vbuf, ebuf, carry_e):
        step = pl.program_id(0)
        tile_index = nt - 1 - step

        @pl.when(step == 0)
        def _():
            for ref in (dcw_ref, dcb_ref, dcs_ref, carry_e):
                ref[...] = jnp.zeros_like(ref)

        pooled, inv, gout = _pool_forward_tile(p_ref, halo_ref, tile_index == 0, tile_index, cw_ref, cb_ref, cs_ref,
                                               vbuf, c, gc, ts)
        ebuf[ts:ts + HALO_P, :] = carry_e[...]
        for g, win in enumerate(POOL_WINDOWS):
            cols = slice(g * gc, (g + 1) * gc)
            z = p_ref[:, c + g * gc:c + (g + 1) * gc]
            sz = _sigmoid(z)
            du_g = du_ref[:, cols]
            scale = cs_ref[:, cols]
            d_y = du_g * (z * sz)
            dp_ref[:, c + g * gc:c + (g + 1) * gc] = (du_g * (gout[g] * scale) * _dsilu(z, sz)).astype(BF16)
            dcs_ref[:, cols] += _sum_rows(d_y * gout[g])
            d_gout = d_y * scale
            dcb_ref[:, cols] += _sum_rows(d_gout)
            d_gout_b = d_gout.astype(BF16)
            dcw_ref[g] += lax.dot_general(pooled[g], d_gout_b, (((0,), (0,)), ((), ())), preferred_element_type=F32)
            d_pool = lax.dot_general(d_gout_b, cw_ref[g], (((1,), (1,)), ((), ())), preferred_element_type=F32)
            e = d_pool * inv[g]
            ebuf[0:ts, cols] = e
            d_v = -d_pool
            for o in range(win):
                d_v = d_v + ebuf[o:o + ts, cols]
            dp_ref[:, cols] = d_v.astype(BF16)
            carry_e[:, cols] = e[0:HALO_P, :]

    rev = lambda i: (nt - 1 - i, 0)
    small = [(N_GROUPS, gc, gc), (1, c), (1, c)]
    return pl.pallas_call(
        body, name=name, grid=(nt,),
        in_specs=[pl.BlockSpec((ts, 2 * c), rev),
                  pl.BlockSpec((HALO_P, c), lambda i: (jnp.maximum((nt - 1 - i) * (ts // HALO_P) - 1, 0), 0)),
                  pl.BlockSpec((ts, c), rev),
                  pl.BlockSpec((N_GROUPS, gc, gc), lambda i: (0, 0, 0)),
                  pl.BlockSpec((1, c), lambda i: (0, 0)), pl.BlockSpec((1, c), lambda i: (0, 0))],
        out_specs=[pl.BlockSpec((ts, 2 * c), rev),
                   pl.BlockSpec((N_GROUPS, gc, gc), lambda i: (0, 0, 0)),
                   pl.BlockSpec((1, c), lambda i: (0, 0)), pl.BlockSpec((1, c), lambda i: (0, 0))],
        out_shape=[jax.ShapeDtypeStruct((s, 2 * c), BF16)] + [jax.ShapeDtypeStruct(sh, F32) for sh in small],
        scratch_shapes=[pltpu.VMEM((HALO_P + ts, c), F32), pltpu.VMEM((ts + HALO_P, c), F32),
                        pltpu.VMEM((HALO_P, c), F32)],
        compiler_params=_params("arbitrary"),
    )(p, p, du, cw, cb, cs)


def _cast_into_slot(a, coords, name):
    r, cols = a.shape
    tr = _row_tile(r // 2, 256)
    per = r // 2 // tr

    def body(co_ref, a_ref, o_ref):
        o_ref[0, 0] = a_ref[...].astype(BF16)

    return pl.pallas_call(
        body, name=name,
        grid_spec=pltpu.PrefetchScalarGridSpec(
            num_scalar_prefetch=1, grid=(2, per),
            in_specs=[pl.BlockSpec((tr, cols), lambda h, i, co: (h * per + i, 0))],
            out_specs=pl.BlockSpec((1, 1, tr, cols), lambda h, i, co: (co[0], h, i, 0))),
        out_shape=jax.ShapeDtypeStruct((N_SHARDS, 2, r // 2, cols), BF16),
        compiler_params=_params("arbitrary", "arbitrary"),
    )(coords, a)


def _chip_sum(g, other, coords, name):
    n_sh, _, r2, cols = g.shape
    tr = _row_tile(r2, 256)

    def body(co_ref, g_ref, o_ref, sum_ref, mine_ref):
        v = (g_ref[0, 0].astype(F32) + o_ref[0].astype(F32)).astype(BF16)
        sum_ref[0] = v

        @pl.when(pl.program_id(1) == co_ref[0])
        def _():
            mine_ref[0] = v

    piece = pl.BlockSpec((1, tr, cols), lambda i, s, co: (s, i, 0))
    return pl.pallas_call(
        body, name=name,
        grid_spec=pltpu.PrefetchScalarGridSpec(
            num_scalar_prefetch=1, grid=(r2 // tr, n_sh),
            in_specs=[pl.BlockSpec((1, 1, tr, cols), lambda i, s, co: (s, co[1], i, 0)), piece],
            out_specs=[piece, pl.BlockSpec((1, tr, cols), lambda i, s, co: (co[0], i, 0))]),
        out_shape=[jax.ShapeDtypeStruct((n_sh, r2, cols), BF16)] * 2,
        compiler_params=_params("arbitrary", "arbitrary"),
    )(coords, g, other)


def _shard_sum(pieces, coords, name):
    n_sh, r2, cols = pieces.shape
    tr = _row_tile(r2, 256)

    def body(co_ref, p_ref, o_ref):
        acc = p_ref[0].astype(F32)
        for k in range(1, n_sh):
            acc = acc + p_ref[k].astype(F32)
        o_ref[0] = acc

    return pl.pallas_call(
        body, name=name,
        grid_spec=pltpu.PrefetchScalarGridSpec(
            num_scalar_prefetch=1, grid=(r2 // tr,),
            in_specs=[pl.BlockSpec((n_sh, tr, cols), lambda i, co: (0, i, 0))],
            out_specs=pl.BlockSpec((1, tr, cols), lambda i, co: (co[1], i, 0))),
        out_shape=jax.ShapeDtypeStruct((2, r2, cols), F32),
        compiler_params=_params("arbitrary"),
    )(coords, pieces)


def _sum_small(a, name):
    n, r, cols = a.shape

    def body(a_ref, o_ref):
        acc = a_ref[0]
        for k in range(1, n):
            acc = acc + a_ref[k]
        o_ref[...] = acc

    return pl.pallas_call(
        body, name=name,
        in_specs=[pl.BlockSpec((n, r, cols), lambda: (0, 0, 0))],
        out_specs=pl.BlockSpec((r, cols), lambda: (0, 0)),
        out_shape=jax.ShapeDtypeStruct((r, cols), F32),
        compiler_params=_params(),
    )(a)


def _adamw(w, g, m, v, name):
    r, cols = w.shape
    tr = _row_tile(r, 256) if r % SUBLANES_BF16 == 0 else r

    def body(w_ref, g_ref, m_ref, v_ref, d_ref, nm_ref, nv_ref):
        g = g_ref[...]
        m = ADAM_B1 * m_ref[...] + (1.0 - ADAM_B1) * g
        v = ADAM_B2 * v_ref[...] + (1.0 - ADAM_B2) * (g * g)
        m_hat = m / (1.0 - ADAM_B1 ** ADAM_STEP)
        v_hat = v / (1.0 - ADAM_B2 ** ADAM_STEP)
        d_ref[...] = -ADAM_LR * (m_hat / (jnp.sqrt(v_hat) + ADAM_EPS) + ADAM_WD * w_ref[...])
        nm_ref[...] = m
        nv_ref[...] = v

    blk = pl.BlockSpec((tr, cols), lambda i: (i, 0))
    return pl.pallas_call(
        body, name=name, grid=(r // tr,),
        in_specs=[blk] * 4, out_specs=[blk] * 3,
        out_shape=[jax.ShapeDtypeStruct((r, cols), F32)] * 3,
        compiler_params=_params("arbitrary"),
    )(w, g, m, v)


def _place():
    x, y, c = lax.axis_index("x"), lax.axis_index("y"), lax.axis_index("c")
    other_chips = [(1 - x, y), (x, 1 - y), (1 - x, 1 - y)]
    return x, y, c, other_chips


def _chip(xy):
    return 2 * xy[0] + xy[1]


def _remote(src, dst, send_sem, recv_sem, to):
    return pltpu.make_async_remote_copy(src_ref=src, dst_ref=dst, send_sem=send_sem, recv_sem=recv_sem,
                                        device_id=to, device_id_type=MESH)


def _gather_ici(ctx, k, j, start):
    (x, y, c, chips), b, send, recv = ctx
    blk = b[k].at[_chip((x, y)) if start else _chip(chips[j]), c]
    return _remote(blk, blk, send.at[6 * k + j], recv.at[6 * k + j], (*chips[j], c))


def _gather_d2d(ctx, k, j, start):
    (x, y, c, chips), b, send, recv = ctx
    blk = b[k].at[_chip(chips[j]), c if start else 1 - c]
    return _remote(blk, blk, send.at[6 * k + 3 + j], recv.at[6 * k + 3 + j], (x, y, 1 - c))


def _gather_small(ctx, n, j, start):
    (x, y, c, chips), b, send, recv = ctx
    blk = b[n].at[_chip((x, y)) if start else _chip(chips[j])]
    return _remote(blk, blk, send.at[6 * n + j], recv.at[6 * n + j], (*chips[j], c))


def _gather_comm(bufs, small, forward_at):
    n = len(bufs)
    ici, d2d = _gather_ici, _gather_d2d

    def small_copy(ctx, j, start):
        return _gather_small(ctx, n, j, start)

    def start(srcs, b, send, recv):
        ctx = (_place(), b, send, recv)
        for k in range(n):
            for j in range(3):
                ici(ctx, k, j, True).start()
        if small is not None:
            for j in range(3):
                small_copy(ctx, j, True).start()

    def forward(srcs, b, send, recv):
        ctx = (_place(), b, send, recv)
        for k in range(n):
            for j in range(3):
                ici(ctx, k, j, False).wait_recv()
                d2d(ctx, k, j, True).start()

    def finish(srcs, b, send, recv):
        ctx = (_place(), b, send, recv)
        if small is not None:
            for j in range(3):
                small_copy(ctx, j, False).wait_recv()
                small_copy(ctx, j, True).wait_send()
        for k in range(n):
            for j in range(3):
                d2d(ctx, k, j, False).wait_recv()
                ici(ctx, k, j, True).wait_send()
                d2d(ctx, k, j, True).wait_send()

    all_bufs = list(bufs) + ([small] if small is not None else [])
    return _Comm([], all_bufs, 6 * n + 3, [(0, start), (forward_at, forward)], finish)


def _rmsnorm(x, gain, name):
    s, d = x.shape
    tm = _row_tile(s, 512)

    def body(x_ref, g_ref, h_ref):
        xv = x_ref[...]
        r = lax.rsqrt(_mean_last(xv * xv) + EPS)
        h_ref[...] = (xv * r * g_ref[...]).astype(BF16)

    return pl.pallas_call(
        body, name=name, grid=(s // tm,),
        in_specs=[pl.BlockSpec((tm, d), lambda i: (i, 0)), pl.BlockSpec((1, d), lambda i: (0, 0))],
        out_specs=pl.BlockSpec((tm, d), lambda i: (i, 0)),
        out_shape=jax.ShapeDtypeStruct((s, d), BF16),
        compiler_params=_params("arbitrary"),
    )(x, gain)


def _gathered_in_proj(h, bufs, small, order, name):
    s, d = h.shape
    n_sh, _, r2, ns = bufs[0].shape
    assert d == 2 * r2
    n = len(bufs)
    tm = _row_tile(s, 512)
    n_i = s // tm
    hook_i = max(n_i - 2, 0)
    n_sems = 6 * n + 3

    def body(order_ref, h_ref, *rest):
        p_ref = rest[n + 1]
        b = rest[n + 2:2 * n + 3]
        w_vmem, w_sems, send, recv = rest[2 * n + 3:]
        j, i = pl.program_id(0), pl.program_id(1)
        ctx = (_place(), b, send, recv)

        def fetch(q):
            return pltpu.make_async_copy(b[0].at[order_ref[q]], w_vmem.at[q % 2], w_sems.at[q % 2])

        @pl.when((j == 0) & (i == 0))
        def _():
            for k in range(n):
                for peer in range(3):
                    _gather_ici(ctx, k, peer, True).start()
            for peer in range(3):
                _gather_small(ctx, n, peer, True).start()
            fetch(0).start()
            fetch(0).wait()

        for q in range(1, n_sh):
            @pl.when((j == q - 1) & (i == hook_i))
            def _(q=q):
                _gather_ici(ctx, 0, q - 1, False).wait_recv()
                _gather_d2d(ctx, 0, q - 1, True).start()
                _gather_d2d(ctx, 0, q - 1, False).wait_recv()
                fetch(q).start()
                if q == n_sh - 1:
                    for k in range(1, n):
                        for peer in range(3):
                            _gather_ici(ctx, k, peer, False).wait_recv()
                            _gather_d2d(ctx, k, peer, True).start()

            @pl.when((j == q) & (i == 0))
            def _(q=q):
                fetch(q).wait()

        wv = w_vmem.at[j % 2]
        p_ref[...] = (jnp.dot(h_ref[:, 0:r2], wv[0], preferred_element_type=F32)
                      + jnp.dot(h_ref[:, r2:d], wv[1], preferred_element_type=F32))

        @pl.when((j == n_sh - 1) & (i == n_i - 1))
        def _():
            for peer in range(3):
                _gather_small(ctx, n, peer, False).wait_recv()
                _gather_small(ctx, n, peer, True).wait_send()
            for k in range(n):
                for peer in range(3):
                    if k > 0:
                        _gather_d2d(ctx, k, peer, False).wait_recv()
                    _gather_ici(ctx, k, peer, True).wait_send()
                    _gather_d2d(ctx, k, peer, True).wait_send()

    all_bufs = list(bufs) + [small]
    out = pl.pallas_call(
        body, name=name,
        grid_spec=pltpu.PrefetchScalarGridSpec(
            num_scalar_prefetch=1, grid=(n_sh, n_i),
            in_specs=[pl.BlockSpec((tm, d), lambda j, i, o: (i, 0))] + [ANY] * (n + 1),
            out_specs=[pl.BlockSpec((tm, ns), lambda j, i, o: (i, o[j]))] + [ANY] * (n + 1),
            scratch_shapes=[pltpu.VMEM((2, 2, r2, ns), BF16), pltpu.SemaphoreType.DMA((2,)),
                            pltpu.SemaphoreType.DMA((n_sems,)), pltpu.SemaphoreType.DMA((n_sems,))]),
        out_shape=[jax.ShapeDtypeStruct((s, n_sh * ns), F32)]
        + [jax.ShapeDtypeStruct(a.shape, a.dtype) for a in all_bufs],
        input_output_aliases={2 + t: 1 + t for t in range(n + 1)},
        compiler_params=_params("arbitrary", "arbitrary"),
    )(order, h, *all_bufs)
    return out[0], list(out[1:])


def _exchange_core_halves(grads, name):
    n = len(grads)

    def body(*refs):
        ins, outs = refs[:n], refs[n:2 * n]
        send_sems, recv_sems = refs[2 * n:]
        x, y, c, _ = _place()
        copies = [_remote(ins[k].at[s, 1 - c], outs[k].at[s], send_sems.at[N_SHARDS * k + s],
                          recv_sems.at[N_SHARDS * k + s], (x, y, 1 - c))
                  for k in range(n) for s in range(N_SHARDS)]
        for cp in copies:
            cp.start()
        for cp in copies:
            cp.wait()

    return pl.pallas_call(
        body, name=name, in_specs=[ANY] * n, out_specs=[ANY] * n,
        out_shape=[jax.ShapeDtypeStruct((N_SHARDS,) + a.shape[2:], a.dtype) for a in grads],
        scratch_shapes=[pltpu.SemaphoreType.DMA((N_SHARDS * n,))] * 2,
    )(*grads)


def _scatter_comm(chip_sums, landing, rows=None):
    n = len(chip_sums)
    span = slice(None) if rows is None else pl.ds(rows[0], rows[1])

    def big(srcs, b, send, recv, k, j, start):
        x, y, c, chips = _place()
        dst = b[k].at[_chip((x, y)) if start else _chip(chips[j]), span]
        return _remote(srcs[k].at[_chip(chips[j]), span], dst, send.at[3 * k + j], recv.at[3 * k + j],
                       (*chips[j], c))

    def start(srcs, b, send, recv):
        for k in range(n):
            for j in range(3):
                big(srcs, b, send, recv, k, j, True).start()

    def finish(srcs, b, send, recv):
        for k in range(n):
            for j in range(3):
                big(srcs, b, send, recv, k, j, False).wait_recv()
                big(srcs, b, send, recv, k, j, True).wait_send()

    return _Comm(chip_sums, landing, 3 * n, [(0, start)], finish)


def _join_comm(halves, small):
    n = len(halves)
    flips = [(fx, fy, fc) for fx in (0, 1) for fy in (0, 1) for fc in (0, 1)][1:]

    def half(b, send, recv, k, start):
        x, y, c, _ = _place()
        return _remote(b[k].at[c], b[k].at[c if start else 1 - c], send.at[k], recv.at[k], (x, y, 1 - c))

    def small_copy(b, send, recv, q, start):
        x, y, c, _ = _place()
        px, py, pc = x ^ flips[q][0], y ^ flips[q][1], c ^ flips[q][2]
        blk = b[n].at[4 * x + 2 * y + c if start else 4 * px + 2 * py + pc]
        return _remote(blk, blk, send.at[n + q], recv.at[n + q], (px, py, pc))

    def start(srcs, b, send, recv):
        for q in range(len(flips)):
            small_copy(b, send, recv, q, True).start()
        for k in range(n):
            half(b, send, recv, k, True).start()

    def finish(srcs, b, send, recv):
        for q in range(len(flips)):
            small_copy(b, send, recv, q, False).wait()
        for k in range(n):
            half(b, send, recv, k, False).wait()

    return _Comm([], list(halves) + [small], n + len(flips), [(0, start)], finish)


def _flat_rows(parts):
    flat = jnp.concatenate([p.reshape(-1) for p in parts])
    assert flat.shape[0] % LANES == 0
    return flat.reshape(-1, LANES)


def _unflatten(flat, shapes):
    out, at = [], 0
    for sh in shapes:
        size = 1
        for dim in sh:
            size *= dim
        out.append(flat[at:at + size].reshape(sh))
        at += size
    assert at == flat.shape[0], (at, flat.shape)
    return out


def _col_shards_to_full(a, rows):
    q = a.shape[1] // rows
    return a.reshape(N_SHARDS, rows, q).transpose(1, 0, 2).reshape(rows, N_SHARDS * q)


def _my_col_shard(full, chip):
    rows, cols = full.shape
    q = cols // N_SHARDS
    return lax.dynamic_index_in_dim(full.reshape(rows, N_SHARDS, q), chip, axis=1, keepdims=False)


def kernel(x, e_norm_pre, e_norm_post, e_w_in, e_a_conv, e_b_conv, e_b_conv_bias, e_b_ln_g, e_b_ln_b, e_w_out, o_norm_pre, o_norm_post, o_w_in, o_c_w, o_c_b, o_c_scale, o_w_out, loss_target, m_e_norm_pre, m_e_norm_post, m_e_w_in, m_e_a_conv, m_e_b_conv, m_e_b_conv_bias, m_e_b_ln_g, m_e_b_ln_b, m_e_w_out, m_o_norm_pre, m_o_norm_post, m_o_w_in, m_o_c_w, m_o_c_b, m_o_c_scale, m_o_w_out, v_e_norm_pre, v_e_norm_post, v_e_w_in, v_e_a_conv, v_e_b_conv, v_e_b_conv_bias, v_e_b_ln_g, v_e_b_ln_b, v_e_w_out, v_o_norm_pre, v_o_norm_post, v_o_w_in, v_o_c_w, v_o_c_b, v_o_c_scale, v_o_w_out):
    _, s, d = x.shape
    w = d // 2
    c = d
    gc = c // N_GROUPS
    wq, cq, gq = w // N_SHARDS, c // N_SHARDS, gc // N_SHARDS
    chip = 2 * lax.axis_index("x") + lax.axis_index("y")
    core = lax.axis_index("c")
    x2 = x.reshape(s, d)
    target = loss_target.reshape(s, d)

    big_w = [e_w_in[0], e_w_out[0], o_w_in[0], o_c_w[0].reshape(N_GROUPS * gq, gc), o_w_out[0]]
    big_m = [m_e_w_in[0], m_e_w_out[0], m_o_w_in[0], m_o_c_w[0].reshape(N_GROUPS * gq, gc), m_o_w_out[0]]
    big_v = [v_e_w_in[0], v_e_w_out[0], v_o_w_in[0], v_o_c_w[0].reshape(N_GROUPS * gq, gc), v_o_w_out[0]]
    coords = jnp.stack([chip, core]).astype(jnp.int32)
    slots = [_cast_into_slot(a, coords, "cast_%d" % k) for k, a in enumerate(big_w)]
    sharded_small = _flat_rows([e_a_conv[0], e_b_conv[0], o_norm_pre, o_norm_post, o_c_scale, o_c_b[0]])
    small_slots = lax.dynamic_update_index_in_dim(jnp.zeros((N_SHARDS,) + sharded_small.shape, F32), sharded_small,
                                                  chip, 0)
    xi, yi = lax.axis_index("x"), lax.axis_index("y")
    order = jnp.stack([chip, 2 * (1 - xi) + yi, 2 * xi + (1 - yi), 2 * (1 - xi) + (1 - yi)]).astype(jnp.int32)
    h0 = _rmsnorm(x2, e_norm_pre, "e_pre_norm")
    p0, (e_w_in_g, e_w_out_g, small_g4) = _gathered_in_proj(h0, slots[:2], small_slots, order, "e_in_proj")
    e_w_in_sm = e_w_in_g.reshape((N_SHARDS,) + big_w[0].shape)
    e_w_out_f = e_w_out_g.reshape(w + w, d)
    sm = small_g4.reshape(N_SHARDS, -1)
    at = [0]

    def take(rows, q):
        blk = sm[:, at[0]:at[0] + rows * q]
        at[0] += rows * q
        return _col_shards_to_full(blk, rows)

    a_conv_f = take(CONV_A, wq)
    b_conv_f = take(CONV_B, wq)
    o_pre_f = take(1, cq)
    o_post_f = take(1, cq)
    cs_f = take(1, cq)
    cb_f = take(N_GROUPS, gq).reshape(1, c)

    mixer_steps = s // _row_tile(s, 128)
    (u0, x1, y0), odd_g = _even_mixer_fwd(p0, e_w_out_f, x2, e_norm_post, a_conv_f, b_conv_f, e_b_conv_bias, e_b_ln_g,
                                          e_b_ln_b, "e_mixer_out_proj",
                                          comm=_gather_comm(slots[2:], None, max(mixer_steps - 4, 0)))
    o_w_in_sm = odd_g[0].reshape((N_SHARDS,) + big_w[2].shape)
    cw_f = odd_g[1].reshape(N_SHARDS, N_GROUPS, gq, gc).transpose(1, 0, 2, 3).reshape(N_GROUPS, gc, gc)
    o_w_out_f = odd_g[2].reshape(c, d)
    (p1, h1), _ = _norm_matmul(x1, o_pre_f, o_w_in_sm, "o_in_proj")
    u1 = _odd_mixer_fwd(p1, cw_f, cb_f, cs_f, "o_mixer_fwd")
    d_y1, d_x2, d_o_post, loss_part = _matmul_post_loss(u1, o_w_out_f, x1, o_post_f, target, "o_out_proj_loss")

    def as_pieces(g, k):
        return g.reshape(N_SHARDS, 2, big_w[k].shape[0] // 2, big_w[k].shape[1])

    def chip_sums(ks, grads, name):
        pieces = [as_pieces(g, k) for k, g in zip(ks, grads)]
        from_sibling = _exchange_core_halves(pieces, name)
        both = [_chip_sum(g, o, coords, "chip_sum_%d" % k) for k, g, o in zip(ks, pieces, from_sibling)]
        return [b[0] for b in both], [b[1] for b in both]

    g_o_w_out = _matmul_tn(u1, d_y1, 1, "o_w_out_grad")
    d_u1, _ = _matmul_nt(d_y1, o_w_out_f[None], "o_out_proj_bwd")
    d_p1, d_cw, d_cb, d_cs = _odd_mixer_bwd(p1, d_u1, cw_f, cb_f, cs_f, "o_mixer_bwd")
    g_o_w_in = _matmul_tn(h1, d_p1, N_SHARDS, "o_w_in_grad")
    d_h1, _ = _matmul_nt(d_p1, o_w_in_sm, "o_in_proj_bwd")
    d_x1, d_o_pre, d_y0, d_e_post = _norm_bwd(d_h1, x1, o_pre_f, d_x2, "o_pre_norm_bwd", post=(y0, e_norm_post))

    g_e_w_out = _matmul_tn(u0, d_y0, 1, "e_w_out_grad")
    g_cw = d_cw.reshape(N_GROUPS, N_SHARDS, gq, gc).transpose(1, 0, 2, 3).astype(BF16)
    sums_a, landing_a = chip_sums([1, 2, 3, 4], [g_e_w_out, g_o_w_in, g_cw, g_o_w_out], "exchange_core_halves_a")
    d_u0, _ = _matmul_nt(d_y0, e_w_out_f[None], "e_out_proj_bwd")
    (d_p0, d_a_conv, d_b_conv, d_bias, d_ln_g, d_ln_b), landed_a = _even_mixer_bwd(
        p0, d_u0, a_conv_f, b_conv_f, e_b_conv_bias, e_b_ln_g, e_b_ln_b, "e_mixer_bwd",
        comm=_scatter_comm(sums_a, landing_a))
    g_e_w_in = _matmul_tn(h0, d_p0, N_SHARDS, "e_w_in_grad")
    sums_b, landing_b = chip_sums([0], [g_e_w_in], "exchange_core_halves_b")
    piece_rows = sums_b[0].shape[1]
    early_rows = piece_rows * 13 // 16 // SUBLANES_BF16 * SUBLANES_BF16
    d_h0, landed_b = _matmul_nt(d_p0, e_w_in_sm, "e_in_proj_bwd",
                                comm=_scatter_comm(sums_b, landing_b, (0, early_rows)))
    grad_x, d_e_pre, landed_b = _norm_bwd(
        d_h0, x2, e_norm_pre, d_x1, "e_pre_norm_bwd",
        comm=_scatter_comm(sums_b, landed_b, (early_rows, piece_rows - early_rows)))

    landed = landed_b + landed_a
    reduced = [_shard_sum(sc, coords, "shard_sum_%d" % k) for k, sc in enumerate(landed)]
    small_parts = _flat_rows([loss_part[0], d_e_pre, d_e_post, d_bias, d_ln_g, d_ln_b, d_a_conv, d_b_conv,
                              d_o_pre, d_o_post, d_cs, d_cb])
    small_rows = lax.dynamic_update_index_in_dim(jnp.zeros((N_DEVICES,) + small_parts.shape, F32), small_parts,
                                                 2 * chip + core, 0)
    joined = _comm_only(_join_comm(reduced, small_rows), "join_core_halves")
    big_g = [j.reshape(a.shape) for j, a in zip(joined[:5], big_w)]
    small_sum = _sum_small(joined[5], "small_sum").reshape(-1)
    (loss_row, g_e_pre, g_e_post, g_bias, g_ln_g, g_ln_b, g_a_conv_f, g_b_conv_f, g_o_pre_f, g_o_post_f, g_cs_f,
     g_cb_f) = _unflatten(small_sum, [(LANES,), (1, d), (1, d), (1, w), (1, w), (1, w), (CONV_A, w), (CONV_B, w),
                                      (1, c), (1, c), (1, c), (1, c)])
    loss = loss_row[0]
    g_a_conv = _my_col_shard(g_a_conv_f, chip)
    g_b_conv = _my_col_shard(g_b_conv_f, chip)
    g_o_pre = _my_col_shard(g_o_pre_f, chip)
    g_o_post = _my_col_shard(g_o_post_f, chip)
    g_cs = _my_col_shard(g_cs_f, chip)
    g_cb = _my_col_shard(g_cb_f.reshape(N_GROUPS, gc), chip)

    big_upd = [_adamw(wt, g, m, v, "adamw_%d" % k) for k, (wt, g, m, v) in enumerate(zip(big_w, big_g, big_m, big_v))]
    small_w = [e_norm_pre, e_norm_post, e_b_conv_bias, e_b_ln_g, e_b_ln_b, e_a_conv[0], e_b_conv[0],
               o_norm_pre, o_norm_post, o_c_b[0], o_c_scale]
    small_m = [m_e_norm_pre, m_e_norm_post, m_e_b_conv_bias, m_e_b_ln_g, m_e_b_ln_b, m_e_a_conv[0], m_e_b_conv[0],
               m_o_norm_pre, m_o_norm_post, m_o_c_b[0], m_o_c_scale]
    small_v = [v_e_norm_pre, v_e_norm_post, v_e_b_conv_bias, v_e_b_ln_g, v_e_b_ln_b, v_e_a_conv[0], v_e_b_conv[0],
               v_o_norm_pre, v_o_norm_post, v_o_c_b[0], v_o_c_scale]
    small_g = [g_e_pre, g_e_post, g_bias, g_ln_g, g_ln_b, g_a_conv, g_b_conv, g_o_pre, g_o_post, g_cb, g_cs]
    small_shapes = [a.shape for a in small_w]
    small_upd = _adamw(_flat_rows(small_w), _flat_rows(small_g), _flat_rows(small_m), _flat_rows(small_v),
                       "adamw_small")
    small_delta, small_new_m, small_new_v = [_unflatten(u.reshape(-1), small_shapes) for u in small_upd]

    def ordered(small, big, lead):
        (n_pre, n_post, bias, ln_g, ln_b, a_conv, b_conv, o_pre, o_post, cb, cs) = small
        (w_in, w_out, ow_in, cw, ow_out) = big
        out = [n_pre, n_post, w_in[None], a_conv[None], b_conv[None], bias, ln_g, ln_b, w_out[None], o_pre, o_post,
               ow_in[None], cw.reshape(1, N_GROUPS, gq, gc), cb[None], cs, ow_out[None]]
        return out

    grads = ordered(small_g, big_g, None)
    deltas = ordered(small_delta, [u[0] for u in big_upd], None)
    new_m = ordered(small_new_m, [u[1] for u in big_upd], None)
    new_v = ordered(small_new_v, [u[2] for u in big_upd], None)
    return (loss, grad_x.reshape(1, s, d), *grads, *deltas, *new_m, *new_v)
```

```python
import functools

import jax
import jax.numpy as jnp
from jax import lax
from jax.experimental import pallas as pl
from jax.experimental.pallas import tpu as pltpu

F32 = jnp.float32
BF16 = jnp.bfloat16
MESH = pl.DeviceIdType.MESH

EPS = 1e-6
CONV_A = 3
CONV_B = 31
POOL_WINDOWS = (2, 4, 8, 16)
N_GROUPS = len(POOL_WINDOWS)
N_SHARDS = 4
N_DEVICES = 8
ADAM_LR = 0.001
ADAM_B1 = 0.9
ADAM_B2 = 0.999
ADAM_EPS = 1e-08
ADAM_WD = 0.01
ADAM_STEP = 10

LANES = 128
SUBLANES_BF16 = 16
HALO_A = 8
HALO_B = 32
HALO_P = 16
SHIFTS = 8
ROW_BLOCK = 32
LANE_BLOCK = 256
VMEM_LIMIT = 56 * 1024 * 1024
TN_ACC_BYTES = 8 * 1024 * 1024
EVEN_BWD_ROWS = 128


def _row_tile(n, pref):
    t = max(min(n, pref) // SUBLANES_BF16, 1) * SUBLANES_BF16
    while t > SUBLANES_BF16 and (n % t or t % SUBLANES_BF16):
        t -= SUBLANES_BF16
    assert n % t == 0, (n, pref)
    return t


def _col_chunk(n, pref):
    t = (min(n, pref) // LANES) * LANES
    while t > LANES and n % t:
        t -= LANES
    assert t >= LANES and n % t == 0, (n, pref)
    return t


def _params(*sem):
    return pltpu.CompilerParams(dimension_semantics=tuple(sem) if sem else None, vmem_limit_bytes=VMEM_LIMIT)


ANY = pl.BlockSpec(memory_space=pl.ANY)


class _Comm:
    def __init__(self, srcs, bufs, n_sems, phases, finish):
        self.srcs, self.bufs, self.n_sems, self.phases, self.finish = list(srcs), list(bufs), n_sems, phases, finish


def _call(body, *, name, grid, in_specs, out_specs, out_shape, args, scratch_shapes=(), comm=None):
    params = _params(*(("arbitrary",) * len(grid)))
    if comm is None:
        out = pl.pallas_call(body, name=name, grid=grid, in_specs=in_specs, out_specs=out_specs, out_shape=out_shape,
                             scratch_shapes=scratch_shapes, compiler_params=params)(*args)
        return list(out), []
    n_in, n_out, n_scr = len(in_specs), len(out_specs), len(scratch_shapes)
    ns, nb = len(comm.srcs), len(comm.bufs)
    total = 1
    for size in grid:
        total *= size

    def fused(*refs):
        ins, srcs = refs[:n_in], refs[n_in:n_in + ns]
        at = n_in + ns + nb
        outs, bufs = refs[at:at + n_out], refs[at + n_out:at + n_out + nb]
        scratch = refs[at + n_out + nb:at + n_out + nb + n_scr]
        send_sems, recv_sems = refs[-2:]
        step = 0
        for axis, size in enumerate(grid):
            step = step * size + pl.program_id(axis)
        for when, fn in comm.phases:
            pl.when(step == when)(functools.partial(fn, srcs, bufs, send_sems, recv_sems))
        body(*ins, *outs, *scratch)
        pl.when(step == total - 1)(functools.partial(comm.finish, srcs, bufs, send_sems, recv_sems))

    out = pl.pallas_call(
        fused, name=name, grid=grid,
        in_specs=list(in_specs) + [ANY] * (ns + nb), out_specs=list(out_specs) + [ANY] * nb,
        out_shape=list(out_shape) + [jax.ShapeDtypeStruct(b.shape, b.dtype) for b in comm.bufs],
        input_output_aliases={n_in + ns + i: n_out + i for i in range(nb)},
        scratch_shapes=list(scratch_shapes) + [pltpu.SemaphoreType.DMA((comm.n_sems,))] * 2,
        compiler_params=params,
    )(*args, *comm.srcs, *comm.bufs)
    return list(out[:n_out]), list(out[n_out:])


def _comm_only(comm, name):
    ns, nb = len(comm.srcs), len(comm.bufs)

    def body(*refs):
        srcs, bufs = refs[:ns], refs[ns + nb:ns + 2 * nb]
        send_sems, recv_sems = refs[-2:]
        for _, fn in comm.phases:
            fn(srcs, bufs, send_sems, recv_sems)
        comm.finish(srcs, bufs, send_sems, recv_sems)

    return pl.pallas_call(
        body, name=name, in_specs=[ANY] * (ns + nb), out_specs=[ANY] * nb,
        out_shape=[jax.ShapeDtypeStruct(b.shape, b.dtype) for b in comm.bufs],
        input_output_aliases={ns + i: i for i in range(nb)},
        scratch_shapes=[pltpu.SemaphoreType.DMA((comm.n_sems,))] * 2,
    )(*comm.srcs, *comm.bufs)


def _sigmoid(v):
    return jax.nn.sigmoid(v)


def _dsilu(v, s):
    return s * (1.0 + v * (1.0 - s))


def _mean_last(v):
    return jnp.mean(v, axis=-1, keepdims=True)


def _sum_rows(v):
    return jnp.sum(v, axis=0, keepdims=True)


def _norm_matmul(x, gain, w_sm, name):
    s, d = x.shape
    n_sh, _, ns = w_sm.shape
    tm = _row_tile(s, 1024)

    def body(x_ref, g_ref, w_ref, p_ref, h_ref):
        @pl.when(pl.program_id(1) == 0)
        def _():
            xv = x_ref[...]
            r = lax.rsqrt(_mean_last(xv * xv) + EPS)
            h_ref[...] = (xv * r * g_ref[...]).astype(BF16)

        p_ref[...] = jnp.dot(h_ref[...], w_ref[0], preferred_element_type=F32)

    return _call(
        body, name=name, grid=(s // tm, n_sh),
        in_specs=[pl.BlockSpec((tm, d), lambda i, j: (i, 0)),
                  pl.BlockSpec((1, d), lambda i, j: (0, 0)),
                  pl.BlockSpec((1, d, ns), lambda i, j: (j, 0, 0))],
        out_specs=[pl.BlockSpec((tm, ns), lambda i, j: (i, j)),
                   pl.BlockSpec((tm, d), lambda i, j: (i, 0))],
        out_shape=[jax.ShapeDtypeStruct((s, n_sh * ns), F32), jax.ShapeDtypeStruct((s, d), BF16)],
        args=(x, gain, w_sm))[0]


def _matmul_post_loss(u, w, x_res, gain, target, name):
    s, k = u.shape
    d = w.shape[1]
    tm = _row_tile(s, 256)

    def body(u_ref, w_ref, x_ref, g_ref, t_ref, dy_ref, dout_ref, dg_ref, loss_ref):
        @pl.when(pl.program_id(0) == 0)
        def _():
            dg_ref[...] = jnp.zeros_like(dg_ref)
            loss_ref[...] = jnp.zeros_like(loss_ref)

        y = jnp.dot(u_ref[...], w_ref[...], preferred_element_type=F32)
        r = lax.rsqrt(_mean_last(y * y) + EPS)
        n = y * r
        g = g_ref[...]
        err = x_ref[...] + n * g - t_ref[...]
        loss_ref[...] += 0.5 * jnp.sum(_mean_last(err * err))
        dout = err * (1.0 / d)
        dout_ref[...] = dout
        dg_ref[...] += _sum_rows(dout * n)
        dn = dout * g
        dy_ref[...] = (r * (dn - n * _mean_last(dn * n))).astype(BF16)

    return pl.pallas_call(
        body, name=name, grid=(s // tm,),
        in_specs=[pl.BlockSpec((tm, k), lambda i: (i, 0)),
                  pl.BlockSpec((k, d), lambda i: (0, 0)),
                  pl.BlockSpec((tm, d), lambda i: (i, 0)),
                  pl.BlockSpec((1, d), lambda i: (0, 0)),
                  pl.BlockSpec((tm, d), lambda i: (i, 0))],
        out_specs=[pl.BlockSpec((tm, d), lambda i: (i, 0)),
                   pl.BlockSpec((tm, d), lambda i: (i, 0)),
                   pl.BlockSpec((1, d), lambda i: (0, 0)),
                   pl.BlockSpec((8, LANES), lambda i: (0, 0))],
        out_shape=[jax.ShapeDtypeStruct((s, d), BF16), jax.ShapeDtypeStruct((s, d), F32),
                   jax.ShapeDtypeStruct((1, d), F32), jax.ShapeDtypeStruct((8, LANES), F32)],
        compiler_params=_params("arbitrary"),
    )(u, w, x_res, gain, target)


def _matmul_nt(a, w_sm, name, comm=None):
    s, ncols = a.shape
    n_sh, r, ns = w_sm.shape
    assert ncols == n_sh * ns
    tm = _row_tile(s, 1024)
    nc = _col_chunk(ns, 1792)
    per = ns // nc
    steps = n_sh * per

    def body(a_ref, w_ref, o_ref):
        part = lax.dot_general(a_ref[...], w_ref[0], (((1,), (1,)), ((), ())), preferred_element_type=F32)

        @pl.when(pl.program_id(1) == 0)
        def _():
            o_ref[...] = part

        @pl.when(pl.program_id(1) > 0)
        def _():
            o_ref[...] += part

    out, bufs = _call(
        body, name=name, grid=(s // tm, steps),
        in_specs=[pl.BlockSpec((tm, nc), lambda i, j: (i, j)),
                  pl.BlockSpec((1, r, nc), lambda i, j: (j // per, 0, j % per))],
        out_specs=[pl.BlockSpec((tm, r), lambda i, j: (i, 0))],
        out_shape=[jax.ShapeDtypeStruct((s, r), F32)],
        args=(a, w_sm), comm=comm)
    return out[0], bufs


def _matmul_tn(a, b, n_sh, name):
    s, k = a.shape
    n = b.shape[1]
    ns = n // n_sh
    tk = _col_chunk(k, TN_ACC_BYTES // (4 * ns))
    ts = _row_tile(s, 1024)
    n_s = s // ts

    def body(a_ref, b_ref, o_ref, acc_ref):
        part = lax.dot_general(a_ref[...], b_ref[...], (((0,), (0,)), ((), ())), preferred_element_type=F32)

        @pl.when(pl.program_id(2) == 0)
        def _():
            acc_ref[...] = part

        @pl.when(pl.program_id(2) > 0)
        def _():
            acc_ref[...] += part

        @pl.when(pl.program_id(2) == n_s - 1)
        def _():
            o_ref[0] = acc_ref[...].astype(BF16)

    return pl.pallas_call(
        body, name=name, grid=(n_sh, k // tk, n_s),
        in_specs=[pl.BlockSpec((ts, tk), lambda j, i, t: (t, i)),
                  pl.BlockSpec((ts, ns), lambda j, i, t: (t, j))],
        out_specs=pl.BlockSpec((1, tk, ns), lambda j, i, t: (j, i, 0)),
        out_shape=jax.ShapeDtypeStruct((n_sh, k, ns), BF16),
        scratch_shapes=[pltpu.VMEM((tk, ns), F32)],
        compiler_params=_params("arbitrary", "arbitrary", "arbitrary"),
    )(a, b)


def _norm_bwd(dh, x, gain, dres, name, post=None):
    s, d = x.shape
    tm = _row_tile(s, 256)
    with_post = post is not None

    def rms_bwd(dout, v, g):
        r = lax.rsqrt(_mean_last(v * v) + EPS)
        n = v * r
        dn = dout * g
        return r * (dn - n * _mean_last(dn * n)), _sum_rows(dout * n)

    def body(*refs):
        if with_post:
            dh_ref, x_ref, g_ref, dres_ref, y_ref, gp_ref, dx_ref, dg_ref, dy_ref, dgp_ref = refs
        else:
            dh_ref, x_ref, g_ref, dres_ref, dx_ref, dg_ref = refs

        @pl.when(pl.program_id(0) == 0)
        def _():
            dg_ref[...] = jnp.zeros_like(dg_ref)
            if with_post:
                dgp_ref[...] = jnp.zeros_like(dgp_ref)

        dv, dg = rms_bwd(dh_ref[...], x_ref[...], g_ref[...])
        dx = dres_ref[...] + dv
        dx_ref[...] = dx
        dg_ref[...] += dg
        if with_post:
            dy, dgp = rms_bwd(dx, y_ref[...], gp_ref[...])
            dy_ref[...] = dy.astype(BF16)
            dgp_ref[...] += dgp

    row = pl.BlockSpec((tm, d), lambda i: (i, 0))
    vec = pl.BlockSpec((1, d), lambda i: (0, 0))
    in_specs = [row, row, vec, row]
    out_specs = [row, vec]
    out_shape = [jax.ShapeDtypeStruct((s, d), F32), jax.ShapeDtypeStruct((1, d), F32)]
    args = [dh, x, gain, dres]
    if with_post:
        in_specs += [row, vec]
        out_specs += [row, vec]
        out_shape += [jax.ShapeDtypeStruct((s, d), BF16), jax.ShapeDtypeStruct((1, d), F32)]
        args += list(post)
    return _call(body, name=name, grid=(s // tm,), in_specs=in_specs, out_specs=out_specs, out_shape=out_shape,
                 args=args)[0]


def _fill_shifted_down(sh, rows):
    for b in range(1, SHIFTS):
        sh[b, SHIFTS:rows, :] = sh[0, SHIFTS - b:rows - b, :]


def _fill_shifted_up(sh, rows):
    for b in range(1, SHIFTS):
        sh[b, 0:rows - SHIFTS, :] = sh[0, b:rows - SHIFTS + b, :]


def _for_blocks(ts, w, fn):
    lb = min(LANE_BLOCK, w)
    for l0 in range(0, w, lb):
        def rows(rb, carry, l0=l0):
            fn(pl.multiple_of(rb * ROW_BLOCK, ROW_BLOCK), slice(l0, l0 + lb))
            return carry

        lax.fori_loop(0, ts // ROW_BLOCK, rows, 0)


TAP_SPAN = SHIFTS * ((CONV_B - 1) // SHIFTS)
WINDOW = ROW_BLOCK + TAP_SPAN


def _taps_of(b):
    return [(a, SHIFTS * a + b) for a in range((CONV_B - 1 - b) // SHIFTS + 1)]


def _conv31(sh, base, step, wt_ref, bias_ref, out_ref, ts, w):
    low = min(0, step * (TAP_SPAN // SHIFTS))

    def block(r0, lanes):
        acc = [jnp.zeros((SHIFTS, lanes.stop - lanes.start), F32) for _ in range(ROW_BLOCK // SHIFTS)]
        for b in range(SHIFTS):
            window = sh[b, pl.ds(pl.multiple_of(r0 + (base + low), SHIFTS), WINDOW), lanes]
            for a, j in (_taps_of(b) if step > 0 else reversed(_taps_of(b))):
                at = step * a - low
                wt = wt_ref[CONV_B - 1 - j, :, lanes]
                acc = [v + wt * window[at + SHIFTS * r:at + SHIFTS * (r + 1), :] for r, v in enumerate(acc)]
        for r, v in enumerate(acc):
            if bias_ref is not None:
                v = v + bias_ref[:, lanes]
            out_ref[pl.ds(pl.multiple_of(r0 + SHIFTS * r, SHIFTS), SHIFTS), lanes] = v

    _for_blocks(ts, w, block)


def _conv31_weight_grad(d_sh, x_sh, wacc, ts, w):
    def block(r0, lanes):
        d = d_sh[0, pl.ds(r0, ROW_BLOCK), lanes]
        for b in range(SHIFTS):
            window = x_sh[b, pl.ds(pl.multiple_of(r0 + (HALO_B - TAP_SPAN), SHIFTS), WINDOW), lanes]
            for a, j in _taps_of(b):
                at = TAP_SPAN - SHIFTS * a
                prod = d * window[at:at + ROW_BLOCK, :]
                part = prod[0:SHIFTS, :]
                for q in range(1, ROW_BLOCK // SHIFTS):
                    part = part + prod[q * SHIFTS:(q + 1) * SHIFTS, :]
                wacc[CONV_B - 1 - j, :, lanes] += part

    _for_blocks(ts, w, block)


def _even_forward_tile(p_ref, halo_ref, first, a_conv_ref, b_conv_ref, bias_ref, lng_ref, lnb_ref, qbuf, ysh, y1buf,
                       wb, w, ts):
    @pl.when(pl.program_id(0) == 0)
    def _():
        for k in range(CONV_B):
            wb[k] = jnp.broadcast_to(b_conv_ref[k:k + 1, :], (SHIFTS, w))

    def col(ref, k, rows=slice(None)):
        return ref[rows, k * w:(k + 1) * w]

    a_x, a_b, a_c, a_z = col(p_ref, 0), col(p_ref, 1), col(p_ref, 2), col(p_ref, 3)
    b_val, b_gate, b_z = col(p_ref, 4), col(p_ref, 5), col(p_ref, 6)
    keep = jnp.where(first, 0.0, 1.0)

    rows_a = slice(HALO_B - HALO_A, HALO_B)
    qbuf[0:HALO_A, :] = col(halo_ref, 2, rows_a) * col(halo_ref, 0, rows_a) * keep
    qbuf[HALO_A:HALO_A + ts, :] = a_c * a_x
    cq = jnp.zeros((ts, w), F32)
    for j in range(CONV_A):
        cq = cq + a_conv_ref[CONV_A - 1 - j:CONV_A - j, :] * qbuf[HALO_A - j:HALO_A - j + ts, :]
    ya = a_b * cq

    ysh[0, 0:HALO_B, :] = col(halo_ref, 4) * _sigmoid(col(halo_ref, 5)) * keep
    ysh[0, HALO_B:HALO_B + ts, :] = b_val * _sigmoid(b_gate)
    _fill_shifted_down(ysh, HALO_B + ts)
    _conv31(ysh, HALO_B, -SHIFTS, wb, bias_ref, y1buf, ts, w)
    yb1 = y1buf[...]
    xc = yb1 - _mean_last(yb1)
    rstd = lax.rsqrt(_mean_last(xc * xc) + EPS)
    xhat = xc * rstd
    yb2 = xhat * lng_ref[...] + lnb_ref[...]
    return dict(a_x=a_x, a_b=a_b, a_c=a_c, a_z=a_z, b_val=b_val, b_gate=b_gate, b_z=b_z,
                cq=cq, ya=ya, rstd=rstd, xhat=xhat, yb2=yb2)


def _even_specs(s, w, ts):
    tile = pl.BlockSpec((ts, 7 * w), lambda i: (i, 0))
    halo = pl.BlockSpec((HALO_B, 7 * w), lambda i: (jnp.maximum(i * (ts // HALO_B) - 1, 0), 0))
    return tile, halo


def _small_specs(shapes, index=lambda i: (0, 0)):
    return [pl.BlockSpec(sh, index) for sh in shapes]


def _even_mixer_fwd(p, w_out, x_res, gain, a_conv, b_conv, bias, ln_g, ln_b, name, comm=None):
    s, d = x_res.shape
    w = p.shape[1] // 7
    ts = _row_tile(s, 128)
    assert ts % HALO_B == 0

    def body(p_ref, halo_ref, wout_ref, x_ref, g_ref, ac_ref, bc_ref, bias_ref, lng_ref, lnb_ref,
             u_ref, xn_ref, y_ref, qbuf, ysh, y1buf, wb):
        first = pl.program_id(0) == 0
        f = _even_forward_tile(p_ref, halo_ref, first, ac_ref, bc_ref, bias_ref, lng_ref, lnb_ref, qbuf, ysh, y1buf,
                               wb, w, ts)
        yb3 = f["yb2"] * _sigmoid(f["yb2"])
        u_a = (f["ya"] * (f["a_z"] * _sigmoid(f["a_z"]))).astype(BF16)
        u_b = (yb3 * (f["b_z"] * _sigmoid(f["b_z"]))).astype(BF16)
        u_ref[:, 0:w] = u_a
        u_ref[:, w:2 * w] = u_b
        y = (jnp.dot(u_a, wout_ref[0:w, :], preferred_element_type=F32)
             + jnp.dot(u_b, wout_ref[w:2 * w, :], preferred_element_type=F32))
        r = lax.rsqrt(_mean_last(y * y) + EPS)
        y_ref[...] = y
        xn_ref[...] = x_ref[...] + (y * r) * g_ref[...]

    tile, halo = _even_specs(s, w, ts)
    row = pl.BlockSpec((ts, d), lambda i: (i, 0))
    return _call(
        body, name=name, grid=(s // ts,),
        in_specs=[tile, halo, pl.BlockSpec((2 * w, d), lambda i: (0, 0)), row, pl.BlockSpec((1, d), lambda i: (0, 0))]
        + _small_specs([(CONV_A, w), (CONV_B, w), (1, w), (1, w), (1, w)]),
        out_specs=[pl.BlockSpec((ts, 2 * w), lambda i: (i, 0)), row, row],
        out_shape=[jax.ShapeDtypeStruct((s, 2 * w), BF16), jax.ShapeDtypeStruct((s, d), F32),
                   jax.ShapeDtypeStruct((s, d), F32)],
        scratch_shapes=[pltpu.VMEM((HALO_A + ts, w), F32), pltpu.VMEM((SHIFTS, HALO_B + ts, w), F32),
                        pltpu.VMEM((ts, w), F32), pltpu.VMEM((CONV_B, SHIFTS, w), F32)],
        args=(p, p, w_out, x_res, gain, a_conv, b_conv, bias, ln_g, ln_b), comm=comm)


def _even_mixer_bwd(p, du, a_conv, b_conv, bias, ln_g, ln_b, name, comm=None):
    s = p.shape[0]
    w = p.shape[1] // 7
    ts = _row_tile(s, EVEN_BWD_ROWS)
    nt = s // ts
    assert ts % HALO_B == 0

    def body(p_ref, halo_ref, du_ref, ac_ref, bc_ref, bias_ref, lng_ref, lnb_ref,
             dp_ref, dac_ref, dbc_ref, dbias_ref, dlng_ref, dlnb_ref,
             qbuf, ysh, y1buf, dqbuf, dsh, dy0buf, wacc, carry_dq, carry_dy, wb):
        step = pl.program_id(0)
        first = step == nt - 1

        @pl.when(step == 0)
        def _():
            for ref in (dac_ref, dbias_ref, dlng_ref, dlnb_ref, wacc, carry_dq, carry_dy):
                ref[...] = jnp.zeros_like(ref)

        f = _even_forward_tile(p_ref, halo_ref, first, ac_ref, bc_ref, bias_ref, lng_ref, lnb_ref, qbuf, ysh, y1buf,
                               wb, w, ts)
        a_z, b_z, yb2 = f["a_z"], f["b_z"], f["yb2"]
        s_az, s_bz, s_y2 = _sigmoid(a_z), _sigmoid(b_z), _sigmoid(yb2)
        du_a = du_ref[:, 0:w]
        du_b = du_ref[:, w:2 * w]

        d_ya = du_a * (a_z * s_az)
        dp_ref[:, 3 * w:4 * w] = (du_a * f["ya"] * _dsilu(a_z, s_az)).astype(BF16)
        dp_ref[:, 1 * w:2 * w] = (d_ya * f["cq"]).astype(BF16)
        d_cq = d_ya * f["a_b"]
        dqbuf[0:ts, :] = d_cq
        dqbuf[ts:ts + HALO_A, :] = carry_dq[...]
        d_q = jnp.zeros((ts, w), F32)
        for o in range(CONV_A):
            d_q = d_q + ac_ref[CONV_A - 1 - o:CONV_A - o, :] * dqbuf[o:o + ts, :]
        for j in range(CONV_A):
            k = CONV_A - 1 - j
            dac_ref[k:k + 1, :] += _sum_rows(d_cq * qbuf[HALO_A - j:HALO_A - j + ts, :])
        carry_dq[...] = d_cq[0:HALO_A, :]
        dp_ref[:, 2 * w:3 * w] = (d_q * f["a_x"]).astype(BF16)
        dp_ref[:, 0 * w:1 * w] = (d_q * f["a_c"]).astype(BF16)

        d_yb3 = du_b * (b_z * s_bz)
        dp_ref[:, 6 * w:7 * w] = (du_b * (yb2 * s_y2) * _dsilu(b_z, s_bz)).astype(BF16)
        d_yb2 = d_yb3 * _dsilu(yb2, s_y2)
        xhat = f["xhat"]
        dlng_ref[...] += _sum_rows(d_yb2 * xhat)
        dlnb_ref[...] += _sum_rows(d_yb2)
        d_xh = d_yb2 * lng_ref[...]
        d_yb1 = f["rstd"] * (d_xh - _mean_last(d_xh) - xhat * _mean_last(d_xh * xhat))
        dbias_ref[...] += _sum_rows(d_yb1)
        dsh[0, 0:ts, :] = d_yb1
        dsh[0, ts:ts + HALO_B, :] = carry_dy[...]
        carry_dy[...] = d_yb1[0:HALO_B, :]
        _fill_shifted_up(dsh, ts + HALO_B)
        _conv31(dsh, 0, SHIFTS, wb, None, dy0buf, ts, w)
        _conv31_weight_grad(dsh, ysh, wacc, ts, w)

        @pl.when(step == nt - 1)
        def _():
            for k in range(CONV_B):
                dbc_ref[k:k + 1, :] = _sum_rows(wacc[k])

        d_yb0 = dy0buf[...]
        s_g = _sigmoid(f["b_gate"])
        dp_ref[:, 4 * w:5 * w] = (d_yb0 * s_g).astype(BF16)
        dp_ref[:, 5 * w:6 * w] = (d_yb0 * f["b_val"] * s_g * (1.0 - s_g)).astype(BF16)

    rev = lambda i: (nt - 1 - i, 0)
    tile = pl.BlockSpec((ts, 7 * w), rev)
    halo = pl.BlockSpec((HALO_B, 7 * w), lambda i: (jnp.maximum((nt - 1 - i) * (ts // HALO_B) - 1, 0), 0))
    small = [(CONV_A, w), (CONV_B, w), (1, w), (1, w), (1, w)]
    return _call(
        body, name=name, grid=(nt,),
        in_specs=[tile, halo, pl.BlockSpec((ts, 2 * w), rev)] + _small_specs(small),
        out_specs=[pl.BlockSpec((ts, 7 * w), rev)] + _small_specs(small),
        out_shape=[jax.ShapeDtypeStruct((s, 7 * w), BF16)] + [jax.ShapeDtypeStruct(sh, F32) for sh in small],
        scratch_shapes=[pltpu.VMEM((HALO_A + ts, w), F32), pltpu.VMEM((SHIFTS, HALO_B + ts, w), F32),
                        pltpu.VMEM((ts, w), F32),
                        pltpu.VMEM((ts + HALO_A, w), F32), pltpu.VMEM((SHIFTS, ts + HALO_B, w), F32),
                        pltpu.VMEM((ts, w), F32), pltpu.VMEM((CONV_B, SHIFTS, w), F32),
                        pltpu.VMEM((HALO_A, w), F32), pltpu.VMEM((HALO_B, w), F32),
                        pltpu.VMEM((CONV_B, SHIFTS, w), F32)],
        args=(p, p, du, a_conv, b_conv, bias, ln_g, ln_b), comm=comm)


def _pool_forward_tile(p_ref, halo_ref, first, tile_index, cw_ref, cb_ref, cs_ref, vbuf, c, gc, ts):
    vbuf[0:HALO_P, :] = halo_ref[...] * jnp.where(first, 0.0, 1.0)
    vbuf[HALO_P:HALO_P + ts, :] = p_ref[:, 0:c]
    pos = tile_index * ts + lax.broadcasted_iota(jnp.int32, (ts, 1), 0) + 1
    pooled, inv, gout = [], [], []
    for g, win in enumerate(POOL_WINDOWS):
        cols = slice(g * gc, (g + 1) * gc)
        acc = jnp.zeros((ts, gc), F32)
        for j in range(win):
            acc = acc + vbuf[HALO_P - j:HALO_P - j + ts, cols]
        inv_g = 1.0 / jnp.minimum(pos, win).astype(F32)
        pooled_g = (acc * inv_g - p_ref[:, cols]).astype(BF16)
        pooled.append(pooled_g)
        inv.append(inv_g)
        gout.append(jnp.dot(pooled_g, cw_ref[g], preferred_element_type=F32) + cb_ref[:, cols])
    return pooled, inv, gout


def _odd_mixer_fwd(p, cw, cb, cs, name):
    s = p.shape[0]
    c = p.shape[1] // 2
    gc = c // N_GROUPS
    ts = _row_tile(s, 256)

    def body(p_ref, halo_ref, cw_ref, cb_ref, cs_ref, u_ref, vbuf):
        i = pl.program_id(0)
        _, _, gout = _pool_forward_tile(p_ref, halo_ref, i == 0, i, cw_ref, cb_ref, cs_ref, vbuf, c, gc, ts)
        for g in range(N_GROUPS):
            cols = slice(g * gc, (g + 1) * gc)
            z = p_ref[:, c + g * gc:c + (g + 1) * gc]
            u_ref[:, cols] = (gout[g] * cs_ref[:, cols] * (z * _sigmoid(z))).astype(BF16)

    return pl.pallas_call(
        body, name=name, grid=(s // ts,),
        in_specs=[pl.BlockSpec((ts, 2 * c), lambda i: (i, 0)),
                  pl.BlockSpec((HALO_P, c), lambda i: (jnp.maximum(i * (ts // HALO_P) - 1, 0), 0)),
                  pl.BlockSpec((N_GROUPS, gc, gc), lambda i: (0, 0, 0)),
                  pl.BlockSpec((1, c), lambda i: (0, 0)), pl.BlockSpec((1, c), lambda i: (0, 0))],
        out_specs=pl.BlockSpec((ts, c), lambda i: (i, 0)),
        out_shape=jax.ShapeDtypeStruct((s, c), BF16),
        scratch_shapes=[pltpu.VMEM((HALO_P + ts, c), F32)],
        compiler_params=_params("arbitrary"),
    )(p, p, cw, cb, cs)


def _odd_mixer_bwd(p, du, cw, cb, cs, name):
    s = p.shape[0]
    c = p.shape[1] // 2
    gc = c // N_GROUPS
    ts = _row_tile(s, 256)
    nt = s // ts

    def body(p_ref, halo_ref, du_ref, cw_ref, cb_ref, cs_ref, dp_ref, dcw_ref, dcb_ref, dcs_ref, vbuf, ebuf, carry_e):
        step = pl.program_id(0)
        tile_index = nt - 1 - step

        @pl.when(step == 0)
        def _():
            for ref in (dcw_ref, dcb_ref, dcs_ref, carry_e):
                ref[...] = jnp.zeros_like(ref)

        pooled, inv, gout = _pool_forward_tile(p_ref, halo_ref, tile_index == 0, tile_index, cw_ref, cb_ref, cs_ref,
                                               vbuf, c, gc, ts)
        ebuf[ts:ts + HALO_P, :] = carry_e[...]
        for g, win in enumerate(POOL_WINDOWS):
            cols = slice(g * gc, (g + 1) * gc)
            z = p_ref[:, c + g * gc:c + (g + 1) * gc]
            sz = _sigmoid(z)
            du_g = du_ref[:, cols]
            scale = cs_ref[:, cols]
            d_y = du_g * (z * sz)
            dp_ref[:, c + g * gc:c + (g + 1) * gc] = (du_g * (gout[g] * scale) * _dsilu(z, sz)).astype(BF16)
            dcs_ref[:, cols] += _sum_rows(d_y * gout[g])
            d_gout = d_y * scale
            dcb_ref[:, cols] += _sum_rows(d_gout)
            d_gout_b = d_gout.astype(BF16)
            dcw_ref[g] += lax.dot_general(pooled[g], d_gout_b, (((0,), (0,)), ((), ())), preferred_element_type=F32)
            d_pool = lax.dot_general(d_gout_b, cw_ref[g], (((1,), (1,)), ((), ())), preferred_element_type=F32)
            e = d_pool * inv[g]
            ebuf[0:ts, cols] = e
            d_v = -d_pool
            for o in range(win):
                d_v = d_v + ebuf[o:o + ts, cols]
            dp_ref[:, cols] = d_v.astype(BF16)
            carry_e[:, cols] = e[0:HALO_P, :]

    rev = lambda i: (nt - 1 - i, 0)
    small = [(N_GROUPS, gc, gc), (1, c), (1, c)]
    return pl.pallas_call(
        body, name=name, grid=(nt,),
        in_specs=[pl.BlockSpec((ts, 2 * c), rev),
                  pl.BlockSpec((HALO_P, c), lambda i: (jnp.maximum((nt - 1 - i) * (ts // HALO_P) - 1, 0), 0)),
                  pl.BlockSpec((ts, c), rev),
                  pl.BlockSpec((N_GROUPS, gc, gc), lambda i: (0, 0, 0)),
                  pl.BlockSpec((1, c), lambda i: (0, 0)), pl.BlockSpec((1, c), lambda i: (0, 0))],
        out_specs=[pl.BlockSpec((ts, 2 * c), rev),
                   pl.BlockSpec((N_GROUPS, gc, gc), lambda i: (0, 0, 0)),
                   pl.BlockSpec((1, c), lambda i: (0, 0)), pl.BlockSpec((1, c), lambda i: (0, 0))],
        out_shape=[jax.ShapeDtypeStruct((s, 2 * c), BF16)] + [jax.ShapeDtypeStruct(sh, F32) for sh in small],
        scratch_shapes=[pltpu.VMEM((HALO_P + ts, c), F32), pltpu.VMEM((ts + HALO_P, c), F32),
                        pltpu.VMEM((HALO_P, c), F32)],
        compiler_params=_params("arbitrary"),
    )(p, p, du, cw, cb, cs)


def _cast_into_slot(a, coords, name):
    r, cols = a.shape
    tr = _row_tile(r // 2, 256)
    per = r // 2 // tr

    def body(co_ref, a_ref, o_ref):
        o_ref[0, 0] = a_ref[...].astype(BF16)

    return pl.pallas_call(
        body, name=name,
        grid_spec=pltpu.PrefetchScalarGridSpec(
            num_scalar_prefetch=1, grid=(2, per),
            in_specs=[pl.BlockSpec((tr, cols), lambda h, i, co: (h * per + i, 0))],
            out_specs=pl.BlockSpec((1, 1, tr, cols), lambda h, i, co: (co[0], h, i, 0))),
        out_shape=jax.ShapeDtypeStruct((N_SHARDS, 2, r // 2, cols), BF16),
        compiler_params=_params("arbitrary", "arbitrary"),
    )(coords, a)


def _chip_sum(g, other, coords, name):
    n_sh, _, r2, cols = g.shape
    tr = _row_tile(r2, 256)

    def body(co_ref, g_ref, o_ref, sum_ref, mine_ref):
        v = (g_ref[0, 0].astype(F32) + o_ref[0].astype(F32)).astype(BF16)
        sum_ref[0] = v

        @pl.when(pl.program_id(1) == co_ref[0])
        def _():
            mine_ref[0] = v

    piece = pl.BlockSpec((1, tr, cols), lambda i, s, co: (s, i, 0))
    return pl.pallas_call(
        body, name=name,
        grid_spec=pltpu.PrefetchScalarGridSpec(
            num_scalar_prefetch=1, grid=(r2 // tr, n_sh),
            in_specs=[pl.BlockSpec((1, 1, tr, cols), lambda i, s, co: (s, co[1], i, 0)), piece],
            out_specs=[piece, pl.BlockSpec((1, tr, cols), lambda i, s, co: (co[0], i, 0))]),
        out_shape=[jax.ShapeDtypeStruct((n_sh, r2, cols), BF16)] * 2,
        compiler_params=_params("arbitrary", "arbitrary"),
    )(coords, g, other)


def _shard_sum(pieces, coords, name):
    n_sh, r2, cols = pieces.shape
    tr = _row_tile(r2, 256)

    def body(co_ref, p_ref, o_ref):
        acc = p_ref[0].astype(F32)
        for k in range(1, n_sh):
            acc = acc + p_ref[k].astype(F32)
        o_ref[0] = acc

    return pl.pallas_call(
        body, name=name,
        grid_spec=pltpu.PrefetchScalarGridSpec(
            num_scalar_prefetch=1, grid=(r2 // tr,),
            in_specs=[pl.BlockSpec((n_sh, tr, cols), lambda i, co: (0, i, 0))],
            out_specs=pl.BlockSpec((1, tr, cols), lambda i, co: (co[1], i, 0))),
        out_shape=jax.ShapeDtypeStruct((2, r2, cols), F32),
        compiler_params=_params("arbitrary"),
    )(coords, pieces)


def _sum_small(a, name):
    n, r, cols = a.shape

    def body(a_ref, o_ref):
        acc = a_ref[0]
        for k in range(1, n):
            acc = acc + a_ref[k]
        o_ref[...] = acc

    return pl.pallas_call(
        body, name=name,
        in_specs=[pl.BlockSpec((n, r, cols), lambda: (0, 0, 0))],
        out_specs=pl.BlockSpec((r, cols), lambda: (0, 0)),
        out_shape=jax.ShapeDtypeStruct((r, cols), F32),
        compiler_params=_params(),
    )(a)


def _adamw(w, g, m, v, name):
    r, cols = w.shape
    tr = _row_tile(r, 256) if r % SUBLANES_BF16 == 0 else r

    def body(w_ref, g_ref, m_ref, v_ref, d_ref, nm_ref, nv_ref):
        g = g_ref[...]
        m = ADAM_B1 * m_ref[...] + (1.0 - ADAM_B1) * g
        v = ADAM_B2 * v_ref[...] + (1.0 - ADAM_B2) * (g * g)
        m_hat = m / (1.0 - ADAM_B1 ** ADAM_STEP)
        v_hat = v / (1.0 - ADAM_B2 ** ADAM_STEP)
        d_ref[...] = -ADAM_LR * (m_hat / (jnp.sqrt(v_hat) + ADAM_EPS) + ADAM_WD * w_ref[...])
        nm_ref[...] = m
        nv_ref[...] = v

    blk = pl.BlockSpec((tr, cols), lambda i: (i, 0))
    return pl.pallas_call(
        body, name=name, grid=(r // tr,),
        in_specs=[blk] * 4, out_specs=[blk] * 3,
        out_shape=[jax.ShapeDtypeStruct((r, cols), F32)] * 3,
        compiler_params=_params("arbitrary"),
    )(w, g, m, v)


def _place():
    x, y, c = lax.axis_index("x"), lax.axis_index("y"), lax.axis_index("c")
    other_chips = [(1 - x, y), (x, 1 - y), (1 - x, 1 - y)]
    return x, y, c, other_chips


def _chip(xy):
    return 2 * xy[0] + xy[1]


def _remote(src, dst, send_sem, recv_sem, to):
    return pltpu.make_async_remote_copy(src_ref=src, dst_ref=dst, send_sem=send_sem, recv_sem=recv_sem,
                                        device_id=to, device_id_type=MESH)


def _gather_ici(ctx, k, j, start):
    (x, y, c, chips), b, send, recv = ctx
    blk = b[k].at[_chip((x, y)) if start else _chip(chips[j]), c]
    return _remote(blk, blk, send.at[6 * k + j], recv.at[6 * k + j], (*chips[j], c))


def _gather_d2d(ctx, k, j, start):
    (x, y, c, chips), b, send, recv = ctx
    blk = b[k].at[_chip(chips[j]), c if start else 1 - c]
    return _remote(blk, blk, send.at[6 * k + 3 + j], recv.at[6 * k + 3 + j], (x, y, 1 - c))


def _gather_small(ctx, n, j, start):
    (x, y, c, chips), b, send, recv = ctx
    blk = b[n].at[_chip((x, y)) if start else _chip(chips[j])]
    return _remote(blk, blk, send.at[6 * n + j], recv.at[6 * n + j], (*chips[j], c))


def _gather_comm(bufs, small, forward_at):
    n = len(bufs)
    ici, d2d = _gather_ici, _gather_d2d

    def small_copy(ctx, j, start):
        return _gather_small(ctx, n, j, start)

    def start(srcs, b, send, recv):
        ctx = (_place(), b, send, recv)
        for k in range(n):
            for j in range(3):
                ici(ctx, k, j, True).start()
        if small is not None:
            for j in range(3):
                small_copy(ctx, j, True).start()

    def forward(srcs, b, send, recv):
        ctx = (_place(), b, send, recv)
        for k in range(n):
            for j in range(3):
                ici(ctx, k, j, False).wait_recv()
                d2d(ctx, k, j, True).start()

    def finish(srcs, b, send, recv):
        ctx = (_place(), b, send, recv)
        if small is not None:
            for j in range(3):
                small_copy(ctx, j, False).wait_recv()
                small_copy(ctx, j, True).wait_send()
        for k in range(n):
            for j in range(3):
                d2d(ctx, k, j, False).wait_recv()
                ici(ctx, k, j, True).wait_send()
                d2d(ctx, k, j, True).wait_send()

    all_bufs = list(bufs) + ([small] if small is not None else [])
    return _Comm([], all_bufs, 6 * n + 3, [(0, start), (forward_at, forward)], finish)


def _rmsnorm(x, gain, name):
    s, d = x.shape
    tm = _row_tile(s, 512)

    def body(x_ref, g_ref, h_ref):
        xv = x_ref[...]
        r = lax.rsqrt(_mean_last(xv * xv) + EPS)
        h_ref[...] = (xv * r * g_ref[...]).astype(BF16)

    return pl.pallas_call(
        body, name=name, grid=(s // tm,),
        in_specs=[pl.BlockSpec((tm, d), lambda i: (i, 0)), pl.BlockSpec((1, d), lambda i: (0, 0))],
        out_specs=pl.BlockSpec((tm, d), lambda i: (i, 0)),
        out_shape=jax.ShapeDtypeStruct((s, d), BF16),
        compiler_params=_params("arbitrary"),
    )(x, gain)


def _gathered_in_proj(h, bufs, small, order, name):
    s, d = h.shape
    n_sh, _, r2, ns = bufs[0].shape
    assert d == 2 * r2
    n = len(bufs)
    tm = _row_tile(s, 512)
    n_i = s // tm
    hook_i = max(n_i - 2, 0)
    n_sems = 6 * n + 3

    def body(order_ref, h_ref, *rest):
        p_ref = rest[n + 1]
        b = rest[n + 2:2 * n + 3]
        w_vmem, w_sems, send, recv = rest[2 * n + 3:]
        j, i = pl.program_id(0), pl.program_id(1)
        ctx = (_place(), b, send, recv)

        def fetch(q):
            return pltpu.make_async_copy(b[0].at[order_ref[q]], w_vmem.at[q % 2], w_sems.at[q % 2])

        @pl.when((j == 0) & (i == 0))
        def _():
            for k in range(n):
                for peer in range(3):
                    _gather_ici(ctx, k, peer, True).start()
            for peer in range(3):
                _gather_small(ctx, n, peer, True).start()
            fetch(0).start()
            fetch(0).wait()

        for q in range(1, n_sh):
            @pl.when((j == q - 1) & (i == hook_i))
            def _(q=q):
                _gather_ici(ctx, 0, q - 1, False).wait_recv()
                _gather_d2d(ctx, 0, q - 1, True).start()
                _gather_d2d(ctx, 0, q - 1, False).wait_recv()
                fetch(q).start()
                if q == n_sh - 1:
                    for k in range(1, n):
                        for peer in range(3):
                            _gather_ici(ctx, k, peer, False).wait_recv()
                            _gather_d2d(ctx, k, peer, True).start()

            @pl.when((j == q) & (i == 0))
            def _(q=q):
                fetch(q).wait()

        wv = w_vmem.at[j % 2]
        p_ref[...] = (jnp.dot(h_ref[:, 0:r2], wv[0], preferred_element_type=F32)
                      + jnp.dot(h_ref[:, r2:d], wv[1], preferred_element_type=F32))

        @pl.when((j == n_sh - 1) & (i == n_i - 1))
        def _():
            for peer in range(3):
                _gather_small(ctx, n, peer, False).wait_recv()
                _gather_small(ctx, n, peer, True).wait_send()
            for k in range(n):
                for peer in range(3):
                    if k > 0:
                        _gather_d2d(ctx, k, peer, False).wait_recv()
                    _gather_ici(ctx, k, peer, True).wait_send()
                    _gather_d2d(ctx, k, peer, True).wait_send()

    all_bufs = list(bufs) + [small]
    out = pl.pallas_call(
        body, name=name,
        grid_spec=pltpu.PrefetchScalarGridSpec(
            num_scalar_prefetch=1, grid=(n_sh, n_i),
            in_specs=[pl.BlockSpec((tm, d), lambda j, i, o: (i, 0))] + [ANY] * (n + 1),
            out_specs=[pl.BlockSpec((tm, ns), lambda j, i, o: (i, o[j]))] + [ANY] * (n + 1),
            scratch_shapes=[pltpu.VMEM((2, 2, r2, ns), BF16), pltpu.SemaphoreType.DMA((2,)),
                            pltpu.SemaphoreType.DMA((n_sems,)), pltpu.SemaphoreType.DMA((n_sems,))]),
        out_shape=[jax.ShapeDtypeStruct((s, n_sh * ns), F32)]
        + [jax.ShapeDtypeStruct(a.shape, a.dtype) for a in all_bufs],
        input_output_aliases={2 + t: 1 + t for t in range(n + 1)},
        compiler_params=_params("arbitrary", "arbitrary"),
    )(order, h, *all_bufs)
    return out[0], list(out[1:])


def _exchange_comm(grads):
    n = len(grads)
    landing = [lax.empty((N_SHARDS,) + a.shape[2:], a.dtype) for a in grads]

    def copies(srcs, b, send, recv):
        x, y, c, _ = _place()
        return [_remote(srcs[k].at[s, 1 - c], b[k].at[s], send.at[N_SHARDS * k + s], recv.at[N_SHARDS * k + s],
                        (x, y, 1 - c)) for k in range(n) for s in range(N_SHARDS)]

    def start(srcs, b, send, recv):
        for cp in copies(srcs, b, send, recv):
            cp.start()

    def finish(srcs, b, send, recv):
        for cp in copies(srcs, b, send, recv):
            cp.wait()

    return _Comm(grads, landing, N_SHARDS * n, [(0, start)], finish)


def _scatter_comm(chip_sums, landing):
    n = len(chip_sums)

    def big(srcs, b, send, recv, k, j, start):
        x, y, c, chips = _place()
        dst = b[k].at[_chip((x, y)) if start else _chip(chips[j])]
        return _remote(srcs[k].at[_chip(chips[j])], dst, send.at[3 * k + j], recv.at[3 * k + j], (*chips[j], c))

    def start(srcs, b, send, recv):
        for k in range(n):
            for j in range(3):
                big(srcs, b, send, recv, k, j, True).start()

    def finish(srcs, b, send, recv):
        for k in range(n):
            for j in range(3):
                big(srcs, b, send, recv, k, j, False).wait_recv()
                big(srcs, b, send, recv, k, j, True).wait_send()

    return _Comm(chip_sums, landing, 3 * n, [(0, start)], finish)


def _join_comm(halves, small):
    n = len(halves)
    flips = [(fx, fy, fc) for fx in (0, 1) for fy in (0, 1) for fc in (0, 1)][1:]

    def half(b, send, recv, k, start):
        x, y, c, _ = _place()
        return _remote(b[k].at[c], b[k].at[c if start else 1 - c], send.at[k], recv.at[k], (x, y, 1 - c))

    def small_copy(b, send, recv, q, start):
        x, y, c, _ = _place()
        px, py, pc = x ^ flips[q][0], y ^ flips[q][1], c ^ flips[q][2]
        blk = b[n].at[4 * x + 2 * y + c if start else 4 * px + 2 * py + pc]
        return _remote(blk, blk, send.at[n + q], recv.at[n + q], (px, py, pc))

    def start(srcs, b, send, recv):
        for q in range(len(flips)):
            small_copy(b, send, recv, q, True).start()
        for k in range(n):
            half(b, send, recv, k, True).start()

    def finish(srcs, b, send, recv):
        for q in range(len(flips)):
            small_copy(b, send, recv, q, False).wait()
        for k in range(n):
            half(b, send, recv, k, False).wait()

    return _Comm([], list(halves) + [small], n + len(flips), [(0, start)], finish)


def _flat_rows(parts):
    flat = jnp.concatenate([p.reshape(-1) for p in parts])
    assert flat.shape[0] % LANES == 0
    return flat.reshape(-1, LANES)


def _unflatten(flat, shapes):
    out, at = [], 0
    for sh in shapes:
        size = 1
        for dim in sh:
            size *= dim
        out.append(flat[at:at + size].reshape(sh))
        at += size
    assert at == flat.shape[0], (at, flat.shape)
    return out


def _col_shards_to_full(a, rows):
    q = a.shape[1] // rows
    return a.reshape(N_SHARDS, rows, q).transpose(1, 0, 2).reshape(rows, N_SHARDS * q)


def _my_col_shard(full, chip):
    rows, cols = full.shape
    q = cols // N_SHARDS
    return lax.dynamic_index_in_dim(full.reshape(rows, N_SHARDS, q), chip, axis=1, keepdims=False)


def kernel(x, e_norm_pre, e_norm_post, e_w_in, e_a_conv, e_b_conv, e_b_conv_bias, e_b_ln_g, e_b_ln_b, e_w_out, o_norm_pre, o_norm_post, o_w_in, o_c_w, o_c_b, o_c_scale, o_w_out, loss_target, m_e_norm_pre, m_e_norm_post, m_e_w_in, m_e_a_conv, m_e_b_conv, m_e_b_conv_bias, m_e_b_ln_g, m_e_b_ln_b, m_e_w_out, m_o_norm_pre, m_o_norm_post, m_o_w_in, m_o_c_w, m_o_c_b, m_o_c_scale, m_o_w_out, v_e_norm_pre, v_e_norm_post, v_e_w_in, v_e_a_conv, v_e_b_conv, v_e_b_conv_bias, v_e_b_ln_g, v_e_b_ln_b, v_e_w_out, v_o_norm_pre, v_o_norm_post, v_o_w_in, v_o_c_w, v_o_c_b, v_o_c_scale, v_o_w_out):
    _, s, d = x.shape
    w = d // 2
    c = d
    gc = c // N_GROUPS
    wq, cq, gq = w // N_SHARDS, c // N_SHARDS, gc // N_SHARDS
    chip = 2 * lax.axis_index("x") + lax.axis_index("y")
    core = lax.axis_index("c")
    x2 = x.reshape(s, d)
    target = loss_target.reshape(s, d)

    big_w = [e_w_in[0], e_w_out[0], o_w_in[0], o_c_w[0].reshape(N_GROUPS * gq, gc), o_w_out[0]]
    big_m = [m_e_w_in[0], m_e_w_out[0], m_o_w_in[0], m_o_c_w[0].reshape(N_GROUPS * gq, gc), m_o_w_out[0]]
    big_v = [v_e_w_in[0], v_e_w_out[0], v_o_w_in[0], v_o_c_w[0].reshape(N_GROUPS * gq, gc), v_o_w_out[0]]
    coords = jnp.stack([chip, core]).astype(jnp.int32)
    slots = [_cast_into_slot(a, coords, "cast_%d" % k) for k, a in enumerate(big_w)]
    sharded_small = _flat_rows([e_a_conv[0], e_b_conv[0], o_norm_pre, o_norm_post, o_c_scale, o_c_b[0]])
    small_slots = lax.dynamic_update_index_in_dim(jnp.zeros((N_SHARDS,) + sharded_small.shape, F32), sharded_small,
                                                  chip, 0)
    xi, yi = lax.axis_index("x"), lax.axis_index("y")
    order = jnp.stack([chip, 2 * (1 - xi) + yi, 2 * xi + (1 - yi), 2 * (1 - xi) + (1 - yi)]).astype(jnp.int32)
    h0 = _rmsnorm(x2, e_norm_pre, "e_pre_norm")
    p0, (e_w_in_g, e_w_out_g, small_g4) = _gathered_in_proj(h0, slots[:2], small_slots, order, "e_in_proj")
    e_w_in_sm = e_w_in_g.reshape((N_SHARDS,) + big_w[0].shape)
    e_w_out_f = e_w_out_g.reshape(w + w, d)
    sm = small_g4.reshape(N_SHARDS, -1)
    at = [0]

    def take(rows, q):
        blk = sm[:, at[0]:at[0] + rows * q]
        at[0] += rows * q
        return _col_shards_to_full(blk, rows)

    a_conv_f = take(CONV_A, wq)
    b_conv_f = take(CONV_B, wq)
    o_pre_f = take(1, cq)
    o_post_f = take(1, cq)
    cs_f = take(1, cq)
    cb_f = take(N_GROUPS, gq).reshape(1, c)

    mixer_steps = s // _row_tile(s, 128)
    (u0, x1, y0), odd_g = _even_mixer_fwd(p0, e_w_out_f, x2, e_norm_post, a_conv_f, b_conv_f, e_b_conv_bias, e_b_ln_g,
                                          e_b_ln_b, "e_mixer_out_proj",
                                          comm=_gather_comm(slots[2:], None, max(mixer_steps - 4, 0)))
    o_w_in_sm = odd_g[0].reshape((N_SHARDS,) + big_w[2].shape)
    cw_f = odd_g[1].reshape(N_SHARDS, N_GROUPS, gq, gc).transpose(1, 0, 2, 3).reshape(N_GROUPS, gc, gc)
    o_w_out_f = odd_g[2].reshape(c, d)
    p1, h1 = _norm_matmul(x1, o_pre_f, o_w_in_sm, "o_in_proj")
    u1 = _odd_mixer_fwd(p1, cw_f, cb_f, cs_f, "o_mixer_fwd")
    d_y1, d_x2, d_o_post, loss_part = _matmul_post_loss(u1, o_w_out_f, x1, o_post_f, target, "o_out_proj_loss")

    def as_pieces(g, k):
        return g.reshape(N_SHARDS, 2, big_w[k].shape[0] // 2, big_w[k].shape[1])

    def chip_sums(ks, pieces, from_sibling):
        both = [_chip_sum(g, o, coords, "chip_sum_%d" % k) for k, g, o in zip(ks, pieces, from_sibling)]
        return [b[0] for b in both], [b[1] for b in both]

    g_o_w_out = _matmul_tn(u1, d_y1, 1, "o_w_out_grad")
    d_u1, _ = _matmul_nt(d_y1, o_w_out_f[None], "o_out_proj_bwd")
    d_p1, d_cw, d_cb, d_cs = _odd_mixer_bwd(p1, d_u1, cw_f, cb_f, cs_f, "o_mixer_bwd")
    g_o_w_in = _matmul_tn(h1, d_p1, N_SHARDS, "o_w_in_grad")
    g_cw = d_cw.reshape(N_GROUPS, N_SHARDS, gq, gc).transpose(1, 0, 2, 3).astype(BF16)
    pieces_o = [as_pieces(g_o_w_in, 2), as_pieces(g_cw, 3), as_pieces(g_o_w_out, 4)]
    d_h1, sibling_o = _matmul_nt(d_p1, o_w_in_sm, "o_in_proj_bwd", comm=_exchange_comm(pieces_o))
    d_x1, d_o_pre, d_y0, d_e_post = _norm_bwd(d_h1, x1, o_pre_f, d_x2, "o_pre_norm_bwd", post=(y0, e_norm_post))

    pieces_e = [as_pieces(_matmul_tn(u0, d_y0, 1, "e_w_out_grad"), 1)]
    d_u0, sibling_e = _matmul_nt(d_y0, e_w_out_f[None], "e_out_proj_bwd", comm=_exchange_comm(pieces_e))
    sums_a, landing_a = chip_sums([1, 2, 3, 4], pieces_e + pieces_o, sibling_e + sibling_o)
    (d_p0, d_a_conv, d_b_conv, d_bias, d_ln_g, d_ln_b), landed_a = _even_mixer_bwd(
        p0, d_u0, a_conv_f, b_conv_f, e_b_conv_bias, e_b_ln_g, e_b_ln_b, "e_mixer_bwd",
        comm=_scatter_comm(sums_a, landing_a))
    pieces_b = [as_pieces(_matmul_tn(h0, d_p0, N_SHARDS, "e_w_in_grad"), 0)]
    sums_b, landing_b = chip_sums([0], pieces_b, _comm_only(_exchange_comm(pieces_b), "exchange_core_halves"))
    d_h0, landed_b = _matmul_nt(d_p0, e_w_in_sm, "e_in_proj_bwd", comm=_scatter_comm(sums_b, landing_b))
    grad_x, d_e_pre = _norm_bwd(d_h0, x2, e_norm_pre, d_x1, "e_pre_norm_bwd")

    landed = landed_b + landed_a
    reduced = [_shard_sum(sc, coords, "shard_sum_%d" % k) for k, sc in enumerate(landed)]
    small_parts = _flat_rows([loss_part[0], d_e_pre, d_e_post, d_bias, d_ln_g, d_ln_b, d_a_conv, d_b_conv,
                              d_o_pre, d_o_post, d_cs, d_cb])
    small_rows = lax.dynamic_update_index_in_dim(jnp.zeros((N_DEVICES,) + small_parts.shape, F32), small_parts,
                                                 2 * chip + core, 0)
    joined = _comm_only(_join_comm(reduced, small_rows), "join_core_halves")
    big_g = [j.reshape(a.shape) for j, a in zip(joined[:5], big_w)]
    small_sum = _sum_small(joined[5], "small_sum").reshape(-1)
    (loss_row, g_e_pre, g_e_post, g_bias, g_ln_g, g_ln_b, g_a_conv_f, g_b_conv_f, g_o_pre_f, g_o_post_f, g_cs_f,
     g_cb_f) = _unflatten(small_sum, [(LANES,), (1, d), (1, d), (1, w), (1, w), (1, w), (CONV_A, w), (CONV_B, w),
                                      (1, c), (1, c), (1, c), (1, c)])
    loss = loss_row[0]
    g_a_conv = _my_col_shard(g_a_conv_f, chip)
    g_b_conv = _my_col_shard(g_b_conv_f, chip)
    g_o_pre = _my_col_shard(g_o_pre_f, chip)
    g_o_post = _my_col_shard(g_o_post_f, chip)
    g_cs = _my_col_shard(g_cs_f, chip)
    g_cb = _my_col_shard(g_cb_f.reshape(N_GROUPS, gc), chip)

    big_upd = [_adamw(wt, g, m, v, "adamw_%d" % k) for k, (wt, g, m, v) in enumerate(zip(big_w, big_g, big_m, big_v))]
    small_w = [e_norm_pre, e_norm_post, e_b_conv_bias, e_b_ln_g, e_b_ln_b, e_a_conv[0], e_b_conv[0],
               o_norm_pre, o_norm_post, o_c_b[0], o_c_scale]
    small_m = [m_e_norm_pre, m_e_norm_post, m_e_b_conv_bias, m_e_b_ln_g, m_e_b_ln_b, m_e_a_conv[0], m_e_b_conv[0],
               m_o_norm_pre, m_o_norm_post, m_o_c_b[0], m_o_c_scale]
    small_v = [v_e_norm_pre, v_e_norm_post, v_e_b_conv_bias, v_e_b_ln_g, v_e_b_ln_b, v_e_a_conv[0], v_e_b_conv[0],
               v_o_norm_pre, v_o_norm_post, v_o_c_b[0], v_o_c_scale]
    small_g = [g_e_pre, g_e_post, g_bias, g_ln_g, g_ln_b, g_a_conv, g_b_conv, g_o_pre, g_o_post, g_cb, g_cs]
    small_shapes = [a.shape for a in small_w]
    small_upd = _adamw(_flat_rows(small_w), _flat_rows(small_g), _flat_rows(small_m), _flat_rows(small_v),
                       "adamw_small")
    small_delta, small_new_m, small_new_v = [_unflatten(u.reshape(-1), small_shapes) for u in small_upd]

    def ordered(small, big, lead):
        (n_pre, n_post, bias, ln_g, ln_b, a_conv, b_conv, o_pre, o_post, cb, cs) = small
        (w_in, w_out, ow_in, cw, ow_out) = big
        out = [n_pre, n_post, w_in[None], a_conv[None], b_conv[None], bias, ln_g, ln_b, w_out[None], o_pre, o_post,
               ow_in[None], cw.reshape(1, N_GROUPS, gq, gc), cb[None], cs, ow_out[None]]
        return out

    grads = ordered(small_g, big_g, None)
    deltas = ordered(small_delta, [u[0] for u in big_upd], None)
    new_m = ordered(small_new_m, [u[1] for u in big_upd], None)
    new_v = ordered(small_new_v, [u[2] for u in big_upd], None)
    return (loss, grad_x.reshape(1, s, d), *grads, *deltas, *new_m, *new_v)
```

```python
import functools

import jax
import jax.numpy as jnp
from jax import lax
from jax.experimental import pallas as pl
from jax.experimental.pallas import tpu as pltpu

F32 = jnp.float32
BF16 = jnp.bfloat16
MESH = pl.DeviceIdType.MESH

EPS = 1e-6
CONV_A = 3
CONV_B = 31
POOL_WINDOWS = (2, 4, 8, 16)
N_GROUPS = len(POOL_WINDOWS)
N_SHARDS = 4
N_DEVICES = 8
ADAM_LR = 0.001
ADAM_B1 = 0.9
ADAM_B2 = 0.999
ADAM_EPS = 1e-08
ADAM_WD = 0.01
ADAM_STEP = 10

LANES = 128
SUBLANES_BF16 = 16
HALO_A = 8
HALO_B = 32
HALO_P = 16
SHIFTS = 8
ROW_BLOCK = 32
LANE_BLOCK = 256
VMEM_LIMIT = 56 * 1024 * 1024
TN_ACC_BYTES = 8 * 1024 * 1024
EVEN_BWD_ROWS = 128


def _row_tile(n, pref):
    t = max(min(n, pref) // SUBLANES_BF16, 1) * SUBLANES_BF16
    while t > SUBLANES_BF16 and (n % t or t % SUBLANES_BF16):
        t -= SUBLANES_BF16
    assert n % t == 0, (n, pref)
    return t


def _col_chunk(n, pref):
    t = (min(n, pref) // LANES) * LANES
    while t > LANES and n % t:
        t -= LANES
    assert t >= LANES and n % t == 0, (n, pref)
    return t


def _params(*sem):
    return pltpu.CompilerParams(dimension_semantics=tuple(sem) if sem else None, vmem_limit_bytes=VMEM_LIMIT)


ANY = pl.BlockSpec(memory_space=pl.ANY)


class _Comm:
    def __init__(self, srcs, bufs, n_sems, phases, finish):
        self.srcs, self.bufs, self.n_sems, self.phases, self.finish = list(srcs), list(bufs), n_sems, phases, finish


def _call(body, *, name, grid, in_specs, out_specs, out_shape, args, scratch_shapes=(), comm=None):
    params = _params(*(("arbitrary",) * len(grid)))
    if comm is None:
        out = pl.pallas_call(body, name=name, grid=grid, in_specs=in_specs, out_specs=out_specs, out_shape=out_shape,
                             scratch_shapes=scratch_shapes, compiler_params=params)(*args)
        return list(out), []
    n_in, n_out, n_scr = len(in_specs), len(out_specs), len(scratch_shapes)
    ns, nb = len(comm.srcs), len(comm.bufs)
    total = 1
    for size in grid:
        total *= size

    def fused(*refs):
        ins, srcs = refs[:n_in], refs[n_in:n_in + ns]
        at = n_in + ns + nb
        outs, bufs = refs[at:at + n_out], refs[at + n_out:at + n_out + nb]
        scratch = refs[at + n_out + nb:at + n_out + nb + n_scr]
        send_sems, recv_sems = refs[-2:]
        step = 0
        for axis, size in enumerate(grid):
            step = step * size + pl.program_id(axis)
        for when, fn in comm.phases:
            pl.when(step == when)(functools.partial(fn, srcs, bufs, send_sems, recv_sems))
        body(*ins, *outs, *scratch)
        pl.when(step == total - 1)(functools.partial(comm.finish, srcs, bufs, send_sems, recv_sems))

    out = pl.pallas_call(
        fused, name=name, grid=grid,
        in_specs=list(in_specs) + [ANY] * (ns + nb), out_specs=list(out_specs) + [ANY] * nb,
        out_shape=list(out_shape) + [jax.ShapeDtypeStruct(b.shape, b.dtype) for b in comm.bufs],
        input_output_aliases={n_in + ns + i: n_out + i for i in range(nb)},
        scratch_shapes=list(scratch_shapes) + [pltpu.SemaphoreType.DMA((comm.n_sems,))] * 2,
        compiler_params=params,
    )(*args, *comm.srcs, *comm.bufs)
    return list(out[:n_out]), list(out[n_out:])


def _comm_only(comm, name):
    ns, nb = len(comm.srcs), len(comm.bufs)

    def body(*refs):
        srcs, bufs = refs[:ns], refs[ns + nb:ns + 2 * nb]
        send_sems, recv_sems = refs[-2:]
        for _, fn in comm.phases:
            fn(srcs, bufs, send_sems, recv_sems)
        comm.finish(srcs, bufs, send_sems, recv_sems)

    return pl.pallas_call(
        body, name=name, in_specs=[ANY] * (ns + nb), out_specs=[ANY] * nb,
        out_shape=[jax.ShapeDtypeStruct(b.shape, b.dtype) for b in comm.bufs],
        input_output_aliases={ns + i: i for i in range(nb)},
        scratch_shapes=[pltpu.SemaphoreType.DMA((comm.n_sems,))] * 2,
    )(*comm.srcs, *comm.bufs)


def _sigmoid(v):
    return jax.nn.sigmoid(v)


def _dsilu(v, s):
    return s * (1.0 + v * (1.0 - s))


def _mean_last(v):
    return jnp.mean(v, axis=-1, keepdims=True)


def _sum_rows(v):
    return jnp.sum(v, axis=0, keepdims=True)


def _norm_matmul(x, gain, w_sm, name):
    s, d = x.shape
    n_sh, _, ns = w_sm.shape
    tm = _row_tile(s, 1024)

    def body(x_ref, g_ref, w_ref, p_ref, h_ref):
        @pl.when(pl.program_id(1) == 0)
        def _():
            xv = x_ref[...]
            r = lax.rsqrt(_mean_last(xv * xv) + EPS)
            h_ref[...] = (xv * r * g_ref[...]).astype(BF16)

        p_ref[...] = jnp.dot(h_ref[...], w_ref[0], preferred_element_type=F32)

    return _call(
        body, name=name, grid=(s // tm, n_sh),
        in_specs=[pl.BlockSpec((tm, d), lambda i, j: (i, 0)),
                  pl.BlockSpec((1, d), lambda i, j: (0, 0)),
                  pl.BlockSpec((1, d, ns), lambda i, j: (j, 0, 0))],
        out_specs=[pl.BlockSpec((tm, ns), lambda i, j: (i, j)),
                   pl.BlockSpec((tm, d), lambda i, j: (i, 0))],
        out_shape=[jax.ShapeDtypeStruct((s, n_sh * ns), F32), jax.ShapeDtypeStruct((s, d), BF16)],
        args=(x, gain, w_sm))[0]


def _matmul_post_loss(u, w, x_res, gain, target, name):
    s, k = u.shape
    d = w.shape[1]
    tm = _row_tile(s, 256)

    def body(u_ref, w_ref, x_ref, g_ref, t_ref, dy_ref, dout_ref, dg_ref, loss_ref):
        @pl.when(pl.program_id(0) == 0)
        def _():
            dg_ref[...] = jnp.zeros_like(dg_ref)
            loss_ref[...] = jnp.zeros_like(loss_ref)

        y = jnp.dot(u_ref[...], w_ref[...], preferred_element_type=F32)
        r = lax.rsqrt(_mean_last(y * y) + EPS)
        n = y * r
        g = g_ref[...]
        err = x_ref[...] + n * g - t_ref[...]
        loss_ref[...] += 0.5 * jnp.sum(_mean_last(err * err))
        dout = err * (1.0 / d)
        dout_ref[...] = dout
        dg_ref[...] += _sum_rows(dout * n)
        dn = dout * g
        dy_ref[...] = (r * (dn - n * _mean_last(dn * n))).astype(BF16)

    return pl.pallas_call(
        body, name=name, grid=(s // tm,),
        in_specs=[pl.BlockSpec((tm, k), lambda i: (i, 0)),
                  pl.BlockSpec((k, d), lambda i: (0, 0)),
                  pl.BlockSpec((tm, d), lambda i: (i, 0)),
                  pl.BlockSpec((1, d), lambda i: (0, 0)),
                  pl.BlockSpec((tm, d), lambda i: (i, 0))],
        out_specs=[pl.BlockSpec((tm, d), lambda i: (i, 0)),
                   pl.BlockSpec((tm, d), lambda i: (i, 0)),
                   pl.BlockSpec((1, d), lambda i: (0, 0)),
                   pl.BlockSpec((8, LANES), lambda i: (0, 0))],
        out_shape=[jax.ShapeDtypeStruct((s, d), BF16), jax.ShapeDtypeStruct((s, d), F32),
                   jax.ShapeDtypeStruct((1, d), F32), jax.ShapeDtypeStruct((8, LANES), F32)],
        compiler_params=_params("arbitrary"),
    )(u, w, x_res, gain, target)


def _matmul_nt(a, w_sm, name, comm=None):
    s, ncols = a.shape
    n_sh, r, ns = w_sm.shape
    assert ncols == n_sh * ns
    tm = _row_tile(s, 1024)
    nc = _col_chunk(ns, 1792)
    per = ns // nc
    steps = n_sh * per

    def body(a_ref, w_ref, o_ref):
        part = lax.dot_general(a_ref[...], w_ref[0], (((1,), (1,)), ((), ())), preferred_element_type=F32)

        @pl.when(pl.program_id(1) == 0)
        def _():
            o_ref[...] = part

        @pl.when(pl.program_id(1) > 0)
        def _():
            o_ref[...] += part

    out, bufs = _call(
        body, name=name, grid=(s // tm, steps),
        in_specs=[pl.BlockSpec((tm, nc), lambda i, j: (i, j)),
                  pl.BlockSpec((1, r, nc), lambda i, j: (j // per, 0, j % per))],
        out_specs=[pl.BlockSpec((tm, r), lambda i, j: (i, 0))],
        out_shape=[jax.ShapeDtypeStruct((s, r), F32)],
        args=(a, w_sm), comm=comm)
    return out[0], bufs


def _matmul_tn(a, b, n_sh, name):
    s, k = a.shape
    n = b.shape[1]
    ns = n // n_sh
    tk = _col_chunk(k, TN_ACC_BYTES // (4 * ns))
    ts = _row_tile(s, 2048)
    n_s = s // ts

    def body(a_ref, b_ref, o_ref, acc_ref):
        part = lax.dot_general(a_ref[...], b_ref[...], (((0,), (0,)), ((), ())), preferred_element_type=F32)

        @pl.when(pl.program_id(2) == 0)
        def _():
            acc_ref[...] = part

        @pl.when(pl.program_id(2) > 0)
        def _():
            acc_ref[...] += part

        @pl.when(pl.program_id(2) == n_s - 1)
        def _():
            o_ref[0] = acc_ref[...].astype(BF16)

    return pl.pallas_call(
        body, name=name, grid=(n_sh, k // tk, n_s),
        in_specs=[pl.BlockSpec((ts, tk), lambda j, i, t: (t, i)),
                  pl.BlockSpec((ts, ns), lambda j, i, t: (t, j))],
        out_specs=pl.BlockSpec((1, tk, ns), lambda j, i, t: (j, i, 0)),
        out_shape=jax.ShapeDtypeStruct((n_sh, k, ns), BF16),
        scratch_shapes=[pltpu.VMEM((tk, ns), F32)],
        compiler_params=_params("arbitrary", "arbitrary", "arbitrary"),
    )(a, b)


def _norm_bwd(dh, x, gain, dres, name, post=None):
    s, d = x.shape
    tm = _row_tile(s, 256)
    with_post = post is not None

    def rms_bwd(dout, v, g):
        r = lax.rsqrt(_mean_last(v * v) + EPS)
        n = v * r
        dn = dout * g
        return r * (dn - n * _mean_last(dn * n)), _sum_rows(dout * n)

    def body(*refs):
        if with_post:
            dh_ref, x_ref, g_ref, dres_ref, y_ref, gp_ref, dx_ref, dg_ref, dy_ref, dgp_ref = refs
        else:
            dh_ref, x_ref, g_ref, dres_ref, dx_ref, dg_ref = refs

        @pl.when(pl.program_id(0) == 0)
        def _():
            dg_ref[...] = jnp.zeros_like(dg_ref)
            if with_post:
                dgp_ref[...] = jnp.zeros_like(dgp_ref)

        dv, dg = rms_bwd(dh_ref[...], x_ref[...], g_ref[...])
        dx = dres_ref[...] + dv
        dx_ref[...] = dx
        dg_ref[...] += dg
        if with_post:
            dy, dgp = rms_bwd(dx, y_ref[...], gp_ref[...])
            dy_ref[...] = dy.astype(BF16)
            dgp_ref[...] += dgp

    row = pl.BlockSpec((tm, d), lambda i: (i, 0))
    vec = pl.BlockSpec((1, d), lambda i: (0, 0))
    in_specs = [row, row, vec, row]
    out_specs = [row, vec]
    out_shape = [jax.ShapeDtypeStruct((s, d), F32), jax.ShapeDtypeStruct((1, d), F32)]
    args = [dh, x, gain, dres]
    if with_post:
        in_specs += [row, vec]
        out_specs += [row, vec]
        out_shape += [jax.ShapeDtypeStruct((s, d), BF16), jax.ShapeDtypeStruct((1, d), F32)]
        args += list(post)
    return _call(body, name=name, grid=(s // tm,), in_specs=in_specs, out_specs=out_specs, out_shape=out_shape,
                 args=args)[0]


def _fill_shifted_down(sh, rows):
    for b in range(1, SHIFTS):
        sh[b, SHIFTS:rows, :] = sh[0, SHIFTS - b:rows - b, :]


def _fill_shifted_up(sh, rows):
    for b in range(1, SHIFTS):
        sh[b, 0:rows - SHIFTS, :] = sh[0, b:rows - SHIFTS + b, :]


def _for_blocks(ts, w, fn):
    lb = min(LANE_BLOCK, w)
    for l0 in range(0, w, lb):
        def rows(rb, carry, l0=l0):
            fn(pl.multiple_of(rb * ROW_BLOCK, ROW_BLOCK), slice(l0, l0 + lb))
            return carry

        lax.fori_loop(0, ts // ROW_BLOCK, rows, 0)


TAP_SPAN = SHIFTS * ((CONV_B - 1) // SHIFTS)
WINDOW = ROW_BLOCK + TAP_SPAN


def _taps_of(b):
    return [(a, SHIFTS * a + b) for a in range((CONV_B - 1 - b) // SHIFTS + 1)]


def _conv31(sh, base, step, wt_ref, bias_ref, out_ref, ts, w):
    low = min(0, step * (TAP_SPAN // SHIFTS))

    def block(r0, lanes):
        acc = [jnp.zeros((SHIFTS, lanes.stop - lanes.start), F32) for _ in range(ROW_BLOCK // SHIFTS)]
        for b in range(SHIFTS):
            window = sh[b, pl.ds(pl.multiple_of(r0 + (base + low), SHIFTS), WINDOW), lanes]
            for a, j in (_taps_of(b) if step > 0 else reversed(_taps_of(b))):
                at = step * a - low
                wt = wt_ref[CONV_B - 1 - j, :, lanes]
                acc = [v + wt * window[at + SHIFTS * r:at + SHIFTS * (r + 1), :] for r, v in enumerate(acc)]
        for r, v in enumerate(acc):
            if bias_ref is not None:
                v = v + bias_ref[:, lanes]
            out_ref[pl.ds(pl.multiple_of(r0 + SHIFTS * r, SHIFTS), SHIFTS), lanes] = v

    _for_blocks(ts, w, block)


def _conv31_weight_grad(d_sh, x_sh, wacc, ts, w):
    def block(r0, lanes):
        d = d_sh[0, pl.ds(r0, ROW_BLOCK), lanes]
        for b in range(SHIFTS):
            window = x_sh[b, pl.ds(pl.multiple_of(r0 + (HALO_B - TAP_SPAN), SHIFTS), WINDOW), lanes]
            for a, j in _taps_of(b):
                at = TAP_SPAN - SHIFTS * a
                prod = d * window[at:at + ROW_BLOCK, :]
                part = prod[0:SHIFTS, :]
                for q in range(1, ROW_BLOCK // SHIFTS):
                    part = part + prod[q * SHIFTS:(q + 1) * SHIFTS, :]
                wacc[CONV_B - 1 - j, :, lanes] += part

    _for_blocks(ts, w, block)


def _even_forward_tile(p_ref, halo_ref, first, a_conv_ref, b_conv_ref, bias_ref, lng_ref, lnb_ref, qbuf, ysh, y1buf,
                       wb, w, ts):
    @pl.when(pl.program_id(0) == 0)
    def _():
        for k in range(CONV_B):
            wb[k] = jnp.broadcast_to(b_conv_ref[k:k + 1, :], (SHIFTS, w))

    def col(ref, k, rows=slice(None)):
        return ref[rows, k * w:(k + 1) * w]

    a_x, a_b, a_c, a_z = col(p_ref, 0), col(p_ref, 1), col(p_ref, 2), col(p_ref, 3)
    b_val, b_gate, b_z = col(p_ref, 4), col(p_ref, 5), col(p_ref, 6)
    keep = jnp.where(first, 0.0, 1.0)

    rows_a = slice(HALO_B - HALO_A, HALO_B)
    qbuf[0:HALO_A, :] = col(halo_ref, 2, rows_a) * col(halo_ref, 0, rows_a) * keep
    qbuf[HALO_A:HALO_A + ts, :] = a_c * a_x
    cq = jnp.zeros((ts, w), F32)
    for j in range(CONV_A):
        cq = cq + a_conv_ref[CONV_A - 1 - j:CONV_A - j, :] * qbuf[HALO_A - j:HALO_A - j + ts, :]
    ya = a_b * cq

    ysh[0, 0:HALO_B, :] = col(halo_ref, 4) * _sigmoid(col(halo_ref, 5)) * keep
    ysh[0, HALO_B:HALO_B + ts, :] = b_val * _sigmoid(b_gate)
    _fill_shifted_down(ysh, HALO_B + ts)
    _conv31(ysh, HALO_B, -SHIFTS, wb, bias_ref, y1buf, ts, w)
    yb1 = y1buf[...]
    xc = yb1 - _mean_last(yb1)
    rstd = lax.rsqrt(_mean_last(xc * xc) + EPS)
    xhat = xc * rstd
    yb2 = xhat * lng_ref[...] + lnb_ref[...]
    return dict(a_x=a_x, a_b=a_b, a_c=a_c, a_z=a_z, b_val=b_val, b_gate=b_gate, b_z=b_z,
                cq=cq, ya=ya, rstd=rstd, xhat=xhat, yb2=yb2)


def _even_specs(s, w, ts):
    tile = pl.BlockSpec((ts, 7 * w), lambda i: (i, 0))
    halo = pl.BlockSpec((HALO_B, 7 * w), lambda i: (jnp.maximum(i * (ts // HALO_B) - 1, 0), 0))
    return tile, halo


def _small_specs(shapes, index=lambda i: (0, 0)):
    return [pl.BlockSpec(sh, index) for sh in shapes]


def _even_mixer_fwd(p, w_out, x_res, gain, a_conv, b_conv, bias, ln_g, ln_b, name, comm=None):
    s, d = x_res.shape
    w = p.shape[1] // 7
    ts = _row_tile(s, 128)
    assert ts % HALO_B == 0

    def body(p_ref, halo_ref, wout_ref, x_ref, g_ref, ac_ref, bc_ref, bias_ref, lng_ref, lnb_ref,
             u_ref, xn_ref, y_ref, qbuf, ysh, y1buf, wb):
        first = pl.program_id(0) == 0
        f = _even_forward_tile(p_ref, halo_ref, first, ac_ref, bc_ref, bias_ref, lng_ref, lnb_ref, qbuf, ysh, y1buf,
                               wb, w, ts)
        yb3 = f["yb2"] * _sigmoid(f["yb2"])
        u_a = (f["ya"] * (f["a_z"] * _sigmoid(f["a_z"]))).astype(BF16)
        u_b = (yb3 * (f["b_z"] * _sigmoid(f["b_z"]))).astype(BF16)
        u_ref[:, 0:w] = u_a
        u_ref[:, w:2 * w] = u_b
        y = (jnp.dot(u_a, wout_ref[0:w, :], preferred_element_type=F32)
             + jnp.dot(u_b, wout_ref[w:2 * w, :], preferred_element_type=F32))
        r = lax.rsqrt(_mean_last(y * y) + EPS)
        y_ref[...] = y
        xn_ref[...] = x_ref[...] + (y * r) * g_ref[...]

    tile, halo = _even_specs(s, w, ts)
    row = pl.BlockSpec((ts, d), lambda i: (i, 0))
    return _call(
        body, name=name, grid=(s // ts,),
        in_specs=[tile, halo, pl.BlockSpec((2 * w, d), lambda i: (0, 0)), row, pl.BlockSpec((1, d), lambda i: (0, 0))]
        + _small_specs([(CONV_A, w), (CONV_B, w), (1, w), (1, w), (1, w)]),
        out_specs=[pl.BlockSpec((ts, 2 * w), lambda i: (i, 0)), row, row],
        out_shape=[jax.ShapeDtypeStruct((s, 2 * w), BF16), jax.ShapeDtypeStruct((s, d), F32),
                   jax.ShapeDtypeStruct((s, d), F32)],
        scratch_shapes=[pltpu.VMEM((HALO_A + ts, w), F32), pltpu.VMEM((SHIFTS, HALO_B + ts, w), F32),
                        pltpu.VMEM((ts, w), F32), pltpu.VMEM((CONV_B, SHIFTS, w), F32)],
        args=(p, p, w_out, x_res, gain, a_conv, b_conv, bias, ln_g, ln_b), comm=comm)


def _even_mixer_bwd(p, du, a_conv, b_conv, bias, ln_g, ln_b, name, comm=None):
    s = p.shape[0]
    w = p.shape[1] // 7
    ts = _row_tile(s, EVEN_BWD_ROWS)
    nt = s // ts
    assert ts % HALO_B == 0

    def body(p_ref, halo_ref, du_ref, ac_ref, bc_ref, bias_ref, lng_ref, lnb_ref,
             dp_ref, dac_ref, dbc_ref, dbias_ref, dlng_ref, dlnb_ref,
             qbuf, ysh, y1buf, dqbuf, dsh, dy0buf, wacc, carry_dq, carry_dy, wb):
        step = pl.program_id(0)
        first = step == nt - 1

        @pl.when(step == 0)
        def _():
            for ref in (dac_ref, dbias_ref, dlng_ref, dlnb_ref, wacc, carry_dq, carry_dy):
                ref[...] = jnp.zeros_like(ref)

        f = _even_forward_tile(p_ref, halo_ref, first, ac_ref, bc_ref, bias_ref, lng_ref, lnb_ref, qbuf, ysh, y1buf,
                               wb, w, ts)
        a_z, b_z, yb2 = f["a_z"], f["b_z"], f["yb2"]
        s_az, s_bz, s_y2 = _sigmoid(a_z), _sigmoid(b_z), _sigmoid(yb2)
        du_a = du_ref[:, 0:w]
        du_b = du_ref[:, w:2 * w]

        d_ya = du_a * (a_z * s_az)
        dp_ref[:, 3 * w:4 * w] = (du_a * f["ya"] * _dsilu(a_z, s_az)).astype(BF16)
        dp_ref[:, 1 * w:2 * w] = (d_ya * f["cq"]).astype(BF16)
        d_cq = d_ya * f["a_b"]
        dqbuf[0:ts, :] = d_cq
        dqbuf[ts:ts + HALO_A, :] = carry_dq[...]
        d_q = jnp.zeros((ts, w), F32)
        for o in range(CONV_A):
            d_q = d_q + ac_ref[CONV_A - 1 - o:CONV_A - o, :] * dqbuf[o:o + ts, :]
        for j in range(CONV_A):
            k = CONV_A - 1 - j
            dac_ref[k:k + 1, :] += _sum_rows(d_cq * qbuf[HALO_A - j:HALO_A - j + ts, :])
        carry_dq[...] = d_cq[0:HALO_A, :]
        dp_ref[:, 2 * w:3 * w] = (d_q * f["a_x"]).astype(BF16)
        dp_ref[:, 0 * w:1 * w] = (d_q * f["a_c"]).astype(BF16)

        d_yb3 = du_b * (b_z * s_bz)
        dp_ref[:, 6 * w:7 * w] = (du_b * (yb2 * s_y2) * _dsilu(b_z, s_bz)).astype(BF16)
        d_yb2 = d_yb3 * _dsilu(yb2, s_y2)
        xhat = f["xhat"]
        dlng_ref[...] += _sum_rows(d_yb2 * xhat)
        dlnb_ref[...] += _sum_rows(d_yb2)
        d_xh = d_yb2 * lng_ref[...]
        d_yb1 = f["rstd"] * (d_xh - _mean_last(d_xh) - xhat * _mean_last(d_xh * xhat))
        dbias_ref[...] += _sum_rows(d_yb1)
        dsh[0, 0:ts, :] = d_yb1
        dsh[0, ts:ts + HALO_B, :] = carry_dy[...]
        carry_dy[...] = d_yb1[0:HALO_B, :]
        _fill_shifted_up(dsh, ts + HALO_B)
        _conv31(dsh, 0, SHIFTS, wb, None, dy0buf, ts, w)
        _conv31_weight_grad(dsh, ysh, wacc, ts, w)

        @pl.when(step == nt - 1)
        def _():
            for k in range(CONV_B):
                dbc_ref[k:k + 1, :] = _sum_rows(wacc[k])

        d_yb0 = dy0buf[...]
        s_g = _sigmoid(f["b_gate"])
        dp_ref[:, 4 * w:5 * w] = (d_yb0 * s_g).astype(BF16)
        dp_ref[:, 5 * w:6 * w] = (d_yb0 * f["b_val"] * s_g * (1.0 - s_g)).astype(BF16)

    rev = lambda i: (nt - 1 - i, 0)
    tile = pl.BlockSpec((ts, 7 * w), rev)
    halo = pl.BlockSpec((HALO_B, 7 * w), lambda i: (jnp.maximum((nt - 1 - i) * (ts // HALO_B) - 1, 0), 0))
    small = [(CONV_A, w), (CONV_B, w), (1, w), (1, w), (1, w)]
    return _call(
        body, name=name, grid=(nt,),
        in_specs=[tile, halo, pl.BlockSpec((ts, 2 * w), rev)] + _small_specs(small),
        out_specs=[pl.BlockSpec((ts, 7 * w), rev)] + _small_specs(small),
        out_shape=[jax.ShapeDtypeStruct((s, 7 * w), BF16)] + [jax.ShapeDtypeStruct(sh, F32) for sh in small],
        scratch_shapes=[pltpu.VMEM((HALO_A + ts, w), F32), pltpu.VMEM((SHIFTS, HALO_B + ts, w), F32),
                        pltpu.VMEM((ts, w), F32),
                        pltpu.VMEM((ts + HALO_A, w), F32), pltpu.VMEM((SHIFTS, ts + HALO_B, w), F32),
                        pltpu.VMEM((ts, w), F32), pltpu.VMEM((CONV_B, SHIFTS, w), F32),
                        pltpu.VMEM((HALO_A, w), F32), pltpu.VMEM((HALO_B, w), F32),
                        pltpu.VMEM((CONV_B, SHIFTS, w), F32)],
        args=(p, p, du, a_conv, b_conv, bias, ln_g, ln_b), comm=comm)


def _pool_forward_tile(p_ref, halo_ref, first, tile_index, cw_ref, cb_ref, cs_ref, vbuf, c, gc, ts):
    vbuf[0:HALO_P, :] = halo_ref[...] * jnp.where(first, 0.0, 1.0)
    vbuf[HALO_P:HALO_P + ts, :] = p_ref[:, 0:c]
    pos = tile_index * ts + lax.broadcasted_iota(jnp.int32, (ts, 1), 0) + 1
    pooled, inv, gout = [], [], []
    for g, win in enumerate(POOL_WINDOWS):
        cols = slice(g * gc, (g + 1) * gc)
        acc = jnp.zeros((ts, gc), F32)
        for j in range(win):
            acc = acc + vbuf[HALO_P - j:HALO_P - j + ts, cols]
        inv_g = 1.0 / jnp.minimum(pos, win).astype(F32)
        pooled_g = (acc * inv_g - p_ref[:, cols]).astype(BF16)
        pooled.append(pooled_g)
        inv.append(inv_g)
        gout.append(jnp.dot(pooled_g, cw_ref[g], preferred_element_type=F32) + cb_ref[:, cols])
    return pooled, inv, gout


def _odd_mixer_fwd(p, cw, cb, cs, name):
    s = p.shape[0]
    c = p.shape[1] // 2
    gc = c // N_GROUPS
    ts = _row_tile(s, 256)

    def body(p_ref, halo_ref, cw_ref, cb_ref, cs_ref, u_ref, vbuf):
        i = pl.program_id(0)
        _, _, gout = _pool_forward_tile(p_ref, halo_ref, i == 0, i, cw_ref, cb_ref, cs_ref, vbuf, c, gc, ts)
        for g in range(N_GROUPS):
            cols = slice(g * gc, (g + 1) * gc)
            z = p_ref[:, c + g * gc:c + (g + 1) * gc]
            u_ref[:, cols] = (gout[g] * cs_ref[:, cols] * (z * _sigmoid(z))).astype(BF16)

    return pl.pallas_call(
        body, name=name, grid=(s // ts,),
        in_specs=[pl.BlockSpec((ts, 2 * c), lambda i: (i, 0)),
                  pl.BlockSpec((HALO_P, c), lambda i: (jnp.maximum(i * (ts // HALO_P) - 1, 0), 0)),
                  pl.BlockSpec((N_GROUPS, gc, gc), lambda i: (0, 0, 0)),
                  pl.BlockSpec((1, c), lambda i: (0, 0)), pl.BlockSpec((1, c), lambda i: (0, 0))],
        out_specs=pl.BlockSpec((ts, c), lambda i: (i, 0)),
        out_shape=jax.ShapeDtypeStruct((s, c), BF16),
        scratch_shapes=[pltpu.VMEM((HALO_P + ts, c), F32)],
        compiler_params=_params("arbitrary"),
    )(p, p, cw, cb, cs)


def _odd_mixer_bwd(p, du, cw, cb, cs, name):
    s = p.shape[0]
    c = p.shape[1] // 2
    gc = c // N_GROUPS
    ts = _row_tile(s, 256)
    nt = s // ts

    def body(p_ref, halo_ref, du_ref, cw_ref, cb_ref, cs_ref, dp_ref, dcw_ref, dcb_ref, dcs_ref, vbuf, ebuf, carry_e):
        step = pl.program_id(0)
        tile_index = nt - 1 - step

        @pl.when(step == 0)
        def _():
            for ref in (dcw_ref, dcb_ref, dcs_ref, carry_e):
                ref[...] = jnp.zeros_like(ref)

        pooled, inv, gout = _pool_forward_tile(p_ref, halo_ref, tile_index == 0, tile_index, cw_ref, cb_ref, cs_ref,
                                               vbuf, c, gc, ts)
        ebuf[ts:ts + HALO_P, :] = carry_e[...]
        for g, win in enumerate(POOL_WINDOWS):
            cols = slice(g * gc, (g + 1) * gc)
            z = p_ref[:, c + g * gc:c + (g + 1) * gc]
            sz = _sigmoid(z)
            du_g = du_ref[:, cols]
            scale = cs_ref[:, cols]
            d_y = du_g * (z * sz)
            dp_ref[:, c + g * gc:c + (g + 1) * gc] = (du_g * (gout[g] * scale) * _dsilu(z, sz)).astype(BF16)
            dcs_ref[:, cols] += _sum_rows(d_y * gout[g])
            d_gout = d_y * scale
            dcb_ref[:, cols] += _sum_rows(d_gout)
            d_gout_b = d_gout.astype(BF16)
            dcw_ref[g] += lax.dot_general(pooled[g], d_gout_b, (((0,), (0,)), ((), ())), preferred_element_type=F32)
            d_pool = lax.dot_general(d_gout_b, cw_ref[g], (((1,), (1,)), ((), ())), preferred_element_type=F32)
            e = d_pool * inv[g]
            ebuf[0:ts, cols] = e
            d_v = -d_pool
            for o in range(win):
                d_v = d_v + ebuf[o:o + ts, cols]
            dp_ref[:, cols] = d_v.astype(BF16)
            carry_e[:, cols] = e[0:HALO_P, :]

    rev = lambda i: (nt - 1 - i, 0)
    small = [(N_GROUPS, gc, gc), (1, c), (1, c)]
    return pl.pallas_call(
        body, name=name, grid=(nt,),
        in_specs=[pl.BlockSpec((ts, 2 * c), rev),
                  pl.BlockSpec((HALO_P, c), lambda i: (jnp.maximum((nt - 1 - i) * (ts // HALO_P) - 1, 0), 0)),
                  pl.BlockSpec((ts, c), rev),
                  pl.BlockSpec((N_GROUPS, gc, gc), lambda i: (0, 0, 0)),
                  pl.BlockSpec((1, c), lambda i: (0, 0)), pl.BlockSpec((1, c), lambda i: (0, 0))],
        out_specs=[pl.BlockSpec((ts, 2 * c), rev),
                   pl.BlockSpec((N_GROUPS, gc, gc), lambda i: (0, 0, 0)),
                   pl.BlockSpec((1, c), lambda i: (0, 0)), pl.BlockSpec((1, c), lambda i: (0, 0))],
        out_shape=[jax.ShapeDtypeStruct((s, 2 * c), BF16)] + [jax.ShapeDtypeStruct(sh, F32) for sh in small],
        scratch_shapes=[pltpu.VMEM((HALO_P + ts, c), F32), pltpu.VMEM((ts + HALO_P, c), F32),
                        pltpu.VMEM((HALO_P, c), F32)],
        compiler_params=_params("arbitrary"),
    )(p, p, du, cw, cb, cs)


def _cast_into_slot(a, coords, name):
    r, cols = a.shape
    tr = _row_tile(r // 2, 256)
    per = r // 2 // tr

    def body(co_ref, a_ref, o_ref):
        o_ref[0, 0] = a_ref[...].astype(BF16)

    return pl.pallas_call(
        body, name=name,
        grid_spec=pltpu.PrefetchScalarGridSpec(
            num_scalar_prefetch=1, grid=(2, per),
            in_specs=[pl.BlockSpec((tr, cols), lambda h, i, co: (h * per + i, 0))],
            out_specs=pl.BlockSpec((1, 1, tr, cols), lambda h, i, co: (co[0], h, i, 0))),
        out_shape=jax.ShapeDtypeStruct((N_SHARDS, 2, r // 2, cols), BF16),
        compiler_params=_params("arbitrary", "arbitrary"),
    )(coords, a)


def _chip_sum(g, other, coords, name):
    n_sh, _, r2, cols = g.shape
    tr = _row_tile(r2, 256)

    def body(co_ref, g_ref, o_ref, sum_ref, mine_ref):
        v = (g_ref[0, 0].astype(F32) + o_ref[0].astype(F32)).astype(BF16)
        sum_ref[0] = v

        @pl.when(pl.program_id(1) == co_ref[0])
        def _():
            mine_ref[0] = v

    piece = pl.BlockSpec((1, tr, cols), lambda i, s, co: (s, i, 0))
    return pl.pallas_call(
        body, name=name,
        grid_spec=pltpu.PrefetchScalarGridSpec(
            num_scalar_prefetch=1, grid=(r2 // tr, n_sh),
            in_specs=[pl.BlockSpec((1, 1, tr, cols), lambda i, s, co: (s, co[1], i, 0)), piece],
            out_specs=[piece, pl.BlockSpec((1, tr, cols), lambda i, s, co: (co[0], i, 0))]),
        out_shape=[jax.ShapeDtypeStruct((n_sh, r2, cols), BF16)] * 2,
        compiler_params=_params("arbitrary", "arbitrary"),
    )(coords, g, other)


def _shard_sum(pieces, coords, name):
    n_sh, r2, cols = pieces.shape
    tr = _row_tile(r2, 256)

    def body(co_ref, p_ref, o_ref):
        acc = p_ref[0].astype(F32)
        for k in range(1, n_sh):
            acc = acc + p_ref[k].astype(F32)
        o_ref[0] = acc

    return pl.pallas_call(
        body, name=name,
        grid_spec=pltpu.PrefetchScalarGridSpec(
            num_scalar_prefetch=1, grid=(r2 // tr,),
            in_specs=[pl.BlockSpec((n_sh, tr, cols), lambda i, co: (0, i, 0))],
            out_specs=pl.BlockSpec((1, tr, cols), lambda i, co: (co[1], i, 0))),
        out_shape=jax.ShapeDtypeStruct((2, r2, cols), F32),
        compiler_params=_params("arbitrary"),
    )(coords, pieces)


def _sum_small(a, name):
    n, r, cols = a.shape

    def body(a_ref, o_ref):
        acc = a_ref[0]
        for k in range(1, n):
            acc = acc + a_ref[k]
        o_ref[...] = acc

    return pl.pallas_call(
        body, name=name,
        in_specs=[pl.BlockSpec((n, r, cols), lambda: (0, 0, 0))],
        out_specs=pl.BlockSpec((r, cols), lambda: (0, 0)),
        out_shape=jax.ShapeDtypeStruct((r, cols), F32),
        compiler_params=_params(),
    )(a)


def _adamw(w, g, m, v, name):
    r, cols = w.shape
    tr = _row_tile(r, 256) if r % SUBLANES_BF16 == 0 else r

    def body(w_ref, g_ref, m_ref, v_ref, d_ref, nm_ref, nv_ref):
        g = g_ref[...]
        m = ADAM_B1 * m_ref[...] + (1.0 - ADAM_B1) * g
        v = ADAM_B2 * v_ref[...] + (1.0 - ADAM_B2) * (g * g)
        m_hat = m / (1.0 - ADAM_B1 ** ADAM_STEP)
        v_hat = v / (1.0 - ADAM_B2 ** ADAM_STEP)
        d_ref[...] = -ADAM_LR * (m_hat / (jnp.sqrt(v_hat) + ADAM_EPS) + ADAM_WD * w_ref[...])
        nm_ref[...] = m
        nv_ref[...] = v

    blk = pl.BlockSpec((tr, cols), lambda i: (i, 0))
    return pl.pallas_call(
        body, name=name, grid=(r // tr,),
        in_specs=[blk] * 4, out_specs=[blk] * 3,
        out_shape=[jax.ShapeDtypeStruct((r, cols), F32)] * 3,
        compiler_params=_params("arbitrary"),
    )(w, g, m, v)


def _place():
    x, y, c = lax.axis_index("x"), lax.axis_index("y"), lax.axis_index("c")
    other_chips = [(1 - x, y), (x, 1 - y), (1 - x, 1 - y)]
    return x, y, c, other_chips


def _chip(xy):
    return 2 * xy[0] + xy[1]


def _remote(src, dst, send_sem, recv_sem, to):
    return pltpu.make_async_remote_copy(src_ref=src, dst_ref=dst, send_sem=send_sem, recv_sem=recv_sem,
                                        device_id=to, device_id_type=MESH)


def _gather_ici(ctx, k, j, start):
    (x, y, c, chips), b, send, recv = ctx
    blk = b[k].at[_chip((x, y)) if start else _chip(chips[j]), c]
    return _remote(blk, blk, send.at[6 * k + j], recv.at[6 * k + j], (*chips[j], c))


def _gather_d2d(ctx, k, j, start):
    (x, y, c, chips), b, send, recv = ctx
    blk = b[k].at[_chip(chips[j]), c if start else 1 - c]
    return _remote(blk, blk, send.at[6 * k + 3 + j], recv.at[6 * k + 3 + j], (x, y, 1 - c))


def _gather_small(ctx, n, j, start):
    (x, y, c, chips), b, send, recv = ctx
    blk = b[n].at[_chip((x, y)) if start else _chip(chips[j])]
    return _remote(blk, blk, send.at[6 * n + j], recv.at[6 * n + j], (*chips[j], c))


def _gather_comm(bufs, small, forward_at):
    n = len(bufs)
    ici, d2d = _gather_ici, _gather_d2d

    def small_copy(ctx, j, start):
        return _gather_small(ctx, n, j, start)

    def start(srcs, b, send, recv):
        ctx = (_place(), b, send, recv)
        for k in range(n):
            for j in range(3):
                ici(ctx, k, j, True).start()
        if small is not None:
            for j in range(3):
                small_copy(ctx, j, True).start()

    def forward(srcs, b, send, recv):
        ctx = (_place(), b, send, recv)
        for k in range(n):
            for j in range(3):
                ici(ctx, k, j, False).wait_recv()
                d2d(ctx, k, j, True).start()

    def finish(srcs, b, send, recv):
        ctx = (_place(), b, send, recv)
        if small is not None:
            for j in range(3):
                small_copy(ctx, j, False).wait_recv()
                small_copy(ctx, j, True).wait_send()
        for k in range(n):
            for j in range(3):
                d2d(ctx, k, j, False).wait_recv()
                ici(ctx, k, j, True).wait_send()
                d2d(ctx, k, j, True).wait_send()

    all_bufs = list(bufs) + ([small] if small is not None else [])
    return _Comm([], all_bufs, 6 * n + 3, [(0, start), (forward_at, forward)], finish)


def _rmsnorm(x, gain, name):
    s, d = x.shape
    tm = _row_tile(s, 512)

    def body(x_ref, g_ref, h_ref):
        xv = x_ref[...]
        r = lax.rsqrt(_mean_last(xv * xv) + EPS)
        h_ref[...] = (xv * r * g_ref[...]).astype(BF16)

    return pl.pallas_call(
        body, name=name, grid=(s // tm,),
        in_specs=[pl.BlockSpec((tm, d), lambda i: (i, 0)), pl.BlockSpec((1, d), lambda i: (0, 0))],
        out_specs=pl.BlockSpec((tm, d), lambda i: (i, 0)),
        out_shape=jax.ShapeDtypeStruct((s, d), BF16),
        compiler_params=_params("arbitrary"),
    )(x, gain)


def _gathered_in_proj(h, bufs, small, order, name):
    s, d = h.shape
    n_sh, _, r2, ns = bufs[0].shape
    assert d == 2 * r2
    n = len(bufs)
    tm = _row_tile(s, 512)
    n_i = s // tm
    hook_i = max(n_i - 2, 0)
    n_sems = 6 * n + 3

    def body(order_ref, h_ref, *rest):
        p_ref = rest[n + 1]
        b = rest[n + 2:2 * n + 3]
        w_vmem, w_sems, send, recv = rest[2 * n + 3:]
        j, i = pl.program_id(0), pl.program_id(1)
        ctx = (_place(), b, send, recv)

        def fetch(q):
            return pltpu.make_async_copy(b[0].at[order_ref[q]], w_vmem.at[q % 2], w_sems.at[q % 2])

        @pl.when((j == 0) & (i == 0))
        def _():
            for k in range(n):
                for peer in range(3):
                    _gather_ici(ctx, k, peer, True).start()
            for peer in range(3):
                _gather_small(ctx, n, peer, True).start()
            fetch(0).start()
            fetch(0).wait()

        for q in range(1, n_sh):
            @pl.when((j == q - 1) & (i == hook_i))
            def _(q=q):
                _gather_ici(ctx, 0, q - 1, False).wait_recv()
                _gather_d2d(ctx, 0, q - 1, True).start()
                _gather_d2d(ctx, 0, q - 1, False).wait_recv()
                fetch(q).start()
                if q == n_sh - 1:
                    for k in range(1, n):
                        for peer in range(3):
                            _gather_ici(ctx, k, peer, False).wait_recv()
                            _gather_d2d(ctx, k, peer, True).start()

            @pl.when((j == q) & (i == 0))
            def _(q=q):
                fetch(q).wait()

        wv = w_vmem.at[j % 2]
        p_ref[...] = (jnp.dot(h_ref[:, 0:r2], wv[0], preferred_element_type=F32)
                      + jnp.dot(h_ref[:, r2:d], wv[1], preferred_element_type=F32))

        @pl.when((j == n_sh - 1) & (i == n_i - 1))
        def _():
            for peer in range(3):
                _gather_small(ctx, n, peer, False).wait_recv()
                _gather_small(ctx, n, peer, True).wait_send()
            for k in range(n):
                for peer in range(3):
                    if k > 0:
                        _gather_d2d(ctx, k, peer, False).wait_recv()
                    _gather_ici(ctx, k, peer, True).wait_send()
                    _gather_d2d(ctx, k, peer, True).wait_send()

    all_bufs = list(bufs) + [small]
    out = pl.pallas_call(
        body, name=name,
        grid_spec=pltpu.PrefetchScalarGridSpec(
            num_scalar_prefetch=1, grid=(n_sh, n_i),
            in_specs=[pl.BlockSpec((tm, d), lambda j, i, o: (i, 0))] + [ANY] * (n + 1),
            out_specs=[pl.BlockSpec((tm, ns), lambda j, i, o: (i, o[j]))] + [ANY] * (n + 1),
            scratch_shapes=[pltpu.VMEM((2, 2, r2, ns), BF16), pltpu.SemaphoreType.DMA((2,)),
                            pltpu.SemaphoreType.DMA((n_sems,)), pltpu.SemaphoreType.DMA((n_sems,))]),
        out_shape=[jax.ShapeDtypeStruct((s, n_sh * ns), F32)]
        + [jax.ShapeDtypeStruct(a.shape, a.dtype) for a in all_bufs],
        input_output_aliases={2 + t: 1 + t for t in range(n + 1)},
        compiler_params=_params("arbitrary", "arbitrary"),
    )(order, h, *all_bufs)
    return out[0], list(out[1:])


def _exchange_comm(grads):
    n = len(grads)
    landing = [lax.empty((N_SHARDS,) + a.shape[2:], a.dtype) for a in grads]

    def copies(srcs, b, send, recv):
        x, y, c, _ = _place()
        return [_remote(srcs[k].at[s, 1 - c], b[k].at[s], send.at[N_SHARDS * k + s], recv.at[N_SHARDS * k + s],
                        (x, y, 1 - c)) for k in range(n) for s in range(N_SHARDS)]

    def start(srcs, b, send, recv):
        for cp in copies(srcs, b, send, recv):
            cp.start()

    def finish(srcs, b, send, recv):
        for cp in copies(srcs, b, send, recv):
            cp.wait()

    return _Comm(grads, landing, N_SHARDS * n, [(0, start)], finish)


def _scatter_comm(chip_sums, landing):
    n = len(chip_sums)

    def big(srcs, b, send, recv, k, j, start):
        x, y, c, chips = _place()
        dst = b[k].at[_chip((x, y)) if start else _chip(chips[j])]
        return _remote(srcs[k].at[_chip(chips[j])], dst, send.at[3 * k + j], recv.at[3 * k + j], (*chips[j], c))

    def start(srcs, b, send, recv):
        for k in range(n):
            for j in range(3):
                big(srcs, b, send, recv, k, j, True).start()

    def finish(srcs, b, send, recv):
        for k in range(n):
            for j in range(3):
                big(srcs, b, send, recv, k, j, False).wait_recv()
                big(srcs, b, send, recv, k, j, True).wait_send()

    return _Comm(chip_sums, landing, 3 * n, [(0, start)], finish)


def _join_comm(halves, small):
    n = len(halves)
    flips = [(fx, fy, fc) for fx in (0, 1) for fy in (0, 1) for fc in (0, 1)][1:]

    def half(b, send, recv, k, start):
        x, y, c, _ = _place()
        return _remote(b[k].at[c], b[k].at[c if start else 1 - c], send.at[k], recv.at[k], (x, y, 1 - c))

    def small_copy(b, send, recv, q, start):
        x, y, c, _ = _place()
        px, py, pc = x ^ flips[q][0], y ^ flips[q][1], c ^ flips[q][2]
        blk = b[n].at[4 * x + 2 * y + c if start else 4 * px + 2 * py + pc]
        return _remote(blk, blk, send.at[n + q], recv.at[n + q], (px, py, pc))

    def start(srcs, b, send, recv):
        for q in range(len(flips)):
            small_copy(b, send, recv, q, True).start()
        for k in range(n):
            half(b, send, recv, k, True).start()

    def finish(srcs, b, send, recv):
        for q in range(len(flips)):
            small_copy(b, send, recv, q, False).wait()
        for k in range(n):
            half(b, send, recv, k, False).wait()

    return _Comm([], list(halves) + [small], n + len(flips), [(0, start)], finish)


def _flat_rows(parts):
    flat = jnp.concatenate([p.reshape(-1) for p in parts])
    assert flat.shape[0] % LANES == 0
    return flat.reshape(-1, LANES)


def _unflatten(flat, shapes):
    out, at = [], 0
    for sh in shapes:
        size = 1
        for dim in sh:
            size *= dim
        out.append(flat[at:at + size].reshape(sh))
        at += size
    assert at == flat.shape[0], (at, flat.shape)
    return out


def _col_shards_to_full(a, rows):
    q = a.shape[1] // rows
    return a.reshape(N_SHARDS, rows, q).transpose(1, 0, 2).reshape(rows, N_SHARDS * q)


def _my_col_shard(full, chip):
    rows, cols = full.shape
    q = cols // N_SHARDS
    return lax.dynamic_index_in_dim(full.reshape(rows, N_SHARDS, q), chip, axis=1, keepdims=False)


def kernel(x, e_norm_pre, e_norm_post, e_w_in, e_a_conv, e_b_conv, e_b_conv_bias, e_b_ln_g, e_b_ln_b, e_w_out, o_norm_pre, o_norm_post, o_w_in, o_c_w, o_c_b, o_c_scale, o_w_out, loss_target, m_e_norm_pre, m_e_norm_post, m_e_w_in, m_e_a_conv, m_e_b_conv, m_e_b_conv_bias, m_e_b_ln_g, m_e_b_ln_b, m_e_w_out, m_o_norm_pre, m_o_norm_post, m_o_w_in, m_o_c_w, m_o_c_b, m_o_c_scale, m_o_w_out, v_e_norm_pre, v_e_norm_post, v_e_w_in, v_e_a_conv, v_e_b_conv, v_e_b_conv_bias, v_e_b_ln_g, v_e_b_ln_b, v_e_w_out, v_o_norm_pre, v_o_norm_post, v_o_w_in, v_o_c_w, v_o_c_b, v_o_c_scale, v_o_w_out):
    _, s, d = x.shape
    w = d // 2
    c = d
    gc = c // N_GROUPS
    wq, cq, gq = w // N_SHARDS, c // N_SHARDS, gc // N_SHARDS
    chip = 2 * lax.axis_index("x") + lax.axis_index("y")
    core = lax.axis_index("c")
    x2 = x.reshape(s, d)
    target = loss_target.reshape(s, d)

    big_w = [e_w_in[0], e_w_out[0], o_w_in[0], o_c_w[0].reshape(N_GROUPS * gq, gc), o_w_out[0]]
    big_m = [m_e_w_in[0], m_e_w_out[0], m_o_w_in[0], m_o_c_w[0].reshape(N_GROUPS * gq, gc), m_o_w_out[0]]
    big_v = [v_e_w_in[0], v_e_w_out[0], v_o_w_in[0], v_o_c_w[0].reshape(N_GROUPS * gq, gc), v_o_w_out[0]]
    coords = jnp.stack([chip, core]).astype(jnp.int32)
    slots = [_cast_into_slot(a, coords, "cast_%d" % k) for k, a in enumerate(big_w)]
    sharded_small = _flat_rows([e_a_conv[0], e_b_conv[0], o_norm_pre, o_norm_post, o_c_scale, o_c_b[0]])
    small_slots = lax.dynamic_update_index_in_dim(jnp.zeros((N_SHARDS,) + sharded_small.shape, F32), sharded_small,
                                                  chip, 0)
    xi, yi = lax.axis_index("x"), lax.axis_index("y")
    order = jnp.stack([chip, 2 * (1 - xi) + yi, 2 * xi + (1 - yi), 2 * (1 - xi) + (1 - yi)]).astype(jnp.int32)
    h0 = _rmsnorm(x2, e_norm_pre, "e_pre_norm")
    p0, (e_w_in_g, e_w_out_g, small_g4) = _gathered_in_proj(h0, slots[:2], small_slots, order, "e_in_proj")
    e_w_in_sm = e_w_in_g.reshape((N_SHARDS,) + big_w[0].shape)
    e_w_out_f = e_w_out_g.reshape(w + w, d)
    sm = small_g4.reshape(N_SHARDS, -1)
    at = [0]

    def take(rows, q):
        blk = sm[:, at[0]:at[0] + rows * q]
        at[0] += rows * q
        return _col_shards_to_full(blk, rows)

    a_conv_f = take(CONV_A, wq)
    b_conv_f = take(CONV_B, wq)
    o_pre_f = take(1, cq)
    o_post_f = take(1, cq)
    cs_f = take(1, cq)
    cb_f = take(N_GROUPS, gq).reshape(1, c)

    mixer_steps = s // _row_tile(s, 128)
    (u0, x1, y0), odd_g = _even_mixer_fwd(p0, e_w_out_f, x2, e_norm_post, a_conv_f, b_conv_f, e_b_conv_bias, e_b_ln_g,
                                          e_b_ln_b, "e_mixer_out_proj",
                                          comm=_gather_comm(slots[2:], None, max(mixer_steps - 4, 0)))
    o_w_in_sm = odd_g[0].reshape((N_SHARDS,) + big_w[2].shape)
    cw_f = odd_g[1].reshape(N_SHARDS, N_GROUPS, gq, gc).transpose(1, 0, 2, 3).reshape(N_GROUPS, gc, gc)
    o_w_out_f = odd_g[2].reshape(c, d)
    p1, h1 = _norm_matmul(x1, o_pre_f, o_w_in_sm, "o_in_proj")
    u1 = _odd_mixer_fwd(p1, cw_f, cb_f, cs_f, "o_mixer_fwd")
    d_y1, d_x2, d_o_post, loss_part = _matmul_post_loss(u1, o_w_out_f, x1, o_post_f, target, "o_out_proj_loss")

    def as_pieces(g, k):
        return g.reshape(N_SHARDS, 2, big_w[k].shape[0] // 2, big_w[k].shape[1])

    def chip_sums(ks, pieces, from_sibling):
        both = [_chip_sum(g, o, coords, "chip_sum_%d" % k) for k, g, o in zip(ks, pieces, from_sibling)]
        return [b[0] for b in both], [b[1] for b in both]

    g_o_w_out = _matmul_tn(u1, d_y1, 1, "o_w_out_grad")
    d_u1, _ = _matmul_nt(d_y1, o_w_out_f[None], "o_out_proj_bwd")
    d_p1, d_cw, d_cb, d_cs = _odd_mixer_bwd(p1, d_u1, cw_f, cb_f, cs_f, "o_mixer_bwd")
    g_o_w_in = _matmul_tn(h1, d_p1, N_SHARDS, "o_w_in_grad")
    g_cw = d_cw.reshape(N_GROUPS, N_SHARDS, gq, gc).transpose(1, 0, 2, 3).astype(BF16)
    pieces_o = [as_pieces(g_o_w_in, 2), as_pieces(g_cw, 3), as_pieces(g_o_w_out, 4)]
    d_h1, sibling_o = _matmul_nt(d_p1, o_w_in_sm, "o_in_proj_bwd", comm=_exchange_comm(pieces_o))
    d_x1, d_o_pre, d_y0, d_e_post = _norm_bwd(d_h1, x1, o_pre_f, d_x2, "o_pre_norm_bwd", post=(y0, e_norm_post))

    pieces_e = [as_pieces(_matmul_tn(u0, d_y0, 1, "e_w_out_grad"), 1)]
    d_u0, sibling_e = _matmul_nt(d_y0, e_w_out_f[None], "e_out_proj_bwd", comm=_exchange_comm(pieces_e))
    sums_a, landing_a = chip_sums([1, 2, 3, 4], pieces_e + pieces_o, sibling_e + sibling_o)
    (d_p0, d_a_conv, d_b_conv, d_bias, d_ln_g, d_ln_b), landed_a = _even_mixer_bwd(
        p0, d_u0, a_conv_f, b_conv_f, e_b_conv_bias, e_b_ln_g, e_b_ln_b, "e_mixer_bwd",
        comm=_scatter_comm(sums_a, landing_a))
    pieces_b = [as_pieces(_matmul_tn(h0, d_p0, N_SHARDS, "e_w_in_grad"), 0)]
    sums_b, landing_b = chip_sums([0], pieces_b, _comm_only(_exchange_comm(pieces_b), "exchange_core_halves"))
    d_h0, landed_b = _matmul_nt(d_p0, e_w_in_sm, "e_in_proj_bwd", comm=_scatter_comm(sums_b, landing_b))
    grad_x, d_e_pre = _norm_bwd(d_h0, x2, e_norm_pre, d_x1, "e_pre_norm_bwd")

    landed = landed_b + landed_a
    reduced = [_shard_sum(sc, coords, "shard_sum_%d" % k) for k, sc in enumerate(landed)]
    small_parts = _flat_rows([loss_part[0], d_e_pre, d_e_post, d_bias, d_ln_g, d_ln_b, d_a_conv, d_b_conv,
                              d_o_pre, d_o_post, d_cs, d_cb])
    small_rows = lax.dynamic_update_index_in_dim(jnp.zeros((N_DEVICES,) + small_parts.shape, F32), small_parts,
                                                 2 * chip + core, 0)
    joined = _comm_only(_join_comm(reduced, small_rows), "join_core_halves")
    big_g = [j.reshape(a.shape) for j, a in zip(joined[:5], big_w)]
    small_sum = _sum_small(joined[5], "small_sum").reshape(-1)
    (loss_row, g_e_pre, g_e_post, g_bias, g_ln_g, g_ln_b, g_a_conv_f, g_b_conv_f, g_o_pre_f, g_o_post_f, g_cs_f,
     g_cb_f) = _unflatten(small_sum, [(LANES,), (1, d), (1, d), (1, w), (1, w), (1, w), (CONV_A, w), (CONV_B, w),
                                      (1, c), (1, c), (1, c), (1, c)])
    loss = loss_row[0]
    g_a_conv = _my_col_shard(g_a_conv_f, chip)
    g_b_conv = _my_col_shard(g_b_conv_f, chip)
    g_o_pre = _my_col_shard(g_o_pre_f, chip)
    g_o_post = _my_col_shard(g_o_post_f, chip)
    g_cs = _my_col_shard(g_cs_f, chip)
    g_cb = _my_col_shard(g_cb_f.reshape(N_GROUPS, gc), chip)

    big_upd = [_adamw(wt, g, m, v, "adamw_%d" % k) for k, (wt, g, m, v) in enumerate(zip(big_w, big_g, big_m, big_v))]
    small_w = [e_norm_pre, e_norm_post, e_b_conv_bias, e_b_ln_g, e_b_ln_b, e_a_conv[0], e_b_conv[0],
               o_norm_pre, o_norm_post, o_c_b[0], o_c_scale]
    small_m = [m_e_norm_pre, m_e_norm_post, m_e_b_conv_bias, m_e_b_ln_g, m_e_b_ln_b, m_e_a_conv[0], m_e_b_conv[0],
               m_o_norm_pre, m_o_norm_post, m_o_c_b[0], m_o_c_scale]
    small_v = [v_e_norm_pre, v_e_norm_post, v_e_b_conv_bias, v_e_b_ln_g, v_e_b_ln_b, v_e_a_conv[0], v_e_b_conv[0],
               v_o_norm_pre, v_o_norm_post, v_o_c_b[0], v_o_c_scale]
    small_g = [g_e_pre, g_e_post, g_bias, g_ln_g, g_ln_b, g_a_conv, g_b_conv, g_o_pre, g_o_post, g_cb, g_cs]
    small_shapes = [a.shape for a in small_w]
    small_upd = _adamw(_flat_rows(small_w), _flat_rows(small_g), _flat_rows(small_m), _flat_rows(small_v),
                       "adamw_small")
    small_delta, small_new_m, small_new_v = [_unflatten(u.reshape(-1), small_shapes) for u in small_upd]

    def ordered(small, big, lead):
        (n_pre, n_post, bias, ln_g, ln_b, a_conv, b_conv, o_pre, o_post, cb, cs) = small
        (w_in, w_out, ow_in, cw, ow_out) = big
        out = [n_pre, n_post, w_in[None], a_conv[None], b_conv[None], bias, ln_g, ln_b, w_out[None], o_pre, o_post,
               ow_in[None], cw.reshape(1, N_GROUPS, gq, gc), cb[None], cs, ow_out[None]]
        return out

    grads = ordered(small_g, big_g, None)
    deltas = ordered(small_delta, [u[0] for u in big_upd], None)
    new_m = ordered(small_new_m, [u[1] for u in big_upd], None)
    new_v = ordered(small_new_v, [u[2] for u in big_upd], None)
    return (loss, grad_x.reshape(1, s, d), *grads, *deltas, *new_m, *new_v)
```

```python
import functools

import jax
import jax.numpy as jnp
from jax import lax
from jax.experimental import pallas as pl
from jax.experimental.pallas import tpu as pltpu

F32 = jnp.float32
BF16 = jnp.bfloat16
MESH = pl.DeviceIdType.MESH

EPS = 1e-6
CONV_A = 3
CONV_B = 31
POOL_WINDOWS = (2, 4, 8, 16)
N_GROUPS = len(POOL_WINDOWS)
N_SHARDS = 4
N_DEVICES = 8
ADAM_LR = 0.001
ADAM_B1 = 0.9
ADAM_B2 = 0.999
ADAM_EPS = 1e-08
ADAM_WD = 0.01
ADAM_STEP = 10

LANES = 128
SUBLANES_BF16 = 16
HALO_A = 8
HALO_B = 32
HALO_P = 16
SHIFTS = 8
ROW_BLOCK = 32
LANE_BLOCK = 256
VMEM_LIMIT = 56 * 1024 * 1024
TN_ACC_BYTES = 8 * 1024 * 1024
EVEN_BWD_ROWS = 128


def _row_tile(n, pref):
    t = max(min(n, pref) // SUBLANES_BF16, 1) * SUBLANES_BF16
    while t > SUBLANES_BF16 and (n % t or t % SUBLANES_BF16):
        t -= SUBLANES_BF16
    assert n % t == 0, (n, pref)
    return t


def _col_chunk(n, pref):
    t = (min(n, pref) // LANES) * LANES
    while t > LANES and n % t:
        t -= LANES
    assert t >= LANES and n % t == 0, (n, pref)
    return t


def _params(*sem):
    return pltpu.CompilerParams(dimension_semantics=tuple(sem) if sem else None, vmem_limit_bytes=VMEM_LIMIT)


ANY = pl.BlockSpec(memory_space=pl.ANY)


class _Comm:
    def __init__(self, srcs, bufs, n_sems, phases, finish):
        self.srcs, self.bufs, self.n_sems, self.phases, self.finish = list(srcs), list(bufs), n_sems, phases, finish


def _call(body, *, name, grid, in_specs, out_specs, out_shape, args, scratch_shapes=(), comm=None):
    params = _params(*(("arbitrary",) * len(grid)))
    if comm is None:
        out = pl.pallas_call(body, name=name, grid=grid, in_specs=in_specs, out_specs=out_specs, out_shape=out_shape,
                             scratch_shapes=scratch_shapes, compiler_params=params)(*args)
        return list(out), []
    n_in, n_out, n_scr = len(in_specs), len(out_specs), len(scratch_shapes)
    ns, nb = len(comm.srcs), len(comm.bufs)
    total = 1
    for size in grid:
        total *= size

    def fused(*refs):
        ins, srcs = refs[:n_in], refs[n_in:n_in + ns]
        at = n_in + ns + nb
        outs, bufs = refs[at:at + n_out], refs[at + n_out:at + n_out + nb]
        scratch = refs[at + n_out + nb:at + n_out + nb + n_scr]
        send_sems, recv_sems = refs[-2:]
        step = 0
        for axis, size in enumerate(grid):
            step = step * size + pl.program_id(axis)
        for when, fn in comm.phases:
            pl.when(step == when)(functools.partial(fn, srcs, bufs, send_sems, recv_sems))
        body(*ins, *outs, *scratch)
        pl.when(step == total - 1)(functools.partial(comm.finish, srcs, bufs, send_sems, recv_sems))

    out = pl.pallas_call(
        fused, name=name, grid=grid,
        in_specs=list(in_specs) + [ANY] * (ns + nb), out_specs=list(out_specs) + [ANY] * nb,
        out_shape=list(out_shape) + [jax.ShapeDtypeStruct(b.shape, b.dtype) for b in comm.bufs],
        input_output_aliases={n_in + ns + i: n_out + i for i in range(nb)},
        scratch_shapes=list(scratch_shapes) + [pltpu.SemaphoreType.DMA((comm.n_sems,))] * 2,
        compiler_params=params,
    )(*args, *comm.srcs, *comm.bufs)
    return list(out[:n_out]), list(out[n_out:])


def _comm_only(comm, name):
    ns, nb = len(comm.srcs), len(comm.bufs)

    def body(*refs):
        srcs, bufs = refs[:ns], refs[ns + nb:ns + 2 * nb]
        send_sems, recv_sems = refs[-2:]
        for _, fn in comm.phases:
            fn(srcs, bufs, send_sems, recv_sems)
        comm.finish(srcs, bufs, send_sems, recv_sems)

    return pl.pallas_call(
        body, name=name, in_specs=[ANY] * (ns + nb), out_specs=[ANY] * nb,
        out_shape=[jax.ShapeDtypeStruct(b.shape, b.dtype) for b in comm.bufs],
        input_output_aliases={ns + i: i for i in range(nb)},
        scratch_shapes=[pltpu.SemaphoreType.DMA((comm.n_sems,))] * 2,
    )(*comm.srcs, *comm.bufs)


def _sigmoid(v):
    return jax.nn.sigmoid(v)


def _dsilu(v, s):
    return s * (1.0 + v * (1.0 - s))


def _mean_last(v):
    return jnp.mean(v, axis=-1, keepdims=True)


def _sum_rows(v):
    return jnp.sum(v, axis=0, keepdims=True)


def _norm_matmul(x, gain, w_sm, name):
    s, d = x.shape
    n_sh, _, ns = w_sm.shape
    tm = _row_tile(s, 1024)

    def body(x_ref, g_ref, w_ref, p_ref, h_ref):
        @pl.when(pl.program_id(1) == 0)
        def _():
            xv = x_ref[...]
            r = lax.rsqrt(_mean_last(xv * xv) + EPS)
            h_ref[...] = (xv * r * g_ref[...]).astype(BF16)

        p_ref[...] = jnp.dot(h_ref[...], w_ref[0], preferred_element_type=F32)

    return _call(
        body, name=name, grid=(s // tm, n_sh),
        in_specs=[pl.BlockSpec((tm, d), lambda i, j: (i, 0)),
                  pl.BlockSpec((1, d), lambda i, j: (0, 0)),
                  pl.BlockSpec((1, d, ns), lambda i, j: (j, 0, 0))],
        out_specs=[pl.BlockSpec((tm, ns), lambda i, j: (i, j)),
                   pl.BlockSpec((tm, d), lambda i, j: (i, 0))],
        out_shape=[jax.ShapeDtypeStruct((s, n_sh * ns), F32), jax.ShapeDtypeStruct((s, d), BF16)],
        args=(x, gain, w_sm))[0]


def _matmul_post_loss(u, w, x_res, gain, target, name):
    s, k = u.shape
    d = w.shape[1]
    tm = _row_tile(s, 256)

    def body(u_ref, w_ref, x_ref, g_ref, t_ref, dy_ref, dout_ref, dg_ref, loss_ref):
        @pl.when(pl.program_id(0) == 0)
        def _():
            dg_ref[...] = jnp.zeros_like(dg_ref)
            loss_ref[...] = jnp.zeros_like(loss_ref)

        y = jnp.dot(u_ref[...], w_ref[...], preferred_element_type=F32)
        r = lax.rsqrt(_mean_last(y * y) + EPS)
        n = y * r
        g = g_ref[...]
        err = x_ref[...] + n * g - t_ref[...]
        loss_ref[...] += 0.5 * jnp.sum(_mean_last(err * err))
        dout = err * (1.0 / d)
        dout_ref[...] = dout
        dg_ref[...] += _sum_rows(dout * n)
        dn = dout * g
        dy_ref[...] = (r * (dn - n * _mean_last(dn * n))).astype(BF16)

    return pl.pallas_call(
        body, name=name, grid=(s // tm,),
        in_specs=[pl.BlockSpec((tm, k), lambda i: (i, 0)),
                  pl.BlockSpec((k, d), lambda i: (0, 0)),
                  pl.BlockSpec((tm, d), lambda i: (i, 0)),
                  pl.BlockSpec((1, d), lambda i: (0, 0)),
                  pl.BlockSpec((tm, d), lambda i: (i, 0))],
        out_specs=[pl.BlockSpec((tm, d), lambda i: (i, 0)),
                   pl.BlockSpec((tm, d), lambda i: (i, 0)),
                   pl.BlockSpec((1, d), lambda i: (0, 0)),
                   pl.BlockSpec((8, LANES), lambda i: (0, 0))],
        out_shape=[jax.ShapeDtypeStruct((s, d), BF16), jax.ShapeDtypeStruct((s, d), F32),
                   jax.ShapeDtypeStruct((1, d), F32), jax.ShapeDtypeStruct((8, LANES), F32)],
        compiler_params=_params("arbitrary"),
    )(u, w, x_res, gain, target)


def _matmul_nt(a, w_sm, name, comm=None):
    s, ncols = a.shape
    n_sh, r, ns = w_sm.shape
    assert ncols == n_sh * ns
    tm = _row_tile(s, 1024)
    nc = _col_chunk(ns, 1792)
    per = ns // nc
    steps = n_sh * per

    def body(a_ref, w_ref, o_ref):
        part = lax.dot_general(a_ref[...], w_ref[0], (((1,), (1,)), ((), ())), preferred_element_type=F32)

        @pl.when(pl.program_id(1) == 0)
        def _():
            o_ref[...] = part

        @pl.when(pl.program_id(1) > 0)
        def _():
            o_ref[...] += part

    out, bufs = _call(
        body, name=name, grid=(s // tm, steps),
        in_specs=[pl.BlockSpec((tm, nc), lambda i, j: (i, j)),
                  pl.BlockSpec((1, r, nc), lambda i, j: (j // per, 0, j % per))],
        out_specs=[pl.BlockSpec((tm, r), lambda i, j: (i, 0))],
        out_shape=[jax.ShapeDtypeStruct((s, r), F32)],
        args=(a, w_sm), comm=comm)
    return out[0], bufs


def _matmul_tn(a, b, n_sh, name):
    s, k = a.shape
    n = b.shape[1]
    ns = n // n_sh
    tk = _col_chunk(k, TN_ACC_BYTES // (4 * ns))
    ts = _row_tile(s, 2048)
    n_s = s // ts

    def body(a_ref, b_ref, o_ref, acc_ref):
        part = lax.dot_general(a_ref[...], b_ref[...], (((0,), (0,)), ((), ())), preferred_element_type=F32)

        @pl.when(pl.program_id(2) == 0)
        def _():
            acc_ref[...] = part

        @pl.when(pl.program_id(2) > 0)
        def _():
            acc_ref[...] += part

        @pl.when(pl.program_id(2) == n_s - 1)
        def _():
            o_ref[0] = acc_ref[...].astype(BF16)

    return pl.pallas_call(
        body, name=name, grid=(n_sh, k // tk, n_s),
        in_specs=[pl.BlockSpec((ts, tk), lambda j, i, t: (t, i)),
                  pl.BlockSpec((ts, ns), lambda j, i, t: (t, j))],
        out_specs=pl.BlockSpec((1, tk, ns), lambda j, i, t: (j, i, 0)),
        out_shape=jax.ShapeDtypeStruct((n_sh, k, ns), BF16),
        scratch_shapes=[pltpu.VMEM((tk, ns), F32)],
        compiler_params=_params("arbitrary", "arbitrary", "arbitrary"),
    )(a, b)


def _norm_bwd(dh, x, gain, dres, name, post=None):
    s, d = x.shape
    tm = _row_tile(s, 256)
    with_post = post is not None

    def rms_bwd(dout, v, g):
        r = lax.rsqrt(_mean_last(v * v) + EPS)
        n = v * r
        dn = dout * g
        return r * (dn - n * _mean_last(dn * n)), _sum_rows(dout * n)

    def body(*refs):
        if with_post:
            dh_ref, x_ref, g_ref, dres_ref, y_ref, gp_ref, dx_ref, dg_ref, dy_ref, dgp_ref = refs
        else:
            dh_ref, x_ref, g_ref, dres_ref, dx_ref, dg_ref = refs

        @pl.when(pl.program_id(0) == 0)
        def _():
            dg_ref[...] = jnp.zeros_like(dg_ref)
            if with_post:
                dgp_ref[...] = jnp.zeros_like(dgp_ref)

        dv, dg = rms_bwd(dh_ref[...], x_ref[...], g_ref[...])
        dx = dres_ref[...] + dv
        dx_ref[...] = dx
        dg_ref[...] += dg
        if with_post:
            dy, dgp = rms_bwd(dx, y_ref[...], gp_ref[...])
            dy_ref[...] = dy.astype(BF16)
            dgp_ref[...] += dgp

    row = pl.BlockSpec((tm, d), lambda i: (i, 0))
    vec = pl.BlockSpec((1, d), lambda i: (0, 0))
    in_specs = [row, row, vec, row]
    out_specs = [row, vec]
    out_shape = [jax.ShapeDtypeStruct((s, d), F32), jax.ShapeDtypeStruct((1, d), F32)]
    args = [dh, x, gain, dres]
    if with_post:
        in_specs += [row, vec]
        out_specs += [row, vec]
        out_shape += [jax.ShapeDtypeStruct((s, d), BF16), jax.ShapeDtypeStruct((1, d), F32)]
        args += list(post)
    return _call(body, name=name, grid=(s // tm,), in_specs=in_specs, out_specs=out_specs, out_shape=out_shape,
                 args=args)[0]


def _fill_shifted_down(sh, rows):
    for b in range(1, SHIFTS):
        sh[b, SHIFTS:rows, :] = sh[0, SHIFTS - b:rows - b, :]


def _fill_shifted_up(sh, rows):
    for b in range(1, SHIFTS):
        sh[b, 0:rows - SHIFTS, :] = sh[0, b:rows - SHIFTS + b, :]


def _for_blocks(ts, w, fn):
    lb = min(LANE_BLOCK, w)
    for l0 in range(0, w, lb):
        def rows(rb, carry, l0=l0):
            fn(pl.multiple_of(rb * ROW_BLOCK, ROW_BLOCK), slice(l0, l0 + lb))
            return carry

        lax.fori_loop(0, ts // ROW_BLOCK, rows, 0)


TAP_SPAN = SHIFTS * ((CONV_B - 1) // SHIFTS)
WINDOW = ROW_BLOCK + TAP_SPAN


def _taps_of(b):
    return [(a, SHIFTS * a + b) for a in range((CONV_B - 1 - b) // SHIFTS + 1)]


def _conv31(sh, base, step, wt_ref, bias_ref, out_ref, ts, w):
    low = min(0, step * (TAP_SPAN // SHIFTS))

    def block(r0, lanes):
        acc = [jnp.zeros((SHIFTS, lanes.stop - lanes.start), F32) for _ in range(ROW_BLOCK // SHIFTS)]
        for b in range(SHIFTS):
            window = sh[b, pl.ds(pl.multiple_of(r0 + (base + low), SHIFTS), WINDOW), lanes]
            for a, j in (_taps_of(b) if step > 0 else reversed(_taps_of(b))):
                at = step * a - low
                wt = wt_ref[CONV_B - 1 - j, :, lanes]
                acc = [v + wt * window[at + SHIFTS * r:at + SHIFTS * (r + 1), :] for r, v in enumerate(acc)]
        for r, v in enumerate(acc):
            if bias_ref is not None:
                v = v + bias_ref[:, lanes]
            out_ref[pl.ds(pl.multiple_of(r0 + SHIFTS * r, SHIFTS), SHIFTS), lanes] = v

    _for_blocks(ts, w, block)


def _conv31_weight_grad(d_sh, x_sh, wacc, ts, w):
    def block(r0, lanes):
        d = d_sh[0, pl.ds(r0, ROW_BLOCK), lanes]
        for b in range(SHIFTS):
            window = x_sh[b, pl.ds(pl.multiple_of(r0 + (HALO_B - TAP_SPAN), SHIFTS), WINDOW), lanes]
            for a, j in _taps_of(b):
                at = TAP_SPAN - SHIFTS * a
                prod = d * window[at:at + ROW_BLOCK, :]
                part = prod[0:SHIFTS, :]
                for q in range(1, ROW_BLOCK // SHIFTS):
                    part = part + prod[q * SHIFTS:(q + 1) * SHIFTS, :]
                wacc[CONV_B - 1 - j, :, lanes] += part

    _for_blocks(ts, w, block)


def _even_forward_tile(p_ref, halo_ref, first, a_conv_ref, b_conv_ref, bias_ref, lng_ref, lnb_ref, qbuf, ysh, y1buf,
                       wb, w, ts):
    @pl.when(pl.program_id(0) == 0)
    def _():
        for k in range(CONV_B):
            wb[k] = jnp.broadcast_to(b_conv_ref[k:k + 1, :], (SHIFTS, w))

    def col(ref, k, rows=slice(None)):
        return ref[rows, k * w:(k + 1) * w]

    a_x, a_b, a_c, a_z = col(p_ref, 0), col(p_ref, 1), col(p_ref, 2), col(p_ref, 3)
    b_val, b_gate, b_z = col(p_ref, 4), col(p_ref, 5), col(p_ref, 6)
    keep = jnp.where(first, 0.0, 1.0)

    rows_a = slice(HALO_B - HALO_A, HALO_B)
    qbuf[0:HALO_A, :] = col(halo_ref, 2, rows_a) * col(halo_ref, 0, rows_a) * keep
    qbuf[HALO_A:HALO_A + ts, :] = a_c * a_x
    cq = jnp.zeros((ts, w), F32)
    for j in range(CONV_A):
        cq = cq + a_conv_ref[CONV_A - 1 - j:CONV_A - j, :] * qbuf[HALO_A - j:HALO_A - j + ts, :]
    ya = a_b * cq

    ysh[0, 0:HALO_B, :] = col(halo_ref, 4) * _sigmoid(col(halo_ref, 5)) * keep
    ysh[0, HALO_B:HALO_B + ts, :] = b_val * _sigmoid(b_gate)
    _fill_shifted_down(ysh, HALO_B + ts)
    _conv31(ysh, HALO_B, -SHIFTS, wb, bias_ref, y1buf, ts, w)
    yb1 = y1buf[...]
    xc = yb1 - _mean_last(yb1)
    rstd = lax.rsqrt(_mean_last(xc * xc) + EPS)
    xhat = xc * rstd
    yb2 = xhat * lng_ref[...] + lnb_ref[...]
    return dict(a_x=a_x, a_b=a_b, a_c=a_c, a_z=a_z, b_val=b_val, b_gate=b_gate, b_z=b_z,
                cq=cq, ya=ya, rstd=rstd, xhat=xhat, yb2=yb2)


def _even_specs(s, w, ts):
    tile = pl.BlockSpec((ts, 7 * w), lambda i: (i, 0))
    halo = pl.BlockSpec((HALO_B, 7 * w), lambda i: (jnp.maximum(i * (ts // HALO_B) - 1, 0), 0))
    return tile, halo


def _small_specs(shapes, index=lambda i: (0, 0)):
    return [pl.BlockSpec(sh, index) for sh in shapes]


def _even_mixer_fwd(p, w_out, x_res, gain, a_conv, b_conv, bias, ln_g, ln_b, name, comm=None):
    s, d = x_res.shape
    w = p.shape[1] // 7
    ts = _row_tile(s, 128)
    assert ts % HALO_B == 0

    def body(p_ref, halo_ref, wout_ref, x_ref, g_ref, ac_ref, bc_ref, bias_ref, lng_ref, lnb_ref,
             u_ref, xn_ref, y_ref, qbuf, ysh, y1buf, wb):
        first = pl.program_id(0) == 0
        f = _even_forward_tile(p_ref, halo_ref, first, ac_ref, bc_ref, bias_ref, lng_ref, lnb_ref, qbuf, ysh, y1buf,
                               wb, w, ts)
        yb3 = f["yb2"] * _sigmoid(f["yb2"])
        u_a = (f["ya"] * (f["a_z"] * _sigmoid(f["a_z"]))).astype(BF16)
        u_b = (yb3 * (f["b_z"] * _sigmoid(f["b_z"]))).astype(BF16)
        u_ref[:, 0:w] = u_a
        u_ref[:, w:2 * w] = u_b
        y = (jnp.dot(u_a, wout_ref[0:w, :], preferred_element_type=F32)
             + jnp.dot(u_b, wout_ref[w:2 * w, :], preferred_element_type=F32))
        r = lax.rsqrt(_mean_last(y * y) + EPS)
        y_ref[...] = y
        xn_ref[...] = x_ref[...] + (y * r) * g_ref[...]

    tile, halo = _even_specs(s, w, ts)
    row = pl.BlockSpec((ts, d), lambda i: (i, 0))
    return _call(
        body, name=name, grid=(s // ts,),
        in_specs=[tile, halo, pl.BlockSpec((2 * w, d), lambda i: (0, 0)), row, pl.BlockSpec((1, d), lambda i: (0, 0))]
        + _small_specs([(CONV_A, w), (CONV_B, w), (1, w), (1, w), (1, w)]),
        out_specs=[pl.BlockSpec((ts, 2 * w), lambda i: (i, 0)), row, row],
        out_shape=[jax.ShapeDtypeStruct((s, 2 * w), BF16), jax.ShapeDtypeStruct((s, d), F32),
                   jax.ShapeDtypeStruct((s, d), F32)],
        scratch_shapes=[pltpu.VMEM((HALO_A + ts, w), F32), pltpu.VMEM((SHIFTS, HALO_B + ts, w), F32),
                        pltpu.VMEM((ts, w), F32), pltpu.VMEM((CONV_B, SHIFTS, w), F32)],
        args=(p, p, w_out, x_res, gain, a_conv, b_conv, bias, ln_g, ln_b), comm=comm)


def _even_mixer_bwd(p, du, a_conv, b_conv, bias, ln_g, ln_b, name, comm=None):
    s = p.shape[0]
    w = p.shape[1] // 7
    ts = _row_tile(s, EVEN_BWD_ROWS)
    nt = s // ts
    assert ts % HALO_B == 0

    def body(p_ref, halo_ref, du_ref, ac_ref, bc_ref, bias_ref, lng_ref, lnb_ref,
             dp_ref, dac_ref, dbc_ref, dbias_ref, dlng_ref, dlnb_ref,
             qbuf, ysh, y1buf, dqbuf, dsh, dy0buf, wacc, carry_dq, carry_dy, wb):
        step = pl.program_id(0)
        first = step == nt - 1

        @pl.when(step == 0)
        def _():
            for ref in (dac_ref, dbias_ref, dlng_ref, dlnb_ref, wacc, carry_dq, carry_dy):
                ref[...] = jnp.zeros_like(ref)

        f = _even_forward_tile(p_ref, halo_ref, first, ac_ref, bc_ref, bias_ref, lng_ref, lnb_ref, qbuf, ysh, y1buf,
                               wb, w, ts)
        a_z, b_z, yb2 = f["a_z"], f["b_z"], f["yb2"]
        s_az, s_bz, s_y2 = _sigmoid(a_z), _sigmoid(b_z), _sigmoid(yb2)
        du_a = du_ref[:, 0:w]
        du_b = du_ref[:, w:2 * w]

        d_ya = du_a * (a_z * s_az)
        dp_ref[:, 3 * w:4 * w] = (du_a * f["ya"] * _dsilu(a_z, s_az)).astype(BF16)
        dp_ref[:, 1 * w:2 * w] = (d_ya * f["cq"]).astype(BF16)
        d_cq = d_ya * f["a_b"]
        dqbuf[0:ts, :] = d_cq
        dqbuf[ts:ts + HALO_A, :] = carry_dq[...]
        d_q = jnp.zeros((ts, w), F32)
        for o in range(CONV_A):
            d_q = d_q + ac_ref[CONV_A - 1 - o:CONV_A - o, :] * dqbuf[o:o + ts, :]
        for j in range(CONV_A):
            k = CONV_A - 1 - j
            dac_ref[k:k + 1, :] += _sum_rows(d_cq * qbuf[HALO_A - j:HALO_A - j + ts, :])
        carry_dq[...] = d_cq[0:HALO_A, :]
        dp_ref[:, 2 * w:3 * w] = (d_q * f["a_x"]).astype(BF16)
        dp_ref[:, 0 * w:1 * w] = (d_q * f["a_c"]).astype(BF16)

        d_yb3 = du_b * (b_z * s_bz)
        dp_ref[:, 6 * w:7 * w] = (du_b * (yb2 * s_y2) * _dsilu(b_z, s_bz)).astype(BF16)
        d_yb2 = d_yb3 * _dsilu(yb2, s_y2)
        xhat = f["xhat"]
        dlng_ref[...] += _sum_rows(d_yb2 * xhat)
        dlnb_ref[...] += _sum_rows(d_yb2)
        d_xh = d_yb2 * lng_ref[...]
        d_yb1 = f["rstd"] * (d_xh - _mean_last(d_xh) - xhat * _mean_last(d_xh * xhat))
        dbias_ref[...] += _sum_rows(d_yb1)
        dsh[0, 0:ts, :] = d_yb1
        dsh[0, ts:ts + HALO_B, :] = carry_dy[...]
        carry_dy[...] = d_yb1[0:HALO_B, :]
        _fill_shifted_up(dsh, ts + HALO_B)
        _conv31(dsh, 0, SHIFTS, wb, None, dy0buf, ts, w)
        _conv31_weight_grad(dsh, ysh, wacc, ts, w)

        @pl.when(step == nt - 1)
        def _():
            for k in range(CONV_B):
                dbc_ref[k:k + 1, :] = _sum_rows(wacc[k])

        d_yb0 = dy0buf[...]
        s_g = _sigmoid(f["b_gate"])
        dp_ref[:, 4 * w:5 * w] = (d_yb0 * s_g).astype(BF16)
        dp_ref[:, 5 * w:6 * w] = (d_yb0 * f["b_val"] * s_g * (1.0 - s_g)).astype(BF16)

    rev = lambda i: (nt - 1 - i, 0)
    tile = pl.BlockSpec((ts, 7 * w), rev)
    halo = pl.BlockSpec((HALO_B, 7 * w), lambda i: (jnp.maximum((nt - 1 - i) * (ts // HALO_B) - 1, 0), 0))
    small = [(CONV_A, w), (CONV_B, w), (1, w), (1, w), (1, w)]
    return _call(
        body, name=name, grid=(nt,),
        in_specs=[tile, halo, pl.BlockSpec((ts, 2 * w), rev)] + _small_specs(small),
        out_specs=[pl.BlockSpec((ts, 7 * w), rev)] + _small_specs(small),
        out_shape=[jax.ShapeDtypeStruct((s, 7 * w), BF16)] + [jax.ShapeDtypeStruct(sh, F32) for sh in small],
        scratch_shapes=[pltpu.VMEM((HALO_A + ts, w), F32), pltpu.VMEM((SHIFTS, HALO_B + ts, w), F32),
                        pltpu.VMEM((ts, w), F32),
                        pltpu.VMEM((ts + HALO_A, w), F32), pltpu.VMEM((SHIFTS, ts + HALO_B, w), F32),
                        pltpu.VMEM((ts, w), F32), pltpu.VMEM((CONV_B, SHIFTS, w), F32),
                        pltpu.VMEM((HALO_A, w), F32), pltpu.VMEM((HALO_B, w), F32),
                        pltpu.VMEM((CONV_B, SHIFTS, w), F32)],
        args=(p, p, du, a_conv, b_conv, bias, ln_g, ln_b), comm=comm)


def _pool_forward_tile(p_ref, halo_ref, first, tile_index, cw_ref, cb_ref, cs_ref, vbuf, c, gc, ts):
    vbuf[0:HALO_P, :] = halo_ref[...] * jnp.where(first, 0.0, 1.0)
    vbuf[HALO_P:HALO_P + ts, :] = p_ref[:, 0:c]
    pos = tile_index * ts + lax.broadcasted_iota(jnp.int32, (ts, 1), 0) + 1
    pooled, inv, gout = [], [], []
    for g, win in enumerate(POOL_WINDOWS):
        cols = slice(g * gc, (g + 1) * gc)
        acc = jnp.zeros((ts, gc), F32)
        for j in range(win):
            acc = acc + vbuf[HALO_P - j:HALO_P - j + ts, cols]
        inv_g = 1.0 / jnp.minimum(pos, win).astype(F32)
        pooled_g = (acc * inv_g - p_ref[:, cols]).astype(BF16)
        pooled.append(pooled_g)
        inv.append(inv_g)
        gout.append(jnp.dot(pooled_g, cw_ref[g], preferred_element_type=F32) + cb_ref[:, cols])
    return pooled, inv, gout


def _odd_mixer_fwd(p, cw, cb, cs, name):
    s = p.shape[0]
    c = p.shape[1] // 2
    gc = c // N_GROUPS
    ts = _row_tile(s, 256)

    def body(p_ref, halo_ref, cw_ref, cb_ref, cs_ref, u_ref, vbuf):
        i = pl.program_id(0)
        _, _, gout = _pool_forward_tile(p_ref, halo_ref, i == 0, i, cw_ref, cb_ref, cs_ref, vbuf, c, gc, ts)
        for g in range(N_GROUPS):
            cols = slice(g * gc, (g + 1) * gc)
            z = p_ref[:, c + g * gc:c + (g + 1) * gc]
            u_ref[:, cols] = (gout[g] * cs_ref[:, cols] * (z * _sigmoid(z))).astype(BF16)

    return pl.pallas_call(
        body, name=name, grid=(s // ts,),
        in_specs=[pl.BlockSpec((ts, 2 * c), lambda i: (i, 0)),
                  pl.BlockSpec((HALO_P, c), lambda i: (jnp.maximum(i * (ts // HALO_P) - 1, 0), 0)),
                  pl.BlockSpec((N_GROUPS, gc, gc), lambda i: (0, 0, 0)),
                  pl.BlockSpec((1, c), lambda i: (0, 0)), pl.BlockSpec((1, c), lambda i: (0, 0))],
        out_specs=pl.BlockSpec((ts, c), lambda i: (i, 0)),
        out_shape=jax.ShapeDtypeStruct((s, c), BF16),
        scratch_shapes=[pltpu.VMEM((HALO_P + ts, c), F32)],
        compiler_params=_params("arbitrary"),
    )(p, p, cw, cb, cs)


def _odd_mixer_bwd(p, du, cw, cb, cs, name):
    s = p.shape[0]
    c = p.shape[1] // 2
    gc = c // N_GROUPS
    ts = _row_tile(s, 256)
    nt = s // ts

    def body(p_ref, halo_ref, du_ref, cw_ref, cb_ref, cs_ref, dp_ref, dcw_ref, dcb_ref, dcs_ref, vbuf, ebuf, carry_e):
        step = pl.program_id(0)
        tile_index = nt - 1 - step

        @pl.when(step == 0)
        def _():
            for ref in (dcw_ref, dcb_ref, dcs_ref, carry_e):
                ref[...] = jnp.zeros_like(ref)

        pooled, inv, gout = _pool_forward_tile(p_ref, halo_ref, tile_index == 0, tile_index, cw_ref, cb_ref, cs_ref,
                                               vbuf, c, gc, ts)
        ebuf[ts:ts + HALO_P, :] = carry_e[...]
        for g, win in enumerate(POOL_WINDOWS):
            cols = slice(g * gc, (g + 1) * gc)
            z = p_ref[:, c + g * gc:c + (g + 1) * gc]
            sz = _sigmoid(z)
            du_g = du_ref[:, cols]
            scale = cs_ref[:, cols]
            d_y = du_g * (z * sz)
            dp_ref[:, c + g * gc:c + (g + 1) * gc] = (du_g * (gout[g] * scale) * _dsilu(z, sz)).astype(BF16)
            dcs_ref[:, cols] += _sum_rows(d_y * gout[g])
            d_gout = d_y * scale
            dcb_ref[:, cols] += _sum_rows(d_gout)
            d_gout_b = d_gout.astype(BF16)
            dcw_ref[g] += lax.dot_general(pooled[g], d_gout_b, (((0,), (0,)), ((), ())), preferred_element_type=F32)
            d_pool = lax.dot_general(d_gout_b, cw_ref[g], (((1,), (1,)), ((), ())), preferred_element_type=F32)
            e = d_pool * inv[g]
            ebuf[0:ts, cols] = e
            d_v = -d_pool
            for o in range(win):
                d_v = d_v + ebuf[o:o + ts, cols]
            dp_ref[:, cols] = d_v.astype(BF16)
            carry_e[:, cols] = e[0:HALO_P, :]

    rev = lambda i: (nt - 1 - i, 0)
    small = [(N_GROUPS, gc, gc), (1, c), (1, c)]
    return pl.pallas_call(
        body, name=name, grid=(nt,),
        in_specs=[pl.BlockSpec((ts, 2 * c), rev),
                  pl.BlockSpec((HALO_P, c), lambda i: (jnp.maximum((nt - 1 - i) * (ts // HALO_P) - 1, 0), 0)),
                  pl.BlockSpec((ts, c), rev),
                  pl.BlockSpec((N_GROUPS, gc, gc), lambda i: (0, 0, 0)),
                  pl.BlockSpec((1, c), lambda i: (0, 0)), pl.BlockSpec((1, c), lambda i: (0, 0))],
        out_specs=[pl.BlockSpec((ts, 2 * c), rev),
                   pl.BlockSpec((N_GROUPS, gc, gc), lambda i: (0, 0, 0)),
                   pl.BlockSpec((1, c), lambda i: (0, 0)), pl.BlockSpec((1, c), lambda i: (0, 0))],
        out_shape=[jax.ShapeDtypeStruct((s, 2 * c), BF16)] + [jax.ShapeDtypeStruct(sh, F32) for sh in small],
        scratch_shapes=[pltpu.VMEM((HALO_P + ts, c), F32), pltpu.VMEM((ts + HALO_P, c), F32),
                        pltpu.VMEM((HALO_P, c), F32)],
        compiler_params=_params("arbitrary"),
    )(p, p, du, cw, cb, cs)


def _cast_into_slot(a, coords, name):
    r, cols = a.shape
    tr = _row_tile(r // 2, 256)
    per = r // 2 // tr

    def body(co_ref, a_ref, o_ref):
        o_ref[0, 0] = a_ref[...].astype(BF16)

    return pl.pallas_call(
        body, name=name,
        grid_spec=pltpu.PrefetchScalarGridSpec(
            num_scalar_prefetch=1, grid=(2, per),
            in_specs=[pl.BlockSpec((tr, cols), lambda h, i, co: (h * per + i, 0))],
            out_specs=pl.BlockSpec((1, 1, tr, cols), lambda h, i, co: (co[0], h, i, 0))),
        out_shape=jax.ShapeDtypeStruct((N_SHARDS, 2, r // 2, cols), BF16),
        compiler_params=_params("arbitrary", "arbitrary"),
    )(coords, a)


def _chip_sum(g, other, coords, name):
    n_sh, _, r2, cols = g.shape
    tr = _row_tile(r2, 256)

    def body(co_ref, g_ref, o_ref, sum_ref, mine_ref):
        v = (g_ref[0, 0].astype(F32) + o_ref[0].astype(F32)).astype(BF16)
        sum_ref[0] = v

        @pl.when(pl.program_id(1) == co_ref[0])
        def _():
            mine_ref[0] = v

    piece = pl.BlockSpec((1, tr, cols), lambda i, s, co: (s, i, 0))
    return pl.pallas_call(
        body, name=name,
        grid_spec=pltpu.PrefetchScalarGridSpec(
            num_scalar_prefetch=1, grid=(r2 // tr, n_sh),
            in_specs=[pl.BlockSpec((1, 1, tr, cols), lambda i, s, co: (s, co[1], i, 0)), piece],
            out_specs=[piece, pl.BlockSpec((1, tr, cols), lambda i, s, co: (co[0], i, 0))]),
        out_shape=[jax.ShapeDtypeStruct((n_sh, r2, cols), BF16)] * 2,
        compiler_params=_params("arbitrary", "arbitrary"),
    )(coords, g, other)


def _shard_sum(pieces, coords, name):
    n_sh, r2, cols = pieces.shape
    tr = _row_tile(r2, 256)

    def body(co_ref, p_ref, o_ref):
        acc = p_ref[0].astype(F32)
        for k in range(1, n_sh):
            acc = acc + p_ref[k].astype(F32)
        o_ref[0] = acc

    return pl.pallas_call(
        body, name=name,
        grid_spec=pltpu.PrefetchScalarGridSpec(
            num_scalar_prefetch=1, grid=(r2 // tr,),
            in_specs=[pl.BlockSpec((n_sh, tr, cols), lambda i, co: (0, i, 0))],
            out_specs=pl.BlockSpec((1, tr, cols), lambda i, co: (co[1], i, 0))),
        out_shape=jax.ShapeDtypeStruct((2, r2, cols), F32),
        compiler_params=_params("arbitrary"),
    )(coords, pieces)


def _sum_small(a, name):
    n, r, cols = a.shape

    def body(a_ref, o_ref):
        acc = a_ref[0]
        for k in range(1, n):
            acc = acc + a_ref[k]
        o_ref[...] = acc

    return pl.pallas_call(
        body, name=name,
        in_specs=[pl.BlockSpec((n, r, cols), lambda: (0, 0, 0))],
        out_specs=pl.BlockSpec((r, cols), lambda: (0, 0)),
        out_shape=jax.ShapeDtypeStruct((r, cols), F32),
        compiler_params=_params(),
    )(a)


def _adamw(w, g, m, v, name):
    r, cols = w.shape
    tr = _row_tile(r, 256) if r % SUBLANES_BF16 == 0 else r

    def body(w_ref, g_ref, m_ref, v_ref, d_ref, nm_ref, nv_ref):
        g = g_ref[...]
        m = ADAM_B1 * m_ref[...] + (1.0 - ADAM_B1) * g
        v = ADAM_B2 * v_ref[...] + (1.0 - ADAM_B2) * (g * g)
        m_hat = m / (1.0 - ADAM_B1 ** ADAM_STEP)
        v_hat = v / (1.0 - ADAM_B2 ** ADAM_STEP)
        d_ref[...] = -ADAM_LR * (m_hat / (jnp.sqrt(v_hat) + ADAM_EPS) + ADAM_WD * w_ref[...])
        nm_ref[...] = m
        nv_ref[...] = v

    blk = pl.BlockSpec((tr, cols), lambda i: (i, 0))
    return pl.pallas_call(
        body, name=name, grid=(r // tr,),
        in_specs=[blk] * 4, out_specs=[blk] * 3,
        out_shape=[jax.ShapeDtypeStruct((r, cols), F32)] * 3,
        compiler_params=_params("arbitrary"),
    )(w, g, m, v)


def _place():
    x, y, c = lax.axis_index("x"), lax.axis_index("y"), lax.axis_index("c")
    other_chips = [(1 - x, y), (x, 1 - y), (1 - x, 1 - y)]
    return x, y, c, other_chips


def _chip(xy):
    return 2 * xy[0] + xy[1]


def _remote(src, dst, send_sem, recv_sem, to):
    return pltpu.make_async_remote_copy(src_ref=src, dst_ref=dst, send_sem=send_sem, recv_sem=recv_sem,
                                        device_id=to, device_id_type=MESH)


def _gather_ici(ctx, k, j, start):
    (x, y, c, chips), b, send, recv = ctx
    if j < 2:
        chip, to = (_chip((x, y)) if start else _chip(chips[j])), (*chips[j], c)
    else:
        chip = 2 * (x ^ c) + (y ^ (1 - c)) if start else _chip(chips[2])
        to = (x ^ (1 - c), y ^ c, c)
    blk = b[k].at[chip, c]
    return _remote(blk, blk, send.at[6 * k + j], recv.at[6 * k + j], to)


def _gather_d2d(ctx, k, j, start):
    (x, y, c, chips), b, send, recv = ctx
    blk = b[k].at[_chip(chips[j]), c if start else 1 - c]
    return _remote(blk, blk, send.at[6 * k + 3 + j], recv.at[6 * k + 3 + j], (x, y, 1 - c))


def _gather_small(ctx, n, j, start):
    (x, y, c, chips), b, send, recv = ctx
    blk = b[n].at[_chip((x, y)) if start else _chip(chips[j])]
    return _remote(blk, blk, send.at[6 * n + j], recv.at[6 * n + j], (*chips[j], c))


def _gather_neighbours_landed(ctx, k):
    for j in range(2):
        _gather_ici(ctx, k, j, False).wait_recv()
    _gather_ici(ctx, k, 2, True).start()
    for j in range(2):
        _gather_d2d(ctx, k, j, True).start()


def _gather_diagonal_landed(ctx, k):
    _gather_ici(ctx, k, 2, False).wait_recv()
    _gather_d2d(ctx, k, 2, True).start()


def _gather_comm(bufs, relay_at, forward_at):
    n = len(bufs)

    def start(srcs, b, send, recv):
        for k in range(n):
            for j in range(2):
                _gather_ici((_place(), b, send, recv), k, j, True).start()

    def relay(srcs, b, send, recv):
        for k in range(n):
            _gather_neighbours_landed((_place(), b, send, recv), k)

    def forward(srcs, b, send, recv):
        for k in range(n):
            _gather_diagonal_landed((_place(), b, send, recv), k)

    def finish(srcs, b, send, recv):
        ctx = (_place(), b, send, recv)
        for k in range(n):
            for j in range(3):
                _gather_d2d(ctx, k, j, False).wait_recv()
                _gather_ici(ctx, k, j, True).wait_send()
                _gather_d2d(ctx, k, j, True).wait_send()

    return _Comm([], bufs, 6 * n, [(0, start), (relay_at, relay), (forward_at, forward)], finish)


def _rmsnorm(x, gain, name):
    s, d = x.shape
    tm = _row_tile(s, 512)

    def body(x_ref, g_ref, h_ref):
        xv = x_ref[...]
        r = lax.rsqrt(_mean_last(xv * xv) + EPS)
        h_ref[...] = (xv * r * g_ref[...]).astype(BF16)

    return pl.pallas_call(
        body, name=name, grid=(s // tm,),
        in_specs=[pl.BlockSpec((tm, d), lambda i: (i, 0)), pl.BlockSpec((1, d), lambda i: (0, 0))],
        out_specs=pl.BlockSpec((tm, d), lambda i: (i, 0)),
        out_shape=jax.ShapeDtypeStruct((s, d), BF16),
        compiler_params=_params("arbitrary"),
    )(x, gain)


def _gathered_in_proj(h, bufs, small, order, name):
    s, d = h.shape
    n_sh, _, r2, ns = bufs[0].shape
    assert d == 2 * r2
    n = len(bufs)
    tm = _row_tile(s, 512)
    n_i = s // tm
    hook_i = max(n_i - 2, 0)
    n_sems = 6 * n + 3

    def body(order_ref, h_ref, *rest):
        p_ref = rest[n + 1]
        b = rest[n + 2:2 * n + 3]
        w_vmem, w_sems, send, recv = rest[2 * n + 3:]
        j, i = pl.program_id(0), pl.program_id(1)
        ctx = (_place(), b, send, recv)

        def fetch(q):
            return pltpu.make_async_copy(b[0].at[order_ref[q]], w_vmem.at[q % 2], w_sems.at[q % 2])

        @pl.when((j == 0) & (i == 0))
        def _():
            for k in range(n):
                for peer in range(2):
                    _gather_ici(ctx, k, peer, True).start()
            for peer in range(3):
                _gather_small(ctx, n, peer, True).start()
            fetch(0).start()
            fetch(0).wait()

        for q in range(1, n_sh):
            @pl.when((j == q - 1) & (i == hook_i))
            def _(q=q):
                if q == 1:
                    _gather_neighbours_landed(ctx, 0)
                if q == 2:
                    for k in range(1, n):
                        _gather_neighbours_landed(ctx, k)
                if q == 3:
                    for k in range(n):
                        _gather_diagonal_landed(ctx, k)
                _gather_d2d(ctx, 0, q - 1, False).wait_recv()
                fetch(q).start()

            @pl.when((j == q) & (i == 0))
            def _(q=q):
                fetch(q).wait()

        wv = w_vmem.at[j % 2]
        p_ref[...] = (jnp.dot(h_ref[:, 0:r2], wv[0], preferred_element_type=F32)
                      + jnp.dot(h_ref[:, r2:d], wv[1], preferred_element_type=F32))

        @pl.when((j == n_sh - 1) & (i == n_i - 1))
        def _():
            for peer in range(3):
                _gather_small(ctx, n, peer, False).wait_recv()
                _gather_small(ctx, n, peer, True).wait_send()
            for k in range(n):
                for peer in range(3):
                    if k > 0:
                        _gather_d2d(ctx, k, peer, False).wait_recv()
                    _gather_ici(ctx, k, peer, True).wait_send()
                    _gather_d2d(ctx, k, peer, True).wait_send()

    all_bufs = list(bufs) + [small]
    out = pl.pallas_call(
        body, name=name,
        grid_spec=pltpu.PrefetchScalarGridSpec(
            num_scalar_prefetch=1, grid=(n_sh, n_i),
            in_specs=[pl.BlockSpec((tm, d), lambda j, i, o: (i, 0))] + [ANY] * (n + 1),
            out_specs=[pl.BlockSpec((tm, ns), lambda j, i, o: (i, o[j]))] + [ANY] * (n + 1),
            scratch_shapes=[pltpu.VMEM((2, 2, r2, ns), BF16), pltpu.SemaphoreType.DMA((2,)),
                            pltpu.SemaphoreType.DMA((n_sems,)), pltpu.SemaphoreType.DMA((n_sems,))]),
        out_shape=[jax.ShapeDtypeStruct((s, n_sh * ns), F32)]
        + [jax.ShapeDtypeStruct(a.shape, a.dtype) for a in all_bufs],
        input_output_aliases={2 + t: 1 + t for t in range(n + 1)},
        compiler_params=_params("arbitrary", "arbitrary"),
    )(order, h, *all_bufs)
    return out[0], list(out[1:])


def _exchange_comm(grads):
    n = len(grads)
    landing = [lax.empty((N_SHARDS,) + a.shape[2:], a.dtype) for a in grads]

    def copies(srcs, b, send, recv):
        x, y, c, _ = _place()
        return [_remote(srcs[k].at[s, 1 - c], b[k].at[s], send.at[N_SHARDS * k + s], recv.at[N_SHARDS * k + s],
                        (x, y, 1 - c)) for k in range(n) for s in range(N_SHARDS)]

    def start(srcs, b, send, recv):
        for cp in copies(srcs, b, send, recv):
            cp.start()

    def finish(srcs, b, send, recv):
        for cp in copies(srcs, b, send, recv):
            cp.wait()

    return _Comm(grads, landing, N_SHARDS * n, [(0, start)], finish)


def _scatter_comm(chip_sums, landing):
    n = len(chip_sums)

    def big(srcs, b, send, recv, k, j, start):
        x, y, c, chips = _place()
        dst = b[k].at[_chip((x, y)) if start else _chip(chips[j])]
        return _remote(srcs[k].at[_chip(chips[j])], dst, send.at[3 * k + j], recv.at[3 * k + j], (*chips[j], c))

    def start(srcs, b, send, recv):
        for k in range(n):
            for j in range(3):
                big(srcs, b, send, recv, k, j, True).start()

    def finish(srcs, b, send, recv):
        for k in range(n):
            for j in range(3):
                big(srcs, b, send, recv, k, j, False).wait_recv()
                big(srcs, b, send, recv, k, j, True).wait_send()

    return _Comm(chip_sums, landing, 3 * n, [(0, start)], finish)


def _join_comm(halves, small):
    n = len(halves)
    flips = [(fx, fy, fc) for fx in (0, 1) for fy in (0, 1) for fc in (0, 1)][1:]

    def half(b, send, recv, k, start):
        x, y, c, _ = _place()
        return _remote(b[k].at[c], b[k].at[c if start else 1 - c], send.at[k], recv.at[k], (x, y, 1 - c))

    def small_copy(b, send, recv, q, start):
        x, y, c, _ = _place()
        px, py, pc = x ^ flips[q][0], y ^ flips[q][1], c ^ flips[q][2]
        blk = b[n].at[4 * x + 2 * y + c if start else 4 * px + 2 * py + pc]
        return _remote(blk, blk, send.at[n + q], recv.at[n + q], (px, py, pc))

    def start(srcs, b, send, recv):
        for q in range(len(flips)):
            small_copy(b, send, recv, q, True).start()
        for k in range(n):
            half(b, send, recv, k, True).start()

    def finish(srcs, b, send, recv):
        for q in range(len(flips)):
            small_copy(b, send, recv, q, False).wait()
        for k in range(n):
            half(b, send, recv, k, False).wait()

    return _Comm([], list(halves) + [small], n + len(flips), [(0, start)], finish)


def _flat_rows(parts):
    flat = jnp.concatenate([p.reshape(-1) for p in parts])
    assert flat.shape[0] % LANES == 0
    return flat.reshape(-1, LANES)


def _unflatten(flat, shapes):
    out, at = [], 0
    for sh in shapes:
        size = 1
        for dim in sh:
            size *= dim
        out.append(flat[at:at + size].reshape(sh))
        at += size
    assert at == flat.shape[0], (at, flat.shape)
    return out


def _col_shards_to_full(a, rows):
    q = a.shape[1] // rows
    return a.reshape(N_SHARDS, rows, q).transpose(1, 0, 2).reshape(rows, N_SHARDS * q)


def _my_col_shard(full, chip):
    rows, cols = full.shape
    q = cols // N_SHARDS
    return lax.dynamic_index_in_dim(full.reshape(rows, N_SHARDS, q), chip, axis=1, keepdims=False)


def kernel(x, e_norm_pre, e_norm_post, e_w_in, e_a_conv, e_b_conv, e_b_conv_bias, e_b_ln_g, e_b_ln_b, e_w_out, o_norm_pre, o_norm_post, o_w_in, o_c_w, o_c_b, o_c_scale, o_w_out, loss_target, m_e_norm_pre, m_e_norm_post, m_e_w_in, m_e_a_conv, m_e_b_conv, m_e_b_conv_bias, m_e_b_ln_g, m_e_b_ln_b, m_e_w_out, m_o_norm_pre, m_o_norm_post, m_o_w_in, m_o_c_w, m_o_c_b, m_o_c_scale, m_o_w_out, v_e_norm_pre, v_e_norm_post, v_e_w_in, v_e_a_conv, v_e_b_conv, v_e_b_conv_bias, v_e_b_ln_g, v_e_b_ln_b, v_e_w_out, v_o_norm_pre, v_o_norm_post, v_o_w_in, v_o_c_w, v_o_c_b, v_o_c_scale, v_o_w_out):
    _, s, d = x.shape
    w = d // 2
    c = d
    gc = c // N_GROUPS
    wq, cq, gq = w // N_SHARDS, c // N_SHARDS, gc // N_SHARDS
    chip = 2 * lax.axis_index("x") + lax.axis_index("y")
    core = lax.axis_index("c")
    x2 = x.reshape(s, d)
    target = loss_target.reshape(s, d)

    big_w = [e_w_in[0], e_w_out[0], o_w_in[0], o_c_w[0].reshape(N_GROUPS * gq, gc), o_w_out[0]]
    big_m = [m_e_w_in[0], m_e_w_out[0], m_o_w_in[0], m_o_c_w[0].reshape(N_GROUPS * gq, gc), m_o_w_out[0]]
    big_v = [v_e_w_in[0], v_e_w_out[0], v_o_w_in[0], v_o_c_w[0].reshape(N_GROUPS * gq, gc), v_o_w_out[0]]
    coords = jnp.stack([chip, core]).astype(jnp.int32)
    slots = [_cast_into_slot(a, coords, "cast_%d" % k) for k, a in enumerate(big_w)]
    sharded_small = _flat_rows([e_a_conv[0], e_b_conv[0], o_norm_pre, o_norm_post, o_c_scale, o_c_b[0]])
    small_slots = lax.dynamic_update_index_in_dim(jnp.zeros((N_SHARDS,) + sharded_small.shape, F32), sharded_small,
                                                  chip, 0)
    xi, yi = lax.axis_index("x"), lax.axis_index("y")
    order = jnp.stack([chip, 2 * (1 - xi) + yi, 2 * xi + (1 - yi), 2 * (1 - xi) + (1 - yi)]).astype(jnp.int32)
    h0 = _rmsnorm(x2, e_norm_pre, "e_pre_norm")
    p0, (e_w_in_g, e_w_out_g, small_g4) = _gathered_in_proj(h0, slots[:2], small_slots, order, "e_in_proj")
    e_w_in_sm = e_w_in_g.reshape((N_SHARDS,) + big_w[0].shape)
    e_w_out_f = e_w_out_g.reshape(w + w, d)
    sm = small_g4.reshape(N_SHARDS, -1)
    at = [0]

    def take(rows, q):
        blk = sm[:, at[0]:at[0] + rows * q]
        at[0] += rows * q
        return _col_shards_to_full(blk, rows)

    a_conv_f = take(CONV_A, wq)
    b_conv_f = take(CONV_B, wq)
    o_pre_f = take(1, cq)
    o_post_f = take(1, cq)
    cs_f = take(1, cq)
    cb_f = take(N_GROUPS, gq).reshape(1, c)

    mixer_steps = s // _row_tile(s, 128)
    (u0, x1, y0), odd_g = _even_mixer_fwd(p0, e_w_out_f, x2, e_norm_post, a_conv_f, b_conv_f, e_b_conv_bias, e_b_ln_g,
                                          e_b_ln_b, "e_mixer_out_proj",
                                          comm=_gather_comm(slots[2:], mixer_steps // 2, (25 * mixer_steps) // 32))
    o_w_in_sm = odd_g[0].reshape((N_SHARDS,) + big_w[2].shape)
    cw_f = odd_g[1].reshape(N_SHARDS, N_GROUPS, gq, gc).transpose(1, 0, 2, 3).reshape(N_GROUPS, gc, gc)
    o_w_out_f = odd_g[2].reshape(c, d)
    p1, h1 = _norm_matmul(x1, o_pre_f, o_w_in_sm, "o_in_proj")
    u1 = _odd_mixer_fwd(p1, cw_f, cb_f, cs_f, "o_mixer_fwd")
    d_y1, d_x2, d_o_post, loss_part = _matmul_post_loss(u1, o_w_out_f, x1, o_post_f, target, "o_out_proj_loss")

    def as_pieces(g, k):
        return g.reshape(N_SHARDS, 2, big_w[k].shape[0] // 2, big_w[k].shape[1])

    def chip_sums(ks, pieces, from_sibling):
        both = [_chip_sum(g, o, coords, "chip_sum_%d" % k) for k, g, o in zip(ks, pieces, from_sibling)]
        return [b[0] for b in both], [b[1] for b in both]

    g_o_w_out = _matmul_tn(u1, d_y1, 1, "o_w_out_grad")
    d_u1, _ = _matmul_nt(d_y1, o_w_out_f[None], "o_out_proj_bwd")
    d_p1, d_cw, d_cb, d_cs = _odd_mixer_bwd(p1, d_u1, cw_f, cb_f, cs_f, "o_mixer_bwd")
    g_o_w_in = _matmul_tn(h1, d_p1, N_SHARDS, "o_w_in_grad")
    g_cw = d_cw.reshape(N_GROUPS, N_SHARDS, gq, gc).transpose(1, 0, 2, 3).astype(BF16)
    pieces_o = [as_pieces(g_o_w_in, 2), as_pieces(g_cw, 3), as_pieces(g_o_w_out, 4)]
    d_h1, sibling_o = _matmul_nt(d_p1, o_w_in_sm, "o_in_proj_bwd", comm=_exchange_comm(pieces_o))
    d_x1, d_o_pre, d_y0, d_e_post = _norm_bwd(d_h1, x1, o_pre_f, d_x2, "o_pre_norm_bwd", post=(y0, e_norm_post))

    pieces_e = [as_pieces(_matmul_tn(u0, d_y0, 1, "e_w_out_grad"), 1)]
    d_u0, sibling_e = _matmul_nt(d_y0, e_w_out_f[None], "e_out_proj_bwd", comm=_exchange_comm(pieces_e))
    sums_a, landing_a = chip_sums([1, 2, 3, 4], pieces_e + pieces_o, sibling_e + sibling_o)
    (d_p0, d_a_conv, d_b_conv, d_bias, d_ln_g, d_ln_b), landed_a = _even_mixer_bwd(
        p0, d_u0, a_conv_f, b_conv_f, e_b_conv_bias, e_b_ln_g, e_b_ln_b, "e_mixer_bwd",
        comm=_scatter_comm(sums_a, landing_a))
    pieces_b = [as_pieces(_matmul_tn(h0, d_p0, N_SHARDS, "e_w_in_grad"), 0)]
    sums_b, landing_b = chip_sums([0], pieces_b, _comm_only(_exchange_comm(pieces_b), "exchange_core_halves"))
    d_h0, landed_b = _matmul_nt(d_p0, e_w_in_sm, "e_in_proj_bwd", comm=_scatter_comm(sums_b, landing_b))
    grad_x, d_e_pre = _norm_bwd(d_h0, x2, e_norm_pre, d_x1, "e_pre_norm_bwd")

    landed = landed_b + landed_a
    reduced = [_shard_sum(sc, coords, "shard_sum_%d" % k) for k, sc in enumerate(landed)]
    small_parts = _flat_rows([loss_part[0], d_e_pre, d_e_post, d_bias, d_ln_g, d_ln_b, d_a_conv, d_b_conv,
                              d_o_pre, d_o_post, d_cs, d_cb])
    small_rows = lax.dynamic_update_index_in_dim(jnp.zeros((N_DEVICES,) + small_parts.shape, F32), small_parts,
                                                 2 * chip + core, 0)
    joined = _comm_only(_join_comm(reduced, small_rows), "join_core_halves")
    big_g = [j.reshape(a.shape) for j, a in zip(joined[:5], big_w)]
    small_sum = _sum_small(joined[5], "small_sum").reshape(-1)
    (loss_row, g_e_pre, g_e_post, g_bias, g_ln_g, g_ln_b, g_a_conv_f, g_b_conv_f, g_o_pre_f, g_o_post_f, g_cs_f,
     g_cb_f) = _unflatten(small_sum, [(LANES,), (1, d), (1, d), (1, w), (1, w), (1, w), (CONV_A, w), (CONV_B, w),
                                      (1, c), (1, c), (1, c), (1, c)])
    loss = loss_row[0]
    g_a_conv = _my_col_shard(g_a_conv_f, chip)
    g_b_conv = _my_col_shard(g_b_conv_f, chip)
    g_o_pre = _my_col_shard(g_o_pre_f, chip)
    g_o_post = _my_col_shard(g_o_post_f, chip)
    g_cs = _my_col_shard(g_cs_f, chip)
    g_cb = _my_col_shard(g_cb_f.reshape(N_GROUPS, gc), chip)

    big_upd = [_adamw(wt, g, m, v, "adamw_%d" % k) for k, (wt, g, m, v) in enumerate(zip(big_w, big_g, big_m, big_v))]
    small_w = [e_norm_pre, e_norm_post, e_b_conv_bias, e_b_ln_g, e_b_ln_b, e_a_conv[0], e_b_conv[0],
               o_norm_pre, o_norm_post, o_c_b[0], o_c_scale]
    small_m = [m_e_norm_pre, m_e_norm_post, m_e_b_conv_bias, m_e_b_ln_g, m_e_b_ln_b, m_e_a_conv[0], m_e_b_conv[0],
               m_o_norm_pre, m_o_norm_post, m_o_c_b[0], m_o_c_scale]
    small_v = [v_e_norm_pre, v_e_norm_post, v_e_b_conv_bias, v_e_b_ln_g, v_e_b_ln_b, v_e_a_conv[0], v_e_b_conv[0],
               v_o_norm_pre, v_o_norm_post, v_o_c_b[0], v_o_c_scale]
    small_g = [g_e_pre, g_e_post, g_bias, g_ln_g, g_ln_b, g_a_conv, g_b_conv, g_o_pre, g_o_post, g_cb, g_cs]
    small_shapes = [a.shape for a in small_w]
    small_upd = _adamw(_flat_rows(small_w), _flat_rows(small_g), _flat_rows(small_m), _flat_rows(small_v),
                       "adamw_small")
    small_delta, small_new_m, small_new_v = [_unflatten(u.reshape(-1), small_shapes) for u in small_upd]

    def ordered(small, big):
        (n_pre, n_post, bias, ln_g, ln_b, a_conv, b_conv, o_pre, o_post, cb, cs) = small
        (w_in, w_out, ow_in, cw, ow_out) = big
        return [n_pre, n_post, w_in[None], a_conv[None], b_conv[None], bias, ln_g, ln_b, w_out[None], o_pre, o_post,
                ow_in[None], cw.reshape(1, N_GROUPS, gq, gc), cb[None], cs, ow_out[None]]

    grads = ordered(small_g, big_g)
    deltas = ordered(small_delta, [u[0] for u in big_upd])
    new_m = ordered(small_new_m, [u[1] for u in big_upd])
    new_v = ordered(small_new_v, [u[2] for u in big_upd])
    return (loss, grad_x.reshape(1, s, d), *grads, *deltas, *new_m, *new_v)
```

```python
import functools

import jax
import jax.numpy as jnp
from jax import lax
from jax.experimental import pallas as pl
from jax.experimental.pallas import tpu as pltpu

F32 = jnp.float32
BF16 = jnp.bfloat16
MESH = pl.DeviceIdType.MESH

EPS = 1e-6
CONV_A = 3
CONV_B = 31
POOL_WINDOWS = (2, 4, 8, 16)
N_GROUPS = len(POOL_WINDOWS)
N_SHARDS = 4
N_DEVICES = 8
ADAM_LR = 0.001
ADAM_B1 = 0.9
ADAM_B2 = 0.999
ADAM_EPS = 1e-08
ADAM_WD = 0.01
ADAM_STEP = 10

LANES = 128
SUBLANES_BF16 = 16
HALO_A = 8
HALO_B = 32
HALO_P = 16
SHIFTS = 8
ROW_BLOCK = 32
LANE_BLOCK = 256
LANE_PASS = 256
VMEM_LIMIT = 56 * 1024 * 1024
TN_ACC_BYTES = 8 * 1024 * 1024
EVEN_BWD_ROWS = 128


def _row_tile(n, pref):
    t = max(min(n, pref) // SUBLANES_BF16, 1) * SUBLANES_BF16
    while t > SUBLANES_BF16 and (n % t or t % SUBLANES_BF16):
        t -= SUBLANES_BF16
    assert n % t == 0, (n, pref)
    return t


def _col_chunk(n, pref):
    t = (min(n, pref) // LANES) * LANES
    while t > LANES and n % t:
        t -= LANES
    assert t >= LANES and n % t == 0, (n, pref)
    return t


def _params(*sem):
    return pltpu.CompilerParams(dimension_semantics=tuple(sem) if sem else None, vmem_limit_bytes=VMEM_LIMIT)


ANY = pl.BlockSpec(memory_space=pl.ANY)


class _Comm:
    def __init__(self, srcs, bufs, n_sems, phases, finish):
        self.srcs, self.bufs, self.n_sems, self.phases, self.finish = list(srcs), list(bufs), n_sems, phases, finish


def _call(body, *, name, grid, in_specs, out_specs, out_shape, args, scratch_shapes=(), comm=None):
    params = _params(*(("arbitrary",) * len(grid)))
    if comm is None:
        out = pl.pallas_call(body, name=name, grid=grid, in_specs=in_specs, out_specs=out_specs, out_shape=out_shape,
                             scratch_shapes=scratch_shapes, compiler_params=params)(*args)
        return list(out), []
    n_in, n_out, n_scr = len(in_specs), len(out_specs), len(scratch_shapes)
    ns, nb = len(comm.srcs), len(comm.bufs)
    total = 1
    for size in grid:
        total *= size

    def fused(*refs):
        ins, srcs = refs[:n_in], refs[n_in:n_in + ns]
        at = n_in + ns + nb
        outs, bufs = refs[at:at + n_out], refs[at + n_out:at + n_out + nb]
        scratch = refs[at + n_out + nb:at + n_out + nb + n_scr]
        send_sems, recv_sems = refs[-2:]
        step = 0
        for axis, size in enumerate(grid):
            step = step * size + pl.program_id(axis)
        for when, fn in comm.phases:
            pl.when(step == when)(functools.partial(fn, srcs, bufs, send_sems, recv_sems))
        body(*ins, *outs, *scratch)
        pl.when(step == total - 1)(functools.partial(comm.finish, srcs, bufs, send_sems, recv_sems))

    out = pl.pallas_call(
        fused, name=name, grid=grid,
        in_specs=list(in_specs) + [ANY] * (ns + nb), out_specs=list(out_specs) + [ANY] * nb,
        out_shape=list(out_shape) + [jax.ShapeDtypeStruct(b.shape, b.dtype) for b in comm.bufs],
        input_output_aliases={n_in + ns + i: n_out + i for i in range(nb)},
        scratch_shapes=list(scratch_shapes) + [pltpu.SemaphoreType.DMA((comm.n_sems,))] * 2,
        compiler_params=params,
    )(*args, *comm.srcs, *comm.bufs)
    return list(out[:n_out]), list(out[n_out:])


def _comm_only(comm, name):
    ns, nb = len(comm.srcs), len(comm.bufs)

    def body(*refs):
        srcs, bufs = refs[:ns], refs[ns + nb:ns + 2 * nb]
        send_sems, recv_sems = refs[-2:]
        for _, fn in comm.phases:
            fn(srcs, bufs, send_sems, recv_sems)
        comm.finish(srcs, bufs, send_sems, recv_sems)

    return pl.pallas_call(
        body, name=name, in_specs=[ANY] * (ns + nb), out_specs=[ANY] * nb,
        out_shape=[jax.ShapeDtypeStruct(b.shape, b.dtype) for b in comm.bufs],
        input_output_aliases={ns + i: i for i in range(nb)},
        scratch_shapes=[pltpu.SemaphoreType.DMA((comm.n_sems,))] * 2,
    )(*comm.srcs, *comm.bufs)


def _sigmoid(v):
    return jax.nn.sigmoid(v)


def _dsilu(v, s):
    return s * (1.0 + v * (1.0 - s))


def _mean_last(v):
    return jnp.mean(v, axis=-1, keepdims=True)


def _sum_rows(v):
    return jnp.sum(v, axis=0, keepdims=True)


def _norm_matmul(x, gain, w_sm, name):
    s, d = x.shape
    n_sh, _, ns = w_sm.shape
    tm = _row_tile(s, 1024)

    def body(x_ref, g_ref, w_ref, p_ref, h_ref):
        @pl.when(pl.program_id(1) == 0)
        def _():
            xv = x_ref[...]
            r = lax.rsqrt(_mean_last(xv * xv) + EPS)
            h_ref[...] = (xv * r * g_ref[...]).astype(BF16)

        p_ref[...] = jnp.dot(h_ref[...], w_ref[0], preferred_element_type=F32)

    return _call(
        body, name=name, grid=(s // tm, n_sh),
        in_specs=[pl.BlockSpec((tm, d), lambda i, j: (i, 0)),
                  pl.BlockSpec((1, d), lambda i, j: (0, 0)),
                  pl.BlockSpec((1, d, ns), lambda i, j: (j, 0, 0))],
        out_specs=[pl.BlockSpec((tm, ns), lambda i, j: (i, j)),
                   pl.BlockSpec((tm, d), lambda i, j: (i, 0))],
        out_shape=[jax.ShapeDtypeStruct((s, n_sh * ns), F32), jax.ShapeDtypeStruct((s, d), BF16)],
        args=(x, gain, w_sm))[0]


def _matmul_post_loss(u, w, x_res, gain, target, name):
    s, k = u.shape
    d = w.shape[1]
    tm = _row_tile(s, 256)

    def body(u_ref, w_ref, x_ref, g_ref, t_ref, dy_ref, dout_ref, dg_ref, loss_ref):
        @pl.when(pl.program_id(0) == 0)
        def _():
            dg_ref[...] = jnp.zeros_like(dg_ref)
            loss_ref[...] = jnp.zeros_like(loss_ref)

        y = jnp.dot(u_ref[...], w_ref[...], preferred_element_type=F32)
        r = lax.rsqrt(_mean_last(y * y) + EPS)
        n = y * r
        g = g_ref[...]
        err = x_ref[...] + n * g - t_ref[...]
        loss_ref[...] += 0.5 * jnp.sum(_mean_last(err * err))
        dout = err * (1.0 / d)
        dout_ref[...] = dout
        dg_ref[...] += _sum_rows(dout * n)
        dn = dout * g
        dy_ref[...] = (r * (dn - n * _mean_last(dn * n))).astype(BF16)

    return pl.pallas_call(
        body, name=name, grid=(s // tm,),
        in_specs=[pl.BlockSpec((tm, k), lambda i: (i, 0)),
                  pl.BlockSpec((k, d), lambda i: (0, 0)),
                  pl.BlockSpec((tm, d), lambda i: (i, 0)),
                  pl.BlockSpec((1, d), lambda i: (0, 0)),
                  pl.BlockSpec((tm, d), lambda i: (i, 0))],
        out_specs=[pl.BlockSpec((tm, d), lambda i: (i, 0)),
                   pl.BlockSpec((tm, d), lambda i: (i, 0)),
                   pl.BlockSpec((1, d), lambda i: (0, 0)),
                   pl.BlockSpec((8, LANES), lambda i: (0, 0))],
        out_shape=[jax.ShapeDtypeStruct((s, d), BF16), jax.ShapeDtypeStruct((s, d), F32),
                   jax.ShapeDtypeStruct((1, d), F32), jax.ShapeDtypeStruct((8, LANES), F32)],
        compiler_params=_params("arbitrary"),
    )(u, w, x_res, gain, target)


def _matmul_nt(a, w_sm, name, comm=None):
    s, ncols = a.shape
    n_sh, r, ns = w_sm.shape
    assert ncols == n_sh * ns
    tm = _row_tile(s, 1024)
    nc = _col_chunk(ns, 1792)
    per = ns // nc
    steps = n_sh * per

    def body(a_ref, w_ref, o_ref):
        part = lax.dot_general(a_ref[...], w_ref[0], (((1,), (1,)), ((), ())), preferred_element_type=F32)

        @pl.when(pl.program_id(1) == 0)
        def _():
            o_ref[...] = part

        @pl.when(pl.program_id(1) > 0)
        def _():
            o_ref[...] += part

    out, bufs = _call(
        body, name=name, grid=(s // tm, steps),
        in_specs=[pl.BlockSpec((tm, nc), lambda i, j: (i, j)),
                  pl.BlockSpec((1, r, nc), lambda i, j: (j // per, 0, j % per))],
        out_specs=[pl.BlockSpec((tm, r), lambda i, j: (i, 0))],
        out_shape=[jax.ShapeDtypeStruct((s, r), F32)],
        args=(a, w_sm), comm=comm)
    return out[0], bufs


def _matmul_tn(a, b, n_sh, name):
    s, k = a.shape
    n = b.shape[1]
    ns = n // n_sh
    tk = _col_chunk(k, TN_ACC_BYTES // (4 * ns))
    ts = _row_tile(s, 2048)
    n_s = s // ts

    def body(a_ref, b_ref, o_ref, acc_ref):
        part = lax.dot_general(a_ref[...], b_ref[...], (((0,), (0,)), ((), ())), preferred_element_type=F32)

        @pl.when(pl.program_id(2) == 0)
        def _():
            acc_ref[...] = part

        @pl.when(pl.program_id(2) > 0)
        def _():
            acc_ref[...] += part

        @pl.when(pl.program_id(2) == n_s - 1)
        def _():
            o_ref[0] = acc_ref[...].astype(BF16)

    return pl.pallas_call(
        body, name=name, grid=(n_sh, k // tk, n_s),
        in_specs=[pl.BlockSpec((ts, tk), lambda j, i, t: (t, i)),
                  pl.BlockSpec((ts, ns), lambda j, i, t: (t, j))],
        out_specs=pl.BlockSpec((1, tk, ns), lambda j, i, t: (j, i, 0)),
        out_shape=jax.ShapeDtypeStruct((n_sh, k, ns), BF16),
        scratch_shapes=[pltpu.VMEM((tk, ns), F32)],
        compiler_params=_params("arbitrary", "arbitrary", "arbitrary"),
    )(a, b)


def _norm_bwd(dh, x, gain, dres, name, post=None):
    s, d = x.shape
    tm = _row_tile(s, 256)
    with_post = post is not None

    def rms_bwd(dout, v, g):
        r = lax.rsqrt(_mean_last(v * v) + EPS)
        n = v * r
        dn = dout * g
        return r * (dn - n * _mean_last(dn * n)), _sum_rows(dout * n)

    def body(*refs):
        if with_post:
            dh_ref, x_ref, g_ref, dres_ref, y_ref, gp_ref, dx_ref, dg_ref, dy_ref, dgp_ref = refs
        else:
            dh_ref, x_ref, g_ref, dres_ref, dx_ref, dg_ref = refs

        @pl.when(pl.program_id(0) == 0)
        def _():
            dg_ref[...] = jnp.zeros_like(dg_ref)
            if with_post:
                dgp_ref[...] = jnp.zeros_like(dgp_ref)

        dv, dg = rms_bwd(dh_ref[...], x_ref[...], g_ref[...])
        dx = dres_ref[...] + dv
        dx_ref[...] = dx
        dg_ref[...] += dg
        if with_post:
            dy, dgp = rms_bwd(dx, y_ref[...], gp_ref[...])
            dy_ref[...] = dy.astype(BF16)
            dgp_ref[...] += dgp

    row = pl.BlockSpec((tm, d), lambda i: (i, 0))
    vec = pl.BlockSpec((1, d), lambda i: (0, 0))
    in_specs = [row, row, vec, row]
    out_specs = [row, vec]
    out_shape = [jax.ShapeDtypeStruct((s, d), F32), jax.ShapeDtypeStruct((1, d), F32)]
    args = [dh, x, gain, dres]
    if with_post:
        in_specs += [row, vec]
        out_specs += [row, vec]
        out_shape += [jax.ShapeDtypeStruct((s, d), BF16), jax.ShapeDtypeStruct((1, d), F32)]
        args += list(post)
    return _call(body, name=name, grid=(s // tm,), in_specs=in_specs, out_specs=out_specs, out_shape=out_shape,
                 args=args)[0]


def _fill_shifted_down(sh, rows):
    for b in range(1, SHIFTS):
        sh[b, SHIFTS:rows, :] = sh[0, SHIFTS - b:rows - b, :]


def _fill_shifted_up(sh, rows):
    for b in range(1, SHIFTS):
        sh[b, 0:rows - SHIFTS, :] = sh[0, b:rows - SHIFTS + b, :]


def _for_blocks(ts, w, fn):
    lb = min(LANE_BLOCK, w)
    for l0 in range(0, w, lb):
        def rows(rb, carry, l0=l0):
            fn(pl.multiple_of(rb * ROW_BLOCK, ROW_BLOCK), slice(l0, l0 + lb))
            return carry

        lax.fori_loop(0, ts // ROW_BLOCK, rows, 0)


TAP_SPAN = SHIFTS * ((CONV_B - 1) // SHIFTS)
WINDOW = ROW_BLOCK + TAP_SPAN


def _taps_of(b):
    return [(a, SHIFTS * a + b) for a in range((CONV_B - 1 - b) // SHIFTS + 1)]


def _conv31(sh, base, step, wt_ref, bias_ref, out_ref, ts, w):
    low = min(0, step * (TAP_SPAN // SHIFTS))

    def block(r0, lanes):
        acc = [jnp.zeros((SHIFTS, lanes.stop - lanes.start), F32) for _ in range(ROW_BLOCK // SHIFTS)]
        for b in range(SHIFTS):
            window = sh[b, pl.ds(pl.multiple_of(r0 + (base + low), SHIFTS), WINDOW), lanes]
            for a, j in (_taps_of(b) if step > 0 else reversed(_taps_of(b))):
                at = step * a - low
                wt = wt_ref[CONV_B - 1 - j, :, lanes]
                acc = [v + wt * window[at + SHIFTS * r:at + SHIFTS * (r + 1), :] for r, v in enumerate(acc)]
        for r, v in enumerate(acc):
            if bias_ref is not None:
                v = v + bias_ref[:, lanes]
            out_ref[pl.ds(pl.multiple_of(r0 + SHIFTS * r, SHIFTS), SHIFTS), lanes] = v

    _for_blocks(ts, w, block)


def _conv31_weight_grad(d_sh, x_sh, wacc, ts, w):
    def block(r0, lanes):
        d = d_sh[0, pl.ds(r0, ROW_BLOCK), lanes]
        for b in range(SHIFTS):
            window = x_sh[b, pl.ds(pl.multiple_of(r0 + (HALO_B - TAP_SPAN), SHIFTS), WINDOW), lanes]
            for a, j in _taps_of(b):
                at = TAP_SPAN - SHIFTS * a
                prod = d * window[at:at + ROW_BLOCK, :]
                part = prod[0:SHIFTS, :]
                for q in range(1, ROW_BLOCK // SHIFTS):
                    part = part + prod[q * SHIFTS:(q + 1) * SHIFTS, :]
                wacc[CONV_B - 1 - j, :, lanes] += part

    _for_blocks(ts, w, block)


def _even_forward_tile(p_ref, halo_ref, first, a_conv_ref, b_conv_ref, bias_ref, lng_ref, lnb_ref, qbuf, ysh, y1buf,
                       wb, w, ts):
    @pl.when(pl.program_id(0) == 0)
    def _():
        for k in range(CONV_B):
            wb[k] = jnp.broadcast_to(b_conv_ref[k:k + 1, :], (SHIFTS, w))

    def col(ref, k, rows=slice(None)):
        return ref[rows, k * w:(k + 1) * w]

    a_x, a_b, a_c, a_z = col(p_ref, 0), col(p_ref, 1), col(p_ref, 2), col(p_ref, 3)
    b_val, b_gate, b_z = col(p_ref, 4), col(p_ref, 5), col(p_ref, 6)
    keep = jnp.where(first, 0.0, 1.0)

    rows_a = slice(HALO_B - HALO_A, HALO_B)
    qbuf[0:HALO_A, :] = col(halo_ref, 2, rows_a) * col(halo_ref, 0, rows_a) * keep
    qbuf[HALO_A:HALO_A + ts, :] = a_c * a_x
    cq = jnp.zeros((ts, w), F32)
    for j in range(CONV_A):
        cq = cq + a_conv_ref[CONV_A - 1 - j:CONV_A - j, :] * qbuf[HALO_A - j:HALO_A - j + ts, :]
    ya = a_b * cq

    ysh[0, 0:HALO_B, :] = col(halo_ref, 4) * _sigmoid(col(halo_ref, 5)) * keep
    ysh[0, HALO_B:HALO_B + ts, :] = b_val * _sigmoid(b_gate)
    _fill_shifted_down(ysh, HALO_B + ts)
    _conv31(ysh, HALO_B, -SHIFTS, wb, bias_ref, y1buf, ts, w)
    yb1 = y1buf[...]
    xc = yb1 - _mean_last(yb1)
    rstd = lax.rsqrt(_mean_last(xc * xc) + EPS)
    xhat = xc * rstd
    yb2 = xhat * lng_ref[...] + lnb_ref[...]
    return dict(a_x=a_x, a_b=a_b, a_c=a_c, a_z=a_z, b_val=b_val, b_gate=b_gate, b_z=b_z,
                cq=cq, ya=ya, rstd=rstd, xhat=xhat, yb2=yb2)


def _even_specs(s, w, ts):
    tile = pl.BlockSpec((ts, 7 * w), lambda i: (i, 0))
    halo = pl.BlockSpec((HALO_B, 7 * w), lambda i: (jnp.maximum(i * (ts // HALO_B) - 1, 0), 0))
    return tile, halo


def _small_specs(shapes, index=lambda i: (0, 0)):
    return [pl.BlockSpec(sh, index) for sh in shapes]


def _even_mixer_fwd(p, w_out, x_res, gain, a_conv, b_conv, bias, ln_g, ln_b, name, comm=None):
    s, d = x_res.shape
    w = p.shape[1] // 7
    ts = _row_tile(s, 128)
    assert ts % HALO_B == 0

    def body(p_ref, halo_ref, wout_ref, x_ref, g_ref, ac_ref, bc_ref, bias_ref, lng_ref, lnb_ref,
             u_ref, xn_ref, y_ref, qbuf, ysh, y1buf, wb):
        first = pl.program_id(0) == 0
        f = _even_forward_tile(p_ref, halo_ref, first, ac_ref, bc_ref, bias_ref, lng_ref, lnb_ref, qbuf, ysh, y1buf,
                               wb, w, ts)
        yb3 = f["yb2"] * _sigmoid(f["yb2"])
        u_a = (f["ya"] * (f["a_z"] * _sigmoid(f["a_z"]))).astype(BF16)
        u_b = (yb3 * (f["b_z"] * _sigmoid(f["b_z"]))).astype(BF16)
        u_ref[:, 0:w] = u_a
        u_ref[:, w:2 * w] = u_b
        y = (jnp.dot(u_a, wout_ref[0:w, :], preferred_element_type=F32)
             + jnp.dot(u_b, wout_ref[w:2 * w, :], preferred_element_type=F32))
        r = lax.rsqrt(_mean_last(y * y) + EPS)
        y_ref[...] = y
        xn_ref[...] = x_ref[...] + (y * r) * g_ref[...]

    tile, halo = _even_specs(s, w, ts)
    row = pl.BlockSpec((ts, d), lambda i: (i, 0))
    return _call(
        body, name=name, grid=(s // ts,),
        in_specs=[tile, halo, pl.BlockSpec((2 * w, d), lambda i: (0, 0)), row, pl.BlockSpec((1, d), lambda i: (0, 0))]
        + _small_specs([(CONV_A, w), (CONV_B, w), (1, w), (1, w), (1, w)]),
        out_specs=[pl.BlockSpec((ts, 2 * w), lambda i: (i, 0)), row, row],
        out_shape=[jax.ShapeDtypeStruct((s, 2 * w), BF16), jax.ShapeDtypeStruct((s, d), F32),
                   jax.ShapeDtypeStruct((s, d), F32)],
        scratch_shapes=[pltpu.VMEM((HALO_A + ts, w), F32), pltpu.VMEM((SHIFTS, HALO_B + ts, w), F32),
                        pltpu.VMEM((ts, w), F32), pltpu.VMEM((CONV_B, SHIFTS, w), F32)],
        args=(p, p, w_out, x_res, gain, a_conv, b_conv, bias, ln_g, ln_b), comm=comm)


def _even_mixer_bwd(p, du, a_conv, b_conv, bias, ln_g, ln_b, name, comm=None):
    s = p.shape[0]
    w = p.shape[1] // 7
    ts = _row_tile(s, EVEN_BWD_ROWS)
    nt = s // ts
    assert ts % HALO_B == 0

    def body(p_ref, halo_ref, du_ref, ac_ref, bc_ref, bias_ref, lng_ref, lnb_ref,
             dp_ref, dac_ref, dbc_ref, dbias_ref, dlng_ref, dlnb_ref,
             qbuf, ysh, y1buf, dqbuf, dsh, dy0buf, wacc, carry_dq, carry_dy, wb):
        step = pl.program_id(0)
        first = step == nt - 1

        @pl.when(step == 0)
        def _():
            for ref in (dac_ref, dbias_ref, dlng_ref, dlnb_ref, wacc, carry_dq, carry_dy):
                ref[...] = jnp.zeros_like(ref)

        @pl.when(step == 0)
        def _():
            for k in range(CONV_B):
                wb[k] = jnp.broadcast_to(bc_ref[k:k + 1, :], (SHIFTS, w))

        keep = jnp.where(first, 0.0, 1.0)
        lane_blocks = [slice(l0, l0 + min(LANE_PASS, w)) for l0 in range(0, w, min(LANE_PASS, w))]

        def col(ref, k, lanes, rows=slice(None)):
            return ref[rows, k * w + lanes.start:k * w + lanes.stop]


        rows_a = slice(HALO_B - HALO_A, HALO_B)
        for lanes in lane_blocks:
            a_x, a_b, a_c, a_z = (col(p_ref, k, lanes) for k in range(4))
            qbuf[0:HALO_A, lanes] = col(halo_ref, 2, lanes, rows_a) * col(halo_ref, 0, lanes, rows_a) * keep
            qbuf[HALO_A:HALO_A + ts, lanes] = a_c * a_x
            cq = jnp.zeros((ts, lanes.stop - lanes.start), F32)
            for j in range(CONV_A):
                cq = cq + ac_ref[CONV_A - 1 - j:CONV_A - j, lanes] * qbuf[HALO_A - j:HALO_A - j + ts, lanes]
            s_az = _sigmoid(a_z)
            du_a = du_ref[:, lanes]
            d_ya = du_a * (a_z * s_az)
            dp_ref[:, 3 * w + lanes.start:3 * w + lanes.stop] = (du_a * (a_b * cq) * _dsilu(a_z, s_az)).astype(BF16)
            dp_ref[:, 1 * w + lanes.start:1 * w + lanes.stop] = (d_ya * cq).astype(BF16)
            d_cq = d_ya * a_b
            dqbuf[0:ts, lanes] = d_cq
            dqbuf[ts:ts + HALO_A, lanes] = carry_dq[:, lanes]
            carry_dq[:, lanes] = d_cq[0:HALO_A, :]
            d_q = jnp.zeros_like(cq)
            for o in range(CONV_A):
                d_q = d_q + ac_ref[CONV_A - 1 - o:CONV_A - o, lanes] * dqbuf[o:o + ts, lanes]
            for j in range(CONV_A):
                k = CONV_A - 1 - j
                dac_ref[k:k + 1, lanes] += _sum_rows(d_cq * qbuf[HALO_A - j:HALO_A - j + ts, lanes])
            dp_ref[:, 2 * w + lanes.start:2 * w + lanes.stop] = (d_q * a_x).astype(BF16)
            dp_ref[:, 0 * w + lanes.start:0 * w + lanes.stop] = (d_q * a_c).astype(BF16)
            ysh[0, 0:HALO_B, lanes] = col(halo_ref, 4, lanes) * _sigmoid(col(halo_ref, 5, lanes)) * keep
            ysh[0, HALO_B:HALO_B + ts, lanes] = col(p_ref, 4, lanes) * _sigmoid(col(p_ref, 5, lanes))
        _fill_shifted_down(ysh, HALO_B + ts)
        _conv31(ysh, HALO_B, -SHIFTS, wb, bias_ref, y1buf, ts, w)

        total = jnp.zeros((ts, 1), F32)
        for lanes in lane_blocks:
            total = total + jnp.sum(y1buf[:, lanes], axis=-1, keepdims=True)
        mu = total * (1.0 / w)
        total = jnp.zeros((ts, 1), F32)
        for lanes in lane_blocks:
            xc = y1buf[:, lanes] - mu
            total = total + jnp.sum(xc * xc, axis=-1, keepdims=True)
        rstd = lax.rsqrt(total * (1.0 / w) + EPS)

        sum_dxh = jnp.zeros((ts, 1), F32)
        sum_dxh_xhat = jnp.zeros((ts, 1), F32)
        for lanes in lane_blocks:
            xhat = (y1buf[:, lanes] - mu) * rstd
            yb2 = xhat * lng_ref[:, lanes] + lnb_ref[:, lanes]
            b_z = col(p_ref, 6, lanes)
            s_bz, s_y2 = _sigmoid(b_z), _sigmoid(yb2)
            du_b = du_ref[:, w + lanes.start:w + lanes.stop]
            dp_ref[:, 6 * w + lanes.start:6 * w + lanes.stop] = (du_b * (yb2 * s_y2) * _dsilu(b_z, s_bz)).astype(BF16)
            d_yb2 = du_b * (b_z * s_bz) * _dsilu(yb2, s_y2)
            dlng_ref[:, lanes] += _sum_rows(d_yb2 * xhat)
            dlnb_ref[:, lanes] += _sum_rows(d_yb2)
            d_xh = d_yb2 * lng_ref[:, lanes]
            sum_dxh = sum_dxh + jnp.sum(d_xh, axis=-1, keepdims=True)
            sum_dxh_xhat = sum_dxh_xhat + jnp.sum(d_xh * xhat, axis=-1, keepdims=True)
            y1buf[:, lanes] = xhat
            dy0buf[:, lanes] = d_xh
        mean_dxh = sum_dxh * (1.0 / w)
        mean_dxh_xhat = sum_dxh_xhat * (1.0 / w)

        for lanes in lane_blocks:
            d_yb1 = rstd * (dy0buf[:, lanes] - mean_dxh - y1buf[:, lanes] * mean_dxh_xhat)
            dbias_ref[:, lanes] += _sum_rows(d_yb1)
            dsh[0, 0:ts, lanes] = d_yb1
            dsh[0, ts:ts + HALO_B, lanes] = carry_dy[:, lanes]
            carry_dy[:, lanes] = d_yb1[0:HALO_B, :]
        _fill_shifted_up(dsh, ts + HALO_B)
        _conv31(dsh, 0, SHIFTS, wb, None, dy0buf, ts, w)
        _conv31_weight_grad(dsh, ysh, wacc, ts, w)

        @pl.when(step == nt - 1)
        def _():
            for k in range(CONV_B):
                dbc_ref[k:k + 1, :] = _sum_rows(wacc[k])

        for lanes in lane_blocks:
            d_yb0 = dy0buf[:, lanes]
            s_g = _sigmoid(col(p_ref, 5, lanes))
            dp_ref[:, 4 * w + lanes.start:4 * w + lanes.stop] = (d_yb0 * s_g).astype(BF16)
            dp_ref[:, 5 * w + lanes.start:5 * w + lanes.stop] = (
                d_yb0 * col(p_ref, 4, lanes) * s_g * (1.0 - s_g)).astype(BF16)

    rev = lambda i: (nt - 1 - i, 0)
    tile = pl.BlockSpec((ts, 7 * w), rev)
    halo = pl.BlockSpec((HALO_B, 7 * w), lambda i: (jnp.maximum((nt - 1 - i) * (ts // HALO_B) - 1, 0), 0))
    small = [(CONV_A, w), (CONV_B, w), (1, w), (1, w), (1, w)]
    return _call(
        body, name=name, grid=(nt,),
        in_specs=[tile, halo, pl.BlockSpec((ts, 2 * w), rev)] + _small_specs(small),
        out_specs=[pl.BlockSpec((ts, 7 * w), rev)] + _small_specs(small),
        out_shape=[jax.ShapeDtypeStruct((s, 7 * w), BF16)] + [jax.ShapeDtypeStruct(sh, F32) for sh in small],
        scratch_shapes=[pltpu.VMEM((HALO_A + ts, w), F32), pltpu.VMEM((SHIFTS, HALO_B + ts, w), F32),
                        pltpu.VMEM((ts, w), F32),
                        pltpu.VMEM((ts + HALO_A, w), F32), pltpu.VMEM((SHIFTS, ts + HALO_B, w), F32),
                        pltpu.VMEM((ts, w), F32), pltpu.VMEM((CONV_B, SHIFTS, w), F32),
                        pltpu.VMEM((HALO_A, w), F32), pltpu.VMEM((HALO_B, w), F32),
                        pltpu.VMEM((CONV_B, SHIFTS, w), F32)],
        args=(p, p, du, a_conv, b_conv, bias, ln_g, ln_b), comm=comm)


def _pool_forward_tile(p_ref, halo_ref, first, tile_index, cw_ref, cb_ref, cs_ref, vbuf, c, gc, ts):
    vbuf[0:HALO_P, :] = halo_ref[...] * jnp.where(first, 0.0, 1.0)
    vbuf[HALO_P:HALO_P + ts, :] = p_ref[:, 0:c]
    pos = tile_index * ts + lax.broadcasted_iota(jnp.int32, (ts, 1), 0) + 1
    pooled, inv, gout = [], [], []
    for g, win in enumerate(POOL_WINDOWS):
        cols = slice(g * gc, (g + 1) * gc)
        acc = jnp.zeros((ts, gc), F32)
        for j in range(win):
            acc = acc + vbuf[HALO_P - j:HALO_P - j + ts, cols]
        inv_g = 1.0 / jnp.minimum(pos, win).astype(F32)
        pooled_g = (acc * inv_g - p_ref[:, cols]).astype(BF16)
        pooled.append(pooled_g)
        inv.append(inv_g)
        gout.append(jnp.dot(pooled_g, cw_ref[g], preferred_element_type=F32) + cb_ref[:, cols])
    return pooled, inv, gout


def _odd_mixer_fwd(p, cw, cb, cs, name):
    s = p.shape[0]
    c = p.shape[1] // 2
    gc = c // N_GROUPS
    ts = _row_tile(s, 256)

    def body(p_ref, halo_ref, cw_ref, cb_ref, cs_ref, u_ref, vbuf):
        i = pl.program_id(0)
        _, _, gout = _pool_forward_tile(p_ref, halo_ref, i == 0, i, cw_ref, cb_ref, cs_ref, vbuf, c, gc, ts)
        for g in range(N_GROUPS):
            cols = slice(g * gc, (g + 1) * gc)
            z = p_ref[:, c + g * gc:c + (g + 1) * gc]
            u_ref[:, cols] = (gout[g] * cs_ref[:, cols] * (z * _sigmoid(z))).astype(BF16)

    return pl.pallas_call(
        body, name=name, grid=(s // ts,),
        in_specs=[pl.BlockSpec((ts, 2 * c), lambda i: (i, 0)),
                  pl.BlockSpec((HALO_P, c), lambda i: (jnp.maximum(i * (ts // HALO_P) - 1, 0), 0)),
                  pl.BlockSpec((N_GROUPS, gc, gc), lambda i: (0, 0, 0)),
                  pl.BlockSpec((1, c), lambda i: (0, 0)), pl.BlockSpec((1, c), lambda i: (0, 0))],
        out_specs=pl.BlockSpec((ts, c), lambda i: (i, 0)),
        out_shape=jax.ShapeDtypeStruct((s, c), BF16),
        scratch_shapes=[pltpu.VMEM((HALO_P + ts, c), F32)],
        compiler_params=_params("arbitrary"),
    )(p, p, cw, cb, cs)


def _odd_mixer_bwd(p, du, cw, cb, cs, name):
    s = p.shape[0]
    c = p.shape[1] // 2
    gc = c // N_GROUPS
    ts = _row_tile(s, 256)
    nt = s // ts

    def body(p_ref, halo_ref, du_ref, cw_ref, cb_ref, cs_ref, dp_ref, dcw_ref, dcb_ref, dcs_ref, vbuf, ebuf, carry_e):
        step = pl.program_id(0)
        tile_index = nt - 1 - step

        @pl.when(step == 0)
        def _():
            for ref in (dcw_ref, dcb_ref, dcs_ref, carry_e):
                ref[...] = jnp.zeros_like(ref)

        pooled, inv, gout = _pool_forward_tile(p_ref, halo_ref, tile_index == 0, tile_index, cw_ref, cb_ref, cs_ref,
                                               vbuf, c, gc, ts)
        ebuf[ts:ts + HALO_P, :] = carry_e[...]
        for g, win in enumerate(POOL_WINDOWS):
            cols = slice(g * gc, (g + 1) * gc)
            z = p_ref[:, c + g * gc:c + (g + 1) * gc]
            sz = _sigmoid(z)
            du_g = du_ref[:, cols]
            scale = cs_ref[:, cols]
            d_y = du_g * (z * sz)
            dp_ref[:, c + g * gc:c + (g + 1) * gc] = (du_g * (gout[g] * scale) * _dsilu(z, sz)).astype(BF16)
            dcs_ref[:, cols] += _sum_rows(d_y * gout[g])
            d_gout = d_y * scale
            dcb_ref[:, cols] += _sum_rows(d_gout)
            d_gout_b = d_gout.astype(BF16)
            dcw_ref[g] += lax.dot_general(pooled[g], d_gout_b, (((0,), (0,)), ((), ())), preferred_element_type=F32)
            d_pool = lax.dot_general(d_gout_b, cw_ref[g], (((1,), (1,)), ((), ())), preferred_element_type=F32)
            e = d_pool * inv[g]
            ebuf[0:ts, cols] = e
            d_v = -d_pool
            for o in range(win):
                d_v = d_v + ebuf[o:o + ts, cols]
            dp_ref[:, cols] = d_v.astype(BF16)
            carry_e[:, cols] = e[0:HALO_P, :]

    rev = lambda i: (nt - 1 - i, 0)
    small = [(N_GROUPS, gc, gc), (1, c), (1, c)]
    return pl.pallas_call(
        body, name=name, grid=(nt,),
        in_specs=[pl.BlockSpec((ts, 2 * c), rev),
                  pl.BlockSpec((HALO_P, c), lambda i: (jnp.maximum((nt - 1 - i) * (ts // HALO_P) - 1, 0), 0)),
                  pl.BlockSpec((ts, c), rev),
                  pl.BlockSpec((N_GROUPS, gc, gc), lambda i: (0, 0, 0)),
                  pl.BlockSpec((1, c), lambda i: (0, 0)), pl.BlockSpec((1, c), lambda i: (0, 0))],
        out_specs=[pl.BlockSpec((ts, 2 * c), rev),
                   pl.BlockSpec((N_GROUPS, gc, gc), lambda i: (0, 0, 0)),
                   pl.BlockSpec((1, c), lambda i: (0, 0)), pl.BlockSpec((1, c), lambda i: (0, 0))],
        out_shape=[jax.ShapeDtypeStruct((s, 2 * c), BF16)] + [jax.ShapeDtypeStruct(sh, F32) for sh in small],
        scratch_shapes=[pltpu.VMEM((HALO_P + ts, c), F32), pltpu.VMEM((ts + HALO_P, c), F32),
                        pltpu.VMEM((HALO_P, c), F32)],
        compiler_params=_params("arbitrary"),
    )(p, p, du, cw, cb, cs)


def _cast_into_slot(a, coords, name):
    r, cols = a.shape
    tr = _row_tile(r // 2, 256)
    per = r // 2 // tr

    def body(co_ref, a_ref, o_ref):
        o_ref[0, 0] = a_ref[...].astype(BF16)

    return pl.pallas_call(
        body, name=name,
        grid_spec=pltpu.PrefetchScalarGridSpec(
            num_scalar_prefetch=1, grid=(2, per),
            in_specs=[pl.BlockSpec((tr, cols), lambda h, i, co: (h * per + i, 0))],
            out_specs=pl.BlockSpec((1, 1, tr, cols), lambda h, i, co: (co[0], h, i, 0))),
        out_shape=jax.ShapeDtypeStruct((N_SHARDS, 2, r // 2, cols), BF16),
        compiler_params=_params("arbitrary", "arbitrary"),
    )(coords, a)


def _chip_sum(g, other, coords, name):
    n_sh, _, r2, cols = g.shape
    tr = _row_tile(r2, 256)

    def body(co_ref, g_ref, o_ref, sum_ref, mine_ref):
        v = (g_ref[0, 0].astype(F32) + o_ref[0].astype(F32)).astype(BF16)
        sum_ref[0] = v

        @pl.when(pl.program_id(1) == co_ref[0])
        def _():
            mine_ref[0] = v

    piece = pl.BlockSpec((1, tr, cols), lambda i, s, co: (s, i, 0))
    return pl.pallas_call(
        body, name=name,
        grid_spec=pltpu.PrefetchScalarGridSpec(
            num_scalar_prefetch=1, grid=(r2 // tr, n_sh),
            in_specs=[pl.BlockSpec((1, 1, tr, cols), lambda i, s, co: (s, co[1], i, 0)), piece],
            out_specs=[piece, pl.BlockSpec((1, tr, cols), lambda i, s, co: (co[0], i, 0))]),
        out_shape=[jax.ShapeDtypeStruct((n_sh, r2, cols), BF16)] * 2,
        compiler_params=_params("arbitrary", "arbitrary"),
    )(coords, g, other)


def _shard_sum(pieces, coords, name):
    n_sh, r2, cols = pieces.shape
    tr = _row_tile(r2, 256)

    def body(co_ref, p_ref, o_ref):
        acc = p_ref[0].astype(F32)
        for k in range(1, n_sh):
            acc = acc + p_ref[k].astype(F32)
        o_ref[0] = acc

    return pl.pallas_call(
        body, name=name,
        grid_spec=pltpu.PrefetchScalarGridSpec(
            num_scalar_prefetch=1, grid=(r2 // tr,),
            in_specs=[pl.BlockSpec((n_sh, tr, cols), lambda i, co: (0, i, 0))],
            out_specs=pl.BlockSpec((1, tr, cols), lambda i, co: (co[1], i, 0))),
        out_shape=jax.ShapeDtypeStruct((2, r2, cols), F32),
        compiler_params=_params("arbitrary"),
    )(coords, pieces)


def _sum_small(a, name):
    n, r, cols = a.shape

    def body(a_ref, o_ref):
        acc = a_ref[0]
        for k in range(1, n):
            acc = acc + a_ref[k]
        o_ref[...] = acc

    return pl.pallas_call(
        body, name=name,
        in_specs=[pl.BlockSpec((n, r, cols), lambda: (0, 0, 0))],
        out_specs=pl.BlockSpec((r, cols), lambda: (0, 0)),
        out_shape=jax.ShapeDtypeStruct((r, cols), F32),
        compiler_params=_params(),
    )(a)


def _adamw(w, g, m, v, name):
    r, cols = w.shape
    tr = _row_tile(r, 256) if r % SUBLANES_BF16 == 0 else r

    def body(w_ref, g_ref, m_ref, v_ref, d_ref, nm_ref, nv_ref):
        g = g_ref[...]
        m = ADAM_B1 * m_ref[...] + (1.0 - ADAM_B1) * g
        v = ADAM_B2 * v_ref[...] + (1.0 - ADAM_B2) * (g * g)
        m_hat = m / (1.0 - ADAM_B1 ** ADAM_STEP)
        v_hat = v / (1.0 - ADAM_B2 ** ADAM_STEP)
        d_ref[...] = -ADAM_LR * (m_hat / (jnp.sqrt(v_hat) + ADAM_EPS) + ADAM_WD * w_ref[...])
        nm_ref[...] = m
        nv_ref[...] = v

    blk = pl.BlockSpec((tr, cols), lambda i: (i, 0))
    return pl.pallas_call(
        body, name=name, grid=(r // tr,),
        in_specs=[blk] * 4, out_specs=[blk] * 3,
        out_shape=[jax.ShapeDtypeStruct((r, cols), F32)] * 3,
        compiler_params=_params("arbitrary"),
    )(w, g, m, v)


def _place():
    x, y, c = lax.axis_index("x"), lax.axis_index("y"), lax.axis_index("c")
    other_chips = [(1 - x, y), (x, 1 - y), (1 - x, 1 - y)]
    return x, y, c, other_chips


def _chip(xy):
    return 2 * xy[0] + xy[1]


def _remote(src, dst, send_sem, recv_sem, to):
    return pltpu.make_async_remote_copy(src_ref=src, dst_ref=dst, send_sem=send_sem, recv_sem=recv_sem,
                                        device_id=to, device_id_type=MESH)


def _gather_ici(ctx, k, j, start):
    (x, y, c, chips), b, send, recv = ctx
    if j < 2:
        chip, to = (_chip((x, y)) if start else _chip(chips[j])), (*chips[j], c)
    else:
        chip = 2 * (x ^ c) + (y ^ (1 - c)) if start else _chip(chips[2])
        to = (x ^ (1 - c), y ^ c, c)
    blk = b[k].at[chip, c]
    return _remote(blk, blk, send.at[6 * k + j], recv.at[6 * k + j], to)


def _gather_d2d(ctx, k, j, start):
    (x, y, c, chips), b, send, recv = ctx
    blk = b[k].at[_chip(chips[j]), c if start else 1 - c]
    return _remote(blk, blk, send.at[6 * k + 3 + j], recv.at[6 * k + 3 + j], (x, y, 1 - c))


def _gather_small(ctx, n, j, start):
    (x, y, c, chips), b, send, recv = ctx
    blk = b[n].at[_chip((x, y)) if start else _chip(chips[j])]
    return _remote(blk, blk, send.at[6 * n + j], recv.at[6 * n + j], (*chips[j], c))


def _gather_neighbours_landed(ctx, k):
    for j in range(2):
        _gather_ici(ctx, k, j, False).wait_recv()
    _gather_ici(ctx, k, 2, True).start()
    for j in range(2):
        _gather_d2d(ctx, k, j, True).start()


def _gather_diagonal_landed(ctx, k):
    _gather_ici(ctx, k, 2, False).wait_recv()
    _gather_d2d(ctx, k, 2, True).start()


def _gather_comm(bufs, relay_at, forward_at):
    n = len(bufs)

    def start(srcs, b, send, recv):
        for k in range(n):
            for j in range(2):
                _gather_ici((_place(), b, send, recv), k, j, True).start()

    def relay(srcs, b, send, recv):
        for k in range(n):
            _gather_neighbours_landed((_place(), b, send, recv), k)

    def forward(srcs, b, send, recv):
        for k in range(n):
            _gather_diagonal_landed((_place(), b, send, recv), k)

    def finish(srcs, b, send, recv):
        ctx = (_place(), b, send, recv)
        for k in range(n):
            for j in range(3):
                _gather_d2d(ctx, k, j, False).wait_recv()
                _gather_ici(ctx, k, j, True).wait_send()
                _gather_d2d(ctx, k, j, True).wait_send()

    return _Comm([], bufs, 6 * n, [(0, start), (relay_at, relay), (forward_at, forward)], finish)


def _rmsnorm(x, gain, name):
    s, d = x.shape
    tm = _row_tile(s, 512)

    def body(x_ref, g_ref, h_ref):
        xv = x_ref[...]
        r = lax.rsqrt(_mean_last(xv * xv) + EPS)
        h_ref[...] = (xv * r * g_ref[...]).astype(BF16)

    return pl.pallas_call(
        body, name=name, grid=(s // tm,),
        in_specs=[pl.BlockSpec((tm, d), lambda i: (i, 0)), pl.BlockSpec((1, d), lambda i: (0, 0))],
        out_specs=pl.BlockSpec((tm, d), lambda i: (i, 0)),
        out_shape=jax.ShapeDtypeStruct((s, d), BF16),
        compiler_params=_params("arbitrary"),
    )(x, gain)


def _gathered_in_proj(h, bufs, small, order, name):
    s, d = h.shape
    n_sh, _, r2, ns = bufs[0].shape
    assert d == 2 * r2
    n = len(bufs)
    tm = _row_tile(s, 512)
    n_i = s // tm
    hook_i = max(n_i - 2, 0)
    n_sems = 6 * n + 3

    def body(order_ref, h_ref, *rest):
        p_ref = rest[n + 1]
        b = rest[n + 2:2 * n + 3]
        w_vmem, w_sems, send, recv = rest[2 * n + 3:]
        j, i = pl.program_id(0), pl.program_id(1)
        ctx = (_place(), b, send, recv)

        def fetch(q):
            return pltpu.make_async_copy(b[0].at[order_ref[q]], w_vmem.at[q % 2], w_sems.at[q % 2])

        @pl.when((j == 0) & (i == 0))
        def _():
            for k in range(n):
                for peer in range(2):
                    _gather_ici(ctx, k, peer, True).start()
            for peer in range(3):
                _gather_small(ctx, n, peer, True).start()
            fetch(0).start()
            fetch(0).wait()

        for q in range(1, n_sh):
            @pl.when((j == q - 1) & (i == hook_i))
            def _(q=q):
                if q == 1:
                    _gather_neighbours_landed(ctx, 0)
                if q == 2:
                    for k in range(1, n):
                        _gather_neighbours_landed(ctx, k)
                if q == 3:
                    for k in range(n):
                        _gather_diagonal_landed(ctx, k)
                _gather_d2d(ctx, 0, q - 1, False).wait_recv()
                fetch(q).start()

            @pl.when((j == q) & (i == 0))
            def _(q=q):
                fetch(q).wait()

        wv = w_vmem.at[j % 2]
        p_ref[...] = (jnp.dot(h_ref[:, 0:r2], wv[0], preferred_element_type=F32)
                      + jnp.dot(h_ref[:, r2:d], wv[1], preferred_element_type=F32))

        @pl.when((j == n_sh - 1) & (i == n_i - 1))
        def _():
            for peer in range(3):
                _gather_small(ctx, n, peer, False).wait_recv()
                _gather_small(ctx, n, peer, True).wait_send()
            for k in range(n):
                for peer in range(3):
                    if k > 0:
                        _gather_d2d(ctx, k, peer, False).wait_recv()
                    _gather_ici(ctx, k, peer, True).wait_send()
                    _gather_d2d(ctx, k, peer, True).wait_send()

    all_bufs = list(bufs) + [small]
    out = pl.pallas_call(
        body, name=name,
        grid_spec=pltpu.PrefetchScalarGridSpec(
            num_scalar_prefetch=1, grid=(n_sh, n_i),
            in_specs=[pl.BlockSpec((tm, d), lambda j, i, o: (i, 0))] + [ANY] * (n + 1),
            out_specs=[pl.BlockSpec((tm, ns), lambda j, i, o: (i, o[j]))] + [ANY] * (n + 1),
            scratch_shapes=[pltpu.VMEM((2, 2, r2, ns), BF16), pltpu.SemaphoreType.DMA((2,)),
                            pltpu.SemaphoreType.DMA((n_sems,)), pltpu.SemaphoreType.DMA((n_sems,))]),
        out_shape=[jax.ShapeDtypeStruct((s, n_sh * ns), F32)]
        + [jax.ShapeDtypeStruct(a.shape, a.dtype) for a in all_bufs],
        input_output_aliases={2 + t: 1 + t for t in range(n + 1)},
        compiler_params=_params("arbitrary", "arbitrary"),
    )(order, h, *all_bufs)
    return out[0], list(out[1:])


def _exchange_comm(grads):
    n = len(grads)
    landing = [lax.empty((N_SHARDS,) + a.shape[2:], a.dtype) for a in grads]

    def copies(srcs, b, send, recv):
        x, y, c, _ = _place()
        return [_remote(srcs[k].at[s, 1 - c], b[k].at[s], send.at[N_SHARDS * k + s], recv.at[N_SHARDS * k + s],
                        (x, y, 1 - c)) for k in range(n) for s in range(N_SHARDS)]

    def start(srcs, b, send, recv):
        for cp in copies(srcs, b, send, recv):
            cp.start()

    def finish(srcs, b, send, recv):
        for cp in copies(srcs, b, send, recv):
            cp.wait()

    return _Comm(grads, landing, N_SHARDS * n, [(0, start)], finish)


def _scatter_comm(chip_sums, landing):
    n = len(chip_sums)

    def big(srcs, b, send, recv, k, j, start):
        x, y, c, chips = _place()
        dst = b[k].at[_chip((x, y)) if start else _chip(chips[j])]
        return _remote(srcs[k].at[_chip(chips[j])], dst, send.at[3 * k + j], recv.at[3 * k + j], (*chips[j], c))

    def start(srcs, b, send, recv):
        for k in range(n):
            for j in range(3):
                big(srcs, b, send, recv, k, j, True).start()

    def finish(srcs, b, send, recv):
        for k in range(n):
            for j in range(3):
                big(srcs, b, send, recv, k, j, False).wait_recv()
                big(srcs, b, send, recv, k, j, True).wait_send()

    return _Comm(chip_sums, landing, 3 * n, [(0, start)], finish)


def _join_comm(halves, small):
    n = len(halves)
    flips = [(fx, fy, fc) for fx in (0, 1) for fy in (0, 1) for fc in (0, 1)][1:]

    def half(b, send, recv, k, start):
        x, y, c, _ = _place()
        return _remote(b[k].at[c], b[k].at[c if start else 1 - c], send.at[k], recv.at[k], (x, y, 1 - c))

    def small_copy(b, send, recv, q, start):
        x, y, c, _ = _place()
        px, py, pc = x ^ flips[q][0], y ^ flips[q][1], c ^ flips[q][2]
        blk = b[n].at[4 * x + 2 * y + c if start else 4 * px + 2 * py + pc]
        return _remote(blk, blk, send.at[n + q], recv.at[n + q], (px, py, pc))

    def start(srcs, b, send, recv):
        for q in range(len(flips)):
            small_copy(b, send, recv, q, True).start()
        for k in range(n):
            half(b, send, recv, k, True).start()

    def finish(srcs, b, send, recv):
        for q in range(len(flips)):
            small_copy(b, send, recv, q, False).wait()
        for k in range(n):
            half(b, send, recv, k, False).wait()

    return _Comm([], list(halves) + [small], n + len(flips), [(0, start)], finish)


def _flat_rows(parts):
    flat = jnp.concatenate([p.reshape(-1) for p in parts])
    assert flat.shape[0] % LANES == 0
    return flat.reshape(-1, LANES)


def _unflatten(flat, shapes):
    out, at = [], 0
    for sh in shapes:
        size = 1
        for dim in sh:
            size *= dim
        out.append(flat[at:at + size].reshape(sh))
        at += size
    assert at == flat.shape[0], (at, flat.shape)
    return out


def _col_shards_to_full(a, rows):
    q = a.shape[1] // rows
    return a.reshape(N_SHARDS, rows, q).transpose(1, 0, 2).reshape(rows, N_SHARDS * q)


def _my_col_shard(full, chip):
    rows, cols = full.shape
    q = cols // N_SHARDS
    return lax.dynamic_index_in_dim(full.reshape(rows, N_SHARDS, q), chip, axis=1, keepdims=False)


def kernel(x, e_norm_pre, e_norm_post, e_w_in, e_a_conv, e_b_conv, e_b_conv_bias, e_b_ln_g, e_b_ln_b, e_w_out, o_norm_pre, o_norm_post, o_w_in, o_c_w, o_c_b, o_c_scale, o_w_out, loss_target, m_e_norm_pre, m_e_norm_post, m_e_w_in, m_e_a_conv, m_e_b_conv, m_e_b_conv_bias, m_e_b_ln_g, m_e_b_ln_b, m_e_w_out, m_o_norm_pre, m_o_norm_post, m_o_w_in, m_o_c_w, m_o_c_b, m_o_c_scale, m_o_w_out, v_e_norm_pre, v_e_norm_post, v_e_w_in, v_e_a_conv, v_e_b_conv, v_e_b_conv_bias, v_e_b_ln_g, v_e_b_ln_b, v_e_w_out, v_o_norm_pre, v_o_norm_post, v_o_w_in, v_o_c_w, v_o_c_b, v_o_c_scale, v_o_w_out):
    _, s, d = x.shape
    w = d // 2
    c = d
    gc = c // N_GROUPS
    wq, cq, gq = w // N_SHARDS, c // N_SHARDS, gc // N_SHARDS
    chip = 2 * lax.axis_index("x") + lax.axis_index("y")
    core = lax.axis_index("c")
    x2 = x.reshape(s, d)
    target = loss_target.reshape(s, d)

    big_w = [e_w_in[0], e_w_out[0], o_w_in[0], o_c_w[0].reshape(N_GROUPS * gq, gc), o_w_out[0]]
    big_m = [m_e_w_in[0], m_e_w_out[0], m_o_w_in[0], m_o_c_w[0].reshape(N_GROUPS * gq, gc), m_o_w_out[0]]
    big_v = [v_e_w_in[0], v_e_w_out[0], v_o_w_in[0], v_o_c_w[0].reshape(N_GROUPS * gq, gc), v_o_w_out[0]]
    coords = jnp.stack([chip, core]).astype(jnp.int32)
    slots = [_cast_into_slot(a, coords, "cast_%d" % k) for k, a in enumerate(big_w)]
    sharded_small = _flat_rows([e_a_conv[0], e_b_conv[0], o_norm_pre, o_norm_post, o_c_scale, o_c_b[0]])
    small_slots = lax.dynamic_update_index_in_dim(jnp.zeros((N_SHARDS,) + sharded_small.shape, F32), sharded_small,
                                                  chip, 0)
    xi, yi = lax.axis_index("x"), lax.axis_index("y")
    order = jnp.stack([chip, 2 * (1 - xi) + yi, 2 * xi + (1 - yi), 2 * (1 - xi) + (1 - yi)]).astype(jnp.int32)
    h0 = _rmsnorm(x2, e_norm_pre, "e_pre_norm")
    p0, (e_w_in_g, e_w_out_g, small_g4) = _gathered_in_proj(h0, slots[:2], small_slots, order, "e_in_proj")
    e_w_in_sm = e_w_in_g.reshape((N_SHARDS,) + big_w[0].shape)
    e_w_out_f = e_w_out_g.reshape(w + w, d)
    sm = small_g4.reshape(N_SHARDS, -1)
    at = [0]

    def take(rows, q):
        blk = sm[:, at[0]:at[0] + rows * q]
        at[0] += rows * q
        return _col_shards_to_full(blk, rows)

    a_conv_f = take(CONV_A, wq)
    b_conv_f = take(CONV_B, wq)
    o_pre_f = take(1, cq)
    o_post_f = take(1, cq)
    cs_f = take(1, cq)
    cb_f = take(N_GROUPS, gq).reshape(1, c)

    mixer_steps = s // _row_tile(s, 128)
    (u0, x1, y0), odd_g = _even_mixer_fwd(p0, e_w_out_f, x2, e_norm_post, a_conv_f, b_conv_f, e_b_conv_bias, e_b_ln_g,
                                          e_b_ln_b, "e_mixer_out_proj",
                                          comm=_gather_comm(slots[2:], mixer_steps // 2, (25 * mixer_steps) // 32))
    o_w_in_sm = odd_g[0].reshape((N_SHARDS,) + big_w[2].shape)
    cw_f = odd_g[1].reshape(N_SHARDS, N_GROUPS, gq, gc).transpose(1, 0, 2, 3).reshape(N_GROUPS, gc, gc)
    o_w_out_f = odd_g[2].reshape(c, d)
    p1, h1 = _norm_matmul(x1, o_pre_f, o_w_in_sm, "o_in_proj")
    u1 = _odd_mixer_fwd(p1, cw_f, cb_f, cs_f, "o_mixer_fwd")
    d_y1, d_x2, d_o_post, loss_part = _matmul_post_loss(u1, o_w_out_f, x1, o_post_f, target, "o_out_proj_loss")

    def as_pieces(g, k):
        return g.reshape(N_SHARDS, 2, big_w[k].shape[0] // 2, big_w[k].shape[1])

    def chip_sums(ks, pieces, from_sibling):
        both = [_chip_sum(g, o, coords, "chip_sum_%d" % k) for k, g, o in zip(ks, pieces, from_sibling)]
        return [b[0] for b in both], [b[1] for b in both]

    g_o_w_out = _matmul_tn(u1, d_y1, 1, "o_w_out_grad")
    d_u1, _ = _matmul_nt(d_y1, o_w_out_f[None], "o_out_proj_bwd")
    d_p1, d_cw, d_cb, d_cs = _odd_mixer_bwd(p1, d_u1, cw_f, cb_f, cs_f, "o_mixer_bwd")
    g_o_w_in = _matmul_tn(h1, d_p1, N_SHARDS, "o_w_in_grad")
    g_cw = d_cw.reshape(N_GROUPS, N_SHARDS, gq, gc).transpose(1, 0, 2, 3).astype(BF16)
    pieces_o = [as_pieces(g_o_w_in, 2), as_pieces(g_cw, 3), as_pieces(g_o_w_out, 4)]
    d_h1, sibling_o = _matmul_nt(d_p1, o_w_in_sm, "o_in_proj_bwd", comm=_exchange_comm(pieces_o))
    d_x1, d_o_pre, d_y0, d_e_post = _norm_bwd(d_h1, x1, o_pre_f, d_x2, "o_pre_norm_bwd", post=(y0, e_norm_post))

    pieces_e = [as_pieces(_matmul_tn(u0, d_y0, 1, "e_w_out_grad"), 1)]
    d_u0, sibling_e = _matmul_nt(d_y0, e_w_out_f[None], "e_out_proj_bwd", comm=_exchange_comm(pieces_e))
    sums_a, landing_a = chip_sums([1, 2, 3, 4], pieces_e + pieces_o, sibling_e + sibling_o)
    (d_p0, d_a_conv, d_b_conv, d_bias, d_ln_g, d_ln_b), landed_a = _even_mixer_bwd(
        p0, d_u0, a_conv_f, b_conv_f, e_b_conv_bias, e_b_ln_g, e_b_ln_b, "e_mixer_bwd",
        comm=_scatter_comm(sums_a, landing_a))
    pieces_b = [as_pieces(_matmul_tn(h0, d_p0, N_SHARDS, "e_w_in_grad"), 0)]
    sums_b, landing_b = chip_sums([0], pieces_b, _comm_only(_exchange_comm(pieces_b), "exchange_core_halves"))
    d_h0, landed_b = _matmul_nt(d_p0, e_w_in_sm, "e_in_proj_bwd", comm=_scatter_comm(sums_b, landing_b))
    grad_x, d_e_pre = _norm_bwd(d_h0, x2, e_norm_pre, d_x1, "e_pre_norm_bwd")

    landed = landed_b + landed_a
    reduced = [_shard_sum(sc, coords, "shard_sum_%d" % k) for k, sc in enumerate(landed)]
    small_parts = _flat_rows([loss_part[0], d_e_pre, d_e_post, d_bias, d_ln_g, d_ln_b, d_a_conv, d_b_conv,
                              d_o_pre, d_o_post, d_cs, d_cb])
    small_rows = lax.dynamic_update_index_in_dim(jnp.zeros((N_DEVICES,) + small_parts.shape, F32), small_parts,
                                                 2 * chip + core, 0)
    joined = _comm_only(_join_comm(reduced, small_rows), "join_core_halves")
    big_g = [j.reshape(a.shape) for j, a in zip(joined[:5], big_w)]
    small_sum = _sum_small(joined[5], "small_sum").reshape(-1)
    (loss_row, g_e_pre, g_e_post, g_bias, g_ln_g, g_ln_b, g_a_conv_f, g_b_conv_f, g_o_pre_f, g_o_post_f, g_cs_f,
     g_cb_f) = _unflatten(small_sum, [(LANES,), (1, d), (1, d), (1, w), (1, w), (1, w), (CONV_A, w), (CONV_B, w),
                                      (1, c), (1, c), (1, c), (1, c)])
    loss = loss_row[0]
    g_a_conv = _my_col_shard(g_a_conv_f, chip)
    g_b_conv = _my_col_shard(g_b_conv_f, chip)
    g_o_pre = _my_col_shard(g_o_pre_f, chip)
    g_o_post = _my_col_shard(g_o_post_f, chip)
    g_cs = _my_col_shard(g_cs_f, chip)
    g_cb = _my_col_shard(g_cb_f.reshape(N_GROUPS, gc), chip)

    big_upd = [_adamw(wt, g, m, v, "adamw_%d" % k) for k, (wt, g, m, v) in enumerate(zip(big_w, big_g, big_m, big_v))]
    small_w = [e_norm_pre, e_norm_post, e_b_conv_bias, e_b_ln_g, e_b_ln_b, e_a_conv[0], e_b_conv[0],
               o_norm_pre, o_norm_post, o_c_b[0], o_c_scale]
    small_m = [m_e_norm_pre, m_e_norm_post, m_e_b_conv_bias, m_e_b_ln_g, m_e_b_ln_b, m_e_a_conv[0], m_e_b_conv[0],
               m_o_norm_pre, m_o_norm_post, m_o_c_b[0], m_o_c_scale]
    small_v = [v_e_norm_pre, v_e_norm_post, v_e_b_conv_bias, v_e_b_ln_g, v_e_b_ln_b, v_e_a_conv[0], v_e_b_conv[0],
               v_o_norm_pre, v_o_norm_post, v_o_c_b[0], v_o_c_scale]
    small_g = [g_e_pre, g_e_post, g_bias, g_ln_g, g_ln_b, g_a_conv, g_b_conv, g_o_pre, g_o_post, g_cb, g_cs]
    small_shapes = [a.shape for a in small_w]
    small_upd = _adamw(_flat_rows(small_w), _flat_rows(small_g), _flat_rows(small_m), _flat_rows(small_v),
                       "adamw_small")
    small_delta, small_new_m, small_new_v = [_unflatten(u.reshape(-1), small_shapes) for u in small_upd]

    def ordered(small, big):
        (n_pre, n_post, bias, ln_g, ln_b, a_conv, b_conv, o_pre, o_post, cb, cs) = small
        (w_in, w_out, ow_in, cw, ow_out) = big
        return [n_pre, n_post, w_in[None], a_conv[None], b_conv[None], bias, ln_g, ln_b, w_out[None], o_pre, o_post,
                ow_in[None], cw.reshape(1, N_GROUPS, gq, gc), cb[None], cs, ow_out[None]]

    grads = ordered(small_g, big_g)
    deltas = ordered(small_delta, [u[0] for u in big_upd])
    new_m = ordered(small_new_m, [u[1] for u in big_upd])
    new_v = ordered(small_new_v, [u[2] for u in big_upd])
    return (loss, grad_x.reshape(1, s, d), *grads, *deltas, *new_m, *new_v)
```

```python
import functools

import jax
import jax.numpy as jnp
from jax import lax
from jax.experimental import pallas as pl
from jax.experimental.pallas import tpu as pltpu

F32 = jnp.float32
BF16 = jnp.bfloat16
MESH = pl.DeviceIdType.MESH

EPS = 1e-6
CONV_A = 3
CONV_B = 31
POOL_WINDOWS = (2, 4, 8, 16)
N_GROUPS = len(POOL_WINDOWS)
N_SHARDS = 4
N_DEVICES = 8
ADAM_LR = 0.001
ADAM_B1 = 0.9
ADAM_B2 = 0.999
ADAM_EPS = 1e-08
ADAM_WD = 0.01
ADAM_STEP = 10

LANES = 128
SUBLANES_BF16 = 16
HALO_A = 8
HALO_B = 32
HALO_P = 16
SHIFTS = 8
ROW_BLOCK = 32
LANE_BLOCK = 256
LANE_PASS = 256
VMEM_LIMIT = 56 * 1024 * 1024
TN_ACC_BYTES = 8 * 1024 * 1024
EVEN_BWD_ROWS = 128


def _row_tile(n, pref):
    t = max(min(n, pref) // SUBLANES_BF16, 1) * SUBLANES_BF16
    while t > SUBLANES_BF16 and (n % t or t % SUBLANES_BF16):
        t -= SUBLANES_BF16
    assert n % t == 0, (n, pref)
    return t


def _col_chunk(n, pref):
    t = (min(n, pref) // LANES) * LANES
    while t > LANES and n % t:
        t -= LANES
    assert t >= LANES and n % t == 0, (n, pref)
    return t


def _params(*sem):
    return pltpu.CompilerParams(dimension_semantics=tuple(sem) if sem else None, vmem_limit_bytes=VMEM_LIMIT)


ANY = pl.BlockSpec(memory_space=pl.ANY)


class _Comm:
    def __init__(self, srcs, bufs, n_sems, phases, finish):
        self.srcs, self.bufs, self.n_sems, self.phases, self.finish = list(srcs), list(bufs), n_sems, phases, finish


def _call(body, *, name, grid, in_specs, out_specs, out_shape, args, scratch_shapes=(), comm=None):
    params = _params(*(("arbitrary",) * len(grid)))
    if comm is None:
        out = pl.pallas_call(body, name=name, grid=grid, in_specs=in_specs, out_specs=out_specs, out_shape=out_shape,
                             scratch_shapes=scratch_shapes, compiler_params=params)(*args)
        return list(out), []
    n_in, n_out, n_scr = len(in_specs), len(out_specs), len(scratch_shapes)
    ns, nb = len(comm.srcs), len(comm.bufs)
    total = 1
    for size in grid:
        total *= size

    def fused(*refs):
        ins, srcs = refs[:n_in], refs[n_in:n_in + ns]
        at = n_in + ns + nb
        outs, bufs = refs[at:at + n_out], refs[at + n_out:at + n_out + nb]
        scratch = refs[at + n_out + nb:at + n_out + nb + n_scr]
        send_sems, recv_sems = refs[-2:]
        step = 0
        for axis, size in enumerate(grid):
            step = step * size + pl.program_id(axis)
        for when, fn in comm.phases:
            pl.when(step == when)(functools.partial(fn, srcs, bufs, send_sems, recv_sems))
        body(*ins, *outs, *scratch)
        pl.when(step == total - 1)(functools.partial(comm.finish, srcs, bufs, send_sems, recv_sems))

    out = pl.pallas_call(
        fused, name=name, grid=grid,
        in_specs=list(in_specs) + [ANY] * (ns + nb), out_specs=list(out_specs) + [ANY] * nb,
        out_shape=list(out_shape) + [jax.ShapeDtypeStruct(b.shape, b.dtype) for b in comm.bufs],
        input_output_aliases={n_in + ns + i: n_out + i for i in range(nb)},
        scratch_shapes=list(scratch_shapes) + [pltpu.SemaphoreType.DMA((comm.n_sems,))] * 2,
        compiler_params=params,
    )(*args, *comm.srcs, *comm.bufs)
    return list(out[:n_out]), list(out[n_out:])


def _comm_only(comm, name):
    ns, nb = len(comm.srcs), len(comm.bufs)

    def body(*refs):
        srcs, bufs = refs[:ns], refs[ns + nb:ns + 2 * nb]
        send_sems, recv_sems = refs[-2:]
        for _, fn in comm.phases:
            fn(srcs, bufs, send_sems, recv_sems)
        comm.finish(srcs, bufs, send_sems, recv_sems)

    return pl.pallas_call(
        body, name=name, in_specs=[ANY] * (ns + nb), out_specs=[ANY] * nb,
        out_shape=[jax.ShapeDtypeStruct(b.shape, b.dtype) for b in comm.bufs],
        input_output_aliases={ns + i: i for i in range(nb)},
        scratch_shapes=[pltpu.SemaphoreType.DMA((comm.n_sems,))] * 2,
    )(*comm.srcs, *comm.bufs)


def _sigmoid(v):
    return jax.nn.sigmoid(v)


def _dsilu(v, s):
    return s * (1.0 + v * (1.0 - s))


def _mean_last(v):
    return jnp.mean(v, axis=-1, keepdims=True)


def _sum_rows(v):
    return jnp.sum(v, axis=0, keepdims=True)


def _norm_matmul(x, gain, w_sm, name):
    s, d = x.shape
    n_sh, _, ns = w_sm.shape
    tm = _row_tile(s, 1024)

    def body(x_ref, g_ref, w_ref, p_ref, h_ref):
        @pl.when(pl.program_id(1) == 0)
        def _():
            xv = x_ref[...]
            r = lax.rsqrt(_mean_last(xv * xv) + EPS)
            h_ref[...] = (xv * r * g_ref[...]).astype(BF16)

        p_ref[...] = jnp.dot(h_ref[...], w_ref[0], preferred_element_type=F32)

    return _call(
        body, name=name, grid=(s // tm, n_sh),
        in_specs=[pl.BlockSpec((tm, d), lambda i, j: (i, 0)),
                  pl.BlockSpec((1, d), lambda i, j: (0, 0)),
                  pl.BlockSpec((1, d, ns), lambda i, j: (j, 0, 0))],
        out_specs=[pl.BlockSpec((tm, ns), lambda i, j: (i, j)),
                   pl.BlockSpec((tm, d), lambda i, j: (i, 0))],
        out_shape=[jax.ShapeDtypeStruct((s, n_sh * ns), F32), jax.ShapeDtypeStruct((s, d), BF16)],
        args=(x, gain, w_sm))[0]


def _matmul_post_loss(u, w, x_res, gain, target, name):
    s, k = u.shape
    d = w.shape[1]
    tm = _row_tile(s, 256)

    def body(u_ref, w_ref, x_ref, g_ref, t_ref, dy_ref, dout_ref, dg_ref, loss_ref):
        @pl.when(pl.program_id(0) == 0)
        def _():
            dg_ref[...] = jnp.zeros_like(dg_ref)
            loss_ref[...] = jnp.zeros_like(loss_ref)

        y = jnp.dot(u_ref[...], w_ref[...], preferred_element_type=F32)
        r = lax.rsqrt(_mean_last(y * y) + EPS)
        n = y * r
        g = g_ref[...]
        err = x_ref[...] + n * g - t_ref[...]
        loss_ref[...] += 0.5 * jnp.sum(_mean_last(err * err))
        dout = err * (1.0 / d)
        dout_ref[...] = dout
        dg_ref[...] += _sum_rows(dout * n)
        dn = dout * g
        dy_ref[...] = (r * (dn - n * _mean_last(dn * n))).astype(BF16)

    return pl.pallas_call(
        body, name=name, grid=(s // tm,),
        in_specs=[pl.BlockSpec((tm, k), lambda i: (i, 0)),
                  pl.BlockSpec((k, d), lambda i: (0, 0)),
                  pl.BlockSpec((tm, d), lambda i: (i, 0)),
                  pl.BlockSpec((1, d), lambda i: (0, 0)),
                  pl.BlockSpec((tm, d), lambda i: (i, 0))],
        out_specs=[pl.BlockSpec((tm, d), lambda i: (i, 0)),
                   pl.BlockSpec((tm, d), lambda i: (i, 0)),
                   pl.BlockSpec((1, d), lambda i: (0, 0)),
                   pl.BlockSpec((8, LANES), lambda i: (0, 0))],
        out_shape=[jax.ShapeDtypeStruct((s, d), BF16), jax.ShapeDtypeStruct((s, d), F32),
                   jax.ShapeDtypeStruct((1, d), F32), jax.ShapeDtypeStruct((8, LANES), F32)],
        compiler_params=_params("arbitrary"),
    )(u, w, x_res, gain, target)


def _matmul_nt(a, w_sm, name, comm=None):
    s, ncols = a.shape
    n_sh, r, ns = w_sm.shape
    assert ncols == n_sh * ns
    tm = _row_tile(s, 1024)
    nc = _col_chunk(ns, 1792)
    per = ns // nc
    steps = n_sh * per

    def body(a_ref, w_ref, o_ref):
        part = lax.dot_general(a_ref[...], w_ref[0], (((1,), (1,)), ((), ())), preferred_element_type=F32)

        @pl.when(pl.program_id(1) == 0)
        def _():
            o_ref[...] = part

        @pl.when(pl.program_id(1) > 0)
        def _():
            o_ref[...] += part

    out, bufs = _call(
        body, name=name, grid=(s // tm, steps),
        in_specs=[pl.BlockSpec((tm, nc), lambda i, j: (i, j)),
                  pl.BlockSpec((1, r, nc), lambda i, j: (j // per, 0, j % per))],
        out_specs=[pl.BlockSpec((tm, r), lambda i, j: (i, 0))],
        out_shape=[jax.ShapeDtypeStruct((s, r), F32)],
        args=(a, w_sm), comm=comm)
    return out[0], bufs


def _matmul_tn(a, b, n_sh, name):
    s, k = a.shape
    n = b.shape[1]
    ns = n // n_sh
    tk = _col_chunk(k, TN_ACC_BYTES // (4 * ns))
    ts = _row_tile(s, 2048)
    n_s = s // ts

    def body(a_ref, b_ref, o_ref, acc_ref):
        part = lax.dot_general(a_ref[...], b_ref[...], (((0,), (0,)), ((), ())), preferred_element_type=F32)

        @pl.when(pl.program_id(2) == 0)
        def _():
            acc_ref[...] = part

        @pl.when(pl.program_id(2) > 0)
        def _():
            acc_ref[...] += part

        @pl.when(pl.program_id(2) == n_s - 1)
        def _():
            o_ref[0] = acc_ref[...].astype(BF16)

    return pl.pallas_call(
        body, name=name, grid=(n_sh, k // tk, n_s),
        in_specs=[pl.BlockSpec((ts, tk), lambda j, i, t: (t, i)),
                  pl.BlockSpec((ts, ns), lambda j, i, t: (t, j))],
        out_specs=pl.BlockSpec((1, tk, ns), lambda j, i, t: (j, i, 0)),
        out_shape=jax.ShapeDtypeStruct((n_sh, k, ns), BF16),
        scratch_shapes=[pltpu.VMEM((tk, ns), F32)],
        compiler_params=_params("arbitrary", "arbitrary", "arbitrary"),
    )(a, b)


def _matmul_tn_quarter(a, b, n_sh, quarter, name, comm=None):
    s, k = a.shape
    ns = b.shape[1] // n_sh
    tk = k // 4
    ts = _row_tile(s, 2048)
    n_s = s // ts

    def body(a_ref, b_ref, o_ref, acc_ref):
        part = lax.dot_general(a_ref[...], b_ref[...], (((0,), (0,)), ((), ())), preferred_element_type=F32)

        @pl.when(pl.program_id(2) == 0)
        def _():
            acc_ref[...] = part

        @pl.when(pl.program_id(2) > 0)
        def _():
            acc_ref[...] += part

        @pl.when(pl.program_id(2) == n_s - 1)
        def _():
            o_ref[0, 0] = acc_ref[...].astype(BF16)

    out, bufs = _call(
        body, name=name, grid=(n_sh, 2, n_s),
        in_specs=[pl.BlockSpec((ts, tk), lambda j, h, t: (t, 2 * h + quarter)),
                  pl.BlockSpec((ts, ns), lambda j, h, t: (t, j))],
        out_specs=[pl.BlockSpec((1, 1, tk, ns), lambda j, h, t: (j, h, 0, 0))],
        out_shape=[jax.ShapeDtypeStruct((n_sh, 2, tk, ns), BF16)],
        scratch_shapes=[pltpu.VMEM((tk, ns), F32)],
        args=(a, b), comm=comm)
    return out[0], bufs


def _norm_bwd(dh, x, gain, dres, name, post=None):
    s, d = x.shape
    tm = _row_tile(s, 256)
    with_post = post is not None

    def rms_bwd(dout, v, g):
        r = lax.rsqrt(_mean_last(v * v) + EPS)
        n = v * r
        dn = dout * g
        return r * (dn - n * _mean_last(dn * n)), _sum_rows(dout * n)

    def body(*refs):
        if with_post:
            dh_ref, x_ref, g_ref, dres_ref, y_ref, gp_ref, dx_ref, dg_ref, dy_ref, dgp_ref = refs
        else:
            dh_ref, x_ref, g_ref, dres_ref, dx_ref, dg_ref = refs

        @pl.when(pl.program_id(0) == 0)
        def _():
            dg_ref[...] = jnp.zeros_like(dg_ref)
            if with_post:
                dgp_ref[...] = jnp.zeros_like(dgp_ref)

        dv, dg = rms_bwd(dh_ref[...], x_ref[...], g_ref[...])
        dx = dres_ref[...] + dv
        dx_ref[...] = dx
        dg_ref[...] += dg
        if with_post:
            dy, dgp = rms_bwd(dx, y_ref[...], gp_ref[...])
            dy_ref[...] = dy.astype(BF16)
            dgp_ref[...] += dgp

    row = pl.BlockSpec((tm, d), lambda i: (i, 0))
    vec = pl.BlockSpec((1, d), lambda i: (0, 0))
    in_specs = [row, row, vec, row]
    out_specs = [row, vec]
    out_shape = [jax.ShapeDtypeStruct((s, d), F32), jax.ShapeDtypeStruct((1, d), F32)]
    args = [dh, x, gain, dres]
    if with_post:
        in_specs += [row, vec]
        out_specs += [row, vec]
        out_shape += [jax.ShapeDtypeStruct((s, d), BF16), jax.ShapeDtypeStruct((1, d), F32)]
        args += list(post)
    return _call(body, name=name, grid=(s // tm,), in_specs=in_specs, out_specs=out_specs, out_shape=out_shape,
                 args=args)[0]


def _fill_shifted_down(sh, rows):
    for b in range(1, SHIFTS):
        sh[b, SHIFTS:rows, :] = sh[0, SHIFTS - b:rows - b, :]


def _fill_shifted_up(sh, rows):
    for b in range(1, SHIFTS):
        sh[b, 0:rows - SHIFTS, :] = sh[0, b:rows - SHIFTS + b, :]


def _for_blocks(ts, w, fn):
    lb = min(LANE_BLOCK, w)
    for l0 in range(0, w, lb):
        def rows(rb, carry, l0=l0):
            fn(pl.multiple_of(rb * ROW_BLOCK, ROW_BLOCK), slice(l0, l0 + lb))
            return carry

        lax.fori_loop(0, ts // ROW_BLOCK, rows, 0)


TAP_SPAN = SHIFTS * ((CONV_B - 1) // SHIFTS)
WINDOW = ROW_BLOCK + TAP_SPAN


def _taps_of(b):
    return [(a, SHIFTS * a + b) for a in range((CONV_B - 1 - b) // SHIFTS + 1)]


def _conv31(sh, base, step, wt_ref, bias_ref, out_ref, ts, w):
    low = min(0, step * (TAP_SPAN // SHIFTS))

    def block(r0, lanes):
        acc = [jnp.zeros((SHIFTS, lanes.stop - lanes.start), F32) for _ in range(ROW_BLOCK // SHIFTS)]
        for b in range(SHIFTS):
            window = sh[b, pl.ds(pl.multiple_of(r0 + (base + low), SHIFTS), WINDOW), lanes]
            for a, j in (_taps_of(b) if step > 0 else reversed(_taps_of(b))):
                at = step * a - low
                wt = wt_ref[CONV_B - 1 - j, :, lanes]
                acc = [v + wt * window[at + SHIFTS * r:at + SHIFTS * (r + 1), :] for r, v in enumerate(acc)]
        for r, v in enumerate(acc):
            if bias_ref is not None:
                v = v + bias_ref[:, lanes]
            out_ref[pl.ds(pl.multiple_of(r0 + SHIFTS * r, SHIFTS), SHIFTS), lanes] = v

    _for_blocks(ts, w, block)


def _conv31_weight_grad(d_sh, x_sh, wacc, ts, w):
    def block(r0, lanes):
        d = d_sh[0, pl.ds(r0, ROW_BLOCK), lanes]
        for b in range(SHIFTS):
            window = x_sh[b, pl.ds(pl.multiple_of(r0 + (HALO_B - TAP_SPAN), SHIFTS), WINDOW), lanes]
            for a, j in _taps_of(b):
                at = TAP_SPAN - SHIFTS * a
                prod = d * window[at:at + ROW_BLOCK, :]
                part = prod[0:SHIFTS, :]
                for q in range(1, ROW_BLOCK // SHIFTS):
                    part = part + prod[q * SHIFTS:(q + 1) * SHIFTS, :]
                wacc[CONV_B - 1 - j, :, lanes] += part

    _for_blocks(ts, w, block)


def _even_forward_tile(p_ref, halo_ref, first, a_conv_ref, b_conv_ref, bias_ref, lng_ref, lnb_ref, qbuf, ysh, y1buf,
                       wb, w, ts):
    @pl.when(pl.program_id(0) == 0)
    def _():
        for k in range(CONV_B):
            wb[k] = jnp.broadcast_to(b_conv_ref[k:k + 1, :], (SHIFTS, w))

    def col(ref, k, rows=slice(None)):
        return ref[rows, k * w:(k + 1) * w]

    a_x, a_b, a_c, a_z = col(p_ref, 0), col(p_ref, 1), col(p_ref, 2), col(p_ref, 3)
    b_val, b_gate, b_z = col(p_ref, 4), col(p_ref, 5), col(p_ref, 6)
    keep = jnp.where(first, 0.0, 1.0)

    rows_a = slice(HALO_B - HALO_A, HALO_B)
    qbuf[0:HALO_A, :] = col(halo_ref, 2, rows_a) * col(halo_ref, 0, rows_a) * keep
    qbuf[HALO_A:HALO_A + ts, :] = a_c * a_x
    cq = jnp.zeros((ts, w), F32)
    for j in range(CONV_A):
        cq = cq + a_conv_ref[CONV_A - 1 - j:CONV_A - j, :] * qbuf[HALO_A - j:HALO_A - j + ts, :]
    ya = a_b * cq

    ysh[0, 0:HALO_B, :] = col(halo_ref, 4) * _sigmoid(col(halo_ref, 5)) * keep
    ysh[0, HALO_B:HALO_B + ts, :] = b_val * _sigmoid(b_gate)
    _fill_shifted_down(ysh, HALO_B + ts)
    _conv31(ysh, HALO_B, -SHIFTS, wb, bias_ref, y1buf, ts, w)
    yb1 = y1buf[...]
    xc = yb1 - _mean_last(yb1)
    rstd = lax.rsqrt(_mean_last(xc * xc) + EPS)
    xhat = xc * rstd
    yb2 = xhat * lng_ref[...] + lnb_ref[...]
    return dict(a_x=a_x, a_b=a_b, a_c=a_c, a_z=a_z, b_val=b_val, b_gate=b_gate, b_z=b_z,
                cq=cq, ya=ya, rstd=rstd, xhat=xhat, yb2=yb2)


def _even_specs(s, w, ts):
    tile = pl.BlockSpec((ts, 7 * w), lambda i: (i, 0))
    halo = pl.BlockSpec((HALO_B, 7 * w), lambda i: (jnp.maximum(i * (ts // HALO_B) - 1, 0), 0))
    return tile, halo


def _small_specs(shapes, index=lambda i: (0, 0)):
    return [pl.BlockSpec(sh, index) for sh in shapes]


def _even_mixer_fwd(p, w_out, x_res, gain, a_conv, b_conv, bias, ln_g, ln_b, name, comm=None):
    s, d = x_res.shape
    w = p.shape[1] // 7
    ts = _row_tile(s, 128)
    assert ts % HALO_B == 0

    def body(p_ref, halo_ref, wout_ref, x_ref, g_ref, ac_ref, bc_ref, bias_ref, lng_ref, lnb_ref,
             u_ref, xn_ref, y_ref, qbuf, ysh, y1buf, wb):
        first = pl.program_id(0) == 0
        f = _even_forward_tile(p_ref, halo_ref, first, ac_ref, bc_ref, bias_ref, lng_ref, lnb_ref, qbuf, ysh, y1buf,
                               wb, w, ts)
        yb3 = f["yb2"] * _sigmoid(f["yb2"])
        u_a = (f["ya"] * (f["a_z"] * _sigmoid(f["a_z"]))).astype(BF16)
        u_b = (yb3 * (f["b_z"] * _sigmoid(f["b_z"]))).astype(BF16)
        u_ref[:, 0:w] = u_a
        u_ref[:, w:2 * w] = u_b
        y = (jnp.dot(u_a, wout_ref[0:w, :], preferred_element_type=F32)
             + jnp.dot(u_b, wout_ref[w:2 * w, :], preferred_element_type=F32))
        r = lax.rsqrt(_mean_last(y * y) + EPS)
        y_ref[...] = y
        xn_ref[...] = x_ref[...] + (y * r) * g_ref[...]

    tile, halo = _even_specs(s, w, ts)
    row = pl.BlockSpec((ts, d), lambda i: (i, 0))
    return _call(
        body, name=name, grid=(s // ts,),
        in_specs=[tile, halo, pl.BlockSpec((2 * w, d), lambda i: (0, 0)), row, pl.BlockSpec((1, d), lambda i: (0, 0))]
        + _small_specs([(CONV_A, w), (CONV_B, w), (1, w), (1, w), (1, w)]),
        out_specs=[pl.BlockSpec((ts, 2 * w), lambda i: (i, 0)), row, row],
        out_shape=[jax.ShapeDtypeStruct((s, 2 * w), BF16), jax.ShapeDtypeStruct((s, d), F32),
                   jax.ShapeDtypeStruct((s, d), F32)],
        scratch_shapes=[pltpu.VMEM((HALO_A + ts, w), F32), pltpu.VMEM((SHIFTS, HALO_B + ts, w), F32),
                        pltpu.VMEM((ts, w), F32), pltpu.VMEM((CONV_B, SHIFTS, w), F32)],
        args=(p, p, w_out, x_res, gain, a_conv, b_conv, bias, ln_g, ln_b), comm=comm)


def _even_mixer_bwd(p, du, a_conv, b_conv, bias, ln_g, ln_b, name, comm=None):
    s = p.shape[0]
    w = p.shape[1] // 7
    ts = _row_tile(s, EVEN_BWD_ROWS)
    nt = s // ts
    assert ts % HALO_B == 0

    def body(p_ref, halo_ref, du_ref, ac_ref, bc_ref, bias_ref, lng_ref, lnb_ref,
             dp_ref, dac_ref, dbc_ref, dbias_ref, dlng_ref, dlnb_ref,
             qbuf, ysh, y1buf, dqbuf, dsh, dy0buf, wacc, carry_dq, carry_dy, wb):
        step = pl.program_id(0)
        first = step == nt - 1

        @pl.when(step == 0)
        def _():
            for ref in (dac_ref, dbias_ref, dlng_ref, dlnb_ref, wacc, carry_dq, carry_dy):
                ref[...] = jnp.zeros_like(ref)

        @pl.when(step == 0)
        def _():
            for k in range(CONV_B):
                wb[k] = jnp.broadcast_to(bc_ref[k:k + 1, :], (SHIFTS, w))

        keep = jnp.where(first, 0.0, 1.0)
        lane_blocks = [slice(l0, l0 + min(LANE_PASS, w)) for l0 in range(0, w, min(LANE_PASS, w))]

        def col(ref, k, lanes, rows=slice(None)):
            return ref[rows, k * w + lanes.start:k * w + lanes.stop]


        rows_a = slice(HALO_B - HALO_A, HALO_B)
        for lanes in lane_blocks:
            a_x, a_b, a_c, a_z = (col(p_ref, k, lanes) for k in range(4))
            qbuf[0:HALO_A, lanes] = col(halo_ref, 2, lanes, rows_a) * col(halo_ref, 0, lanes, rows_a) * keep
            qbuf[HALO_A:HALO_A + ts, lanes] = a_c * a_x
            cq = jnp.zeros((ts, lanes.stop - lanes.start), F32)
            for j in range(CONV_A):
                cq = cq + ac_ref[CONV_A - 1 - j:CONV_A - j, lanes] * qbuf[HALO_A - j:HALO_A - j + ts, lanes]
            s_az = _sigmoid(a_z)
            du_a = du_ref[:, lanes]
            d_ya = du_a * (a_z * s_az)
            dp_ref[:, 3 * w + lanes.start:3 * w + lanes.stop] = (du_a * (a_b * cq) * _dsilu(a_z, s_az)).astype(BF16)
            dp_ref[:, 1 * w + lanes.start:1 * w + lanes.stop] = (d_ya * cq).astype(BF16)
            d_cq = d_ya * a_b
            dqbuf[0:ts, lanes] = d_cq
            dqbuf[ts:ts + HALO_A, lanes] = carry_dq[:, lanes]
            carry_dq[:, lanes] = d_cq[0:HALO_A, :]
            d_q = jnp.zeros_like(cq)
            for o in range(CONV_A):
                d_q = d_q + ac_ref[CONV_A - 1 - o:CONV_A - o, lanes] * dqbuf[o:o + ts, lanes]
            for j in range(CONV_A):
                k = CONV_A - 1 - j
                dac_ref[k:k + 1, lanes] += _sum_rows(d_cq * qbuf[HALO_A - j:HALO_A - j + ts, lanes])
            dp_ref[:, 2 * w + lanes.start:2 * w + lanes.stop] = (d_q * a_x).astype(BF16)
            dp_ref[:, 0 * w + lanes.start:0 * w + lanes.stop] = (d_q * a_c).astype(BF16)
            ysh[0, 0:HALO_B, lanes] = col(halo_ref, 4, lanes) * _sigmoid(col(halo_ref, 5, lanes)) * keep
            ysh[0, HALO_B:HALO_B + ts, lanes] = col(p_ref, 4, lanes) * _sigmoid(col(p_ref, 5, lanes))
        _fill_shifted_down(ysh, HALO_B + ts)
        _conv31(ysh, HALO_B, -SHIFTS, wb, bias_ref, y1buf, ts, w)

        total = jnp.zeros((ts, 1), F32)
        for lanes in lane_blocks:
            total = total + jnp.sum(y1buf[:, lanes], axis=-1, keepdims=True)
        mu = total * (1.0 / w)
        total = jnp.zeros((ts, 1), F32)
        for lanes in lane_blocks:
            xc = y1buf[:, lanes] - mu
            total = total + jnp.sum(xc * xc, axis=-1, keepdims=True)
        rstd = lax.rsqrt(total * (1.0 / w) + EPS)

        sum_dxh = jnp.zeros((ts, 1), F32)
        sum_dxh_xhat = jnp.zeros((ts, 1), F32)
        for lanes in lane_blocks:
            xhat = (y1buf[:, lanes] - mu) * rstd
            yb2 = xhat * lng_ref[:, lanes] + lnb_ref[:, lanes]
            b_z = col(p_ref, 6, lanes)
            s_bz, s_y2 = _sigmoid(b_z), _sigmoid(yb2)
            du_b = du_ref[:, w + lanes.start:w + lanes.stop]
            dp_ref[:, 6 * w + lanes.start:6 * w + lanes.stop] = (du_b * (yb2 * s_y2) * _dsilu(b_z, s_bz)).astype(BF16)
            d_yb2 = du_b * (b_z * s_bz) * _dsilu(yb2, s_y2)
            dlng_ref[:, lanes] += _sum_rows(d_yb2 * xhat)
            dlnb_ref[:, lanes] += _sum_rows(d_yb2)
            d_xh = d_yb2 * lng_ref[:, lanes]
            sum_dxh = sum_dxh + jnp.sum(d_xh, axis=-1, keepdims=True)
            sum_dxh_xhat = sum_dxh_xhat + jnp.sum(d_xh * xhat, axis=-1, keepdims=True)
            y1buf[:, lanes] = xhat
            dy0buf[:, lanes] = d_xh
        mean_dxh = sum_dxh * (1.0 / w)
        mean_dxh_xhat = sum_dxh_xhat * (1.0 / w)

        for lanes in lane_blocks:
            d_yb1 = rstd * (dy0buf[:, lanes] - mean_dxh - y1buf[:, lanes] * mean_dxh_xhat)
            dbias_ref[:, lanes] += _sum_rows(d_yb1)
            dsh[0, 0:ts, lanes] = d_yb1
            dsh[0, ts:ts + HALO_B, lanes] = carry_dy[:, lanes]
            carry_dy[:, lanes] = d_yb1[0:HALO_B, :]
        _fill_shifted_up(dsh, ts + HALO_B)
        _conv31(dsh, 0, SHIFTS, wb, None, dy0buf, ts, w)
        _conv31_weight_grad(dsh, ysh, wacc, ts, w)

        @pl.when(step == nt - 1)
        def _():
            for k in range(CONV_B):
                dbc_ref[k:k + 1, :] = _sum_rows(wacc[k])

        for lanes in lane_blocks:
            d_yb0 = dy0buf[:, lanes]
            s_g = _sigmoid(col(p_ref, 5, lanes))
            dp_ref[:, 4 * w + lanes.start:4 * w + lanes.stop] = (d_yb0 * s_g).astype(BF16)
            dp_ref[:, 5 * w + lanes.start:5 * w + lanes.stop] = (
                d_yb0 * col(p_ref, 4, lanes) * s_g * (1.0 - s_g)).astype(BF16)

    rev = lambda i: (nt - 1 - i, 0)
    tile = pl.BlockSpec((ts, 7 * w), rev)
    halo = pl.BlockSpec((HALO_B, 7 * w), lambda i: (jnp.maximum((nt - 1 - i) * (ts // HALO_B) - 1, 0), 0))
    small = [(CONV_A, w), (CONV_B, w), (1, w), (1, w), (1, w)]
    return _call(
        body, name=name, grid=(nt,),
        in_specs=[tile, halo, pl.BlockSpec((ts, 2 * w), rev)] + _small_specs(small),
        out_specs=[pl.BlockSpec((ts, 7 * w), rev)] + _small_specs(small),
        out_shape=[jax.ShapeDtypeStruct((s, 7 * w), BF16)] + [jax.ShapeDtypeStruct(sh, F32) for sh in small],
        scratch_shapes=[pltpu.VMEM((HALO_A + ts, w), F32), pltpu.VMEM((SHIFTS, HALO_B + ts, w), F32),
                        pltpu.VMEM((ts, w), F32),
                        pltpu.VMEM((ts + HALO_A, w), F32), pltpu.VMEM((SHIFTS, ts + HALO_B, w), F32),
                        pltpu.VMEM((ts, w), F32), pltpu.VMEM((CONV_B, SHIFTS, w), F32),
                        pltpu.VMEM((HALO_A, w), F32), pltpu.VMEM((HALO_B, w), F32),
                        pltpu.VMEM((CONV_B, SHIFTS, w), F32)],
        args=(p, p, du, a_conv, b_conv, bias, ln_g, ln_b), comm=comm)


def _pool_forward_tile(p_ref, halo_ref, first, tile_index, cw_ref, cb_ref, cs_ref, vbuf, c, gc, ts):
    vbuf[0:HALO_P, :] = halo_ref[...] * jnp.where(first, 0.0, 1.0)
    vbuf[HALO_P:HALO_P + ts, :] = p_ref[:, 0:c]
    pos = tile_index * ts + lax.broadcasted_iota(jnp.int32, (ts, 1), 0) + 1
    pooled, inv, gout = [], [], []
    for g, win in enumerate(POOL_WINDOWS):
        cols = slice(g * gc, (g + 1) * gc)
        acc = jnp.zeros((ts, gc), F32)
        for j in range(win):
            acc = acc + vbuf[HALO_P - j:HALO_P - j + ts, cols]
        inv_g = 1.0 / jnp.minimum(pos, win).astype(F32)
        pooled_g = (acc * inv_g - p_ref[:, cols]).astype(BF16)
        pooled.append(pooled_g)
        inv.append(inv_g)
        gout.append(jnp.dot(pooled_g, cw_ref[g], preferred_element_type=F32) + cb_ref[:, cols])
    return pooled, inv, gout


def _odd_mixer_fwd(p, cw, cb, cs, name):
    s = p.shape[0]
    c = p.shape[1] // 2
    gc = c // N_GROUPS
    ts = _row_tile(s, 256)

    def body(p_ref, halo_ref, cw_ref, cb_ref, cs_ref, u_ref, vbuf):
        i = pl.program_id(0)
        _, _, gout = _pool_forward_tile(p_ref, halo_ref, i == 0, i, cw_ref, cb_ref, cs_ref, vbuf, c, gc, ts)
        for g in range(N_GROUPS):
            cols = slice(g * gc, (g + 1) * gc)
            z = p_ref[:, c + g * gc:c + (g + 1) * gc]
            u_ref[:, cols] = (gout[g] * cs_ref[:, cols] * (z * _sigmoid(z))).astype(BF16)

    return pl.pallas_call(
        body, name=name, grid=(s // ts,),
        in_specs=[pl.BlockSpec((ts, 2 * c), lambda i: (i, 0)),
                  pl.BlockSpec((HALO_P, c), lambda i: (jnp.maximum(i * (ts // HALO_P) - 1, 0), 0)),
                  pl.BlockSpec((N_GROUPS, gc, gc), lambda i: (0, 0, 0)),
                  pl.BlockSpec((1, c), lambda i: (0, 0)), pl.BlockSpec((1, c), lambda i: (0, 0))],
        out_specs=pl.BlockSpec((ts, c), lambda i: (i, 0)),
        out_shape=jax.ShapeDtypeStruct((s, c), BF16),
        scratch_shapes=[pltpu.VMEM((HALO_P + ts, c), F32)],
        compiler_params=_params("arbitrary"),
    )(p, p, cw, cb, cs)


def _odd_mixer_bwd(p, du, cw, cb, cs, name):
    s = p.shape[0]
    c = p.shape[1] // 2
    gc = c // N_GROUPS
    ts = _row_tile(s, 256)
    nt = s // ts

    def body(p_ref, halo_ref, du_ref, cw_ref, cb_ref, cs_ref, dp_ref, dcw_ref, dcb_ref, dcs_ref, vbuf, ebuf, carry_e):
        step = pl.program_id(0)
        tile_index = nt - 1 - step

        @pl.when(step == 0)
        def _():
            for ref in (dcw_ref, dcb_ref, dcs_ref, carry_e):
                ref[...] = jnp.zeros_like(ref)

        pooled, inv, gout = _pool_forward_tile(p_ref, halo_ref, tile_index == 0, tile_index, cw_ref, cb_ref, cs_ref,
                                               vbuf, c, gc, ts)
        ebuf[ts:ts + HALO_P, :] = carry_e[...]
        for g, win in enumerate(POOL_WINDOWS):
            cols = slice(g * gc, (g + 1) * gc)
            z = p_ref[:, c + g * gc:c + (g + 1) * gc]
            sz = _sigmoid(z)
            du_g = du_ref[:, cols]
            scale = cs_ref[:, cols]
            d_y = du_g * (z * sz)
            dp_ref[:, c + g * gc:c + (g + 1) * gc] = (du_g * (gout[g] * scale) * _dsilu(z, sz)).astype(BF16)
            dcs_ref[:, cols] += _sum_rows(d_y * gout[g])
            d_gout = d_y * scale
            dcb_ref[:, cols] += _sum_rows(d_gout)
            d_gout_b = d_gout.astype(BF16)
            dcw_ref[g] += lax.dot_general(pooled[g], d_gout_b, (((0,), (0,)), ((), ())), preferred_element_type=F32)
            d_pool = lax.dot_general(d_gout_b, cw_ref[g], (((1,), (1,)), ((), ())), preferred_element_type=F32)
            e = d_pool * inv[g]
            ebuf[0:ts, cols] = e
            d_v = -d_pool
            for o in range(win):
                d_v = d_v + ebuf[o:o + ts, cols]
            dp_ref[:, cols] = d_v.astype(BF16)
            carry_e[:, cols] = e[0:HALO_P, :]

    rev = lambda i: (nt - 1 - i, 0)
    small = [(N_GROUPS, gc, gc), (1, c), (1, c)]
    return pl.pallas_call(
        body, name=name, grid=(nt,),
        in_specs=[pl.BlockSpec((ts, 2 * c), rev),
                  pl.BlockSpec((HALO_P, c), lambda i: (jnp.maximum((nt - 1 - i) * (ts // HALO_P) - 1, 0), 0)),
                  pl.BlockSpec((ts, c), rev),
                  pl.BlockSpec((N_GROUPS, gc, gc), lambda i: (0, 0, 0)),
                  pl.BlockSpec((1, c), lambda i: (0, 0)), pl.BlockSpec((1, c), lambda i: (0, 0))],
        out_specs=[pl.BlockSpec((ts, 2 * c), rev),
                   pl.BlockSpec((N_GROUPS, gc, gc), lambda i: (0, 0, 0)),
                   pl.BlockSpec((1, c), lambda i: (0, 0)), pl.BlockSpec((1, c), lambda i: (0, 0))],
        out_shape=[jax.ShapeDtypeStruct((s, 2 * c), BF16)] + [jax.ShapeDtypeStruct(sh, F32) for sh in small],
        scratch_shapes=[pltpu.VMEM((HALO_P + ts, c), F32), pltpu.VMEM((ts + HALO_P, c), F32),
                        pltpu.VMEM((HALO_P, c), F32)],
        compiler_params=_params("arbitrary"),
    )(p, p, du, cw, cb, cs)


def _cast_into_slot(a, coords, name):
    r, cols = a.shape
    tr = _row_tile(r // 2, 256)
    per = r // 2 // tr

    def body(co_ref, a_ref, o_ref):
        o_ref[0, 0] = a_ref[...].astype(BF16)

    return pl.pallas_call(
        body, name=name,
        grid_spec=pltpu.PrefetchScalarGridSpec(
            num_scalar_prefetch=1, grid=(2, per),
            in_specs=[pl.BlockSpec((tr, cols), lambda h, i, co: (h * per + i, 0))],
            out_specs=pl.BlockSpec((1, 1, tr, cols), lambda h, i, co: (co[0], h, i, 0))),
        out_shape=jax.ShapeDtypeStruct((N_SHARDS, 2, r // 2, cols), BF16),
        compiler_params=_params("arbitrary", "arbitrary"),
    )(coords, a)


def _chip_sum(g, other, coords, name):
    n_sh, _, r2, cols = g.shape
    tr = _row_tile(r2, 256)

    def body(co_ref, g_ref, o_ref, sum_ref, mine_ref):
        v = (g_ref[0, 0].astype(F32) + o_ref[0].astype(F32)).astype(BF16)
        sum_ref[0] = v

        @pl.when(pl.program_id(1) == co_ref[0])
        def _():
            mine_ref[0] = v

    piece = pl.BlockSpec((1, tr, cols), lambda i, s, co: (s, i, 0))
    return pl.pallas_call(
        body, name=name,
        grid_spec=pltpu.PrefetchScalarGridSpec(
            num_scalar_prefetch=1, grid=(r2 // tr, n_sh),
            in_specs=[pl.BlockSpec((1, 1, tr, cols), lambda i, s, co: (s, co[1], i, 0)), piece],
            out_specs=[piece, pl.BlockSpec((1, tr, cols), lambda i, s, co: (co[0], i, 0))]),
        out_shape=[jax.ShapeDtypeStruct((n_sh, r2, cols), BF16)] * 2,
        compiler_params=_params("arbitrary", "arbitrary"),
    )(coords, g, other)


def _shard_sum(pieces, coords, name):
    n_sh, r2, cols = pieces.shape
    tr = _row_tile(r2, 256)

    def body(co_ref, p_ref, o_ref):
        acc = p_ref[0].astype(F32)
        for k in range(1, n_sh):
            acc = acc + p_ref[k].astype(F32)
        o_ref[0] = acc

    return pl.pallas_call(
        body, name=name,
        grid_spec=pltpu.PrefetchScalarGridSpec(
            num_scalar_prefetch=1, grid=(r2 // tr,),
            in_specs=[pl.BlockSpec((n_sh, tr, cols), lambda i, co: (0, i, 0))],
            out_specs=pl.BlockSpec((1, tr, cols), lambda i, co: (co[1], i, 0))),
        out_shape=jax.ShapeDtypeStruct((2, r2, cols), F32),
        compiler_params=_params("arbitrary"),
    )(coords, pieces)


def _sum_small(a, name):
    n, r, cols = a.shape

    def body(a_ref, o_ref):
        acc = a_ref[0]
        for k in range(1, n):
            acc = acc + a_ref[k]
        o_ref[...] = acc

    return pl.pallas_call(
        body, name=name,
        in_specs=[pl.BlockSpec((n, r, cols), lambda: (0, 0, 0))],
        out_specs=pl.BlockSpec((r, cols), lambda: (0, 0)),
        out_shape=jax.ShapeDtypeStruct((r, cols), F32),
        compiler_params=_params(),
    )(a)


def _adamw_step(w, g, m, v):
    m = ADAM_B1 * m + (1.0 - ADAM_B1) * g
    v = ADAM_B2 * v + (1.0 - ADAM_B2) * (g * g)
    m_hat = m / (1.0 - ADAM_B1 ** ADAM_STEP)
    v_hat = v / (1.0 - ADAM_B2 ** ADAM_STEP)
    return -ADAM_LR * (m_hat / (jnp.sqrt(v_hat) + ADAM_EPS) + ADAM_WD * w), m, v


def _adamw(w, g, m, v, name):
    r, cols = w.shape
    tr = _row_tile(r, 256) if r % SUBLANES_BF16 == 0 else r

    def body(w_ref, g_ref, m_ref, v_ref, d_ref, nm_ref, nv_ref):
        d_ref[...], nm_ref[...], nv_ref[...] = _adamw_step(w_ref[...], g_ref[...], m_ref[...], v_ref[...])

    blk = pl.BlockSpec((tr, cols), lambda i: (i, 0))
    return pl.pallas_call(
        body, name=name, grid=(r // tr,),
        in_specs=[blk] * 4, out_specs=[blk] * 3,
        out_shape=[jax.ShapeDtypeStruct((r, cols), F32)] * 3,
        compiler_params=_params("arbitrary"),
    )(w, g, m, v)


def _adamw_quarters(w, g_even, g_odd, m, v, name):
    r, cols = w.shape
    tr = _row_tile(r // 4, 256)
    per = r // 4 // tr

    def body(w_ref, ge_ref, go_ref, m_ref, v_ref, g_ref, d_ref, nm_ref, nv_ref):
        odd = (pl.program_id(0) // per) % 2 == 1
        g = jnp.where(odd, go_ref[0], ge_ref[0])
        g_ref[...] = g
        d_ref[...], nm_ref[...], nv_ref[...] = _adamw_step(w_ref[...], g, m_ref[...], v_ref[...])

    blk = pl.BlockSpec((tr, cols), lambda i: (i, 0))
    part = pl.BlockSpec((1, tr, cols), lambda i: (i // (2 * per), i % per, 0))
    return pl.pallas_call(
        body, name=name, grid=(r // tr,),
        in_specs=[blk, part, part, blk, blk], out_specs=[blk] * 4,
        out_shape=[jax.ShapeDtypeStruct((r, cols), F32)] * 4,
        compiler_params=_params("arbitrary"),
    )(w, g_even, g_odd, m, v)


def _place():
    x, y, c = lax.axis_index("x"), lax.axis_index("y"), lax.axis_index("c")
    other_chips = [(1 - x, y), (x, 1 - y), (1 - x, 1 - y)]
    return x, y, c, other_chips


def _chip(xy):
    return 2 * xy[0] + xy[1]


def _remote(src, dst, send_sem, recv_sem, to):
    return pltpu.make_async_remote_copy(src_ref=src, dst_ref=dst, send_sem=send_sem, recv_sem=recv_sem,
                                        device_id=to, device_id_type=MESH)


def _gather_ici(ctx, k, j, start):
    (x, y, c, chips), b, send, recv = ctx
    if j < 2:
        chip, to = (_chip((x, y)) if start else _chip(chips[j])), (*chips[j], c)
    else:
        chip = 2 * (x ^ c) + (y ^ (1 - c)) if start else _chip(chips[2])
        to = (x ^ (1 - c), y ^ c, c)
    blk = b[k].at[chip, c]
    return _remote(blk, blk, send.at[6 * k + j], recv.at[6 * k + j], to)


def _gather_d2d(ctx, k, j, start):
    (x, y, c, chips), b, send, recv = ctx
    blk = b[k].at[_chip(chips[j]), c if start else 1 - c]
    return _remote(blk, blk, send.at[6 * k + 3 + j], recv.at[6 * k + 3 + j], (x, y, 1 - c))


def _gather_small(ctx, n, j, start):
    (x, y, c, chips), b, send, recv = ctx
    blk = b[n].at[_chip((x, y)) if start else _chip(chips[j])]
    return _remote(blk, blk, send.at[6 * n + j], recv.at[6 * n + j], (*chips[j], c))


def _gather_neighbours_landed(ctx, k):
    for j in range(2):
        _gather_ici(ctx, k, j, False).wait_recv()
    _gather_ici(ctx, k, 2, True).start()
    for j in range(2):
        _gather_d2d(ctx, k, j, True).start()


def _gather_diagonal_landed(ctx, k):
    _gather_ici(ctx, k, 2, False).wait_recv()
    _gather_d2d(ctx, k, 2, True).start()


def _gather_comm(bufs, relay_at, forward_at):
    n = len(bufs)

    def start(srcs, b, send, recv):
        for k in range(n):
            for j in range(2):
                _gather_ici((_place(), b, send, recv), k, j, True).start()

    def relay(srcs, b, send, recv):
        for k in range(n):
            _gather_neighbours_landed((_place(), b, send, recv), k)

    def forward(srcs, b, send, recv):
        for k in range(n):
            _gather_diagonal_landed((_place(), b, send, recv), k)

    def finish(srcs, b, send, recv):
        ctx = (_place(), b, send, recv)
        for k in range(n):
            for j in range(3):
                _gather_d2d(ctx, k, j, False).wait_recv()
                _gather_ici(ctx, k, j, True).wait_send()
                _gather_d2d(ctx, k, j, True).wait_send()

    return _Comm([], bufs, 6 * n, [(0, start), (relay_at, relay), (forward_at, forward)], finish)


def _rmsnorm(x, gain, name):
    s, d = x.shape
    tm = _row_tile(s, 512)

    def body(x_ref, g_ref, h_ref):
        xv = x_ref[...]
        r = lax.rsqrt(_mean_last(xv * xv) + EPS)
        h_ref[...] = (xv * r * g_ref[...]).astype(BF16)

    return pl.pallas_call(
        body, name=name, grid=(s // tm,),
        in_specs=[pl.BlockSpec((tm, d), lambda i: (i, 0)), pl.BlockSpec((1, d), lambda i: (0, 0))],
        out_specs=pl.BlockSpec((tm, d), lambda i: (i, 0)),
        out_shape=jax.ShapeDtypeStruct((s, d), BF16),
        compiler_params=_params("arbitrary"),
    )(x, gain)


def _gathered_in_proj(h, bufs, small, order, name):
    s, d = h.shape
    n_sh, _, r2, ns = bufs[0].shape
    assert d == 2 * r2
    n = len(bufs)
    tm = _row_tile(s, 512)
    n_i = s // tm
    hook_i = max(n_i - 2, 0)
    n_sems = 6 * n + 3

    def body(order_ref, h_ref, *rest):
        p_ref = rest[n + 1]
        b = rest[n + 2:2 * n + 3]
        w_vmem, w_sems, send, recv = rest[2 * n + 3:]
        j, i = pl.program_id(0), pl.program_id(1)
        ctx = (_place(), b, send, recv)

        def fetch(q):
            return pltpu.make_async_copy(b[0].at[order_ref[q]], w_vmem.at[q % 2], w_sems.at[q % 2])

        @pl.when((j == 0) & (i == 0))
        def _():
            for k in range(n):
                for peer in range(2):
                    _gather_ici(ctx, k, peer, True).start()
            for peer in range(3):
                _gather_small(ctx, n, peer, True).start()
            fetch(0).start()
            fetch(0).wait()

        for q in range(1, n_sh):
            @pl.when((j == q - 1) & (i == hook_i))
            def _(q=q):
                if q == 1:
                    _gather_neighbours_landed(ctx, 0)
                if q == 2:
                    for k in range(1, n):
                        _gather_neighbours_landed(ctx, k)
                if q == 3:
                    for k in range(n):
                        _gather_diagonal_landed(ctx, k)
                _gather_d2d(ctx, 0, q - 1, False).wait_recv()
                fetch(q).start()

            @pl.when((j == q) & (i == 0))
            def _(q=q):
                fetch(q).wait()

        wv = w_vmem.at[j % 2]
        p_ref[...] = (jnp.dot(h_ref[:, 0:r2], wv[0], preferred_element_type=F32)
                      + jnp.dot(h_ref[:, r2:d], wv[1], preferred_element_type=F32))

        @pl.when((j == n_sh - 1) & (i == n_i - 1))
        def _():
            for peer in range(3):
                _gather_small(ctx, n, peer, False).wait_recv()
                _gather_small(ctx, n, peer, True).wait_send()
            for k in range(n):
                for peer in range(3):
                    if k > 0:
                        _gather_d2d(ctx, k, peer, False).wait_recv()
                    _gather_ici(ctx, k, peer, True).wait_send()
                    _gather_d2d(ctx, k, peer, True).wait_send()

    all_bufs = list(bufs) + [small]
    out = pl.pallas_call(
        body, name=name,
        grid_spec=pltpu.PrefetchScalarGridSpec(
            num_scalar_prefetch=1, grid=(n_sh, n_i),
            in_specs=[pl.BlockSpec((tm, d), lambda j, i, o: (i, 0))] + [ANY] * (n + 1),
            out_specs=[pl.BlockSpec((tm, ns), lambda j, i, o: (i, o[j]))] + [ANY] * (n + 1),
            scratch_shapes=[pltpu.VMEM((2, 2, r2, ns), BF16), pltpu.SemaphoreType.DMA((2,)),
                            pltpu.SemaphoreType.DMA((n_sems,)), pltpu.SemaphoreType.DMA((n_sems,))]),
        out_shape=[jax.ShapeDtypeStruct((s, n_sh * ns), F32)]
        + [jax.ShapeDtypeStruct(a.shape, a.dtype) for a in all_bufs],
        input_output_aliases={2 + t: 1 + t for t in range(n + 1)},
        compiler_params=_params("arbitrary", "arbitrary"),
    )(order, h, *all_bufs)
    return out[0], list(out[1:])


def _exchange_comm(grads):
    n = len(grads)
    landing = [lax.empty((N_SHARDS,) + a.shape[2:], a.dtype) for a in grads]

    def copies(srcs, b, send, recv):
        x, y, c, _ = _place()
        return [_remote(srcs[k].at[s, 1 - c], b[k].at[s], send.at[N_SHARDS * k + s], recv.at[N_SHARDS * k + s],
                        (x, y, 1 - c)) for k in range(n) for s in range(N_SHARDS)]

    def start(srcs, b, send, recv):
        for cp in copies(srcs, b, send, recv):
            cp.start()

    def finish(srcs, b, send, recv):
        for cp in copies(srcs, b, send, recv):
            cp.wait()

    return _Comm(grads, landing, N_SHARDS * n, [(0, start)], finish)


def _scatter_comm(chip_sums, landing):
    n = len(chip_sums)

    def big(srcs, b, send, recv, k, j, start):
        x, y, c, chips = _place()
        dst = b[k].at[_chip((x, y)) if start else _chip(chips[j])]
        return _remote(srcs[k].at[_chip(chips[j])], dst, send.at[3 * k + j], recv.at[3 * k + j], (*chips[j], c))

    def start(srcs, b, send, recv):
        for k in range(n):
            for j in range(3):
                big(srcs, b, send, recv, k, j, True).start()

    def finish(srcs, b, send, recv):
        for k in range(n):
            for j in range(3):
                big(srcs, b, send, recv, k, j, False).wait_recv()
                big(srcs, b, send, recv, k, j, True).wait_send()

    return _Comm(chip_sums, landing, 3 * n, [(0, start)], finish)


def _join_comm(halves, small):
    n = len(halves)
    flips = [(fx, fy, fc) for fx in (0, 1) for fy in (0, 1) for fc in (0, 1)][1:]

    def half(b, send, recv, k, start):
        x, y, c, _ = _place()
        return _remote(b[k].at[c], b[k].at[c if start else 1 - c], send.at[k], recv.at[k], (x, y, 1 - c))

    def small_copy(b, send, recv, q, start):
        x, y, c, _ = _place()
        px, py, pc = x ^ flips[q][0], y ^ flips[q][1], c ^ flips[q][2]
        blk = b[n].at[4 * x + 2 * y + c if start else 4 * px + 2 * py + pc]
        return _remote(blk, blk, send.at[n + q], recv.at[n + q], (px, py, pc))

    def start(srcs, b, send, recv):
        for q in range(len(flips)):
            small_copy(b, send, recv, q, True).start()
        for k in range(n):
            half(b, send, recv, k, True).start()

    def finish(srcs, b, send, recv):
        for q in range(len(flips)):
            small_copy(b, send, recv, q, False).wait()
        for k in range(n):
            half(b, send, recv, k, False).wait()

    return _Comm([], list(halves) + [small], n + len(flips), [(0, start)], finish)


def _flat_rows(parts):
    flat = jnp.concatenate([p.reshape(-1) for p in parts])
    assert flat.shape[0] % LANES == 0
    return flat.reshape(-1, LANES)


def _unflatten(flat, shapes):
    out, at = [], 0
    for sh in shapes:
        size = 1
        for dim in sh:
            size *= dim
        out.append(flat[at:at + size].reshape(sh))
        at += size
    assert at == flat.shape[0], (at, flat.shape)
    return out


def _col_shards_to_full(a, rows):
    q = a.shape[1] // rows
    return a.reshape(N_SHARDS, rows, q).transpose(1, 0, 2).reshape(rows, N_SHARDS * q)


def _my_col_shard(full, chip):
    rows, cols = full.shape
    q = cols // N_SHARDS
    return lax.dynamic_index_in_dim(full.reshape(rows, N_SHARDS, q), chip, axis=1, keepdims=False)


def kernel(x, e_norm_pre, e_norm_post, e_w_in, e_a_conv, e_b_conv, e_b_conv_bias, e_b_ln_g, e_b_ln_b, e_w_out, o_norm_pre, o_norm_post, o_w_in, o_c_w, o_c_b, o_c_scale, o_w_out, loss_target, m_e_norm_pre, m_e_norm_post, m_e_w_in, m_e_a_conv, m_e_b_conv, m_e_b_conv_bias, m_e_b_ln_g, m_e_b_ln_b, m_e_w_out, m_o_norm_pre, m_o_norm_post, m_o_w_in, m_o_c_w, m_o_c_b, m_o_c_scale, m_o_w_out, v_e_norm_pre, v_e_norm_post, v_e_w_in, v_e_a_conv, v_e_b_conv, v_e_b_conv_bias, v_e_b_ln_g, v_e_b_ln_b, v_e_w_out, v_o_norm_pre, v_o_norm_post, v_o_w_in, v_o_c_w, v_o_c_b, v_o_c_scale, v_o_w_out):
    _, s, d = x.shape
    w = d // 2
    c = d
    gc = c // N_GROUPS
    wq, cq, gq = w // N_SHARDS, c // N_SHARDS, gc // N_SHARDS
    chip = 2 * lax.axis_index("x") + lax.axis_index("y")
    core = lax.axis_index("c")
    x2 = x.reshape(s, d)
    target = loss_target.reshape(s, d)

    big_w = [e_w_in[0], e_w_out[0], o_w_in[0], o_c_w[0].reshape(N_GROUPS * gq, gc), o_w_out[0]]
    big_m = [m_e_w_in[0], m_e_w_out[0], m_o_w_in[0], m_o_c_w[0].reshape(N_GROUPS * gq, gc), m_o_w_out[0]]
    big_v = [v_e_w_in[0], v_e_w_out[0], v_o_w_in[0], v_o_c_w[0].reshape(N_GROUPS * gq, gc), v_o_w_out[0]]
    coords = jnp.stack([chip, core]).astype(jnp.int32)
    slots = [_cast_into_slot(a, coords, "cast_%d" % k) for k, a in enumerate(big_w)]
    sharded_small = _flat_rows([e_a_conv[0], e_b_conv[0], o_norm_pre, o_norm_post, o_c_scale, o_c_b[0]])
    small_slots = lax.dynamic_update_index_in_dim(jnp.zeros((N_SHARDS,) + sharded_small.shape, F32), sharded_small,
                                                  chip, 0)
    xi, yi = lax.axis_index("x"), lax.axis_index("y")
    order = jnp.stack([chip, 2 * (1 - xi) + yi, 2 * xi + (1 - yi), 2 * (1 - xi) + (1 - yi)]).astype(jnp.int32)
    h0 = _rmsnorm(x2, e_norm_pre, "e_pre_norm")
    p0, (e_w_in_g, e_w_out_g, small_g4) = _gathered_in_proj(h0, slots[:2], small_slots, order, "e_in_proj")
    e_w_in_sm = e_w_in_g.reshape((N_SHARDS,) + big_w[0].shape)
    e_w_out_f = e_w_out_g.reshape(w + w, d)
    sm = small_g4.reshape(N_SHARDS, -1)
    at = [0]

    def take(rows, q):
        blk = sm[:, at[0]:at[0] + rows * q]
        at[0] += rows * q
        return _col_shards_to_full(blk, rows)

    a_conv_f = take(CONV_A, wq)
    b_conv_f = take(CONV_B, wq)
    o_pre_f = take(1, cq)
    o_post_f = take(1, cq)
    cs_f = take(1, cq)
    cb_f = take(N_GROUPS, gq).reshape(1, c)

    mixer_steps = s // _row_tile(s, 128)
    (u0, x1, y0), odd_g = _even_mixer_fwd(p0, e_w_out_f, x2, e_norm_post, a_conv_f, b_conv_f, e_b_conv_bias, e_b_ln_g,
                                          e_b_ln_b, "e_mixer_out_proj",
                                          comm=_gather_comm(slots[2:], mixer_steps // 2, (25 * mixer_steps) // 32))
    o_w_in_sm = odd_g[0].reshape((N_SHARDS,) + big_w[2].shape)
    cw_f = odd_g[1].reshape(N_SHARDS, N_GROUPS, gq, gc).transpose(1, 0, 2, 3).reshape(N_GROUPS, gc, gc)
    o_w_out_f = odd_g[2].reshape(c, d)
    p1, h1 = _norm_matmul(x1, o_pre_f, o_w_in_sm, "o_in_proj")
    u1 = _odd_mixer_fwd(p1, cw_f, cb_f, cs_f, "o_mixer_fwd")
    d_y1, d_x2, d_o_post, loss_part = _matmul_post_loss(u1, o_w_out_f, x1, o_post_f, target, "o_out_proj_loss")

    def as_pieces(g, k):
        return g.reshape(N_SHARDS, 2, big_w[k].shape[0] // 2, big_w[k].shape[1])

    def chip_sums(ks, pieces, from_sibling):
        both = [_chip_sum(g, o, coords, "chip_sum_%s" % k) for k, g, o in zip(ks, pieces, from_sibling)]
        return [b[0] for b in both], [b[1] for b in both]

    g_o_w_out = _matmul_tn(u1, d_y1, 1, "o_w_out_grad")
    d_u1, _ = _matmul_nt(d_y1, o_w_out_f[None], "o_out_proj_bwd")
    d_p1, d_cw, d_cb, d_cs = _odd_mixer_bwd(p1, d_u1, cw_f, cb_f, cs_f, "o_mixer_bwd")
    g_o_w_in = _matmul_tn(h1, d_p1, N_SHARDS, "o_w_in_grad")
    g_cw = d_cw.reshape(N_GROUPS, N_SHARDS, gq, gc).transpose(1, 0, 2, 3).astype(BF16)
    pieces_o = [as_pieces(g_o_w_in, 2), as_pieces(g_cw, 3), as_pieces(g_o_w_out, 4)]
    d_h1, sibling_o = _matmul_nt(d_p1, o_w_in_sm, "o_in_proj_bwd", comm=_exchange_comm(pieces_o))
    d_x1, d_o_pre, d_y0, d_e_post = _norm_bwd(d_h1, x1, o_pre_f, d_x2, "o_pre_norm_bwd", post=(y0, e_norm_post))

    pieces_e = [as_pieces(_matmul_tn(u0, d_y0, 1, "e_w_out_grad"), 1)]
    d_u0, sibling_e = _matmul_nt(d_y0, e_w_out_f[None], "e_out_proj_bwd", comm=_exchange_comm(pieces_e))
    sums_a, landing_a = chip_sums([1, 2, 3, 4], pieces_e + pieces_o, sibling_e + sibling_o)
    (d_p0, d_a_conv, d_b_conv, d_bias, d_ln_g, d_ln_b), landed_a = _even_mixer_bwd(
        p0, d_u0, a_conv_f, b_conv_f, e_b_conv_bias, e_b_ln_g, e_b_ln_b, "e_mixer_bwd",
        comm=_scatter_comm(sums_a, landing_a))
    quarter_0, _ = _matmul_tn_quarter(h0, d_p0, N_SHARDS, 0, "e_w_in_grad_0")
    sums_0, landing_0 = chip_sums(["0a"], [quarter_0],
                                  _comm_only(_exchange_comm([quarter_0]), "exchange_core_halves_0"))
    quarter_1, landed_0 = _matmul_tn_quarter(h0, d_p0, N_SHARDS, 1, "e_w_in_grad_1",
                                             comm=_scatter_comm(sums_0, landing_0))
    sums_1, landing_1 = chip_sums(["0b"], [quarter_1],
                                  _comm_only(_exchange_comm([quarter_1]), "exchange_core_halves_1"))
    d_h0, landed_1 = _matmul_nt(d_p0, e_w_in_sm, "e_in_proj_bwd", comm=_scatter_comm(sums_1, landing_1))
    grad_x, d_e_pre = _norm_bwd(d_h0, x2, e_norm_pre, d_x1, "e_pre_norm_bwd")

    landed = landed_0 + landed_1 + landed_a
    reduced = [_shard_sum(sc, coords, "shard_sum_%d" % k) for k, sc in enumerate(landed)]
    small_parts = _flat_rows([loss_part[0], d_e_pre, d_e_post, d_bias, d_ln_g, d_ln_b, d_a_conv, d_b_conv,
                              d_o_pre, d_o_post, d_cs, d_cb])
    small_rows = lax.dynamic_update_index_in_dim(jnp.zeros((N_DEVICES,) + small_parts.shape, F32), small_parts,
                                                 2 * chip + core, 0)
    joined = _comm_only(_join_comm(reduced, small_rows), "join_core_halves")
    rest_g = [j.reshape(a.shape) for j, a in zip(joined[2:6], big_w[1:])]
    small_sum = _sum_small(joined[6], "small_sum").reshape(-1)
    (loss_row, g_e_pre, g_e_post, g_bias, g_ln_g, g_ln_b, g_a_conv_f, g_b_conv_f, g_o_pre_f, g_o_post_f, g_cs_f,
     g_cb_f) = _unflatten(small_sum, [(LANES,), (1, d), (1, d), (1, w), (1, w), (1, w), (CONV_A, w), (CONV_B, w),
                                      (1, c), (1, c), (1, c), (1, c)])
    loss = loss_row[0]
    g_a_conv = _my_col_shard(g_a_conv_f, chip)
    g_b_conv = _my_col_shard(g_b_conv_f, chip)
    g_o_pre = _my_col_shard(g_o_pre_f, chip)
    g_o_post = _my_col_shard(g_o_post_f, chip)
    g_cs = _my_col_shard(g_cs_f, chip)
    g_cb = _my_col_shard(g_cb_f.reshape(N_GROUPS, gc), chip)

    g_e_w_in, *upd_e_w_in = _adamw_quarters(big_w[0], joined[0], joined[1], big_m[0], big_v[0], "adamw_0")
    big_g = [g_e_w_in] + rest_g
    big_upd = [upd_e_w_in] + [_adamw(big_w[k], big_g[k], big_m[k], big_v[k], "adamw_%d" % k) for k in range(1, 5)]
    small_w = [e_norm_pre, e_norm_post, e_b_conv_bias, e_b_ln_g, e_b_ln_b, e_a_conv[0], e_b_conv[0],
               o_norm_pre, o_norm_post, o_c_b[0], o_c_scale]
    small_m = [m_e_norm_pre, m_e_norm_post, m_e_b_conv_bias, m_e_b_ln_g, m_e_b_ln_b, m_e_a_conv[0], m_e_b_conv[0],
               m_o_norm_pre, m_o_norm_post, m_o_c_b[0], m_o_c_scale]
    small_v = [v_e_norm_pre, v_e_norm_post, v_e_b_conv_bias, v_e_b_ln_g, v_e_b_ln_b, v_e_a_conv[0], v_e_b_conv[0],
               v_o_norm_pre, v_o_norm_post, v_o_c_b[0], v_o_c_scale]
    small_g = [g_e_pre, g_e_post, g_bias, g_ln_g, g_ln_b, g_a_conv, g_b_conv, g_o_pre, g_o_post, g_cb, g_cs]
    small_shapes = [a.shape for a in small_w]
    small_upd = _adamw(_flat_rows(small_w), _flat_rows(small_g), _flat_rows(small_m), _flat_rows(small_v),
                       "adamw_small")
    small_delta, small_new_m, small_new_v = [_unflatten(u.reshape(-1), small_shapes) for u in small_upd]

    def ordered(small, big):
        (n_pre, n_post, bias, ln_g, ln_b, a_conv, b_conv, o_pre, o_post, cb, cs) = small
        (w_in, w_out, ow_in, cw, ow_out) = big
        return [n_pre, n_post, w_in[None], a_conv[None], b_conv[None], bias, ln_g, ln_b, w_out[None], o_pre, o_post,
                ow_in[None], cw.reshape(1, N_GROUPS, gq, gc), cb[None], cs, ow_out[None]]

    grads = ordered(small_g, big_g)
    deltas = ordered(small_delta, [u[0] for u in big_upd])
    new_m = ordered(small_new_m, [u[1] for u in big_upd])
    new_v = ordered(small_new_v, [u[2] for u in big_upd])
    return (loss, grad_x.reshape(1, s, d), *grads, *deltas, *new_m, *new_v)
```

```python
import functools

import jax
import jax.numpy as jnp
from jax import lax
from jax.experimental import pallas as pl
from jax.experimental.pallas import tpu as pltpu

F32 = jnp.float32
BF16 = jnp.bfloat16
MESH = pl.DeviceIdType.MESH

EPS = 1e-6
CONV_A = 3
CONV_B = 31
POOL_WINDOWS = (2, 4, 8, 16)
N_GROUPS = len(POOL_WINDOWS)
N_SHARDS = 4
N_DEVICES = 8
ADAM_LR = 0.001
ADAM_B1 = 0.9
ADAM_B2 = 0.999
ADAM_EPS = 1e-08
ADAM_WD = 0.01
ADAM_STEP = 10

LANES = 128
SUBLANES_BF16 = 16
HALO_A = 8
HALO_B = 32
HALO_P = 16
PAD_P = 8
SHIFTS = 8
ROW_BLOCK = 32
LANE_BLOCK = 256
LANE_PASS = 256
VMEM_LIMIT = 56 * 1024 * 1024
TN_ACC_BYTES = 8 * 1024 * 1024
EVEN_BWD_ROWS = 128


def _row_tile(n, pref):
    t = max(min(n, pref) // SUBLANES_BF16, 1) * SUBLANES_BF16
    while t > SUBLANES_BF16 and (n % t or t % SUBLANES_BF16):
        t -= SUBLANES_BF16
    assert n % t == 0, (n, pref)
    return t


def _col_chunk(n, pref):
    t = (min(n, pref) // LANES) * LANES
    while t > LANES and n % t:
        t -= LANES
    assert t >= LANES and n % t == 0, (n, pref)
    return t


def _params(*sem):
    return pltpu.CompilerParams(dimension_semantics=tuple(sem) if sem else None, vmem_limit_bytes=VMEM_LIMIT)


ANY = pl.BlockSpec(memory_space=pl.ANY)


class _Comm:
    def __init__(self, srcs, bufs, n_sems, phases, finish):
        self.srcs, self.bufs, self.n_sems, self.phases, self.finish = list(srcs), list(bufs), n_sems, phases, finish


def _call(body, *, name, grid, in_specs, out_specs, out_shape, args, scratch_shapes=(), comm=None):
    params = _params(*(("arbitrary",) * len(grid)))
    if comm is None:
        out = pl.pallas_call(body, name=name, grid=grid, in_specs=in_specs, out_specs=out_specs, out_shape=out_shape,
                             scratch_shapes=scratch_shapes, compiler_params=params)(*args)
        return list(out), []
    n_in, n_out, n_scr = len(in_specs), len(out_specs), len(scratch_shapes)
    ns, nb = len(comm.srcs), len(comm.bufs)
    total = 1
    for size in grid:
        total *= size

    def fused(*refs):
        ins, srcs = refs[:n_in], refs[n_in:n_in + ns]
        at = n_in + ns + nb
        outs, bufs = refs[at:at + n_out], refs[at + n_out:at + n_out + nb]
        scratch = refs[at + n_out + nb:at + n_out + nb + n_scr]
        send_sems, recv_sems = refs[-2:]
        step = 0
        for axis, size in enumerate(grid):
            step = step * size + pl.program_id(axis)
        for when, fn in comm.phases:
            pl.when(step == when)(functools.partial(fn, srcs, bufs, send_sems, recv_sems))
        body(*ins, *outs, *scratch)
        pl.when(step == total - 1)(functools.partial(comm.finish, srcs, bufs, send_sems, recv_sems))

    out = pl.pallas_call(
        fused, name=name, grid=grid,
        in_specs=list(in_specs) + [ANY] * (ns + nb), out_specs=list(out_specs) + [ANY] * nb,
        out_shape=list(out_shape) + [jax.ShapeDtypeStruct(b.shape, b.dtype) for b in comm.bufs],
        input_output_aliases={n_in + ns + i: n_out + i for i in range(nb)},
        scratch_shapes=list(scratch_shapes) + [pltpu.SemaphoreType.DMA((comm.n_sems,))] * 2,
        compiler_params=params,
    )(*args, *comm.srcs, *comm.bufs)
    return list(out[:n_out]), list(out[n_out:])


def _comm_only(comm, name):
    ns, nb = len(comm.srcs), len(comm.bufs)

    def body(*refs):
        srcs, bufs = refs[:ns], refs[ns + nb:ns + 2 * nb]
        send_sems, recv_sems = refs[-2:]
        for _, fn in comm.phases:
            fn(srcs, bufs, send_sems, recv_sems)
        comm.finish(srcs, bufs, send_sems, recv_sems)

    return pl.pallas_call(
        body, name=name, in_specs=[ANY] * (ns + nb), out_specs=[ANY] * nb,
        out_shape=[jax.ShapeDtypeStruct(b.shape, b.dtype) for b in comm.bufs],
        input_output_aliases={ns + i: i for i in range(nb)},
        scratch_shapes=[pltpu.SemaphoreType.DMA((comm.n_sems,))] * 2,
    )(*comm.srcs, *comm.bufs)


def _sigmoid(v):
    return jax.nn.sigmoid(v)


def _dsilu(v, s):
    return s * (1.0 + v * (1.0 - s))


def _mean_last(v):
    return jnp.mean(v, axis=-1, keepdims=True)


def _sum_rows(v):
    return jnp.sum(v, axis=0, keepdims=True)


def _norm_matmul(x, gain, w_sm, name):
    s, d = x.shape
    n_sh, _, ns = w_sm.shape
    tm = _row_tile(s, 1024)

    def body(x_ref, g_ref, w_ref, p_ref, h_ref):
        @pl.when(pl.program_id(1) == 0)
        def _():
            xv = x_ref[...]
            r = lax.rsqrt(_mean_last(xv * xv) + EPS)
            h_ref[...] = (xv * r * g_ref[...]).astype(BF16)

        p_ref[...] = jnp.dot(h_ref[...], w_ref[0], preferred_element_type=F32)

    return _call(
        body, name=name, grid=(s // tm, n_sh),
        in_specs=[pl.BlockSpec((tm, d), lambda i, j: (i, 0)),
                  pl.BlockSpec((1, d), lambda i, j: (0, 0)),
                  pl.BlockSpec((1, d, ns), lambda i, j: (j, 0, 0))],
        out_specs=[pl.BlockSpec((tm, ns), lambda i, j: (i, j)),
                   pl.BlockSpec((tm, d), lambda i, j: (i, 0))],
        out_shape=[jax.ShapeDtypeStruct((s, n_sh * ns), F32), jax.ShapeDtypeStruct((s, d), BF16)],
        args=(x, gain, w_sm))[0]


def _matmul_post_loss(u, w, x_res, gain, target, name):
    s, k = u.shape
    d = w.shape[1]
    tm = _row_tile(s, 256)

    def body(u_ref, w_ref, x_ref, g_ref, t_ref, dy_ref, dout_ref, dg_ref, loss_ref):
        @pl.when(pl.program_id(0) == 0)
        def _():
            dg_ref[...] = jnp.zeros_like(dg_ref)
            loss_ref[...] = jnp.zeros_like(loss_ref)

        y = jnp.dot(u_ref[...], w_ref[...], preferred_element_type=F32)
        r = lax.rsqrt(_mean_last(y * y) + EPS)
        n = y * r
        g = g_ref[...]
        err = x_ref[...] + n * g - t_ref[...]
        loss_ref[...] += 0.5 * jnp.sum(_mean_last(err * err))
        dout = err * (1.0 / d)
        dout_ref[...] = dout
        dg_ref[...] += _sum_rows(dout * n)
        dn = dout * g
        dy_ref[...] = (r * (dn - n * _mean_last(dn * n))).astype(BF16)

    return pl.pallas_call(
        body, name=name, grid=(s // tm,),
        in_specs=[pl.BlockSpec((tm, k), lambda i: (i, 0)),
                  pl.BlockSpec((k, d), lambda i: (0, 0)),
                  pl.BlockSpec((tm, d), lambda i: (i, 0)),
                  pl.BlockSpec((1, d), lambda i: (0, 0)),
                  pl.BlockSpec((tm, d), lambda i: (i, 0))],
        out_specs=[pl.BlockSpec((tm, d), lambda i: (i, 0)),
                   pl.BlockSpec((tm, d), lambda i: (i, 0)),
                   pl.BlockSpec((1, d), lambda i: (0, 0)),
                   pl.BlockSpec((8, LANES), lambda i: (0, 0))],
        out_shape=[jax.ShapeDtypeStruct((s, d), BF16), jax.ShapeDtypeStruct((s, d), F32),
                   jax.ShapeDtypeStruct((1, d), F32), jax.ShapeDtypeStruct((8, LANES), F32)],
        compiler_params=_params("arbitrary"),
    )(u, w, x_res, gain, target)


def _matmul_nt(a, w_sm, name, comm=None):
    s, ncols = a.shape
    n_sh, r, ns = w_sm.shape
    assert ncols == n_sh * ns
    tm = _row_tile(s, 1024)
    nc = _col_chunk(ns, 1792)
    per = ns // nc
    steps = n_sh * per

    def body(a_ref, w_ref, o_ref):
        part = lax.dot_general(a_ref[...], w_ref[0], (((1,), (1,)), ((), ())), preferred_element_type=F32)

        @pl.when(pl.program_id(1) == 0)
        def _():
            o_ref[...] = part

        @pl.when(pl.program_id(1) > 0)
        def _():
            o_ref[...] += part

    out, bufs = _call(
        body, name=name, grid=(s // tm, steps),
        in_specs=[pl.BlockSpec((tm, nc), lambda i, j: (i, j)),
                  pl.BlockSpec((1, r, nc), lambda i, j: (j // per, 0, j % per))],
        out_specs=[pl.BlockSpec((tm, r), lambda i, j: (i, 0))],
        out_shape=[jax.ShapeDtypeStruct((s, r), F32)],
        args=(a, w_sm), comm=comm)
    return out[0], bufs


def _matmul_tn(a, b, n_sh, name):
    s, k = a.shape
    n = b.shape[1]
    ns = n // n_sh
    tk = _col_chunk(k, TN_ACC_BYTES // (4 * ns))
    ts = _row_tile(s, 2048)
    n_s = s // ts

    def body(a_ref, b_ref, o_ref, acc_ref):
        part = lax.dot_general(a_ref[...], b_ref[...], (((0,), (0,)), ((), ())), preferred_element_type=F32)

        @pl.when(pl.program_id(2) == 0)
        def _():
            acc_ref[...] = part

        @pl.when(pl.program_id(2) > 0)
        def _():
            acc_ref[...] += part

        @pl.when(pl.program_id(2) == n_s - 1)
        def _():
            o_ref[0] = acc_ref[...].astype(BF16)

    return pl.pallas_call(
        body, name=name, grid=(n_sh, k // tk, n_s),
        in_specs=[pl.BlockSpec((ts, tk), lambda j, i, t: (t, i)),
                  pl.BlockSpec((ts, ns), lambda j, i, t: (t, j))],
        out_specs=pl.BlockSpec((1, tk, ns), lambda j, i, t: (j, i, 0)),
        out_shape=jax.ShapeDtypeStruct((n_sh, k, ns), BF16),
        scratch_shapes=[pltpu.VMEM((tk, ns), F32)],
        compiler_params=_params("arbitrary", "arbitrary", "arbitrary"),
    )(a, b)


def _norm_bwd(dh, x, gain, dres, name, post=None):
    s, d = x.shape
    tm = _row_tile(s, 256)
    with_post = post is not None

    def rms_bwd(dout, v, g):
        r = lax.rsqrt(_mean_last(v * v) + EPS)
        n = v * r
        dn = dout * g
        return r * (dn - n * _mean_last(dn * n)), _sum_rows(dout * n)

    def body(*refs):
        if with_post:
            dh_ref, x_ref, g_ref, dres_ref, y_ref, gp_ref, dx_ref, dg_ref, dy_ref, dgp_ref = refs
        else:
            dh_ref, x_ref, g_ref, dres_ref, dx_ref, dg_ref = refs

        @pl.when(pl.program_id(0) == 0)
        def _():
            dg_ref[...] = jnp.zeros_like(dg_ref)
            if with_post:
                dgp_ref[...] = jnp.zeros_like(dgp_ref)

        dv, dg = rms_bwd(dh_ref[...], x_ref[...], g_ref[...])
        dx = dres_ref[...] + dv
        dx_ref[...] = dx
        dg_ref[...] += dg
        if with_post:
            dy, dgp = rms_bwd(dx, y_ref[...], gp_ref[...])
            dy_ref[...] = dy.astype(BF16)
            dgp_ref[...] += dgp

    row = pl.BlockSpec((tm, d), lambda i: (i, 0))
    vec = pl.BlockSpec((1, d), lambda i: (0, 0))
    in_specs = [row, row, vec, row]
    out_specs = [row, vec]
    out_shape = [jax.ShapeDtypeStruct((s, d), F32), jax.ShapeDtypeStruct((1, d), F32)]
    args = [dh, x, gain, dres]
    if with_post:
        in_specs += [row, vec]
        out_specs += [row, vec]
        out_shape += [jax.ShapeDtypeStruct((s, d), BF16), jax.ShapeDtypeStruct((1, d), F32)]
        args += list(post)
    return _call(body, name=name, grid=(s // tm,), in_specs=in_specs, out_specs=out_specs, out_shape=out_shape,
                 args=args)[0]


def _fill_shifted_down(sh, rows):
    for b in range(1, SHIFTS):
        sh[b, SHIFTS:rows, :] = sh[0, SHIFTS - b:rows - b, :]


def _fill_shifted_up(sh, rows):
    for b in range(1, SHIFTS):
        sh[b, 0:rows - SHIFTS, :] = sh[0, b:rows - SHIFTS + b, :]


def _for_blocks(ts, w, fn):
    lb = min(LANE_BLOCK, w)
    for l0 in range(0, w, lb):
        def rows(rb, carry, l0=l0):
            fn(pl.multiple_of(rb * ROW_BLOCK, ROW_BLOCK), slice(l0, l0 + lb))
            return carry

        lax.fori_loop(0, ts // ROW_BLOCK, rows, 0)


TAP_SPAN = SHIFTS * ((CONV_B - 1) // SHIFTS)
WINDOW = ROW_BLOCK + TAP_SPAN


def _taps_of(b):
    return [(a, SHIFTS * a + b) for a in range((CONV_B - 1 - b) // SHIFTS + 1)]


def _conv31(sh, base, step, wt_ref, bias_ref, out_ref, ts, w):
    low = min(0, step * (TAP_SPAN // SHIFTS))

    def block(r0, lanes):
        acc = [jnp.zeros((SHIFTS, lanes.stop - lanes.start), F32) for _ in range(ROW_BLOCK // SHIFTS)]
        for b in range(SHIFTS):
            window = sh[b, pl.ds(pl.multiple_of(r0 + (base + low), SHIFTS), WINDOW), lanes]
            for a, j in (_taps_of(b) if step > 0 else reversed(_taps_of(b))):
                at = step * a - low
                wt = wt_ref[CONV_B - 1 - j, :, lanes]
                acc = [v + wt * window[at + SHIFTS * r:at + SHIFTS * (r + 1), :] for r, v in enumerate(acc)]
        for r, v in enumerate(acc):
            if bias_ref is not None:
                v = v + bias_ref[:, lanes]
            out_ref[pl.ds(pl.multiple_of(r0 + SHIFTS * r, SHIFTS), SHIFTS), lanes] = v

    _for_blocks(ts, w, block)


def _conv31_weight_grad(d_sh, x_sh, wacc, ts, w):
    def block(r0, lanes):
        d = d_sh[0, pl.ds(r0, ROW_BLOCK), lanes]
        for b in range(SHIFTS):
            window = x_sh[b, pl.ds(pl.multiple_of(r0 + (HALO_B - TAP_SPAN), SHIFTS), WINDOW), lanes]
            for a, j in _taps_of(b):
                at = TAP_SPAN - SHIFTS * a
                prod = d * window[at:at + ROW_BLOCK, :]
                part = prod[0:SHIFTS, :]
                for q in range(1, ROW_BLOCK // SHIFTS):
                    part = part + prod[q * SHIFTS:(q + 1) * SHIFTS, :]
                wacc[CONV_B - 1 - j, :, lanes] += part

    _for_blocks(ts, w, block)


def _even_forward_tile(p_ref, halo_ref, first, a_conv_ref, b_conv_ref, bias_ref, lng_ref, lnb_ref, qbuf, ysh, y1buf,
                       wb, w, ts):
    @pl.when(pl.program_id(0) == 0)
    def _():
        for k in range(CONV_B):
            wb[k] = jnp.broadcast_to(b_conv_ref[k:k + 1, :], (SHIFTS, w))

    def col(ref, k, rows=slice(None)):
        return ref[rows, k * w:(k + 1) * w]

    a_x, a_b, a_c, a_z = col(p_ref, 0), col(p_ref, 1), col(p_ref, 2), col(p_ref, 3)
    b_val, b_gate, b_z = col(p_ref, 4), col(p_ref, 5), col(p_ref, 6)
    keep = jnp.where(first, 0.0, 1.0)

    rows_a = slice(HALO_B - HALO_A, HALO_B)
    qbuf[0:HALO_A, :] = col(halo_ref, 2, rows_a) * col(halo_ref, 0, rows_a) * keep
    qbuf[HALO_A:HALO_A + ts, :] = a_c * a_x
    cq = jnp.zeros((ts, w), F32)
    for j in range(CONV_A):
        cq = cq + a_conv_ref[CONV_A - 1 - j:CONV_A - j, :] * qbuf[HALO_A - j:HALO_A - j + ts, :]
    ya = a_b * cq

    ysh[0, 0:HALO_B, :] = col(halo_ref, 4) * _sigmoid(col(halo_ref, 5)) * keep
    ysh[0, HALO_B:HALO_B + ts, :] = b_val * _sigmoid(b_gate)
    _fill_shifted_down(ysh, HALO_B + ts)
    _conv31(ysh, HALO_B, -SHIFTS, wb, bias_ref, y1buf, ts, w)
    yb1 = y1buf[...]
    xc = yb1 - _mean_last(yb1)
    rstd = lax.rsqrt(_mean_last(xc * xc) + EPS)
    xhat = xc * rstd
    yb2 = xhat * lng_ref[...] + lnb_ref[...]
    return dict(a_x=a_x, a_b=a_b, a_c=a_c, a_z=a_z, b_val=b_val, b_gate=b_gate, b_z=b_z,
                cq=cq, ya=ya, rstd=rstd, xhat=xhat, yb2=yb2)


def _even_specs(s, w, ts):
    tile = pl.BlockSpec((ts, 7 * w), lambda i: (i, 0))
    halo = pl.BlockSpec((HALO_B, 7 * w), lambda i: (jnp.maximum(i * (ts // HALO_B) - 1, 0), 0))
    return tile, halo


def _small_specs(shapes, index=lambda i: (0, 0)):
    return [pl.BlockSpec(sh, index) for sh in shapes]


def _even_mixer_fwd(p, w_out, x_res, gain, a_conv, b_conv, bias, ln_g, ln_b, name, comm=None):
    s, d = x_res.shape
    w = p.shape[1] // 7
    ts = _row_tile(s, 128)
    assert ts % HALO_B == 0

    def body(p_ref, halo_ref, wout_ref, x_ref, g_ref, ac_ref, bc_ref, bias_ref, lng_ref, lnb_ref,
             u_ref, xn_ref, y_ref, qbuf, ysh, y1buf, wb):
        first = pl.program_id(0) == 0
        f = _even_forward_tile(p_ref, halo_ref, first, ac_ref, bc_ref, bias_ref, lng_ref, lnb_ref, qbuf, ysh, y1buf,
                               wb, w, ts)
        yb3 = f["yb2"] * _sigmoid(f["yb2"])
        u_a = (f["ya"] * (f["a_z"] * _sigmoid(f["a_z"]))).astype(BF16)
        u_b = (yb3 * (f["b_z"] * _sigmoid(f["b_z"]))).astype(BF16)
        u_ref[:, 0:w] = u_a
        u_ref[:, w:2 * w] = u_b
        y = (jnp.dot(u_a, wout_ref[0:w, :], preferred_element_type=F32)
             + jnp.dot(u_b, wout_ref[w:2 * w, :], preferred_element_type=F32))
        r = lax.rsqrt(_mean_last(y * y) + EPS)
        y_ref[...] = y
        xn_ref[...] = x_ref[...] + (y * r) * g_ref[...]

    tile, halo = _even_specs(s, w, ts)
    row = pl.BlockSpec((ts, d), lambda i: (i, 0))
    return _call(
        body, name=name, grid=(s // ts,),
        in_specs=[tile, halo, pl.BlockSpec((2 * w, d), lambda i: (0, 0)), row, pl.BlockSpec((1, d), lambda i: (0, 0))]
        + _small_specs([(CONV_A, w), (CONV_B, w), (1, w), (1, w), (1, w)]),
        out_specs=[pl.BlockSpec((ts, 2 * w), lambda i: (i, 0)), row, row],
        out_shape=[jax.ShapeDtypeStruct((s, 2 * w), BF16), jax.ShapeDtypeStruct((s, d), F32),
                   jax.ShapeDtypeStruct((s, d), F32)],
        scratch_shapes=[pltpu.VMEM((HALO_A + ts, w), F32), pltpu.VMEM((SHIFTS, HALO_B + ts, w), F32),
                        pltpu.VMEM((ts, w), F32), pltpu.VMEM((CONV_B, SHIFTS, w), F32)],
        args=(p, p, w_out, x_res, gain, a_conv, b_conv, bias, ln_g, ln_b), comm=comm)


def _even_mixer_bwd(p, du, a_conv, b_conv, bias, ln_g, ln_b, name, comm=None):
    s = p.shape[0]
    w = p.shape[1] // 7
    ts = _row_tile(s, EVEN_BWD_ROWS)
    nt = s // ts
    assert ts % HALO_B == 0

    def body(p_ref, halo_ref, du_ref, ac_ref, bc_ref, bias_ref, lng_ref, lnb_ref,
             dp_ref, dac_ref, dbc_ref, dbias_ref, dlng_ref, dlnb_ref,
             qbuf, ysh, y1buf, dqbuf, dsh, dy0buf, wacc, carry_dq, carry_dy, wb):
        step = pl.program_id(0)
        first = step == nt - 1

        @pl.when(step == 0)
        def _():
            for ref in (dac_ref, dbias_ref, dlng_ref, dlnb_ref, wacc, carry_dq, carry_dy):
                ref[...] = jnp.zeros_like(ref)

        @pl.when(step == 0)
        def _():
            for k in range(CONV_B):
                wb[k] = jnp.broadcast_to(bc_ref[k:k + 1, :], (SHIFTS, w))

        keep = jnp.where(first, 0.0, 1.0)
        lane_blocks = [slice(l0, l0 + min(LANE_PASS, w)) for l0 in range(0, w, min(LANE_PASS, w))]

        def col(ref, k, lanes, rows=slice(None)):
            return ref[rows, k * w + lanes.start:k * w + lanes.stop]


        rows_a = slice(HALO_B - HALO_A, HALO_B)
        for lanes in lane_blocks:
            a_x, a_b, a_c, a_z = (col(p_ref, k, lanes) for k in range(4))
            qbuf[0:HALO_A, lanes] = col(halo_ref, 2, lanes, rows_a) * col(halo_ref, 0, lanes, rows_a) * keep
            qbuf[HALO_A:HALO_A + ts, lanes] = a_c * a_x
            cq = jnp.zeros((ts, lanes.stop - lanes.start), F32)
            for j in range(CONV_A):
                cq = cq + ac_ref[CONV_A - 1 - j:CONV_A - j, lanes] * qbuf[HALO_A - j:HALO_A - j + ts, lanes]
            s_az = _sigmoid(a_z)
            du_a = du_ref[:, lanes]
            d_ya = du_a * (a_z * s_az)
            dp_ref[:, 3 * w + lanes.start:3 * w + lanes.stop] = (du_a * (a_b * cq) * _dsilu(a_z, s_az)).astype(BF16)
            dp_ref[:, 1 * w + lanes.start:1 * w + lanes.stop] = (d_ya * cq).astype(BF16)
            d_cq = d_ya * a_b
            dqbuf[0:ts, lanes] = d_cq
            dqbuf[ts:ts + HALO_A, lanes] = carry_dq[:, lanes]
            carry_dq[:, lanes] = d_cq[0:HALO_A, :]
            d_q = jnp.zeros_like(cq)
            for o in range(CONV_A):
                d_q = d_q + ac_ref[CONV_A - 1 - o:CONV_A - o, lanes] * dqbuf[o:o + ts, lanes]
            for j in range(CONV_A):
                k = CONV_A - 1 - j
                dac_ref[k:k + 1, lanes] += _sum_rows(d_cq * qbuf[HALO_A - j:HALO_A - j + ts, lanes])
            dp_ref[:, 2 * w + lanes.start:2 * w + lanes.stop] = (d_q * a_x).astype(BF16)
            dp_ref[:, 0 * w + lanes.start:0 * w + lanes.stop] = (d_q * a_c).astype(BF16)
            ysh[0, 0:HALO_B, lanes] = col(halo_ref, 4, lanes) * _sigmoid(col(halo_ref, 5, lanes)) * keep
            ysh[0, HALO_B:HALO_B + ts, lanes] = col(p_ref, 4, lanes) * _sigmoid(col(p_ref, 5, lanes))
        _fill_shifted_down(ysh, HALO_B + ts)
        _conv31(ysh, HALO_B, -SHIFTS, wb, bias_ref, y1buf, ts, w)

        total = jnp.zeros((ts, 1), F32)
        for lanes in lane_blocks:
            total = total + jnp.sum(y1buf[:, lanes], axis=-1, keepdims=True)
        mu = total * (1.0 / w)
        total = jnp.zeros((ts, 1), F32)
        for lanes in lane_blocks:
            xc = y1buf[:, lanes] - mu
            total = total + jnp.sum(xc * xc, axis=-1, keepdims=True)
        rstd = lax.rsqrt(total * (1.0 / w) + EPS)

        sum_dxh = jnp.zeros((ts, 1), F32)
        sum_dxh_xhat = jnp.zeros((ts, 1), F32)
        for lanes in lane_blocks:
            xhat = (y1buf[:, lanes] - mu) * rstd
            yb2 = xhat * lng_ref[:, lanes] + lnb_ref[:, lanes]
            b_z = col(p_ref, 6, lanes)
            s_bz, s_y2 = _sigmoid(b_z), _sigmoid(yb2)
            du_b = du_ref[:, w + lanes.start:w + lanes.stop]
            dp_ref[:, 6 * w + lanes.start:6 * w + lanes.stop] = (du_b * (yb2 * s_y2) * _dsilu(b_z, s_bz)).astype(BF16)
            d_yb2 = du_b * (b_z * s_bz) * _dsilu(yb2, s_y2)
            dlng_ref[:, lanes] += _sum_rows(d_yb2 * xhat)
            dlnb_ref[:, lanes] += _sum_rows(d_yb2)
            d_xh = d_yb2 * lng_ref[:, lanes]
            sum_dxh = sum_dxh + jnp.sum(d_xh, axis=-1, keepdims=True)
            sum_dxh_xhat = sum_dxh_xhat + jnp.sum(d_xh * xhat, axis=-1, keepdims=True)
            y1buf[:, lanes] = xhat
            dy0buf[:, lanes] = d_xh
        mean_dxh = sum_dxh * (1.0 / w)
        mean_dxh_xhat = sum_dxh_xhat * (1.0 / w)

        for lanes in lane_blocks:
            d_yb1 = rstd * (dy0buf[:, lanes] - mean_dxh - y1buf[:, lanes] * mean_dxh_xhat)
            dbias_ref[:, lanes] += _sum_rows(d_yb1)
            dsh[0, 0:ts, lanes] = d_yb1
            dsh[0, ts:ts + HALO_B, lanes] = carry_dy[:, lanes]
            carry_dy[:, lanes] = d_yb1[0:HALO_B, :]
        _fill_shifted_up(dsh, ts + HALO_B)
        _conv31(dsh, 0, SHIFTS, wb, None, dy0buf, ts, w)
        _conv31_weight_grad(dsh, ysh, wacc, ts, w)

        @pl.when(step == nt - 1)
        def _():
            for k in range(CONV_B):
                dbc_ref[k:k + 1, :] = _sum_rows(wacc[k])

        for lanes in lane_blocks:
            d_yb0 = dy0buf[:, lanes]
            s_g = _sigmoid(col(p_ref, 5, lanes))
            dp_ref[:, 4 * w + lanes.start:4 * w + lanes.stop] = (d_yb0 * s_g).astype(BF16)
            dp_ref[:, 5 * w + lanes.start:5 * w + lanes.stop] = (
                d_yb0 * col(p_ref, 4, lanes) * s_g * (1.0 - s_g)).astype(BF16)

    rev = lambda i: (nt - 1 - i, 0)
    tile = pl.BlockSpec((ts, 7 * w), rev)
    halo = pl.BlockSpec((HALO_B, 7 * w), lambda i: (jnp.maximum((nt - 1 - i) * (ts // HALO_B) - 1, 0), 0))
    small = [(CONV_A, w), (CONV_B, w), (1, w), (1, w), (1, w)]
    return _call(
        body, name=name, grid=(nt,),
        in_specs=[tile, halo, pl.BlockSpec((ts, 2 * w), rev)] + _small_specs(small),
        out_specs=[pl.BlockSpec((ts, 7 * w), rev)] + _small_specs(small),
        out_shape=[jax.ShapeDtypeStruct((s, 7 * w), BF16)] + [jax.ShapeDtypeStruct(sh, F32) for sh in small],
        scratch_shapes=[pltpu.VMEM((HALO_A + ts, w), F32), pltpu.VMEM((SHIFTS, HALO_B + ts, w), F32),
                        pltpu.VMEM((ts, w), F32),
                        pltpu.VMEM((ts + HALO_A, w), F32), pltpu.VMEM((SHIFTS, ts + HALO_B, w), F32),
                        pltpu.VMEM((ts, w), F32), pltpu.VMEM((CONV_B, SHIFTS, w), F32),
                        pltpu.VMEM((HALO_A, w), F32), pltpu.VMEM((HALO_B, w), F32),
                        pltpu.VMEM((CONV_B, SHIFTS, w), F32)],
        args=(p, p, du, a_conv, b_conv, bias, ln_g, ln_b), comm=comm)


def _trailing_sums(buf_a, buf_b, cols, win, rows, first_out):
    src, dst, shift = buf_a, buf_b, 1
    while True:
        last = 2 * shift >= win
        lo = first_out if last else 0
        val = src[PAD_P + lo:PAD_P + rows, cols] + src[PAD_P + lo - shift:PAD_P + rows - shift, cols]
        if last:
            return val
        dst[PAD_P + lo:PAD_P + rows, cols] = val
        src, dst, shift = dst, src, 2 * shift


def _leading_sums(buf_a, buf_b, cols, win, rows, n_out):
    src, dst, shift = buf_a, buf_b, 1
    while True:
        last = 2 * shift >= win
        hi = n_out if last else rows
        val = src[0:hi, cols] + src[shift:hi + shift, cols]
        if last:
            return val
        dst[0:hi, cols] = val
        src, dst, shift = dst, src, 2 * shift


def _pool_forward_tile(p_ref, halo_ref, first, tile_index, cw_ref, cb_ref, cs_ref, vbuf, vtmp, c, gc, ts):
    vbuf[PAD_P:PAD_P + HALO_P, :] = halo_ref[...] * jnp.where(first, 0.0, 1.0)
    vbuf[PAD_P + HALO_P:PAD_P + HALO_P + ts, :] = p_ref[:, 0:c]
    pos = tile_index * ts + lax.broadcasted_iota(jnp.int32, (ts, 1), 0) + 1
    pooled, inv, gout = [], [], []
    for g, win in enumerate(POOL_WINDOWS):
        cols = slice(g * gc, (g + 1) * gc)
        acc = _trailing_sums(vbuf, vtmp, cols, win, HALO_P + ts, HALO_P)
        inv_g = 1.0 / jnp.minimum(pos, win).astype(F32)
        pooled_g = (acc * inv_g - p_ref[:, cols]).astype(BF16)
        pooled.append(pooled_g)
        inv.append(inv_g)
        gout.append(jnp.dot(pooled_g, cw_ref[g], preferred_element_type=F32) + cb_ref[:, cols])
    return pooled, inv, gout


def _odd_mixer_fwd(p, cw, cb, cs, name):
    s = p.shape[0]
    c = p.shape[1] // 2
    gc = c // N_GROUPS
    ts = _row_tile(s, 256)

    def body(p_ref, halo_ref, cw_ref, cb_ref, cs_ref, u_ref, vbuf, vtmp):
        i = pl.program_id(0)

        @pl.when(i == 0)
        def _():
            vbuf[0:PAD_P, :] = jnp.zeros((PAD_P, c), F32)
            vtmp[0:PAD_P, :] = jnp.zeros((PAD_P, c), F32)

        _, _, gout = _pool_forward_tile(p_ref, halo_ref, i == 0, i, cw_ref, cb_ref, cs_ref, vbuf, vtmp, c, gc, ts)
        for g in range(N_GROUPS):
            cols = slice(g * gc, (g + 1) * gc)
            z = p_ref[:, c + g * gc:c + (g + 1) * gc]
            u_ref[:, cols] = (gout[g] * cs_ref[:, cols] * (z * _sigmoid(z))).astype(BF16)

    return pl.pallas_call(
        body, name=name, grid=(s // ts,),
        in_specs=[pl.BlockSpec((ts, 2 * c), lambda i: (i, 0)),
                  pl.BlockSpec((HALO_P, c), lambda i: (jnp.maximum(i * (ts // HALO_P) - 1, 0), 0)),
                  pl.BlockSpec((N_GROUPS, gc, gc), lambda i: (0, 0, 0)),
                  pl.BlockSpec((1, c), lambda i: (0, 0)), pl.BlockSpec((1, c), lambda i: (0, 0))],
        out_specs=pl.BlockSpec((ts, c), lambda i: (i, 0)),
        out_shape=jax.ShapeDtypeStruct((s, c), BF16),
        scratch_shapes=[pltpu.VMEM((PAD_P + HALO_P + ts, c), F32)] * 2,
        compiler_params=_params("arbitrary"),
    )(p, p, cw, cb, cs)


def _odd_mixer_bwd(p, du, cw, cb, cs, name):
    s = p.shape[0]
    c = p.shape[1] // 2
    gc = c // N_GROUPS
    ts = _row_tile(s, 256)
    nt = s // ts

    def body(p_ref, halo_ref, du_ref, cw_ref, cb_ref, cs_ref, dp_ref, dcw_ref, dcb_ref, dcs_ref,
             vbuf, vtmp, ebuf, etmp, carry_e):
        step = pl.program_id(0)
        tile_index = nt - 1 - step

        @pl.when(step == 0)
        def _():
            for ref in (dcw_ref, dcb_ref, dcs_ref, carry_e):
                ref[...] = jnp.zeros_like(ref)
            for ref in (vbuf, vtmp):
                ref[0:PAD_P, :] = jnp.zeros((PAD_P, c), F32)
            for ref in (ebuf, etmp):
                ref[ts + HALO_P:ts + HALO_P + PAD_P, :] = jnp.zeros((PAD_P, c), F32)

        pooled, inv, gout = _pool_forward_tile(p_ref, halo_ref, tile_index == 0, tile_index, cw_ref, cb_ref, cs_ref,
                                               vbuf, vtmp, c, gc, ts)
        ebuf[ts:ts + HALO_P, :] = carry_e[...]
        for g, win in enumerate(POOL_WINDOWS):
            cols = slice(g * gc, (g + 1) * gc)
            z = p_ref[:, c + g * gc:c + (g + 1) * gc]
            sz = _sigmoid(z)
            du_g = du_ref[:, cols]
            scale = cs_ref[:, cols]
            d_y = du_g * (z * sz)
            dp_ref[:, c + g * gc:c + (g + 1) * gc] = (du_g * (gout[g] * scale) * _dsilu(z, sz)).astype(BF16)
            dcs_ref[:, cols] += _sum_rows(d_y * gout[g])
            d_gout = d_y * scale
            dcb_ref[:, cols] += _sum_rows(d_gout)
            d_gout_b = d_gout.astype(BF16)
            dcw_ref[g] += lax.dot_general(pooled[g], d_gout_b, (((0,), (0,)), ((), ())), preferred_element_type=F32)
            d_pool = lax.dot_general(d_gout_b, cw_ref[g], (((1,), (1,)), ((), ())), preferred_element_type=F32)
            e = d_pool * inv[g]
            ebuf[0:ts, cols] = e
            carry_e[:, cols] = e[0:HALO_P, :]
            d_v = _leading_sums(ebuf, etmp, cols, win, ts + HALO_P, ts) - d_pool
            dp_ref[:, cols] = d_v.astype(BF16)

    rev = lambda i: (nt - 1 - i, 0)
    small = [(N_GROUPS, gc, gc), (1, c), (1, c)]
    return pl.pallas_call(
        body, name=name, grid=(nt,),
        in_specs=[pl.BlockSpec((ts, 2 * c), rev),
                  pl.BlockSpec((HALO_P, c), lambda i: (jnp.maximum((nt - 1 - i) * (ts // HALO_P) - 1, 0), 0)),
                  pl.BlockSpec((ts, c), rev),
                  pl.BlockSpec((N_GROUPS, gc, gc), lambda i: (0, 0, 0)),
                  pl.BlockSpec((1, c), lambda i: (0, 0)), pl.BlockSpec((1, c), lambda i: (0, 0))],
        out_specs=[pl.BlockSpec((ts, 2 * c), rev),
                   pl.BlockSpec((N_GROUPS, gc, gc), lambda i: (0, 0, 0)),
                   pl.BlockSpec((1, c), lambda i: (0, 0)), pl.BlockSpec((1, c), lambda i: (0, 0))],
        out_shape=[jax.ShapeDtypeStruct((s, 2 * c), BF16)] + [jax.ShapeDtypeStruct(sh, F32) for sh in small],
        scratch_shapes=[pltpu.VMEM((PAD_P + HALO_P + ts, c), F32)] * 2 + [pltpu.VMEM((ts + HALO_P + PAD_P, c), F32)] * 2
        + [pltpu.VMEM((HALO_P, c), F32)],
        compiler_params=_params("arbitrary"),
    )(p, p, du, cw, cb, cs)


def _cast_into_slot(a, coords, name):
    r, cols = a.shape
    tr = _row_tile(r // 2, 256)
    per = r // 2 // tr

    def body(co_ref, a_ref, o_ref):
        o_ref[0, 0] = a_ref[...].astype(BF16)

    return pl.pallas_call(
        body, name=name,
        grid_spec=pltpu.PrefetchScalarGridSpec(
            num_scalar_prefetch=1, grid=(2, per),
            in_specs=[pl.BlockSpec((tr, cols), lambda h, i, co: (h * per + i, 0))],
            out_specs=pl.BlockSpec((1, 1, tr, cols), lambda h, i, co: (co[0], h, i, 0))),
        out_shape=jax.ShapeDtypeStruct((N_SHARDS, 2, r // 2, cols), BF16),
        compiler_params=_params("arbitrary", "arbitrary"),
    )(coords, a)


def _chip_sum(g, other, coords, name):
    n_sh, _, r2, cols = g.shape
    tr = _row_tile(r2, 256)

    def body(co_ref, g_ref, o_ref, sum_ref, mine_ref):
        v = (g_ref[0, 0].astype(F32) + o_ref[0].astype(F32)).astype(BF16)
        sum_ref[0] = v

        @pl.when(pl.program_id(1) == co_ref[0])
        def _():
            mine_ref[0] = v

    piece = pl.BlockSpec((1, tr, cols), lambda i, s, co: (s, i, 0))
    return pl.pallas_call(
        body, name=name,
        grid_spec=pltpu.PrefetchScalarGridSpec(
            num_scalar_prefetch=1, grid=(r2 // tr, n_sh),
            in_specs=[pl.BlockSpec((1, 1, tr, cols), lambda i, s, co: (s, co[1], i, 0)), piece],
            out_specs=[piece, pl.BlockSpec((1, tr, cols), lambda i, s, co: (co[0], i, 0))]),
        out_shape=[jax.ShapeDtypeStruct((n_sh, r2, cols), BF16)] * 2,
        compiler_params=_params("arbitrary", "arbitrary"),
    )(coords, g, other)


def _shard_sum(pieces, coords, name):
    n_sh, r2, cols = pieces.shape
    tr = _row_tile(r2, 256)

    def body(co_ref, p_ref, o_ref):
        acc = p_ref[0].astype(F32)
        for k in range(1, n_sh):
            acc = acc + p_ref[k].astype(F32)
        o_ref[0] = acc

    return pl.pallas_call(
        body, name=name,
        grid_spec=pltpu.PrefetchScalarGridSpec(
            num_scalar_prefetch=1, grid=(r2 // tr,),
            in_specs=[pl.BlockSpec((n_sh, tr, cols), lambda i, co: (0, i, 0))],
            out_specs=pl.BlockSpec((1, tr, cols), lambda i, co: (co[1], i, 0))),
        out_shape=jax.ShapeDtypeStruct((2, r2, cols), F32),
        compiler_params=_params("arbitrary"),
    )(coords, pieces)


def _sum_small(a, name):
    n, r, cols = a.shape

    def body(a_ref, o_ref):
        acc = a_ref[0]
        for k in range(1, n):
            acc = acc + a_ref[k]
        o_ref[...] = acc

    return pl.pallas_call(
        body, name=name,
        in_specs=[pl.BlockSpec((n, r, cols), lambda: (0, 0, 0))],
        out_specs=pl.BlockSpec((r, cols), lambda: (0, 0)),
        out_shape=jax.ShapeDtypeStruct((r, cols), F32),
        compiler_params=_params(),
    )(a)


def _adamw_step(w, g, m, v):
    m = ADAM_B1 * m + (1.0 - ADAM_B1) * g
    v = ADAM_B2 * v + (1.0 - ADAM_B2) * (g * g)
    m_hat = m / (1.0 - ADAM_B1 ** ADAM_STEP)
    v_hat = v / (1.0 - ADAM_B2 ** ADAM_STEP)
    return -ADAM_LR * (m_hat / (jnp.sqrt(v_hat) + ADAM_EPS) + ADAM_WD * w), m, v


def _adamw(w, g, m, v, name):
    r, cols = w.shape
    tr = _row_tile(r, 256) if r % SUBLANES_BF16 == 0 else r

    def body(w_ref, g_ref, m_ref, v_ref, d_ref, nm_ref, nv_ref):
        d_ref[...], nm_ref[...], nv_ref[...] = _adamw_step(w_ref[...], g_ref[...], m_ref[...], v_ref[...])

    blk = pl.BlockSpec((tr, cols), lambda i: (i, 0))
    return pl.pallas_call(
        body, name=name, grid=(r // tr,),
        in_specs=[blk] * 4, out_specs=[blk] * 3,
        out_shape=[jax.ShapeDtypeStruct((r, cols), F32)] * 3,
        compiler_params=_params("arbitrary"),
    )(w, g, m, v)


def _adamw_many(ws, gs, ms, vs, name):
    n = len(ws)

    def body(*refs):
        w_refs, g_refs, m_refs, v_refs = (refs[t * n:(t + 1) * n] for t in range(4))
        d_refs, nm_refs, nv_refs = (refs[(4 + t) * n:(5 + t) * n] for t in range(3))
        for k in range(n):
            d_refs[k][...], nm_refs[k][...], nv_refs[k][...] = _adamw_step(
                w_refs[k][...], g_refs[k][...], m_refs[k][...], v_refs[k][...])

    specs = [pl.BlockSpec(a.shape, lambda: (0, 0)) for a in ws]
    out = pl.pallas_call(
        body, name=name, in_specs=specs * 4, out_specs=specs * 3,
        out_shape=[jax.ShapeDtypeStruct(a.shape, F32) for a in ws] * 3,
        compiler_params=_params(),
    )(*ws, *gs, *ms, *vs)
    return out[:n], out[n:2 * n], out[2 * n:]


def _place():
    x, y, c = lax.axis_index("x"), lax.axis_index("y"), lax.axis_index("c")
    other_chips = [(1 - x, y), (x, 1 - y), (1 - x, 1 - y)]
    return x, y, c, other_chips


def _chip(xy):
    return 2 * xy[0] + xy[1]


def _remote(src, dst, send_sem, recv_sem, to):
    return pltpu.make_async_remote_copy(src_ref=src, dst_ref=dst, send_sem=send_sem, recv_sem=recv_sem,
                                        device_id=to, device_id_type=MESH)


def _gather_ici(ctx, k, j, start):
    (x, y, c, chips), b, send, recv = ctx
    if j < 2:
        chip, to = (_chip((x, y)) if start else _chip(chips[j])), (*chips[j], c)
    else:
        chip = 2 * (x ^ c) + (y ^ (1 - c)) if start else _chip(chips[2])
        to = (x ^ (1 - c), y ^ c, c)
    blk = b[k].at[chip, c]
    return _remote(blk, blk, send.at[6 * k + j], recv.at[6 * k + j], to)


def _gather_d2d(ctx, k, j, start):
    (x, y, c, chips), b, send, recv = ctx
    blk = b[k].at[_chip(chips[j]), c if start else 1 - c]
    return _remote(blk, blk, send.at[6 * k + 3 + j], recv.at[6 * k + 3 + j], (x, y, 1 - c))


def _gather_small(ctx, n, j, start):
    (x, y, c, chips), b, send, recv = ctx
    blk = b[n].at[_chip((x, y)) if start else _chip(chips[j])]
    return _remote(blk, blk, send.at[6 * n + j], recv.at[6 * n + j], (*chips[j], c))


def _gather_neighbours_landed(ctx, k):
    for j in range(2):
        _gather_ici(ctx, k, j, False).wait_recv()
    _gather_ici(ctx, k, 2, True).start()
    for j in range(2):
        _gather_d2d(ctx, k, j, True).start()


def _gather_diagonal_landed(ctx, k):
    _gather_ici(ctx, k, 2, False).wait_recv()
    _gather_d2d(ctx, k, 2, True).start()


def _gather_comm(bufs, relay_at, forward_at):
    n = len(bufs)

    def start(srcs, b, send, recv):
        for k in range(n):
            for j in range(2):
                _gather_ici((_place(), b, send, recv), k, j, True).start()

    def relay(srcs, b, send, recv):
        for k in range(n):
            _gather_neighbours_landed((_place(), b, send, recv), k)

    def forward(srcs, b, send, recv):
        for k in range(n):
            _gather_diagonal_landed((_place(), b, send, recv), k)

    def finish(srcs, b, send, recv):
        ctx = (_place(), b, send, recv)
        for k in range(n):
            for j in range(3):
                _gather_d2d(ctx, k, j, False).wait_recv()
                _gather_ici(ctx, k, j, True).wait_send()
                _gather_d2d(ctx, k, j, True).wait_send()

    return _Comm([], bufs, 6 * n, [(0, start), (relay_at, relay), (forward_at, forward)], finish)


def _rmsnorm(x, gain, name):
    s, d = x.shape
    tm = _row_tile(s, 512)

    def body(x_ref, g_ref, h_ref):
        xv = x_ref[...]
        r = lax.rsqrt(_mean_last(xv * xv) + EPS)
        h_ref[...] = (xv * r * g_ref[...]).astype(BF16)

    return pl.pallas_call(
        body, name=name, grid=(s // tm,),
        in_specs=[pl.BlockSpec((tm, d), lambda i: (i, 0)), pl.BlockSpec((1, d), lambda i: (0, 0))],
        out_specs=pl.BlockSpec((tm, d), lambda i: (i, 0)),
        out_shape=jax.ShapeDtypeStruct((s, d), BF16),
        compiler_params=_params("arbitrary"),
    )(x, gain)


def _gathered_in_proj(h, bufs, small, order, name):
    s, d = h.shape
    n_sh, _, r2, ns = bufs[0].shape
    assert d == 2 * r2
    n = len(bufs)
    tm = _row_tile(s, 512)
    n_i = s // tm
    hook_i = max(n_i - 2, 0)
    n_sems = 6 * n + 3

    def body(order_ref, h_ref, *rest):
        p_ref = rest[n + 1]
        b = rest[n + 2:2 * n + 3]
        w_vmem, w_sems, send, recv = rest[2 * n + 3:]
        j, i = pl.program_id(0), pl.program_id(1)
        ctx = (_place(), b, send, recv)

        def fetch(q):
            return pltpu.make_async_copy(b[0].at[order_ref[q]], w_vmem.at[q % 2], w_sems.at[q % 2])

        @pl.when((j == 0) & (i == 0))
        def _():
            for k in range(n):
                for peer in range(2):
                    _gather_ici(ctx, k, peer, True).start()
            for peer in range(3):
                _gather_small(ctx, n, peer, True).start()
            fetch(0).start()
            fetch(0).wait()

        for q in range(1, n_sh):
            @pl.when((j == q - 1) & (i == hook_i))
            def _(q=q):
                if q == 1:
                    _gather_neighbours_landed(ctx, 0)
                if q == 2:
                    for k in range(1, n):
                        _gather_neighbours_landed(ctx, k)
                if q == 3:
                    for k in range(n):
                        _gather_diagonal_landed(ctx, k)
                _gather_d2d(ctx, 0, q - 1, False).wait_recv()
                fetch(q).start()

            @pl.when((j == q) & (i == 0))
            def _(q=q):
                fetch(q).wait()

        wv = w_vmem.at[j % 2]
        p_ref[...] = (jnp.dot(h_ref[:, 0:r2], wv[0], preferred_element_type=F32)
                      + jnp.dot(h_ref[:, r2:d], wv[1], preferred_element_type=F32))

        @pl.when((j == n_sh - 1) & (i == n_i - 1))
        def _():
            for peer in range(3):
                _gather_small(ctx, n, peer, False).wait_recv()
                _gather_small(ctx, n, peer, True).wait_send()
            for k in range(n):
                for peer in range(3):
                    if k > 0:
                        _gather_d2d(ctx, k, peer, False).wait_recv()
                    _gather_ici(ctx, k, peer, True).wait_send()
                    _gather_d2d(ctx, k, peer, True).wait_send()

    all_bufs = list(bufs) + [small]
    out = pl.pallas_call(
        body, name=name,
        grid_spec=pltpu.PrefetchScalarGridSpec(
            num_scalar_prefetch=1, grid=(n_sh, n_i),
            in_specs=[pl.BlockSpec((tm, d), lambda j, i, o: (i, 0))] + [ANY] * (n + 1),
            out_specs=[pl.BlockSpec((tm, ns), lambda j, i, o: (i, o[j]))] + [ANY] * (n + 1),
            scratch_shapes=[pltpu.VMEM((2, 2, r2, ns), BF16), pltpu.SemaphoreType.DMA((2,)),
                            pltpu.SemaphoreType.DMA((n_sems,)), pltpu.SemaphoreType.DMA((n_sems,))]),
        out_shape=[jax.ShapeDtypeStruct((s, n_sh * ns), F32)]
        + [jax.ShapeDtypeStruct(a.shape, a.dtype) for a in all_bufs],
        input_output_aliases={2 + t: 1 + t for t in range(n + 1)},
        compiler_params=_params("arbitrary", "arbitrary"),
    )(order, h, *all_bufs)
    return out[0], list(out[1:])


def _exchange_comm(grads):
    n = len(grads)
    landing = [lax.empty((N_SHARDS,) + a.shape[2:], a.dtype) for a in grads]

    def copies(srcs, b, send, recv):
        x, y, c, _ = _place()
        return [_remote(srcs[k].at[s, 1 - c], b[k].at[s], send.at[N_SHARDS * k + s], recv.at[N_SHARDS * k + s],
                        (x, y, 1 - c)) for k in range(n) for s in range(N_SHARDS)]

    def start(srcs, b, send, recv):
        for cp in copies(srcs, b, send, recv):
            cp.start()

    def finish(srcs, b, send, recv):
        for cp in copies(srcs, b, send, recv):
            cp.wait()

    return _Comm(grads, landing, N_SHARDS * n, [(0, start)], finish)


def _scatter_comm(chip_sums, landing):
    n = len(chip_sums)

    def big(srcs, b, send, recv, k, j, start):
        x, y, c, chips = _place()
        dst = b[k].at[_chip((x, y)) if start else _chip(chips[j])]
        return _remote(srcs[k].at[_chip(chips[j])], dst, send.at[3 * k + j], recv.at[3 * k + j], (*chips[j], c))

    def start(srcs, b, send, recv):
        for k in range(n):
            for j in range(3):
                big(srcs, b, send, recv, k, j, True).start()

    def finish(srcs, b, send, recv):
        for k in range(n):
            for j in range(3):
                big(srcs, b, send, recv, k, j, False).wait_recv()
                big(srcs, b, send, recv, k, j, True).wait_send()

    return _Comm(chip_sums, landing, 3 * n, [(0, start)], finish)


def _join_comm(halves, small):
    n = len(halves)
    flips = [(fx, fy, fc) for fx in (0, 1) for fy in (0, 1) for fc in (0, 1)][1:]

    def half(b, send, recv, k, start):
        x, y, c, _ = _place()
        return _remote(b[k].at[c], b[k].at[c if start else 1 - c], send.at[k], recv.at[k], (x, y, 1 - c))

    def small_copy(b, send, recv, q, start):
        x, y, c, _ = _place()
        px, py, pc = x ^ flips[q][0], y ^ flips[q][1], c ^ flips[q][2]
        blk = b[n].at[4 * x + 2 * y + c if start else 4 * px + 2 * py + pc]
        return _remote(blk, blk, send.at[n + q], recv.at[n + q], (px, py, pc))

    def start(srcs, b, send, recv):
        for q in range(len(flips)):
            small_copy(b, send, recv, q, True).start()
        for k in range(n):
            half(b, send, recv, k, True).start()

    def finish(srcs, b, send, recv):
        for q in range(len(flips)):
            small_copy(b, send, recv, q, False).wait()
        for k in range(n):
            half(b, send, recv, k, False).wait()

    return _Comm([], list(halves) + [small], n + len(flips), [(0, start)], finish)


def _flat_rows(parts):
    flat = jnp.concatenate([p.reshape(-1) for p in parts])
    assert flat.shape[0] % LANES == 0
    return flat.reshape(-1, LANES)


def _unflatten(flat, shapes):
    out, at = [], 0
    for sh in shapes:
        size = 1
        for dim in sh:
            size *= dim
        out.append(flat[at:at + size].reshape(sh))
        at += size
    assert at == flat.shape[0], (at, flat.shape)
    return out


def _col_shards_to_full(a, rows):
    q = a.shape[1] // rows
    return a.reshape(N_SHARDS, rows, q).transpose(1, 0, 2).reshape(rows, N_SHARDS * q)


def _my_col_shard(full, chip):
    rows, cols = full.shape
    q = cols // N_SHARDS
    return lax.dynamic_index_in_dim(full.reshape(rows, N_SHARDS, q), chip, axis=1, keepdims=False)


def kernel(x, e_norm_pre, e_norm_post, e_w_in, e_a_conv, e_b_conv, e_b_conv_bias, e_b_ln_g, e_b_ln_b, e_w_out, o_norm_pre, o_norm_post, o_w_in, o_c_w, o_c_b, o_c_scale, o_w_out, loss_target, m_e_norm_pre, m_e_norm_post, m_e_w_in, m_e_a_conv, m_e_b_conv, m_e_b_conv_bias, m_e_b_ln_g, m_e_b_ln_b, m_e_w_out, m_o_norm_pre, m_o_norm_post, m_o_w_in, m_o_c_w, m_o_c_b, m_o_c_scale, m_o_w_out, v_e_norm_pre, v_e_norm_post, v_e_w_in, v_e_a_conv, v_e_b_conv, v_e_b_conv_bias, v_e_b_ln_g, v_e_b_ln_b, v_e_w_out, v_o_norm_pre, v_o_norm_post, v_o_w_in, v_o_c_w, v_o_c_b, v_o_c_scale, v_o_w_out):
    _, s, d = x.shape
    w = d // 2
    c = d
    gc = c // N_GROUPS
    wq, cq, gq = w // N_SHARDS, c // N_SHARDS, gc // N_SHARDS
    chip = 2 * lax.axis_index("x") + lax.axis_index("y")
    core = lax.axis_index("c")
    x2 = x.reshape(s, d)
    target = loss_target.reshape(s, d)

    big_w = [e_w_in[0], e_w_out[0], o_w_in[0], o_c_w[0].reshape(N_GROUPS * gq, gc), o_w_out[0]]
    big_m = [m_e_w_in[0], m_e_w_out[0], m_o_w_in[0], m_o_c_w[0].reshape(N_GROUPS * gq, gc), m_o_w_out[0]]
    big_v = [v_e_w_in[0], v_e_w_out[0], v_o_w_in[0], v_o_c_w[0].reshape(N_GROUPS * gq, gc), v_o_w_out[0]]
    coords = jnp.stack([chip, core]).astype(jnp.int32)
    slots = [_cast_into_slot(a, coords, "cast_%d" % k) for k, a in enumerate(big_w)]
    sharded_small = _flat_rows([e_a_conv[0], e_b_conv[0], o_norm_pre, o_norm_post, o_c_scale, o_c_b[0]])
    small_slots = lax.dynamic_update_index_in_dim(jnp.zeros((N_SHARDS,) + sharded_small.shape, F32), sharded_small,
                                                  chip, 0)
    xi, yi = lax.axis_index("x"), lax.axis_index("y")
    order = jnp.stack([chip, 2 * (1 - xi) + yi, 2 * xi + (1 - yi), 2 * (1 - xi) + (1 - yi)]).astype(jnp.int32)
    h0 = _rmsnorm(x2, e_norm_pre, "e_pre_norm")
    p0, (e_w_in_g, e_w_out_g, small_g4) = _gathered_in_proj(h0, slots[:2], small_slots, order, "e_in_proj")
    e_w_in_sm = e_w_in_g.reshape((N_SHARDS,) + big_w[0].shape)
    e_w_out_f = e_w_out_g.reshape(w + w, d)
    sm = small_g4.reshape(N_SHARDS, -1)
    at = [0]

    def take(rows, q):
        blk = sm[:, at[0]:at[0] + rows * q]
        at[0] += rows * q
        return _col_shards_to_full(blk, rows)

    a_conv_f = take(CONV_A, wq)
    b_conv_f = take(CONV_B, wq)
    o_pre_f = take(1, cq)
    o_post_f = take(1, cq)
    cs_f = take(1, cq)
    cb_f = take(N_GROUPS, gq).reshape(1, c)

    mixer_steps = s // _row_tile(s, 128)
    (u0, x1, y0), odd_g = _even_mixer_fwd(p0, e_w_out_f, x2, e_norm_post, a_conv_f, b_conv_f, e_b_conv_bias, e_b_ln_g,
                                          e_b_ln_b, "e_mixer_out_proj",
                                          comm=_gather_comm(slots[2:], mixer_steps // 2, (25 * mixer_steps) // 32))
    o_w_in_sm = odd_g[0].reshape((N_SHARDS,) + big_w[2].shape)
    cw_f = odd_g[1].reshape(N_SHARDS, N_GROUPS, gq, gc).transpose(1, 0, 2, 3).reshape(N_GROUPS, gc, gc)
    o_w_out_f = odd_g[2].reshape(c, d)
    p1, h1 = _norm_matmul(x1, o_pre_f, o_w_in_sm, "o_in_proj")
    u1 = _odd_mixer_fwd(p1, cw_f, cb_f, cs_f, "o_mixer_fwd")
    d_y1, d_x2, d_o_post, loss_part = _matmul_post_loss(u1, o_w_out_f, x1, o_post_f, target, "o_out_proj_loss")

    def as_pieces(g, k):
        return g.reshape(N_SHARDS, 2, big_w[k].shape[0] // 2, big_w[k].shape[1])

    def chip_sums(ks, pieces, from_sibling):
        both = [_chip_sum(g, o, coords, "chip_sum_%d" % k) for k, g, o in zip(ks, pieces, from_sibling)]
        return [b[0] for b in both], [b[1] for b in both]

    g_o_w_out = _matmul_tn(u1, d_y1, 1, "o_w_out_grad")
    d_u1, _ = _matmul_nt(d_y1, o_w_out_f[None], "o_out_proj_bwd")
    d_p1, d_cw, d_cb, d_cs = _odd_mixer_bwd(p1, d_u1, cw_f, cb_f, cs_f, "o_mixer_bwd")
    g_o_w_in = _matmul_tn(h1, d_p1, N_SHARDS, "o_w_in_grad")
    g_cw = d_cw.reshape(N_GROUPS, N_SHARDS, gq, gc).transpose(1, 0, 2, 3).astype(BF16)
    pieces_o = [as_pieces(g_o_w_in, 2), as_pieces(g_cw, 3), as_pieces(g_o_w_out, 4)]
    d_h1, sibling_o = _matmul_nt(d_p1, o_w_in_sm, "o_in_proj_bwd", comm=_exchange_comm(pieces_o))
    d_x1, d_o_pre, d_y0, d_e_post = _norm_bwd(d_h1, x1, o_pre_f, d_x2, "o_pre_norm_bwd", post=(y0, e_norm_post))

    pieces_e = [as_pieces(_matmul_tn(u0, d_y0, 1, "e_w_out_grad"), 1)]
    d_u0, sibling_e = _matmul_nt(d_y0, e_w_out_f[None], "e_out_proj_bwd", comm=_exchange_comm(pieces_e))
    sums_a, landing_a = chip_sums([1, 2, 3, 4], pieces_e + pieces_o, sibling_e + sibling_o)
    (d_p0, d_a_conv, d_b_conv, d_bias, d_ln_g, d_ln_b), landed_a = _even_mixer_bwd(
        p0, d_u0, a_conv_f, b_conv_f, e_b_conv_bias, e_b_ln_g, e_b_ln_b, "e_mixer_bwd",
        comm=_scatter_comm(sums_a, landing_a))
    pieces_b = [as_pieces(_matmul_tn(h0, d_p0, N_SHARDS, "e_w_in_grad"), 0)]
    sums_b, landing_b = chip_sums([0], pieces_b, _comm_only(_exchange_comm(pieces_b), "exchange_core_halves"))
    d_h0, landed_b = _matmul_nt(d_p0, e_w_in_sm, "e_in_proj_bwd", comm=_scatter_comm(sums_b, landing_b))
    grad_x, d_e_pre = _norm_bwd(d_h0, x2, e_norm_pre, d_x1, "e_pre_norm_bwd")

    landed = landed_b + landed_a
    reduced = [_shard_sum(sc, coords, "shard_sum_%d" % k) for k, sc in enumerate(landed)]
    small_parts = _flat_rows([loss_part[0], d_e_pre, d_e_post, d_bias, d_ln_g, d_ln_b, d_a_conv, d_b_conv,
                              d_o_pre, d_o_post, d_cs, d_cb])
    small_rows = lax.dynamic_update_index_in_dim(jnp.zeros((N_DEVICES,) + small_parts.shape, F32), small_parts,
                                                 2 * chip + core, 0)
    joined = _comm_only(_join_comm(reduced, small_rows), "join_core_halves")
    big_g = [j.reshape(a.shape) for j, a in zip(joined[:5], big_w)]
    small_sum = _sum_small(joined[5], "small_sum").reshape(-1)
    (loss_row, g_e_pre, g_e_post, g_bias, g_ln_g, g_ln_b, g_a_conv_f, g_b_conv_f, g_o_pre_f, g_o_post_f, g_cs_f,
     g_cb_f) = _unflatten(small_sum, [(LANES,), (1, d), (1, d), (1, w), (1, w), (1, w), (CONV_A, w), (CONV_B, w),
                                      (1, c), (1, c), (1, c), (1, c)])
    loss = loss_row[0]
    g_a_conv = _my_col_shard(g_a_conv_f, chip)
    g_b_conv = _my_col_shard(g_b_conv_f, chip)
    g_o_pre = _my_col_shard(g_o_pre_f, chip)
    g_o_post = _my_col_shard(g_o_post_f, chip)
    g_cs = _my_col_shard(g_cs_f, chip)
    g_cb = _my_col_shard(g_cb_f.reshape(N_GROUPS, gc), chip)

    big_upd = [_adamw(wt, g, m, v, "adamw_%d" % k) for k, (wt, g, m, v) in enumerate(zip(big_w, big_g, big_m, big_v))]
    small_w = [e_norm_pre, e_norm_post, e_b_conv_bias, e_b_ln_g, e_b_ln_b, e_a_conv[0], e_b_conv[0],
               o_norm_pre, o_norm_post, o_c_b[0], o_c_scale]
    small_m = [m_e_norm_pre, m_e_norm_post, m_e_b_conv_bias, m_e_b_ln_g, m_e_b_ln_b, m_e_a_conv[0], m_e_b_conv[0],
               m_o_norm_pre, m_o_norm_post, m_o_c_b[0], m_o_c_scale]
    small_v = [v_e_norm_pre, v_e_norm_post, v_e_b_conv_bias, v_e_b_ln_g, v_e_b_ln_b, v_e_a_conv[0], v_e_b_conv[0],
               v_o_norm_pre, v_o_norm_post, v_o_c_b[0], v_o_c_scale]
    small_g = [g_e_pre, g_e_post, g_bias, g_ln_g, g_ln_b, g_a_conv, g_b_conv, g_o_pre, g_o_post, g_cb, g_cs]
    small_delta, small_new_m, small_new_v = _adamw_many(small_w, small_g, small_m, small_v, "adamw_small")

    def ordered(small, big):
        (n_pre, n_post, bias, ln_g, ln_b, a_conv, b_conv, o_pre, o_post, cb, cs) = small
        (w_in, w_out, ow_in, cw, ow_out) = big
        return [n_pre, n_post, w_in[None], a_conv[None], b_conv[None], bias, ln_g, ln_b, w_out[None], o_pre, o_post,
                ow_in[None], cw.reshape(1, N_GROUPS, gq, gc), cb[None], cs, ow_out[None]]

    grads = ordered(small_g, big_g)
    deltas = ordered(small_delta, [u[0] for u in big_upd])
    new_m = ordered(small_new_m, [u[1] for u in big_upd])
    new_v = ordered(small_new_v, [u[2] for u in big_upd])
    return (loss, grad_x.reshape(1, s, d), *grads, *deltas, *new_m, *new_v)
```

```python
import functools

import jax
import jax.numpy as jnp
from jax import lax
from jax.experimental import pallas as pl
from jax.experimental.pallas import tpu as pltpu

F32 = jnp.float32
BF16 = jnp.bfloat16
MESH = pl.DeviceIdType.MESH

EPS = 1e-6
CONV_A = 3
CONV_B = 31
POOL_WINDOWS = (2, 4, 8, 16)
N_GROUPS = len(POOL_WINDOWS)
N_SHARDS = 4
N_DEVICES = 8
ADAM_LR = 0.001
ADAM_B1 = 0.9
ADAM_B2 = 0.999
ADAM_EPS = 1e-08
ADAM_WD = 0.01
ADAM_STEP = 10

LANES = 128
SUBLANES_BF16 = 16
HALO_A = 8
HALO_B = 32
HALO_P = 16
PAD_P = 8
SHIFTS = 8
ROW_BLOCK = 32
LANE_BLOCK = 256
LANE_PASS = 256
VMEM_LIMIT = 56 * 1024 * 1024
TN_ACC_BYTES = 8 * 1024 * 1024
EVEN_BWD_ROWS = 128


def _row_tile(n, pref):
    t = max(min(n, pref) // SUBLANES_BF16, 1) * SUBLANES_BF16
    while t > SUBLANES_BF16 and (n % t or t % SUBLANES_BF16):
        t -= SUBLANES_BF16
    assert n % t == 0, (n, pref)
    return t


def _col_chunk(n, pref):
    t = (min(n, pref) // LANES) * LANES
    while t > LANES and n % t:
        t -= LANES
    assert t >= LANES and n % t == 0, (n, pref)
    return t


def _params(*sem):
    return pltpu.CompilerParams(dimension_semantics=tuple(sem) if sem else None, vmem_limit_bytes=VMEM_LIMIT)


ANY = pl.BlockSpec(memory_space=pl.ANY)


class _Comm:
    def __init__(self, srcs, bufs, n_sems, phases, finish):
        self.srcs, self.bufs, self.n_sems, self.phases, self.finish = list(srcs), list(bufs), n_sems, phases, finish


def _call(body, *, name, grid, in_specs, out_specs, out_shape, args, scratch_shapes=(), comm=None):
    params = _params(*(("arbitrary",) * len(grid)))
    if comm is None:
        out = pl.pallas_call(body, name=name, grid=grid, in_specs=in_specs, out_specs=out_specs, out_shape=out_shape,
                             scratch_shapes=scratch_shapes, compiler_params=params)(*args)
        return list(out), []
    n_in, n_out, n_scr = len(in_specs), len(out_specs), len(scratch_shapes)
    ns, nb = len(comm.srcs), len(comm.bufs)
    total = 1
    for size in grid:
        total *= size

    def fused(*refs):
        ins, srcs = refs[:n_in], refs[n_in:n_in + ns]
        at = n_in + ns + nb
        outs, bufs = refs[at:at + n_out], refs[at + n_out:at + n_out + nb]
        scratch = refs[at + n_out + nb:at + n_out + nb + n_scr]
        send_sems, recv_sems = refs[-2:]
        step = 0
        for axis, size in enumerate(grid):
            step = step * size + pl.program_id(axis)
        for when, fn in comm.phases:
            pl.when(step == when)(functools.partial(fn, srcs, bufs, send_sems, recv_sems))
        body(*ins, *outs, *scratch)
        pl.when(step == total - 1)(functools.partial(comm.finish, srcs, bufs, send_sems, recv_sems))

    out = pl.pallas_call(
        fused, name=name, grid=grid,
        in_specs=list(in_specs) + [ANY] * (ns + nb), out_specs=list(out_specs) + [ANY] * nb,
        out_shape=list(out_shape) + [jax.ShapeDtypeStruct(b.shape, b.dtype) for b in comm.bufs],
        input_output_aliases={n_in + ns + i: n_out + i for i in range(nb)},
        scratch_shapes=list(scratch_shapes) + [pltpu.SemaphoreType.DMA((comm.n_sems,))] * 2,
        compiler_params=params,
    )(*args, *comm.srcs, *comm.bufs)
    return list(out[:n_out]), list(out[n_out:])


def _comm_only(comm, name):
    ns, nb = len(comm.srcs), len(comm.bufs)

    def body(*refs):
        srcs, bufs = refs[:ns], refs[ns + nb:ns + 2 * nb]
        send_sems, recv_sems = refs[-2:]
        for _, fn in comm.phases:
            fn(srcs, bufs, send_sems, recv_sems)
        comm.finish(srcs, bufs, send_sems, recv_sems)

    return pl.pallas_call(
        body, name=name, in_specs=[ANY] * (ns + nb), out_specs=[ANY] * nb,
        out_shape=[jax.ShapeDtypeStruct(b.shape, b.dtype) for b in comm.bufs],
        input_output_aliases={ns + i: i for i in range(nb)},
        scratch_shapes=[pltpu.SemaphoreType.DMA((comm.n_sems,))] * 2,
    )(*comm.srcs, *comm.bufs)


def _sigmoid(v):
    return jax.nn.sigmoid(v)


def _dsilu(v, s):
    return s * (1.0 + v * (1.0 - s))


def _mean_last(v):
    return jnp.mean(v, axis=-1, keepdims=True)


def _sum_rows(v):
    return jnp.sum(v, axis=0, keepdims=True)


def _norm_matmul(x, gain, w_sm, name):
    s, d = x.shape
    n_sh, _, ns = w_sm.shape
    tm = _row_tile(s, 1024)

    def body(x_ref, g_ref, w_ref, p_ref, h_ref):
        @pl.when(pl.program_id(1) == 0)
        def _():
            xv = x_ref[...]
            r = lax.rsqrt(_mean_last(xv * xv) + EPS)
            h_ref[...] = (xv * r * g_ref[...]).astype(BF16)

        p_ref[...] = jnp.dot(h_ref[...], w_ref[0], preferred_element_type=F32)

    return _call(
        body, name=name, grid=(s // tm, n_sh),
        in_specs=[pl.BlockSpec((tm, d), lambda i, j: (i, 0)),
                  pl.BlockSpec((1, d), lambda i, j: (0, 0)),
                  pl.BlockSpec((1, d, ns), lambda i, j: (j, 0, 0))],
        out_specs=[pl.BlockSpec((tm, ns), lambda i, j: (i, j)),
                   pl.BlockSpec((tm, d), lambda i, j: (i, 0))],
        out_shape=[jax.ShapeDtypeStruct((s, n_sh * ns), F32), jax.ShapeDtypeStruct((s, d), BF16)],
        args=(x, gain, w_sm))[0]


def _matmul_post_loss(u, w, x_res, gain, target, name):
    s, k = u.shape
    d = w.shape[1]
    tm = _row_tile(s, 256)

    def body(u_ref, w_ref, x_ref, g_ref, t_ref, dy_ref, dout_ref, dg_ref, loss_ref):
        @pl.when(pl.program_id(0) == 0)
        def _():
            dg_ref[...] = jnp.zeros_like(dg_ref)
            loss_ref[...] = jnp.zeros_like(loss_ref)

        y = jnp.dot(u_ref[...], w_ref[...], preferred_element_type=F32)
        r = lax.rsqrt(_mean_last(y * y) + EPS)
        n = y * r
        g = g_ref[...]
        err = x_ref[...] + n * g - t_ref[...]
        loss_ref[...] += 0.5 * jnp.sum(_mean_last(err * err))
        dout = err * (1.0 / d)
        dout_ref[...] = dout
        dg_ref[...] += _sum_rows(dout * n)
        dn = dout * g
        dy_ref[...] = (r * (dn - n * _mean_last(dn * n))).astype(BF16)

    return pl.pallas_call(
        body, name=name, grid=(s // tm,),
        in_specs=[pl.BlockSpec((tm, k), lambda i: (i, 0)),
                  pl.BlockSpec((k, d), lambda i: (0, 0)),
                  pl.BlockSpec((tm, d), lambda i: (i, 0)),
                  pl.BlockSpec((1, d), lambda i: (0, 0)),
                  pl.BlockSpec((tm, d), lambda i: (i, 0))],
        out_specs=[pl.BlockSpec((tm, d), lambda i: (i, 0)),
                   pl.BlockSpec((tm, d), lambda i: (i, 0)),
                   pl.BlockSpec((1, d), lambda i: (0, 0)),
                   pl.BlockSpec((8, LANES), lambda i: (0, 0))],
        out_shape=[jax.ShapeDtypeStruct((s, d), BF16), jax.ShapeDtypeStruct((s, d), F32),
                   jax.ShapeDtypeStruct((1, d), F32), jax.ShapeDtypeStruct((8, LANES), F32)],
        compiler_params=_params("arbitrary"),
    )(u, w, x_res, gain, target)


def _matmul_nt(a, w_sm, name, comm=None):
    s, ncols = a.shape
    n_sh, r, ns = w_sm.shape
    assert ncols == n_sh * ns
    tm = _row_tile(s, 1024)
    nc = _col_chunk(ns, 1792)
    per = ns // nc
    steps = n_sh * per

    def body(a_ref, w_ref, o_ref):
        part = lax.dot_general(a_ref[...], w_ref[0], (((1,), (1,)), ((), ())), preferred_element_type=F32)

        @pl.when(pl.program_id(1) == 0)
        def _():
            o_ref[...] = part

        @pl.when(pl.program_id(1) > 0)
        def _():
            o_ref[...] += part

    out, bufs = _call(
        body, name=name, grid=(s // tm, steps),
        in_specs=[pl.BlockSpec((tm, nc), lambda i, j: (i, j)),
                  pl.BlockSpec((1, r, nc), lambda i, j: (j // per, 0, j % per))],
        out_specs=[pl.BlockSpec((tm, r), lambda i, j: (i, 0))],
        out_shape=[jax.ShapeDtypeStruct((s, r), F32)],
        args=(a, w_sm), comm=comm)
    return out[0], bufs


def _matmul_tn(a, b, n_sh, name):
    s, k = a.shape
    n = b.shape[1]
    ns = n // n_sh
    tk = _col_chunk(k, TN_ACC_BYTES // (4 * ns))
    ts = _row_tile(s, 2048)
    n_s = s // ts

    def body(a_ref, b_ref, o_ref, acc_ref):
        part = lax.dot_general(a_ref[...], b_ref[...], (((0,), (0,)), ((), ())), preferred_element_type=F32)

        @pl.when(pl.program_id(2) == 0)
        def _():
            acc_ref[...] = part

        @pl.when(pl.program_id(2) > 0)
        def _():
            acc_ref[...] += part

        @pl.when(pl.program_id(2) == n_s - 1)
        def _():
            o_ref[0] = acc_ref[...].astype(BF16)

    return pl.pallas_call(
        body, name=name, grid=(n_sh, k // tk, n_s),
        in_specs=[pl.BlockSpec((ts, tk), lambda j, i, t: (t, i)),
                  pl.BlockSpec((ts, ns), lambda j, i, t: (t, j))],
        out_specs=pl.BlockSpec((1, tk, ns), lambda j, i, t: (j, i, 0)),
        out_shape=jax.ShapeDtypeStruct((n_sh, k, ns), BF16),
        scratch_shapes=[pltpu.VMEM((tk, ns), F32)],
        compiler_params=_params("arbitrary", "arbitrary", "arbitrary"),
    )(a, b)


def _norm_bwd(dh, x, gain, dres, name, post=None):
    s, d = x.shape
    tm = _row_tile(s, 256)
    with_post = post is not None

    def rms_bwd(dout, v, g):
        r = lax.rsqrt(_mean_last(v * v) + EPS)
        n = v * r
        dn = dout * g
        return r * (dn - n * _mean_last(dn * n)), _sum_rows(dout * n)

    def body(*refs):
        if with_post:
            dh_ref, x_ref, g_ref, dres_ref, y_ref, gp_ref, dx_ref, dg_ref, dy_ref, dgp_ref = refs
        else:
            dh_ref, x_ref, g_ref, dres_ref, dx_ref, dg_ref = refs

        @pl.when(pl.program_id(0) == 0)
        def _():
            dg_ref[...] = jnp.zeros_like(dg_ref)
            if with_post:
                dgp_ref[...] = jnp.zeros_like(dgp_ref)

        dv, dg = rms_bwd(dh_ref[...], x_ref[...], g_ref[...])
        dx = dres_ref[...] + dv
        dx_ref[...] = dx
        dg_ref[...] += dg
        if with_post:
            dy, dgp = rms_bwd(dx, y_ref[...], gp_ref[...])
            dy_ref[...] = dy.astype(BF16)
            dgp_ref[...] += dgp

    row = pl.BlockSpec((tm, d), lambda i: (i, 0))
    vec = pl.BlockSpec((1, d), lambda i: (0, 0))
    in_specs = [row, row, vec, row]
    out_specs = [row, vec]
    out_shape = [jax.ShapeDtypeStruct((s, d), F32), jax.ShapeDtypeStruct((1, d), F32)]
    args = [dh, x, gain, dres]
    if with_post:
        in_specs += [row, vec]
        out_specs += [row, vec]
        out_shape += [jax.ShapeDtypeStruct((s, d), BF16), jax.ShapeDtypeStruct((1, d), F32)]
        args += list(post)
    return _call(body, name=name, grid=(s // tm,), in_specs=in_specs, out_specs=out_specs, out_shape=out_shape,
                 args=args)[0]


def _fill_shifted_down(sh, rows):
    for b in range(1, SHIFTS):
        sh[b, SHIFTS:rows, :] = sh[0, SHIFTS - b:rows - b, :]


def _fill_shifted_up(sh, rows):
    for b in range(1, SHIFTS):
        sh[b, 0:rows - SHIFTS, :] = sh[0, b:rows - SHIFTS + b, :]


def _for_blocks(ts, w, fn):
    lb = min(LANE_BLOCK, w)
    for l0 in range(0, w, lb):
        def rows(rb, carry, l0=l0):
            fn(pl.multiple_of(rb * ROW_BLOCK, ROW_BLOCK), slice(l0, l0 + lb))
            return carry

        lax.fori_loop(0, ts // ROW_BLOCK, rows, 0)


TAP_SPAN = SHIFTS * ((CONV_B - 1) // SHIFTS)
WINDOW = ROW_BLOCK + TAP_SPAN


def _taps_of(b):
    return [(a, SHIFTS * a + b) for a in range((CONV_B - 1 - b) // SHIFTS + 1)]


def _conv31(sh, base, step, wt_ref, bias_ref, out_ref, ts, w):
    low = min(0, step * (TAP_SPAN // SHIFTS))

    def block(r0, lanes):
        acc = [jnp.zeros((SHIFTS, lanes.stop - lanes.start), F32) for _ in range(ROW_BLOCK // SHIFTS)]
        for b in range(SHIFTS):
            window = sh[b, pl.ds(pl.multiple_of(r0 + (base + low), SHIFTS), WINDOW), lanes]
            for a, j in (_taps_of(b) if step > 0 else reversed(_taps_of(b))):
                at = step * a - low
                wt = wt_ref[CONV_B - 1 - j, :, lanes]
                acc = [v + wt * window[at + SHIFTS * r:at + SHIFTS * (r + 1), :] for r, v in enumerate(acc)]
        for r, v in enumerate(acc):
            if bias_ref is not None:
                v = v + bias_ref[:, lanes]
            out_ref[pl.ds(pl.multiple_of(r0 + SHIFTS * r, SHIFTS), SHIFTS), lanes] = v

    _for_blocks(ts, w, block)


def _conv31_weight_grad(d_sh, x_sh, wacc, ts, w):
    def block(r0, lanes):
        d = d_sh[0, pl.ds(r0, ROW_BLOCK), lanes]
        for b in range(SHIFTS):
            window = x_sh[b, pl.ds(pl.multiple_of(r0 + (HALO_B - TAP_SPAN), SHIFTS), WINDOW), lanes]
            for a, j in _taps_of(b):
                at = TAP_SPAN - SHIFTS * a
                prod = d * window[at:at + ROW_BLOCK, :]
                part = prod[0:SHIFTS, :]
                for q in range(1, ROW_BLOCK // SHIFTS):
                    part = part + prod[q * SHIFTS:(q + 1) * SHIFTS, :]
                wacc[CONV_B - 1 - j, :, lanes] += part

    _for_blocks(ts, w, block)


def _even_forward_tile(p_ref, halo_ref, first, a_conv_ref, b_conv_ref, bias_ref, lng_ref, lnb_ref, qbuf, ysh, y1buf,
                       wb, w, ts):
    @pl.when(pl.program_id(0) == 0)
    def _():
        for k in range(CONV_B):
            wb[k] = jnp.broadcast_to(b_conv_ref[k:k + 1, :], (SHIFTS, w))

    def col(ref, k, rows=slice(None)):
        return ref[rows, k * w:(k + 1) * w]

    a_x, a_b, a_c, a_z = col(p_ref, 0), col(p_ref, 1), col(p_ref, 2), col(p_ref, 3)
    b_val, b_gate, b_z = col(p_ref, 4), col(p_ref, 5), col(p_ref, 6)
    keep = jnp.where(first, 0.0, 1.0)

    rows_a = slice(HALO_B - HALO_A, HALO_B)
    qbuf[0:HALO_A, :] = col(halo_ref, 2, rows_a) * col(halo_ref, 0, rows_a) * keep
    qbuf[HALO_A:HALO_A + ts, :] = a_c * a_x
    cq = jnp.zeros((ts, w), F32)
    for j in range(CONV_A):
        cq = cq + a_conv_ref[CONV_A - 1 - j:CONV_A - j, :] * qbuf[HALO_A - j:HALO_A - j + ts, :]
    ya = a_b * cq

    ysh[0, 0:HALO_B, :] = col(halo_ref, 4) * _sigmoid(col(halo_ref, 5)) * keep
    ysh[0, HALO_B:HALO_B + ts, :] = b_val * _sigmoid(b_gate)
    _fill_shifted_down(ysh, HALO_B + ts)
    _conv31(ysh, HALO_B, -SHIFTS, wb, bias_ref, y1buf, ts, w)
    yb1 = y1buf[...]
    xc = yb1 - _mean_last(yb1)
    rstd = lax.rsqrt(_mean_last(xc * xc) + EPS)
    xhat = xc * rstd
    yb2 = xhat * lng_ref[...] + lnb_ref[...]
    return dict(a_x=a_x, a_b=a_b, a_c=a_c, a_z=a_z, b_val=b_val, b_gate=b_gate, b_z=b_z,
                cq=cq, ya=ya, rstd=rstd, xhat=xhat, yb2=yb2)


def _even_specs(s, w, ts):
    tile = pl.BlockSpec((ts, 7 * w), lambda i: (i, 0))
    halo = pl.BlockSpec((HALO_B, 7 * w), lambda i: (jnp.maximum(i * (ts // HALO_B) - 1, 0), 0))
    return tile, halo


def _small_specs(shapes, index=lambda i: (0, 0)):
    return [pl.BlockSpec(sh, index) for sh in shapes]


def _even_mixer_fwd(p, w_out, x_res, gain, a_conv, b_conv, bias, ln_g, ln_b, name, comm=None):
    s, d = x_res.shape
    w = p.shape[1] // 7
    ts = _row_tile(s, 128)
    assert ts % HALO_B == 0

    def body(p_ref, halo_ref, wout_ref, x_ref, g_ref, ac_ref, bc_ref, bias_ref, lng_ref, lnb_ref,
             u_ref, xn_ref, y_ref, qbuf, ysh, y1buf, wb):
        first = pl.program_id(0) == 0
        f = _even_forward_tile(p_ref, halo_ref, first, ac_ref, bc_ref, bias_ref, lng_ref, lnb_ref, qbuf, ysh, y1buf,
                               wb, w, ts)
        yb3 = f["yb2"] * _sigmoid(f["yb2"])
        u_a = (f["ya"] * (f["a_z"] * _sigmoid(f["a_z"]))).astype(BF16)
        u_b = (yb3 * (f["b_z"] * _sigmoid(f["b_z"]))).astype(BF16)
        u_ref[:, 0:w] = u_a
        u_ref[:, w:2 * w] = u_b
        y = (jnp.dot(u_a, wout_ref[0:w, :], preferred_element_type=F32)
             + jnp.dot(u_b, wout_ref[w:2 * w, :], preferred_element_type=F32))
        r = lax.rsqrt(_mean_last(y * y) + EPS)
        y_ref[...] = y
        xn_ref[...] = x_ref[...] + (y * r) * g_ref[...]

    tile, halo = _even_specs(s, w, ts)
    row = pl.BlockSpec((ts, d), lambda i: (i, 0))
    return _call(
        body, name=name, grid=(s // ts,),
        in_specs=[tile, halo, pl.BlockSpec((2 * w, d), lambda i: (0, 0)), row, pl.BlockSpec((1, d), lambda i: (0, 0))]
        + _small_specs([(CONV_A, w), (CONV_B, w), (1, w), (1, w), (1, w)]),
        out_specs=[pl.BlockSpec((ts, 2 * w), lambda i: (i, 0)), row, row],
        out_shape=[jax.ShapeDtypeStruct((s, 2 * w), BF16), jax.ShapeDtypeStruct((s, d), F32),
                   jax.ShapeDtypeStruct((s, d), F32)],
        scratch_shapes=[pltpu.VMEM((HALO_A + ts, w), F32), pltpu.VMEM((SHIFTS, HALO_B + ts, w), F32),
                        pltpu.VMEM((ts, w), F32), pltpu.VMEM((CONV_B, SHIFTS, w), F32)],
        args=(p, p, w_out, x_res, gain, a_conv, b_conv, bias, ln_g, ln_b), comm=comm)


def _even_mixer_bwd(p, du, a_conv, b_conv, bias, ln_g, ln_b, name, comm=None):
    s = p.shape[0]
    w = p.shape[1] // 7
    ts = _row_tile(s, EVEN_BWD_ROWS)
    nt = s // ts
    assert ts % HALO_B == 0

    def body(p_ref, halo_ref, du_ref, ac_ref, bc_ref, bias_ref, lng_ref, lnb_ref,
             dp_ref, dac_ref, dbc_ref, dbias_ref, dlng_ref, dlnb_ref,
             qbuf, ysh, y1buf, dqbuf, dsh, dy0buf, wacc, carry_dq, carry_dy, wb):
        step = pl.program_id(0)
        first = step == nt - 1

        @pl.when(step == 0)
        def _():
            for ref in (dac_ref, dbias_ref, dlng_ref, dlnb_ref, wacc, carry_dq, carry_dy):
                ref[...] = jnp.zeros_like(ref)

        @pl.when(step == 0)
        def _():
            for k in range(CONV_B):
                wb[k] = jnp.broadcast_to(bc_ref[k:k + 1, :], (SHIFTS, w))

        keep = jnp.where(first, 0.0, 1.0)
        lane_blocks = [slice(l0, l0 + min(LANE_PASS, w)) for l0 in range(0, w, min(LANE_PASS, w))]

        def col(ref, k, lanes, rows=slice(None)):
            return ref[rows, k * w + lanes.start:k * w + lanes.stop]


        rows_a = slice(HALO_B - HALO_A, HALO_B)
        for lanes in lane_blocks:
            a_x, a_b, a_c, a_z = (col(p_ref, k, lanes) for k in range(4))
            qbuf[0:HALO_A, lanes] = col(halo_ref, 2, lanes, rows_a) * col(halo_ref, 0, lanes, rows_a) * keep
            qbuf[HALO_A:HALO_A + ts, lanes] = a_c * a_x
            cq = jnp.zeros((ts, lanes.stop - lanes.start), F32)
            for j in range(CONV_A):
                cq = cq + ac_ref[CONV_A - 1 - j:CONV_A - j, lanes] * qbuf[HALO_A - j:HALO_A - j + ts, lanes]
            s_az = _sigmoid(a_z)
            du_a = du_ref[:, lanes]
            d_ya = du_a * (a_z * s_az)
            dp_ref[:, 3 * w + lanes.start:3 * w + lanes.stop] = (du_a * (a_b * cq) * _dsilu(a_z, s_az)).astype(BF16)
            dp_ref[:, 1 * w + lanes.start:1 * w + lanes.stop] = (d_ya * cq).astype(BF16)
            d_cq = d_ya * a_b
            dqbuf[0:ts, lanes] = d_cq
            dqbuf[ts:ts + HALO_A, lanes] = carry_dq[:, lanes]
            carry_dq[:, lanes] = d_cq[0:HALO_A, :]
            d_q = jnp.zeros_like(cq)
            for o in range(CONV_A):
                d_q = d_q + ac_ref[CONV_A - 1 - o:CONV_A - o, lanes] * dqbuf[o:o + ts, lanes]
            for j in range(CONV_A):
                k = CONV_A - 1 - j
                dac_ref[k:k + 1, lanes] += _sum_rows(d_cq * qbuf[HALO_A - j:HALO_A - j + ts, lanes])
            dp_ref[:, 2 * w + lanes.start:2 * w + lanes.stop] = (d_q * a_x).astype(BF16)
            dp_ref[:, 0 * w + lanes.start:0 * w + lanes.stop] = (d_q * a_c).astype(BF16)
            ysh[0, 0:HALO_B, lanes] = col(halo_ref, 4, lanes) * _sigmoid(col(halo_ref, 5, lanes)) * keep
            ysh[0, HALO_B:HALO_B + ts, lanes] = col(p_ref, 4, lanes) * _sigmoid(col(p_ref, 5, lanes))
        _fill_shifted_down(ysh, HALO_B + ts)
        _conv31(ysh, HALO_B, -SHIFTS, wb, bias_ref, y1buf, ts, w)

        total = jnp.zeros((ts, 1), F32)
        for lanes in lane_blocks:
            total = total + jnp.sum(y1buf[:, lanes], axis=-1, keepdims=True)
        mu = total * (1.0 / w)
        total = jnp.zeros((ts, 1), F32)
        for lanes in lane_blocks:
            xc = y1buf[:, lanes] - mu
            total = total + jnp.sum(xc * xc, axis=-1, keepdims=True)
        rstd = lax.rsqrt(total * (1.0 / w) + EPS)

        sum_dxh = jnp.zeros((ts, 1), F32)
        sum_dxh_xhat = jnp.zeros((ts, 1), F32)
        for lanes in lane_blocks:
            xhat = (y1buf[:, lanes] - mu) * rstd
            yb2 = xhat * lng_ref[:, lanes] + lnb_ref[:, lanes]
            b_z = col(p_ref, 6, lanes)
            s_bz, s_y2 = _sigmoid(b_z), _sigmoid(yb2)
            du_b = du_ref[:, w + lanes.start:w + lanes.stop]
            dp_ref[:, 6 * w + lanes.start:6 * w + lanes.stop] = (du_b * (yb2 * s_y2) * _dsilu(b_z, s_bz)).astype(BF16)
            d_yb2 = du_b * (b_z * s_bz) * _dsilu(yb2, s_y2)
            dlng_ref[:, lanes] += _sum_rows(d_yb2 * xhat)
            dlnb_ref[:, lanes] += _sum_rows(d_yb2)
            d_xh = d_yb2 * lng_ref[:, lanes]
            sum_dxh = sum_dxh + jnp.sum(d_xh, axis=-1, keepdims=True)
            sum_dxh_xhat = sum_dxh_xhat + jnp.sum(d_xh * xhat, axis=-1, keepdims=True)
            y1buf[:, lanes] = xhat
            dy0buf[:, lanes] = d_xh
        mean_dxh = sum_dxh * (1.0 / w)
        mean_dxh_xhat = sum_dxh_xhat * (1.0 / w)

        for lanes in lane_blocks:
            d_yb1 = rstd * (dy0buf[:, lanes] - mean_dxh - y1buf[:, lanes] * mean_dxh_xhat)
            dbias_ref[:, lanes] += _sum_rows(d_yb1)
            dsh[0, 0:ts, lanes] = d_yb1
            dsh[0, ts:ts + HALO_B, lanes] = carry_dy[:, lanes]
            carry_dy[:, lanes] = d_yb1[0:HALO_B, :]
        _fill_shifted_up(dsh, ts + HALO_B)
        _conv31(dsh, 0, SHIFTS, wb, None, dy0buf, ts, w)
        _conv31_weight_grad(dsh, ysh, wacc, ts, w)

        @pl.when(step == nt - 1)
        def _():
            for k in range(CONV_B):
                dbc_ref[k:k + 1, :] = _sum_rows(wacc[k])

        for lanes in lane_blocks:
            d_yb0 = dy0buf[:, lanes]
            s_g = _sigmoid(col(p_ref, 5, lanes))
            dp_ref[:, 4 * w + lanes.start:4 * w + lanes.stop] = (d_yb0 * s_g).astype(BF16)
            dp_ref[:, 5 * w + lanes.start:5 * w + lanes.stop] = (
                d_yb0 * col(p_ref, 4, lanes) * s_g * (1.0 - s_g)).astype(BF16)

    rev = lambda i: (nt - 1 - i, 0)
    tile = pl.BlockSpec((ts, 7 * w), rev)
    halo = pl.BlockSpec((HALO_B, 7 * w), lambda i: (jnp.maximum((nt - 1 - i) * (ts // HALO_B) - 1, 0), 0))
    small = [(CONV_A, w), (CONV_B, w), (1, w), (1, w), (1, w)]
    return _call(
        body, name=name, grid=(nt,),
        in_specs=[tile, halo, pl.BlockSpec((ts, 2 * w), rev)] + _small_specs(small),
        out_specs=[pl.BlockSpec((ts, 7 * w), rev)] + _small_specs(small),
        out_shape=[jax.ShapeDtypeStruct((s, 7 * w), BF16)] + [jax.ShapeDtypeStruct(sh, F32) for sh in small],
        scratch_shapes=[pltpu.VMEM((HALO_A + ts, w), F32), pltpu.VMEM((SHIFTS, HALO_B + ts, w), F32),
                        pltpu.VMEM((ts, w), F32),
                        pltpu.VMEM((ts + HALO_A, w), F32), pltpu.VMEM((SHIFTS, ts + HALO_B, w), F32),
                        pltpu.VMEM((ts, w), F32), pltpu.VMEM((CONV_B, SHIFTS, w), F32),
                        pltpu.VMEM((HALO_A, w), F32), pltpu.VMEM((HALO_B, w), F32),
                        pltpu.VMEM((CONV_B, SHIFTS, w), F32)],
        args=(p, p, du, a_conv, b_conv, bias, ln_g, ln_b), comm=comm)


def _trailing_sums(buf_a, buf_b, cols, win, rows, first_out):
    src, dst, shift = buf_a, buf_b, 1
    while True:
        last = 2 * shift >= win
        lo = first_out if last else 0
        val = src[PAD_P + lo:PAD_P + rows, cols] + src[PAD_P + lo - shift:PAD_P + rows - shift, cols]
        if last:
            return val
        dst[PAD_P + lo:PAD_P + rows, cols] = val
        src, dst, shift = dst, src, 2 * shift


def _leading_sums(buf_a, buf_b, cols, win, rows, n_out):
    src, dst, shift = buf_a, buf_b, 1
    while True:
        last = 2 * shift >= win
        hi = n_out if last else rows
        val = src[0:hi, cols] + src[shift:hi + shift, cols]
        if last:
            return val
        dst[0:hi, cols] = val
        src, dst, shift = dst, src, 2 * shift


def _pool_forward_tile(p_ref, halo_ref, first, tile_index, cw_ref, cb_ref, cs_ref, vbuf, vtmp, c, gc, ts):
    vbuf[PAD_P:PAD_P + HALO_P, :] = halo_ref[...] * jnp.where(first, 0.0, 1.0)
    vbuf[PAD_P + HALO_P:PAD_P + HALO_P + ts, :] = p_ref[:, 0:c]
    pos = tile_index * ts + lax.broadcasted_iota(jnp.int32, (ts, 1), 0) + 1
    pooled, inv, gout = [], [], []
    for g, win in enumerate(POOL_WINDOWS):
        cols = slice(g * gc, (g + 1) * gc)
        acc = _trailing_sums(vbuf, vtmp, cols, win, HALO_P + ts, HALO_P)
        inv_g = 1.0 / jnp.minimum(pos, win).astype(F32)
        pooled_g = (acc * inv_g - p_ref[:, cols]).astype(BF16)
        pooled.append(pooled_g)
        inv.append(inv_g)
        gout.append(jnp.dot(pooled_g, cw_ref[g], preferred_element_type=F32) + cb_ref[:, cols])
    return pooled, inv, gout


def _odd_mixer_fwd(p, cw, cb, cs, name):
    s = p.shape[0]
    c = p.shape[1] // 2
    gc = c // N_GROUPS
    ts = _row_tile(s, 256)

    def body(p_ref, halo_ref, cw_ref, cb_ref, cs_ref, u_ref, vbuf, vtmp):
        i = pl.program_id(0)

        @pl.when(i == 0)
        def _():
            vbuf[0:PAD_P, :] = jnp.zeros((PAD_P, c), F32)
            vtmp[0:PAD_P, :] = jnp.zeros((PAD_P, c), F32)

        _, _, gout = _pool_forward_tile(p_ref, halo_ref, i == 0, i, cw_ref, cb_ref, cs_ref, vbuf, vtmp, c, gc, ts)
        for g in range(N_GROUPS):
            cols = slice(g * gc, (g + 1) * gc)
            z = p_ref[:, c + g * gc:c + (g + 1) * gc]
            u_ref[:, cols] = (gout[g] * cs_ref[:, cols] * (z * _sigmoid(z))).astype(BF16)

    return pl.pallas_call(
        body, name=name, grid=(s // ts,),
        in_specs=[pl.BlockSpec((ts, 2 * c), lambda i: (i, 0)),
                  pl.BlockSpec((HALO_P, c), lambda i: (jnp.maximum(i * (ts // HALO_P) - 1, 0), 0)),
                  pl.BlockSpec((N_GROUPS, gc, gc), lambda i: (0, 0, 0)),
                  pl.BlockSpec((1, c), lambda i: (0, 0)), pl.BlockSpec((1, c), lambda i: (0, 0))],
        out_specs=pl.BlockSpec((ts, c), lambda i: (i, 0)),
        out_shape=jax.ShapeDtypeStruct((s, c), BF16),
        scratch_shapes=[pltpu.VMEM((PAD_P + HALO_P + ts, c), F32)] * 2,
        compiler_params=_params("arbitrary"),
    )(p, p, cw, cb, cs)


def _odd_mixer_bwd(p, du, cw, cb, cs, name):
    s = p.shape[0]
    c = p.shape[1] // 2
    gc = c // N_GROUPS
    ts = _row_tile(s, 256)
    nt = s // ts

    def body(p_ref, halo_ref, du_ref, cw_ref, cb_ref, cs_ref, dp_ref, dcw_ref, dcb_ref, dcs_ref,
             vbuf, vtmp, ebuf, etmp, carry_e):
        step = pl.program_id(0)
        tile_index = nt - 1 - step

        @pl.when(step == 0)
        def _():
            for ref in (dcw_ref, dcb_ref, dcs_ref, carry_e):
                ref[...] = jnp.zeros_like(ref)
            for ref in (vbuf, vtmp):
                ref[0:PAD_P, :] = jnp.zeros((PAD_P, c), F32)
            for ref in (ebuf, etmp):
                ref[ts + HALO_P:ts + HALO_P + PAD_P, :] = jnp.zeros((PAD_P, c), F32)

        pooled, inv, gout = _pool_forward_tile(p_ref, halo_ref, tile_index == 0, tile_index, cw_ref, cb_ref, cs_ref,
                                               vbuf, vtmp, c, gc, ts)
        ebuf[ts:ts + HALO_P, :] = carry_e[...]
        for g, win in enumerate(POOL_WINDOWS):
            cols = slice(g * gc, (g + 1) * gc)
            z = p_ref[:, c + g * gc:c + (g + 1) * gc]
            sz = _sigmoid(z)
            du_g = du_ref[:, cols]
            scale = cs_ref[:, cols]
            d_y = du_g * (z * sz)
            dp_ref[:, c + g * gc:c + (g + 1) * gc] = (du_g * (gout[g] * scale) * _dsilu(z, sz)).astype(BF16)
            dcs_ref[:, cols] += _sum_rows(d_y * gout[g])
            d_gout = d_y * scale
            dcb_ref[:, cols] += _sum_rows(d_gout)
            d_gout_b = d_gout.astype(BF16)
            dcw_ref[g] += lax.dot_general(pooled[g], d_gout_b, (((0,), (0,)), ((), ())), preferred_element_type=F32)
            d_pool = lax.dot_general(d_gout_b, cw_ref[g], (((1,), (1,)), ((), ())), preferred_element_type=F32)
            e = d_pool * inv[g]
            ebuf[0:ts, cols] = e
            carry_e[:, cols] = e[0:HALO_P, :]
            d_v = _leading_sums(ebuf, etmp, cols, win, ts + HALO_P, ts) - d_pool
            dp_ref[:, cols] = d_v.astype(BF16)

    rev = lambda i: (nt - 1 - i, 0)
    small = [(N_GROUPS, gc, gc), (1, c), (1, c)]
    return pl.pallas_call(
        body, name=name, grid=(nt,),
        in_specs=[pl.BlockSpec((ts, 2 * c), rev),
                  pl.BlockSpec((HALO_P, c), lambda i: (jnp.maximum((nt - 1 - i) * (ts // HALO_P) - 1, 0), 0)),
                  pl.BlockSpec((ts, c), rev),
                  pl.BlockSpec((N_GROUPS, gc, gc), lambda i: (0, 0, 0)),
                  pl.BlockSpec((1, c), lambda i: (0, 0)), pl.BlockSpec((1, c), lambda i: (0, 0))],
        out_specs=[pl.BlockSpec((ts, 2 * c), rev),
                   pl.BlockSpec((N_GROUPS, gc, gc), lambda i: (0, 0, 0)),
                   pl.BlockSpec((1, c), lambda i: (0, 0)), pl.BlockSpec((1, c), lambda i: (0, 0))],
        out_shape=[jax.ShapeDtypeStruct((s, 2 * c), BF16)] + [jax.ShapeDtypeStruct(sh, F32) for sh in small],
        scratch_shapes=[pltpu.VMEM((PAD_P + HALO_P + ts, c), F32)] * 2 + [pltpu.VMEM((ts + HALO_P + PAD_P, c), F32)] * 2
        + [pltpu.VMEM((HALO_P, c), F32)],
        compiler_params=_params("arbitrary"),
    )(p, p, du, cw, cb, cs)


def _cast_into_slot(a, coords, name):
    r, cols = a.shape
    tr = _row_tile(r // 2, 1024)
    per = r // 2 // tr

    def body(co_ref, a_ref, o_ref):
        o_ref[0, 0] = a_ref[...].astype(BF16)

    return pl.pallas_call(
        body, name=name,
        grid_spec=pltpu.PrefetchScalarGridSpec(
            num_scalar_prefetch=1, grid=(2, per),
            in_specs=[pl.BlockSpec((tr, cols), lambda h, i, co: (h * per + i, 0))],
            out_specs=pl.BlockSpec((1, 1, tr, cols), lambda h, i, co: (co[0], h, i, 0))),
        out_shape=jax.ShapeDtypeStruct((N_SHARDS, 2, r // 2, cols), BF16),
        compiler_params=_params("arbitrary", "arbitrary"),
    )(coords, a)


def _chip_sum(g, other, coords, name):
    n_sh, _, r2, cols = g.shape
    tr = _row_tile(r2, 512)

    def body(co_ref, g_ref, o_ref, sum_ref, mine_ref):
        v = (g_ref[0, 0].astype(F32) + o_ref[0].astype(F32)).astype(BF16)
        sum_ref[0] = v

        @pl.when(pl.program_id(1) == co_ref[0])
        def _():
            mine_ref[0] = v

    piece = pl.BlockSpec((1, tr, cols), lambda i, s, co: (s, i, 0))
    return pl.pallas_call(
        body, name=name,
        grid_spec=pltpu.PrefetchScalarGridSpec(
            num_scalar_prefetch=1, grid=(r2 // tr, n_sh),
            in_specs=[pl.BlockSpec((1, 1, tr, cols), lambda i, s, co: (s, co[1], i, 0)), piece],
            out_specs=[piece, pl.BlockSpec((1, tr, cols), lambda i, s, co: (co[0], i, 0))]),
        out_shape=[jax.ShapeDtypeStruct((n_sh, r2, cols), BF16)] * 2,
        compiler_params=_params("arbitrary", "arbitrary"),
    )(coords, g, other)


def _shard_sum(pieces, coords, name):
    n_sh, r2, cols = pieces.shape
    tr = _row_tile(r2, 512)

    def body(co_ref, p_ref, o_ref):
        acc = p_ref[0].astype(F32)
        for k in range(1, n_sh):
            acc = acc + p_ref[k].astype(F32)
        o_ref[0] = acc

    return pl.pallas_call(
        body, name=name,
        grid_spec=pltpu.PrefetchScalarGridSpec(
            num_scalar_prefetch=1, grid=(r2 // tr,),
            in_specs=[pl.BlockSpec((n_sh, tr, cols), lambda i, co: (0, i, 0))],
            out_specs=pl.BlockSpec((1, tr, cols), lambda i, co: (co[1], i, 0))),
        out_shape=jax.ShapeDtypeStruct((2, r2, cols), F32),
        compiler_params=_params("arbitrary"),
    )(coords, pieces)


def _sum_small(a, name):
    n, r, cols = a.shape

    def body(a_ref, o_ref):
        acc = a_ref[0]
        for k in range(1, n):
            acc = acc + a_ref[k]
        o_ref[...] = acc

    return pl.pallas_call(
        body, name=name,
        in_specs=[pl.BlockSpec((n, r, cols), lambda: (0, 0, 0))],
        out_specs=pl.BlockSpec((r, cols), lambda: (0, 0)),
        out_shape=jax.ShapeDtypeStruct((r, cols), F32),
        compiler_params=_params(),
    )(a)


def _adamw_step(w, g, m, v):
    m = ADAM_B1 * m + (1.0 - ADAM_B1) * g
    v = ADAM_B2 * v + (1.0 - ADAM_B2) * (g * g)
    m_hat = m / (1.0 - ADAM_B1 ** ADAM_STEP)
    v_hat = v / (1.0 - ADAM_B2 ** ADAM_STEP)
    return -ADAM_LR * (m_hat / (jnp.sqrt(v_hat) + ADAM_EPS) + ADAM_WD * w), m, v


def _adamw(w, g, m, v, name):
    r, cols = w.shape
    tr = _row_tile(r, 256) if r % SUBLANES_BF16 == 0 else r

    def body(w_ref, g_ref, m_ref, v_ref, d_ref, nm_ref, nv_ref):
        d_ref[...], nm_ref[...], nv_ref[...] = _adamw_step(w_ref[...], g_ref[...], m_ref[...], v_ref[...])

    blk = pl.BlockSpec((tr, cols), lambda i: (i, 0))
    return pl.pallas_call(
        body, name=name, grid=(r // tr,),
        in_specs=[blk] * 4, out_specs=[blk] * 3,
        out_shape=[jax.ShapeDtypeStruct((r, cols), F32)] * 3,
        compiler_params=_params("arbitrary"),
    )(w, g, m, v)


def _adamw_many(ws, gs, ms, vs, name):
    n = len(ws)

    def body(*refs):
        w_refs, g_refs, m_refs, v_refs = (refs[t * n:(t + 1) * n] for t in range(4))
        d_refs, nm_refs, nv_refs = (refs[(4 + t) * n:(5 + t) * n] for t in range(3))
        for k in range(n):
            d_refs[k][...], nm_refs[k][...], nv_refs[k][...] = _adamw_step(
                w_refs[k][...], g_refs[k][...], m_refs[k][...], v_refs[k][...])

    specs = [pl.BlockSpec(a.shape, lambda: (0, 0)) for a in ws]
    out = pl.pallas_call(
        body, name=name, in_specs=specs * 4, out_specs=specs * 3,
        out_shape=[jax.ShapeDtypeStruct(a.shape, F32) for a in ws] * 3,
        compiler_params=_params(),
    )(*ws, *gs, *ms, *vs)
    return out[:n], out[n:2 * n], out[2 * n:]


def _place():
    x, y, c = lax.axis_index("x"), lax.axis_index("y"), lax.axis_index("c")
    other_chips = [(1 - x, y), (x, 1 - y), (1 - x, 1 - y)]
    return x, y, c, other_chips


def _chip(xy):
    return 2 * xy[0] + xy[1]


def _remote(src, dst, send_sem, recv_sem, to):
    return pltpu.make_async_remote_copy(src_ref=src, dst_ref=dst, send_sem=send_sem, recv_sem=recv_sem,
                                        device_id=to, device_id_type=MESH)


def _gather_ici(ctx, k, j, start):
    (x, y, c, chips), b, send, recv = ctx
    if j < 2:
        chip, to = (_chip((x, y)) if start else _chip(chips[j])), (*chips[j], c)
    else:
        chip = 2 * (x ^ c) + (y ^ (1 - c)) if start else _chip(chips[2])
        to = (x ^ (1 - c), y ^ c, c)
    blk = b[k].at[chip, c]
    return _remote(blk, blk, send.at[6 * k + j], recv.at[6 * k + j], to)


def _gather_d2d(ctx, k, j, start):
    (x, y, c, chips), b, send, recv = ctx
    blk = b[k].at[_chip(chips[j]), c if start else 1 - c]
    return _remote(blk, blk, send.at[6 * k + 3 + j], recv.at[6 * k + 3 + j], (x, y, 1 - c))


def _gather_small(ctx, n, j, start):
    (x, y, c, chips), b, send, recv = ctx
    blk = b[n].at[_chip((x, y)) if start else _chip(chips[j])]
    return _remote(blk, blk, send.at[6 * n + j], recv.at[6 * n + j], (*chips[j], c))


def _gather_neighbours_landed(ctx, k):
    for j in range(2):
        _gather_ici(ctx, k, j, False).wait_recv()
    _gather_ici(ctx, k, 2, True).start()
    for j in range(2):
        _gather_d2d(ctx, k, j, True).start()


def _gather_diagonal_landed(ctx, k):
    _gather_ici(ctx, k, 2, False).wait_recv()
    _gather_d2d(ctx, k, 2, True).start()


def _gather_comm(bufs, relay_at, forward_at):
    n = len(bufs)

    def start(srcs, b, send, recv):
        for k in range(n):
            for j in range(2):
                _gather_ici((_place(), b, send, recv), k, j, True).start()

    def relay(srcs, b, send, recv):
        for k in range(n):
            _gather_neighbours_landed((_place(), b, send, recv), k)

    def forward(srcs, b, send, recv):
        for k in range(n):
            _gather_diagonal_landed((_place(), b, send, recv), k)

    def finish(srcs, b, send, recv):
        ctx = (_place(), b, send, recv)
        for k in range(n):
            for j in range(3):
                _gather_d2d(ctx, k, j, False).wait_recv()
                _gather_ici(ctx, k, j, True).wait_send()
                _gather_d2d(ctx, k, j, True).wait_send()

    return _Comm([], bufs, 6 * n, [(0, start), (relay_at, relay), (forward_at, forward)], finish)


def _rmsnorm(x, gain, name):
    s, d = x.shape
    tm = _row_tile(s, 512)

    def body(x_ref, g_ref, h_ref):
        xv = x_ref[...]
        r = lax.rsqrt(_mean_last(xv * xv) + EPS)
        h_ref[...] = (xv * r * g_ref[...]).astype(BF16)

    return pl.pallas_call(
        body, name=name, grid=(s // tm,),
        in_specs=[pl.BlockSpec((tm, d), lambda i: (i, 0)), pl.BlockSpec((1, d), lambda i: (0, 0))],
        out_specs=pl.BlockSpec((tm, d), lambda i: (i, 0)),
        out_shape=jax.ShapeDtypeStruct((s, d), BF16),
        compiler_params=_params("arbitrary"),
    )(x, gain)


def _gathered_in_proj(h, bufs, small, order, name):
    s, d = h.shape
    n_sh, _, r2, ns = bufs[0].shape
    assert d == 2 * r2
    n = len(bufs)
    tm = _row_tile(s, 512)
    n_i = s // tm
    hook_i = max(n_i - 2, 0)
    n_sems = 6 * n + 3

    def body(order_ref, h_ref, *rest):
        p_ref = rest[n + 1]
        b = rest[n + 2:2 * n + 3]
        w_vmem, w_sems, send, recv = rest[2 * n + 3:]
        j, i = pl.program_id(0), pl.program_id(1)
        ctx = (_place(), b, send, recv)

        def fetch(q):
            return pltpu.make_async_copy(b[0].at[order_ref[q]], w_vmem.at[q % 2], w_sems.at[q % 2])

        @pl.when((j == 0) & (i == 0))
        def _():
            for k in range(n):
                for peer in range(2):
                    _gather_ici(ctx, k, peer, True).start()
            for peer in range(3):
                _gather_small(ctx, n, peer, True).start()
            fetch(0).start()
            fetch(0).wait()

        for q in range(1, n_sh):
            @pl.when((j == q - 1) & (i == hook_i))
            def _(q=q):
                if q == 1:
                    _gather_neighbours_landed(ctx, 0)
                if q == 2:
                    for k in range(1, n):
                        _gather_neighbours_landed(ctx, k)
                if q == 3:
                    for k in range(n):
                        _gather_diagonal_landed(ctx, k)
                _gather_d2d(ctx, 0, q - 1, False).wait_recv()
                fetch(q).start()

            @pl.when((j == q) & (i == 0))
            def _(q=q):
                fetch(q).wait()

        wv = w_vmem.at[j % 2]
        p_ref[...] = (jnp.dot(h_ref[:, 0:r2], wv[0], preferred_element_type=F32)
                      + jnp.dot(h_ref[:, r2:d], wv[1], preferred_element_type=F32))

        @pl.when((j == n_sh - 1) & (i == n_i - 1))
        def _():
            for peer in range(3):
                _gather_small(ctx, n, peer, False).wait_recv()
                _gather_small(ctx, n, peer, True).wait_send()
            for k in range(n):
                for peer in range(3):
                    if k > 0:
                        _gather_d2d(ctx, k, peer, False).wait_recv()
                    _gather_ici(ctx, k, peer, True).wait_send()
                    _gather_d2d(ctx, k, peer, True).wait_send()

    all_bufs = list(bufs) + [small]
    out = pl.pallas_call(
        body, name=name,
        grid_spec=pltpu.PrefetchScalarGridSpec(
            num_scalar_prefetch=1, grid=(n_sh, n_i),
            in_specs=[pl.BlockSpec((tm, d), lambda j, i, o: (i, 0))] + [ANY] * (n + 1),
            out_specs=[pl.BlockSpec((tm, ns), lambda j, i, o: (i, o[j]))] + [ANY] * (n + 1),
            scratch_shapes=[pltpu.VMEM((2, 2, r2, ns), BF16), pltpu.SemaphoreType.DMA((2,)),
                            pltpu.SemaphoreType.DMA((n_sems,)), pltpu.SemaphoreType.DMA((n_sems,))]),
        out_shape=[jax.ShapeDtypeStruct((s, n_sh * ns), F32)]
        + [jax.ShapeDtypeStruct(a.shape, a.dtype) for a in all_bufs],
        input_output_aliases={2 + t: 1 + t for t in range(n + 1)},
        compiler_params=_params("arbitrary", "arbitrary"),
    )(order, h, *all_bufs)
    return out[0], list(out[1:])


def _exchange_comm(grads):
    n = len(grads)
    landing = [lax.empty((N_SHARDS,) + a.shape[2:], a.dtype) for a in grads]

    def copies(srcs, b, send, recv):
        x, y, c, _ = _place()
        return [_remote(srcs[k].at[s, 1 - c], b[k].at[s], send.at[N_SHARDS * k + s], recv.at[N_SHARDS * k + s],
                        (x, y, 1 - c)) for k in range(n) for s in range(N_SHARDS)]

    def start(srcs, b, send, recv):
        for cp in copies(srcs, b, send, recv):
            cp.start()

    def finish(srcs, b, send, recv):
        for cp in copies(srcs, b, send, recv):
            cp.wait()

    return _Comm(grads, landing, N_SHARDS * n, [(0, start)], finish)


def _scatter_comm(chip_sums, landing):
    n = len(chip_sums)

    def big(srcs, b, send, recv, k, j, start):
        x, y, c, chips = _place()
        dst = b[k].at[_chip((x, y)) if start else _chip(chips[j])]
        return _remote(srcs[k].at[_chip(chips[j])], dst, send.at[3 * k + j], recv.at[3 * k + j], (*chips[j], c))

    def start(srcs, b, send, recv):
        for k in range(n):
            for j in range(3):
                big(srcs, b, send, recv, k, j, True).start()

    def finish(srcs, b, send, recv):
        for k in range(n):
            for j in range(3):
                big(srcs, b, send, recv, k, j, False).wait_recv()
                big(srcs, b, send, recv, k, j, True).wait_send()

    return _Comm(chip_sums, landing, 3 * n, [(0, start)], finish)


def _join_comm(halves, small):
    n = len(halves)
    flips = [(fx, fy, fc) for fx in (0, 1) for fy in (0, 1) for fc in (0, 1)][1:]

    def half(b, send, recv, k, start):
        x, y, c, _ = _place()
        return _remote(b[k].at[c], b[k].at[c if start else 1 - c], send.at[k], recv.at[k], (x, y, 1 - c))

    def small_copy(b, send, recv, q, start):
        x, y, c, _ = _place()
        px, py, pc = x ^ flips[q][0], y ^ flips[q][1], c ^ flips[q][2]
        blk = b[n].at[4 * x + 2 * y + c if start else 4 * px + 2 * py + pc]
        return _remote(blk, blk, send.at[n + q], recv.at[n + q], (px, py, pc))

    def start(srcs, b, send, recv):
        for q in range(len(flips)):
            small_copy(b, send, recv, q, True).start()
        for k in range(n):
            half(b, send, recv, k, True).start()

    def finish(srcs, b, send, recv):
        for q in range(len(flips)):
            small_copy(b, send, recv, q, False).wait()
        for k in range(n):
            half(b, send, recv, k, False).wait()

    return _Comm([], list(halves) + [small], n + len(flips), [(0, start)], finish)


def _flat_rows(parts):
    flat = jnp.concatenate([p.reshape(-1) for p in parts])
    assert flat.shape[0] % LANES == 0
    return flat.reshape(-1, LANES)


def _unflatten(flat, shapes):
    out, at = [], 0
    for sh in shapes:
        size = 1
        for dim in sh:
            size *= dim
        out.append(flat[at:at + size].reshape(sh))
        at += size
    assert at == flat.shape[0], (at, flat.shape)
    return out


def _col_shards_to_full(a, rows):
    q = a.shape[1] // rows
    return a.reshape(N_SHARDS, rows, q).transpose(1, 0, 2).reshape(rows, N_SHARDS * q)


def _my_col_shard(full, chip):
    rows, cols = full.shape
    q = cols // N_SHARDS
    return lax.dynamic_index_in_dim(full.reshape(rows, N_SHARDS, q), chip, axis=1, keepdims=False)


def kernel(x, e_norm_pre, e_norm_post, e_w_in, e_a_conv, e_b_conv, e_b_conv_bias, e_b_ln_g, e_b_ln_b, e_w_out, o_norm_pre, o_norm_post, o_w_in, o_c_w, o_c_b, o_c_scale, o_w_out, loss_target, m_e_norm_pre, m_e_norm_post, m_e_w_in, m_e_a_conv, m_e_b_conv, m_e_b_conv_bias, m_e_b_ln_g, m_e_b_ln_b, m_e_w_out, m_o_norm_pre, m_o_norm_post, m_o_w_in, m_o_c_w, m_o_c_b, m_o_c_scale, m_o_w_out, v_e_norm_pre, v_e_norm_post, v_e_w_in, v_e_a_conv, v_e_b_conv, v_e_b_conv_bias, v_e_b_ln_g, v_e_b_ln_b, v_e_w_out, v_o_norm_pre, v_o_norm_post, v_o_w_in, v_o_c_w, v_o_c_b, v_o_c_scale, v_o_w_out):
    _, s, d = x.shape
    w = d // 2
    c = d
    gc = c // N_GROUPS
    wq, cq, gq = w // N_SHARDS, c // N_SHARDS, gc // N_SHARDS
    chip = 2 * lax.axis_index("x") + lax.axis_index("y")
    core = lax.axis_index("c")
    x2 = x.reshape(s, d)
    target = loss_target.reshape(s, d)

    big_w = [e_w_in[0], e_w_out[0], o_w_in[0], o_c_w[0].reshape(N_GROUPS * gq, gc), o_w_out[0]]
    big_m = [m_e_w_in[0], m_e_w_out[0], m_o_w_in[0], m_o_c_w[0].reshape(N_GROUPS * gq, gc), m_o_w_out[0]]
    big_v = [v_e_w_in[0], v_e_w_out[0], v_o_w_in[0], v_o_c_w[0].reshape(N_GROUPS * gq, gc), v_o_w_out[0]]
    coords = jnp.stack([chip, core]).astype(jnp.int32)
    slots = [_cast_into_slot(a, coords, "cast_%d" % k) for k, a in enumerate(big_w)]
    sharded_small = _flat_rows([e_a_conv[0], e_b_conv[0], o_norm_pre, o_norm_post, o_c_scale, o_c_b[0]])
    small_slots = lax.dynamic_update_index_in_dim(jnp.zeros((N_SHARDS,) + sharded_small.shape, F32), sharded_small,
                                                  chip, 0)
    xi, yi = lax.axis_index("x"), lax.axis_index("y")
    order = jnp.stack([chip, 2 * (1 - xi) + yi, 2 * xi + (1 - yi), 2 * (1 - xi) + (1 - yi)]).astype(jnp.int32)
    h0 = _rmsnorm(x2, e_norm_pre, "e_pre_norm")
    p0, (e_w_in_g, e_w_out_g, small_g4) = _gathered_in_proj(h0, slots[:2], small_slots, order, "e_in_proj")
    e_w_in_sm = e_w_in_g.reshape((N_SHARDS,) + big_w[0].shape)
    e_w_out_f = e_w_out_g.reshape(w + w, d)
    sm = small_g4.reshape(N_SHARDS, -1)
    at = [0]

    def take(rows, q):
        blk = sm[:, at[0]:at[0] + rows * q]
        at[0] += rows * q
        return _col_shards_to_full(blk, rows)

    a_conv_f = take(CONV_A, wq)
    b_conv_f = take(CONV_B, wq)
    o_pre_f = take(1, cq)
    o_post_f = take(1, cq)
    cs_f = take(1, cq)
    cb_f = take(N_GROUPS, gq).reshape(1, c)

    mixer_steps = s // _row_tile(s, 128)
    (u0, x1, y0), odd_g = _even_mixer_fwd(p0, e_w_out_f, x2, e_norm_post, a_conv_f, b_conv_f, e_b_conv_bias, e_b_ln_g,
                                          e_b_ln_b, "e_mixer_out_proj",
                                          comm=_gather_comm(slots[2:], mixer_steps // 2, (25 * mixer_steps) // 32))
    o_w_in_sm = odd_g[0].reshape((N_SHARDS,) + big_w[2].shape)
    cw_f = odd_g[1].reshape(N_SHARDS, N_GROUPS, gq, gc).transpose(1, 0, 2, 3).reshape(N_GROUPS, gc, gc)
    o_w_out_f = odd_g[2].reshape(c, d)
    p1, h1 = _norm_matmul(x1, o_pre_f, o_w_in_sm, "o_in_proj")
    u1 = _odd_mixer_fwd(p1, cw_f, cb_f, cs_f, "o_mixer_fwd")
    d_y1, d_x2, d_o_post, loss_part = _matmul_post_loss(u1, o_w_out_f, x1, o_post_f, target, "o_out_proj_loss")

    def as_pieces(g, k):
        return g.reshape(N_SHARDS, 2, big_w[k].shape[0] // 2, big_w[k].shape[1])

    def chip_sums(ks, pieces, from_sibling):
        both = [_chip_sum(g, o, coords, "chip_sum_%d" % k) for k, g, o in zip(ks, pieces, from_sibling)]
        return [b[0] for b in both], [b[1] for b in both]

    g_o_w_out = _matmul_tn(u1, d_y1, 1, "o_w_out_grad")
    d_u1, _ = _matmul_nt(d_y1, o_w_out_f[None], "o_out_proj_bwd")
    d_p1, d_cw, d_cb, d_cs = _odd_mixer_bwd(p1, d_u1, cw_f, cb_f, cs_f, "o_mixer_bwd")
    g_o_w_in = _matmul_tn(h1, d_p1, N_SHARDS, "o_w_in_grad")
    g_cw = d_cw.reshape(N_GROUPS, N_SHARDS, gq, gc).transpose(1, 0, 2, 3).astype(BF16)
    pieces_o = [as_pieces(g_o_w_in, 2), as_pieces(g_cw, 3), as_pieces(g_o_w_out, 4)]
    d_h1, sibling_o = _matmul_nt(d_p1, o_w_in_sm, "o_in_proj_bwd", comm=_exchange_comm(pieces_o))
    d_x1, d_o_pre, d_y0, d_e_post = _norm_bwd(d_h1, x1, o_pre_f, d_x2, "o_pre_norm_bwd", post=(y0, e_norm_post))

    pieces_e = [as_pieces(_matmul_tn(u0, d_y0, 1, "e_w_out_grad"), 1)]
    d_u0, sibling_e = _matmul_nt(d_y0, e_w_out_f[None], "e_out_proj_bwd", comm=_exchange_comm(pieces_e))
    sums_a, landing_a = chip_sums([1, 2, 3, 4], pieces_e + pieces_o, sibling_e + sibling_o)
    (d_p0, d_a_conv, d_b_conv, d_bias, d_ln_g, d_ln_b), landed_a = _even_mixer_bwd(
        p0, d_u0, a_conv_f, b_conv_f, e_b_conv_bias, e_b_ln_g, e_b_ln_b, "e_mixer_bwd",
        comm=_scatter_comm(sums_a, landing_a))
    pieces_b = [as_pieces(_matmul_tn(h0, d_p0, N_SHARDS, "e_w_in_grad"), 0)]
    sums_b, landing_b = chip_sums([0], pieces_b, _comm_only(_exchange_comm(pieces_b), "exchange_core_halves"))
    d_h0, landed_b = _matmul_nt(d_p0, e_w_in_sm, "e_in_proj_bwd", comm=_scatter_comm(sums_b, landing_b))
    grad_x, d_e_pre = _norm_bwd(d_h0, x2, e_norm_pre, d_x1, "e_pre_norm_bwd")

    landed = landed_b + landed_a
    reduced = [_shard_sum(sc, coords, "shard_sum_%d" % k) for k, sc in enumerate(landed)]
    small_parts = _flat_rows([loss_part[0], d_e_pre, d_e_post, d_bias, d_ln_g, d_ln_b, d_a_conv, d_b_conv,
                              d_o_pre, d_o_post, d_cs, d_cb])
    small_rows = lax.dynamic_update_index_in_dim(jnp.zeros((N_DEVICES,) + small_parts.shape, F32), small_parts,
                                                 2 * chip + core, 0)
    joined = _comm_only(_join_comm(reduced, small_rows), "join_core_halves")
    big_g = [j.reshape(a.shape) for j, a in zip(joined[:5], big_w)]
    small_sum = _sum_small(joined[5], "small_sum").reshape(-1)
    (loss_row, g_e_pre, g_e_post, g_bias, g_ln_g, g_ln_b, g_a_conv_f, g_b_conv_f, g_o_pre_f, g_o_post_f, g_cs_f,
     g_cb_f) = _unflatten(small_sum, [(LANES,), (1, d), (1, d), (1, w), (1, w), (1, w), (CONV_A, w), (CONV_B, w),
                                      (1, c), (1, c), (1, c), (1, c)])
    loss = loss_row[0]
    g_a_conv = _my_col_shard(g_a_conv_f, chip)
    g_b_conv = _my_col_shard(g_b_conv_f, chip)
    g_o_pre = _my_col_shard(g_o_pre_f, chip)
    g_o_post = _my_col_shard(g_o_post_f, chip)
    g_cs = _my_col_shard(g_cs_f, chip)
    g_cb = _my_col_shard(g_cb_f.reshape(N_GROUPS, gc), chip)

    big_upd = [_adamw(wt, g, m, v, "adamw_%d" % k) for k, (wt, g, m, v) in enumerate(zip(big_w, big_g, big_m, big_v))]
    small_w = [e_norm_pre, e_norm_post, e_b_conv_bias, e_b_ln_g, e_b_ln_b, e_a_conv[0], e_b_conv[0],
               o_norm_pre, o_norm_post, o_c_b[0], o_c_scale]
    small_m = [m_e_norm_pre, m_e_norm_post, m_e_b_conv_bias, m_e_b_ln_g, m_e_b_ln_b, m_e_a_conv[0], m_e_b_conv[0],
               m_o_norm_pre, m_o_norm_post, m_o_c_b[0], m_o_c_scale]
    small_v = [v_e_norm_pre, v_e_norm_post, v_e_b_conv_bias, v_e_b_ln_g, v_e_b_ln_b, v_e_a_conv[0], v_e_b_conv[0],
               v_o_norm_pre, v_o_norm_post, v_o_c_b[0], v_o_c_scale]
    small_g = [g_e_pre, g_e_post, g_bias, g_ln_g, g_ln_b, g_a_conv, g_b_conv, g_o_pre, g_o_post, g_cb, g_cs]
    small_delta, small_new_m, small_new_v = _adamw_many(small_w, small_g, small_m, small_v, "adamw_small")

    def ordered(small, big):
        (n_pre, n_post, bias, ln_g, ln_b, a_conv, b_conv, o_pre, o_post, cb, cs) = small
        (w_in, w_out, ow_in, cw, ow_out) = big
        return [n_pre, n_post, w_in[None], a_conv[None], b_conv[None], bias, ln_g, ln_b, w_out[None], o_pre, o_post,
                ow_in[None], cw.reshape(1, N_GROUPS, gq, gc), cb[None], cs, ow_out[None]]

    grads = ordered(small_g, big_g)
    deltas = ordered(small_delta, [u[0] for u in big_upd])
    new_m = ordered(small_new_m, [u[1] for u in big_upd])
    new_v = ordered(small_new_v, [u[2] for u in big_upd])
    return (loss, grad_x.reshape(1, s, d), *grads, *deltas, *new_m, *new_v)
```

```python
import functools

import jax
import jax.numpy as jnp
from jax import lax
from jax.experimental import pallas as pl
from jax.experimental.pallas import tpu as pltpu

F32 = jnp.float32
BF16 = jnp.bfloat16
MESH = pl.DeviceIdType.MESH

EPS = 1e-6
CONV_A = 3
CONV_B = 31
POOL_WINDOWS = (2, 4, 8, 16)
N_GROUPS = len(POOL_WINDOWS)
N_SHARDS = 4
N_DEVICES = 8
ADAM_LR = 0.001
ADAM_B1 = 0.9
ADAM_B2 = 0.999
ADAM_EPS = 1e-08
ADAM_WD = 0.01
ADAM_STEP = 10

LANES = 128
SUBLANES_BF16 = 16
HALO_A = 8
HALO_B = 32
HALO_P = 16
PAD_P = 8
SHIFTS = 8
ROW_BLOCK = 32
LANE_BLOCK = 256
LANE_PASS = 256
VMEM_LIMIT = 56 * 1024 * 1024
TN_ACC_BYTES = 8 * 1024 * 1024
EVEN_BWD_ROWS = 128


def _row_tile(n, pref):
    t = max(min(n, pref) // SUBLANES_BF16, 1) * SUBLANES_BF16
    while t > SUBLANES_BF16 and (n % t or t % SUBLANES_BF16):
        t -= SUBLANES_BF16
    assert n % t == 0, (n, pref)
    return t


def _col_chunk(n, pref):
    t = (min(n, pref) // LANES) * LANES
    while t > LANES and n % t:
        t -= LANES
    assert t >= LANES and n % t == 0, (n, pref)
    return t


def _params(*sem):
    return pltpu.CompilerParams(dimension_semantics=tuple(sem) if sem else None, vmem_limit_bytes=VMEM_LIMIT)


ANY = pl.BlockSpec(memory_space=pl.ANY)


class _Comm:
    def __init__(self, srcs, bufs, n_sems, phases, finish):
        self.srcs, self.bufs, self.n_sems, self.phases, self.finish = list(srcs), list(bufs), n_sems, phases, finish


def _call(body, *, name, grid, in_specs, out_specs, out_shape, args, scratch_shapes=(), comm=None):
    params = _params(*(("arbitrary",) * len(grid)))
    if comm is None:
        out = pl.pallas_call(body, name=name, grid=grid, in_specs=in_specs, out_specs=out_specs, out_shape=out_shape,
                             scratch_shapes=scratch_shapes, compiler_params=params)(*args)
        return list(out), []
    n_in, n_out, n_scr = len(in_specs), len(out_specs), len(scratch_shapes)
    ns, nb = len(comm.srcs), len(comm.bufs)
    total = 1
    for size in grid:
        total *= size

    def fused(*refs):
        ins, srcs = refs[:n_in], refs[n_in:n_in + ns]
        at = n_in + ns + nb
        outs, bufs = refs[at:at + n_out], refs[at + n_out:at + n_out + nb]
        scratch = refs[at + n_out + nb:at + n_out + nb + n_scr]
        send_sems, recv_sems = refs[-2:]
        step = 0
        for axis, size in enumerate(grid):
            step = step * size + pl.program_id(axis)
        for when, fn in comm.phases:
            pl.when(step == when)(functools.partial(fn, srcs, bufs, send_sems, recv_sems))
        body(*ins, *outs, *scratch)
        pl.when(step == total - 1)(functools.partial(comm.finish, srcs, bufs, send_sems, recv_sems))

    out = pl.pallas_call(
        fused, name=name, grid=grid,
        in_specs=list(in_specs) + [ANY] * (ns + nb), out_specs=list(out_specs) + [ANY] * nb,
        out_shape=list(out_shape) + [jax.ShapeDtypeStruct(b.shape, b.dtype) for b in comm.bufs],
        input_output_aliases={n_in + ns + i: n_out + i for i in range(nb)},
        scratch_shapes=list(scratch_shapes) + [pltpu.SemaphoreType.DMA((comm.n_sems,))] * 2,
        compiler_params=params,
    )(*args, *comm.srcs, *comm.bufs)
    return list(out[:n_out]), list(out[n_out:])


def _comm_only(comm, name):
    ns, nb = len(comm.srcs), len(comm.bufs)

    def body(*refs):
        srcs, bufs = refs[:ns], refs[ns + nb:ns + 2 * nb]
        send_sems, recv_sems = refs[-2:]
        for _, fn in comm.phases:
            fn(srcs, bufs, send_sems, recv_sems)
        comm.finish(srcs, bufs, send_sems, recv_sems)

    return pl.pallas_call(
        body, name=name, in_specs=[ANY] * (ns + nb), out_specs=[ANY] * nb,
        out_shape=[jax.ShapeDtypeStruct(b.shape, b.dtype) for b in comm.bufs],
        input_output_aliases={ns + i: i for i in range(nb)},
        scratch_shapes=[pltpu.SemaphoreType.DMA((comm.n_sems,))] * 2,
    )(*comm.srcs, *comm.bufs)


def _sigmoid(v):
    return jax.nn.sigmoid(v)


def _dsilu(v, s):
    return s * (1.0 + v * (1.0 - s))


def _mean_last(v):
    return jnp.mean(v, axis=-1, keepdims=True)


def _sum_rows(v):
    return jnp.sum(v, axis=0, keepdims=True)


def _norm_matmul(x, gain, w_sm, name):
    s, d = x.shape
    n_sh, _, ns = w_sm.shape
    tm = _row_tile(s, 1024)

    def body(x_ref, g_ref, w_ref, p_ref, h_ref):
        @pl.when(pl.program_id(1) == 0)
        def _():
            xv = x_ref[...]
            r = lax.rsqrt(_mean_last(xv * xv) + EPS)
            h_ref[...] = (xv * r * g_ref[...]).astype(BF16)

        p_ref[...] = jnp.dot(h_ref[...], w_ref[0], preferred_element_type=F32)

    return _call(
        body, name=name, grid=(s // tm, n_sh),
        in_specs=[pl.BlockSpec((tm, d), lambda i, j: (i, 0)),
                  pl.BlockSpec((1, d), lambda i, j: (0, 0)),
                  pl.BlockSpec((1, d, ns), lambda i, j: (j, 0, 0))],
        out_specs=[pl.BlockSpec((tm, ns), lambda i, j: (i, j)),
                   pl.BlockSpec((tm, d), lambda i, j: (i, 0))],
        out_shape=[jax.ShapeDtypeStruct((s, n_sh * ns), F32), jax.ShapeDtypeStruct((s, d), BF16)],
        args=(x, gain, w_sm))[0]


def _matmul_post_loss(u, w, x_res, gain, target, name):
    s, k = u.shape
    d = w.shape[1]
    tm = _row_tile(s, 256)

    def body(u_ref, w_ref, x_ref, g_ref, t_ref, dy_ref, dout_ref, dg_ref, loss_ref):
        @pl.when(pl.program_id(0) == 0)
        def _():
            dg_ref[...] = jnp.zeros_like(dg_ref)
            loss_ref[...] = jnp.zeros_like(loss_ref)

        y = jnp.dot(u_ref[...], w_ref[...], preferred_element_type=F32)
        r = lax.rsqrt(_mean_last(y * y) + EPS)
        n = y * r
        g = g_ref[...]
        err = x_ref[...] + n * g - t_ref[...]
        loss_ref[...] += 0.5 * jnp.sum(_mean_last(err * err))
        dout = err * (1.0 / d)
        dout_ref[...] = dout
        dg_ref[...] += _sum_rows(dout * n)
        dn = dout * g
        dy_ref[...] = (r * (dn - n * _mean_last(dn * n))).astype(BF16)

    return pl.pallas_call(
        body, name=name, grid=(s // tm,),
        in_specs=[pl.BlockSpec((tm, k), lambda i: (i, 0)),
                  pl.BlockSpec((k, d), lambda i: (0, 0)),
                  pl.BlockSpec((tm, d), lambda i: (i, 0)),
                  pl.BlockSpec((1, d), lambda i: (0, 0)),
                  pl.BlockSpec((tm, d), lambda i: (i, 0))],
        out_specs=[pl.BlockSpec((tm, d), lambda i: (i, 0)),
                   pl.BlockSpec((tm, d), lambda i: (i, 0)),
                   pl.BlockSpec((1, d), lambda i: (0, 0)),
                   pl.BlockSpec((8, LANES), lambda i: (0, 0))],
        out_shape=[jax.ShapeDtypeStruct((s, d), BF16), jax.ShapeDtypeStruct((s, d), F32),
                   jax.ShapeDtypeStruct((1, d), F32), jax.ShapeDtypeStruct((8, LANES), F32)],
        compiler_params=_params("arbitrary"),
    )(u, w, x_res, gain, target)


def _matmul_nt(a, w_sm, name, comm=None):
    s, ncols = a.shape
    n_sh, r, ns = w_sm.shape
    assert ncols == n_sh * ns
    tm = _row_tile(s, 1024)
    nc = _col_chunk(ns, 1792)
    per = ns // nc
    steps = n_sh * per

    def body(a_ref, w_ref, o_ref):
        part = lax.dot_general(a_ref[...], w_ref[0], (((1,), (1,)), ((), ())), preferred_element_type=F32)

        @pl.when(pl.program_id(1) == 0)
        def _():
            o_ref[...] = part

        @pl.when(pl.program_id(1) > 0)
        def _():
            o_ref[...] += part

    out, bufs = _call(
        body, name=name, grid=(s // tm, steps),
        in_specs=[pl.BlockSpec((tm, nc), lambda i, j: (i, j)),
                  pl.BlockSpec((1, r, nc), lambda i, j: (j // per, 0, j % per))],
        out_specs=[pl.BlockSpec((tm, r), lambda i, j: (i, 0))],
        out_shape=[jax.ShapeDtypeStruct((s, r), F32)],
        args=(a, w_sm), comm=comm)
    return out[0], bufs


def _matmul_tn(a, b, n_sh, name, comm=None):
    s, k = a.shape
    n = b.shape[1]
    ns = n // n_sh
    tk = _col_chunk(k, TN_ACC_BYTES // (4 * ns))
    ts = _row_tile(s, 2048)
    n_s = s // ts

    def body(a_ref, b_ref, o_ref, acc_ref):
        part = lax.dot_general(a_ref[...], b_ref[...], (((0,), (0,)), ((), ())), preferred_element_type=F32)

        @pl.when(pl.program_id(2) == 0)
        def _():
            acc_ref[...] = part

        @pl.when(pl.program_id(2) > 0)
        def _():
            acc_ref[...] += part

        @pl.when(pl.program_id(2) == n_s - 1)
        def _():
            o_ref[0] = acc_ref[...].astype(BF16)

    out, bufs = _call(
        body, name=name, grid=(n_sh, k // tk, n_s),
        in_specs=[pl.BlockSpec((ts, tk), lambda j, i, t: (t, i)),
                  pl.BlockSpec((ts, ns), lambda j, i, t: (t, j))],
        out_specs=[pl.BlockSpec((1, tk, ns), lambda j, i, t: (j, i, 0))],
        out_shape=[jax.ShapeDtypeStruct((n_sh, k, ns), BF16)],
        scratch_shapes=[pltpu.VMEM((tk, ns), F32)],
        args=(a, b), comm=comm)
    return (out[0], bufs) if comm is not None else out[0]


def _norm_bwd(dh, x, gain, dres, name, post=None):
    s, d = x.shape
    tm = _row_tile(s, 256)
    with_post = post is not None

    def rms_bwd(dout, v, g):
        r = lax.rsqrt(_mean_last(v * v) + EPS)
        n = v * r
        dn = dout * g
        return r * (dn - n * _mean_last(dn * n)), _sum_rows(dout * n)

    def body(*refs):
        if with_post:
            dh_ref, x_ref, g_ref, dres_ref, y_ref, gp_ref, dx_ref, dg_ref, dy_ref, dgp_ref = refs
        else:
            dh_ref, x_ref, g_ref, dres_ref, dx_ref, dg_ref = refs

        @pl.when(pl.program_id(0) == 0)
        def _():
            dg_ref[...] = jnp.zeros_like(dg_ref)
            if with_post:
                dgp_ref[...] = jnp.zeros_like(dgp_ref)

        dv, dg = rms_bwd(dh_ref[...], x_ref[...], g_ref[...])
        dx = dres_ref[...] + dv
        dx_ref[...] = dx
        dg_ref[...] += dg
        if with_post:
            dy, dgp = rms_bwd(dx, y_ref[...], gp_ref[...])
            dy_ref[...] = dy.astype(BF16)
            dgp_ref[...] += dgp

    row = pl.BlockSpec((tm, d), lambda i: (i, 0))
    vec = pl.BlockSpec((1, d), lambda i: (0, 0))
    in_specs = [row, row, vec, row]
    out_specs = [row, vec]
    out_shape = [jax.ShapeDtypeStruct((s, d), F32), jax.ShapeDtypeStruct((1, d), F32)]
    args = [dh, x, gain, dres]
    if with_post:
        in_specs += [row, vec]
        out_specs += [row, vec]
        out_shape += [jax.ShapeDtypeStruct((s, d), BF16), jax.ShapeDtypeStruct((1, d), F32)]
        args += list(post)
    return _call(body, name=name, grid=(s // tm,), in_specs=in_specs, out_specs=out_specs, out_shape=out_shape,
                 args=args)[0]


def _fill_shifted_down(sh, rows):
    for b in range(1, SHIFTS):
        sh[b, SHIFTS:rows, :] = sh[0, SHIFTS - b:rows - b, :]


def _fill_shifted_up(sh, rows):
    for b in range(1, SHIFTS):
        sh[b, 0:rows - SHIFTS, :] = sh[0, b:rows - SHIFTS + b, :]


def _for_blocks(ts, w, fn):
    lb = min(LANE_BLOCK, w)
    for l0 in range(0, w, lb):
        def rows(rb, carry, l0=l0):
            fn(pl.multiple_of(rb * ROW_BLOCK, ROW_BLOCK), slice(l0, l0 + lb))
            return carry

        lax.fori_loop(0, ts // ROW_BLOCK, rows, 0)


TAP_SPAN = SHIFTS * ((CONV_B - 1) // SHIFTS)
WINDOW = ROW_BLOCK + TAP_SPAN


def _taps_of(b):
    return [(a, SHIFTS * a + b) for a in range((CONV_B - 1 - b) // SHIFTS + 1)]


def _conv31(sh, base, step, wt_ref, bias_ref, out_ref, ts, w):
    low = min(0, step * (TAP_SPAN // SHIFTS))

    def block(r0, lanes):
        acc = [jnp.zeros((SHIFTS, lanes.stop - lanes.start), F32) for _ in range(ROW_BLOCK // SHIFTS)]
        for b in range(SHIFTS):
            window = sh[b, pl.ds(pl.multiple_of(r0 + (base + low), SHIFTS), WINDOW), lanes]
            for a, j in (_taps_of(b) if step > 0 else reversed(_taps_of(b))):
                at = step * a - low
                wt = wt_ref[CONV_B - 1 - j, :, lanes]
                acc = [v + wt * window[at + SHIFTS * r:at + SHIFTS * (r + 1), :] for r, v in enumerate(acc)]
        for r, v in enumerate(acc):
            if bias_ref is not None:
                v = v + bias_ref[:, lanes]
            out_ref[pl.ds(pl.multiple_of(r0 + SHIFTS * r, SHIFTS), SHIFTS), lanes] = v

    _for_blocks(ts, w, block)


def _conv31_weight_grad(d_sh, x_sh, wacc, ts, w):
    def block(r0, lanes):
        d = d_sh[0, pl.ds(r0, ROW_BLOCK), lanes]
        for b in range(SHIFTS):
            window = x_sh[b, pl.ds(pl.multiple_of(r0 + (HALO_B - TAP_SPAN), SHIFTS), WINDOW), lanes]
            for a, j in _taps_of(b):
                at = TAP_SPAN - SHIFTS * a
                prod = d * window[at:at + ROW_BLOCK, :]
                part = prod[0:SHIFTS, :]
                for q in range(1, ROW_BLOCK // SHIFTS):
                    part = part + prod[q * SHIFTS:(q + 1) * SHIFTS, :]
                wacc[CONV_B - 1 - j, :, lanes] += part

    _for_blocks(ts, w, block)


def _even_forward_tile(p_ref, halo_ref, first, a_conv_ref, b_conv_ref, bias_ref, lng_ref, lnb_ref, qbuf, ysh, y1buf,
                       wb, w, ts):
    @pl.when(pl.program_id(0) == 0)
    def _():
        for k in range(CONV_B):
            wb[k] = jnp.broadcast_to(b_conv_ref[k:k + 1, :], (SHIFTS, w))

    def col(ref, k, rows=slice(None)):
        return ref[rows, k * w:(k + 1) * w]

    a_x, a_b, a_c, a_z = col(p_ref, 0), col(p_ref, 1), col(p_ref, 2), col(p_ref, 3)
    b_val, b_gate, b_z = col(p_ref, 4), col(p_ref, 5), col(p_ref, 6)
    keep = jnp.where(first, 0.0, 1.0)

    rows_a = slice(HALO_B - HALO_A, HALO_B)
    qbuf[0:HALO_A, :] = col(halo_ref, 2, rows_a) * col(halo_ref, 0, rows_a) * keep
    qbuf[HALO_A:HALO_A + ts, :] = a_c * a_x
    cq = jnp.zeros((ts, w), F32)
    for j in range(CONV_A):
        cq = cq + a_conv_ref[CONV_A - 1 - j:CONV_A - j, :] * qbuf[HALO_A - j:HALO_A - j + ts, :]
    ya = a_b * cq

    ysh[0, 0:HALO_B, :] = col(halo_ref, 4) * _sigmoid(col(halo_ref, 5)) * keep
    ysh[0, HALO_B:HALO_B + ts, :] = b_val * _sigmoid(b_gate)
    _fill_shifted_down(ysh, HALO_B + ts)
    _conv31(ysh, HALO_B, -SHIFTS, wb, bias_ref, y1buf, ts, w)
    yb1 = y1buf[...]
    xc = yb1 - _mean_last(yb1)
    rstd = lax.rsqrt(_mean_last(xc * xc) + EPS)
    xhat = xc * rstd
    yb2 = xhat * lng_ref[...] + lnb_ref[...]
    return dict(a_x=a_x, a_b=a_b, a_c=a_c, a_z=a_z, b_val=b_val, b_gate=b_gate, b_z=b_z,
                cq=cq, ya=ya, rstd=rstd, xhat=xhat, yb2=yb2)


def _even_specs(s, w, ts):
    tile = pl.BlockSpec((ts, 7 * w), lambda i: (i, 0))
    halo = pl.BlockSpec((HALO_B, 7 * w), lambda i: (jnp.maximum(i * (ts // HALO_B) - 1, 0), 0))
    return tile, halo


def _small_specs(shapes, index=lambda i: (0, 0)):
    return [pl.BlockSpec(sh, index) for sh in shapes]


def _even_mixer_fwd(p, w_out, x_res, gain, a_conv, b_conv, bias, ln_g, ln_b, name, comm=None):
    s, d = x_res.shape
    w = p.shape[1] // 7
    ts = _row_tile(s, 128)
    assert ts % HALO_B == 0

    def body(p_ref, halo_ref, wout_ref, x_ref, g_ref, ac_ref, bc_ref, bias_ref, lng_ref, lnb_ref,
             u_ref, xn_ref, y_ref, qbuf, ysh, y1buf, wb):
        first = pl.program_id(0) == 0
        f = _even_forward_tile(p_ref, halo_ref, first, ac_ref, bc_ref, bias_ref, lng_ref, lnb_ref, qbuf, ysh, y1buf,
                               wb, w, ts)
        yb3 = f["yb2"] * _sigmoid(f["yb2"])
        u_a = (f["ya"] * (f["a_z"] * _sigmoid(f["a_z"]))).astype(BF16)
        u_b = (yb3 * (f["b_z"] * _sigmoid(f["b_z"]))).astype(BF16)
        u_ref[:, 0:w] = u_a
        u_ref[:, w:2 * w] = u_b
        y = (jnp.dot(u_a, wout_ref[0:w, :], preferred_element_type=F32)
             + jnp.dot(u_b, wout_ref[w:2 * w, :], preferred_element_type=F32))
        r = lax.rsqrt(_mean_last(y * y) + EPS)
        y_ref[...] = y
        xn_ref[...] = x_ref[...] + (y * r) * g_ref[...]

    tile, halo = _even_specs(s, w, ts)
    row = pl.BlockSpec((ts, d), lambda i: (i, 0))
    return _call(
        body, name=name, grid=(s // ts,),
        in_specs=[tile, halo, pl.BlockSpec((2 * w, d), lambda i: (0, 0)), row, pl.BlockSpec((1, d), lambda i: (0, 0))]
        + _small_specs([(CONV_A, w), (CONV_B, w), (1, w), (1, w), (1, w)]),
        out_specs=[pl.BlockSpec((ts, 2 * w), lambda i: (i, 0)), row, row],
        out_shape=[jax.ShapeDtypeStruct((s, 2 * w), BF16), jax.ShapeDtypeStruct((s, d), F32),
                   jax.ShapeDtypeStruct((s, d), F32)],
        scratch_shapes=[pltpu.VMEM((HALO_A + ts, w), F32), pltpu.VMEM((SHIFTS, HALO_B + ts, w), F32),
                        pltpu.VMEM((ts, w), F32), pltpu.VMEM((CONV_B, SHIFTS, w), F32)],
        args=(p, p, w_out, x_res, gain, a_conv, b_conv, bias, ln_g, ln_b), comm=comm)


def _even_mixer_bwd(p, du, a_conv, b_conv, bias, ln_g, ln_b, name, comm=None):
    s = p.shape[0]
    w = p.shape[1] // 7
    ts = _row_tile(s, EVEN_BWD_ROWS)
    nt = s // ts
    assert ts % HALO_B == 0

    def body(p_ref, halo_ref, du_ref, ac_ref, bc_ref, bias_ref, lng_ref, lnb_ref,
             dp_ref, dac_ref, dbc_ref, dbias_ref, dlng_ref, dlnb_ref,
             qbuf, ysh, y1buf, dqbuf, dsh, dy0buf, wacc, carry_dq, carry_dy, wb):
        step = pl.program_id(0)
        first = step == nt - 1

        @pl.when(step == 0)
        def _():
            for ref in (dac_ref, dbias_ref, dlng_ref, dlnb_ref, wacc, carry_dq, carry_dy):
                ref[...] = jnp.zeros_like(ref)

        @pl.when(step == 0)
        def _():
            for k in range(CONV_B):
                wb[k] = jnp.broadcast_to(bc_ref[k:k + 1, :], (SHIFTS, w))

        keep = jnp.where(first, 0.0, 1.0)
        lane_blocks = [slice(l0, l0 + min(LANE_PASS, w)) for l0 in range(0, w, min(LANE_PASS, w))]

        def col(ref, k, lanes, rows=slice(None)):
            return ref[rows, k * w + lanes.start:k * w + lanes.stop]


        rows_a = slice(HALO_B - HALO_A, HALO_B)
        for lanes in lane_blocks:
            a_x, a_b, a_c, a_z = (col(p_ref, k, lanes) for k in range(4))
            qbuf[0:HALO_A, lanes] = col(halo_ref, 2, lanes, rows_a) * col(halo_ref, 0, lanes, rows_a) * keep
            qbuf[HALO_A:HALO_A + ts, lanes] = a_c * a_x
            cq = jnp.zeros((ts, lanes.stop - lanes.start), F32)
            for j in range(CONV_A):
                cq = cq + ac_ref[CONV_A - 1 - j:CONV_A - j, lanes] * qbuf[HALO_A - j:HALO_A - j + ts, lanes]
            s_az = _sigmoid(a_z)
            du_a = du_ref[:, lanes]
            d_ya = du_a * (a_z * s_az)
            dp_ref[:, 3 * w + lanes.start:3 * w + lanes.stop] = (du_a * (a_b * cq) * _dsilu(a_z, s_az)).astype(BF16)
            dp_ref[:, 1 * w + lanes.start:1 * w + lanes.stop] = (d_ya * cq).astype(BF16)
            d_cq = d_ya * a_b
            dqbuf[0:ts, lanes] = d_cq
            dqbuf[ts:ts + HALO_A, lanes] = carry_dq[:, lanes]
            carry_dq[:, lanes] = d_cq[0:HALO_A, :]
            d_q = jnp.zeros_like(cq)
            for o in range(CONV_A):
                d_q = d_q + ac_ref[CONV_A - 1 - o:CONV_A - o, lanes] * dqbuf[o:o + ts, lanes]
            for j in range(CONV_A):
                k = CONV_A - 1 - j
                dac_ref[k:k + 1, lanes] += _sum_rows(d_cq * qbuf[HALO_A - j:HALO_A - j + ts, lanes])
            dp_ref[:, 2 * w + lanes.start:2 * w + lanes.stop] = (d_q * a_x).astype(BF16)
            dp_ref[:, 0 * w + lanes.start:0 * w + lanes.stop] = (d_q * a_c).astype(BF16)
            ysh[0, 0:HALO_B, lanes] = col(halo_ref, 4, lanes) * _sigmoid(col(halo_ref, 5, lanes)) * keep
            ysh[0, HALO_B:HALO_B + ts, lanes] = col(p_ref, 4, lanes) * _sigmoid(col(p_ref, 5, lanes))
        _fill_shifted_down(ysh, HALO_B + ts)
        _conv31(ysh, HALO_B, -SHIFTS, wb, bias_ref, y1buf, ts, w)

        total = jnp.zeros((ts, 1), F32)
        for lanes in lane_blocks:
            total = total + jnp.sum(y1buf[:, lanes], axis=-1, keepdims=True)
        mu = total * (1.0 / w)
        total = jnp.zeros((ts, 1), F32)
        for lanes in lane_blocks:
            xc = y1buf[:, lanes] - mu
            total = total + jnp.sum(xc * xc, axis=-1, keepdims=True)
        rstd = lax.rsqrt(total * (1.0 / w) + EPS)

        sum_dxh = jnp.zeros((ts, 1), F32)
        sum_dxh_xhat = jnp.zeros((ts, 1), F32)
        for lanes in lane_blocks:
            xhat = (y1buf[:, lanes] - mu) * rstd
            yb2 = xhat * lng_ref[:, lanes] + lnb_ref[:, lanes]
            b_z = col(p_ref, 6, lanes)
            s_bz, s_y2 = _sigmoid(b_z), _sigmoid(yb2)
            du_b = du_ref[:, w + lanes.start:w + lanes.stop]
            dp_ref[:, 6 * w + lanes.start:6 * w + lanes.stop] = (du_b * (yb2 * s_y2) * _dsilu(b_z, s_bz)).astype(BF16)
            d_yb2 = du_b * (b_z * s_bz) * _dsilu(yb2, s_y2)
            dlng_ref[:, lanes] += _sum_rows(d_yb2 * xhat)
            dlnb_ref[:, lanes] += _sum_rows(d_yb2)
            d_xh = d_yb2 * lng_ref[:, lanes]
            sum_dxh = sum_dxh + jnp.sum(d_xh, axis=-1, keepdims=True)
            sum_dxh_xhat = sum_dxh_xhat + jnp.sum(d_xh * xhat, axis=-1, keepdims=True)
            y1buf[:, lanes] = xhat
            dy0buf[:, lanes] = d_xh
        mean_dxh = sum_dxh * (1.0 / w)
        mean_dxh_xhat = sum_dxh_xhat * (1.0 / w)

        for lanes in lane_blocks:
            d_yb1 = rstd * (dy0buf[:, lanes] - mean_dxh - y1buf[:, lanes] * mean_dxh_xhat)
            dbias_ref[:, lanes] += _sum_rows(d_yb1)
            dsh[0, 0:ts, lanes] = d_yb1
            dsh[0, ts:ts + HALO_B, lanes] = carry_dy[:, lanes]
            carry_dy[:, lanes] = d_yb1[0:HALO_B, :]
        _fill_shifted_up(dsh, ts + HALO_B)
        _conv31(dsh, 0, SHIFTS, wb, None, dy0buf, ts, w)
        _conv31_weight_grad(dsh, ysh, wacc, ts, w)

        @pl.when(step == nt - 1)
        def _():
            for k in range(CONV_B):
                dbc_ref[k:k + 1, :] = _sum_rows(wacc[k])

        for lanes in lane_blocks:
            d_yb0 = dy0buf[:, lanes]
            s_g = _sigmoid(col(p_ref, 5, lanes))
            dp_ref[:, 4 * w + lanes.start:4 * w + lanes.stop] = (d_yb0 * s_g).astype(BF16)
            dp_ref[:, 5 * w + lanes.start:5 * w + lanes.stop] = (
                d_yb0 * col(p_ref, 4, lanes) * s_g * (1.0 - s_g)).astype(BF16)

    rev = lambda i: (nt - 1 - i, 0)
    tile = pl.BlockSpec((ts, 7 * w), rev)
    halo = pl.BlockSpec((HALO_B, 7 * w), lambda i: (jnp.maximum((nt - 1 - i) * (ts // HALO_B) - 1, 0), 0))
    small = [(CONV_A, w), (CONV_B, w), (1, w), (1, w), (1, w)]
    return _call(
        body, name=name, grid=(nt,),
        in_specs=[tile, halo, pl.BlockSpec((ts, 2 * w), rev)] + _small_specs(small),
        out_specs=[pl.BlockSpec((ts, 7 * w), rev)] + _small_specs(small),
        out_shape=[jax.ShapeDtypeStruct((s, 7 * w), BF16)] + [jax.ShapeDtypeStruct(sh, F32) for sh in small],
        scratch_shapes=[pltpu.VMEM((HALO_A + ts, w), F32), pltpu.VMEM((SHIFTS, HALO_B + ts, w), F32),
                        pltpu.VMEM((ts, w), F32),
                        pltpu.VMEM((ts + HALO_A, w), F32), pltpu.VMEM((SHIFTS, ts + HALO_B, w), F32),
                        pltpu.VMEM((ts, w), F32), pltpu.VMEM((CONV_B, SHIFTS, w), F32),
                        pltpu.VMEM((HALO_A, w), F32), pltpu.VMEM((HALO_B, w), F32),
                        pltpu.VMEM((CONV_B, SHIFTS, w), F32)],
        args=(p, p, du, a_conv, b_conv, bias, ln_g, ln_b), comm=comm)


def _trailing_sums(buf_a, buf_b, cols, win, rows, first_out):
    src, dst, shift = buf_a, buf_b, 1
    while True:
        last = 2 * shift >= win
        lo = first_out if last else 0
        val = src[PAD_P + lo:PAD_P + rows, cols] + src[PAD_P + lo - shift:PAD_P + rows - shift, cols]
        if last:
            return val
        dst[PAD_P + lo:PAD_P + rows, cols] = val
        src, dst, shift = dst, src, 2 * shift


def _leading_sums(buf_a, buf_b, cols, win, rows, n_out):
    src, dst, shift = buf_a, buf_b, 1
    while True:
        last = 2 * shift >= win
        hi = n_out if last else rows
        val = src[0:hi, cols] + src[shift:hi + shift, cols]
        if last:
            return val
        dst[0:hi, cols] = val
        src, dst, shift = dst, src, 2 * shift


def _pool_forward_tile(p_ref, halo_ref, first, tile_index, cw_ref, cb_ref, cs_ref, vbuf, vtmp, c, gc, ts):
    vbuf[PAD_P:PAD_P + HALO_P, :] = halo_ref[...] * jnp.where(first, 0.0, 1.0)
    vbuf[PAD_P + HALO_P:PAD_P + HALO_P + ts, :] = p_ref[:, 0:c]
    pos = tile_index * ts + lax.broadcasted_iota(jnp.int32, (ts, 1), 0) + 1
    pooled, inv, gout = [], [], []
    for g, win in enumerate(POOL_WINDOWS):
        cols = slice(g * gc, (g + 1) * gc)
        acc = _trailing_sums(vbuf, vtmp, cols, win, HALO_P + ts, HALO_P)
        inv_g = 1.0 / jnp.minimum(pos, win).astype(F32)
        pooled_g = (acc * inv_g - p_ref[:, cols]).astype(BF16)
        pooled.append(pooled_g)
        inv.append(inv_g)
        gout.append(jnp.dot(pooled_g, cw_ref[g], preferred_element_type=F32) + cb_ref[:, cols])
    return pooled, inv, gout


def _odd_mixer_fwd(p, cw, cb, cs, name):
    s = p.shape[0]
    c = p.shape[1] // 2
    gc = c // N_GROUPS
    ts = _row_tile(s, 256)

    def body(p_ref, halo_ref, cw_ref, cb_ref, cs_ref, u_ref, vbuf, vtmp):
        i = pl.program_id(0)

        @pl.when(i == 0)
        def _():
            vbuf[0:PAD_P, :] = jnp.zeros((PAD_P, c), F32)
            vtmp[0:PAD_P, :] = jnp.zeros((PAD_P, c), F32)

        _, _, gout = _pool_forward_tile(p_ref, halo_ref, i == 0, i, cw_ref, cb_ref, cs_ref, vbuf, vtmp, c, gc, ts)
        for g in range(N_GROUPS):
            cols = slice(g * gc, (g + 1) * gc)
            z = p_ref[:, c + g * gc:c + (g + 1) * gc]
            u_ref[:, cols] = (gout[g] * cs_ref[:, cols] * (z * _sigmoid(z))).astype(BF16)

    return pl.pallas_call(
        body, name=name, grid=(s // ts,),
        in_specs=[pl.BlockSpec((ts, 2 * c), lambda i: (i, 0)),
                  pl.BlockSpec((HALO_P, c), lambda i: (jnp.maximum(i * (ts // HALO_P) - 1, 0), 0)),
                  pl.BlockSpec((N_GROUPS, gc, gc), lambda i: (0, 0, 0)),
                  pl.BlockSpec((1, c), lambda i: (0, 0)), pl.BlockSpec((1, c), lambda i: (0, 0))],
        out_specs=pl.BlockSpec((ts, c), lambda i: (i, 0)),
        out_shape=jax.ShapeDtypeStruct((s, c), BF16),
        scratch_shapes=[pltpu.VMEM((PAD_P + HALO_P + ts, c), F32)] * 2,
        compiler_params=_params("arbitrary"),
    )(p, p, cw, cb, cs)


def _odd_mixer_bwd(p, du, cw, cb, cs, name):
    s = p.shape[0]
    c = p.shape[1] // 2
    gc = c // N_GROUPS
    ts = _row_tile(s, 256)
    nt = s // ts

    def body(p_ref, halo_ref, du_ref, cw_ref, cb_ref, cs_ref, dp_ref, dcw_ref, dcb_ref, dcs_ref,
             vbuf, vtmp, ebuf, etmp, carry_e):
        step = pl.program_id(0)
        tile_index = nt - 1 - step

        @pl.when(step == 0)
        def _():
            for ref in (dcw_ref, dcb_ref, dcs_ref, carry_e):
                ref[...] = jnp.zeros_like(ref)
            for ref in (vbuf, vtmp):
                ref[0:PAD_P, :] = jnp.zeros((PAD_P, c), F32)
            for ref in (ebuf, etmp):
                ref[ts + HALO_P:ts + HALO_P + PAD_P, :] = jnp.zeros((PAD_P, c), F32)

        pooled, inv, gout = _pool_forward_tile(p_ref, halo_ref, tile_index == 0, tile_index, cw_ref, cb_ref, cs_ref,
                                               vbuf, vtmp, c, gc, ts)
        ebuf[ts:ts + HALO_P, :] = carry_e[...]
        for g, win in enumerate(POOL_WINDOWS):
            cols = slice(g * gc, (g + 1) * gc)
            z = p_ref[:, c + g * gc:c + (g + 1) * gc]
            sz = _sigmoid(z)
            du_g = du_ref[:, cols]
            scale = cs_ref[:, cols]
            d_y = du_g * (z * sz)
            dp_ref[:, c + g * gc:c + (g + 1) * gc] = (du_g * (gout[g] * scale) * _dsilu(z, sz)).astype(BF16)
            dcs_ref[:, cols] += _sum_rows(d_y * gout[g])
            d_gout = d_y * scale
            dcb_ref[:, cols] += _sum_rows(d_gout)
            d_gout_b = d_gout.astype(BF16)
            dcw_ref[g] += lax.dot_general(pooled[g], d_gout_b, (((0,), (0,)), ((), ())), preferred_element_type=F32)
            d_pool = lax.dot_general(d_gout_b, cw_ref[g], (((1,), (1,)), ((), ())), preferred_element_type=F32)
            e = d_pool * inv[g]
            ebuf[0:ts, cols] = e
            carry_e[:, cols] = e[0:HALO_P, :]
            d_v = _leading_sums(ebuf, etmp, cols, win, ts + HALO_P, ts) - d_pool
            dp_ref[:, cols] = d_v.astype(BF16)

    rev = lambda i: (nt - 1 - i, 0)
    small = [(N_GROUPS, gc, gc), (1, c), (1, c)]
    return pl.pallas_call(
        body, name=name, grid=(nt,),
        in_specs=[pl.BlockSpec((ts, 2 * c), rev),
                  pl.BlockSpec((HALO_P, c), lambda i: (jnp.maximum((nt - 1 - i) * (ts // HALO_P) - 1, 0), 0)),
                  pl.BlockSpec((ts, c), rev),
                  pl.BlockSpec((N_GROUPS, gc, gc), lambda i: (0, 0, 0)),
                  pl.BlockSpec((1, c), lambda i: (0, 0)), pl.BlockSpec((1, c), lambda i: (0, 0))],
        out_specs=[pl.BlockSpec((ts, 2 * c), rev),
                   pl.BlockSpec((N_GROUPS, gc, gc), lambda i: (0, 0, 0)),
                   pl.BlockSpec((1, c), lambda i: (0, 0)), pl.BlockSpec((1, c), lambda i: (0, 0))],
        out_shape=[jax.ShapeDtypeStruct((s, 2 * c), BF16)] + [jax.ShapeDtypeStruct(sh, F32) for sh in small],
        scratch_shapes=[pltpu.VMEM((PAD_P + HALO_P + ts, c), F32)] * 2 + [pltpu.VMEM((ts + HALO_P + PAD_P, c), F32)] * 2
        + [pltpu.VMEM((HALO_P, c), F32)],
        compiler_params=_params("arbitrary"),
    )(p, p, du, cw, cb, cs)


def _cast_into_slot(a, coords, name):
    r, cols = a.shape
    tr = _row_tile(r // 2, 1024)
    per = r // 2 // tr

    def body(co_ref, a_ref, o_ref):
        o_ref[0, 0] = a_ref[...].astype(BF16)

    return pl.pallas_call(
        body, name=name,
        grid_spec=pltpu.PrefetchScalarGridSpec(
            num_scalar_prefetch=1, grid=(2, per),
            in_specs=[pl.BlockSpec((tr, cols), lambda h, i, co: (h * per + i, 0))],
            out_specs=pl.BlockSpec((1, 1, tr, cols), lambda h, i, co: (co[0], h, i, 0))),
        out_shape=jax.ShapeDtypeStruct((N_SHARDS, 2, r // 2, cols), BF16),
        compiler_params=_params("arbitrary", "arbitrary"),
    )(coords, a)


def _chip_sum(g, other, coords, name):
    n_sh, _, r2, cols = g.shape
    tr = _row_tile(r2, 512)

    def body(co_ref, g_ref, o_ref, sum_ref, mine_ref):
        v = (g_ref[0, 0].astype(F32) + o_ref[0].astype(F32)).astype(BF16)
        sum_ref[0] = v

        @pl.when(pl.program_id(1) == co_ref[0])
        def _():
            mine_ref[0] = v

    piece = pl.BlockSpec((1, tr, cols), lambda i, s, co: (s, i, 0))
    return pl.pallas_call(
        body, name=name,
        grid_spec=pltpu.PrefetchScalarGridSpec(
            num_scalar_prefetch=1, grid=(r2 // tr, n_sh),
            in_specs=[pl.BlockSpec((1, 1, tr, cols), lambda i, s, co: (s, co[1], i, 0)), piece],
            out_specs=[piece, pl.BlockSpec((1, tr, cols), lambda i, s, co: (co[0], i, 0))]),
        out_shape=[jax.ShapeDtypeStruct((n_sh, r2, cols), BF16)] * 2,
        compiler_params=_params("arbitrary", "arbitrary"),
    )(coords, g, other)


def _shard_sum(pieces, coords, name):
    n_sh, r2, cols = pieces.shape
    tr = _row_tile(r2, 512)

    def body(co_ref, p_ref, o_ref):
        acc = p_ref[0].astype(F32)
        for k in range(1, n_sh):
            acc = acc + p_ref[k].astype(F32)
        o_ref[0] = acc

    return pl.pallas_call(
        body, name=name,
        grid_spec=pltpu.PrefetchScalarGridSpec(
            num_scalar_prefetch=1, grid=(r2 // tr,),
            in_specs=[pl.BlockSpec((n_sh, tr, cols), lambda i, co: (0, i, 0))],
            out_specs=pl.BlockSpec((1, tr, cols), lambda i, co: (co[1], i, 0))),
        out_shape=jax.ShapeDtypeStruct((2, r2, cols), F32),
        compiler_params=_params("arbitrary"),
    )(coords, pieces)


def _sum_small(a, name):
    n, r, cols = a.shape

    def body(a_ref, o_ref):
        acc = a_ref[0]
        for k in range(1, n):
            acc = acc + a_ref[k]
        o_ref[...] = acc

    return pl.pallas_call(
        body, name=name,
        in_specs=[pl.BlockSpec((n, r, cols), lambda: (0, 0, 0))],
        out_specs=pl.BlockSpec((r, cols), lambda: (0, 0)),
        out_shape=jax.ShapeDtypeStruct((r, cols), F32),
        compiler_params=_params(),
    )(a)


def _adamw_step(w, g, m, v):
    m = ADAM_B1 * m + (1.0 - ADAM_B1) * g
    v = ADAM_B2 * v + (1.0 - ADAM_B2) * (g * g)
    m_hat = m / (1.0 - ADAM_B1 ** ADAM_STEP)
    v_hat = v / (1.0 - ADAM_B2 ** ADAM_STEP)
    return -ADAM_LR * (m_hat / (jnp.sqrt(v_hat) + ADAM_EPS) + ADAM_WD * w), m, v


def _adamw(w, g, m, v, name):
    r, cols = w.shape
    tr = _row_tile(r, 256) if r % SUBLANES_BF16 == 0 else r

    def body(w_ref, g_ref, m_ref, v_ref, d_ref, nm_ref, nv_ref):
        d_ref[...], nm_ref[...], nv_ref[...] = _adamw_step(w_ref[...], g_ref[...], m_ref[...], v_ref[...])

    blk = pl.BlockSpec((tr, cols), lambda i: (i, 0))
    return pl.pallas_call(
        body, name=name, grid=(r // tr,),
        in_specs=[blk] * 4, out_specs=[blk] * 3,
        out_shape=[jax.ShapeDtypeStruct((r, cols), F32)] * 3,
        compiler_params=_params("arbitrary"),
    )(w, g, m, v)


def _adamw_many(ws, gs, ms, vs, name):
    n = len(ws)

    def body(*refs):
        w_refs, g_refs, m_refs, v_refs = (refs[t * n:(t + 1) * n] for t in range(4))
        d_refs, nm_refs, nv_refs = (refs[(4 + t) * n:(5 + t) * n] for t in range(3))
        for k in range(n):
            d_refs[k][...], nm_refs[k][...], nv_refs[k][...] = _adamw_step(
                w_refs[k][...], g_refs[k][...], m_refs[k][...], v_refs[k][...])

    specs = [pl.BlockSpec(a.shape, lambda: (0, 0)) for a in ws]
    out = pl.pallas_call(
        body, name=name, in_specs=specs * 4, out_specs=specs * 3,
        out_shape=[jax.ShapeDtypeStruct(a.shape, F32) for a in ws] * 3,
        compiler_params=_params(),
    )(*ws, *gs, *ms, *vs)
    return out[:n], out[n:2 * n], out[2 * n:]


def _place():
    x, y, c = lax.axis_index("x"), lax.axis_index("y"), lax.axis_index("c")
    other_chips = [(1 - x, y), (x, 1 - y), (1 - x, 1 - y)]
    return x, y, c, other_chips


def _chip(xy):
    return 2 * xy[0] + xy[1]


def _remote(src, dst, send_sem, recv_sem, to):
    return pltpu.make_async_remote_copy(src_ref=src, dst_ref=dst, send_sem=send_sem, recv_sem=recv_sem,
                                        device_id=to, device_id_type=MESH)


def _gather_ici(ctx, k, j, start):
    (x, y, c, chips), b, send, recv = ctx
    if j < 2:
        chip, to = (_chip((x, y)) if start else _chip(chips[j])), (*chips[j], c)
    else:
        chip = 2 * (x ^ c) + (y ^ (1 - c)) if start else _chip(chips[2])
        to = (x ^ (1 - c), y ^ c, c)
    blk = b[k].at[chip, c]
    return _remote(blk, blk, send.at[6 * k + j], recv.at[6 * k + j], to)


def _gather_d2d(ctx, k, j, start):
    (x, y, c, chips), b, send, recv = ctx
    blk = b[k].at[_chip(chips[j]), c if start else 1 - c]
    return _remote(blk, blk, send.at[6 * k + 3 + j], recv.at[6 * k + 3 + j], (x, y, 1 - c))


def _gather_small(ctx, n, j, start):
    (x, y, c, chips), b, send, recv = ctx
    blk = b[n].at[_chip((x, y)) if start else _chip(chips[j])]
    return _remote(blk, blk, send.at[6 * n + j], recv.at[6 * n + j], (*chips[j], c))


def _gather_neighbours_landed(ctx, k):
    for j in range(2):
        _gather_ici(ctx, k, j, False).wait_recv()
    _gather_ici(ctx, k, 2, True).start()
    for j in range(2):
        _gather_d2d(ctx, k, j, True).start()


def _gather_diagonal_landed(ctx, k):
    _gather_ici(ctx, k, 2, False).wait_recv()
    _gather_d2d(ctx, k, 2, True).start()


def _gather_comm(bufs, relay_at, forward_at):
    n = len(bufs)

    def start(srcs, b, send, recv):
        for k in range(n):
            for j in range(2):
                _gather_ici((_place(), b, send, recv), k, j, True).start()

    def relay(srcs, b, send, recv):
        for k in range(n):
            _gather_neighbours_landed((_place(), b, send, recv), k)

    def forward(srcs, b, send, recv):
        for k in range(n):
            _gather_diagonal_landed((_place(), b, send, recv), k)

    def finish(srcs, b, send, recv):
        ctx = (_place(), b, send, recv)
        for k in range(n):
            for j in range(3):
                _gather_d2d(ctx, k, j, False).wait_recv()
                _gather_ici(ctx, k, j, True).wait_send()
                _gather_d2d(ctx, k, j, True).wait_send()

    return _Comm([], bufs, 6 * n, [(0, start), (relay_at, relay), (forward_at, forward)], finish)


def _rmsnorm(x, gain, name):
    s, d = x.shape
    tm = _row_tile(s, 512)

    def body(x_ref, g_ref, h_ref):
        xv = x_ref[...]
        r = lax.rsqrt(_mean_last(xv * xv) + EPS)
        h_ref[...] = (xv * r * g_ref[...]).astype(BF16)

    return pl.pallas_call(
        body, name=name, grid=(s // tm,),
        in_specs=[pl.BlockSpec((tm, d), lambda i: (i, 0)), pl.BlockSpec((1, d), lambda i: (0, 0))],
        out_specs=pl.BlockSpec((tm, d), lambda i: (i, 0)),
        out_shape=jax.ShapeDtypeStruct((s, d), BF16),
        compiler_params=_params("arbitrary"),
    )(x, gain)


def _gathered_in_proj(h, bufs, small, order, name):
    s, d = h.shape
    n_sh, _, r2, ns = bufs[0].shape
    assert d == 2 * r2
    n = len(bufs)
    tm = _row_tile(s, 512)
    n_i = s // tm
    hook_i = max(n_i - 2, 0)
    n_sems = 6 * n + 3

    def body(order_ref, h_ref, *rest):
        p_ref = rest[n + 1]
        b = rest[n + 2:2 * n + 3]
        w_vmem, w_sems, send, recv = rest[2 * n + 3:]
        j, i = pl.program_id(0), pl.program_id(1)
        ctx = (_place(), b, send, recv)

        def fetch(q):
            return pltpu.make_async_copy(b[0].at[order_ref[q]], w_vmem.at[q % 2], w_sems.at[q % 2])

        @pl.when((j == 0) & (i == 0))
        def _():
            for k in range(n):
                for peer in range(2):
                    _gather_ici(ctx, k, peer, True).start()
            for peer in range(3):
                _gather_small(ctx, n, peer, True).start()
            fetch(0).start()
            fetch(0).wait()

        for q in range(1, n_sh):
            @pl.when((j == q - 1) & (i == hook_i))
            def _(q=q):
                if q == 1:
                    _gather_neighbours_landed(ctx, 0)
                if q == 2:
                    for k in range(1, n):
                        _gather_neighbours_landed(ctx, k)
                if q == 3:
                    for k in range(n):
                        _gather_diagonal_landed(ctx, k)
                _gather_d2d(ctx, 0, q - 1, False).wait_recv()
                fetch(q).start()

            @pl.when((j == q) & (i == 0))
            def _(q=q):
                fetch(q).wait()

        wv = w_vmem.at[j % 2]
        p_ref[...] = (jnp.dot(h_ref[:, 0:r2], wv[0], preferred_element_type=F32)
                      + jnp.dot(h_ref[:, r2:d], wv[1], preferred_element_type=F32))

        @pl.when((j == n_sh - 1) & (i == n_i - 1))
        def _():
            for peer in range(3):
                _gather_small(ctx, n, peer, False).wait_recv()
                _gather_small(ctx, n, peer, True).wait_send()
            for k in range(n):
                for peer in range(3):
                    if k > 0:
                        _gather_d2d(ctx, k, peer, False).wait_recv()
                    _gather_ici(ctx, k, peer, True).wait_send()
                    _gather_d2d(ctx, k, peer, True).wait_send()

    all_bufs = list(bufs) + [small]
    out = pl.pallas_call(
        body, name=name,
        grid_spec=pltpu.PrefetchScalarGridSpec(
            num_scalar_prefetch=1, grid=(n_sh, n_i),
            in_specs=[pl.BlockSpec((tm, d), lambda j, i, o: (i, 0))] + [ANY] * (n + 1),
            out_specs=[pl.BlockSpec((tm, ns), lambda j, i, o: (i, o[j]))] + [ANY] * (n + 1),
            scratch_shapes=[pltpu.VMEM((2, 2, r2, ns), BF16), pltpu.SemaphoreType.DMA((2,)),
                            pltpu.SemaphoreType.DMA((n_sems,)), pltpu.SemaphoreType.DMA((n_sems,))]),
        out_shape=[jax.ShapeDtypeStruct((s, n_sh * ns), F32)]
        + [jax.ShapeDtypeStruct(a.shape, a.dtype) for a in all_bufs],
        input_output_aliases={2 + t: 1 + t for t in range(n + 1)},
        compiler_params=_params("arbitrary", "arbitrary"),
    )(order, h, *all_bufs)
    return out[0], list(out[1:])


def _exchange_comm(grads):
    n = len(grads)
    landing = [lax.empty((N_SHARDS,) + a.shape[2:], a.dtype) for a in grads]

    def copies(srcs, b, send, recv):
        x, y, c, _ = _place()
        return [_remote(srcs[k].at[s, 1 - c], b[k].at[s], send.at[N_SHARDS * k + s], recv.at[N_SHARDS * k + s],
                        (x, y, 1 - c)) for k in range(n) for s in range(N_SHARDS)]

    def start(srcs, b, send, recv):
        for cp in copies(srcs, b, send, recv):
            cp.start()

    def finish(srcs, b, send, recv):
        for cp in copies(srcs, b, send, recv):
            cp.wait()

    return _Comm(grads, landing, N_SHARDS * n, [(0, start)], finish)


def _scatter_comm(chip_sums, landing):
    n = len(chip_sums)

    def big(srcs, b, send, recv, k, j, start):
        x, y, c, chips = _place()
        dst = b[k].at[_chip((x, y)) if start else _chip(chips[j])]
        return _remote(srcs[k].at[_chip(chips[j])], dst, send.at[3 * k + j], recv.at[3 * k + j], (*chips[j], c))

    def start(srcs, b, send, recv):
        for k in range(n):
            for j in range(3):
                big(srcs, b, send, recv, k, j, True).start()

    def finish(srcs, b, send, recv):
        for k in range(n):
            for j in range(3):
                big(srcs, b, send, recv, k, j, False).wait_recv()
                big(srcs, b, send, recv, k, j, True).wait_send()

    return _Comm(chip_sums, landing, 3 * n, [(0, start)], finish)


def _join_comm(halves, small=None):
    n = len(halves)
    flips = [(fx, fy, fc) for fx in (0, 1) for fy in (0, 1) for fc in (0, 1)][1:]
    if small is None:
        flips = []

    def half(b, send, recv, k, start):
        x, y, c, _ = _place()
        return _remote(b[k].at[c], b[k].at[c if start else 1 - c], send.at[k], recv.at[k], (x, y, 1 - c))

    def small_copy(b, send, recv, q, start):
        x, y, c, _ = _place()
        px, py, pc = x ^ flips[q][0], y ^ flips[q][1], c ^ flips[q][2]
        blk = b[n].at[4 * x + 2 * y + c if start else 4 * px + 2 * py + pc]
        return _remote(blk, blk, send.at[n + q], recv.at[n + q], (px, py, pc))

    def start(srcs, b, send, recv):
        for q in range(len(flips)):
            small_copy(b, send, recv, q, True).start()
        for k in range(n):
            half(b, send, recv, k, True).start()

    def finish(srcs, b, send, recv):
        for q in range(len(flips)):
            small_copy(b, send, recv, q, False).wait()
        for k in range(n):
            half(b, send, recv, k, False).wait()

    return _Comm([], list(halves) + ([] if small is None else [small]), n + len(flips), [(0, start)], finish)


def _flat_rows(parts):
    flat = jnp.concatenate([p.reshape(-1) for p in parts])
    assert flat.shape[0] % LANES == 0
    return flat.reshape(-1, LANES)


def _unflatten(flat, shapes):
    out, at = [], 0
    for sh in shapes:
        size = 1
        for dim in sh:
            size *= dim
        out.append(flat[at:at + size].reshape(sh))
        at += size
    assert at == flat.shape[0], (at, flat.shape)
    return out


def _col_shards_to_full(a, rows):
    q = a.shape[1] // rows
    return a.reshape(N_SHARDS, rows, q).transpose(1, 0, 2).reshape(rows, N_SHARDS * q)


def _my_col_shard(full, chip):
    rows, cols = full.shape
    q = cols // N_SHARDS
    return lax.dynamic_index_in_dim(full.reshape(rows, N_SHARDS, q), chip, axis=1, keepdims=False)


def kernel(x, e_norm_pre, e_norm_post, e_w_in, e_a_conv, e_b_conv, e_b_conv_bias, e_b_ln_g, e_b_ln_b, e_w_out, o_norm_pre, o_norm_post, o_w_in, o_c_w, o_c_b, o_c_scale, o_w_out, loss_target, m_e_norm_pre, m_e_norm_post, m_e_w_in, m_e_a_conv, m_e_b_conv, m_e_b_conv_bias, m_e_b_ln_g, m_e_b_ln_b, m_e_w_out, m_o_norm_pre, m_o_norm_post, m_o_w_in, m_o_c_w, m_o_c_b, m_o_c_scale, m_o_w_out, v_e_norm_pre, v_e_norm_post, v_e_w_in, v_e_a_conv, v_e_b_conv, v_e_b_conv_bias, v_e_b_ln_g, v_e_b_ln_b, v_e_w_out, v_o_norm_pre, v_o_norm_post, v_o_w_in, v_o_c_w, v_o_c_b, v_o_c_scale, v_o_w_out):
    _, s, d = x.shape
    w = d // 2
    c = d
    gc = c // N_GROUPS
    wq, cq, gq = w // N_SHARDS, c // N_SHARDS, gc // N_SHARDS
    chip = 2 * lax.axis_index("x") + lax.axis_index("y")
    core = lax.axis_index("c")
    x2 = x.reshape(s, d)
    target = loss_target.reshape(s, d)

    big_w = [e_w_in[0], e_w_out[0], o_w_in[0], o_c_w[0].reshape(N_GROUPS * gq, gc), o_w_out[0]]
    big_m = [m_e_w_in[0], m_e_w_out[0], m_o_w_in[0], m_o_c_w[0].reshape(N_GROUPS * gq, gc), m_o_w_out[0]]
    big_v = [v_e_w_in[0], v_e_w_out[0], v_o_w_in[0], v_o_c_w[0].reshape(N_GROUPS * gq, gc), v_o_w_out[0]]
    coords = jnp.stack([chip, core]).astype(jnp.int32)
    slots = [_cast_into_slot(a, coords, "cast_%d" % k) for k, a in enumerate(big_w)]
    sharded_small = _flat_rows([e_a_conv[0], e_b_conv[0], o_norm_pre, o_norm_post, o_c_scale, o_c_b[0]])
    small_slots = lax.dynamic_update_index_in_dim(jnp.zeros((N_SHARDS,) + sharded_small.shape, F32), sharded_small,
                                                  chip, 0)
    xi, yi = lax.axis_index("x"), lax.axis_index("y")
    order = jnp.stack([chip, 2 * (1 - xi) + yi, 2 * xi + (1 - yi), 2 * (1 - xi) + (1 - yi)]).astype(jnp.int32)
    h0 = _rmsnorm(x2, e_norm_pre, "e_pre_norm")
    p0, (e_w_in_g, e_w_out_g, small_g4) = _gathered_in_proj(h0, slots[:2], small_slots, order, "e_in_proj")
    e_w_in_sm = e_w_in_g.reshape((N_SHARDS,) + big_w[0].shape)
    e_w_out_f = e_w_out_g.reshape(w + w, d)
    sm = small_g4.reshape(N_SHARDS, -1)
    at = [0]

    def take(rows, q):
        blk = sm[:, at[0]:at[0] + rows * q]
        at[0] += rows * q
        return _col_shards_to_full(blk, rows)

    a_conv_f = take(CONV_A, wq)
    b_conv_f = take(CONV_B, wq)
    o_pre_f = take(1, cq)
    o_post_f = take(1, cq)
    cs_f = take(1, cq)
    cb_f = take(N_GROUPS, gq).reshape(1, c)

    mixer_steps = s // _row_tile(s, 128)
    (u0, x1, y0), odd_g = _even_mixer_fwd(p0, e_w_out_f, x2, e_norm_post, a_conv_f, b_conv_f, e_b_conv_bias, e_b_ln_g,
                                          e_b_ln_b, "e_mixer_out_proj",
                                          comm=_gather_comm(slots[2:], mixer_steps // 2, (25 * mixer_steps) // 32))
    o_w_in_sm = odd_g[0].reshape((N_SHARDS,) + big_w[2].shape)
    cw_f = odd_g[1].reshape(N_SHARDS, N_GROUPS, gq, gc).transpose(1, 0, 2, 3).reshape(N_GROUPS, gc, gc)
    o_w_out_f = odd_g[2].reshape(c, d)
    p1, h1 = _norm_matmul(x1, o_pre_f, o_w_in_sm, "o_in_proj")
    u1 = _odd_mixer_fwd(p1, cw_f, cb_f, cs_f, "o_mixer_fwd")
    d_y1, d_x2, d_o_post, loss_part = _matmul_post_loss(u1, o_w_out_f, x1, o_post_f, target, "o_out_proj_loss")

    def as_pieces(g, k):
        return g.reshape(N_SHARDS, 2, big_w[k].shape[0] // 2, big_w[k].shape[1])

    def chip_sums(ks, pieces, from_sibling):
        both = [_chip_sum(g, o, coords, "chip_sum_%d" % k) for k, g, o in zip(ks, pieces, from_sibling)]
        return [b[0] for b in both], [b[1] for b in both]

    g_o_w_out = _matmul_tn(u1, d_y1, 1, "o_w_out_grad")
    d_u1, _ = _matmul_nt(d_y1, o_w_out_f[None], "o_out_proj_bwd")
    d_p1, d_cw, d_cb, d_cs = _odd_mixer_bwd(p1, d_u1, cw_f, cb_f, cs_f, "o_mixer_bwd")
    g_o_w_in = _matmul_tn(h1, d_p1, N_SHARDS, "o_w_in_grad")
    g_cw = d_cw.reshape(N_GROUPS, N_SHARDS, gq, gc).transpose(1, 0, 2, 3).astype(BF16)
    pieces_o = [as_pieces(g_o_w_in, 2), as_pieces(g_cw, 3), as_pieces(g_o_w_out, 4)]
    d_h1, sibling_o = _matmul_nt(d_p1, o_w_in_sm, "o_in_proj_bwd", comm=_exchange_comm(pieces_o))
    d_x1, d_o_pre, d_y0, d_e_post = _norm_bwd(d_h1, x1, o_pre_f, d_x2, "o_pre_norm_bwd", post=(y0, e_norm_post))

    pieces_e = [as_pieces(_matmul_tn(u0, d_y0, 1, "e_w_out_grad"), 1)]
    d_u0, sibling_e = _matmul_nt(d_y0, e_w_out_f[None], "e_out_proj_bwd", comm=_exchange_comm(pieces_e))
    sums_a, landing_a = chip_sums([1, 2, 3, 4], pieces_e + pieces_o, sibling_e + sibling_o)
    (d_p0, d_a_conv, d_b_conv, d_bias, d_ln_g, d_ln_b), landed_a = _even_mixer_bwd(
        p0, d_u0, a_conv_f, b_conv_f, e_b_conv_bias, e_b_ln_g, e_b_ln_b, "e_mixer_bwd",
        comm=_scatter_comm(sums_a, landing_a))
    reduced_a = [_shard_sum(sc, coords, "shard_sum_%d" % (k + 1)) for k, sc in enumerate(landed_a)]
    g_e_w_in, joined_a = _matmul_tn(h0, d_p0, N_SHARDS, "e_w_in_grad", comm=_join_comm(reduced_a))
    pieces_b = [as_pieces(g_e_w_in, 0)]
    sums_b, landing_b = chip_sums([0], pieces_b, _comm_only(_exchange_comm(pieces_b), "exchange_core_halves"))
    d_h0, landed_b = _matmul_nt(d_p0, e_w_in_sm, "e_in_proj_bwd", comm=_scatter_comm(sums_b, landing_b))
    grad_x, d_e_pre = _norm_bwd(d_h0, x2, e_norm_pre, d_x1, "e_pre_norm_bwd")

    reduced_b = _shard_sum(landed_b[0], coords, "shard_sum_0")
    small_parts = _flat_rows([loss_part[0], d_e_pre, d_e_post, d_bias, d_ln_g, d_ln_b, d_a_conv, d_b_conv,
                              d_o_pre, d_o_post, d_cs, d_cb])
    small_rows = lax.dynamic_update_index_in_dim(jnp.zeros((N_DEVICES,) + small_parts.shape, F32), small_parts,
                                                 2 * chip + core, 0)
    joined_b, small_all = _comm_only(_join_comm([reduced_b], small_rows), "join_core_halves")
    big_g = [j.reshape(a.shape) for j, a in zip([joined_b] + joined_a, big_w)]
    small_sum = _sum_small(small_all, "small_sum").reshape(-1)
    (loss_row, g_e_pre, g_e_post, g_bias, g_ln_g, g_ln_b, g_a_conv_f, g_b_conv_f, g_o_pre_f, g_o_post_f, g_cs_f,
     g_cb_f) = _unflatten(small_sum, [(LANES,), (1, d), (1, d), (1, w), (1, w), (1, w), (CONV_A, w), (CONV_B, w),
                                      (1, c), (1, c), (1, c), (1, c)])
    loss = loss_row[0]
    g_a_conv = _my_col_shard(g_a_conv_f, chip)
    g_b_conv = _my_col_shard(g_b_conv_f, chip)
    g_o_pre = _my_col_shard(g_o_pre_f, chip)
    g_o_post = _my_col_shard(g_o_post_f, chip)
    g_cs = _my_col_shard(g_cs_f, chip)
    g_cb = _my_col_shard(g_cb_f.reshape(N_GROUPS, gc), chip)

    big_upd = [_adamw(wt, g, m, v, "adamw_%d" % k) for k, (wt, g, m, v) in enumerate(zip(big_w, big_g, big_m, big_v))]
    small_w = [e_norm_pre, e_norm_post, e_b_conv_bias, e_b_ln_g, e_b_ln_b, e_a_conv[0], e_b_conv[0],
               o_norm_pre, o_norm_post, o_c_b[0], o_c_scale]
    small_m = [m_e_norm_pre, m_e_norm_post, m_e_b_conv_bias, m_e_b_ln_g, m_e_b_ln_b, m_e_a_conv[0], m_e_b_conv[0],
               m_o_norm_pre, m_o_norm_post, m_o_c_b[0], m_o_c_scale]
    small_v = [v_e_norm_pre, v_e_norm_post, v_e_b_conv_bias, v_e_b_ln_g, v_e_b_ln_b, v_e_a_conv[0], v_e_b_conv[0],
               v_o_norm_pre, v_o_norm_post, v_o_c_b[0], v_o_c_scale]
    small_g = [g_e_pre, g_e_post, g_bias, g_ln_g, g_ln_b, g_a_conv, g_b_conv, g_o_pre, g_o_post, g_cb, g_cs]
    small_delta, small_new_m, small_new_v = _adamw_many(small_w, small_g, small_m, small_v, "adamw_small")

    def ordered(small, big):
        (n_pre, n_post, bias, ln_g, ln_b, a_conv, b_conv, o_pre, o_post, cb, cs) = small
        (w_in, w_out, ow_in, cw, ow_out) = big
        return [n_pre, n_post, w_in[None], a_conv[None], b_conv[None], bias, ln_g, ln_b, w_out[None], o_pre, o_post,
                ow_in[None], cw.reshape(1, N_GROUPS, gq, gc), cb[None], cs, ow_out[None]]

    grads = ordered(small_g, big_g)
    deltas = ordered(small_delta, [u[0] for u in big_upd])
    new_m = ordered(small_new_m, [u[1] for u in big_upd])
    new_v = ordered(small_new_v, [u[2] for u in big_upd])
    return (loss, grad_x.reshape(1, s, d), *grads, *deltas, *new_m, *new_v)
```

```python
import functools

import jax
import jax.numpy as jnp
from jax import lax
from jax.experimental import pallas as pl
from jax.experimental.pallas import tpu as pltpu

F32 = jnp.float32
BF16 = jnp.bfloat16
MESH = pl.DeviceIdType.MESH

EPS = 1e-6
CONV_A = 3
CONV_B = 31
POOL_WINDOWS = (2, 4, 8, 16)
N_GROUPS = len(POOL_WINDOWS)
N_SHARDS = 4
N_DEVICES = 8
ADAM_LR = 0.001
ADAM_B1 = 0.9
ADAM_B2 = 0.999
ADAM_EPS = 1e-08
ADAM_WD = 0.01
ADAM_STEP = 10

LANES = 128
SUBLANES_BF16 = 16
HALO_A = 8
HALO_B = 32
HALO_P = 16
PAD_P = 8
SHIFTS = 8
ROW_BLOCK = 32
LANE_BLOCK = 256
LANE_PASS = 256
VMEM_LIMIT = 56 * 1024 * 1024
TN_ACC_BYTES = 8 * 1024 * 1024
EVEN_BWD_ROWS = 128


def _row_tile(n, pref):
    t = max(min(n, pref) // SUBLANES_BF16, 1) * SUBLANES_BF16
    while t > SUBLANES_BF16 and (n % t or t % SUBLANES_BF16):
        t -= SUBLANES_BF16
    assert n % t == 0, (n, pref)
    return t


def _col_chunk(n, pref):
    t = (min(n, pref) // LANES) * LANES
    while t > LANES and n % t:
        t -= LANES
    assert t >= LANES and n % t == 0, (n, pref)
    return t


def _params(*sem):
    return pltpu.CompilerParams(dimension_semantics=tuple(sem) if sem else None, vmem_limit_bytes=VMEM_LIMIT)


ANY = pl.BlockSpec(memory_space=pl.ANY)


class _Comm:
    def __init__(self, srcs, bufs, n_sems, phases, finish):
        self.srcs, self.bufs, self.n_sems, self.phases, self.finish = list(srcs), list(bufs), n_sems, phases, finish


def _call(body, *, name, grid, in_specs, out_specs, out_shape, args, scratch_shapes=(), comm=None):
    params = _params(*(("arbitrary",) * len(grid)))
    if comm is None:
        out = pl.pallas_call(body, name=name, grid=grid, in_specs=in_specs, out_specs=out_specs, out_shape=out_shape,
                             scratch_shapes=scratch_shapes, compiler_params=params)(*args)
        return list(out), []
    n_in, n_out, n_scr = len(in_specs), len(out_specs), len(scratch_shapes)
    ns, nb = len(comm.srcs), len(comm.bufs)
    total = 1
    for size in grid:
        total *= size

    def fused(*refs):
        ins, srcs = refs[:n_in], refs[n_in:n_in + ns]
        at = n_in + ns + nb
        outs, bufs = refs[at:at + n_out], refs[at + n_out:at + n_out + nb]
        scratch = refs[at + n_out + nb:at + n_out + nb + n_scr]
        send_sems, recv_sems = refs[-2:]
        step = 0
        for axis, size in enumerate(grid):
            step = step * size + pl.program_id(axis)
        for when, fn in comm.phases:
            pl.when(step == when)(functools.partial(fn, srcs, bufs, send_sems, recv_sems))
        body(*ins, *outs, *scratch)
        pl.when(step == total - 1)(functools.partial(comm.finish, srcs, bufs, send_sems, recv_sems))

    out = pl.pallas_call(
        fused, name=name, grid=grid,
        in_specs=list(in_specs) + [ANY] * (ns + nb), out_specs=list(out_specs) + [ANY] * nb,
        out_shape=list(out_shape) + [jax.ShapeDtypeStruct(b.shape, b.dtype) for b in comm.bufs],
        input_output_aliases={n_in + ns + i: n_out + i for i in range(nb)},
        scratch_shapes=list(scratch_shapes) + [pltpu.SemaphoreType.DMA((comm.n_sems,))] * 2,
        compiler_params=params,
    )(*args, *comm.srcs, *comm.bufs)
    return list(out[:n_out]), list(out[n_out:])


def _comm_only(comm, name):
    ns, nb = len(comm.srcs), len(comm.bufs)

    def body(*refs):
        srcs, bufs = refs[:ns], refs[ns + nb:ns + 2 * nb]
        send_sems, recv_sems = refs[-2:]
        for _, fn in comm.phases:
            fn(srcs, bufs, send_sems, recv_sems)
        comm.finish(srcs, bufs, send_sems, recv_sems)

    return pl.pallas_call(
        body, name=name, in_specs=[ANY] * (ns + nb), out_specs=[ANY] * nb,
        out_shape=[jax.ShapeDtypeStruct(b.shape, b.dtype) for b in comm.bufs],
        input_output_aliases={ns + i: i for i in range(nb)},
        scratch_shapes=[pltpu.SemaphoreType.DMA((comm.n_sems,))] * 2,
    )(*comm.srcs, *comm.bufs)


def _sigmoid(v):
    return jax.nn.sigmoid(v)


def _dsilu(v, s):
    return s * (1.0 + v * (1.0 - s))


def _mean_last(v):
    return jnp.mean(v, axis=-1, keepdims=True)


def _sum_rows(v):
    return jnp.sum(v, axis=0, keepdims=True)


def _norm_matmul(x, gain, w_sm, name):
    s, d = x.shape
    n_sh, _, ns = w_sm.shape
    tm = _row_tile(s, 1024)

    def body(x_ref, g_ref, w_ref, p_ref, h_ref):
        @pl.when(pl.program_id(1) == 0)
        def _():
            xv = x_ref[...]
            r = lax.rsqrt(_mean_last(xv * xv) + EPS)
            h_ref[...] = (xv * r * g_ref[...]).astype(BF16)

        p_ref[...] = jnp.dot(h_ref[...], w_ref[0], preferred_element_type=F32)

    return _call(
        body, name=name, grid=(s // tm, n_sh),
        in_specs=[pl.BlockSpec((tm, d), lambda i, j: (i, 0)),
                  pl.BlockSpec((1, d), lambda i, j: (0, 0)),
                  pl.BlockSpec((1, d, ns), lambda i, j: (j, 0, 0))],
        out_specs=[pl.BlockSpec((tm, ns), lambda i, j: (i, j)),
                   pl.BlockSpec((tm, d), lambda i, j: (i, 0))],
        out_shape=[jax.ShapeDtypeStruct((s, n_sh * ns), F32), jax.ShapeDtypeStruct((s, d), BF16)],
        args=(x, gain, w_sm))[0]


def _matmul_post_loss(u, w, x_res, gain, target, name):
    s, k = u.shape
    d = w.shape[1]
    tm = _row_tile(s, 256)

    def body(u_ref, w_ref, x_ref, g_ref, t_ref, dy_ref, dout_ref, dg_ref, loss_ref):
        @pl.when(pl.program_id(0) == 0)
        def _():
            dg_ref[...] = jnp.zeros_like(dg_ref)
            loss_ref[...] = jnp.zeros_like(loss_ref)

        y = jnp.dot(u_ref[...], w_ref[...], preferred_element_type=F32)
        r = lax.rsqrt(_mean_last(y * y) + EPS)
        n = y * r
        g = g_ref[...]
        err = x_ref[...] + n * g - t_ref[...]
        loss_ref[...] += 0.5 * jnp.sum(_mean_last(err * err))
        dout = err * (1.0 / d)
        dout_ref[...] = dout
        dg_ref[...] += _sum_rows(dout * n)
        dn = dout * g
        dy_ref[...] = (r * (dn - n * _mean_last(dn * n))).astype(BF16)

    return pl.pallas_call(
        body, name=name, grid=(s // tm,),
        in_specs=[pl.BlockSpec((tm, k), lambda i: (i, 0)),
                  pl.BlockSpec((k, d), lambda i: (0, 0)),
                  pl.BlockSpec((tm, d), lambda i: (i, 0)),
                  pl.BlockSpec((1, d), lambda i: (0, 0)),
                  pl.BlockSpec((tm, d), lambda i: (i, 0))],
        out_specs=[pl.BlockSpec((tm, d), lambda i: (i, 0)),
                   pl.BlockSpec((tm, d), lambda i: (i, 0)),
                   pl.BlockSpec((1, d), lambda i: (0, 0)),
                   pl.BlockSpec((8, LANES), lambda i: (0, 0))],
        out_shape=[jax.ShapeDtypeStruct((s, d), BF16), jax.ShapeDtypeStruct((s, d), F32),
                   jax.ShapeDtypeStruct((1, d), F32), jax.ShapeDtypeStruct((8, LANES), F32)],
        compiler_params=_params("arbitrary"),
    )(u, w, x_res, gain, target)


def _matmul_nt(a, w_sm, name, comm=None):
    s, ncols = a.shape
    n_sh, r, ns = w_sm.shape
    assert ncols == n_sh * ns
    tm = _row_tile(s, 1024)
    nc = _col_chunk(ns, 1792)
    per = ns // nc
    steps = n_sh * per

    def body(a_ref, w_ref, o_ref):
        part = lax.dot_general(a_ref[...], w_ref[0], (((1,), (1,)), ((), ())), preferred_element_type=F32)

        @pl.when(pl.program_id(1) == 0)
        def _():
            o_ref[...] = part

        @pl.when(pl.program_id(1) > 0)
        def _():
            o_ref[...] += part

    out, bufs = _call(
        body, name=name, grid=(s // tm, steps),
        in_specs=[pl.BlockSpec((tm, nc), lambda i, j: (i, j)),
                  pl.BlockSpec((1, r, nc), lambda i, j: (j // per, 0, j % per))],
        out_specs=[pl.BlockSpec((tm, r), lambda i, j: (i, 0))],
        out_shape=[jax.ShapeDtypeStruct((s, r), F32)],
        args=(a, w_sm), comm=comm)
    return out[0], bufs


def _matmul_tn(a, b, n_sh, name):
    s, k = a.shape
    n = b.shape[1]
    ns = n // n_sh
    tk = _col_chunk(k, TN_ACC_BYTES // (4 * ns))
    ts = _row_tile(s, 2048)
    n_s = s // ts

    def body(a_ref, b_ref, o_ref, acc_ref):
        part = lax.dot_general(a_ref[...], b_ref[...], (((0,), (0,)), ((), ())), preferred_element_type=F32)

        @pl.when(pl.program_id(2) == 0)
        def _():
            acc_ref[...] = part

        @pl.when(pl.program_id(2) > 0)
        def _():
            acc_ref[...] += part

        @pl.when(pl.program_id(2) == n_s - 1)
        def _():
            o_ref[0] = acc_ref[...].astype(BF16)

    return pl.pallas_call(
        body, name=name, grid=(n_sh, k // tk, n_s),
        in_specs=[pl.BlockSpec((ts, tk), lambda j, i, t: (t, i)),
                  pl.BlockSpec((ts, ns), lambda j, i, t: (t, j))],
        out_specs=pl.BlockSpec((1, tk, ns), lambda j, i, t: (j, i, 0)),
        out_shape=jax.ShapeDtypeStruct((n_sh, k, ns), BF16),
        scratch_shapes=[pltpu.VMEM((tk, ns), F32)],
        compiler_params=_params("arbitrary", "arbitrary", "arbitrary"),
    )(a, b)


def _norm_bwd(dh, x, gain, dres, name, post=None):
    s, d = x.shape
    with_post = post is not None
    tm = _row_tile(s, 256 if with_post else 512)

    def rms_bwd(dout, v, g):
        r = lax.rsqrt(_mean_last(v * v) + EPS)
        n = v * r
        dn = dout * g
        return r * (dn - n * _mean_last(dn * n)), _sum_rows(dout * n)

    def body(*refs):
        if with_post:
            dh_ref, x_ref, g_ref, dres_ref, y_ref, gp_ref, dx_ref, dg_ref, dy_ref, dgp_ref = refs
        else:
            dh_ref, x_ref, g_ref, dres_ref, dx_ref, dg_ref = refs

        @pl.when(pl.program_id(0) == 0)
        def _():
            dg_ref[...] = jnp.zeros_like(dg_ref)
            if with_post:
                dgp_ref[...] = jnp.zeros_like(dgp_ref)

        dv, dg = rms_bwd(dh_ref[...], x_ref[...], g_ref[...])
        dx = dres_ref[...] + dv
        dx_ref[...] = dx
        dg_ref[...] += dg
        if with_post:
            dy, dgp = rms_bwd(dx, y_ref[...], gp_ref[...])
            dy_ref[...] = dy.astype(BF16)
            dgp_ref[...] += dgp

    row = pl.BlockSpec((tm, d), lambda i: (i, 0))
    vec = pl.BlockSpec((1, d), lambda i: (0, 0))
    in_specs = [row, row, vec, row]
    out_specs = [row, vec]
    out_shape = [jax.ShapeDtypeStruct((s, d), F32), jax.ShapeDtypeStruct((1, d), F32)]
    args = [dh, x, gain, dres]
    if with_post:
        in_specs += [row, vec]
        out_specs += [row, vec]
        out_shape += [jax.ShapeDtypeStruct((s, d), BF16), jax.ShapeDtypeStruct((1, d), F32)]
        args += list(post)
    return _call(body, name=name, grid=(s // tm,), in_specs=in_specs, out_specs=out_specs, out_shape=out_shape,
                 args=args)[0]


def _fill_shifted_down(sh, rows):
    for b in range(1, SHIFTS):
        sh[b, SHIFTS:rows, :] = sh[0, SHIFTS - b:rows - b, :]


def _fill_shifted_up(sh, rows):
    for b in range(1, SHIFTS):
        sh[b, 0:rows - SHIFTS, :] = sh[0, b:rows - SHIFTS + b, :]


def _for_blocks(ts, w, fn):
    lb = min(LANE_BLOCK, w)
    for l0 in range(0, w, lb):
        def rows(rb, carry, l0=l0):
            fn(pl.multiple_of(rb * ROW_BLOCK, ROW_BLOCK), slice(l0, l0 + lb))
            return carry

        lax.fori_loop(0, ts // ROW_BLOCK, rows, 0)


TAP_SPAN = SHIFTS * ((CONV_B - 1) // SHIFTS)
WINDOW = ROW_BLOCK + TAP_SPAN


def _taps_of(b):
    return [(a, SHIFTS * a + b) for a in range((CONV_B - 1 - b) // SHIFTS + 1)]


def _conv31(sh, base, step, wt_ref, bias_ref, out_ref, ts, w):
    low = min(0, step * (TAP_SPAN // SHIFTS))

    def block(r0, lanes):
        acc = [jnp.zeros((SHIFTS, lanes.stop - lanes.start), F32) for _ in range(ROW_BLOCK // SHIFTS)]
        for b in range(SHIFTS):
            window = sh[b, pl.ds(pl.multiple_of(r0 + (base + low), SHIFTS), WINDOW), lanes]
            for a, j in (_taps_of(b) if step > 0 else reversed(_taps_of(b))):
                at = step * a - low
                wt = wt_ref[CONV_B - 1 - j, :, lanes]
                acc = [v + wt * window[at + SHIFTS * r:at + SHIFTS * (r + 1), :] for r, v in enumerate(acc)]
        for r, v in enumerate(acc):
            if bias_ref is not None:
                v = v + bias_ref[:, lanes]
            out_ref[pl.ds(pl.multiple_of(r0 + SHIFTS * r, SHIFTS), SHIFTS), lanes] = v

    _for_blocks(ts, w, block)


def _conv31_weight_grad(d_sh, x_sh, wacc, ts, w):
    def block(r0, lanes):
        d = d_sh[0, pl.ds(r0, ROW_BLOCK), lanes]
        for b in range(SHIFTS):
            window = x_sh[b, pl.ds(pl.multiple_of(r0 + (HALO_B - TAP_SPAN), SHIFTS), WINDOW), lanes]
            for a, j in _taps_of(b):
                at = TAP_SPAN - SHIFTS * a
                prod = d * window[at:at + ROW_BLOCK, :]
                part = prod[0:SHIFTS, :]
                for q in range(1, ROW_BLOCK // SHIFTS):
                    part = part + prod[q * SHIFTS:(q + 1) * SHIFTS, :]
                wacc[CONV_B - 1 - j, :, lanes] += part

    _for_blocks(ts, w, block)


def _even_forward_tile(p_ref, halo_ref, first, a_conv_ref, b_conv_ref, bias_ref, lng_ref, lnb_ref, qbuf, ysh, y1buf,
                       wb, w, ts):
    @pl.when(pl.program_id(0) == 0)
    def _():
        for k in range(CONV_B):
            wb[k] = jnp.broadcast_to(b_conv_ref[k:k + 1, :], (SHIFTS, w))

    def col(ref, k, rows=slice(None)):
        return ref[rows, k * w:(k + 1) * w]

    a_x, a_b, a_c, a_z = col(p_ref, 0), col(p_ref, 1), col(p_ref, 2), col(p_ref, 3)
    b_val, b_gate, b_z = col(p_ref, 4), col(p_ref, 5), col(p_ref, 6)
    keep = jnp.where(first, 0.0, 1.0)

    rows_a = slice(HALO_B - HALO_A, HALO_B)
    qbuf[0:HALO_A, :] = col(halo_ref, 2, rows_a) * col(halo_ref, 0, rows_a) * keep
    qbuf[HALO_A:HALO_A + ts, :] = a_c * a_x
    cq = jnp.zeros((ts, w), F32)
    for j in range(CONV_A):
        cq = cq + a_conv_ref[CONV_A - 1 - j:CONV_A - j, :] * qbuf[HALO_A - j:HALO_A - j + ts, :]
    ya = a_b * cq

    ysh[0, 0:HALO_B, :] = col(halo_ref, 4) * _sigmoid(col(halo_ref, 5)) * keep
    ysh[0, HALO_B:HALO_B + ts, :] = b_val * _sigmoid(b_gate)
    _fill_shifted_down(ysh, HALO_B + ts)
    _conv31(ysh, HALO_B, -SHIFTS, wb, bias_ref, y1buf, ts, w)
    yb1 = y1buf[...]
    xc = yb1 - _mean_last(yb1)
    rstd = lax.rsqrt(_mean_last(xc * xc) + EPS)
    xhat = xc * rstd
    yb2 = xhat * lng_ref[...] + lnb_ref[...]
    return dict(a_x=a_x, a_b=a_b, a_c=a_c, a_z=a_z, b_val=b_val, b_gate=b_gate, b_z=b_z,
                cq=cq, ya=ya, rstd=rstd, xhat=xhat, yb2=yb2)


def _even_specs(s, w, ts):
    tile = pl.BlockSpec((ts, 7 * w), lambda i: (i, 0))
    halo = pl.BlockSpec((HALO_B, 7 * w), lambda i: (jnp.maximum(i * (ts // HALO_B) - 1, 0), 0))
    return tile, halo


def _small_specs(shapes, index=lambda i: (0, 0)):
    return [pl.BlockSpec(sh, index) for sh in shapes]


def _even_mixer_fwd(p, w_out, x_res, gain, a_conv, b_conv, bias, ln_g, ln_b, name, comm=None):
    s, d = x_res.shape
    w = p.shape[1] // 7
    ts = _row_tile(s, 128)
    assert ts % HALO_B == 0

    def body(p_ref, halo_ref, wout_ref, x_ref, g_ref, ac_ref, bc_ref, bias_ref, lng_ref, lnb_ref,
             u_ref, xn_ref, y_ref, qbuf, ysh, y1buf, wb):
        first = pl.program_id(0) == 0
        f = _even_forward_tile(p_ref, halo_ref, first, ac_ref, bc_ref, bias_ref, lng_ref, lnb_ref, qbuf, ysh, y1buf,
                               wb, w, ts)
        yb3 = f["yb2"] * _sigmoid(f["yb2"])
        u_a = (f["ya"] * (f["a_z"] * _sigmoid(f["a_z"]))).astype(BF16)
        u_b = (yb3 * (f["b_z"] * _sigmoid(f["b_z"]))).astype(BF16)
        u_ref[:, 0:w] = u_a
        u_ref[:, w:2 * w] = u_b
        y = (jnp.dot(u_a, wout_ref[0:w, :], preferred_element_type=F32)
             + jnp.dot(u_b, wout_ref[w:2 * w, :], preferred_element_type=F32))
        r = lax.rsqrt(_mean_last(y * y) + EPS)
        y_ref[...] = y
        xn_ref[...] = x_ref[...] + (y * r) * g_ref[...]

    tile, halo = _even_specs(s, w, ts)
    row = pl.BlockSpec((ts, d), lambda i: (i, 0))
    return _call(
        body, name=name, grid=(s // ts,),
        in_specs=[tile, halo, pl.BlockSpec((2 * w, d), lambda i: (0, 0)), row, pl.BlockSpec((1, d), lambda i: (0, 0))]
        + _small_specs([(CONV_A, w), (CONV_B, w), (1, w), (1, w), (1, w)]),
        out_specs=[pl.BlockSpec((ts, 2 * w), lambda i: (i, 0)), row, row],
        out_shape=[jax.ShapeDtypeStruct((s, 2 * w), BF16), jax.ShapeDtypeStruct((s, d), F32),
                   jax.ShapeDtypeStruct((s, d), F32)],
        scratch_shapes=[pltpu.VMEM((HALO_A + ts, w), F32), pltpu.VMEM((SHIFTS, HALO_B + ts, w), F32),
                        pltpu.VMEM((ts, w), F32), pltpu.VMEM((CONV_B, SHIFTS, w), F32)],
        args=(p, p, w_out, x_res, gain, a_conv, b_conv, bias, ln_g, ln_b), comm=comm)


def _even_mixer_bwd(p, du, a_conv, b_conv, bias, ln_g, ln_b, name, comm=None):
    s = p.shape[0]
    w = p.shape[1] // 7
    ts = _row_tile(s, EVEN_BWD_ROWS)
    nt = s // ts
    assert ts % HALO_B == 0

    def body(p_ref, halo_ref, du_ref, ac_ref, bc_ref, bias_ref, lng_ref, lnb_ref,
             dp_ref, dac_ref, dbc_ref, dbias_ref, dlng_ref, dlnb_ref,
             qbuf, ysh, y1buf, dqbuf, dsh, dy0buf, wacc, carry_dq, carry_dy, wb):
        step = pl.program_id(0)
        first = step == nt - 1

        @pl.when(step == 0)
        def _():
            for ref in (dac_ref, dbias_ref, dlng_ref, dlnb_ref, wacc, carry_dq, carry_dy):
                ref[...] = jnp.zeros_like(ref)

        @pl.when(step == 0)
        def _():
            for k in range(CONV_B):
                wb[k] = jnp.broadcast_to(bc_ref[k:k + 1, :], (SHIFTS, w))

        keep = jnp.where(first, 0.0, 1.0)
        lane_blocks = [slice(l0, l0 + min(LANE_PASS, w)) for l0 in range(0, w, min(LANE_PASS, w))]

        def col(ref, k, lanes, rows=slice(None)):
            return ref[rows, k * w + lanes.start:k * w + lanes.stop]


        rows_a = slice(HALO_B - HALO_A, HALO_B)
        for lanes in lane_blocks:
            a_x, a_b, a_c, a_z = (col(p_ref, k, lanes) for k in range(4))
            qbuf[0:HALO_A, lanes] = col(halo_ref, 2, lanes, rows_a) * col(halo_ref, 0, lanes, rows_a) * keep
            qbuf[HALO_A:HALO_A + ts, lanes] = a_c * a_x
            cq = jnp.zeros((ts, lanes.stop - lanes.start), F32)
            for j in range(CONV_A):
                cq = cq + ac_ref[CONV_A - 1 - j:CONV_A - j, lanes] * qbuf[HALO_A - j:HALO_A - j + ts, lanes]
            s_az = _sigmoid(a_z)
            du_a = du_ref[:, lanes]
            d_ya = du_a * (a_z * s_az)
            dp_ref[:, 3 * w + lanes.start:3 * w + lanes.stop] = (du_a * (a_b * cq) * _dsilu(a_z, s_az)).astype(BF16)
            dp_ref[:, 1 * w + lanes.start:1 * w + lanes.stop] = (d_ya * cq).astype(BF16)
            d_cq = d_ya * a_b
            dqbuf[0:ts, lanes] = d_cq
            dqbuf[ts:ts + HALO_A, lanes] = carry_dq[:, lanes]
            carry_dq[:, lanes] = d_cq[0:HALO_A, :]
            d_q = jnp.zeros_like(cq)
            for o in range(CONV_A):
                d_q = d_q + ac_ref[CONV_A - 1 - o:CONV_A - o, lanes] * dqbuf[o:o + ts, lanes]
            for j in range(CONV_A):
                k = CONV_A - 1 - j
                dac_ref[k:k + 1, lanes] += _sum_rows(d_cq * qbuf[HALO_A - j:HALO_A - j + ts, lanes])
            dp_ref[:, 2 * w + lanes.start:2 * w + lanes.stop] = (d_q * a_x).astype(BF16)
            dp_ref[:, 0 * w + lanes.start:0 * w + lanes.stop] = (d_q * a_c).astype(BF16)
            ysh[0, 0:HALO_B, lanes] = col(halo_ref, 4, lanes) * _sigmoid(col(halo_ref, 5, lanes)) * keep
            ysh[0, HALO_B:HALO_B + ts, lanes] = col(p_ref, 4, lanes) * _sigmoid(col(p_ref, 5, lanes))
        _fill_shifted_down(ysh, HALO_B + ts)
        _conv31(ysh, HALO_B, -SHIFTS, wb, bias_ref, y1buf, ts, w)

        total = jnp.zeros((ts, 1), F32)
        for lanes in lane_blocks:
            total = total + jnp.sum(y1buf[:, lanes], axis=-1, keepdims=True)
        mu = total * (1.0 / w)
        total = jnp.zeros((ts, 1), F32)
        for lanes in lane_blocks:
            xc = y1buf[:, lanes] - mu
            total = total + jnp.sum(xc * xc, axis=-1, keepdims=True)
        rstd = lax.rsqrt(total * (1.0 / w) + EPS)

        sum_dxh = jnp.zeros((ts, 1), F32)
        sum_dxh_xhat = jnp.zeros((ts, 1), F32)
        for lanes in lane_blocks:
            xhat = (y1buf[:, lanes] - mu) * rstd
            yb2 = xhat * lng_ref[:, lanes] + lnb_ref[:, lanes]
            b_z = col(p_ref, 6, lanes)
            s_bz, s_y2 = _sigmoid(b_z), _sigmoid(yb2)
            du_b = du_ref[:, w + lanes.start:w + lanes.stop]
            dp_ref[:, 6 * w + lanes.start:6 * w + lanes.stop] = (du_b * (yb2 * s_y2) * _dsilu(b_z, s_bz)).astype(BF16)
            d_yb2 = du_b * (b_z * s_bz) * _dsilu(yb2, s_y2)
            dlng_ref[:, lanes] += _sum_rows(d_yb2 * xhat)
            dlnb_ref[:, lanes] += _sum_rows(d_yb2)
            d_xh = d_yb2 * lng_ref[:, lanes]
            sum_dxh = sum_dxh + jnp.sum(d_xh, axis=-1, keepdims=True)
            sum_dxh_xhat = sum_dxh_xhat + jnp.sum(d_xh * xhat, axis=-1, keepdims=True)
            y1buf[:, lanes] = xhat
            dy0buf[:, lanes] = d_xh
        mean_dxh = sum_dxh * (1.0 / w)
        mean_dxh_xhat = sum_dxh_xhat * (1.0 / w)

        for lanes in lane_blocks:
            d_yb1 = rstd * (dy0buf[:, lanes] - mean_dxh - y1buf[:, lanes] * mean_dxh_xhat)
            dbias_ref[:, lanes] += _sum_rows(d_yb1)
            dsh[0, 0:ts, lanes] = d_yb1
            dsh[0, ts:ts + HALO_B, lanes] = carry_dy[:, lanes]
            carry_dy[:, lanes] = d_yb1[0:HALO_B, :]
        _fill_shifted_up(dsh, ts + HALO_B)
        _conv31(dsh, 0, SHIFTS, wb, None, dy0buf, ts, w)
        _conv31_weight_grad(dsh, ysh, wacc, ts, w)

        @pl.when(step == nt - 1)
        def _():
            for k in range(CONV_B):
                dbc_ref[k:k + 1, :] = _sum_rows(wacc[k])

        for lanes in lane_blocks:
            d_yb0 = dy0buf[:, lanes]
            s_g = _sigmoid(col(p_ref, 5, lanes))
            dp_ref[:, 4 * w + lanes.start:4 * w + lanes.stop] = (d_yb0 * s_g).astype(BF16)
            dp_ref[:, 5 * w + lanes.start:5 * w + lanes.stop] = (
                d_yb0 * col(p_ref, 4, lanes) * s_g * (1.0 - s_g)).astype(BF16)

    rev = lambda i: (nt - 1 - i, 0)
    tile = pl.BlockSpec((ts, 7 * w), rev)
    halo = pl.BlockSpec((HALO_B, 7 * w), lambda i: (jnp.maximum((nt - 1 - i) * (ts // HALO_B) - 1, 0), 0))
    small = [(CONV_A, w), (CONV_B, w), (1, w), (1, w), (1, w)]
    return _call(
        body, name=name, grid=(nt,),
        in_specs=[tile, halo, pl.BlockSpec((ts, 2 * w), rev)] + _small_specs(small),
        out_specs=[pl.BlockSpec((ts, 7 * w), rev)] + _small_specs(small),
        out_shape=[jax.ShapeDtypeStruct((s, 7 * w), BF16)] + [jax.ShapeDtypeStruct(sh, F32) for sh in small],
        scratch_shapes=[pltpu.VMEM((HALO_A + ts, w), F32), pltpu.VMEM((SHIFTS, HALO_B + ts, w), F32),
                        pltpu.VMEM((ts, w), F32),
                        pltpu.VMEM((ts + HALO_A, w), F32), pltpu.VMEM((SHIFTS, ts + HALO_B, w), F32),
                        pltpu.VMEM((ts, w), F32), pltpu.VMEM((CONV_B, SHIFTS, w), F32),
                        pltpu.VMEM((HALO_A, w), F32), pltpu.VMEM((HALO_B, w), F32),
                        pltpu.VMEM((CONV_B, SHIFTS, w), F32)],
        args=(p, p, du, a_conv, b_conv, bias, ln_g, ln_b), comm=comm)


def _trailing_sums(buf_a, buf_b, cols, win, rows, first_out):
    src, dst, shift = buf_a, buf_b, 1
    while True:
        last = 2 * shift >= win
        lo = first_out if last else 0
        val = src[PAD_P + lo:PAD_P + rows, cols] + src[PAD_P + lo - shift:PAD_P + rows - shift, cols]
        if last:
            return val
        dst[PAD_P + lo:PAD_P + rows, cols] = val
        src, dst, shift = dst, src, 2 * shift


def _leading_sums(buf_a, buf_b, cols, win, rows, n_out):
    src, dst, shift = buf_a, buf_b, 1
    while True:
        last = 2 * shift >= win
        hi = n_out if last else rows
        val = src[0:hi, cols] + src[shift:hi + shift, cols]
        if last:
            return val
        dst[0:hi, cols] = val
        src, dst, shift = dst, src, 2 * shift


def _pool_forward_tile(p_ref, halo_ref, first, tile_index, cw_ref, cb_ref, cs_ref, vbuf, vtmp, c, gc, ts):
    vbuf[PAD_P:PAD_P + HALO_P, :] = halo_ref[...] * jnp.where(first, 0.0, 1.0)
    vbuf[PAD_P + HALO_P:PAD_P + HALO_P + ts, :] = p_ref[:, 0:c]
    pos = tile_index * ts + lax.broadcasted_iota(jnp.int32, (ts, 1), 0) + 1
    pooled, inv, gout = [], [], []
    for g, win in enumerate(POOL_WINDOWS):
        cols = slice(g * gc, (g + 1) * gc)
        acc = _trailing_sums(vbuf, vtmp, cols, win, HALO_P + ts, HALO_P)
        inv_g = 1.0 / jnp.minimum(pos, win).astype(F32)
        pooled_g = (acc * inv_g - p_ref[:, cols]).astype(BF16)
        pooled.append(pooled_g)
        inv.append(inv_g)
        gout.append(jnp.dot(pooled_g, cw_ref[g], preferred_element_type=F32) + cb_ref[:, cols])
    return pooled, inv, gout


def _odd_mixer_fwd(p, cw, cb, cs, name):
    s = p.shape[0]
    c = p.shape[1] // 2
    gc = c // N_GROUPS
    ts = _row_tile(s, 256)

    def body(p_ref, halo_ref, cw_ref, cb_ref, cs_ref, u_ref, vbuf, vtmp):
        i = pl.program_id(0)

        @pl.when(i == 0)
        def _():
            vbuf[0:PAD_P, :] = jnp.zeros((PAD_P, c), F32)
            vtmp[0:PAD_P, :] = jnp.zeros((PAD_P, c), F32)

        _, _, gout = _pool_forward_tile(p_ref, halo_ref, i == 0, i, cw_ref, cb_ref, cs_ref, vbuf, vtmp, c, gc, ts)
        for g in range(N_GROUPS):
            cols = slice(g * gc, (g + 1) * gc)
            z = p_ref[:, c + g * gc:c + (g + 1) * gc]
            u_ref[:, cols] = (gout[g] * cs_ref[:, cols] * (z * _sigmoid(z))).astype(BF16)

    return pl.pallas_call(
        body, name=name, grid=(s // ts,),
        in_specs=[pl.BlockSpec((ts, 2 * c), lambda i: (i, 0)),
                  pl.BlockSpec((HALO_P, c), lambda i: (jnp.maximum(i * (ts // HALO_P) - 1, 0), 0)),
                  pl.BlockSpec((N_GROUPS, gc, gc), lambda i: (0, 0, 0)),
                  pl.BlockSpec((1, c), lambda i: (0, 0)), pl.BlockSpec((1, c), lambda i: (0, 0))],
        out_specs=pl.BlockSpec((ts, c), lambda i: (i, 0)),
        out_shape=jax.ShapeDtypeStruct((s, c), BF16),
        scratch_shapes=[pltpu.VMEM((PAD_P + HALO_P + ts, c), F32)] * 2,
        compiler_params=_params("arbitrary"),
    )(p, p, cw, cb, cs)


def _odd_mixer_bwd(p, du, cw, cb, cs, name):
    s = p.shape[0]
    c = p.shape[1] // 2
    gc = c // N_GROUPS
    ts = _row_tile(s, 256)
    nt = s // ts

    def body(p_ref, halo_ref, du_ref, cw_ref, cb_ref, cs_ref, dp_ref, dcw_ref, dcb_ref, dcs_ref,
             vbuf, vtmp, ebuf, etmp, carry_e):
        step = pl.program_id(0)
        tile_index = nt - 1 - step

        @pl.when(step == 0)
        def _():
            for ref in (dcw_ref, dcb_ref, dcs_ref, carry_e):
                ref[...] = jnp.zeros_like(ref)
            for ref in (vbuf, vtmp):
                ref[0:PAD_P, :] = jnp.zeros((PAD_P, c), F32)
            for ref in (ebuf, etmp):
                ref[ts + HALO_P:ts + HALO_P + PAD_P, :] = jnp.zeros((PAD_P, c), F32)

        pooled, inv, gout = _pool_forward_tile(p_ref, halo_ref, tile_index == 0, tile_index, cw_ref, cb_ref, cs_ref,
                                               vbuf, vtmp, c, gc, ts)
        ebuf[ts:ts + HALO_P, :] = carry_e[...]
        for g, win in enumerate(POOL_WINDOWS):
            cols = slice(g * gc, (g + 1) * gc)
            z = p_ref[:, c + g * gc:c + (g + 1) * gc]
            sz = _sigmoid(z)
            du_g = du_ref[:, cols]
            scale = cs_ref[:, cols]
            d_y = du_g * (z * sz)
            dp_ref[:, c + g * gc:c + (g + 1) * gc] = (du_g * (gout[g] * scale) * _dsilu(z, sz)).astype(BF16)
            dcs_ref[:, cols] += _sum_rows(d_y * gout[g])
            d_gout = d_y * scale
            dcb_ref[:, cols] += _sum_rows(d_gout)
            d_gout_b = d_gout.astype(BF16)
            dcw_ref[g] += lax.dot_general(pooled[g], d_gout_b, (((0,), (0,)), ((), ())), preferred_element_type=F32)
            d_pool = lax.dot_general(d_gout_b, cw_ref[g], (((1,), (1,)), ((), ())), preferred_element_type=F32)
            e = d_pool * inv[g]
            ebuf[0:ts, cols] = e
            carry_e[:, cols] = e[0:HALO_P, :]
            d_v = _leading_sums(ebuf, etmp, cols, win, ts + HALO_P, ts) - d_pool
            dp_ref[:, cols] = d_v.astype(BF16)

    rev = lambda i: (nt - 1 - i, 0)
    small = [(N_GROUPS, gc, gc), (1, c), (1, c)]
    return pl.pallas_call(
        body, name=name, grid=(nt,),
        in_specs=[pl.BlockSpec((ts, 2 * c), rev),
                  pl.BlockSpec((HALO_P, c), lambda i: (jnp.maximum((nt - 1 - i) * (ts // HALO_P) - 1, 0), 0)),
                  pl.BlockSpec((ts, c), rev),
                  pl.BlockSpec((N_GROUPS, gc, gc), lambda i: (0, 0, 0)),
                  pl.BlockSpec((1, c), lambda i: (0, 0)), pl.BlockSpec((1, c), lambda i: (0, 0))],
        out_specs=[pl.BlockSpec((ts, 2 * c), rev),
                   pl.BlockSpec((N_GROUPS, gc, gc), lambda i: (0, 0, 0)),
                   pl.BlockSpec((1, c), lambda i: (0, 0)), pl.BlockSpec((1, c), lambda i: (0, 0))],
        out_shape=[jax.ShapeDtypeStruct((s, 2 * c), BF16)] + [jax.ShapeDtypeStruct(sh, F32) for sh in small],
        scratch_shapes=[pltpu.VMEM((PAD_P + HALO_P + ts, c), F32)] * 2 + [pltpu.VMEM((ts + HALO_P + PAD_P, c), F32)] * 2
        + [pltpu.VMEM((HALO_P, c), F32)],
        compiler_params=_params("arbitrary"),
    )(p, p, du, cw, cb, cs)


def _cast_into_slot(a, coords, name):
    r, cols = a.shape
    tr = _row_tile(r // 2, 1024)
    per = r // 2 // tr

    def body(co_ref, a_ref, o_ref):
        o_ref[0, 0] = a_ref[...].astype(BF16)

    return pl.pallas_call(
        body, name=name,
        grid_spec=pltpu.PrefetchScalarGridSpec(
            num_scalar_prefetch=1, grid=(2, per),
            in_specs=[pl.BlockSpec((tr, cols), lambda h, i, co: (h * per + i, 0))],
            out_specs=pl.BlockSpec((1, 1, tr, cols), lambda h, i, co: (co[0], h, i, 0))),
        out_shape=jax.ShapeDtypeStruct((N_SHARDS, 2, r // 2, cols), BF16),
        compiler_params=_params("arbitrary", "arbitrary"),
    )(coords, a)


def _chip_sum(g, other, coords, name):
    n_sh, _, r2, cols = g.shape
    tr = _row_tile(r2, 512)

    def body(co_ref, g_ref, o_ref, sum_ref, mine_ref):
        v = (g_ref[0, 0].astype(F32) + o_ref[0].astype(F32)).astype(BF16)
        sum_ref[0] = v

        @pl.when(pl.program_id(1) == co_ref[0])
        def _():
            mine_ref[0] = v

    piece = pl.BlockSpec((1, tr, cols), lambda i, s, co: (s, i, 0))
    return pl.pallas_call(
        body, name=name,
        grid_spec=pltpu.PrefetchScalarGridSpec(
            num_scalar_prefetch=1, grid=(r2 // tr, n_sh),
            in_specs=[pl.BlockSpec((1, 1, tr, cols), lambda i, s, co: (s, co[1], i, 0)), piece],
            out_specs=[piece, pl.BlockSpec((1, tr, cols), lambda i, s, co: (co[0], i, 0))]),
        out_shape=[jax.ShapeDtypeStruct((n_sh, r2, cols), BF16)] * 2,
        compiler_params=_params("arbitrary", "arbitrary"),
    )(coords, g, other)


def _shard_sum(pieces, coords, name):
    n_sh, r2, cols = pieces.shape
    tr = _row_tile(r2, 512)

    def body(co_ref, p_ref, o_ref):
        acc = p_ref[0].astype(F32)
        for k in range(1, n_sh):
            acc = acc + p_ref[k].astype(F32)
        o_ref[0] = acc

    return pl.pallas_call(
        body, name=name,
        grid_spec=pltpu.PrefetchScalarGridSpec(
            num_scalar_prefetch=1, grid=(r2 // tr,),
            in_specs=[pl.BlockSpec((n_sh, tr, cols), lambda i, co: (0, i, 0))],
            out_specs=pl.BlockSpec((1, tr, cols), lambda i, co: (co[1], i, 0))),
        out_shape=jax.ShapeDtypeStruct((2, r2, cols), F32),
        compiler_params=_params("arbitrary"),
    )(coords, pieces)


def _sum_small(a, name):
    n, r, cols = a.shape

    def body(a_ref, o_ref):
        acc = a_ref[0]
        for k in range(1, n):
            acc = acc + a_ref[k]
        o_ref[...] = acc

    return pl.pallas_call(
        body, name=name,
        in_specs=[pl.BlockSpec((n, r, cols), lambda: (0, 0, 0))],
        out_specs=pl.BlockSpec((r, cols), lambda: (0, 0)),
        out_shape=jax.ShapeDtypeStruct((r, cols), F32),
        compiler_params=_params(),
    )(a)


def _adamw_step(w, g, m, v):
    m = ADAM_B1 * m + (1.0 - ADAM_B1) * g
    v = ADAM_B2 * v + (1.0 - ADAM_B2) * (g * g)
    m_hat = m / (1.0 - ADAM_B1 ** ADAM_STEP)
    v_hat = v / (1.0 - ADAM_B2 ** ADAM_STEP)
    return -ADAM_LR * (m_hat / (jnp.sqrt(v_hat) + ADAM_EPS) + ADAM_WD * w), m, v


def _adamw(w, g, m, v, name):
    r, cols = w.shape
    tr = _row_tile(r, 256) if r % SUBLANES_BF16 == 0 else r

    def body(w_ref, g_ref, m_ref, v_ref, d_ref, nm_ref, nv_ref):
        d_ref[...], nm_ref[...], nv_ref[...] = _adamw_step(w_ref[...], g_ref[...], m_ref[...], v_ref[...])

    blk = pl.BlockSpec((tr, cols), lambda i: (i, 0))
    return pl.pallas_call(
        body, name=name, grid=(r // tr,),
        in_specs=[blk] * 4, out_specs=[blk] * 3,
        out_shape=[jax.ShapeDtypeStruct((r, cols), F32)] * 3,
        compiler_params=_params("arbitrary"),
    )(w, g, m, v)


def _adamw_many(ws, gs, ms, vs, name):
    n = len(ws)

    def body(*refs):
        w_refs, g_refs, m_refs, v_refs = (refs[t * n:(t + 1) * n] for t in range(4))
        d_refs, nm_refs, nv_refs = (refs[(4 + t) * n:(5 + t) * n] for t in range(3))
        for k in range(n):
            d_refs[k][...], nm_refs[k][...], nv_refs[k][...] = _adamw_step(
                w_refs[k][...], g_refs[k][...], m_refs[k][...], v_refs[k][...])

    specs = [pl.BlockSpec(a.shape, lambda: (0, 0)) for a in ws]
    out = pl.pallas_call(
        body, name=name, in_specs=specs * 4, out_specs=specs * 3,
        out_shape=[jax.ShapeDtypeStruct(a.shape, F32) for a in ws] * 3,
        compiler_params=_params(),
    )(*ws, *gs, *ms, *vs)
    return out[:n], out[n:2 * n], out[2 * n:]


def _place():
    x, y, c = lax.axis_index("x"), lax.axis_index("y"), lax.axis_index("c")
    other_chips = [(1 - x, y), (x, 1 - y), (1 - x, 1 - y)]
    return x, y, c, other_chips


def _chip(xy):
    return 2 * xy[0] + xy[1]


def _remote(src, dst, send_sem, recv_sem, to):
    return pltpu.make_async_remote_copy(src_ref=src, dst_ref=dst, send_sem=send_sem, recv_sem=recv_sem,
                                        device_id=to, device_id_type=MESH)


def _gather_ici(ctx, k, j, start):
    (x, y, c, chips), b, send, recv = ctx
    if j < 2:
        chip, to = (_chip((x, y)) if start else _chip(chips[j])), (*chips[j], c)
    else:
        chip = 2 * (x ^ c) + (y ^ (1 - c)) if start else _chip(chips[2])
        to = (x ^ (1 - c), y ^ c, c)
    blk = b[k].at[chip, c]
    return _remote(blk, blk, send.at[6 * k + j], recv.at[6 * k + j], to)


def _gather_d2d(ctx, k, j, start):
    (x, y, c, chips), b, send, recv = ctx
    blk = b[k].at[_chip(chips[j]), c if start else 1 - c]
    return _remote(blk, blk, send.at[6 * k + 3 + j], recv.at[6 * k + 3 + j], (x, y, 1 - c))


def _gather_small(ctx, n, j, start):
    (x, y, c, chips), b, send, recv = ctx
    blk = b[n].at[_chip((x, y)) if start else _chip(chips[j])]
    return _remote(blk, blk, send.at[6 * n + j], recv.at[6 * n + j], (*chips[j], c))


def _gather_neighbours_landed(ctx, k):
    for j in range(2):
        _gather_ici(ctx, k, j, False).wait_recv()
    _gather_ici(ctx, k, 2, True).start()
    for j in range(2):
        _gather_d2d(ctx, k, j, True).start()


def _gather_diagonal_landed(ctx, k):
    _gather_ici(ctx, k, 2, False).wait_recv()
    _gather_d2d(ctx, k, 2, True).start()


def _gather_comm(bufs, relay_at, forward_at):
    n = len(bufs)

    def start(srcs, b, send, recv):
        for k in range(n):
            for j in range(2):
                _gather_ici((_place(), b, send, recv), k, j, True).start()

    def relay(srcs, b, send, recv):
        for k in range(n):
            _gather_neighbours_landed((_place(), b, send, recv), k)

    def forward(srcs, b, send, recv):
        for k in range(n):
            _gather_diagonal_landed((_place(), b, send, recv), k)

    def finish(srcs, b, send, recv):
        ctx = (_place(), b, send, recv)
        for k in range(n):
            for j in range(3):
                _gather_d2d(ctx, k, j, False).wait_recv()
                _gather_ici(ctx, k, j, True).wait_send()
                _gather_d2d(ctx, k, j, True).wait_send()

    return _Comm([], bufs, 6 * n, [(0, start), (relay_at, relay), (forward_at, forward)], finish)


def _rmsnorm(x, gain, name):
    s, d = x.shape
    tm = _row_tile(s, 1024)

    def body(x_ref, g_ref, h_ref):
        xv = x_ref[...]
        r = lax.rsqrt(_mean_last(xv * xv) + EPS)
        h_ref[...] = (xv * r * g_ref[...]).astype(BF16)

    return pl.pallas_call(
        body, name=name, grid=(s // tm,),
        in_specs=[pl.BlockSpec((tm, d), lambda i: (i, 0)), pl.BlockSpec((1, d), lambda i: (0, 0))],
        out_specs=pl.BlockSpec((tm, d), lambda i: (i, 0)),
        out_shape=jax.ShapeDtypeStruct((s, d), BF16),
        compiler_params=_params("arbitrary"),
    )(x, gain)


def _gathered_in_proj(h, bufs, small, order, name):
    s, d = h.shape
    n_sh, _, r2, ns = bufs[0].shape
    assert d == 2 * r2
    n = len(bufs)
    tm = _row_tile(s, 512)
    n_i = s // tm
    hook_i = max(n_i - 2, 0)
    n_sems = 6 * n + 3

    def body(order_ref, h_ref, *rest):
        p_ref = rest[n + 1]
        b = rest[n + 2:2 * n + 3]
        w_vmem, w_sems, send, recv = rest[2 * n + 3:]
        j, i = pl.program_id(0), pl.program_id(1)
        ctx = (_place(), b, send, recv)

        def fetch(q):
            return pltpu.make_async_copy(b[0].at[order_ref[q]], w_vmem.at[q % 2], w_sems.at[q % 2])

        @pl.when((j == 0) & (i == 0))
        def _():
            for k in range(n):
                for peer in range(2):
                    _gather_ici(ctx, k, peer, True).start()
            for peer in range(3):
                _gather_small(ctx, n, peer, True).start()
            fetch(0).start()
            fetch(0).wait()

        for q in range(1, n_sh):
            @pl.when((j == q - 1) & (i == hook_i))
            def _(q=q):
                if q == 1:
                    _gather_neighbours_landed(ctx, 0)
                if q == 2:
                    for k in range(1, n):
                        _gather_neighbours_landed(ctx, k)
                if q == 3:
                    for k in range(n):
                        _gather_diagonal_landed(ctx, k)
                _gather_d2d(ctx, 0, q - 1, False).wait_recv()
                fetch(q).start()

            @pl.when((j == q) & (i == 0))
            def _(q=q):
                fetch(q).wait()

        wv = w_vmem.at[j % 2]
        p_ref[...] = (jnp.dot(h_ref[:, 0:r2], wv[0], preferred_element_type=F32)
                      + jnp.dot(h_ref[:, r2:d], wv[1], preferred_element_type=F32))

        @pl.when((j == n_sh - 1) & (i == n_i - 1))
        def _():
            for peer in range(3):
                _gather_small(ctx, n, peer, False).wait_recv()
                _gather_small(ctx, n, peer, True).wait_send()
            for k in range(n):
                for peer in range(3):
                    if k > 0:
                        _gather_d2d(ctx, k, peer, False).wait_recv()
                    _gather_ici(ctx, k, peer, True).wait_send()
                    _gather_d2d(ctx, k, peer, True).wait_send()

    all_bufs = list(bufs) + [small]
    out = pl.pallas_call(
        body, name=name,
        grid_spec=pltpu.PrefetchScalarGridSpec(
            num_scalar_prefetch=1, grid=(n_sh, n_i),
            in_specs=[pl.BlockSpec((tm, d), lambda j, i, o: (i, 0))] + [ANY] * (n + 1),
            out_specs=[pl.BlockSpec((tm, ns), lambda j, i, o: (i, o[j]))] + [ANY] * (n + 1),
            scratch_shapes=[pltpu.VMEM((2, 2, r2, ns), BF16), pltpu.SemaphoreType.DMA((2,)),
                            pltpu.SemaphoreType.DMA((n_sems,)), pltpu.SemaphoreType.DMA((n_sems,))]),
        out_shape=[jax.ShapeDtypeStruct((s, n_sh * ns), F32)]
        + [jax.ShapeDtypeStruct(a.shape, a.dtype) for a in all_bufs],
        input_output_aliases={2 + t: 1 + t for t in range(n + 1)},
        compiler_params=_params("arbitrary", "arbitrary"),
    )(order, h, *all_bufs)
    return out[0], list(out[1:])


def _exchange_comm(grads):
    n = len(grads)
    landing = [lax.empty((N_SHARDS,) + a.shape[2:], a.dtype) for a in grads]

    def copies(srcs, b, send, recv):
        x, y, c, _ = _place()
        return [_remote(srcs[k].at[s, 1 - c], b[k].at[s], send.at[N_SHARDS * k + s], recv.at[N_SHARDS * k + s],
                        (x, y, 1 - c)) for k in range(n) for s in range(N_SHARDS)]

    def start(srcs, b, send, recv):
        for cp in copies(srcs, b, send, recv):
            cp.start()

    def finish(srcs, b, send, recv):
        for cp in copies(srcs, b, send, recv):
            cp.wait()

    return _Comm(grads, landing, N_SHARDS * n, [(0, start)], finish)


def _scatter_comm(chip_sums, landing):
    n = len(chip_sums)

    def big(srcs, b, send, recv, k, j, start):
        x, y, c, chips = _place()
        dst = b[k].at[_chip((x, y)) if start else _chip(chips[j])]
        return _remote(srcs[k].at[_chip(chips[j])], dst, send.at[3 * k + j], recv.at[3 * k + j], (*chips[j], c))

    def start(srcs, b, send, recv):
        for k in range(n):
            for j in range(3):
                big(srcs, b, send, recv, k, j, True).start()

    def finish(srcs, b, send, recv):
        for k in range(n):
            for j in range(3):
                big(srcs, b, send, recv, k, j, False).wait_recv()
                big(srcs, b, send, recv, k, j, True).wait_send()

    return _Comm(chip_sums, landing, 3 * n, [(0, start)], finish)


def _join_comm(halves, small):
    n = len(halves)
    flips = [(fx, fy, fc) for fx in (0, 1) for fy in (0, 1) for fc in (0, 1)][1:]

    def half(b, send, recv, k, start):
        x, y, c, _ = _place()
        return _remote(b[k].at[c], b[k].at[c if start else 1 - c], send.at[k], recv.at[k], (x, y, 1 - c))

    def small_copy(b, send, recv, q, start):
        x, y, c, _ = _place()
        px, py, pc = x ^ flips[q][0], y ^ flips[q][1], c ^ flips[q][2]
        blk = b[n].at[4 * x + 2 * y + c if start else 4 * px + 2 * py + pc]
        return _remote(blk, blk, send.at[n + q], recv.at[n + q], (px, py, pc))

    def start(srcs, b, send, recv):
        for q in range(len(flips)):
            small_copy(b, send, recv, q, True).start()
        for k in range(n):
            half(b, send, recv, k, True).start()

    def finish(srcs, b, send, recv):
        for q in range(len(flips)):
            small_copy(b, send, recv, q, False).wait()
        for k in range(n):
            half(b, send, recv, k, False).wait()

    return _Comm([], list(halves) + [small], n + len(flips), [(0, start)], finish)


def _flat_rows(parts):
    flat = jnp.concatenate([p.reshape(-1) for p in parts])
    assert flat.shape[0] % LANES == 0
    return flat.reshape(-1, LANES)


def _unflatten(flat, shapes):
    out, at = [], 0
    for sh in shapes:
        size = 1
        for dim in sh:
            size *= dim
        out.append(flat[at:at + size].reshape(sh))
        at += size
    assert at == flat.shape[0], (at, flat.shape)
    return out


def _col_shards_to_full(a, rows):
    q = a.shape[1] // rows
    return a.reshape(N_SHARDS, rows, q).transpose(1, 0, 2).reshape(rows, N_SHARDS * q)


def _my_col_shard(full, chip):
    rows, cols = full.shape
    q = cols // N_SHARDS
    return lax.dynamic_index_in_dim(full.reshape(rows, N_SHARDS, q), chip, axis=1, keepdims=False)


def kernel(x, e_norm_pre, e_norm_post, e_w_in, e_a_conv, e_b_conv, e_b_conv_bias, e_b_ln_g, e_b_ln_b, e_w_out, o_norm_pre, o_norm_post, o_w_in, o_c_w, o_c_b, o_c_scale, o_w_out, loss_target, m_e_norm_pre, m_e_norm_post, m_e_w_in, m_e_a_conv, m_e_b_conv, m_e_b_conv_bias, m_e_b_ln_g, m_e_b_ln_b, m_e_w_out, m_o_norm_pre, m_o_norm_post, m_o_w_in, m_o_c_w, m_o_c_b, m_o_c_scale, m_o_w_out, v_e_norm_pre, v_e_norm_post, v_e_w_in, v_e_a_conv, v_e_b_conv, v_e_b_conv_bias, v_e_b_ln_g, v_e_b_ln_b, v_e_w_out, v_o_norm_pre, v_o_norm_post, v_o_w_in, v_o_c_w, v_o_c_b, v_o_c_scale, v_o_w_out):
    _, s, d = x.shape
    w = d // 2
    c = d
    gc = c // N_GROUPS
    wq, cq, gq = w // N_SHARDS, c // N_SHARDS, gc // N_SHARDS
    chip = 2 * lax.axis_index("x") + lax.axis_index("y")
    core = lax.axis_index("c")
    x2 = x.reshape(s, d)
    target = loss_target.reshape(s, d)

    big_w = [e_w_in[0], e_w_out[0], o_w_in[0], o_c_w[0].reshape(N_GROUPS * gq, gc), o_w_out[0]]
    big_m = [m_e_w_in[0], m_e_w_out[0], m_o_w_in[0], m_o_c_w[0].reshape(N_GROUPS * gq, gc), m_o_w_out[0]]
    big_v = [v_e_w_in[0], v_e_w_out[0], v_o_w_in[0], v_o_c_w[0].reshape(N_GROUPS * gq, gc), v_o_w_out[0]]
    coords = jnp.stack([chip, core]).astype(jnp.int32)
    slots = [_cast_into_slot(a, coords, "cast_%d" % k) for k, a in enumerate(big_w)]
    sharded_small = _flat_rows([e_a_conv[0], e_b_conv[0], o_norm_pre, o_norm_post, o_c_scale, o_c_b[0]])
    small_slots = lax.dynamic_update_index_in_dim(jnp.zeros((N_SHARDS,) + sharded_small.shape, F32), sharded_small,
                                                  chip, 0)
    xi, yi = lax.axis_index("x"), lax.axis_index("y")
    order = jnp.stack([chip, 2 * (1 - xi) + yi, 2 * xi + (1 - yi), 2 * (1 - xi) + (1 - yi)]).astype(jnp.int32)
    h0 = _rmsnorm(x2, e_norm_pre, "e_pre_norm")
    p0, (e_w_in_g, e_w_out_g, small_g4) = _gathered_in_proj(h0, slots[:2], small_slots, order, "e_in_proj")
    e_w_in_sm = e_w_in_g.reshape((N_SHARDS,) + big_w[0].shape)
    e_w_out_f = e_w_out_g.reshape(w + w, d)
    sm = small_g4.reshape(N_SHARDS, -1)
    at = [0]

    def take(rows, q):
        blk = sm[:, at[0]:at[0] + rows * q]
        at[0] += rows * q
        return _col_shards_to_full(blk, rows)

    a_conv_f = take(CONV_A, wq)
    b_conv_f = take(CONV_B, wq)
    o_pre_f = take(1, cq)
    o_post_f = take(1, cq)
    cs_f = take(1, cq)
    cb_f = take(N_GROUPS, gq).reshape(1, c)

    mixer_steps = s // _row_tile(s, 128)
    (u0, x1, y0), odd_g = _even_mixer_fwd(p0, e_w_out_f, x2, e_norm_post, a_conv_f, b_conv_f, e_b_conv_bias, e_b_ln_g,
                                          e_b_ln_b, "e_mixer_out_proj",
                                          comm=_gather_comm(slots[2:], mixer_steps // 2, (25 * mixer_steps) // 32))
    o_w_in_sm = odd_g[0].reshape((N_SHARDS,) + big_w[2].shape)
    cw_f = odd_g[1].reshape(N_SHARDS, N_GROUPS, gq, gc).transpose(1, 0, 2, 3).reshape(N_GROUPS, gc, gc)
    o_w_out_f = odd_g[2].reshape(c, d)
    p1, h1 = _norm_matmul(x1, o_pre_f, o_w_in_sm, "o_in_proj")
    u1 = _odd_mixer_fwd(p1, cw_f, cb_f, cs_f, "o_mixer_fwd")
    d_y1, d_x2, d_o_post, loss_part = _matmul_post_loss(u1, o_w_out_f, x1, o_post_f, target, "o_out_proj_loss")

    def as_pieces(g, k):
        return g.reshape(N_SHARDS, 2, big_w[k].shape[0] // 2, big_w[k].shape[1])

    def chip_sums(ks, pieces, from_sibling):
        both = [_chip_sum(g, o, coords, "chip_sum_%d" % k) for k, g, o in zip(ks, pieces, from_sibling)]
        return [b[0] for b in both], [b[1] for b in both]

    g_o_w_out = _matmul_tn(u1, d_y1, 1, "o_w_out_grad")
    d_u1, _ = _matmul_nt(d_y1, o_w_out_f[None], "o_out_proj_bwd")
    d_p1, d_cw, d_cb, d_cs = _odd_mixer_bwd(p1, d_u1, cw_f, cb_f, cs_f, "o_mixer_bwd")
    g_o_w_in = _matmul_tn(h1, d_p1, N_SHARDS, "o_w_in_grad")
    g_cw = d_cw.reshape(N_GROUPS, N_SHARDS, gq, gc).transpose(1, 0, 2, 3).astype(BF16)
    pieces_o = [as_pieces(g_o_w_in, 2), as_pieces(g_cw, 3), as_pieces(g_o_w_out, 4)]
    d_h1, sibling_o = _matmul_nt(d_p1, o_w_in_sm, "o_in_proj_bwd", comm=_exchange_comm(pieces_o))
    d_x1, d_o_pre, d_y0, d_e_post = _norm_bwd(d_h1, x1, o_pre_f, d_x2, "o_pre_norm_bwd", post=(y0, e_norm_post))

    pieces_e = [as_pieces(_matmul_tn(u0, d_y0, 1, "e_w_out_grad"), 1)]
    d_u0, sibling_e = _matmul_nt(d_y0, e_w_out_f[None], "e_out_proj_bwd", comm=_exchange_comm(pieces_e))
    sums_a, landing_a = chip_sums([1, 2, 3, 4], pieces_e + pieces_o, sibling_e + sibling_o)
    (d_p0, d_a_conv, d_b_conv, d_bias, d_ln_g, d_ln_b), landed_a = _even_mixer_bwd(
        p0, d_u0, a_conv_f, b_conv_f, e_b_conv_bias, e_b_ln_g, e_b_ln_b, "e_mixer_bwd",
        comm=_scatter_comm(sums_a, landing_a))
    pieces_b = [as_pieces(_matmul_tn(h0, d_p0, N_SHARDS, "e_w_in_grad"), 0)]
    sums_b, landing_b = chip_sums([0], pieces_b, _comm_only(_exchange_comm(pieces_b), "exchange_core_halves"))
    d_h0, landed_b = _matmul_nt(d_p0, e_w_in_sm, "e_in_proj_bwd", comm=_scatter_comm(sums_b, landing_b))
    grad_x, d_e_pre = _norm_bwd(d_h0, x2, e_norm_pre, d_x1, "e_pre_norm_bwd")

    landed = landed_b + landed_a
    reduced = [_shard_sum(sc, coords, "shard_sum_%d" % k) for k, sc in enumerate(landed)]
    small_parts = _flat_rows([loss_part[0], d_e_pre, d_e_post, d_bias, d_ln_g, d_ln_b, d_a_conv, d_b_conv,
                              d_o_pre, d_o_post, d_cs, d_cb])
    small_rows = lax.dynamic_update_index_in_dim(jnp.zeros((N_DEVICES,) + small_parts.shape, F32), small_parts,
                                                 2 * chip + core, 0)
    joined = _comm_only(_join_comm(reduced, small_rows), "join_core_halves")
    big_g = [j.reshape(a.shape) for j, a in zip(joined[:5], big_w)]
    small_sum = _sum_small(joined[5], "small_sum").reshape(-1)
    (loss_row, g_e_pre, g_e_post, g_bias, g_ln_g, g_ln_b, g_a_conv_f, g_b_conv_f, g_o_pre_f, g_o_post_f, g_cs_f,
     g_cb_f) = _unflatten(small_sum, [(LANES,), (1, d), (1, d), (1, w), (1, w), (1, w), (CONV_A, w), (CONV_B, w),
                                      (1, c), (1, c), (1, c), (1, c)])
    loss = loss_row[0]
    g_a_conv = _my_col_shard(g_a_conv_f, chip)
    g_b_conv = _my_col_shard(g_b_conv_f, chip)
    g_o_pre = _my_col_shard(g_o_pre_f, chip)
    g_o_post = _my_col_shard(g_o_post_f, chip)
    g_cs = _my_col_shard(g_cs_f, chip)
    g_cb = _my_col_shard(g_cb_f.reshape(N_GROUPS, gc), chip)

    big_upd = [_adamw(wt, g, m, v, "adamw_%d" % k) for k, (wt, g, m, v) in enumerate(zip(big_w, big_g, big_m, big_v))]
    small_w = [e_norm_pre, e_norm_post, e_b_conv_bias, e_b_ln_g, e_b_ln_b, e_a_conv[0], e_b_conv[0],
               o_norm_pre, o_norm_post, o_c_b[0], o_c_scale]
    small_m = [m_e_norm_pre, m_e_norm_post, m_e_b_conv_bias, m_e_b_ln_g, m_e_b_ln_b, m_e_a_conv[0], m_e_b_conv[0],
               m_o_norm_pre, m_o_norm_post, m_o_c_b[0], m_o_c_scale]
    small_v = [v_e_norm_pre, v_e_norm_post, v_e_b_conv_bias, v_e_b_ln_g, v_e_b_ln_b, v_e_a_conv[0], v_e_b_conv[0],
               v_o_norm_pre, v_o_norm_post, v_o_c_b[0], v_o_c_scale]
    small_g = [g_e_pre, g_e_post, g_bias, g_ln_g, g_ln_b, g_a_conv, g_b_conv, g_o_pre, g_o_post, g_cb, g_cs]
    small_delta, small_new_m, small_new_v = _adamw_many(small_w, small_g, small_m, small_v, "adamw_small")

    def ordered(small, big):
        (n_pre, n_post, bias, ln_g, ln_b, a_conv, b_conv, o_pre, o_post, cb, cs) = small
        (w_in, w_out, ow_in, cw, ow_out) = big
        return [n_pre, n_post, w_in[None], a_conv[None], b_conv[None], bias, ln_g, ln_b, w_out[None], o_pre, o_post,
                ow_in[None], cw.reshape(1, N_GROUPS, gq, gc), cb[None], cs, ow_out[None]]

    grads = ordered(small_g, big_g)
    deltas = ordered(small_delta, [u[0] for u in big_upd])
    new_m = ordered(small_new_m, [u[1] for u in big_upd])
    new_v = ordered(small_new_v, [u[2] for u in big_upd])
    return (loss, grad_x.reshape(1, s, d), *grads, *deltas, *new_m, *new_v)
```

```python
import functools

import jax
import jax.numpy as jnp
from jax import lax
from jax.experimental import pallas as pl
from jax.experimental.pallas import tpu as pltpu

F32 = jnp.float32
BF16 = jnp.bfloat16
MESH = pl.DeviceIdType.MESH

EPS = 1e-6
CONV_A = 3
CONV_B = 31
POOL_WINDOWS = (2, 4, 8, 16)
N_GROUPS = len(POOL_WINDOWS)
N_SHARDS = 4
N_DEVICES = 8
ADAM_LR = 0.001
ADAM_B1 = 0.9
ADAM_B2 = 0.999
ADAM_EPS = 1e-08
ADAM_WD = 0.01
ADAM_STEP = 10

LANES = 128
SUBLANES_BF16 = 16
HALO_A = 8
HALO_B = 32
HALO_P = 16
PAD_P = 8
SHIFTS = 8
ROW_BLOCK = 32
LANE_BLOCK = 256
LANE_PASS = 256
VMEM_LIMIT = 56 * 1024 * 1024
TN_ACC_BYTES = 8 * 1024 * 1024
EVEN_BWD_ROWS = 128


def _row_tile(n, pref):
    t = max(min(n, pref) // SUBLANES_BF16, 1) * SUBLANES_BF16
    while t > SUBLANES_BF16 and (n % t or t % SUBLANES_BF16):
        t -= SUBLANES_BF16
    assert n % t == 0, (n, pref)
    return t


def _col_chunk(n, pref):
    t = (min(n, pref) // LANES) * LANES
    while t > LANES and n % t:
        t -= LANES
    assert t >= LANES and n % t == 0, (n, pref)
    return t


def _params(*sem):
    return pltpu.CompilerParams(dimension_semantics=tuple(sem) if sem else None, vmem_limit_bytes=VMEM_LIMIT)


ANY = pl.BlockSpec(memory_space=pl.ANY)


class _Comm:
    def __init__(self, srcs, bufs, n_sems, phases, finish):
        self.srcs, self.bufs, self.n_sems, self.phases, self.finish = list(srcs), list(bufs), n_sems, phases, finish


def _call(body, *, name, grid, in_specs, out_specs, out_shape, args, scratch_shapes=(), comm=None):
    params = _params(*(("arbitrary",) * len(grid)))
    if comm is None:
        out = pl.pallas_call(body, name=name, grid=grid, in_specs=in_specs, out_specs=out_specs, out_shape=out_shape,
                             scratch_shapes=scratch_shapes, compiler_params=params)(*args)
        return list(out), []
    n_in, n_out, n_scr = len(in_specs), len(out_specs), len(scratch_shapes)
    ns, nb = len(comm.srcs), len(comm.bufs)
    total = 1
    for size in grid:
        total *= size

    def fused(*refs):
        ins, srcs = refs[:n_in], refs[n_in:n_in + ns]
        at = n_in + ns + nb
        outs, bufs = refs[at:at + n_out], refs[at + n_out:at + n_out + nb]
        scratch = refs[at + n_out + nb:at + n_out + nb + n_scr]
        send_sems, recv_sems = refs[-2:]
        step = 0
        for axis, size in enumerate(grid):
            step = step * size + pl.program_id(axis)
        for when, fn in comm.phases:
            pl.when(step == when)(functools.partial(fn, srcs, bufs, send_sems, recv_sems))
        body(*ins, *outs, *scratch)
        pl.when(step == total - 1)(functools.partial(comm.finish, srcs, bufs, send_sems, recv_sems))

    out = pl.pallas_call(
        fused, name=name, grid=grid,
        in_specs=list(in_specs) + [ANY] * (ns + nb), out_specs=list(out_specs) + [ANY] * nb,
        out_shape=list(out_shape) + [jax.ShapeDtypeStruct(b.shape, b.dtype) for b in comm.bufs],
        input_output_aliases={n_in + ns + i: n_out + i for i in range(nb)},
        scratch_shapes=list(scratch_shapes) + [pltpu.SemaphoreType.DMA((comm.n_sems,))] * 2,
        compiler_params=params,
    )(*args, *comm.srcs, *comm.bufs)
    return list(out[:n_out]), list(out[n_out:])


def _comm_only(comm, name):
    ns, nb = len(comm.srcs), len(comm.bufs)

    def body(*refs):
        srcs, bufs = refs[:ns], refs[ns + nb:ns + 2 * nb]
        send_sems, recv_sems = refs[-2:]
        for _, fn in comm.phases:
            fn(srcs, bufs, send_sems, recv_sems)
        comm.finish(srcs, bufs, send_sems, recv_sems)

    return pl.pallas_call(
        body, name=name, in_specs=[ANY] * (ns + nb), out_specs=[ANY] * nb,
        out_shape=[jax.ShapeDtypeStruct(b.shape, b.dtype) for b in comm.bufs],
        input_output_aliases={ns + i: i for i in range(nb)},
        scratch_shapes=[pltpu.SemaphoreType.DMA((comm.n_sems,))] * 2,
    )(*comm.srcs, *comm.bufs)


def _sigmoid(v):
    return jax.nn.sigmoid(v)


def _dsilu(v, s):
    return s * (1.0 + v * (1.0 - s))


def _mean_last(v):
    return jnp.mean(v, axis=-1, keepdims=True)


def _sum_rows(v):
    return jnp.sum(v, axis=0, keepdims=True)


def _norm_matmul(x, gain, w_sm, name):
    s, d = x.shape
    n_sh, _, ns = w_sm.shape
    tm = _row_tile(s, 1024)

    def body(x_ref, g_ref, w_ref, p_ref, h_ref):
        @pl.when(pl.program_id(1) == 0)
        def _():
            xv = x_ref[...]
            r = lax.rsqrt(_mean_last(xv * xv) + EPS)
            h_ref[...] = (xv * r * g_ref[...]).astype(BF16)

        p_ref[...] = jnp.dot(h_ref[...], w_ref[0], preferred_element_type=F32)

    return _call(
        body, name=name, grid=(s // tm, n_sh),
        in_specs=[pl.BlockSpec((tm, d), lambda i, j: (i, 0)),
                  pl.BlockSpec((1, d), lambda i, j: (0, 0)),
                  pl.BlockSpec((1, d, ns), lambda i, j: (j, 0, 0))],
        out_specs=[pl.BlockSpec((tm, ns), lambda i, j: (i, j)),
                   pl.BlockSpec((tm, d), lambda i, j: (i, 0))],
        out_shape=[jax.ShapeDtypeStruct((s, n_sh * ns), F32), jax.ShapeDtypeStruct((s, d), BF16)],
        args=(x, gain, w_sm))[0]


def _matmul_post_loss(u, w, x_res, gain, target, name):
    s, k = u.shape
    d = w.shape[1]
    tm = _row_tile(s, 512)
    half = tm // 2

    def body(u_ref, w_ref, x_ref, g_ref, t_ref, dy_ref, dout_ref, dg_ref, loss_ref):
        @pl.when(pl.program_id(0) == 0)
        def _():
            dg_ref[...] = jnp.zeros_like(dg_ref)
            loss_ref[...] = jnp.zeros_like(loss_ref)

        g = g_ref[...]
        for rows in (slice(0, half), slice(half, tm)):
            y = jnp.dot(u_ref[rows, :], w_ref[...], preferred_element_type=F32)
            r = lax.rsqrt(_mean_last(y * y) + EPS)
            n = y * r
            err = x_ref[rows, :] + n * g - t_ref[rows, :]
            loss_ref[...] += 0.5 * jnp.sum(_mean_last(err * err))
            dout = err * (1.0 / d)
            dout_ref[rows, :] = dout
            dg_ref[...] += _sum_rows(dout * n)
            dn = dout * g
            dy_ref[rows, :] = (r * (dn - n * _mean_last(dn * n))).astype(BF16)

    return pl.pallas_call(
        body, name=name, grid=(s // tm,),
        in_specs=[pl.BlockSpec((tm, k), lambda i: (i, 0)),
                  pl.BlockSpec((k, d), lambda i: (0, 0), pipeline_mode=pl.Buffered(1)),
                  pl.BlockSpec((tm, d), lambda i: (i, 0)),
                  pl.BlockSpec((1, d), lambda i: (0, 0)),
                  pl.BlockSpec((tm, d), lambda i: (i, 0))],
        out_specs=[pl.BlockSpec((tm, d), lambda i: (i, 0)),
                   pl.BlockSpec((tm, d), lambda i: (i, 0)),
                   pl.BlockSpec((1, d), lambda i: (0, 0)),
                   pl.BlockSpec((8, LANES), lambda i: (0, 0))],
        out_shape=[jax.ShapeDtypeStruct((s, d), BF16), jax.ShapeDtypeStruct((s, d), F32),
                   jax.ShapeDtypeStruct((1, d), F32), jax.ShapeDtypeStruct((8, LANES), F32)],
        compiler_params=_params("arbitrary"),
    )(u, w, x_res, gain, target)


def _matmul_nt(a, w_sm, name, comm=None):
    s, ncols = a.shape
    n_sh, r, ns = w_sm.shape
    assert ncols == n_sh * ns
    tm = _row_tile(s, 1024)
    nc = _col_chunk(ns, 1792)
    per = ns // nc
    steps = n_sh * per

    def body(a_ref, w_ref, o_ref):
        part = lax.dot_general(a_ref[...], w_ref[0], (((1,), (1,)), ((), ())), preferred_element_type=F32)

        @pl.when(pl.program_id(1) == 0)
        def _():
            o_ref[...] = part

        @pl.when(pl.program_id(1) > 0)
        def _():
            o_ref[...] += part

    out, bufs = _call(
        body, name=name, grid=(s // tm, steps),
        in_specs=[pl.BlockSpec((tm, nc), lambda i, j: (i, j)),
                  pl.BlockSpec((1, r, nc), lambda i, j: (j // per, 0, j % per))],
        out_specs=[pl.BlockSpec((tm, r), lambda i, j: (i, 0))],
        out_shape=[jax.ShapeDtypeStruct((s, r), F32)],
        args=(a, w_sm), comm=comm)
    return out[0], bufs


def _matmul_tn(a, b, n_sh, name):
    s, k = a.shape
    n = b.shape[1]
    ns = n // n_sh
    tk = _col_chunk(k, TN_ACC_BYTES // (4 * ns))
    ts = _row_tile(s, 2048)
    n_s = s // ts

    def body(a_ref, b_ref, o_ref, acc_ref):
        part = lax.dot_general(a_ref[...], b_ref[...], (((0,), (0,)), ((), ())), preferred_element_type=F32)

        @pl.when(pl.program_id(2) == 0)
        def _():
            acc_ref[...] = part

        @pl.when(pl.program_id(2) > 0)
        def _():
            acc_ref[...] += part

        @pl.when(pl.program_id(2) == n_s - 1)
        def _():
            o_ref[0] = acc_ref[...].astype(BF16)

    return pl.pallas_call(
        body, name=name, grid=(n_sh, k // tk, n_s),
        in_specs=[pl.BlockSpec((ts, tk), lambda j, i, t: (t, i)),
                  pl.BlockSpec((ts, ns), lambda j, i, t: (t, j))],
        out_specs=pl.BlockSpec((1, tk, ns), lambda j, i, t: (j, i, 0)),
        out_shape=jax.ShapeDtypeStruct((n_sh, k, ns), BF16),
        scratch_shapes=[pltpu.VMEM((tk, ns), F32)],
        compiler_params=_params("arbitrary", "arbitrary", "arbitrary"),
    )(a, b)


def _norm_bwd(dh, x, gain, dres, name, post=None):
    s, d = x.shape
    tm = _row_tile(s, 256)
    with_post = post is not None

    def rms_bwd(dout, v, g):
        r = lax.rsqrt(_mean_last(v * v) + EPS)
        n = v * r
        dn = dout * g
        return r * (dn - n * _mean_last(dn * n)), _sum_rows(dout * n)

    def body(*refs):
        if with_post:
            dh_ref, x_ref, g_ref, dres_ref, y_ref, gp_ref, dx_ref, dg_ref, dy_ref, dgp_ref = refs
        else:
            dh_ref, x_ref, g_ref, dres_ref, dx_ref, dg_ref = refs

        @pl.when(pl.program_id(0) == 0)
        def _():
            dg_ref[...] = jnp.zeros_like(dg_ref)
            if with_post:
                dgp_ref[...] = jnp.zeros_like(dgp_ref)

        dv, dg = rms_bwd(dh_ref[...], x_ref[...], g_ref[...])
        dx = dres_ref[...] + dv
        dx_ref[...] = dx
        dg_ref[...] += dg
        if with_post:
            dy, dgp = rms_bwd(dx, y_ref[...], gp_ref[...])
            dy_ref[...] = dy.astype(BF16)
            dgp_ref[...] += dgp

    row = pl.BlockSpec((tm, d), lambda i: (i, 0))
    vec = pl.BlockSpec((1, d), lambda i: (0, 0))
    in_specs = [row, row, vec, row]
    out_specs = [row, vec]
    out_shape = [jax.ShapeDtypeStruct((s, d), F32), jax.ShapeDtypeStruct((1, d), F32)]
    args = [dh, x, gain, dres]
    if with_post:
        in_specs += [row, vec]
        out_specs += [row, vec]
        out_shape += [jax.ShapeDtypeStruct((s, d), BF16), jax.ShapeDtypeStruct((1, d), F32)]
        args += list(post)
    return _call(body, name=name, grid=(s // tm,), in_specs=in_specs, out_specs=out_specs, out_shape=out_shape,
                 args=args)[0]


def _fill_shifted_down(sh, rows):
    for b in range(1, SHIFTS):
        sh[b, SHIFTS:rows, :] = sh[0, SHIFTS - b:rows - b, :]


def _fill_shifted_up(sh, rows):
    for b in range(1, SHIFTS):
        sh[b, 0:rows - SHIFTS, :] = sh[0, b:rows - SHIFTS + b, :]


def _for_blocks(ts, w, fn):
    lb = min(LANE_BLOCK, w)
    for l0 in range(0, w, lb):
        def rows(rb, carry, l0=l0):
            fn(pl.multiple_of(rb * ROW_BLOCK, ROW_BLOCK), slice(l0, l0 + lb))
            return carry

        lax.fori_loop(0, ts // ROW_BLOCK, rows, 0)


TAP_SPAN = SHIFTS * ((CONV_B - 1) // SHIFTS)
WINDOW = ROW_BLOCK + TAP_SPAN


def _taps_of(b):
    return [(a, SHIFTS * a + b) for a in range((CONV_B - 1 - b) // SHIFTS + 1)]


def _conv31(sh, base, step, wt_ref, bias_ref, out_ref, ts, w):
    low = min(0, step * (TAP_SPAN // SHIFTS))

    def block(r0, lanes):
        acc = [jnp.zeros((SHIFTS, lanes.stop - lanes.start), F32) for _ in range(ROW_BLOCK // SHIFTS)]
        for b in range(SHIFTS):
            window = sh[b, pl.ds(pl.multiple_of(r0 + (base + low), SHIFTS), WINDOW), lanes]
            for a, j in (_taps_of(b) if step > 0 else reversed(_taps_of(b))):
                at = step * a - low
                wt = wt_ref[CONV_B - 1 - j, :, lanes]
                acc = [v + wt * window[at + SHIFTS * r:at + SHIFTS * (r + 1), :] for r, v in enumerate(acc)]
        for r, v in enumerate(acc):
            if bias_ref is not None:
                v = v + bias_ref[:, lanes]
            out_ref[pl.ds(pl.multiple_of(r0 + SHIFTS * r, SHIFTS), SHIFTS), lanes] = v

    _for_blocks(ts, w, block)


def _conv31_weight_grad(d_sh, x_sh, wacc, ts, w):
    def block(r0, lanes):
        d = d_sh[0, pl.ds(r0, ROW_BLOCK), lanes]
        for b in range(SHIFTS):
            window = x_sh[b, pl.ds(pl.multiple_of(r0 + (HALO_B - TAP_SPAN), SHIFTS), WINDOW), lanes]
            for a, j in _taps_of(b):
                at = TAP_SPAN - SHIFTS * a
                prod = d * window[at:at + ROW_BLOCK, :]
                part = prod[0:SHIFTS, :]
                for q in range(1, ROW_BLOCK // SHIFTS):
                    part = part + prod[q * SHIFTS:(q + 1) * SHIFTS, :]
                wacc[CONV_B - 1 - j, :, lanes] += part

    _for_blocks(ts, w, block)


def _even_forward_tile(p_ref, halo_ref, first, a_conv_ref, b_conv_ref, bias_ref, lng_ref, lnb_ref, qbuf, ysh, y1buf,
                       wb, w, ts):
    @pl.when(pl.program_id(0) == 0)
    def _():
        for k in range(CONV_B):
            wb[k] = jnp.broadcast_to(b_conv_ref[k:k + 1, :], (SHIFTS, w))

    def col(ref, k, rows=slice(None)):
        return ref[rows, k * w:(k + 1) * w]

    a_x, a_b, a_c, a_z = col(p_ref, 0), col(p_ref, 1), col(p_ref, 2), col(p_ref, 3)
    b_val, b_gate, b_z = col(p_ref, 4), col(p_ref, 5), col(p_ref, 6)
    keep = jnp.where(first, 0.0, 1.0)

    rows_a = slice(HALO_B - HALO_A, HALO_B)
    qbuf[0:HALO_A, :] = col(halo_ref, 2, rows_a) * col(halo_ref, 0, rows_a) * keep
    qbuf[HALO_A:HALO_A + ts, :] = a_c * a_x
    cq = jnp.zeros((ts, w), F32)
    for j in range(CONV_A):
        cq = cq + a_conv_ref[CONV_A - 1 - j:CONV_A - j, :] * qbuf[HALO_A - j:HALO_A - j + ts, :]
    ya = a_b * cq

    ysh[0, 0:HALO_B, :] = col(halo_ref, 4) * _sigmoid(col(halo_ref, 5)) * keep
    ysh[0, HALO_B:HALO_B + ts, :] = b_val * _sigmoid(b_gate)
    _fill_shifted_down(ysh, HALO_B + ts)
    _conv31(ysh, HALO_B, -SHIFTS, wb, bias_ref, y1buf, ts, w)
    yb1 = y1buf[...]
    xc = yb1 - _mean_last(yb1)
    rstd = lax.rsqrt(_mean_last(xc * xc) + EPS)
    xhat = xc * rstd
    yb2 = xhat * lng_ref[...] + lnb_ref[...]
    return dict(a_x=a_x, a_b=a_b, a_c=a_c, a_z=a_z, b_val=b_val, b_gate=b_gate, b_z=b_z,
                cq=cq, ya=ya, rstd=rstd, xhat=xhat, yb2=yb2)


def _even_specs(s, w, ts):
    tile = pl.BlockSpec((ts, 7 * w), lambda i: (i, 0))
    halo = pl.BlockSpec((HALO_B, 7 * w), lambda i: (jnp.maximum(i * (ts // HALO_B) - 1, 0), 0))
    return tile, halo


def _small_specs(shapes, index=lambda i: (0, 0)):
    return [pl.BlockSpec(sh, index) for sh in shapes]


def _even_mixer_fwd(p, w_out, x_res, gain, a_conv, b_conv, bias, ln_g, ln_b, name, comm=None):
    s, d = x_res.shape
    w = p.shape[1] // 7
    ts = _row_tile(s, 128)
    assert ts % HALO_B == 0

    def body(p_ref, halo_ref, wout_ref, x_ref, g_ref, ac_ref, bc_ref, bias_ref, lng_ref, lnb_ref,
             u_ref, xn_ref, y_ref, qbuf, ysh, y1buf, wb):
        first = pl.program_id(0) == 0
        f = _even_forward_tile(p_ref, halo_ref, first, ac_ref, bc_ref, bias_ref, lng_ref, lnb_ref, qbuf, ysh, y1buf,
                               wb, w, ts)
        yb3 = f["yb2"] * _sigmoid(f["yb2"])
        u_a = (f["ya"] * (f["a_z"] * _sigmoid(f["a_z"]))).astype(BF16)
        u_b = (yb3 * (f["b_z"] * _sigmoid(f["b_z"]))).astype(BF16)
        u_ref[:, 0:w] = u_a
        u_ref[:, w:2 * w] = u_b
        y = (jnp.dot(u_a, wout_ref[0:w, :], preferred_element_type=F32)
             + jnp.dot(u_b, wout_ref[w:2 * w, :], preferred_element_type=F32))
        r = lax.rsqrt(_mean_last(y * y) + EPS)
        y_ref[...] = y
        xn_ref[...] = x_ref[...] + (y * r) * g_ref[...]

    tile, halo = _even_specs(s, w, ts)
    row = pl.BlockSpec((ts, d), lambda i: (i, 0))
    return _call(
        body, name=name, grid=(s // ts,),
        in_specs=[tile, halo, pl.BlockSpec((2 * w, d), lambda i: (0, 0)), row, pl.BlockSpec((1, d), lambda i: (0, 0))]
        + _small_specs([(CONV_A, w), (CONV_B, w), (1, w), (1, w), (1, w)]),
        out_specs=[pl.BlockSpec((ts, 2 * w), lambda i: (i, 0)), row, row],
        out_shape=[jax.ShapeDtypeStruct((s, 2 * w), BF16), jax.ShapeDtypeStruct((s, d), F32),
                   jax.ShapeDtypeStruct((s, d), F32)],
        scratch_shapes=[pltpu.VMEM((HALO_A + ts, w), F32), pltpu.VMEM((SHIFTS, HALO_B + ts, w), F32),
                        pltpu.VMEM((ts, w), F32), pltpu.VMEM((CONV_B, SHIFTS, w), F32)],
        args=(p, p, w_out, x_res, gain, a_conv, b_conv, bias, ln_g, ln_b), comm=comm)


def _even_mixer_bwd(p, du, a_conv, b_conv, bias, ln_g, ln_b, name, comm=None):
    s = p.shape[0]
    w = p.shape[1] // 7
    ts = _row_tile(s, EVEN_BWD_ROWS)
    nt = s // ts
    assert ts % HALO_B == 0

    def body(p_ref, halo_ref, du_ref, ac_ref, bc_ref, bias_ref, lng_ref, lnb_ref,
             dp_ref, dac_ref, dbc_ref, dbias_ref, dlng_ref, dlnb_ref,
             qbuf, ysh, y1buf, dqbuf, dsh, dy0buf, wacc, carry_dq, carry_dy, wb):
        step = pl.program_id(0)
        first = step == nt - 1

        @pl.when(step == 0)
        def _():
            for ref in (dac_ref, dbias_ref, dlng_ref, dlnb_ref, wacc, carry_dq, carry_dy):
                ref[...] = jnp.zeros_like(ref)

        @pl.when(step == 0)
        def _():
            for k in range(CONV_B):
                wb[k] = jnp.broadcast_to(bc_ref[k:k + 1, :], (SHIFTS, w))

        keep = jnp.where(first, 0.0, 1.0)
        lane_blocks = [slice(l0, l0 + min(LANE_PASS, w)) for l0 in range(0, w, min(LANE_PASS, w))]

        def col(ref, k, lanes, rows=slice(None)):
            return ref[rows, k * w + lanes.start:k * w + lanes.stop]


        rows_a = slice(HALO_B - HALO_A, HALO_B)
        for lanes in lane_blocks:
            a_x, a_b, a_c, a_z = (col(p_ref, k, lanes) for k in range(4))
            qbuf[0:HALO_A, lanes] = col(halo_ref, 2, lanes, rows_a) * col(halo_ref, 0, lanes, rows_a) * keep
            qbuf[HALO_A:HALO_A + ts, lanes] = a_c * a_x
            cq = jnp.zeros((ts, lanes.stop - lanes.start), F32)
            for j in range(CONV_A):
                cq = cq + ac_ref[CONV_A - 1 - j:CONV_A - j, lanes] * qbuf[HALO_A - j:HALO_A - j + ts, lanes]
            s_az = _sigmoid(a_z)
            du_a = du_ref[:, lanes]
            d_ya = du_a * (a_z * s_az)
            dp_ref[:, 3 * w + lanes.start:3 * w + lanes.stop] = (du_a * (a_b * cq) * _dsilu(a_z, s_az)).astype(BF16)
            dp_ref[:, 1 * w + lanes.start:1 * w + lanes.stop] = (d_ya * cq).astype(BF16)
            d_cq = d_ya * a_b
            dqbuf[0:ts, lanes] = d_cq
            dqbuf[ts:ts + HALO_A, lanes] = carry_dq[:, lanes]
            carry_dq[:, lanes] = d_cq[0:HALO_A, :]
            d_q = jnp.zeros_like(cq)
            for o in range(CONV_A):
                d_q = d_q + ac_ref[CONV_A - 1 - o:CONV_A - o, lanes] * dqbuf[o:o + ts, lanes]
            for j in range(CONV_A):
                k = CONV_A - 1 - j
                dac_ref[k:k + 1, lanes] += _sum_rows(d_cq * qbuf[HALO_A - j:HALO_A - j + ts, lanes])
            dp_ref[:, 2 * w + lanes.start:2 * w + lanes.stop] = (d_q * a_x).astype(BF16)
            dp_ref[:, 0 * w + lanes.start:0 * w + lanes.stop] = (d_q * a_c).astype(BF16)
            ysh[0, 0:HALO_B, lanes] = col(halo_ref, 4, lanes) * _sigmoid(col(halo_ref, 5, lanes)) * keep
            ysh[0, HALO_B:HALO_B + ts, lanes] = col(p_ref, 4, lanes) * _sigmoid(col(p_ref, 5, lanes))
        _fill_shifted_down(ysh, HALO_B + ts)
        _conv31(ysh, HALO_B, -SHIFTS, wb, bias_ref, y1buf, ts, w)

        total = jnp.zeros((ts, 1), F32)
        for lanes in lane_blocks:
            total = total + jnp.sum(y1buf[:, lanes], axis=-1, keepdims=True)
        mu = total * (1.0 / w)
        total = jnp.zeros((ts, 1), F32)
        for lanes in lane_blocks:
            xc = y1buf[:, lanes] - mu
            total = total + jnp.sum(xc * xc, axis=-1, keepdims=True)
        rstd = lax.rsqrt(total * (1.0 / w) + EPS)

        sum_dxh = jnp.zeros((ts, 1), F32)
        sum_dxh_xhat = jnp.zeros((ts, 1), F32)
        for lanes in lane_blocks:
            xhat = (y1buf[:, lanes] - mu) * rstd
            yb2 = xhat * lng_ref[:, lanes] + lnb_ref[:, lanes]
            b_z = col(p_ref, 6, lanes)
            s_bz, s_y2 = _sigmoid(b_z), _sigmoid(yb2)
            du_b = du_ref[:, w + lanes.start:w + lanes.stop]
            dp_ref[:, 6 * w + lanes.start:6 * w + lanes.stop] = (du_b * (yb2 * s_y2) * _dsilu(b_z, s_bz)).astype(BF16)
            d_yb2 = du_b * (b_z * s_bz) * _dsilu(yb2, s_y2)
            dlng_ref[:, lanes] += _sum_rows(d_yb2 * xhat)
            dlnb_ref[:, lanes] += _sum_rows(d_yb2)
            d_xh = d_yb2 * lng_ref[:, lanes]
            sum_dxh = sum_dxh + jnp.sum(d_xh, axis=-1, keepdims=True)
            sum_dxh_xhat = sum_dxh_xhat + jnp.sum(d_xh * xhat, axis=-1, keepdims=True)
            y1buf[:, lanes] = xhat
            dy0buf[:, lanes] = d_xh
        mean_dxh = sum_dxh * (1.0 / w)
        mean_dxh_xhat = sum_dxh_xhat * (1.0 / w)

        for lanes in lane_blocks:
            d_yb1 = rstd * (dy0buf[:, lanes] - mean_dxh - y1buf[:, lanes] * mean_dxh_xhat)
            dbias_ref[:, lanes] += _sum_rows(d_yb1)
            dsh[0, 0:ts, lanes] = d_yb1
            dsh[0, ts:ts + HALO_B, lanes] = carry_dy[:, lanes]
            carry_dy[:, lanes] = d_yb1[0:HALO_B, :]
        _fill_shifted_up(dsh, ts + HALO_B)
        _conv31(dsh, 0, SHIFTS, wb, None, dy0buf, ts, w)
        _conv31_weight_grad(dsh, ysh, wacc, ts, w)

        @pl.when(step == nt - 1)
        def _():
            for k in range(CONV_B):
                dbc_ref[k:k + 1, :] = _sum_rows(wacc[k])

        for lanes in lane_blocks:
            d_yb0 = dy0buf[:, lanes]
            s_g = _sigmoid(col(p_ref, 5, lanes))
            dp_ref[:, 4 * w + lanes.start:4 * w + lanes.stop] = (d_yb0 * s_g).astype(BF16)
            dp_ref[:, 5 * w + lanes.start:5 * w + lanes.stop] = (
                d_yb0 * col(p_ref, 4, lanes) * s_g * (1.0 - s_g)).astype(BF16)

    rev = lambda i: (nt - 1 - i, 0)
    tile = pl.BlockSpec((ts, 7 * w), rev)
    halo = pl.BlockSpec((HALO_B, 7 * w), lambda i: (jnp.maximum((nt - 1 - i) * (ts // HALO_B) - 1, 0), 0))
    small = [(CONV_A, w), (CONV_B, w), (1, w), (1, w), (1, w)]
    return _call(
        body, name=name, grid=(nt,),
        in_specs=[tile, halo, pl.BlockSpec((ts, 2 * w), rev)] + _small_specs(small),
        out_specs=[pl.BlockSpec((ts, 7 * w), rev)] + _small_specs(small),
        out_shape=[jax.ShapeDtypeStruct((s, 7 * w), BF16)] + [jax.ShapeDtypeStruct(sh, F32) for sh in small],
        scratch_shapes=[pltpu.VMEM((HALO_A + ts, w), F32), pltpu.VMEM((SHIFTS, HALO_B + ts, w), F32),
                        pltpu.VMEM((ts, w), F32),
                        pltpu.VMEM((ts + HALO_A, w), F32), pltpu.VMEM((SHIFTS, ts + HALO_B, w), F32),
                        pltpu.VMEM((ts, w), F32), pltpu.VMEM((CONV_B, SHIFTS, w), F32),
                        pltpu.VMEM((HALO_A, w), F32), pltpu.VMEM((HALO_B, w), F32),
                        pltpu.VMEM((CONV_B, SHIFTS, w), F32)],
        args=(p, p, du, a_conv, b_conv, bias, ln_g, ln_b), comm=comm)


def _trailing_sums(buf_a, buf_b, cols, win, rows, first_out):
    src, dst, shift = buf_a, buf_b, 1
    while True:
        last = 2 * shift >= win
        lo = first_out if last else 0
        val = src[PAD_P + lo:PAD_P + rows, cols] + src[PAD_P + lo - shift:PAD_P + rows - shift, cols]
        if last:
            return val
        dst[PAD_P + lo:PAD_P + rows, cols] = val
        src, dst, shift = dst, src, 2 * shift


def _leading_sums(buf_a, buf_b, cols, win, rows, n_out):
    src, dst, shift = buf_a, buf_b, 1
    while True:
        last = 2 * shift >= win
        hi = n_out if last else rows
        val = src[0:hi, cols] + src[shift:hi + shift, cols]
        if last:
            return val
        dst[0:hi, cols] = val
        src, dst, shift = dst, src, 2 * shift


def _pool_forward_tile(p_ref, halo_ref, first, tile_index, cw_ref, cb_ref, cs_ref, vbuf, vtmp, c, gc, ts):
    vbuf[PAD_P:PAD_P + HALO_P, :] = halo_ref[...] * jnp.where(first, 0.0, 1.0)
    vbuf[PAD_P + HALO_P:PAD_P + HALO_P + ts, :] = p_ref[:, 0:c]
    pos = tile_index * ts + lax.broadcasted_iota(jnp.int32, (ts, 1), 0) + 1
    pooled, inv, gout = [], [], []
    for g, win in enumerate(POOL_WINDOWS):
        cols = slice(g * gc, (g + 1) * gc)
        acc = _trailing_sums(vbuf, vtmp, cols, win, HALO_P + ts, HALO_P)
        inv_g = 1.0 / jnp.minimum(pos, win).astype(F32)
        pooled_g = (acc * inv_g - p_ref[:, cols]).astype(BF16)
        pooled.append(pooled_g)
        inv.append(inv_g)
        gout.append(jnp.dot(pooled_g, cw_ref[g], preferred_element_type=F32) + cb_ref[:, cols])
    return pooled, inv, gout


def _odd_mixer_fwd(p, cw, cb, cs, name):
    s = p.shape[0]
    c = p.shape[1] // 2
    gc = c // N_GROUPS
    ts = _row_tile(s, 256)

    def body(p_ref, halo_ref, cw_ref, cb_ref, cs_ref, u_ref, vbuf, vtmp):
        i = pl.program_id(0)

        @pl.when(i == 0)
        def _():
            vbuf[0:PAD_P, :] = jnp.zeros((PAD_P, c), F32)
            vtmp[0:PAD_P, :] = jnp.zeros((PAD_P, c), F32)

        _, _, gout = _pool_forward_tile(p_ref, halo_ref, i == 0, i, cw_ref, cb_ref, cs_ref, vbuf, vtmp, c, gc, ts)
        for g in range(N_GROUPS):
            cols = slice(g * gc, (g + 1) * gc)
            z = p_ref[:, c + g * gc:c + (g + 1) * gc]
            u_ref[:, cols] = (gout[g] * cs_ref[:, cols] * (z * _sigmoid(z))).astype(BF16)

    return pl.pallas_call(
        body, name=name, grid=(s // ts,),
        in_specs=[pl.BlockSpec((ts, 2 * c), lambda i: (i, 0)),
                  pl.BlockSpec((HALO_P, c), lambda i: (jnp.maximum(i * (ts // HALO_P) - 1, 0), 0)),
                  pl.BlockSpec((N_GROUPS, gc, gc), lambda i: (0, 0, 0)),
                  pl.BlockSpec((1, c), lambda i: (0, 0)), pl.BlockSpec((1, c), lambda i: (0, 0))],
        out_specs=pl.BlockSpec((ts, c), lambda i: (i, 0)),
        out_shape=jax.ShapeDtypeStruct((s, c), BF16),
        scratch_shapes=[pltpu.VMEM((PAD_P + HALO_P + ts, c), F32)] * 2,
        compiler_params=_params("arbitrary"),
    )(p, p, cw, cb, cs)


def _odd_mixer_bwd(p, du, cw, cb, cs, name):
    s = p.shape[0]
    c = p.shape[1] // 2
    gc = c // N_GROUPS
    ts = _row_tile(s, 256)
    nt = s // ts

    def body(p_ref, halo_ref, du_ref, cw_ref, cb_ref, cs_ref, dp_ref, dcw_ref, dcb_ref, dcs_ref,
             vbuf, vtmp, ebuf, etmp, carry_e):
        step = pl.program_id(0)
        tile_index = nt - 1 - step

        @pl.when(step == 0)
        def _():
            for ref in (dcw_ref, dcb_ref, dcs_ref, carry_e):
                ref[...] = jnp.zeros_like(ref)
            for ref in (vbuf, vtmp):
                ref[0:PAD_P, :] = jnp.zeros((PAD_P, c), F32)
            for ref in (ebuf, etmp):
                ref[ts + HALO_P:ts + HALO_P + PAD_P, :] = jnp.zeros((PAD_P, c), F32)

        pooled, inv, gout = _pool_forward_tile(p_ref, halo_ref, tile_index == 0, tile_index, cw_ref, cb_ref, cs_ref,
                                               vbuf, vtmp, c, gc, ts)
        ebuf[ts:ts + HALO_P, :] = carry_e[...]
        for g, win in enumerate(POOL_WINDOWS):
            cols = slice(g * gc, (g + 1) * gc)
            z = p_ref[:, c + g * gc:c + (g + 1) * gc]
            sz = _sigmoid(z)
            du_g = du_ref[:, cols]
            scale = cs_ref[:, cols]
            d_y = du_g * (z * sz)
            dp_ref[:, c + g * gc:c + (g + 1) * gc] = (du_g * (gout[g] * scale) * _dsilu(z, sz)).astype(BF16)
            dcs_ref[:, cols] += _sum_rows(d_y * gout[g])
            d_gout = d_y * scale
            dcb_ref[:, cols] += _sum_rows(d_gout)
            d_gout_b = d_gout.astype(BF16)
            dcw_ref[g] += lax.dot_general(pooled[g], d_gout_b, (((0,), (0,)), ((), ())), preferred_element_type=F32)
            d_pool = lax.dot_general(d_gout_b, cw_ref[g], (((1,), (1,)), ((), ())), preferred_element_type=F32)
            e = d_pool * inv[g]
            ebuf[0:ts, cols] = e
            carry_e[:, cols] = e[0:HALO_P, :]
            d_v = _leading_sums(ebuf, etmp, cols, win, ts + HALO_P, ts) - d_pool
            dp_ref[:, cols] = d_v.astype(BF16)

    rev = lambda i: (nt - 1 - i, 0)
    small = [(N_GROUPS, gc, gc), (1, c), (1, c)]
    return pl.pallas_call(
        body, name=name, grid=(nt,),
        in_specs=[pl.BlockSpec((ts, 2 * c), rev),
                  pl.BlockSpec((HALO_P, c), lambda i: (jnp.maximum((nt - 1 - i) * (ts // HALO_P) - 1, 0), 0)),
                  pl.BlockSpec((ts, c), rev),
                  pl.BlockSpec((N_GROUPS, gc, gc), lambda i: (0, 0, 0)),
                  pl.BlockSpec((1, c), lambda i: (0, 0)), pl.BlockSpec((1, c), lambda i: (0, 0))],
        out_specs=[pl.BlockSpec((ts, 2 * c), rev),
                   pl.BlockSpec((N_GROUPS, gc, gc), lambda i: (0, 0, 0)),
                   pl.BlockSpec((1, c), lambda i: (0, 0)), pl.BlockSpec((1, c), lambda i: (0, 0))],
        out_shape=[jax.ShapeDtypeStruct((s, 2 * c), BF16)] + [jax.ShapeDtypeStruct(sh, F32) for sh in small],
        scratch_shapes=[pltpu.VMEM((PAD_P + HALO_P + ts, c), F32)] * 2 + [pltpu.VMEM((ts + HALO_P + PAD_P, c), F32)] * 2
        + [pltpu.VMEM((HALO_P, c), F32)],
        compiler_params=_params("arbitrary"),
    )(p, p, du, cw, cb, cs)


def _cast_into_slot(a, coords, name):
    r, cols = a.shape
    tr = _row_tile(r // 2, 1024)
    per = r // 2 // tr

    def body(co_ref, a_ref, o_ref):
        o_ref[0, 0] = a_ref[...].astype(BF16)

    return pl.pallas_call(
        body, name=name,
        grid_spec=pltpu.PrefetchScalarGridSpec(
            num_scalar_prefetch=1, grid=(2, per),
            in_specs=[pl.BlockSpec((tr, cols), lambda h, i, co: (h * per + i, 0))],
            out_specs=pl.BlockSpec((1, 1, tr, cols), lambda h, i, co: (co[0], h, i, 0))),
        out_shape=jax.ShapeDtypeStruct((N_SHARDS, 2, r // 2, cols), BF16),
        compiler_params=_params("arbitrary", "arbitrary"),
    )(coords, a)


def _chip_sum(g, other, coords, name):
    n_sh, _, r2, cols = g.shape
    tr = _row_tile(r2, 512)

    def body(co_ref, g_ref, o_ref, sum_ref, mine_ref):
        v = (g_ref[0, 0].astype(F32) + o_ref[0].astype(F32)).astype(BF16)
        sum_ref[0] = v

        @pl.when(pl.program_id(1) == co_ref[0])
        def _():
            mine_ref[0] = v

    piece = pl.BlockSpec((1, tr, cols), lambda i, s, co: (s, i, 0))
    return pl.pallas_call(
        body, name=name,
        grid_spec=pltpu.PrefetchScalarGridSpec(
            num_scalar_prefetch=1, grid=(r2 // tr, n_sh),
            in_specs=[pl.BlockSpec((1, 1, tr, cols), lambda i, s, co: (s, co[1], i, 0)), piece],
            out_specs=[piece, pl.BlockSpec((1, tr, cols), lambda i, s, co: (co[0], i, 0))]),
        out_shape=[jax.ShapeDtypeStruct((n_sh, r2, cols), BF16)] * 2,
        compiler_params=_params("arbitrary", "arbitrary"),
    )(coords, g, other)


def _shard_sum(pieces, coords, name):
    n_sh, r2, cols = pieces.shape
    tr = _row_tile(r2, 512)

    def body(co_ref, p_ref, o_ref):
        acc = p_ref[0].astype(F32)
        for k in range(1, n_sh):
            acc = acc + p_ref[k].astype(F32)
        o_ref[0] = acc

    return pl.pallas_call(
        body, name=name,
        grid_spec=pltpu.PrefetchScalarGridSpec(
            num_scalar_prefetch=1, grid=(r2 // tr,),
            in_specs=[pl.BlockSpec((n_sh, tr, cols), lambda i, co: (0, i, 0))],
            out_specs=pl.BlockSpec((1, tr, cols), lambda i, co: (co[1], i, 0))),
        out_shape=jax.ShapeDtypeStruct((2, r2, cols), F32),
        compiler_params=_params("arbitrary"),
    )(coords, pieces)


def _sum_small(a, name):
    n, r, cols = a.shape

    def body(a_ref, o_ref):
        acc = a_ref[0]
        for k in range(1, n):
            acc = acc + a_ref[k]
        o_ref[...] = acc

    return pl.pallas_call(
        body, name=name,
        in_specs=[pl.BlockSpec((n, r, cols), lambda: (0, 0, 0))],
        out_specs=pl.BlockSpec((r, cols), lambda: (0, 0)),
        out_shape=jax.ShapeDtypeStruct((r, cols), F32),
        compiler_params=_params(),
    )(a)


def _adamw_step(w, g, m, v):
    m = ADAM_B1 * m + (1.0 - ADAM_B1) * g
    v = ADAM_B2 * v + (1.0 - ADAM_B2) * (g * g)
    m_hat = m / (1.0 - ADAM_B1 ** ADAM_STEP)
    v_hat = v / (1.0 - ADAM_B2 ** ADAM_STEP)
    return -ADAM_LR * (m_hat / (jnp.sqrt(v_hat) + ADAM_EPS) + ADAM_WD * w), m, v


def _adamw(w, g, m, v, name):
    r, cols = w.shape
    tr = _row_tile(r, 256) if r % SUBLANES_BF16 == 0 else r

    def body(w_ref, g_ref, m_ref, v_ref, d_ref, nm_ref, nv_ref):
        d_ref[...], nm_ref[...], nv_ref[...] = _adamw_step(w_ref[...], g_ref[...], m_ref[...], v_ref[...])

    blk = pl.BlockSpec((tr, cols), lambda i: (i, 0))
    return pl.pallas_call(
        body, name=name, grid=(r // tr,),
        in_specs=[blk] * 4, out_specs=[blk] * 3,
        out_shape=[jax.ShapeDtypeStruct((r, cols), F32)] * 3,
        compiler_params=_params("arbitrary"),
    )(w, g, m, v)


def _adamw_many(ws, gs, ms, vs, name):
    n = len(ws)

    def body(*refs):
        w_refs, g_refs, m_refs, v_refs = (refs[t * n:(t + 1) * n] for t in range(4))
        d_refs, nm_refs, nv_refs = (refs[(4 + t) * n:(5 + t) * n] for t in range(3))
        for k in range(n):
            d_refs[k][...], nm_refs[k][...], nv_refs[k][...] = _adamw_step(
                w_refs[k][...], g_refs[k][...], m_refs[k][...], v_refs[k][...])

    specs = [pl.BlockSpec(a.shape, lambda: (0, 0)) for a in ws]
    out = pl.pallas_call(
        body, name=name, in_specs=specs * 4, out_specs=specs * 3,
        out_shape=[jax.ShapeDtypeStruct(a.shape, F32) for a in ws] * 3,
        compiler_params=_params(),
    )(*ws, *gs, *ms, *vs)
    return out[:n], out[n:2 * n], out[2 * n:]


def _place():
    x, y, c = lax.axis_index("x"), lax.axis_index("y"), lax.axis_index("c")
    other_chips = [(1 - x, y), (x, 1 - y), (1 - x, 1 - y)]
    return x, y, c, other_chips


def _chip(xy):
    return 2 * xy[0] + xy[1]


def _remote(src, dst, send_sem, recv_sem, to):
    return pltpu.make_async_remote_copy(src_ref=src, dst_ref=dst, send_sem=send_sem, recv_sem=recv_sem,
                                        device_id=to, device_id_type=MESH)


def _gather_ici(ctx, k, j, start):
    (x, y, c, chips), b, send, recv = ctx
    if j < 2:
        chip, to = (_chip((x, y)) if start else _chip(chips[j])), (*chips[j], c)
    else:
        chip = 2 * (x ^ c) + (y ^ (1 - c)) if start else _chip(chips[2])
        to = (x ^ (1 - c), y ^ c, c)
    blk = b[k].at[chip, c]
    return _remote(blk, blk, send.at[6 * k + j], recv.at[6 * k + j], to)


def _gather_d2d(ctx, k, j, start):
    (x, y, c, chips), b, send, recv = ctx
    blk = b[k].at[_chip(chips[j]), c if start else 1 - c]
    return _remote(blk, blk, send.at[6 * k + 3 + j], recv.at[6 * k + 3 + j], (x, y, 1 - c))


def _gather_small(ctx, n, j, start):
    (x, y, c, chips), b, send, recv = ctx
    blk = b[n].at[_chip((x, y)) if start else _chip(chips[j])]
    return _remote(blk, blk, send.at[6 * n + j], recv.at[6 * n + j], (*chips[j], c))


def _gather_neighbours_landed(ctx, k):
    for j in range(2):
        _gather_ici(ctx, k, j, False).wait_recv()
    _gather_ici(ctx, k, 2, True).start()
    for j in range(2):
        _gather_d2d(ctx, k, j, True).start()


def _gather_diagonal_landed(ctx, k):
    _gather_ici(ctx, k, 2, False).wait_recv()
    _gather_d2d(ctx, k, 2, True).start()


def _gather_comm(bufs, relay_at, forward_at):
    n = len(bufs)

    def start(srcs, b, send, recv):
        for k in range(n):
            for j in range(2):
                _gather_ici((_place(), b, send, recv), k, j, True).start()

    def relay(srcs, b, send, recv):
        for k in range(n):
            _gather_neighbours_landed((_place(), b, send, recv), k)

    def forward(srcs, b, send, recv):
        for k in range(n):
            _gather_diagonal_landed((_place(), b, send, recv), k)

    def finish(srcs, b, send, recv):
        ctx = (_place(), b, send, recv)
        for k in range(n):
            for j in range(3):
                _gather_d2d(ctx, k, j, False).wait_recv()
                _gather_ici(ctx, k, j, True).wait_send()
                _gather_d2d(ctx, k, j, True).wait_send()

    return _Comm([], bufs, 6 * n, [(0, start), (relay_at, relay), (forward_at, forward)], finish)


def _rmsnorm(x, gain, name):
    s, d = x.shape
    tm = _row_tile(s, 512)

    def body(x_ref, g_ref, h_ref):
        xv = x_ref[...]
        r = lax.rsqrt(_mean_last(xv * xv) + EPS)
        h_ref[...] = (xv * r * g_ref[...]).astype(BF16)

    return pl.pallas_call(
        body, name=name, grid=(s // tm,),
        in_specs=[pl.BlockSpec((tm, d), lambda i: (i, 0)), pl.BlockSpec((1, d), lambda i: (0, 0))],
        out_specs=pl.BlockSpec((tm, d), lambda i: (i, 0)),
        out_shape=jax.ShapeDtypeStruct((s, d), BF16),
        compiler_params=_params("arbitrary"),
    )(x, gain)


def _gathered_in_proj(h, bufs, small, order, name):
    s, d = h.shape
    n_sh, _, r2, ns = bufs[0].shape
    assert d == 2 * r2
    n = len(bufs)
    tm = _row_tile(s, 512)
    n_i = s // tm
    hook_i = max(n_i - 2, 0)
    n_sems = 6 * n + 3

    def body(order_ref, h_ref, *rest):
        p_ref = rest[n + 1]
        b = rest[n + 2:2 * n + 3]
        w_vmem, w_sems, send, recv = rest[2 * n + 3:]
        j, i = pl.program_id(0), pl.program_id(1)
        ctx = (_place(), b, send, recv)

        def fetch(q):
            return pltpu.make_async_copy(b[0].at[order_ref[q]], w_vmem.at[q % 2], w_sems.at[q % 2])

        @pl.when((j == 0) & (i == 0))
        def _():
            for k in range(n):
                for peer in range(2):
                    _gather_ici(ctx, k, peer, True).start()
            for peer in range(3):
                _gather_small(ctx, n, peer, True).start()
            fetch(0).start()
            fetch(0).wait()

        for q in range(1, n_sh):
            @pl.when((j == q - 1) & (i == hook_i))
            def _(q=q):
                if q == 1:
                    _gather_neighbours_landed(ctx, 0)
                if q == 2:
                    for k in range(1, n):
                        _gather_neighbours_landed(ctx, k)
                if q == 3:
                    for k in range(n):
                        _gather_diagonal_landed(ctx, k)
                _gather_d2d(ctx, 0, q - 1, False).wait_recv()
                fetch(q).start()

            @pl.when((j == q) & (i == 0))
            def _(q=q):
                fetch(q).wait()

        wv = w_vmem.at[j % 2]
        p_ref[...] = (jnp.dot(h_ref[:, 0:r2], wv[0], preferred_element_type=F32)
                      + jnp.dot(h_ref[:, r2:d], wv[1], preferred_element_type=F32))

        @pl.when((j == n_sh - 1) & (i == n_i - 1))
        def _():
            for peer in range(3):
                _gather_small(ctx, n, peer, False).wait_recv()
                _gather_small(ctx, n, peer, True).wait_send()
            for k in range(n):
                for peer in range(3):
                    if k > 0:
                        _gather_d2d(ctx, k, peer, False).wait_recv()
                    _gather_ici(ctx, k, peer, True).wait_send()
                    _gather_d2d(ctx, k, peer, True).wait_send()

    all_bufs = list(bufs) + [small]
    out = pl.pallas_call(
        body, name=name,
        grid_spec=pltpu.PrefetchScalarGridSpec(
            num_scalar_prefetch=1, grid=(n_sh, n_i),
            in_specs=[pl.BlockSpec((tm, d), lambda j, i, o: (i, 0))] + [ANY] * (n + 1),
            out_specs=[pl.BlockSpec((tm, ns), lambda j, i, o: (i, o[j]))] + [ANY] * (n + 1),
            scratch_shapes=[pltpu.VMEM((2, 2, r2, ns), BF16), pltpu.SemaphoreType.DMA((2,)),
                            pltpu.SemaphoreType.DMA((n_sems,)), pltpu.SemaphoreType.DMA((n_sems,))]),
        out_shape=[jax.ShapeDtypeStruct((s, n_sh * ns), F32)]
        + [jax.ShapeDtypeStruct(a.shape, a.dtype) for a in all_bufs],
        input_output_aliases={2 + t: 1 + t for t in range(n + 1)},
        compiler_params=_params("arbitrary", "arbitrary"),
    )(order, h, *all_bufs)
    return out[0], list(out[1:])


def _exchange_comm(grads):
    n = len(grads)
    landing = [lax.empty((N_SHARDS,) + a.shape[2:], a.dtype) for a in grads]

    def copies(srcs, b, send, recv):
        x, y, c, _ = _place()
        return [_remote(srcs[k].at[s, 1 - c], b[k].at[s], send.at[N_SHARDS * k + s], recv.at[N_SHARDS * k + s],
                        (x, y, 1 - c)) for k in range(n) for s in range(N_SHARDS)]

    def start(srcs, b, send, recv):
        for cp in copies(srcs, b, send, recv):
            cp.start()

    def finish(srcs, b, send, recv):
        for cp in copies(srcs, b, send, recv):
            cp.wait()

    return _Comm(grads, landing, N_SHARDS * n, [(0, start)], finish)


def _scatter_comm(chip_sums, landing):
    n = len(chip_sums)

    def big(srcs, b, send, recv, k, j, start):
        x, y, c, chips = _place()
        dst = b[k].at[_chip((x, y)) if start else _chip(chips[j])]
        return _remote(srcs[k].at[_chip(chips[j])], dst, send.at[3 * k + j], recv.at[3 * k + j], (*chips[j], c))

    def start(srcs, b, send, recv):
        for k in range(n):
            for j in range(3):
                big(srcs, b, send, recv, k, j, True).start()

    def finish(srcs, b, send, recv):
        for k in range(n):
            for j in range(3):
                big(srcs, b, send, recv, k, j, False).wait_recv()
                big(srcs, b, send, recv, k, j, True).wait_send()

    return _Comm(chip_sums, landing, 3 * n, [(0, start)], finish)


def _join_comm(halves, small):
    n = len(halves)
    flips = [(fx, fy, fc) for fx in (0, 1) for fy in (0, 1) for fc in (0, 1)][1:]

    def half(b, send, recv, k, start):
        x, y, c, _ = _place()
        return _remote(b[k].at[c], b[k].at[c if start else 1 - c], send.at[k], recv.at[k], (x, y, 1 - c))

    def small_copy(b, send, recv, q, start):
        x, y, c, _ = _place()
        px, py, pc = x ^ flips[q][0], y ^ flips[q][1], c ^ flips[q][2]
        blk = b[n].at[4 * x + 2 * y + c if start else 4 * px + 2 * py + pc]
        return _remote(blk, blk, send.at[n + q], recv.at[n + q], (px, py, pc))

    def start(srcs, b, send, recv):
        for q in range(len(flips)):
            small_copy(b, send, recv, q, True).start()
        for k in range(n):
            half(b, send, recv, k, True).start()

    def finish(srcs, b, send, recv):
        for q in range(len(flips)):
            small_copy(b, send, recv, q, False).wait()
        for k in range(n):
            half(b, send, recv, k, False).wait()

    return _Comm([], list(halves) + [small], n + len(flips), [(0, start)], finish)


def _flat_rows(parts):
    flat = jnp.concatenate([p.reshape(-1) for p in parts])
    assert flat.shape[0] % LANES == 0
    return flat.reshape(-1, LANES)


def _unflatten(flat, shapes):
    out, at = [], 0
    for sh in shapes:
        size = 1
        for dim in sh:
            size *= dim
        out.append(flat[at:at + size].reshape(sh))
        at += size
    assert at == flat.shape[0], (at, flat.shape)
    return out


def _col_shards_to_full(a, rows):
    q = a.shape[1] // rows
    return a.reshape(N_SHARDS, rows, q).transpose(1, 0, 2).reshape(rows, N_SHARDS * q)


def _my_col_shard(full, chip):
    rows, cols = full.shape
    q = cols // N_SHARDS
    return lax.dynamic_index_in_dim(full.reshape(rows, N_SHARDS, q), chip, axis=1, keepdims=False)


def kernel(x, e_norm_pre, e_norm_post, e_w_in, e_a_conv, e_b_conv, e_b_conv_bias, e_b_ln_g, e_b_ln_b, e_w_out, o_norm_pre, o_norm_post, o_w_in, o_c_w, o_c_b, o_c_scale, o_w_out, loss_target, m_e_norm_pre, m_e_norm_post, m_e_w_in, m_e_a_conv, m_e_b_conv, m_e_b_conv_bias, m_e_b_ln_g, m_e_b_ln_b, m_e_w_out, m_o_norm_pre, m_o_norm_post, m_o_w_in, m_o_c_w, m_o_c_b, m_o_c_scale, m_o_w_out, v_e_norm_pre, v_e_norm_post, v_e_w_in, v_e_a_conv, v_e_b_conv, v_e_b_conv_bias, v_e_b_ln_g, v_e_b_ln_b, v_e_w_out, v_o_norm_pre, v_o_norm_post, v_o_w_in, v_o_c_w, v_o_c_b, v_o_c_scale, v_o_w_out):
    _, s, d = x.shape
    w = d // 2
    c = d
    gc = c // N_GROUPS
    wq, cq, gq = w // N_SHARDS, c // N_SHARDS, gc // N_SHARDS
    chip = 2 * lax.axis_index("x") + lax.axis_index("y")
    core = lax.axis_index("c")
    x2 = x.reshape(s, d)
    target = loss_target.reshape(s, d)

    big_w = [e_w_in[0], e_w_out[0], o_w_in[0], o_c_w[0].reshape(N_GROUPS * gq, gc), o_w_out[0]]
    big_m = [m_e_w_in[0], m_e_w_out[0], m_o_w_in[0], m_o_c_w[0].reshape(N_GROUPS * gq, gc), m_o_w_out[0]]
    big_v = [v_e_w_in[0], v_e_w_out[0], v_o_w_in[0], v_o_c_w[0].reshape(N_GROUPS * gq, gc), v_o_w_out[0]]
    coords = jnp.stack([chip, core]).astype(jnp.int32)
    slots = [_cast_into_slot(a, coords, "cast_%d" % k) for k, a in enumerate(big_w)]
    sharded_small = _flat_rows([e_a_conv[0], e_b_conv[0], o_norm_pre, o_norm_post, o_c_scale, o_c_b[0]])
    small_slots = lax.dynamic_update_index_in_dim(jnp.zeros((N_SHARDS,) + sharded_small.shape, F32), sharded_small,
                                                  chip, 0)
    xi, yi = lax.axis_index("x"), lax.axis_index("y")
    order = jnp.stack([chip, 2 * (1 - xi) + yi, 2 * xi + (1 - yi), 2 * (1 - xi) + (1 - yi)]).astype(jnp.int32)
    h0 = _rmsnorm(x2, e_norm_pre, "e_pre_norm")
    p0, (e_w_in_g, e_w_out_g, small_g4) = _gathered_in_proj(h0, slots[:2], small_slots, order, "e_in_proj")
    e_w_in_sm = e_w_in_g.reshape((N_SHARDS,) + big_w[0].shape)
    e_w_out_f = e_w_out_g.reshape(w + w, d)
    sm = small_g4.reshape(N_SHARDS, -1)
    at = [0]

    def take(rows, q):
        blk = sm[:, at[0]:at[0] + rows * q]
        at[0] += rows * q
        return _col_shards_to_full(blk, rows)

    a_conv_f = take(CONV_A, wq)
    b_conv_f = take(CONV_B, wq)
    o_pre_f = take(1, cq)
    o_post_f = take(1, cq)
    cs_f = take(1, cq)
    cb_f = take(N_GROUPS, gq).reshape(1, c)

    mixer_steps = s // _row_tile(s, 128)
    (u0, x1, y0), odd_g = _even_mixer_fwd(p0, e_w_out_f, x2, e_norm_post, a_conv_f, b_conv_f, e_b_conv_bias, e_b_ln_g,
                                          e_b_ln_b, "e_mixer_out_proj",
                                          comm=_gather_comm(slots[2:], mixer_steps // 2, (25 * mixer_steps) // 32))
    o_w_in_sm = odd_g[0].reshape((N_SHARDS,) + big_w[2].shape)
    cw_f = odd_g[1].reshape(N_SHARDS, N_GROUPS, gq, gc).transpose(1, 0, 2, 3).reshape(N_GROUPS, gc, gc)
    o_w_out_f = odd_g[2].reshape(c, d)
    p1, h1 = _norm_matmul(x1, o_pre_f, o_w_in_sm, "o_in_proj")
    u1 = _odd_mixer_fwd(p1, cw_f, cb_f, cs_f, "o_mixer_fwd")
    d_y1, d_x2, d_o_post, loss_part = _matmul_post_loss(u1, o_w_out_f, x1, o_post_f, target, "o_out_proj_loss")

    def as_pieces(g, k):
        return g.reshape(N_SHARDS, 2, big_w[k].shape[0] // 2, big_w[k].shape[1])

    def chip_sums(ks, pieces, from_sibling):
        both = [_chip_sum(g, o, coords, "chip_sum_%d" % k) for k, g, o in zip(ks, pieces, from_sibling)]
        return [b[0] for b in both], [b[1] for b in both]

    g_o_w_out = _matmul_tn(u1, d_y1, 1, "o_w_out_grad")
    d_u1, _ = _matmul_nt(d_y1, o_w_out_f[None], "o_out_proj_bwd")
    d_p1, d_cw, d_cb, d_cs = _odd_mixer_bwd(p1, d_u1, cw_f, cb_f, cs_f, "o_mixer_bwd")
    g_o_w_in = _matmul_tn(h1, d_p1, N_SHARDS, "o_w_in_grad")
    g_cw = d_cw.reshape(N_GROUPS, N_SHARDS, gq, gc).transpose(1, 0, 2, 3).astype(BF16)
    pieces_o = [as_pieces(g_o_w_in, 2), as_pieces(g_cw, 3), as_pieces(g_o_w_out, 4)]
    d_h1, sibling_o = _matmul_nt(d_p1, o_w_in_sm, "o_in_proj_bwd", comm=_exchange_comm(pieces_o))
    d_x1, d_o_pre, d_y0, d_e_post = _norm_bwd(d_h1, x1, o_pre_f, d_x2, "o_pre_norm_bwd", post=(y0, e_norm_post))

    pieces_e = [as_pieces(_matmul_tn(u0, d_y0, 1, "e_w_out_grad"), 1)]
    d_u0, sibling_e = _matmul_nt(d_y0, e_w_out_f[None], "e_out_proj_bwd", comm=_exchange_comm(pieces_e))
    sums_a, landing_a = chip_sums([1, 2, 3, 4], pieces_e + pieces_o, sibling_e + sibling_o)
    (d_p0, d_a_conv, d_b_conv, d_bias, d_ln_g, d_ln_b), landed_a = _even_mixer_bwd(
        p0, d_u0, a_conv_f, b_conv_f, e_b_conv_bias, e_b_ln_g, e_b_ln_b, "e_mixer_bwd",
        comm=_scatter_comm(sums_a, landing_a))
    pieces_b = [as_pieces(_matmul_tn(h0, d_p0, N_SHARDS, "e_w_in_grad"), 0)]
    sums_b, landing_b = chip_sums([0], pieces_b, _comm_only(_exchange_comm(pieces_b), "exchange_core_halves"))
    d_h0, landed_b = _matmul_nt(d_p0, e_w_in_sm, "e_in_proj_bwd", comm=_scatter_comm(sums_b, landing_b))
    grad_x, d_e_pre = _norm_bwd(d_h0, x2, e_norm_pre, d_x1, "e_pre_norm_bwd")

    landed = landed_b + landed_a
    reduced = [_shard_sum(sc, coords, "shard_sum_%d" % k) for k, sc in enumerate(landed)]
    small_parts = _flat_rows([loss_part[0], d_e_pre, d_e_post, d_bias, d_ln_g, d_ln_b, d_a_conv, d_b_conv,
                              d_o_pre, d_o_post, d_cs, d_cb])
    small_rows = lax.dynamic_update_index_in_dim(jnp.zeros((N_DEVICES,) + small_parts.shape, F32), small_parts,
                                                 2 * chip + core, 0)
    joined = _comm_only(_join_comm(reduced, small_rows), "join_core_halves")
    big_g = [j.reshape(a.shape) for j, a in zip(joined[:5], big_w)]
    small_sum = _sum_small(joined[5], "small_sum").reshape(-1)
    (loss_row, g_e_pre, g_e_post, g_bias, g_ln_g, g_ln_b, g_a_conv_f, g_b_conv_f, g_o_pre_f, g_o_post_f, g_cs_f,
     g_cb_f) = _unflatten(small_sum, [(LANES,), (1, d), (1, d), (1, w), (1, w), (1, w), (CONV_A, w), (CONV_B, w),
                                      (1, c), (1, c), (1, c), (1, c)])
    loss = loss_row[0]
    g_a_conv = _my_col_shard(g_a_conv_f, chip)
    g_b_conv = _my_col_shard(g_b_conv_f, chip)
    g_o_pre = _my_col_shard(g_o_pre_f, chip)
    g_o_post = _my_col_shard(g_o_post_f, chip)
    g_cs = _my_col_shard(g_cs_f, chip)
    g_cb = _my_col_shard(g_cb_f.reshape(N_GROUPS, gc), chip)

    big_upd = [_adamw(wt, g, m, v, "adamw_%d" % k) for k, (wt, g, m, v) in enumerate(zip(big_w, big_g, big_m, big_v))]
    small_w = [e_norm_pre, e_norm_post, e_b_conv_bias, e_b_ln_g, e_b_ln_b, e_a_conv[0], e_b_conv[0],
               o_norm_pre, o_norm_post, o_c_b[0], o_c_scale]
    small_m = [m_e_norm_pre, m_e_norm_post, m_e_b_conv_bias, m_e_b_ln_g, m_e_b_ln_b, m_e_a_conv[0], m_e_b_conv[0],
               m_o_norm_pre, m_o_norm_post, m_o_c_b[0], m_o_c_scale]
    small_v = [v_e_norm_pre, v_e_norm_post, v_e_b_conv_bias, v_e_b_ln_g, v_e_b_ln_b, v_e_a_conv[0], v_e_b_conv[0],
               v_o_norm_pre, v_o_norm_post, v_o_c_b[0], v_o_c_scale]
    small_g = [g_e_pre, g_e_post, g_bias, g_ln_g, g_ln_b, g_a_conv, g_b_conv, g_o_pre, g_o_post, g_cb, g_cs]
    small_delta, small_new_m, small_new_v = _adamw_many(small_w, small_g, small_m, small_v, "adamw_small")

    def ordered(small, big):
        (n_pre, n_post, bias, ln_g, ln_b, a_conv, b_conv, o_pre, o_post, cb, cs) = small
        (w_in, w_out, ow_in, cw, ow_out) = big
        return [n_pre, n_post, w_in[None], a_conv[None], b_conv[None], bias, ln_g, ln_b, w_out[None], o_pre, o_post,
                ow_in[None], cw.reshape(1, N_GROUPS, gq, gc), cb[None], cs, ow_out[None]]

    grads = ordered(small_g, big_g)
    deltas = ordered(small_delta, [u[0] for u in big_upd])
    new_m = ordered(small_new_m, [u[1] for u in big_upd])
    new_v = ordered(small_new_v, [u[2] for u in big_upd])
    return (loss, grad_x.reshape(1, s, d), *grads, *deltas, *new_m, *new_v)
```

```python
import functools

import jax
import jax.numpy as jnp
from jax import lax
from jax.experimental import pallas as pl
from jax.experimental.pallas import tpu as pltpu

F32 = jnp.float32
BF16 = jnp.bfloat16
MESH = pl.DeviceIdType.MESH

EPS = 1e-6
CONV_A = 3
CONV_B = 31
POOL_WINDOWS = (2, 4, 8, 16)
N_GROUPS = len(POOL_WINDOWS)
N_SHARDS = 4
N_DEVICES = 8
ADAM_LR = 0.001
ADAM_B1 = 0.9
ADAM_B2 = 0.999
ADAM_EPS = 1e-08
ADAM_WD = 0.01
ADAM_STEP = 10

LANES = 128
SUBLANES_BF16 = 16
HALO_A = 8
HALO_B = 32
HALO_P = 16
PAD_P = 8
SHIFTS = 8
ROW_BLOCK = 32
LANE_BLOCK = 256
LANE_PASS = 256
VMEM_LIMIT = 56 * 1024 * 1024
TN_ACC_BYTES = 8 * 1024 * 1024
EVEN_BWD_ROWS = 128


def _row_tile(n, pref):
    t = max(min(n, pref) // SUBLANES_BF16, 1) * SUBLANES_BF16
    while t > SUBLANES_BF16 and (n % t or t % SUBLANES_BF16):
        t -= SUBLANES_BF16
    assert n % t == 0, (n, pref)
    return t


def _col_chunk(n, pref):
    t = (min(n, pref) // LANES) * LANES
    while t > LANES and n % t:
        t -= LANES
    assert t >= LANES and n % t == 0, (n, pref)
    return t


def _params(*sem):
    return pltpu.CompilerParams(dimension_semantics=tuple(sem) if sem else None, vmem_limit_bytes=VMEM_LIMIT)


ANY = pl.BlockSpec(memory_space=pl.ANY)


class _Comm:
    def __init__(self, srcs, bufs, n_sems, phases, finish):
        self.srcs, self.bufs, self.n_sems, self.phases, self.finish = list(srcs), list(bufs), n_sems, phases, finish


def _call(body, *, name, grid, in_specs, out_specs, out_shape, args, scratch_shapes=(), comm=None):
    params = _params(*(("arbitrary",) * len(grid)))
    if comm is None:
        out = pl.pallas_call(body, name=name, grid=grid, in_specs=in_specs, out_specs=out_specs, out_shape=out_shape,
                             scratch_shapes=scratch_shapes, compiler_params=params)(*args)
        return list(out), []
    n_in, n_out, n_scr = len(in_specs), len(out_specs), len(scratch_shapes)
    ns, nb = len(comm.srcs), len(comm.bufs)
    total = 1
    for size in grid:
        total *= size

    def fused(*refs):
        ins, srcs = refs[:n_in], refs[n_in:n_in + ns]
        at = n_in + ns + nb
        outs, bufs = refs[at:at + n_out], refs[at + n_out:at + n_out + nb]
        scratch = refs[at + n_out + nb:at + n_out + nb + n_scr]
        send_sems, recv_sems = refs[-2:]
        step = 0
        for axis, size in enumerate(grid):
            step = step * size + pl.program_id(axis)
        for when, fn in comm.phases:
            pl.when(step == when)(functools.partial(fn, srcs, bufs, send_sems, recv_sems))
        body(*ins, *outs, *scratch)
        pl.when(step == total - 1)(functools.partial(comm.finish, srcs, bufs, send_sems, recv_sems))

    out = pl.pallas_call(
        fused, name=name, grid=grid,
        in_specs=list(in_specs) + [ANY] * (ns + nb), out_specs=list(out_specs) + [ANY] * nb,
        out_shape=list(out_shape) + [jax.ShapeDtypeStruct(b.shape, b.dtype) for b in comm.bufs],
        input_output_aliases={n_in + ns + i: n_out + i for i in range(nb)},
        scratch_shapes=list(scratch_shapes) + [pltpu.SemaphoreType.DMA((comm.n_sems,))] * 2,
        compiler_params=params,
    )(*args, *comm.srcs, *comm.bufs)
    return list(out[:n_out]), list(out[n_out:])


def _comm_only(comm, name):
    ns, nb = len(comm.srcs), len(comm.bufs)

    def body(*refs):
        srcs, bufs = refs[:ns], refs[ns + nb:ns + 2 * nb]
        send_sems, recv_sems = refs[-2:]
        for _, fn in comm.phases:
            fn(srcs, bufs, send_sems, recv_sems)
        comm.finish(srcs, bufs, send_sems, recv_sems)

    return pl.pallas_call(
        body, name=name, in_specs=[ANY] * (ns + nb), out_specs=[ANY] * nb,
        out_shape=[jax.ShapeDtypeStruct(b.shape, b.dtype) for b in comm.bufs],
        input_output_aliases={ns + i: i for i in range(nb)},
        scratch_shapes=[pltpu.SemaphoreType.DMA((comm.n_sems,))] * 2,
    )(*comm.srcs, *comm.bufs)


def _sigmoid(v):
    return jax.nn.sigmoid(v)


def _dsilu(v, s):
    return s * (1.0 + v * (1.0 - s))


def _mean_last(v):
    return jnp.mean(v, axis=-1, keepdims=True)


def _sum_rows(v):
    return jnp.sum(v, axis=0, keepdims=True)


def _norm_matmul(x, gain, w_sm, name):
    s, d = x.shape
    n_sh, _, ns = w_sm.shape
    tm = _row_tile(s, 1024)

    def body(x_ref, g_ref, w_ref, p_ref, h_ref):
        @pl.when(pl.program_id(1) == 0)
        def _():
            xv = x_ref[...]
            r = lax.rsqrt(_mean_last(xv * xv) + EPS)
            h_ref[...] = (xv * r * g_ref[...]).astype(BF16)

        p_ref[...] = jnp.dot(h_ref[...], w_ref[0], preferred_element_type=F32)

    return _call(
        body, name=name, grid=(s // tm, n_sh),
        in_specs=[pl.BlockSpec((tm, d), lambda i, j: (i, 0)),
                  pl.BlockSpec((1, d), lambda i, j: (0, 0)),
                  pl.BlockSpec((1, d, ns), lambda i, j: (j, 0, 0))],
        out_specs=[pl.BlockSpec((tm, ns), lambda i, j: (i, j)),
                   pl.BlockSpec((tm, d), lambda i, j: (i, 0))],
        out_shape=[jax.ShapeDtypeStruct((s, n_sh * ns), F32), jax.ShapeDtypeStruct((s, d), BF16)],
        args=(x, gain, w_sm))[0]


def _matmul_post_loss(u, w, x_res, gain, target, name):
    s, k = u.shape
    d = w.shape[1]
    tm = _row_tile(s, 512)
    half = tm // 2

    def body(u_ref, w_ref, x_ref, g_ref, t_ref, dy_ref, dout_ref, dg_ref, loss_ref):
        @pl.when(pl.program_id(0) == 0)
        def _():
            dg_ref[...] = jnp.zeros_like(dg_ref)
            loss_ref[...] = jnp.zeros_like(loss_ref)

        g = g_ref[...]
        for rows in (slice(0, half), slice(half, tm)):
            y = jnp.dot(u_ref[rows, :], w_ref[...], preferred_element_type=F32)
            r = lax.rsqrt(_mean_last(y * y) + EPS)
            n = y * r
            err = x_ref[rows, :] + n * g - t_ref[rows, :]
            loss_ref[...] += 0.5 * jnp.sum(_mean_last(err * err))
            dout = err * (1.0 / d)
            dout_ref[rows, :] = dout
            dg_ref[...] += _sum_rows(dout * n)
            dn = dout * g
            dy_ref[rows, :] = (r * (dn - n * _mean_last(dn * n))).astype(BF16)

    return pl.pallas_call(
        body, name=name, grid=(s // tm,),
        in_specs=[pl.BlockSpec((tm, k), lambda i: (i, 0)),
                  pl.BlockSpec((k, d), lambda i: (0, 0), pipeline_mode=pl.Buffered(1)),
                  pl.BlockSpec((tm, d), lambda i: (i, 0)),
                  pl.BlockSpec((1, d), lambda i: (0, 0)),
                  pl.BlockSpec((tm, d), lambda i: (i, 0))],
        out_specs=[pl.BlockSpec((tm, d), lambda i: (i, 0)),
                   pl.BlockSpec((tm, d), lambda i: (i, 0)),
                   pl.BlockSpec((1, d), lambda i: (0, 0)),
                   pl.BlockSpec((8, LANES), lambda i: (0, 0))],
        out_shape=[jax.ShapeDtypeStruct((s, d), BF16), jax.ShapeDtypeStruct((s, d), F32),
                   jax.ShapeDtypeStruct((1, d), F32), jax.ShapeDtypeStruct((8, LANES), F32)],
        compiler_params=_params("arbitrary"),
    )(u, w, x_res, gain, target)


def _matmul_nt(a, w_sm, name, comm=None):
    s, ncols = a.shape
    n_sh, r, ns = w_sm.shape
    assert ncols == n_sh * ns
    tm = _row_tile(s, 1024)
    nc = _col_chunk(ns, 1792)
    per = ns // nc
    steps = n_sh * per

    def body(a_ref, w_ref, o_ref):
        part = lax.dot_general(a_ref[...], w_ref[0], (((1,), (1,)), ((), ())), preferred_element_type=F32)

        @pl.when(pl.program_id(1) == 0)
        def _():
            o_ref[...] = part

        @pl.when(pl.program_id(1) > 0)
        def _():
            o_ref[...] += part

    out, bufs = _call(
        body, name=name, grid=(s // tm, steps),
        in_specs=[pl.BlockSpec((tm, nc), lambda i, j: (i, j)),
                  pl.BlockSpec((1, r, nc), lambda i, j: (j // per, 0, j % per))],
        out_specs=[pl.BlockSpec((tm, r), lambda i, j: (i, 0))],
        out_shape=[jax.ShapeDtypeStruct((s, r), F32)],
        args=(a, w_sm), comm=comm)
    return out[0], bufs


def _matmul_tn(a, b, n_sh, name):
    s, k = a.shape
    n = b.shape[1]
    ns = n // n_sh
    tk = _col_chunk(k, TN_ACC_BYTES // (4 * ns))
    ts = _row_tile(s, 2048)
    n_s = s // ts

    def body(a_ref, b_ref, o_ref, acc_ref):
        part = lax.dot_general(a_ref[...], b_ref[...], (((0,), (0,)), ((), ())), preferred_element_type=F32)

        @pl.when(pl.program_id(2) == 0)
        def _():
            acc_ref[...] = part

        @pl.when(pl.program_id(2) > 0)
        def _():
            acc_ref[...] += part

        @pl.when(pl.program_id(2) == n_s - 1)
        def _():
            o_ref[0] = acc_ref[...].astype(BF16)

    return pl.pallas_call(
        body, name=name, grid=(n_sh, k // tk, n_s),
        in_specs=[pl.BlockSpec((ts, tk), lambda j, i, t: (t, i)),
                  pl.BlockSpec((ts, ns), lambda j, i, t: (t, j))],
        out_specs=pl.BlockSpec((1, tk, ns), lambda j, i, t: (j, i, 0)),
        out_shape=jax.ShapeDtypeStruct((n_sh, k, ns), BF16),
        scratch_shapes=[pltpu.VMEM((tk, ns), F32)],
        compiler_params=_params("arbitrary", "arbitrary", "arbitrary"),
    )(a, b)


def _norm_bwd(dh, x, gain, dres, name, post=None):
    s, d = x.shape
    tm = _row_tile(s, 256)
    with_post = post is not None

    def rms_bwd(dout, v, g):
        r = lax.rsqrt(_mean_last(v * v) + EPS)
        n = v * r
        dn = dout * g
        return r * (dn - n * _mean_last(dn * n)), _sum_rows(dout * n)

    def body(*refs):
        if with_post:
            dh_ref, x_ref, g_ref, dres_ref, y_ref, gp_ref, dx_ref, dg_ref, dy_ref, dgp_ref = refs
        else:
            dh_ref, x_ref, g_ref, dres_ref, dx_ref, dg_ref = refs

        @pl.when(pl.program_id(0) == 0)
        def _():
            dg_ref[...] = jnp.zeros_like(dg_ref)
            if with_post:
                dgp_ref[...] = jnp.zeros_like(dgp_ref)

        dv, dg = rms_bwd(dh_ref[...], x_ref[...], g_ref[...])
        dx = dres_ref[...] + dv
        dx_ref[...] = dx
        dg_ref[...] += dg
        if with_post:
            dy, dgp = rms_bwd(dx, y_ref[...], gp_ref[...])
            dy_ref[...] = dy.astype(BF16)
            dgp_ref[...] += dgp

    row = pl.BlockSpec((tm, d), lambda i: (i, 0))
    vec = pl.BlockSpec((1, d), lambda i: (0, 0))
    in_specs = [row, row, vec, row]
    out_specs = [row, vec]
    out_shape = [jax.ShapeDtypeStruct((s, d), F32), jax.ShapeDtypeStruct((1, d), F32)]
    args = [dh, x, gain, dres]
    if with_post:
        in_specs += [row, vec]
        out_specs += [row, vec]
        out_shape += [jax.ShapeDtypeStruct((s, d), BF16), jax.ShapeDtypeStruct((1, d), F32)]
        args += list(post)
    return _call(body, name=name, grid=(s // tm,), in_specs=in_specs, out_specs=out_specs, out_shape=out_shape,
                 args=args)[0]


def _fill_shifted_down(sh, rows):
    for b in range(1, SHIFTS):
        sh[b, SHIFTS:rows, :] = sh[0, SHIFTS - b:rows - b, :]


def _fill_shifted_up(sh, rows):
    for b in range(1, SHIFTS):
        sh[b, 0:rows - SHIFTS, :] = sh[0, b:rows - SHIFTS + b, :]


def _for_blocks(ts, w, fn):
    lb = min(LANE_BLOCK, w)
    for l0 in range(0, w, lb):
        def rows(rb, carry, l0=l0):
            fn(pl.multiple_of(rb * ROW_BLOCK, ROW_BLOCK), slice(l0, l0 + lb))
            return carry

        lax.fori_loop(0, ts // ROW_BLOCK, rows, 0)


TAP_SPAN = SHIFTS * ((CONV_B - 1) // SHIFTS)
WINDOW = ROW_BLOCK + TAP_SPAN


def _taps_of(b):
    return [(a, SHIFTS * a + b) for a in range((CONV_B - 1 - b) // SHIFTS + 1)]


def _conv31(sh, base, step, wt_ref, bias_ref, out_ref, ts, w):
    low = min(0, step * (TAP_SPAN // SHIFTS))

    def block(r0, lanes):
        acc = [jnp.zeros((SHIFTS, lanes.stop - lanes.start), F32) for _ in range(ROW_BLOCK // SHIFTS)]
        for b in range(SHIFTS):
            window = sh[b, pl.ds(pl.multiple_of(r0 + (base + low), SHIFTS), WINDOW), lanes]
            for a, j in (_taps_of(b) if step > 0 else reversed(_taps_of(b))):
                at = step * a - low
                wt = wt_ref[CONV_B - 1 - j, :, lanes]
                acc = [v + wt * window[at + SHIFTS * r:at + SHIFTS * (r + 1), :] for r, v in enumerate(acc)]
        for r, v in enumerate(acc):
            if bias_ref is not None:
                v = v + bias_ref[:, lanes]
            out_ref[pl.ds(pl.multiple_of(r0 + SHIFTS * r, SHIFTS), SHIFTS), lanes] = v

    _for_blocks(ts, w, block)


def _conv31_weight_grad(d_sh, x_sh, wacc, ts, w):
    def block(r0, lanes):
        d = d_sh[0, pl.ds(r0, ROW_BLOCK), lanes]
        for b in range(SHIFTS):
            window = x_sh[b, pl.ds(pl.multiple_of(r0 + (HALO_B - TAP_SPAN), SHIFTS), WINDOW), lanes]
            for a, j in _taps_of(b):
                at = TAP_SPAN - SHIFTS * a
                prod = d * window[at:at + ROW_BLOCK, :]
                part = prod[0:SHIFTS, :]
                for q in range(1, ROW_BLOCK // SHIFTS):
                    part = part + prod[q * SHIFTS:(q + 1) * SHIFTS, :]
                wacc[CONV_B - 1 - j, :, lanes] += part

    _for_blocks(ts, w, block)


def _even_forward_tile(p_ref, halo_ref, first, a_conv_ref, b_conv_ref, bias_ref, lng_ref, lnb_ref, qbuf, ysh, y1buf,
                       wb, w, ts):
    @pl.when(pl.program_id(0) == 0)
    def _():
        for k in range(CONV_B):
            wb[k] = jnp.broadcast_to(b_conv_ref[k:k + 1, :], (SHIFTS, w))

    def col(ref, k, rows=slice(None)):
        return ref[rows, k * w:(k + 1) * w]

    a_x, a_b, a_c, a_z = col(p_ref, 0), col(p_ref, 1), col(p_ref, 2), col(p_ref, 3)
    b_val, b_gate, b_z = col(p_ref, 4), col(p_ref, 5), col(p_ref, 6)
    keep = jnp.where(first, 0.0, 1.0)

    rows_a = slice(HALO_B - HALO_A, HALO_B)
    qbuf[0:HALO_A, :] = col(halo_ref, 2, rows_a) * col(halo_ref, 0, rows_a) * keep
    qbuf[HALO_A:HALO_A + ts, :] = a_c * a_x
    cq = jnp.zeros((ts, w), F32)
    for j in range(CONV_A):
        cq = cq + a_conv_ref[CONV_A - 1 - j:CONV_A - j, :] * qbuf[HALO_A - j:HALO_A - j + ts, :]
    ya = a_b * cq

    ysh[0, 0:HALO_B, :] = col(halo_ref, 4) * _sigmoid(col(halo_ref, 5)) * keep
    ysh[0, HALO_B:HALO_B + ts, :] = b_val * _sigmoid(b_gate)
    _fill_shifted_down(ysh, HALO_B + ts)
    _conv31(ysh, HALO_B, -SHIFTS, wb, bias_ref, y1buf, ts, w)
    yb1 = y1buf[...]
    xc = yb1 - _mean_last(yb1)
    rstd = lax.rsqrt(_mean_last(xc * xc) + EPS)
    xhat = xc * rstd
    yb2 = xhat * lng_ref[...] + lnb_ref[...]
    return dict(a_x=a_x, a_b=a_b, a_c=a_c, a_z=a_z, b_val=b_val, b_gate=b_gate, b_z=b_z,
                cq=cq, ya=ya, rstd=rstd, xhat=xhat, yb2=yb2)


def _even_specs(s, w, ts):
    tile = pl.BlockSpec((ts, 7 * w), lambda i: (i, 0))
    halo = pl.BlockSpec((HALO_B, 7 * w), lambda i: (jnp.maximum(i * (ts // HALO_B) - 1, 0), 0))
    return tile, halo


def _small_specs(shapes, index=lambda i: (0, 0)):
    return [pl.BlockSpec(sh, index) for sh in shapes]


def _even_mixer_fwd(p, w_out, x_res, gain, a_conv, b_conv, bias, ln_g, ln_b, name, comm=None):
    s, d = x_res.shape
    w = p.shape[1] // 7
    ts = _row_tile(s, 128)
    assert ts % HALO_B == 0

    def body(p_ref, halo_ref, wout_ref, x_ref, g_ref, ac_ref, bc_ref, bias_ref, lng_ref, lnb_ref,
             u_ref, xn_ref, y_ref, conv_ref, qbuf, ysh, wb):
        first = pl.program_id(0) == 0
        f = _even_forward_tile(p_ref, halo_ref, first, ac_ref, bc_ref, bias_ref, lng_ref, lnb_ref, qbuf, ysh, conv_ref,
                               wb, w, ts)
        yb3 = f["yb2"] * _sigmoid(f["yb2"])
        u_a = (f["ya"] * (f["a_z"] * _sigmoid(f["a_z"]))).astype(BF16)
        u_b = (yb3 * (f["b_z"] * _sigmoid(f["b_z"]))).astype(BF16)
        u_ref[:, 0:w] = u_a
        u_ref[:, w:2 * w] = u_b
        y = (jnp.dot(u_a, wout_ref[0:w, :], preferred_element_type=F32)
             + jnp.dot(u_b, wout_ref[w:2 * w, :], preferred_element_type=F32))
        r = lax.rsqrt(_mean_last(y * y) + EPS)
        y_ref[...] = y
        xn_ref[...] = x_ref[...] + (y * r) * g_ref[...]

    tile, halo = _even_specs(s, w, ts)
    row = pl.BlockSpec((ts, d), lambda i: (i, 0))
    return _call(
        body, name=name, grid=(s // ts,),
        in_specs=[tile, halo, pl.BlockSpec((2 * w, d), lambda i: (0, 0)), row, pl.BlockSpec((1, d), lambda i: (0, 0))]
        + _small_specs([(CONV_A, w), (CONV_B, w), (1, w), (1, w), (1, w)]),
        out_specs=[pl.BlockSpec((ts, 2 * w), lambda i: (i, 0)), row, row, pl.BlockSpec((ts, w), lambda i: (i, 0))],
        out_shape=[jax.ShapeDtypeStruct((s, 2 * w), BF16), jax.ShapeDtypeStruct((s, d), F32),
                   jax.ShapeDtypeStruct((s, d), F32), jax.ShapeDtypeStruct((s, w), F32)],
        scratch_shapes=[pltpu.VMEM((HALO_A + ts, w), F32), pltpu.VMEM((SHIFTS, HALO_B + ts, w), F32),
                        pltpu.VMEM((CONV_B, SHIFTS, w), F32)],
        args=(p, p, w_out, x_res, gain, a_conv, b_conv, bias, ln_g, ln_b), comm=comm)


def _even_mixer_bwd(p, du, conv, a_conv, b_conv, ln_g, ln_b, name, comm=None):
    s = p.shape[0]
    w = p.shape[1] // 7
    ts = _row_tile(s, EVEN_BWD_ROWS)
    nt = s // ts
    assert ts % HALO_B == 0

    def body(p_ref, halo_ref, du_ref, conv_ref, ac_ref, bc_ref, lng_ref, lnb_ref,
             dp_ref, dac_ref, dbc_ref, dbias_ref, dlng_ref, dlnb_ref,
             qbuf, ysh, y1buf, dqbuf, dsh, dy0buf, wacc, carry_dq, carry_dy, wb):
        step = pl.program_id(0)
        first = step == nt - 1

        @pl.when(step == 0)
        def _():
            for ref in (dac_ref, dbias_ref, dlng_ref, dlnb_ref, wacc, carry_dq, carry_dy):
                ref[...] = jnp.zeros_like(ref)

        @pl.when(step == 0)
        def _():
            for k in range(CONV_B):
                wb[k] = jnp.broadcast_to(bc_ref[k:k + 1, :], (SHIFTS, w))

        keep = jnp.where(first, 0.0, 1.0)
        lane_blocks = [slice(l0, l0 + min(LANE_PASS, w)) for l0 in range(0, w, min(LANE_PASS, w))]

        def col(ref, k, lanes, rows=slice(None)):
            return ref[rows, k * w + lanes.start:k * w + lanes.stop]


        rows_a = slice(HALO_B - HALO_A, HALO_B)
        for lanes in lane_blocks:
            a_x, a_b, a_c, a_z = (col(p_ref, k, lanes) for k in range(4))
            qbuf[0:HALO_A, lanes] = col(halo_ref, 2, lanes, rows_a) * col(halo_ref, 0, lanes, rows_a) * keep
            qbuf[HALO_A:HALO_A + ts, lanes] = a_c * a_x
            cq = jnp.zeros((ts, lanes.stop - lanes.start), F32)
            for j in range(CONV_A):
                cq = cq + ac_ref[CONV_A - 1 - j:CONV_A - j, lanes] * qbuf[HALO_A - j:HALO_A - j + ts, lanes]
            s_az = _sigmoid(a_z)
            du_a = du_ref[:, lanes]
            d_ya = du_a * (a_z * s_az)
            dp_ref[:, 3 * w + lanes.start:3 * w + lanes.stop] = (du_a * (a_b * cq) * _dsilu(a_z, s_az)).astype(BF16)
            dp_ref[:, 1 * w + lanes.start:1 * w + lanes.stop] = (d_ya * cq).astype(BF16)
            d_cq = d_ya * a_b
            dqbuf[0:ts, lanes] = d_cq
            dqbuf[ts:ts + HALO_A, lanes] = carry_dq[:, lanes]
            carry_dq[:, lanes] = d_cq[0:HALO_A, :]
            d_q = jnp.zeros_like(cq)
            for o in range(CONV_A):
                d_q = d_q + ac_ref[CONV_A - 1 - o:CONV_A - o, lanes] * dqbuf[o:o + ts, lanes]
            for j in range(CONV_A):
                k = CONV_A - 1 - j
                dac_ref[k:k + 1, lanes] += _sum_rows(d_cq * qbuf[HALO_A - j:HALO_A - j + ts, lanes])
            dp_ref[:, 2 * w + lanes.start:2 * w + lanes.stop] = (d_q * a_x).astype(BF16)
            dp_ref[:, 0 * w + lanes.start:0 * w + lanes.stop] = (d_q * a_c).astype(BF16)
            ysh[0, 0:HALO_B, lanes] = col(halo_ref, 4, lanes) * _sigmoid(col(halo_ref, 5, lanes)) * keep
            ysh[0, HALO_B:HALO_B + ts, lanes] = col(p_ref, 4, lanes) * _sigmoid(col(p_ref, 5, lanes))
        _fill_shifted_down(ysh, HALO_B + ts)

        total = jnp.zeros((ts, 1), F32)
        for lanes in lane_blocks:
            total = total + jnp.sum(conv_ref[:, lanes], axis=-1, keepdims=True)
        mu = total * (1.0 / w)
        total = jnp.zeros((ts, 1), F32)
        for lanes in lane_blocks:
            xc = conv_ref[:, lanes] - mu
            total = total + jnp.sum(xc * xc, axis=-1, keepdims=True)
        rstd = lax.rsqrt(total * (1.0 / w) + EPS)

        sum_dxh = jnp.zeros((ts, 1), F32)
        sum_dxh_xhat = jnp.zeros((ts, 1), F32)
        for lanes in lane_blocks:
            xhat = (conv_ref[:, lanes] - mu) * rstd
            yb2 = xhat * lng_ref[:, lanes] + lnb_ref[:, lanes]
            b_z = col(p_ref, 6, lanes)
            s_bz, s_y2 = _sigmoid(b_z), _sigmoid(yb2)
            du_b = du_ref[:, w + lanes.start:w + lanes.stop]
            dp_ref[:, 6 * w + lanes.start:6 * w + lanes.stop] = (du_b * (yb2 * s_y2) * _dsilu(b_z, s_bz)).astype(BF16)
            d_yb2 = du_b * (b_z * s_bz) * _dsilu(yb2, s_y2)
            dlng_ref[:, lanes] += _sum_rows(d_yb2 * xhat)
            dlnb_ref[:, lanes] += _sum_rows(d_yb2)
            d_xh = d_yb2 * lng_ref[:, lanes]
            sum_dxh = sum_dxh + jnp.sum(d_xh, axis=-1, keepdims=True)
            sum_dxh_xhat = sum_dxh_xhat + jnp.sum(d_xh * xhat, axis=-1, keepdims=True)
            y1buf[:, lanes] = xhat
            dy0buf[:, lanes] = d_xh
        mean_dxh = sum_dxh * (1.0 / w)
        mean_dxh_xhat = sum_dxh_xhat * (1.0 / w)

        for lanes in lane_blocks:
            d_yb1 = rstd * (dy0buf[:, lanes] - mean_dxh - y1buf[:, lanes] * mean_dxh_xhat)
            dbias_ref[:, lanes] += _sum_rows(d_yb1)
            dsh[0, 0:ts, lanes] = d_yb1
            dsh[0, ts:ts + HALO_B, lanes] = carry_dy[:, lanes]
            carry_dy[:, lanes] = d_yb1[0:HALO_B, :]
        _fill_shifted_up(dsh, ts + HALO_B)
        _conv31(dsh, 0, SHIFTS, wb, None, dy0buf, ts, w)
        _conv31_weight_grad(dsh, ysh, wacc, ts, w)

        @pl.when(step == nt - 1)
        def _():
            for k in range(CONV_B):
                dbc_ref[k:k + 1, :] = _sum_rows(wacc[k])

        for lanes in lane_blocks:
            d_yb0 = dy0buf[:, lanes]
            s_g = _sigmoid(col(p_ref, 5, lanes))
            dp_ref[:, 4 * w + lanes.start:4 * w + lanes.stop] = (d_yb0 * s_g).astype(BF16)
            dp_ref[:, 5 * w + lanes.start:5 * w + lanes.stop] = (
                d_yb0 * col(p_ref, 4, lanes) * s_g * (1.0 - s_g)).astype(BF16)

    rev = lambda i: (nt - 1 - i, 0)
    tile = pl.BlockSpec((ts, 7 * w), rev)
    halo = pl.BlockSpec((HALO_B, 7 * w), lambda i: (jnp.maximum((nt - 1 - i) * (ts // HALO_B) - 1, 0), 0))
    small = [(CONV_A, w), (CONV_B, w), (1, w), (1, w), (1, w)]
    return _call(
        body, name=name, grid=(nt,),
        in_specs=[tile, halo, pl.BlockSpec((ts, 2 * w), rev), pl.BlockSpec((ts, w), rev)]
        + _small_specs([(CONV_A, w), (CONV_B, w), (1, w), (1, w)]),
        out_specs=[pl.BlockSpec((ts, 7 * w), rev)] + _small_specs(small),
        out_shape=[jax.ShapeDtypeStruct((s, 7 * w), BF16)] + [jax.ShapeDtypeStruct(sh, F32) for sh in small],
        scratch_shapes=[pltpu.VMEM((HALO_A + ts, w), F32), pltpu.VMEM((SHIFTS, HALO_B + ts, w), F32),
                        pltpu.VMEM((ts, w), F32),
                        pltpu.VMEM((ts + HALO_A, w), F32), pltpu.VMEM((SHIFTS, ts + HALO_B, w), F32),
                        pltpu.VMEM((ts, w), F32), pltpu.VMEM((CONV_B, SHIFTS, w), F32),
                        pltpu.VMEM((HALO_A, w), F32), pltpu.VMEM((HALO_B, w), F32),
                        pltpu.VMEM((CONV_B, SHIFTS, w), F32)],
        args=(p, p, du, conv, a_conv, b_conv, ln_g, ln_b), comm=comm)


def _trailing_sums(buf_a, buf_b, cols, win, rows, first_out):
    src, dst, shift = buf_a, buf_b, 1
    while True:
        last = 2 * shift >= win
        lo = first_out if last else 0
        val = src[PAD_P + lo:PAD_P + rows, cols] + src[PAD_P + lo - shift:PAD_P + rows - shift, cols]
        if last:
            return val
        dst[PAD_P + lo:PAD_P + rows, cols] = val
        src, dst, shift = dst, src, 2 * shift


def _leading_sums(buf_a, buf_b, cols, win, rows, n_out):
    src, dst, shift = buf_a, buf_b, 1
    while True:
        last = 2 * shift >= win
        hi = n_out if last else rows
        val = src[0:hi, cols] + src[shift:hi + shift, cols]
        if last:
            return val
        dst[0:hi, cols] = val
        src, dst, shift = dst, src, 2 * shift


def _pool_forward_tile(p_ref, halo_ref, first, tile_index, cw_ref, cb_ref, cs_ref, vbuf, vtmp, c, gc, ts):
    vbuf[PAD_P:PAD_P + HALO_P, :] = halo_ref[...] * jnp.where(first, 0.0, 1.0)
    vbuf[PAD_P + HALO_P:PAD_P + HALO_P + ts, :] = p_ref[:, 0:c]
    pos = tile_index * ts + lax.broadcasted_iota(jnp.int32, (ts, 1), 0) + 1
    pooled, inv, gout = [], [], []
    for g, win in enumerate(POOL_WINDOWS):
        cols = slice(g * gc, (g + 1) * gc)
        acc = _trailing_sums(vbuf, vtmp, cols, win, HALO_P + ts, HALO_P)
        inv_g = 1.0 / jnp.minimum(pos, win).astype(F32)
        pooled_g = (acc * inv_g - p_ref[:, cols]).astype(BF16)
        pooled.append(pooled_g)
        inv.append(inv_g)
        gout.append(jnp.dot(pooled_g, cw_ref[g], preferred_element_type=F32) + cb_ref[:, cols])
    return pooled, inv, gout


def _odd_mixer_fwd(p, cw, cb, cs, name):
    s = p.shape[0]
    c = p.shape[1] // 2
    gc = c // N_GROUPS
    ts = _row_tile(s, 256)

    def body(p_ref, halo_ref, cw_ref, cb_ref, cs_ref, u_ref, vbuf, vtmp):
        i = pl.program_id(0)

        @pl.when(i == 0)
        def _():
            vbuf[0:PAD_P, :] = jnp.zeros((PAD_P, c), F32)
            vtmp[0:PAD_P, :] = jnp.zeros((PAD_P, c), F32)

        _, _, gout = _pool_forward_tile(p_ref, halo_ref, i == 0, i, cw_ref, cb_ref, cs_ref, vbuf, vtmp, c, gc, ts)
        for g in range(N_GROUPS):
            cols = slice(g * gc, (g + 1) * gc)
            z = p_ref[:, c + g * gc:c + (g + 1) * gc]
            u_ref[:, cols] = (gout[g] * cs_ref[:, cols] * (z * _sigmoid(z))).astype(BF16)

    return pl.pallas_call(
        body, name=name, grid=(s // ts,),
        in_specs=[pl.BlockSpec((ts, 2 * c), lambda i: (i, 0)),
                  pl.BlockSpec((HALO_P, c), lambda i: (jnp.maximum(i * (ts // HALO_P) - 1, 0), 0)),
                  pl.BlockSpec((N_GROUPS, gc, gc), lambda i: (0, 0, 0)),
                  pl.BlockSpec((1, c), lambda i: (0, 0)), pl.BlockSpec((1, c), lambda i: (0, 0))],
        out_specs=pl.BlockSpec((ts, c), lambda i: (i, 0)),
        out_shape=jax.ShapeDtypeStruct((s, c), BF16),
        scratch_shapes=[pltpu.VMEM((PAD_P + HALO_P + ts, c), F32)] * 2,
        compiler_params=_params("arbitrary"),
    )(p, p, cw, cb, cs)


def _odd_mixer_bwd(p, du, cw, cb, cs, name):
    s = p.shape[0]
    c = p.shape[1] // 2
    gc = c // N_GROUPS
    ts = _row_tile(s, 256)
    nt = s // ts

    def body(p_ref, halo_ref, du_ref, cw_ref, cb_ref, cs_ref, dp_ref, dcw_ref, dcb_ref, dcs_ref,
             vbuf, vtmp, ebuf, etmp, carry_e):
        step = pl.program_id(0)
        tile_index = nt - 1 - step

        @pl.when(step == 0)
        def _():
            for ref in (dcw_ref, dcb_ref, dcs_ref, carry_e):
                ref[...] = jnp.zeros_like(ref)
            for ref in (vbuf, vtmp):
                ref[0:PAD_P, :] = jnp.zeros((PAD_P, c), F32)
            for ref in (ebuf, etmp):
                ref[ts + HALO_P:ts + HALO_P + PAD_P, :] = jnp.zeros((PAD_P, c), F32)

        pooled, inv, gout = _pool_forward_tile(p_ref, halo_ref, tile_index == 0, tile_index, cw_ref, cb_ref, cs_ref,
                                               vbuf, vtmp, c, gc, ts)
        ebuf[ts:ts + HALO_P, :] = carry_e[...]
        for g, win in enumerate(POOL_WINDOWS):
            cols = slice(g * gc, (g + 1) * gc)
            z = p_ref[:, c + g * gc:c + (g + 1) * gc]
            sz = _sigmoid(z)
            du_g = du_ref[:, cols]
            scale = cs_ref[:, cols]
            d_y = du_g * (z * sz)
            dp_ref[:, c + g * gc:c + (g + 1) * gc] = (du_g * (gout[g] * scale) * _dsilu(z, sz)).astype(BF16)
            dcs_ref[:, cols] += _sum_rows(d_y * gout[g])
            d_gout = d_y * scale
            dcb_ref[:, cols] += _sum_rows(d_gout)
            d_gout_b = d_gout.astype(BF16)
            dcw_ref[g] += lax.dot_general(pooled[g], d_gout_b, (((0,), (0,)), ((), ())), preferred_element_type=F32)
            d_pool = lax.dot_general(d_gout_b, cw_ref[g], (((1,), (1,)), ((), ())), preferred_element_type=F32)
            e = d_pool * inv[g]
            ebuf[0:ts, cols] = e
            carry_e[:, cols] = e[0:HALO_P, :]
            d_v = _leading_sums(ebuf, etmp, cols, win, ts + HALO_P, ts) - d_pool
            dp_ref[:, cols] = d_v.astype(BF16)

    rev = lambda i: (nt - 1 - i, 0)
    small = [(N_GROUPS, gc, gc), (1, c), (1, c)]
    return pl.pallas_call(
        body, name=name, grid=(nt,),
        in_specs=[pl.BlockSpec((ts, 2 * c), rev),
                  pl.BlockSpec((HALO_P, c), lambda i: (jnp.maximum((nt - 1 - i) * (ts // HALO_P) - 1, 0), 0)),
                  pl.BlockSpec((ts, c), rev),
                  pl.BlockSpec((N_GROUPS, gc, gc), lambda i: (0, 0, 0)),
                  pl.BlockSpec((1, c), lambda i: (0, 0)), pl.BlockSpec((1, c), lambda i: (0, 0))],
        out_specs=[pl.BlockSpec((ts, 2 * c), rev),
                   pl.BlockSpec((N_GROUPS, gc, gc), lambda i: (0, 0, 0)),
                   pl.BlockSpec((1, c), lambda i: (0, 0)), pl.BlockSpec((1, c), lambda i: (0, 0))],
        out_shape=[jax.ShapeDtypeStruct((s, 2 * c), BF16)] + [jax.ShapeDtypeStruct(sh, F32) for sh in small],
        scratch_shapes=[pltpu.VMEM((PAD_P + HALO_P + ts, c), F32)] * 2 + [pltpu.VMEM((ts + HALO_P + PAD_P, c), F32)] * 2
        + [pltpu.VMEM((HALO_P, c), F32)],
        compiler_params=_params("arbitrary"),
    )(p, p, du, cw, cb, cs)


def _cast_into_slot(a, coords, name):
    r, cols = a.shape
    tr = _row_tile(r // 2, 1024)
    per = r // 2 // tr

    def body(co_ref, a_ref, o_ref):
        o_ref[0, 0] = a_ref[...].astype(BF16)

    return pl.pallas_call(
        body, name=name,
        grid_spec=pltpu.PrefetchScalarGridSpec(
            num_scalar_prefetch=1, grid=(2, per),
            in_specs=[pl.BlockSpec((tr, cols), lambda h, i, co: (h * per + i, 0))],
            out_specs=pl.BlockSpec((1, 1, tr, cols), lambda h, i, co: (co[0], h, i, 0))),
        out_shape=jax.ShapeDtypeStruct((N_SHARDS, 2, r // 2, cols), BF16),
        compiler_params=_params("arbitrary", "arbitrary"),
    )(coords, a)


def _chip_sum(g, other, coords, name):
    n_sh, _, r2, cols = g.shape
    tr = _row_tile(r2, 512)

    def body(co_ref, g_ref, o_ref, sum_ref, mine_ref):
        v = (g_ref[0, 0].astype(F32) + o_ref[0].astype(F32)).astype(BF16)
        sum_ref[0] = v

        @pl.when(pl.program_id(1) == co_ref[0])
        def _():
            mine_ref[0] = v

    piece = pl.BlockSpec((1, tr, cols), lambda i, s, co: (s, i, 0))
    return pl.pallas_call(
        body, name=name,
        grid_spec=pltpu.PrefetchScalarGridSpec(
            num_scalar_prefetch=1, grid=(r2 // tr, n_sh),
            in_specs=[pl.BlockSpec((1, 1, tr, cols), lambda i, s, co: (s, co[1], i, 0)), piece],
            out_specs=[piece, pl.BlockSpec((1, tr, cols), lambda i, s, co: (co[0], i, 0))]),
        out_shape=[jax.ShapeDtypeStruct((n_sh, r2, cols), BF16)] * 2,
        compiler_params=_params("arbitrary", "arbitrary"),
    )(coords, g, other)


def _shard_sum(pieces, coords, name):
    n_sh, r2, cols = pieces.shape
    tr = _row_tile(r2, 512)

    def body(co_ref, p_ref, o_ref):
        acc = p_ref[0].astype(F32)
        for k in range(1, n_sh):
            acc = acc + p_ref[k].astype(F32)
        o_ref[0] = acc

    return pl.pallas_call(
        body, name=name,
        grid_spec=pltpu.PrefetchScalarGridSpec(
            num_scalar_prefetch=1, grid=(r2 // tr,),
            in_specs=[pl.BlockSpec((n_sh, tr, cols), lambda i, co: (0, i, 0))],
            out_specs=pl.BlockSpec((1, tr, cols), lambda i, co: (co[1], i, 0))),
        out_shape=jax.ShapeDtypeStruct((2, r2, cols), F32),
        compiler_params=_params("arbitrary"),
    )(coords, pieces)


def _sum_small(a, name):
    n, r, cols = a.shape

    def body(a_ref, o_ref):
        acc = a_ref[0]
        for k in range(1, n):
            acc = acc + a_ref[k]
        o_ref[...] = acc

    return pl.pallas_call(
        body, name=name,
        in_specs=[pl.BlockSpec((n, r, cols), lambda: (0, 0, 0))],
        out_specs=pl.BlockSpec((r, cols), lambda: (0, 0)),
        out_shape=jax.ShapeDtypeStruct((r, cols), F32),
        compiler_params=_params(),
    )(a)


def _adamw_step(w, g, m, v):
    m = ADAM_B1 * m + (1.0 - ADAM_B1) * g
    v = ADAM_B2 * v + (1.0 - ADAM_B2) * (g * g)
    m_hat = m / (1.0 - ADAM_B1 ** ADAM_STEP)
    v_hat = v / (1.0 - ADAM_B2 ** ADAM_STEP)
    return -ADAM_LR * (m_hat / (jnp.sqrt(v_hat) + ADAM_EPS) + ADAM_WD * w), m, v


def _adamw(w, g, m, v, name):
    r, cols = w.shape
    tr = _row_tile(r, 256) if r % SUBLANES_BF16 == 0 else r

    def body(w_ref, g_ref, m_ref, v_ref, d_ref, nm_ref, nv_ref):
        d_ref[...], nm_ref[...], nv_ref[...] = _adamw_step(w_ref[...], g_ref[...], m_ref[...], v_ref[...])

    blk = pl.BlockSpec((tr, cols), lambda i: (i, 0))
    return pl.pallas_call(
        body, name=name, grid=(r // tr,),
        in_specs=[blk] * 4, out_specs=[blk] * 3,
        out_shape=[jax.ShapeDtypeStruct((r, cols), F32)] * 3,
        compiler_params=_params("arbitrary"),
    )(w, g, m, v)


def _adamw_many(ws, gs, ms, vs, name):
    n = len(ws)

    def body(*refs):
        w_refs, g_refs, m_refs, v_refs = (refs[t * n:(t + 1) * n] for t in range(4))
        d_refs, nm_refs, nv_refs = (refs[(4 + t) * n:(5 + t) * n] for t in range(3))
        for k in range(n):
            d_refs[k][...], nm_refs[k][...], nv_refs[k][...] = _adamw_step(
                w_refs[k][...], g_refs[k][...], m_refs[k][...], v_refs[k][...])

    specs = [pl.BlockSpec(a.shape, lambda: (0, 0)) for a in ws]
    out = pl.pallas_call(
        body, name=name, in_specs=specs * 4, out_specs=specs * 3,
        out_shape=[jax.ShapeDtypeStruct(a.shape, F32) for a in ws] * 3,
        compiler_params=_params(),
    )(*ws, *gs, *ms, *vs)
    return out[:n], out[n:2 * n], out[2 * n:]


def _place():
    x, y, c = lax.axis_index("x"), lax.axis_index("y"), lax.axis_index("c")
    other_chips = [(1 - x, y), (x, 1 - y), (1 - x, 1 - y)]
    return x, y, c, other_chips


def _chip(xy):
    return 2 * xy[0] + xy[1]


def _remote(src, dst, send_sem, recv_sem, to):
    return pltpu.make_async_remote_copy(src_ref=src, dst_ref=dst, send_sem=send_sem, recv_sem=recv_sem,
                                        device_id=to, device_id_type=MESH)


def _gather_ici(ctx, k, j, start):
    (x, y, c, chips), b, send, recv = ctx
    if j < 2:
        chip, to = (_chip((x, y)) if start else _chip(chips[j])), (*chips[j], c)
    else:
        chip = 2 * (x ^ c) + (y ^ (1 - c)) if start else _chip(chips[2])
        to = (x ^ (1 - c), y ^ c, c)
    blk = b[k].at[chip, c]
    return _remote(blk, blk, send.at[6 * k + j], recv.at[6 * k + j], to)


def _gather_d2d(ctx, k, j, start):
    (x, y, c, chips), b, send, recv = ctx
    blk = b[k].at[_chip(chips[j]), c if start else 1 - c]
    return _remote(blk, blk, send.at[6 * k + 3 + j], recv.at[6 * k + 3 + j], (x, y, 1 - c))


def _gather_small(ctx, n, j, start):
    (x, y, c, chips), b, send, recv = ctx
    blk = b[n].at[_chip((x, y)) if start else _chip(chips[j])]
    return _remote(blk, blk, send.at[6 * n + j], recv.at[6 * n + j], (*chips[j], c))


def _gather_neighbours_landed(ctx, k):
    for j in range(2):
        _gather_ici(ctx, k, j, False).wait_recv()
    _gather_ici(ctx, k, 2, True).start()
    for j in range(2):
        _gather_d2d(ctx, k, j, True).start()


def _gather_diagonal_landed(ctx, k):
    _gather_ici(ctx, k, 2, False).wait_recv()
    _gather_d2d(ctx, k, 2, True).start()


def _gather_comm(bufs, relay_at, forward_at):
    n = len(bufs)

    def start(srcs, b, send, recv):
        for k in range(n):
            for j in range(2):
                _gather_ici((_place(), b, send, recv), k, j, True).start()

    def relay(srcs, b, send, recv):
        for k in range(n):
            _gather_neighbours_landed((_place(), b, send, recv), k)

    def forward(srcs, b, send, recv):
        for k in range(n):
            _gather_diagonal_landed((_place(), b, send, recv), k)

    def finish(srcs, b, send, recv):
        ctx = (_place(), b, send, recv)
        for k in range(n):
            for j in range(3):
                _gather_d2d(ctx, k, j, False).wait_recv()
                _gather_ici(ctx, k, j, True).wait_send()
                _gather_d2d(ctx, k, j, True).wait_send()

    return _Comm([], bufs, 6 * n, [(0, start), (relay_at, relay), (forward_at, forward)], finish)


def _rmsnorm(x, gain, name):
    s, d = x.shape
    tm = _row_tile(s, 512)

    def body(x_ref, g_ref, h_ref):
        xv = x_ref[...]
        r = lax.rsqrt(_mean_last(xv * xv) + EPS)
        h_ref[...] = (xv * r * g_ref[...]).astype(BF16)

    return pl.pallas_call(
        body, name=name, grid=(s // tm,),
        in_specs=[pl.BlockSpec((tm, d), lambda i: (i, 0)), pl.BlockSpec((1, d), lambda i: (0, 0))],
        out_specs=pl.BlockSpec((tm, d), lambda i: (i, 0)),
        out_shape=jax.ShapeDtypeStruct((s, d), BF16),
        compiler_params=_params("arbitrary"),
    )(x, gain)


def _gathered_in_proj(h, bufs, small, order, name):
    s, d = h.shape
    n_sh, _, r2, ns = bufs[0].shape
    assert d == 2 * r2
    n = len(bufs)
    tm = _row_tile(s, 512)
    n_i = s // tm
    hook_i = max(n_i - 2, 0)
    n_sems = 6 * n + 3

    def body(order_ref, h_ref, *rest):
        p_ref = rest[n + 1]
        b = rest[n + 2:2 * n + 3]
        w_vmem, w_sems, send, recv = rest[2 * n + 3:]
        j, i = pl.program_id(0), pl.program_id(1)
        ctx = (_place(), b, send, recv)

        def fetch(q):
            return pltpu.make_async_copy(b[0].at[order_ref[q]], w_vmem.at[q % 2], w_sems.at[q % 2])

        @pl.when((j == 0) & (i == 0))
        def _():
            for k in range(n):
                for peer in range(2):
                    _gather_ici(ctx, k, peer, True).start()
            for peer in range(3):
                _gather_small(ctx, n, peer, True).start()
            fetch(0).start()
            fetch(0).wait()

        for q in range(1, n_sh):
            @pl.when((j == q - 1) & (i == hook_i))
            def _(q=q):
                if q == 1:
                    _gather_neighbours_landed(ctx, 0)
                if q == 2:
                    for k in range(1, n):
                        _gather_neighbours_landed(ctx, k)
                if q == 3:
                    for k in range(n):
                        _gather_diagonal_landed(ctx, k)
                _gather_d2d(ctx, 0, q - 1, False).wait_recv()
                fetch(q).start()

            @pl.when((j == q) & (i == 0))
            def _(q=q):
                fetch(q).wait()

        wv = w_vmem.at[j % 2]
        p_ref[...] = (jnp.dot(h_ref[:, 0:r2], wv[0], preferred_element_type=F32)
                      + jnp.dot(h_ref[:, r2:d], wv[1], preferred_element_type=F32))

        @pl.when((j == n_sh - 1) & (i == n_i - 1))
        def _():
            for peer in range(3):
                _gather_small(ctx, n, peer, False).wait_recv()
                _gather_small(ctx, n, peer, True).wait_send()
            for k in range(n):
                for peer in range(3):
                    if k > 0:
                        _gather_d2d(ctx, k, peer, False).wait_recv()
                    _gather_ici(ctx, k, peer, True).wait_send()
                    _gather_d2d(ctx, k, peer, True).wait_send()

    all_bufs = list(bufs) + [small]
    out = pl.pallas_call(
        body, name=name,
        grid_spec=pltpu.PrefetchScalarGridSpec(
            num_scalar_prefetch=1, grid=(n_sh, n_i),
            in_specs=[pl.BlockSpec((tm, d), lambda j, i, o: (i, 0))] + [ANY] * (n + 1),
            out_specs=[pl.BlockSpec((tm, ns), lambda j, i, o: (i, o[j]))] + [ANY] * (n + 1),
            scratch_shapes=[pltpu.VMEM((2, 2, r2, ns), BF16), pltpu.SemaphoreType.DMA((2,)),
                            pltpu.SemaphoreType.DMA((n_sems,)), pltpu.SemaphoreType.DMA((n_sems,))]),
        out_shape=[jax.ShapeDtypeStruct((s, n_sh * ns), F32)]
        + [jax.ShapeDtypeStruct(a.shape, a.dtype) for a in all_bufs],
        input_output_aliases={2 + t: 1 + t for t in range(n + 1)},
        compiler_params=_params("arbitrary", "arbitrary"),
    )(order, h, *all_bufs)
    return out[0], list(out[1:])


def _exchange_comm(grads):
    n = len(grads)
    landing = [lax.empty((N_SHARDS,) + a.shape[2:], a.dtype) for a in grads]

    def copies(srcs, b, send, recv):
        x, y, c, _ = _place()
        return [_remote(srcs[k].at[s, 1 - c], b[k].at[s], send.at[N_SHARDS * k + s], recv.at[N_SHARDS * k + s],
                        (x, y, 1 - c)) for k in range(n) for s in range(N_SHARDS)]

    def start(srcs, b, send, recv):
        for cp in copies(srcs, b, send, recv):
            cp.start()

    def finish(srcs, b, send, recv):
        for cp in copies(srcs, b, send, recv):
            cp.wait()

    return _Comm(grads, landing, N_SHARDS * n, [(0, start)], finish)


def _scatter_comm(chip_sums, landing):
    n = len(chip_sums)

    def big(srcs, b, send, recv, k, j, start):
        x, y, c, chips = _place()
        dst = b[k].at[_chip((x, y)) if start else _chip(chips[j])]
        return _remote(srcs[k].at[_chip(chips[j])], dst, send.at[3 * k + j], recv.at[3 * k + j], (*chips[j], c))

    def start(srcs, b, send, recv):
        for k in range(n):
            for j in range(3):
                big(srcs, b, send, recv, k, j, True).start()

    def finish(srcs, b, send, recv):
        for k in range(n):
            for j in range(3):
                big(srcs, b, send, recv, k, j, False).wait_recv()
                big(srcs, b, send, recv, k, j, True).wait_send()

    return _Comm(chip_sums, landing, 3 * n, [(0, start)], finish)


def _join_comm(halves, small):
    n = len(halves)
    flips = [(fx, fy, fc) for fx in (0, 1) for fy in (0, 1) for fc in (0, 1)][1:]

    def half(b, send, recv, k, start):
        x, y, c, _ = _place()
        return _remote(b[k].at[c], b[k].at[c if start else 1 - c], send.at[k], recv.at[k], (x, y, 1 - c))

    def small_copy(b, send, recv, q, start):
        x, y, c, _ = _place()
        px, py, pc = x ^ flips[q][0], y ^ flips[q][1], c ^ flips[q][2]
        blk = b[n].at[4 * x + 2 * y + c if start else 4 * px + 2 * py + pc]
        return _remote(blk, blk, send.at[n + q], recv.at[n + q], (px, py, pc))

    def start(srcs, b, send, recv):
        for q in range(len(flips)):
            small_copy(b, send, recv, q, True).start()
        for k in range(n):
            half(b, send, recv, k, True).start()

    def finish(srcs, b, send, recv):
        for q in range(len(flips)):
            small_copy(b, send, recv, q, False).wait()
        for k in range(n):
            half(b, send, recv, k, False).wait()

    return _Comm([], list(halves) + [small], n + len(flips), [(0, start)], finish)


def _flat_rows(parts):
    flat = jnp.concatenate([p.reshape(-1) for p in parts])
    assert flat.shape[0] % LANES == 0
    return flat.reshape(-1, LANES)


def _unflatten(flat, shapes):
    out, at = [], 0
    for sh in shapes:
        size = 1
        for dim in sh:
            size *= dim
        out.append(flat[at:at + size].reshape(sh))
        at += size
    assert at == flat.shape[0], (at, flat.shape)
    return out


def _col_shards_to_full(a, rows):
    q = a.shape[1] // rows
    return a.reshape(N_SHARDS, rows, q).transpose(1, 0, 2).reshape(rows, N_SHARDS * q)


def _my_col_shard(full, chip):
    rows, cols = full.shape
    q = cols // N_SHARDS
    return lax.dynamic_index_in_dim(full.reshape(rows, N_SHARDS, q), chip, axis=1, keepdims=False)


def kernel(x, e_norm_pre, e_norm_post, e_w_in, e_a_conv, e_b_conv, e_b_conv_bias, e_b_ln_g, e_b_ln_b, e_w_out, o_norm_pre, o_norm_post, o_w_in, o_c_w, o_c_b, o_c_scale, o_w_out, loss_target, m_e_norm_pre, m_e_norm_post, m_e_w_in, m_e_a_conv, m_e_b_conv, m_e_b_conv_bias, m_e_b_ln_g, m_e_b_ln_b, m_e_w_out, m_o_norm_pre, m_o_norm_post, m_o_w_in, m_o_c_w, m_o_c_b, m_o_c_scale, m_o_w_out, v_e_norm_pre, v_e_norm_post, v_e_w_in, v_e_a_conv, v_e_b_conv, v_e_b_conv_bias, v_e_b_ln_g, v_e_b_ln_b, v_e_w_out, v_o_norm_pre, v_o_norm_post, v_o_w_in, v_o_c_w, v_o_c_b, v_o_c_scale, v_o_w_out):
    _, s, d = x.shape
    w = d // 2
    c = d
    gc = c // N_GROUPS
    wq, cq, gq = w // N_SHARDS, c // N_SHARDS, gc // N_SHARDS
    chip = 2 * lax.axis_index("x") + lax.axis_index("y")
    core = lax.axis_index("c")
    x2 = x.reshape(s, d)
    target = loss_target.reshape(s, d)

    big_w = [e_w_in[0], e_w_out[0], o_w_in[0], o_c_w[0].reshape(N_GROUPS * gq, gc), o_w_out[0]]
    big_m = [m_e_w_in[0], m_e_w_out[0], m_o_w_in[0], m_o_c_w[0].reshape(N_GROUPS * gq, gc), m_o_w_out[0]]
    big_v = [v_e_w_in[0], v_e_w_out[0], v_o_w_in[0], v_o_c_w[0].reshape(N_GROUPS * gq, gc), v_o_w_out[0]]
    coords = jnp.stack([chip, core]).astype(jnp.int32)
    slots = [_cast_into_slot(a, coords, "cast_%d" % k) for k, a in enumerate(big_w)]
    sharded_small = _flat_rows([e_a_conv[0], e_b_conv[0], o_norm_pre, o_norm_post, o_c_scale, o_c_b[0]])
    small_slots = lax.dynamic_update_index_in_dim(jnp.zeros((N_SHARDS,) + sharded_small.shape, F32), sharded_small,
                                                  chip, 0)
    xi, yi = lax.axis_index("x"), lax.axis_index("y")
    order = jnp.stack([chip, 2 * (1 - xi) + yi, 2 * xi + (1 - yi), 2 * (1 - xi) + (1 - yi)]).astype(jnp.int32)
    h0 = _rmsnorm(x2, e_norm_pre, "e_pre_norm")
    p0, (e_w_in_g, e_w_out_g, small_g4) = _gathered_in_proj(h0, slots[:2], small_slots, order, "e_in_proj")
    e_w_in_sm = e_w_in_g.reshape((N_SHARDS,) + big_w[0].shape)
    e_w_out_f = e_w_out_g.reshape(w + w, d)
    sm = small_g4.reshape(N_SHARDS, -1)
    at = [0]

    def take(rows, q):
        blk = sm[:, at[0]:at[0] + rows * q]
        at[0] += rows * q
        return _col_shards_to_full(blk, rows)

    a_conv_f = take(CONV_A, wq)
    b_conv_f = take(CONV_B, wq)
    o_pre_f = take(1, cq)
    o_post_f = take(1, cq)
    cs_f = take(1, cq)
    cb_f = take(N_GROUPS, gq).reshape(1, c)

    mixer_steps = s // _row_tile(s, 128)
    (u0, x1, y0, conv0), odd_g = _even_mixer_fwd(
        p0, e_w_out_f, x2, e_norm_post, a_conv_f, b_conv_f, e_b_conv_bias, e_b_ln_g, e_b_ln_b, "e_mixer_out_proj",
        comm=_gather_comm(slots[2:], mixer_steps // 2, (25 * mixer_steps) // 32))
    o_w_in_sm = odd_g[0].reshape((N_SHARDS,) + big_w[2].shape)
    cw_f = odd_g[1].reshape(N_SHARDS, N_GROUPS, gq, gc).transpose(1, 0, 2, 3).reshape(N_GROUPS, gc, gc)
    o_w_out_f = odd_g[2].reshape(c, d)
    p1, h1 = _norm_matmul(x1, o_pre_f, o_w_in_sm, "o_in_proj")
    u1 = _odd_mixer_fwd(p1, cw_f, cb_f, cs_f, "o_mixer_fwd")
    d_y1, d_x2, d_o_post, loss_part = _matmul_post_loss(u1, o_w_out_f, x1, o_post_f, target, "o_out_proj_loss")

    def as_pieces(g, k):
        return g.reshape(N_SHARDS, 2, big_w[k].shape[0] // 2, big_w[k].shape[1])

    def chip_sums(ks, pieces, from_sibling):
        both = [_chip_sum(g, o, coords, "chip_sum_%d" % k) for k, g, o in zip(ks, pieces, from_sibling)]
        return [b[0] for b in both], [b[1] for b in both]

    g_o_w_out = _matmul_tn(u1, d_y1, 1, "o_w_out_grad")
    d_u1, _ = _matmul_nt(d_y1, o_w_out_f[None], "o_out_proj_bwd")
    d_p1, d_cw, d_cb, d_cs = _odd_mixer_bwd(p1, d_u1, cw_f, cb_f, cs_f, "o_mixer_bwd")
    g_o_w_in = _matmul_tn(h1, d_p1, N_SHARDS, "o_w_in_grad")
    g_cw = d_cw.reshape(N_GROUPS, N_SHARDS, gq, gc).transpose(1, 0, 2, 3).astype(BF16)
    pieces_o = [as_pieces(g_o_w_in, 2), as_pieces(g_cw, 3), as_pieces(g_o_w_out, 4)]
    d_h1, sibling_o = _matmul_nt(d_p1, o_w_in_sm, "o_in_proj_bwd", comm=_exchange_comm(pieces_o))
    d_x1, d_o_pre, d_y0, d_e_post = _norm_bwd(d_h1, x1, o_pre_f, d_x2, "o_pre_norm_bwd", post=(y0, e_norm_post))

    pieces_e = [as_pieces(_matmul_tn(u0, d_y0, 1, "e_w_out_grad"), 1)]
    d_u0, sibling_e = _matmul_nt(d_y0, e_w_out_f[None], "e_out_proj_bwd", comm=_exchange_comm(pieces_e))
    sums_a, landing_a = chip_sums([1, 2, 3, 4], pieces_e + pieces_o, sibling_e + sibling_o)
    (d_p0, d_a_conv, d_b_conv, d_bias, d_ln_g, d_ln_b), landed_a = _even_mixer_bwd(
        p0, d_u0, conv0, a_conv_f, b_conv_f, e_b_ln_g, e_b_ln_b, "e_mixer_bwd",
        comm=_scatter_comm(sums_a, landing_a))
    pieces_b = [as_pieces(_matmul_tn(h0, d_p0, N_SHARDS, "e_w_in_grad"), 0)]
    sums_b, landing_b = chip_sums([0], pieces_b, _comm_only(_exchange_comm(pieces_b), "exchange_core_halves"))
    d_h0, landed_b = _matmul_nt(d_p0, e_w_in_sm, "e_in_proj_bwd", comm=_scatter_comm(sums_b, landing_b))
    grad_x, d_e_pre = _norm_bwd(d_h0, x2, e_norm_pre, d_x1, "e_pre_norm_bwd")

    landed = landed_b + landed_a
    reduced = [_shard_sum(sc, coords, "shard_sum_%d" % k) for k, sc in enumerate(landed)]
    small_parts = _flat_rows([loss_part[0], d_e_pre, d_e_post, d_bias, d_ln_g, d_ln_b, d_a_conv, d_b_conv,
                              d_o_pre, d_o_post, d_cs, d_cb])
    small_rows = lax.dynamic_update_index_in_dim(jnp.zeros((N_DEVICES,) + small_parts.shape, F32), small_parts,
                                                 2 * chip + core, 0)
    joined = _comm_only(_join_comm(reduced, small_rows), "join_core_halves")
    big_g = [j.reshape(a.shape) for j, a in zip(joined[:5], big_w)]
    small_sum = _sum_small(joined[5], "small_sum").reshape(-1)
    (loss_row, g_e_pre, g_e_post, g_bias, g_ln_g, g_ln_b, g_a_conv_f, g_b_conv_f, g_o_pre_f, g_o_post_f, g_cs_f,
     g_cb_f) = _unflatten(small_sum, [(LANES,), (1, d), (1, d), (1, w), (1, w), (1, w), (CONV_A, w), (CONV_B, w),
                                      (1, c), (1, c), (1, c), (1, c)])
    loss = loss_row[0]
    g_a_conv = _my_col_shard(g_a_conv_f, chip)
    g_b_conv = _my_col_shard(g_b_conv_f, chip)
    g_o_pre = _my_col_shard(g_o_pre_f, chip)
    g_o_post = _my_col_shard(g_o_post_f, chip)
    g_cs = _my_col_shard(g_cs_f, chip)
    g_cb = _my_col_shard(g_cb_f.reshape(N_GROUPS, gc), chip)

    big_upd = [_adamw(wt, g, m, v, "adamw_%d" % k) for k, (wt, g, m, v) in enumerate(zip(big_w, big_g, big_m, big_v))]
    small_w = [e_norm_pre, e_norm_post, e_b_conv_bias, e_b_ln_g, e_b_ln_b, e_a_conv[0], e_b_conv[0],
               o_norm_pre, o_norm_post, o_c_b[0], o_c_scale]
    small_m = [m_e_norm_pre, m_e_norm_post, m_e_b_conv_bias, m_e_b_ln_g, m_e_b_ln_b, m_e_a_conv[0], m_e_b_conv[0],
               m_o_norm_pre, m_o_norm_post, m_o_c_b[0], m_o_c_scale]
    small_v = [v_e_norm_pre, v_e_norm_post, v_e_b_conv_bias, v_e_b_ln_g, v_e_b_ln_b, v_e_a_conv[0], v_e_b_conv[0],
               v_o_norm_pre, v_o_norm_post, v_o_c_b[0], v_o_c_scale]
    small_g = [g_e_pre, g_e_post, g_bias, g_ln_g, g_ln_b, g_a_conv, g_b_conv, g_o_pre, g_o_post, g_cb, g_cs]
    small_delta, small_new_m, small_new_v = _adamw_many(small_w, small_g, small_m, small_v, "adamw_small")

    def ordered(small, big):
        (n_pre, n_post, bias, ln_g, ln_b, a_conv, b_conv, o_pre, o_post, cb, cs) = small
        (w_in, w_out, ow_in, cw, ow_out) = big
        return [n_pre, n_post, w_in[None], a_conv[None], b_conv[None], bias, ln_g, ln_b, w_out[None], o_pre, o_post,
                ow_in[None], cw.reshape(1, N_GROUPS, gq, gc), cb[None], cs, ow_out[None]]

    grads = ordered(small_g, big_g)
    deltas = ordered(small_delta, [u[0] for u in big_upd])
    new_m = ordered(small_new_m, [u[1] for u in big_upd])
    new_v = ordered(small_new_v, [u[2] for u in big_upd])
    return (loss, grad_x.reshape(1, s, d), *grads, *deltas, *new_m, *new_v)
```

```python
import functools

import jax
import jax.numpy as jnp
from jax import lax
from jax.experimental import pallas as pl
from jax.experimental.pallas import tpu as pltpu

F32 = jnp.float32
BF16 = jnp.bfloat16
MESH = pl.DeviceIdType.MESH

EPS = 1e-6
CONV_A = 3
CONV_B = 31
POOL_WINDOWS = (2, 4, 8, 16)
N_GROUPS = len(POOL_WINDOWS)
N_SHARDS = 4
N_DEVICES = 8
ADAM_LR = 0.001
ADAM_B1 = 0.9
ADAM_B2 = 0.999
ADAM_EPS = 1e-08
ADAM_WD = 0.01
ADAM_STEP = 10

LANES = 128
SUBLANES_BF16 = 16
HALO_A = 8
HALO_B = 32
HALO_P = 16
PAD_P = 8
SHIFTS = 8
ROW_BLOCK = 32
LANE_BLOCK = 256
LANE_PASS = 256
VMEM_LIMIT = 56 * 1024 * 1024
TN_ACC_BYTES = 8 * 1024 * 1024
EVEN_BWD_ROWS = 128


def _row_tile(n, pref):
    t = max(min(n, pref) // SUBLANES_BF16, 1) * SUBLANES_BF16
    while t > SUBLANES_BF16 and (n % t or t % SUBLANES_BF16):
        t -= SUBLANES_BF16
    assert n % t == 0, (n, pref)
    return t


def _col_chunk(n, pref):
    t = (min(n, pref) // LANES) * LANES
    while t > LANES and n % t:
        t -= LANES
    assert t >= LANES and n % t == 0, (n, pref)
    return t


def _params(*sem):
    return pltpu.CompilerParams(dimension_semantics=tuple(sem) if sem else None, vmem_limit_bytes=VMEM_LIMIT)


ANY = pl.BlockSpec(memory_space=pl.ANY)


class _Comm:
    def __init__(self, srcs, bufs, n_sems, phases, finish):
        self.srcs, self.bufs, self.n_sems, self.phases, self.finish = list(srcs), list(bufs), n_sems, phases, finish


def _call(body, *, name, grid, in_specs, out_specs, out_shape, args, scratch_shapes=(), comm=None):
    params = _params(*(("arbitrary",) * len(grid)))
    if comm is None:
        out = pl.pallas_call(body, name=name, grid=grid, in_specs=in_specs, out_specs=out_specs, out_shape=out_shape,
                             scratch_shapes=scratch_shapes, compiler_params=params)(*args)
        return list(out), []
    n_in, n_out, n_scr = len(in_specs), len(out_specs), len(scratch_shapes)
    ns, nb = len(comm.srcs), len(comm.bufs)
    total = 1
    for size in grid:
        total *= size

    def fused(*refs):
        ins, srcs = refs[:n_in], refs[n_in:n_in + ns]
        at = n_in + ns + nb
        outs, bufs = refs[at:at + n_out], refs[at + n_out:at + n_out + nb]
        scratch = refs[at + n_out + nb:at + n_out + nb + n_scr]
        send_sems, recv_sems = refs[-2:]
        step = 0
        for axis, size in enumerate(grid):
            step = step * size + pl.program_id(axis)
        for when, fn in comm.phases:
            pl.when(step == when)(functools.partial(fn, srcs, bufs, send_sems, recv_sems))
        body(*ins, *outs, *scratch)
        pl.when(step == total - 1)(functools.partial(comm.finish, srcs, bufs, send_sems, recv_sems))

    out = pl.pallas_call(
        fused, name=name, grid=grid,
        in_specs=list(in_specs) + [ANY] * (ns + nb), out_specs=list(out_specs) + [ANY] * nb,
        out_shape=list(out_shape) + [jax.ShapeDtypeStruct(b.shape, b.dtype) for b in comm.bufs],
        input_output_aliases={n_in + ns + i: n_out + i for i in range(nb)},
        scratch_shapes=list(scratch_shapes) + [pltpu.SemaphoreType.DMA((comm.n_sems,))] * 2,
        compiler_params=params,
    )(*args, *comm.srcs, *comm.bufs)
    return list(out[:n_out]), list(out[n_out:])


def _comm_only(comm, name):
    ns, nb = len(comm.srcs), len(comm.bufs)

    def body(*refs):
        srcs, bufs = refs[:ns], refs[ns + nb:ns + 2 * nb]
        send_sems, recv_sems = refs[-2:]
        for _, fn in comm.phases:
            fn(srcs, bufs, send_sems, recv_sems)
        comm.finish(srcs, bufs, send_sems, recv_sems)

    return pl.pallas_call(
        body, name=name, in_specs=[ANY] * (ns + nb), out_specs=[ANY] * nb,
        out_shape=[jax.ShapeDtypeStruct(b.shape, b.dtype) for b in comm.bufs],
        input_output_aliases={ns + i: i for i in range(nb)},
        scratch_shapes=[pltpu.SemaphoreType.DMA((comm.n_sems,))] * 2,
    )(*comm.srcs, *comm.bufs)


def _sigmoid(v):
    return jax.nn.sigmoid(v)


def _dsilu(v, s):
    return s * (1.0 + v * (1.0 - s))


def _mean_last(v):
    return jnp.mean(v, axis=-1, keepdims=True)


def _sum_rows(v):
    return jnp.sum(v, axis=0, keepdims=True)


def _norm_matmul(x, gain, w_sm, name):
    s, d = x.shape
    n_sh, _, ns = w_sm.shape
    tm = _row_tile(s, 1024)

    def body(x_ref, g_ref, w_ref, p_ref, h_ref):
        @pl.when(pl.program_id(1) == 0)
        def _():
            xv = x_ref[...]
            r = lax.rsqrt(_mean_last(xv * xv) + EPS)
            h_ref[...] = (xv * r * g_ref[...]).astype(BF16)

        p_ref[...] = jnp.dot(h_ref[...], w_ref[0], preferred_element_type=F32)

    return _call(
        body, name=name, grid=(s // tm, n_sh),
        in_specs=[pl.BlockSpec((tm, d), lambda i, j: (i, 0)),
                  pl.BlockSpec((1, d), lambda i, j: (0, 0)),
                  pl.BlockSpec((1, d, ns), lambda i, j: (j, 0, 0))],
        out_specs=[pl.BlockSpec((tm, ns), lambda i, j: (i, j)),
                   pl.BlockSpec((tm, d), lambda i, j: (i, 0))],
        out_shape=[jax.ShapeDtypeStruct((s, n_sh * ns), F32), jax.ShapeDtypeStruct((s, d), BF16)],
        args=(x, gain, w_sm))[0]


def _matmul_post_loss(u, w, x_res, gain, target, name):
    s, k = u.shape
    d = w.shape[1]
    tm = _row_tile(s, 512)
    half = tm // 2

    def body(u_ref, w_ref, x_ref, g_ref, t_ref, dy_ref, dout_ref, dg_ref, loss_ref):
        @pl.when(pl.program_id(0) == 0)
        def _():
            dg_ref[...] = jnp.zeros_like(dg_ref)
            loss_ref[...] = jnp.zeros_like(loss_ref)

        g = g_ref[...]
        for rows in (slice(0, half), slice(half, tm)):
            y = jnp.dot(u_ref[rows, :], w_ref[...], preferred_element_type=F32)
            r = lax.rsqrt(_mean_last(y * y) + EPS)
            n = y * r
            err = x_ref[rows, :] + n * g - t_ref[rows, :]
            loss_ref[...] += 0.5 * jnp.sum(_mean_last(err * err))
            dout = err * (1.0 / d)
            dout_ref[rows, :] = dout
            dg_ref[...] += _sum_rows(dout * n)
            dn = dout * g
            dy_ref[rows, :] = (r * (dn - n * _mean_last(dn * n))).astype(BF16)

    return pl.pallas_call(
        body, name=name, grid=(s // tm,),
        in_specs=[pl.BlockSpec((tm, k), lambda i: (i, 0)),
                  pl.BlockSpec((k, d), lambda i: (0, 0), pipeline_mode=pl.Buffered(1)),
                  pl.BlockSpec((tm, d), lambda i: (i, 0)),
                  pl.BlockSpec((1, d), lambda i: (0, 0)),
                  pl.BlockSpec((tm, d), lambda i: (i, 0))],
        out_specs=[pl.BlockSpec((tm, d), lambda i: (i, 0)),
                   pl.BlockSpec((tm, d), lambda i: (i, 0)),
                   pl.BlockSpec((1, d), lambda i: (0, 0)),
                   pl.BlockSpec((8, LANES), lambda i: (0, 0))],
        out_shape=[jax.ShapeDtypeStruct((s, d), BF16), jax.ShapeDtypeStruct((s, d), F32),
                   jax.ShapeDtypeStruct((1, d), F32), jax.ShapeDtypeStruct((8, LANES), F32)],
        compiler_params=_params("arbitrary"),
    )(u, w, x_res, gain, target)


def _matmul_nt(a, w_sm, name, comm=None):
    s, ncols = a.shape
    n_sh, r, ns = w_sm.shape
    assert ncols == n_sh * ns
    tm = _row_tile(s, 1024)
    nc = _col_chunk(ns, 1792)
    per = ns // nc
    steps = n_sh * per

    def body(a_ref, w_ref, o_ref):
        part = lax.dot_general(a_ref[...], w_ref[0], (((1,), (1,)), ((), ())), preferred_element_type=F32)

        @pl.when(pl.program_id(1) == 0)
        def _():
            o_ref[...] = part

        @pl.when(pl.program_id(1) > 0)
        def _():
            o_ref[...] += part

    out, bufs = _call(
        body, name=name, grid=(s // tm, steps),
        in_specs=[pl.BlockSpec((tm, nc), lambda i, j: (i, j)),
                  pl.BlockSpec((1, r, nc), lambda i, j: (j // per, 0, j % per))],
        out_specs=[pl.BlockSpec((tm, r), lambda i, j: (i, 0))],
        out_shape=[jax.ShapeDtypeStruct((s, r), F32)],
        args=(a, w_sm), comm=comm)
    return out[0], bufs


def _matmul_tn(a, b, n_sh, name):
    s, k = a.shape
    n = b.shape[1]
    ns = n // n_sh
    tk = _col_chunk(k, TN_ACC_BYTES // (4 * ns))
    ts = _row_tile(s, 2048)
    n_s = s // ts

    def body(a_ref, b_ref, o_ref, acc_ref):
        part = lax.dot_general(a_ref[...], b_ref[...], (((0,), (0,)), ((), ())), preferred_element_type=F32)

        @pl.when(pl.program_id(2) == 0)
        def _():
            acc_ref[...] = part

        @pl.when(pl.program_id(2) > 0)
        def _():
            acc_ref[...] += part

        @pl.when(pl.program_id(2) == n_s - 1)
        def _():
            o_ref[0] = acc_ref[...].astype(BF16)

    return pl.pallas_call(
        body, name=name, grid=(n_sh, k // tk, n_s),
        in_specs=[pl.BlockSpec((ts, tk), lambda j, i, t: (t, i)),
                  pl.BlockSpec((ts, ns), lambda j, i, t: (t, j))],
        out_specs=pl.BlockSpec((1, tk, ns), lambda j, i, t: (j, i, 0)),
        out_shape=jax.ShapeDtypeStruct((n_sh, k, ns), BF16),
        scratch_shapes=[pltpu.VMEM((tk, ns), F32)],
        compiler_params=_params("arbitrary", "arbitrary", "arbitrary"),
    )(a, b)


def _norm_bwd(dh, x, gain, dres, name, post=None):
    s, d = x.shape
    tm = _row_tile(s, 256)
    with_post = post is not None

    def rms_bwd(dout, v, g):
        r = lax.rsqrt(_mean_last(v * v) + EPS)
        n = v * r
        dn = dout * g
        return r * (dn - n * _mean_last(dn * n)), _sum_rows(dout * n)

    def body(*refs):
        if with_post:
            dh_ref, x_ref, g_ref, dres_ref, y_ref, gp_ref, dx_ref, dg_ref, dy_ref, dgp_ref = refs
        else:
            dh_ref, x_ref, g_ref, dres_ref, dx_ref, dg_ref = refs

        @pl.when(pl.program_id(0) == 0)
        def _():
            dg_ref[...] = jnp.zeros_like(dg_ref)
            if with_post:
                dgp_ref[...] = jnp.zeros_like(dgp_ref)

        dv, dg = rms_bwd(dh_ref[...], x_ref[...], g_ref[...])
        dx = dres_ref[...] + dv
        dx_ref[...] = dx
        dg_ref[...] += dg
        if with_post:
            dy, dgp = rms_bwd(dx, y_ref[...], gp_ref[...])
            dy_ref[...] = dy.astype(BF16)
            dgp_ref[...] += dgp

    row = pl.BlockSpec((tm, d), lambda i: (i, 0))
    vec = pl.BlockSpec((1, d), lambda i: (0, 0))
    in_specs = [row, row, vec, row]
    out_specs = [row, vec]
    out_shape = [jax.ShapeDtypeStruct((s, d), F32), jax.ShapeDtypeStruct((1, d), F32)]
    args = [dh, x, gain, dres]
    if with_post:
        in_specs += [row, vec]
        out_specs += [row, vec]
        out_shape += [jax.ShapeDtypeStruct((s, d), BF16), jax.ShapeDtypeStruct((1, d), F32)]
        args += list(post)
    return _call(body, name=name, grid=(s // tm,), in_specs=in_specs, out_specs=out_specs, out_shape=out_shape,
                 args=args)[0]


def _fill_shifted_down(sh, rows):
    for b in range(1, SHIFTS):
        sh[b, SHIFTS:rows, :] = sh[0, SHIFTS - b:rows - b, :]


def _fill_shifted_up(sh, rows):
    for b in range(1, SHIFTS):
        sh[b, 0:rows - SHIFTS, :] = sh[0, b:rows - SHIFTS + b, :]


def _for_blocks(ts, w, fn):
    lb = min(LANE_BLOCK, w)
    for l0 in range(0, w, lb):
        def rows(rb, carry, l0=l0):
            fn(pl.multiple_of(rb * ROW_BLOCK, ROW_BLOCK), slice(l0, l0 + lb))
            return carry

        lax.fori_loop(0, ts // ROW_BLOCK, rows, 0)


TAP_SPAN = SHIFTS * ((CONV_B - 1) // SHIFTS)
WINDOW = ROW_BLOCK + TAP_SPAN


def _taps_of(b):
    return [(a, SHIFTS * a + b) for a in range((CONV_B - 1 - b) // SHIFTS + 1)]


def _conv31(sh, base, step, wt_ref, bias_ref, out_ref, ts, w):
    low = min(0, step * (TAP_SPAN // SHIFTS))

    def block(r0, lanes):
        acc = [jnp.zeros((SHIFTS, lanes.stop - lanes.start), F32) for _ in range(ROW_BLOCK // SHIFTS)]
        for b in range(SHIFTS):
            window = sh[b, pl.ds(pl.multiple_of(r0 + (base + low), SHIFTS), WINDOW), lanes]
            for a, j in (_taps_of(b) if step > 0 else reversed(_taps_of(b))):
                at = step * a - low
                wt = wt_ref[CONV_B - 1 - j, :, lanes]
                acc = [v + wt * window[at + SHIFTS * r:at + SHIFTS * (r + 1), :] for r, v in enumerate(acc)]
        for r, v in enumerate(acc):
            if bias_ref is not None:
                v = v + bias_ref[:, lanes]
            out_ref[pl.ds(pl.multiple_of(r0 + SHIFTS * r, SHIFTS), SHIFTS), lanes] = v

    _for_blocks(ts, w, block)


def _conv31_weight_grad(d_sh, x_sh, wacc, ts, w):
    def block(r0, lanes):
        d = d_sh[0, pl.ds(r0, ROW_BLOCK), lanes]
        for b in range(SHIFTS):
            window = x_sh[b, pl.ds(pl.multiple_of(r0 + (HALO_B - TAP_SPAN), SHIFTS), WINDOW), lanes]
            for a, j in _taps_of(b):
                at = TAP_SPAN - SHIFTS * a
                prod = d * window[at:at + ROW_BLOCK, :]
                part = prod[0:SHIFTS, :]
                for q in range(1, ROW_BLOCK // SHIFTS):
                    part = part + prod[q * SHIFTS:(q + 1) * SHIFTS, :]
                wacc[CONV_B - 1 - j, :, lanes] += part

    _for_blocks(ts, w, block)


def _even_forward_tile(p_ref, halo_ref, first, a_conv_ref, b_conv_ref, bias_ref, lng_ref, lnb_ref, qbuf, ysh, y1buf,
                       wb, w, ts):
    @pl.when(pl.program_id(0) == 0)
    def _():
        for k in range(CONV_B):
            wb[k] = jnp.broadcast_to(b_conv_ref[k:k + 1, :], (SHIFTS, w))

    def col(ref, k, rows=slice(None)):
        return ref[rows, k * w:(k + 1) * w]

    a_x, a_b, a_c, a_z = col(p_ref, 0), col(p_ref, 1), col(p_ref, 2), col(p_ref, 3)
    b_val, b_gate, b_z = col(p_ref, 4), col(p_ref, 5), col(p_ref, 6)
    keep = jnp.where(first, 0.0, 1.0)

    rows_a = slice(HALO_B - HALO_A, HALO_B)
    qbuf[0:HALO_A, :] = col(halo_ref, 2, rows_a) * col(halo_ref, 0, rows_a) * keep
    qbuf[HALO_A:HALO_A + ts, :] = a_c * a_x
    cq = jnp.zeros((ts, w), F32)
    for j in range(CONV_A):
        cq = cq + a_conv_ref[CONV_A - 1 - j:CONV_A - j, :] * qbuf[HALO_A - j:HALO_A - j + ts, :]
    ya = a_b * cq

    ysh[0, 0:HALO_B, :] = col(halo_ref, 4) * _sigmoid(col(halo_ref, 5)) * keep
    ysh[0, HALO_B:HALO_B + ts, :] = b_val * _sigmoid(b_gate)
    _fill_shifted_down(ysh, HALO_B + ts)
    _conv31(ysh, HALO_B, -SHIFTS, wb, bias_ref, y1buf, ts, w)
    yb1 = y1buf[...]
    xc = yb1 - _mean_last(yb1)
    rstd = lax.rsqrt(_mean_last(xc * xc) + EPS)
    xhat = xc * rstd
    yb2 = xhat * lng_ref[...] + lnb_ref[...]
    return dict(a_x=a_x, a_b=a_b, a_c=a_c, a_z=a_z, b_val=b_val, b_gate=b_gate, b_z=b_z,
                cq=cq, ya=ya, rstd=rstd, xhat=xhat, yb2=yb2)


def _even_specs(s, w, ts):
    tile = pl.BlockSpec((ts, 7 * w), lambda i: (i, 0))
    halo = pl.BlockSpec((HALO_B, 7 * w), lambda i: (jnp.maximum(i * (ts // HALO_B) - 1, 0), 0))
    return tile, halo


def _small_specs(shapes, index=lambda i: (0, 0)):
    return [pl.BlockSpec(sh, index) for sh in shapes]


def _even_mixer_fwd(p, w_out, x_res, gain, a_conv, b_conv, bias, ln_g, ln_b, name, comm=None):
    s, d = x_res.shape
    w = p.shape[1] // 7
    ts = _row_tile(s, 128)
    assert ts % HALO_B == 0

    def body(p_ref, halo_ref, wout_ref, x_ref, g_ref, ac_ref, bc_ref, bias_ref, lng_ref, lnb_ref,
             u_ref, xn_ref, y_ref, conv_ref, qbuf, ysh, wb):
        first = pl.program_id(0) == 0
        f = _even_forward_tile(p_ref, halo_ref, first, ac_ref, bc_ref, bias_ref, lng_ref, lnb_ref, qbuf, ysh, conv_ref,
                               wb, w, ts)
        yb3 = f["yb2"] * _sigmoid(f["yb2"])
        u_a = (f["ya"] * (f["a_z"] * _sigmoid(f["a_z"]))).astype(BF16)
        u_b = (yb3 * (f["b_z"] * _sigmoid(f["b_z"]))).astype(BF16)
        u_ref[:, 0:w] = u_a
        u_ref[:, w:2 * w] = u_b
        y = (jnp.dot(u_a, wout_ref[0:w, :], preferred_element_type=F32)
             + jnp.dot(u_b, wout_ref[w:2 * w, :], preferred_element_type=F32))
        r = lax.rsqrt(_mean_last(y * y) + EPS)
        y_ref[...] = y
        xn_ref[...] = x_ref[...] + (y * r) * g_ref[...]

    tile, halo = _even_specs(s, w, ts)
    row = pl.BlockSpec((ts, d), lambda i: (i, 0))
    return _call(
        body, name=name, grid=(s // ts,),
        in_specs=[tile, halo, pl.BlockSpec((2 * w, d), lambda i: (0, 0)), row, pl.BlockSpec((1, d), lambda i: (0, 0))]
        + _small_specs([(CONV_A, w), (CONV_B, w), (1, w), (1, w), (1, w)]),
        out_specs=[pl.BlockSpec((ts, 2 * w), lambda i: (i, 0)), row, row, pl.BlockSpec((ts, w), lambda i: (i, 0))],
        out_shape=[jax.ShapeDtypeStruct((s, 2 * w), BF16), jax.ShapeDtypeStruct((s, d), F32),
                   jax.ShapeDtypeStruct((s, d), F32), jax.ShapeDtypeStruct((s, w), F32)],
        scratch_shapes=[pltpu.VMEM((HALO_A + ts, w), F32), pltpu.VMEM((SHIFTS, HALO_B + ts, w), F32),
                        pltpu.VMEM((CONV_B, SHIFTS, w), F32)],
        args=(p, p, w_out, x_res, gain, a_conv, b_conv, bias, ln_g, ln_b), comm=comm)


def _even_mixer_bwd(p, du, conv, a_conv, b_conv, ln_g, ln_b, name, comm=None):
    s = p.shape[0]
    w = p.shape[1] // 7
    ts = _row_tile(s, EVEN_BWD_ROWS)
    nt = s // ts
    assert ts % HALO_B == 0

    def body(p_ref, halo_ref, du_ref, conv_ref, ac_ref, bc_ref, lng_ref, lnb_ref,
             dp_ref, dac_ref, dbc_ref, dbias_ref, dlng_ref, dlnb_ref,
             qbuf, ysh, y1buf, dqbuf, dsh, dy0buf, wacc, carry_dq, carry_dy, wb):
        step = pl.program_id(0)
        first = step == nt - 1

        @pl.when(step == 0)
        def _():
            for ref in (dac_ref, dbias_ref, dlng_ref, dlnb_ref, wacc, carry_dq, carry_dy):
                ref[...] = jnp.zeros_like(ref)

        @pl.when(step == 0)
        def _():
            for k in range(CONV_B):
                wb[k] = jnp.broadcast_to(bc_ref[k:k + 1, :], (SHIFTS, w))

        keep = jnp.where(first, 0.0, 1.0)
        lane_blocks = [slice(l0, l0 + min(LANE_PASS, w)) for l0 in range(0, w, min(LANE_PASS, w))]

        def col(ref, k, lanes, rows=slice(None)):
            return ref[rows, k * w + lanes.start:k * w + lanes.stop]


        rows_a = slice(HALO_B - HALO_A, HALO_B)
        for lanes in lane_blocks:
            a_x, a_b, a_c, a_z = (col(p_ref, k, lanes) for k in range(4))
            qbuf[0:HALO_A, lanes] = col(halo_ref, 2, lanes, rows_a) * col(halo_ref, 0, lanes, rows_a) * keep
            qbuf[HALO_A:HALO_A + ts, lanes] = a_c * a_x
            cq = jnp.zeros((ts, lanes.stop - lanes.start), F32)
            for j in range(CONV_A):
                cq = cq + ac_ref[CONV_A - 1 - j:CONV_A - j, lanes] * qbuf[HALO_A - j:HALO_A - j + ts, lanes]
            s_az = _sigmoid(a_z)
            du_a = du_ref[:, lanes]
            d_ya = du_a * (a_z * s_az)
            dp_ref[:, 3 * w + lanes.start:3 * w + lanes.stop] = (du_a * (a_b * cq) * _dsilu(a_z, s_az)).astype(BF16)
            dp_ref[:, 1 * w + lanes.start:1 * w + lanes.stop] = (d_ya * cq).astype(BF16)
            d_cq = d_ya * a_b
            dqbuf[0:ts, lanes] = d_cq
            dqbuf[ts:ts + HALO_A, lanes] = carry_dq[:, lanes]
            carry_dq[:, lanes] = d_cq[0:HALO_A, :]
            d_q = jnp.zeros_like(cq)
            for o in range(CONV_A):
                d_q = d_q + ac_ref[CONV_A - 1 - o:CONV_A - o, lanes] * dqbuf[o:o + ts, lanes]
            for j in range(CONV_A):
                k = CONV_A - 1 - j
                dac_ref[k:k + 1, lanes] += _sum_rows(d_cq * qbuf[HALO_A - j:HALO_A - j + ts, lanes])
            dp_ref[:, 2 * w + lanes.start:2 * w + lanes.stop] = (d_q * a_x).astype(BF16)
            dp_ref[:, 0 * w + lanes.start:0 * w + lanes.stop] = (d_q * a_c).astype(BF16)
            ysh[0, 0:HALO_B, lanes] = col(halo_ref, 4, lanes) * _sigmoid(col(halo_ref, 5, lanes)) * keep
            ysh[0, HALO_B:HALO_B + ts, lanes] = col(p_ref, 4, lanes) * _sigmoid(col(p_ref, 5, lanes))
        _fill_shifted_down(ysh, HALO_B + ts)

        total = jnp.zeros((ts, 1), F32)
        for lanes in lane_blocks:
            total = total + jnp.sum(conv_ref[:, lanes], axis=-1, keepdims=True)
        mu = total * (1.0 / w)
        total = jnp.zeros((ts, 1), F32)
        for lanes in lane_blocks:
            xc = conv_ref[:, lanes] - mu
            total = total + jnp.sum(xc * xc, axis=-1, keepdims=True)
        rstd = lax.rsqrt(total * (1.0 / w) + EPS)

        sum_dxh = jnp.zeros((ts, 1), F32)
        sum_dxh_xhat = jnp.zeros((ts, 1), F32)
        for lanes in lane_blocks:
            xhat = (conv_ref[:, lanes] - mu) * rstd
            yb2 = xhat * lng_ref[:, lanes] + lnb_ref[:, lanes]
            b_z = col(p_ref, 6, lanes)
            s_bz, s_y2 = _sigmoid(b_z), _sigmoid(yb2)
            du_b = du_ref[:, w + lanes.start:w + lanes.stop]
            dp_ref[:, 6 * w + lanes.start:6 * w + lanes.stop] = (du_b * (yb2 * s_y2) * _dsilu(b_z, s_bz)).astype(BF16)
            d_yb2 = du_b * (b_z * s_bz) * _dsilu(yb2, s_y2)
            dlng_ref[:, lanes] += _sum_rows(d_yb2 * xhat)
            dlnb_ref[:, lanes] += _sum_rows(d_yb2)
            d_xh = d_yb2 * lng_ref[:, lanes]
            sum_dxh = sum_dxh + jnp.sum(d_xh, axis=-1, keepdims=True)
            sum_dxh_xhat = sum_dxh_xhat + jnp.sum(d_xh * xhat, axis=-1, keepdims=True)
            y1buf[:, lanes] = xhat
            dy0buf[:, lanes] = d_xh
        mean_dxh = sum_dxh * (1.0 / w)
        mean_dxh_xhat = sum_dxh_xhat * (1.0 / w)

        for lanes in lane_blocks:
            d_yb1 = rstd * (dy0buf[:, lanes] - mean_dxh - y1buf[:, lanes] * mean_dxh_xhat)
            dbias_ref[:, lanes] += _sum_rows(d_yb1)
            dsh[0, 0:ts, lanes] = d_yb1
            dsh[0, ts:ts + HALO_B, lanes] = carry_dy[:, lanes]
            carry_dy[:, lanes] = d_yb1[0:HALO_B, :]
        _fill_shifted_up(dsh, ts + HALO_B)
        _conv31(dsh, 0, SHIFTS, wb, None, dy0buf, ts, w)
        _conv31_weight_grad(dsh, ysh, wacc, ts, w)

        @pl.when(step == nt - 1)
        def _():
            for k in range(CONV_B):
                dbc_ref[k:k + 1, :] = _sum_rows(wacc[k])

        for lanes in lane_blocks:
            d_yb0 = dy0buf[:, lanes]
            s_g = _sigmoid(col(p_ref, 5, lanes))
            dp_ref[:, 4 * w + lanes.start:4 * w + lanes.stop] = (d_yb0 * s_g).astype(BF16)
            dp_ref[:, 5 * w + lanes.start:5 * w + lanes.stop] = (
                d_yb0 * col(p_ref, 4, lanes) * s_g * (1.0 - s_g)).astype(BF16)

    rev = lambda i: (nt - 1 - i, 0)
    tile = pl.BlockSpec((ts, 7 * w), rev)
    halo = pl.BlockSpec((HALO_B, 7 * w), lambda i: (jnp.maximum((nt - 1 - i) * (ts // HALO_B) - 1, 0), 0))
    small = [(CONV_A, w), (CONV_B, w), (1, w), (1, w), (1, w)]
    return _call(
        body, name=name, grid=(nt,),
        in_specs=[tile, halo, pl.BlockSpec((ts, 2 * w), rev), pl.BlockSpec((ts, w), rev)]
        + _small_specs([(CONV_A, w), (CONV_B, w), (1, w), (1, w)]),
        out_specs=[pl.BlockSpec((ts, 7 * w), rev)] + _small_specs(small),
        out_shape=[jax.ShapeDtypeStruct((s, 7 * w), BF16)] + [jax.ShapeDtypeStruct(sh, F32) for sh in small],
        scratch_shapes=[pltpu.VMEM((HALO_A + ts, w), F32), pltpu.VMEM((SHIFTS, HALO_B + ts, w), F32),
                        pltpu.VMEM((ts, w), F32),
                        pltpu.VMEM((ts + HALO_A, w), F32), pltpu.VMEM((SHIFTS, ts + HALO_B, w), F32),
                        pltpu.VMEM((ts, w), F32), pltpu.VMEM((CONV_B, SHIFTS, w), F32),
                        pltpu.VMEM((HALO_A, w), F32), pltpu.VMEM((HALO_B, w), F32),
                        pltpu.VMEM((CONV_B, SHIFTS, w), F32)],
        args=(p, p, du, conv, a_conv, b_conv, ln_g, ln_b), comm=comm)


def _trailing_sums(buf_a, buf_b, cols, win, rows, first_out):
    src, dst, shift = buf_a, buf_b, 1
    while True:
        last = 2 * shift >= win
        lo = first_out if last else 0
        val = src[PAD_P + lo:PAD_P + rows, cols] + src[PAD_P + lo - shift:PAD_P + rows - shift, cols]
        if last:
            return val
        dst[PAD_P + lo:PAD_P + rows, cols] = val
        src, dst, shift = dst, src, 2 * shift


def _leading_sums(buf_a, buf_b, cols, win, rows, n_out):
    src, dst, shift = buf_a, buf_b, 1
    while True:
        last = 2 * shift >= win
        hi = n_out if last else rows
        val = src[0:hi, cols] + src[shift:hi + shift, cols]
        if last:
            return val
        dst[0:hi, cols] = val
        src, dst, shift = dst, src, 2 * shift


def _pool_forward_tile(p_ref, halo_ref, first, tile_index, cw_ref, cb_ref, cs_ref, vbuf, vtmp, c, gc, ts):
    vbuf[PAD_P:PAD_P + HALO_P, :] = halo_ref[...] * jnp.where(first, 0.0, 1.0)
    vbuf[PAD_P + HALO_P:PAD_P + HALO_P + ts, :] = p_ref[:, 0:c]
    pos = tile_index * ts + lax.broadcasted_iota(jnp.int32, (ts, 1), 0) + 1
    pooled, inv, gout = [], [], []
    for g, win in enumerate(POOL_WINDOWS):
        cols = slice(g * gc, (g + 1) * gc)
        acc = _trailing_sums(vbuf, vtmp, cols, win, HALO_P + ts, HALO_P)
        inv_g = 1.0 / jnp.minimum(pos, win).astype(F32)
        pooled_g = (acc * inv_g - p_ref[:, cols]).astype(BF16)
        pooled.append(pooled_g)
        inv.append(inv_g)
        gout.append(jnp.dot(pooled_g, cw_ref[g], preferred_element_type=F32) + cb_ref[:, cols])
    return pooled, inv, gout


def _odd_mixer_fwd(p, cw, cb, cs, name):
    s = p.shape[0]
    c = p.shape[1] // 2
    gc = c // N_GROUPS
    ts = _row_tile(s, 256)

    def body(p_ref, halo_ref, cw_ref, cb_ref, cs_ref, u_ref, pooled_ref, vbuf, vtmp):
        i = pl.program_id(0)

        @pl.when(i == 0)
        def _():
            vbuf[0:PAD_P, :] = jnp.zeros((PAD_P, c), F32)
            vtmp[0:PAD_P, :] = jnp.zeros((PAD_P, c), F32)

        pooled, _, gout = _pool_forward_tile(p_ref, halo_ref, i == 0, i, cw_ref, cb_ref, cs_ref, vbuf, vtmp, c, gc, ts)
        for g in range(N_GROUPS):
            cols = slice(g * gc, (g + 1) * gc)
            z = p_ref[:, c + g * gc:c + (g + 1) * gc]
            u_ref[:, cols] = (gout[g] * cs_ref[:, cols] * (z * _sigmoid(z))).astype(BF16)
            pooled_ref[:, cols] = pooled[g]

    return pl.pallas_call(
        body, name=name, grid=(s // ts,),
        in_specs=[pl.BlockSpec((ts, 2 * c), lambda i: (i, 0)),
                  pl.BlockSpec((HALO_P, c), lambda i: (jnp.maximum(i * (ts // HALO_P) - 1, 0), 0)),
                  pl.BlockSpec((N_GROUPS, gc, gc), lambda i: (0, 0, 0)),
                  pl.BlockSpec((1, c), lambda i: (0, 0)), pl.BlockSpec((1, c), lambda i: (0, 0))],
        out_specs=[pl.BlockSpec((ts, c), lambda i: (i, 0))] * 2,
        out_shape=[jax.ShapeDtypeStruct((s, c), BF16)] * 2,
        scratch_shapes=[pltpu.VMEM((PAD_P + HALO_P + ts, c), F32)] * 2,
        compiler_params=_params("arbitrary"),
    )(p, p, cw, cb, cs)


def _odd_mixer_bwd(p, pooled, du, cw, cb, cs, name):
    s = p.shape[0]
    c = p.shape[1] // 2
    gc = c // N_GROUPS
    ts = _row_tile(s, 256)
    nt = s // ts

    def body(z_ref, pooled_ref, du_ref, cw_ref, cb_ref, cs_ref, dp_ref, dcw_ref, dcb_ref, dcs_ref,
             ebuf, etmp, carry_e):
        step = pl.program_id(0)
        tile_index = nt - 1 - step

        @pl.when(step == 0)
        def _():
            for ref in (dcw_ref, dcb_ref, dcs_ref, carry_e):
                ref[...] = jnp.zeros_like(ref)
            for ref in (ebuf, etmp):
                ref[ts + HALO_P:ts + HALO_P + PAD_P, :] = jnp.zeros((PAD_P, c), F32)

        pos = tile_index * ts + lax.broadcasted_iota(jnp.int32, (ts, 1), 0) + 1
        ebuf[ts:ts + HALO_P, :] = carry_e[...]
        for g, win in enumerate(POOL_WINDOWS):
            cols = slice(g * gc, (g + 1) * gc)
            pooled_g = pooled_ref[:, cols]
            gout_g = jnp.dot(pooled_g, cw_ref[g], preferred_element_type=F32) + cb_ref[:, cols]
            z = z_ref[:, cols]
            sz = _sigmoid(z)
            du_g = du_ref[:, cols]
            scale = cs_ref[:, cols]
            d_y = du_g * (z * sz)
            dp_ref[:, c + g * gc:c + (g + 1) * gc] = (du_g * (gout_g * scale) * _dsilu(z, sz)).astype(BF16)
            dcs_ref[:, cols] += _sum_rows(d_y * gout_g)
            d_gout = d_y * scale
            dcb_ref[:, cols] += _sum_rows(d_gout)
            d_gout_b = d_gout.astype(BF16)
            dcw_ref[g] += lax.dot_general(pooled_g, d_gout_b, (((0,), (0,)), ((), ())), preferred_element_type=F32)
            d_pool = lax.dot_general(d_gout_b, cw_ref[g], (((1,), (1,)), ((), ())), preferred_element_type=F32)
            e = d_pool * (1.0 / jnp.minimum(pos, win).astype(F32))
            ebuf[0:ts, cols] = e
            carry_e[:, cols] = e[0:HALO_P, :]
            d_v = _leading_sums(ebuf, etmp, cols, win, ts + HALO_P, ts) - d_pool
            dp_ref[:, cols] = d_v.astype(BF16)

    rev = lambda i: (nt - 1 - i, 0)
    small = [(N_GROUPS, gc, gc), (1, c), (1, c)]
    return pl.pallas_call(
        body, name=name, grid=(nt,),
        in_specs=[pl.BlockSpec((ts, c), lambda i: (nt - 1 - i, 1)),
                  pl.BlockSpec((ts, c), rev),
                  pl.BlockSpec((ts, c), rev),
                  pl.BlockSpec((N_GROUPS, gc, gc), lambda i: (0, 0, 0)),
                  pl.BlockSpec((1, c), lambda i: (0, 0)), pl.BlockSpec((1, c), lambda i: (0, 0))],
        out_specs=[pl.BlockSpec((ts, 2 * c), rev),
                   pl.BlockSpec((N_GROUPS, gc, gc), lambda i: (0, 0, 0)),
                   pl.BlockSpec((1, c), lambda i: (0, 0)), pl.BlockSpec((1, c), lambda i: (0, 0))],
        out_shape=[jax.ShapeDtypeStruct((s, 2 * c), BF16)] + [jax.ShapeDtypeStruct(sh, F32) for sh in small],
        scratch_shapes=[pltpu.VMEM((ts + HALO_P + PAD_P, c), F32)] * 2 + [pltpu.VMEM((HALO_P, c), F32)],
        compiler_params=_params("arbitrary"),
    )(p, pooled, du, cw, cb, cs)


def _cast_into_slot(a, coords, name):
    r, cols = a.shape
    tr = _row_tile(r // 2, 1024)
    per = r // 2 // tr

    def body(co_ref, a_ref, o_ref):
        o_ref[0, 0] = a_ref[...].astype(BF16)

    return pl.pallas_call(
        body, name=name,
        grid_spec=pltpu.PrefetchScalarGridSpec(
            num_scalar_prefetch=1, grid=(2, per),
            in_specs=[pl.BlockSpec((tr, cols), lambda h, i, co: (h * per + i, 0))],
            out_specs=pl.BlockSpec((1, 1, tr, cols), lambda h, i, co: (co[0], h, i, 0))),
        out_shape=jax.ShapeDtypeStruct((N_SHARDS, 2, r // 2, cols), BF16),
        compiler_params=_params("arbitrary", "arbitrary"),
    )(coords, a)


def _chip_sum(g, other, coords, name):
    n_sh, _, r2, cols = g.shape
    tr = _row_tile(r2, 512)

    def body(co_ref, g_ref, o_ref, sum_ref, mine_ref):
        v = (g_ref[0, 0].astype(F32) + o_ref[0].astype(F32)).astype(BF16)
        sum_ref[0] = v

        @pl.when(pl.program_id(1) == co_ref[0])
        def _():
            mine_ref[0] = v

    piece = pl.BlockSpec((1, tr, cols), lambda i, s, co: (s, i, 0))
    return pl.pallas_call(
        body, name=name,
        grid_spec=pltpu.PrefetchScalarGridSpec(
            num_scalar_prefetch=1, grid=(r2 // tr, n_sh),
            in_specs=[pl.BlockSpec((1, 1, tr, cols), lambda i, s, co: (s, co[1], i, 0)), piece],
            out_specs=[piece, pl.BlockSpec((1, tr, cols), lambda i, s, co: (co[0], i, 0))]),
        out_shape=[jax.ShapeDtypeStruct((n_sh, r2, cols), BF16)] * 2,
        compiler_params=_params("arbitrary", "arbitrary"),
    )(coords, g, other)


def _shard_sum(pieces, coords, name):
    n_sh, r2, cols = pieces.shape
    tr = _row_tile(r2, 512)

    def body(co_ref, p_ref, o_ref):
        acc = p_ref[0].astype(F32)
        for k in range(1, n_sh):
            acc = acc + p_ref[k].astype(F32)
        o_ref[0] = acc

    return pl.pallas_call(
        body, name=name,
        grid_spec=pltpu.PrefetchScalarGridSpec(
            num_scalar_prefetch=1, grid=(r2 // tr,),
            in_specs=[pl.BlockSpec((n_sh, tr, cols), lambda i, co: (0, i, 0))],
            out_specs=pl.BlockSpec((1, tr, cols), lambda i, co: (co[1], i, 0))),
        out_shape=jax.ShapeDtypeStruct((2, r2, cols), F32),
        compiler_params=_params("arbitrary"),
    )(coords, pieces)


def _sum_small(a, name):
    n, r, cols = a.shape

    def body(a_ref, o_ref):
        acc = a_ref[0]
        for k in range(1, n):
            acc = acc + a_ref[k]
        o_ref[...] = acc

    return pl.pallas_call(
        body, name=name,
        in_specs=[pl.BlockSpec((n, r, cols), lambda: (0, 0, 0))],
        out_specs=pl.BlockSpec((r, cols), lambda: (0, 0)),
        out_shape=jax.ShapeDtypeStruct((r, cols), F32),
        compiler_params=_params(),
    )(a)


def _adamw_step(w, g, m, v):
    m = ADAM_B1 * m + (1.0 - ADAM_B1) * g
    v = ADAM_B2 * v + (1.0 - ADAM_B2) * (g * g)
    m_hat = m / (1.0 - ADAM_B1 ** ADAM_STEP)
    v_hat = v / (1.0 - ADAM_B2 ** ADAM_STEP)
    return -ADAM_LR * (m_hat / (jnp.sqrt(v_hat) + ADAM_EPS) + ADAM_WD * w), m, v


def _adamw(w, g, m, v, name):
    r, cols = w.shape
    tr = _row_tile(r, 256) if r % SUBLANES_BF16 == 0 else r

    def body(w_ref, g_ref, m_ref, v_ref, d_ref, nm_ref, nv_ref):
        d_ref[...], nm_ref[...], nv_ref[...] = _adamw_step(w_ref[...], g_ref[...], m_ref[...], v_ref[...])

    blk = pl.BlockSpec((tr, cols), lambda i: (i, 0))
    return pl.pallas_call(
        body, name=name, grid=(r // tr,),
        in_specs=[blk] * 4, out_specs=[blk] * 3,
        out_shape=[jax.ShapeDtypeStruct((r, cols), F32)] * 3,
        compiler_params=_params("arbitrary"),
    )(w, g, m, v)


def _adamw_many(ws, gs, ms, vs, name):
    n = len(ws)

    def body(*refs):
        w_refs, g_refs, m_refs, v_refs = (refs[t * n:(t + 1) * n] for t in range(4))
        d_refs, nm_refs, nv_refs = (refs[(4 + t) * n:(5 + t) * n] for t in range(3))
        for k in range(n):
            d_refs[k][...], nm_refs[k][...], nv_refs[k][...] = _adamw_step(
                w_refs[k][...], g_refs[k][...], m_refs[k][...], v_refs[k][...])

    specs = [pl.BlockSpec(a.shape, lambda: (0, 0)) for a in ws]
    out = pl.pallas_call(
        body, name=name, in_specs=specs * 4, out_specs=specs * 3,
        out_shape=[jax.ShapeDtypeStruct(a.shape, F32) for a in ws] * 3,
        compiler_params=_params(),
    )(*ws, *gs, *ms, *vs)
    return out[:n], out[n:2 * n], out[2 * n:]


def _place():
    x, y, c = lax.axis_index("x"), lax.axis_index("y"), lax.axis_index("c")
    other_chips = [(1 - x, y), (x, 1 - y), (1 - x, 1 - y)]
    return x, y, c, other_chips


def _chip(xy):
    return 2 * xy[0] + xy[1]


def _remote(src, dst, send_sem, recv_sem, to):
    return pltpu.make_async_remote_copy(src_ref=src, dst_ref=dst, send_sem=send_sem, recv_sem=recv_sem,
                                        device_id=to, device_id_type=MESH)


def _gather_ici(ctx, k, j, start):
    (x, y, c, chips), b, send, recv = ctx
    if j < 2:
        chip, to = (_chip((x, y)) if start else _chip(chips[j])), (*chips[j], c)
    else:
        chip = 2 * (x ^ c) + (y ^ (1 - c)) if start else _chip(chips[2])
        to = (x ^ (1 - c), y ^ c, c)
    blk = b[k].at[chip, c]
    return _remote(blk, blk, send.at[6 * k + j], recv.at[6 * k + j], to)


def _gather_d2d(ctx, k, j, start):
    (x, y, c, chips), b, send, recv = ctx
    blk = b[k].at[_chip(chips[j]), c if start else 1 - c]
    return _remote(blk, blk, send.at[6 * k + 3 + j], recv.at[6 * k + 3 + j], (x, y, 1 - c))


def _gather_small(ctx, n, j, start):
    (x, y, c, chips), b, send, recv = ctx
    blk = b[n].at[_chip((x, y)) if start else _chip(chips[j])]
    return _remote(blk, blk, send.at[6 * n + j], recv.at[6 * n + j], (*chips[j], c))


def _gather_neighbours_landed(ctx, k):
    for j in range(2):
        _gather_ici(ctx, k, j, False).wait_recv()
    _gather_ici(ctx, k, 2, True).start()
    for j in range(2):
        _gather_d2d(ctx, k, j, True).start()


def _gather_diagonal_landed(ctx, k):
    _gather_ici(ctx, k, 2, False).wait_recv()
    _gather_d2d(ctx, k, 2, True).start()


def _gather_comm(bufs, relay_at, forward_at):
    n = len(bufs)

    def start(srcs, b, send, recv):
        for k in range(n):
            for j in range(2):
                _gather_ici((_place(), b, send, recv), k, j, True).start()

    def relay(srcs, b, send, recv):
        for k in range(n):
            _gather_neighbours_landed((_place(), b, send, recv), k)

    def forward(srcs, b, send, recv):
        for k in range(n):
            _gather_diagonal_landed((_place(), b, send, recv), k)

    def finish(srcs, b, send, recv):
        ctx = (_place(), b, send, recv)
        for k in range(n):
            for j in range(3):
                _gather_d2d(ctx, k, j, False).wait_recv()
                _gather_ici(ctx, k, j, True).wait_send()
                _gather_d2d(ctx, k, j, True).wait_send()

    return _Comm([], bufs, 6 * n, [(0, start), (relay_at, relay), (forward_at, forward)], finish)


def _rmsnorm(x, gain, name):
    s, d = x.shape
    tm = _row_tile(s, 512)

    def body(x_ref, g_ref, h_ref):
        xv = x_ref[...]
        r = lax.rsqrt(_mean_last(xv * xv) + EPS)
        h_ref[...] = (xv * r * g_ref[...]).astype(BF16)

    return pl.pallas_call(
        body, name=name, grid=(s // tm,),
        in_specs=[pl.BlockSpec((tm, d), lambda i: (i, 0)), pl.BlockSpec((1, d), lambda i: (0, 0))],
        out_specs=pl.BlockSpec((tm, d), lambda i: (i, 0)),
        out_shape=jax.ShapeDtypeStruct((s, d), BF16),
        compiler_params=_params("arbitrary"),
    )(x, gain)


def _gathered_in_proj(h, bufs, small, order, name):
    s, d = h.shape
    n_sh, _, r2, ns = bufs[0].shape
    assert d == 2 * r2
    n = len(bufs)
    tm = _row_tile(s, 512)
    n_i = s // tm
    hook_i = max(n_i - 2, 0)
    n_sems = 6 * n + 3

    def body(order_ref, h_ref, *rest):
        p_ref = rest[n + 1]
        b = rest[n + 2:2 * n + 3]
        w_vmem, w_sems, send, recv = rest[2 * n + 3:]
        j, i = pl.program_id(0), pl.program_id(1)
        ctx = (_place(), b, send, recv)

        def fetch(q):
            return pltpu.make_async_copy(b[0].at[order_ref[q]], w_vmem.at[q % 2], w_sems.at[q % 2])

        @pl.when((j == 0) & (i == 0))
        def _():
            for k in range(n):
                for peer in range(2):
                    _gather_ici(ctx, k, peer, True).start()
            for peer in range(3):
                _gather_small(ctx, n, peer, True).start()
            fetch(0).start()
            fetch(0).wait()

        for q in range(1, n_sh):
            @pl.when((j == q - 1) & (i == hook_i))
            def _(q=q):
                if q == 1:
                    _gather_neighbours_landed(ctx, 0)
                if q == 2:
                    for k in range(1, n):
                        _gather_neighbours_landed(ctx, k)
                if q == 3:
                    for k in range(n):
                        _gather_diagonal_landed(ctx, k)
                _gather_d2d(ctx, 0, q - 1, False).wait_recv()
                fetch(q).start()

            @pl.when((j == q) & (i == 0))
            def _(q=q):
                fetch(q).wait()

        wv = w_vmem.at[j % 2]
        p_ref[...] = (jnp.dot(h_ref[:, 0:r2], wv[0], preferred_element_type=F32)
                      + jnp.dot(h_ref[:, r2:d], wv[1], preferred_element_type=F32))

        @pl.when((j == n_sh - 1) & (i == n_i - 1))
        def _():
            for peer in range(3):
                _gather_small(ctx, n, peer, False).wait_recv()
                _gather_small(ctx, n, peer, True).wait_send()
            for k in range(n):
                for peer in range(3):
                    if k > 0:
                        _gather_d2d(ctx, k, peer, False).wait_recv()
                    _gather_ici(ctx, k, peer, True).wait_send()
                    _gather_d2d(ctx, k, peer, True).wait_send()

    all_bufs = list(bufs) + [small]
    out = pl.pallas_call(
        body, name=name,
        grid_spec=pltpu.PrefetchScalarGridSpec(
            num_scalar_prefetch=1, grid=(n_sh, n_i),
            in_specs=[pl.BlockSpec((tm, d), lambda j, i, o: (i, 0))] + [ANY] * (n + 1),
            out_specs=[pl.BlockSpec((tm, ns), lambda j, i, o: (i, o[j]))] + [ANY] * (n + 1),
            scratch_shapes=[pltpu.VMEM((2, 2, r2, ns), BF16), pltpu.SemaphoreType.DMA((2,)),
                            pltpu.SemaphoreType.DMA((n_sems,)), pltpu.SemaphoreType.DMA((n_sems,))]),
        out_shape=[jax.ShapeDtypeStruct((s, n_sh * ns), F32)]
        + [jax.ShapeDtypeStruct(a.shape, a.dtype) for a in all_bufs],
        input_output_aliases={2 + t: 1 + t for t in range(n + 1)},
        compiler_params=_params("arbitrary", "arbitrary"),
    )(order, h, *all_bufs)
    return out[0], list(out[1:])


def _exchange_comm(grads):
    n = len(grads)
    landing = [lax.empty((N_SHARDS,) + a.shape[2:], a.dtype) for a in grads]

    def copies(srcs, b, send, recv):
        x, y, c, _ = _place()
        return [_remote(srcs[k].at[s, 1 - c], b[k].at[s], send.at[N_SHARDS * k + s], recv.at[N_SHARDS * k + s],
                        (x, y, 1 - c)) for k in range(n) for s in range(N_SHARDS)]

    def start(srcs, b, send, recv):
        for cp in copies(srcs, b, send, recv):
            cp.start()

    def finish(srcs, b, send, recv):
        for cp in copies(srcs, b, send, recv):
            cp.wait()

    return _Comm(grads, landing, N_SHARDS * n, [(0, start)], finish)


def _scatter_comm(chip_sums, landing):
    n = len(chip_sums)

    def big(srcs, b, send, recv, k, j, start):
        x, y, c, chips = _place()
        dst = b[k].at[_chip((x, y)) if start else _chip(chips[j])]
        return _remote(srcs[k].at[_chip(chips[j])], dst, send.at[3 * k + j], recv.at[3 * k + j], (*chips[j], c))

    def start(srcs, b, send, recv):
        for k in range(n):
            for j in range(3):
                big(srcs, b, send, recv, k, j, True).start()

    def finish(srcs, b, send, recv):
        for k in range(n):
            for j in range(3):
                big(srcs, b, send, recv, k, j, False).wait_recv()
                big(srcs, b, send, recv, k, j, True).wait_send()

    return _Comm(chip_sums, landing, 3 * n, [(0, start)], finish)


def _join_comm(halves, small):
    n = len(halves)
    flips = [(fx, fy, fc) for fx in (0, 1) for fy in (0, 1) for fc in (0, 1)][1:]

    def half(b, send, recv, k, start):
        x, y, c, _ = _place()
        return _remote(b[k].at[c], b[k].at[c if start else 1 - c], send.at[k], recv.at[k], (x, y, 1 - c))

    def small_copy(b, send, recv, q, start):
        x, y, c, _ = _place()
        px, py, pc = x ^ flips[q][0], y ^ flips[q][1], c ^ flips[q][2]
        blk = b[n].at[4 * x + 2 * y + c if start else 4 * px + 2 * py + pc]
        return _remote(blk, blk, send.at[n + q], recv.at[n + q], (px, py, pc))

    def start(srcs, b, send, recv):
        for q in range(len(flips)):
            small_copy(b, send, recv, q, True).start()
        for k in range(n):
            half(b, send, recv, k, True).start()

    def finish(srcs, b, send, recv):
        for q in range(len(flips)):
            small_copy(b, send, recv, q, False).wait()
        for k in range(n):
            half(b, send, recv, k, False).wait()

    return _Comm([], list(halves) + [small], n + len(flips), [(0, start)], finish)


def _flat_rows(parts):
    flat = jnp.concatenate([p.reshape(-1) for p in parts])
    assert flat.shape[0] % LANES == 0
    return flat.reshape(-1, LANES)


def _unflatten(flat, shapes):
    out, at = [], 0
    for sh in shapes:
        size = 1
        for dim in sh:
            size *= dim
        out.append(flat[at:at + size].reshape(sh))
        at += size
    assert at == flat.shape[0], (at, flat.shape)
    return out


def _col_shards_to_full(a, rows):
    q = a.shape[1] // rows
    return a.reshape(N_SHARDS, rows, q).transpose(1, 0, 2).reshape(rows, N_SHARDS * q)


def _my_col_shard(full, chip):
    rows, cols = full.shape
    q = cols // N_SHARDS
    return lax.dynamic_index_in_dim(full.reshape(rows, N_SHARDS, q), chip, axis=1, keepdims=False)


def kernel(x, e_norm_pre, e_norm_post, e_w_in, e_a_conv, e_b_conv, e_b_conv_bias, e_b_ln_g, e_b_ln_b, e_w_out, o_norm_pre, o_norm_post, o_w_in, o_c_w, o_c_b, o_c_scale, o_w_out, loss_target, m_e_norm_pre, m_e_norm_post, m_e_w_in, m_e_a_conv, m_e_b_conv, m_e_b_conv_bias, m_e_b_ln_g, m_e_b_ln_b, m_e_w_out, m_o_norm_pre, m_o_norm_post, m_o_w_in, m_o_c_w, m_o_c_b, m_o_c_scale, m_o_w_out, v_e_norm_pre, v_e_norm_post, v_e_w_in, v_e_a_conv, v_e_b_conv, v_e_b_conv_bias, v_e_b_ln_g, v_e_b_ln_b, v_e_w_out, v_o_norm_pre, v_o_norm_post, v_o_w_in, v_o_c_w, v_o_c_b, v_o_c_scale, v_o_w_out):
    _, s, d = x.shape
    w = d // 2
    c = d
    gc = c // N_GROUPS
    wq, cq, gq = w // N_SHARDS, c // N_SHARDS, gc // N_SHARDS
    chip = 2 * lax.axis_index("x") + lax.axis_index("y")
    core = lax.axis_index("c")
    x2 = x.reshape(s, d)
    target = loss_target.reshape(s, d)

    big_w = [e_w_in[0], e_w_out[0], o_w_in[0], o_c_w[0].reshape(N_GROUPS * gq, gc), o_w_out[0]]
    big_m = [m_e_w_in[0], m_e_w_out[0], m_o_w_in[0], m_o_c_w[0].reshape(N_GROUPS * gq, gc), m_o_w_out[0]]
    big_v = [v_e_w_in[0], v_e_w_out[0], v_o_w_in[0], v_o_c_w[0].reshape(N_GROUPS * gq, gc), v_o_w_out[0]]
    coords = jnp.stack([chip, core]).astype(jnp.int32)
    slots = [_cast_into_slot(a, coords, "cast_%d" % k) for k, a in enumerate(big_w)]
    sharded_small = _flat_rows([e_a_conv[0], e_b_conv[0], o_norm_pre, o_norm_post, o_c_scale, o_c_b[0]])
    small_slots = lax.dynamic_update_index_in_dim(jnp.zeros((N_SHARDS,) + sharded_small.shape, F32), sharded_small,
                                                  chip, 0)
    xi, yi = lax.axis_index("x"), lax.axis_index("y")
    order = jnp.stack([chip, 2 * (1 - xi) + yi, 2 * xi + (1 - yi), 2 * (1 - xi) + (1 - yi)]).astype(jnp.int32)
    h0 = _rmsnorm(x2, e_norm_pre, "e_pre_norm")
    p0, (e_w_in_g, e_w_out_g, small_g4) = _gathered_in_proj(h0, slots[:2], small_slots, order, "e_in_proj")
    e_w_in_sm = e_w_in_g.reshape((N_SHARDS,) + big_w[0].shape)
    e_w_out_f = e_w_out_g.reshape(w + w, d)
    sm = small_g4.reshape(N_SHARDS, -1)
    at = [0]

    def take(rows, q):
        blk = sm[:, at[0]:at[0] + rows * q]
        at[0] += rows * q
        return _col_shards_to_full(blk, rows)

    a_conv_f = take(CONV_A, wq)
    b_conv_f = take(CONV_B, wq)
    o_pre_f = take(1, cq)
    o_post_f = take(1, cq)
    cs_f = take(1, cq)
    cb_f = take(N_GROUPS, gq).reshape(1, c)

    mixer_steps = s // _row_tile(s, 128)
    (u0, x1, y0, conv0), odd_g = _even_mixer_fwd(
        p0, e_w_out_f, x2, e_norm_post, a_conv_f, b_conv_f, e_b_conv_bias, e_b_ln_g, e_b_ln_b, "e_mixer_out_proj",
        comm=_gather_comm(slots[2:], mixer_steps // 2, (25 * mixer_steps) // 32))
    o_w_in_sm = odd_g[0].reshape((N_SHARDS,) + big_w[2].shape)
    cw_f = odd_g[1].reshape(N_SHARDS, N_GROUPS, gq, gc).transpose(1, 0, 2, 3).reshape(N_GROUPS, gc, gc)
    o_w_out_f = odd_g[2].reshape(c, d)
    p1, h1 = _norm_matmul(x1, o_pre_f, o_w_in_sm, "o_in_proj")
    u1, pooled1 = _odd_mixer_fwd(p1, cw_f, cb_f, cs_f, "o_mixer_fwd")
    d_y1, d_x2, d_o_post, loss_part = _matmul_post_loss(u1, o_w_out_f, x1, o_post_f, target, "o_out_proj_loss")

    def as_pieces(g, k):
        return g.reshape(N_SHARDS, 2, big_w[k].shape[0] // 2, big_w[k].shape[1])

    def chip_sums(ks, pieces, from_sibling):
        both = [_chip_sum(g, o, coords, "chip_sum_%d" % k) for k, g, o in zip(ks, pieces, from_sibling)]
        return [b[0] for b in both], [b[1] for b in both]

    g_o_w_out = _matmul_tn(u1, d_y1, 1, "o_w_out_grad")
    d_u1, _ = _matmul_nt(d_y1, o_w_out_f[None], "o_out_proj_bwd")
    d_p1, d_cw, d_cb, d_cs = _odd_mixer_bwd(p1, pooled1, d_u1, cw_f, cb_f, cs_f, "o_mixer_bwd")
    g_o_w_in = _matmul_tn(h1, d_p1, N_SHARDS, "o_w_in_grad")
    g_cw = d_cw.reshape(N_GROUPS, N_SHARDS, gq, gc).transpose(1, 0, 2, 3).astype(BF16)
    pieces_o = [as_pieces(g_o_w_in, 2), as_pieces(g_cw, 3), as_pieces(g_o_w_out, 4)]
    d_h1, sibling_o = _matmul_nt(d_p1, o_w_in_sm, "o_in_proj_bwd", comm=_exchange_comm(pieces_o))
    d_x1, d_o_pre, d_y0, d_e_post = _norm_bwd(d_h1, x1, o_pre_f, d_x2, "o_pre_norm_bwd", post=(y0, e_norm_post))

    pieces_e = [as_pieces(_matmul_tn(u0, d_y0, 1, "e_w_out_grad"), 1)]
    d_u0, sibling_e = _matmul_nt(d_y0, e_w_out_f[None], "e_out_proj_bwd", comm=_exchange_comm(pieces_e))
    sums_a, landing_a = chip_sums([1, 2, 3, 4], pieces_e + pieces_o, sibling_e + sibling_o)
    (d_p0, d_a_conv, d_b_conv, d_bias, d_ln_g, d_ln_b), landed_a = _even_mixer_bwd(
        p0, d_u0, conv0, a_conv_f, b_conv_f, e_b_ln_g, e_b_ln_b, "e_mixer_bwd",
        comm=_scatter_comm(sums_a, landing_a))
    pieces_b = [as_pieces(_matmul_tn(h0, d_p0, N_SHARDS, "e_w_in_grad"), 0)]
    sums_b, landing_b = chip_sums([0], pieces_b, _comm_only(_exchange_comm(pieces_b), "exchange_core_halves"))
    d_h0, landed_b = _matmul_nt(d_p0, e_w_in_sm, "e_in_proj_bwd", comm=_scatter_comm(sums_b, landing_b))
    grad_x, d_e_pre = _norm_bwd(d_h0, x2, e_norm_pre, d_x1, "e_pre_norm_bwd")

    landed = landed_b + landed_a
    reduced = [_shard_sum(sc, coords, "shard_sum_%d" % k) for k, sc in enumerate(landed)]
    small_parts = _flat_rows([loss_part[0], d_e_pre, d_e_post, d_bias, d_ln_g, d_ln_b, d_a_conv, d_b_conv,
                              d_o_pre, d_o_post, d_cs, d_cb])
    small_rows = lax.dynamic_update_index_in_dim(jnp.zeros((N_DEVICES,) + small_parts.shape, F32), small_parts,
                                                 2 * chip + core, 0)
    joined = _comm_only(_join_comm(reduced, small_rows), "join_core_halves")
    big_g = [j.reshape(a.shape) for j, a in zip(joined[:5], big_w)]
    small_sum = _sum_small(joined[5], "small_sum").reshape(-1)
    (loss_row, g_e_pre, g_e_post, g_bias, g_ln_g, g_ln_b, g_a_conv_f, g_b_conv_f, g_o_pre_f, g_o_post_f, g_cs_f,
     g_cb_f) = _unflatten(small_sum, [(LANES,), (1, d), (1, d), (1, w), (1, w), (1, w), (CONV_A, w), (CONV_B, w),
                                      (1, c), (1, c), (1, c), (1, c)])
    loss = loss_row[0]
    g_a_conv = _my_col_shard(g_a_conv_f, chip)
    g_b_conv = _my_col_shard(g_b_conv_f, chip)
    g_o_pre = _my_col_shard(g_o_pre_f, chip)
    g_o_post = _my_col_shard(g_o_post_f, chip)
    g_cs = _my_col_shard(g_cs_f, chip)
    g_cb = _my_col_shard(g_cb_f.reshape(N_GROUPS, gc), chip)

    big_upd = [_adamw(wt, g, m, v, "adamw_%d" % k) for k, (wt, g, m, v) in enumerate(zip(big_w, big_g, big_m, big_v))]
    small_w = [e_norm_pre, e_norm_post, e_b_conv_bias, e_b_ln_g, e_b_ln_b, e_a_conv[0], e_b_conv[0],
               o_norm_pre, o_norm_post, o_c_b[0], o_c_scale]
    small_m = [m_e_norm_pre, m_e_norm_post, m_e_b_conv_bias, m_e_b_ln_g, m_e_b_ln_b, m_e_a_conv[0], m_e_b_conv[0],
               m_o_norm_pre, m_o_norm_post, m_o_c_b[0], m_o_c_scale]
    small_v = [v_e_norm_pre, v_e_norm_post, v_e_b_conv_bias, v_e_b_ln_g, v_e_b_ln_b, v_e_a_conv[0], v_e_b_conv[0],
               v_o_norm_pre, v_o_norm_post, v_o_c_b[0], v_o_c_scale]
    small_g = [g_e_pre, g_e_post, g_bias, g_ln_g, g_ln_b, g_a_conv, g_b_conv, g_o_pre, g_o_post, g_cb, g_cs]
    small_delta, small_new_m, small_new_v = _adamw_many(small_w, small_g, small_m, small_v, "adamw_small")

    def ordered(small, big):
        (n_pre, n_post, bias, ln_g, ln_b, a_conv, b_conv, o_pre, o_post, cb, cs) = small
        (w_in, w_out, ow_in, cw, ow_out) = big
        return [n_pre, n_post, w_in[None], a_conv[None], b_conv[None], bias, ln_g, ln_b, w_out[None], o_pre, o_post,
                ow_in[None], cw.reshape(1, N_GROUPS, gq, gc), cb[None], cs, ow_out[None]]

    grads = ordered(small_g, big_g)
    deltas = ordered(small_delta, [u[0] for u in big_upd])
    new_m = ordered(small_new_m, [u[1] for u in big_upd])
    new_v = ordered(small_new_v, [u[2] for u in big_upd])
    return (loss, grad_x.reshape(1, s, d), *grads, *deltas, *new_m, *new_v)
```

```python
import functools

import jax
import jax.numpy as jnp
from jax import lax
from jax.experimental import pallas as pl
from jax.experimental.pallas import tpu as pltpu

F32 = jnp.float32
BF16 = jnp.bfloat16
MESH = pl.DeviceIdType.MESH

EPS = 1e-6
CONV_A = 3
CONV_B = 31
POOL_WINDOWS = (2, 4, 8, 16)
N_GROUPS = len(POOL_WINDOWS)
N_SHARDS = 4
N_DEVICES = 8
ADAM_LR = 0.001
ADAM_B1 = 0.9
ADAM_B2 = 0.999
ADAM_EPS = 1e-08
ADAM_WD = 0.01
ADAM_STEP = 10

LANES = 128
SUBLANES_BF16 = 16
HALO_A = 8
HALO_B = 32
HALO_P = 16
PAD_P = 8
SHIFTS = 8
ROW_BLOCK = 32
LANE_BLOCK = 256
LANE_PASS = 256
VMEM_LIMIT = 56 * 1024 * 1024
TN_ACC_BYTES = 8 * 1024 * 1024
EVEN_BWD_ROWS = 128


def _row_tile(n, pref):
    t = max(min(n, pref) // SUBLANES_BF16, 1) * SUBLANES_BF16
    while t > SUBLANES_BF16 and (n % t or t % SUBLANES_BF16):
        t -= SUBLANES_BF16
    assert n % t == 0, (n, pref)
    return t


def _col_chunk(n, pref):
    t = (min(n, pref) // LANES) * LANES
    while t > LANES and n % t:
        t -= LANES
    assert t >= LANES and n % t == 0, (n, pref)
    return t


def _params(*sem):
    return pltpu.CompilerParams(dimension_semantics=tuple(sem) if sem else None, vmem_limit_bytes=VMEM_LIMIT)


ANY = pl.BlockSpec(memory_space=pl.ANY)


class _Comm:
    def __init__(self, srcs, bufs, n_sems, phases, finish):
        self.srcs, self.bufs, self.n_sems, self.phases, self.finish = list(srcs), list(bufs), n_sems, phases, finish


def _call(body, *, name, grid, in_specs, out_specs, out_shape, args, scratch_shapes=(), comm=None):
    params = _params(*(("arbitrary",) * len(grid)))
    if comm is None:
        out = pl.pallas_call(body, name=name, grid=grid, in_specs=in_specs, out_specs=out_specs, out_shape=out_shape,
                             scratch_shapes=scratch_shapes, compiler_params=params)(*args)
        return list(out), []
    n_in, n_out, n_scr = len(in_specs), len(out_specs), len(scratch_shapes)
    ns, nb = len(comm.srcs), len(comm.bufs)
    total = 1
    for size in grid:
        total *= size

    def fused(*refs):
        ins, srcs = refs[:n_in], refs[n_in:n_in + ns]
        at = n_in + ns + nb
        outs, bufs = refs[at:at + n_out], refs[at + n_out:at + n_out + nb]
        scratch = refs[at + n_out + nb:at + n_out + nb + n_scr]
        send_sems, recv_sems = refs[-2:]
        step = 0
        for axis, size in enumerate(grid):
            step = step * size + pl.program_id(axis)
        for when, fn in comm.phases:
            pl.when(step == when)(functools.partial(fn, srcs, bufs, send_sems, recv_sems))
        body(*ins, *outs, *scratch)
        pl.when(step == total - 1)(functools.partial(comm.finish, srcs, bufs, send_sems, recv_sems))

    out = pl.pallas_call(
        fused, name=name, grid=grid,
        in_specs=list(in_specs) + [ANY] * (ns + nb), out_specs=list(out_specs) + [ANY] * nb,
        out_shape=list(out_shape) + [jax.ShapeDtypeStruct(b.shape, b.dtype) for b in comm.bufs],
        input_output_aliases={n_in + ns + i: n_out + i for i in range(nb)},
        scratch_shapes=list(scratch_shapes) + [pltpu.SemaphoreType.DMA((comm.n_sems,))] * 2,
        compiler_params=params,
    )(*args, *comm.srcs, *comm.bufs)
    return list(out[:n_out]), list(out[n_out:])


def _comm_only(comm, name):
    ns, nb = len(comm.srcs), len(comm.bufs)

    def body(*refs):
        srcs, bufs = refs[:ns], refs[ns + nb:ns + 2 * nb]
        send_sems, recv_sems = refs[-2:]
        for _, fn in comm.phases:
            fn(srcs, bufs, send_sems, recv_sems)
        comm.finish(srcs, bufs, send_sems, recv_sems)

    return pl.pallas_call(
        body, name=name, in_specs=[ANY] * (ns + nb), out_specs=[ANY] * nb,
        out_shape=[jax.ShapeDtypeStruct(b.shape, b.dtype) for b in comm.bufs],
        input_output_aliases={ns + i: i for i in range(nb)},
        scratch_shapes=[pltpu.SemaphoreType.DMA((comm.n_sems,))] * 2,
    )(*comm.srcs, *comm.bufs)


def _sigmoid(v):
    return jax.nn.sigmoid(v)


def _dsilu(v, s):
    return s * (1.0 + v * (1.0 - s))


def _mean_last(v):
    return jnp.mean(v, axis=-1, keepdims=True)


def _sum_rows(v):
    return jnp.sum(v, axis=0, keepdims=True)


def _norm_matmul(x, gain, w_sm, name):
    s, d = x.shape
    n_sh, _, ns = w_sm.shape
    tm = _row_tile(s, 1024)

    def body(x_ref, g_ref, w_ref, p_ref, h_ref):
        @pl.when(pl.program_id(1) == 0)
        def _():
            xv = x_ref[...]
            r = lax.rsqrt(_mean_last(xv * xv) + EPS)
            h_ref[...] = (xv * r * g_ref[...]).astype(BF16)

        p_ref[...] = jnp.dot(h_ref[...], w_ref[0], preferred_element_type=F32)

    return _call(
        body, name=name, grid=(s // tm, n_sh),
        in_specs=[pl.BlockSpec((tm, d), lambda i, j: (i, 0)),
                  pl.BlockSpec((1, d), lambda i, j: (0, 0)),
                  pl.BlockSpec((1, d, ns), lambda i, j: (j, 0, 0))],
        out_specs=[pl.BlockSpec((tm, ns), lambda i, j: (i, j)),
                   pl.BlockSpec((tm, d), lambda i, j: (i, 0))],
        out_shape=[jax.ShapeDtypeStruct((s, n_sh * ns), F32), jax.ShapeDtypeStruct((s, d), BF16)],
        args=(x, gain, w_sm))[0]


def _matmul_post_loss(u, w, x_res, gain, target, name):
    s, k = u.shape
    d = w.shape[1]
    tm = _row_tile(s, 512)
    half = tm // 2

    def body(u_ref, w_ref, x_ref, g_ref, t_ref, dy_ref, dout_ref, dg_ref, loss_ref):
        @pl.when(pl.program_id(0) == 0)
        def _():
            dg_ref[...] = jnp.zeros_like(dg_ref)
            loss_ref[...] = jnp.zeros_like(loss_ref)

        g = g_ref[...]
        for rows in (slice(0, half), slice(half, tm)):
            y = jnp.dot(u_ref[rows, :], w_ref[...], preferred_element_type=F32)
            r = lax.rsqrt(_mean_last(y * y) + EPS)
            n = y * r
            err = x_ref[rows, :] + n * g - t_ref[rows, :]
            loss_ref[...] += 0.5 * jnp.sum(_mean_last(err * err))
            dout = err * (1.0 / d)
            dout_ref[rows, :] = dout
            dg_ref[...] += _sum_rows(dout * n)
            dn = dout * g
            dy_ref[rows, :] = (r * (dn - n * _mean_last(dn * n))).astype(BF16)

    return pl.pallas_call(
        body, name=name, grid=(s // tm,),
        in_specs=[pl.BlockSpec((tm, k), lambda i: (i, 0)),
                  pl.BlockSpec((k, d), lambda i: (0, 0), pipeline_mode=pl.Buffered(1)),
                  pl.BlockSpec((tm, d), lambda i: (i, 0)),
                  pl.BlockSpec((1, d), lambda i: (0, 0)),
                  pl.BlockSpec((tm, d), lambda i: (i, 0))],
        out_specs=[pl.BlockSpec((tm, d), lambda i: (i, 0)),
                   pl.BlockSpec((tm, d), lambda i: (i, 0)),
                   pl.BlockSpec((1, d), lambda i: (0, 0)),
                   pl.BlockSpec((8, LANES), lambda i: (0, 0))],
        out_shape=[jax.ShapeDtypeStruct((s, d), BF16), jax.ShapeDtypeStruct((s, d), F32),
                   jax.ShapeDtypeStruct((1, d), F32), jax.ShapeDtypeStruct((8, LANES), F32)],
        compiler_params=_params("arbitrary"),
    )(u, w, x_res, gain, target)


def _matmul_nt(a, w_sm, name, comm=None):
    s, ncols = a.shape
    n_sh, r, ns = w_sm.shape
    assert ncols == n_sh * ns
    tm = _row_tile(s, 1024)
    nc = _col_chunk(ns, 1792)
    per = ns // nc
    steps = n_sh * per

    def body(a_ref, w_ref, o_ref):
        part = lax.dot_general(a_ref[...], w_ref[0], (((1,), (1,)), ((), ())), preferred_element_type=F32)

        @pl.when(pl.program_id(1) == 0)
        def _():
            o_ref[...] = part

        @pl.when(pl.program_id(1) > 0)
        def _():
            o_ref[...] += part

    out, bufs = _call(
        body, name=name, grid=(s // tm, steps),
        in_specs=[pl.BlockSpec((tm, nc), lambda i, j: (i, j)),
                  pl.BlockSpec((1, r, nc), lambda i, j: (j // per, 0, j % per))],
        out_specs=[pl.BlockSpec((tm, r), lambda i, j: (i, 0))],
        out_shape=[jax.ShapeDtypeStruct((s, r), F32)],
        args=(a, w_sm), comm=comm)
    return out[0], bufs


def _matmul_tn(a, b, n_sh, name):
    s, k = a.shape
    n = b.shape[1]
    ns = n // n_sh
    tk = _col_chunk(k, TN_ACC_BYTES // (4 * ns))
    ts = _row_tile(s, 2048)
    n_s = s // ts

    def body(a_ref, b_ref, o_ref, acc_ref):
        part = lax.dot_general(a_ref[...], b_ref[...], (((0,), (0,)), ((), ())), preferred_element_type=F32)

        @pl.when(pl.program_id(2) == 0)
        def _():
            acc_ref[...] = part

        @pl.when(pl.program_id(2) > 0)
        def _():
            acc_ref[...] += part

        @pl.when(pl.program_id(2) == n_s - 1)
        def _():
            o_ref[0] = acc_ref[...].astype(BF16)

    return pl.pallas_call(
        body, name=name, grid=(n_sh, k // tk, n_s),
        in_specs=[pl.BlockSpec((ts, tk), lambda j, i, t: (t, i)),
                  pl.BlockSpec((ts, ns), lambda j, i, t: (t, j))],
        out_specs=pl.BlockSpec((1, tk, ns), lambda j, i, t: (j, i, 0)),
        out_shape=jax.ShapeDtypeStruct((n_sh, k, ns), BF16),
        scratch_shapes=[pltpu.VMEM((tk, ns), F32)],
        compiler_params=_params("arbitrary", "arbitrary", "arbitrary"),
    )(a, b)


def _norm_bwd(dh, x, gain, dres, name, post=None):
    s, d = x.shape
    tm = _row_tile(s, 256)
    with_post = post is not None

    def rms_bwd(dout, v, g):
        r = lax.rsqrt(_mean_last(v * v) + EPS)
        n = v * r
        dn = dout * g
        return r * (dn - n * _mean_last(dn * n)), _sum_rows(dout * n)

    def body(*refs):
        if with_post:
            dh_ref, x_ref, g_ref, dres_ref, y_ref, gp_ref, dx_ref, dg_ref, dy_ref, dgp_ref = refs
        else:
            dh_ref, x_ref, g_ref, dres_ref, dx_ref, dg_ref = refs

        @pl.when(pl.program_id(0) == 0)
        def _():
            dg_ref[...] = jnp.zeros_like(dg_ref)
            if with_post:
                dgp_ref[...] = jnp.zeros_like(dgp_ref)

        dv, dg = rms_bwd(dh_ref[...], x_ref[...], g_ref[...])
        dx = dres_ref[...] + dv
        dx_ref[...] = dx
        dg_ref[...] += dg
        if with_post:
            dy, dgp = rms_bwd(dx, y_ref[...], gp_ref[...])
            dy_ref[...] = dy.astype(BF16)
            dgp_ref[...] += dgp

    row = pl.BlockSpec((tm, d), lambda i: (i, 0))
    vec = pl.BlockSpec((1, d), lambda i: (0, 0))
    in_specs = [row, row, vec, row]
    out_specs = [row, vec]
    out_shape = [jax.ShapeDtypeStruct((s, d), F32), jax.ShapeDtypeStruct((1, d), F32)]
    args = [dh, x, gain, dres]
    if with_post:
        in_specs += [row, vec]
        out_specs += [row, vec]
        out_shape += [jax.ShapeDtypeStruct((s, d), BF16), jax.ShapeDtypeStruct((1, d), F32)]
        args += list(post)
    return _call(body, name=name, grid=(s // tm,), in_specs=in_specs, out_specs=out_specs, out_shape=out_shape,
                 args=args)[0]


def _fill_shifted_down(sh, rows):
    for b in range(1, SHIFTS):
        sh[b, SHIFTS:rows, :] = sh[0, SHIFTS - b:rows - b, :]


def _fill_shifted_up(sh, rows):
    for b in range(1, SHIFTS):
        sh[b, 0:rows - SHIFTS, :] = sh[0, b:rows - SHIFTS + b, :]


def _for_blocks(ts, w, fn):
    lb = min(LANE_BLOCK, w)
    for l0 in range(0, w, lb):
        def rows(rb, carry, l0=l0):
            fn(pl.multiple_of(rb * ROW_BLOCK, ROW_BLOCK), slice(l0, l0 + lb))
            return carry

        lax.fori_loop(0, ts // ROW_BLOCK, rows, 0)


TAP_SPAN = SHIFTS * ((CONV_B - 1) // SHIFTS)
WINDOW = ROW_BLOCK + TAP_SPAN


def _taps_of(b):
    return [(a, SHIFTS * a + b) for a in range((CONV_B - 1 - b) // SHIFTS + 1)]


def _conv31(sh, base, step, wt_ref, bias_ref, out_ref, ts, w):
    low = min(0, step * (TAP_SPAN // SHIFTS))

    def block(r0, lanes):
        acc = [jnp.zeros((SHIFTS, lanes.stop - lanes.start), F32) for _ in range(ROW_BLOCK // SHIFTS)]
        for b in range(SHIFTS):
            window = sh[b, pl.ds(pl.multiple_of(r0 + (base + low), SHIFTS), WINDOW), lanes]
            for a, j in (_taps_of(b) if step > 0 else reversed(_taps_of(b))):
                at = step * a - low
                wt = wt_ref[CONV_B - 1 - j, :, lanes]
                acc = [v + wt * window[at + SHIFTS * r:at + SHIFTS * (r + 1), :] for r, v in enumerate(acc)]
        for r, v in enumerate(acc):
            if bias_ref is not None:
                v = v + bias_ref[:, lanes]
            out_ref[pl.ds(pl.multiple_of(r0 + SHIFTS * r, SHIFTS), SHIFTS), lanes] = v

    _for_blocks(ts, w, block)


def _conv31_weight_grad(d_sh, x_sh, wacc, ts, w):
    def block(r0, lanes):
        d = d_sh[0, pl.ds(r0, ROW_BLOCK), lanes]
        for b in range(SHIFTS):
            window = x_sh[b, pl.ds(pl.multiple_of(r0 + (HALO_B - TAP_SPAN), SHIFTS), WINDOW), lanes]
            for a, j in _taps_of(b):
                at = TAP_SPAN - SHIFTS * a
                prod = d * window[at:at + ROW_BLOCK, :]
                part = prod[0:SHIFTS, :]
                for q in range(1, ROW_BLOCK // SHIFTS):
                    part = part + prod[q * SHIFTS:(q + 1) * SHIFTS, :]
                wacc[CONV_B - 1 - j, :, lanes] += part

    _for_blocks(ts, w, block)


def _even_forward_tile(p_ref, halo_ref, first, a_conv_ref, b_conv_ref, bias_ref, lng_ref, lnb_ref, qbuf, ysh, y1buf,
                       wb, w, ts):
    @pl.when(pl.program_id(0) == 0)
    def _():
        for k in range(CONV_B):
            wb[k] = jnp.broadcast_to(b_conv_ref[k:k + 1, :], (SHIFTS, w))

    def col(ref, k, rows=slice(None)):
        return ref[rows, k * w:(k + 1) * w]

    a_x, a_b, a_c, a_z = col(p_ref, 0), col(p_ref, 1), col(p_ref, 2), col(p_ref, 3)
    b_val, b_gate, b_z = col(p_ref, 4), col(p_ref, 5), col(p_ref, 6)
    keep = jnp.where(first, 0.0, 1.0)

    rows_a = slice(HALO_B - HALO_A, HALO_B)
    qbuf[0:HALO_A, :] = col(halo_ref, 2, rows_a) * col(halo_ref, 0, rows_a) * keep
    qbuf[HALO_A:HALO_A + ts, :] = a_c * a_x
    cq = jnp.zeros((ts, w), F32)
    for j in range(CONV_A):
        cq = cq + a_conv_ref[CONV_A - 1 - j:CONV_A - j, :] * qbuf[HALO_A - j:HALO_A - j + ts, :]
    ya = a_b * cq

    ysh[0, 0:HALO_B, :] = col(halo_ref, 4) * _sigmoid(col(halo_ref, 5)) * keep
    ysh[0, HALO_B:HALO_B + ts, :] = b_val * _sigmoid(b_gate)
    _fill_shifted_down(ysh, HALO_B + ts)
    _conv31(ysh, HALO_B, -SHIFTS, wb, bias_ref, y1buf, ts, w)
    yb1 = y1buf[...]
    xc = yb1 - _mean_last(yb1)
    rstd = lax.rsqrt(_mean_last(xc * xc) + EPS)
    xhat = xc * rstd
    yb2 = xhat * lng_ref[...] + lnb_ref[...]
    return dict(a_x=a_x, a_b=a_b, a_c=a_c, a_z=a_z, b_val=b_val, b_gate=b_gate, b_z=b_z,
                cq=cq, ya=ya, rstd=rstd, xhat=xhat, yb2=yb2)


def _even_specs(s, w, ts):
    tile = pl.BlockSpec((ts, 7 * w), lambda i: (i, 0))
    halo = pl.BlockSpec((HALO_B, 7 * w), lambda i: (jnp.maximum(i * (ts // HALO_B) - 1, 0), 0))
    return tile, halo


def _small_specs(shapes, index=lambda i: (0, 0)):
    return [pl.BlockSpec(sh, index) for sh in shapes]


def _even_mixer_fwd(p, w_out, x_res, gain, a_conv, b_conv, bias, ln_g, ln_b, name, comm=None):
    s, d = x_res.shape
    w = p.shape[1] // 7
    ts = _row_tile(s, 128)
    assert ts % HALO_B == 0

    def body(p_ref, halo_ref, wout_ref, x_ref, g_ref, ac_ref, bc_ref, bias_ref, lng_ref, lnb_ref,
             u_ref, xn_ref, y_ref, conv_ref, qbuf, ysh, wb):
        first = pl.program_id(0) == 0
        f = _even_forward_tile(p_ref, halo_ref, first, ac_ref, bc_ref, bias_ref, lng_ref, lnb_ref, qbuf, ysh, conv_ref,
                               wb, w, ts)
        yb3 = f["yb2"] * _sigmoid(f["yb2"])
        u_a = (f["ya"] * (f["a_z"] * _sigmoid(f["a_z"]))).astype(BF16)
        u_b = (yb3 * (f["b_z"] * _sigmoid(f["b_z"]))).astype(BF16)
        u_ref[:, 0:w] = u_a
        u_ref[:, w:2 * w] = u_b
        y = (jnp.dot(u_a, wout_ref[0:w, :], preferred_element_type=F32)
             + jnp.dot(u_b, wout_ref[w:2 * w, :], preferred_element_type=F32))
        r = lax.rsqrt(_mean_last(y * y) + EPS)
        y_ref[...] = y
        xn_ref[...] = x_ref[...] + (y * r) * g_ref[...]

    tile, halo = _even_specs(s, w, ts)
    row = pl.BlockSpec((ts, d), lambda i: (i, 0))
    return _call(
        body, name=name, grid=(s // ts,),
        in_specs=[tile, halo, pl.BlockSpec((2 * w, d), lambda i: (0, 0)), row, pl.BlockSpec((1, d), lambda i: (0, 0))]
        + _small_specs([(CONV_A, w), (CONV_B, w), (1, w), (1, w), (1, w)]),
        out_specs=[pl.BlockSpec((ts, 2 * w), lambda i: (i, 0)), row, row, pl.BlockSpec((ts, w), lambda i: (i, 0))],
        out_shape=[jax.ShapeDtypeStruct((s, 2 * w), BF16), jax.ShapeDtypeStruct((s, d), F32),
                   jax.ShapeDtypeStruct((s, d), F32), jax.ShapeDtypeStruct((s, w), F32)],
        scratch_shapes=[pltpu.VMEM((HALO_A + ts, w), F32), pltpu.VMEM((SHIFTS, HALO_B + ts, w), F32),
                        pltpu.VMEM((CONV_B, SHIFTS, w), F32)],
        args=(p, p, w_out, x_res, gain, a_conv, b_conv, bias, ln_g, ln_b), comm=comm)


def _even_mixer_bwd(p, du, conv, a_conv, b_conv, ln_g, ln_b, name, comm=None):
    s = p.shape[0]
    w = p.shape[1] // 7
    ts = _row_tile(s, EVEN_BWD_ROWS)
    nt = s // ts
    assert ts % HALO_B == 0

    def body(p_ref, halo_ref, du_ref, conv_ref, ac_ref, bc_ref, lng_ref, lnb_ref,
             dp_ref, dac_ref, dbc_ref, dbias_ref, dlng_ref, dlnb_ref,
             qbuf, ysh, y1buf, dqbuf, dsh, dy0buf, wacc, carry_dq, carry_dy, wb):
        step = pl.program_id(0)
        first = step == nt - 1

        @pl.when(step == 0)
        def _():
            for ref in (dac_ref, dbias_ref, dlng_ref, dlnb_ref, wacc, carry_dq, carry_dy):
                ref[...] = jnp.zeros_like(ref)

        @pl.when(step == 0)
        def _():
            for k in range(CONV_B):
                wb[k] = jnp.broadcast_to(bc_ref[k:k + 1, :], (SHIFTS, w))

        keep = jnp.where(first, 0.0, 1.0)
        lane_blocks = [slice(l0, l0 + min(LANE_PASS, w)) for l0 in range(0, w, min(LANE_PASS, w))]

        def col(ref, k, lanes, rows=slice(None)):
            return ref[rows, k * w + lanes.start:k * w + lanes.stop]


        rows_a = slice(HALO_B - HALO_A, HALO_B)
        for lanes in lane_blocks:
            a_x, a_b, a_c, a_z = (col(p_ref, k, lanes) for k in range(4))
            qbuf[0:HALO_A, lanes] = col(halo_ref, 2, lanes, rows_a) * col(halo_ref, 0, lanes, rows_a) * keep
            qbuf[HALO_A:HALO_A + ts, lanes] = a_c * a_x
            cq = jnp.zeros((ts, lanes.stop - lanes.start), F32)
            for j in range(CONV_A):
                cq = cq + ac_ref[CONV_A - 1 - j:CONV_A - j, lanes] * qbuf[HALO_A - j:HALO_A - j + ts, lanes]
            s_az = _sigmoid(a_z)
            du_a = du_ref[:, lanes]
            d_ya = du_a * (a_z * s_az)
            dp_ref[:, 3 * w + lanes.start:3 * w + lanes.stop] = (du_a * (a_b * cq) * _dsilu(a_z, s_az)).astype(BF16)
            dp_ref[:, 1 * w + lanes.start:1 * w + lanes.stop] = (d_ya * cq).astype(BF16)
            d_cq = d_ya * a_b
            dqbuf[0:ts, lanes] = d_cq
            dqbuf[ts:ts + HALO_A, lanes] = carry_dq[:, lanes]
            carry_dq[:, lanes] = d_cq[0:HALO_A, :]
            d_q = jnp.zeros_like(cq)
            for o in range(CONV_A):
                d_q = d_q + ac_ref[CONV_A - 1 - o:CONV_A - o, lanes] * dqbuf[o:o + ts, lanes]
            for j in range(CONV_A):
                k = CONV_A - 1 - j
                dac_ref[k:k + 1, lanes] += _sum_rows(d_cq * qbuf[HALO_A - j:HALO_A - j + ts, lanes])
            dp_ref[:, 2 * w + lanes.start:2 * w + lanes.stop] = (d_q * a_x).astype(BF16)
            dp_ref[:, 0 * w + lanes.start:0 * w + lanes.stop] = (d_q * a_c).astype(BF16)
            ysh[0, 0:HALO_B, lanes] = col(halo_ref, 4, lanes) * _sigmoid(col(halo_ref, 5, lanes)) * keep
            ysh[0, HALO_B:HALO_B + ts, lanes] = col(p_ref, 4, lanes) * _sigmoid(col(p_ref, 5, lanes))
        _fill_shifted_down(ysh, HALO_B + ts)

        total = jnp.zeros((ts, 1), F32)
        for lanes in lane_blocks:
            total = total + jnp.sum(conv_ref[:, lanes], axis=-1, keepdims=True)
        mu = total * (1.0 / w)
        total = jnp.zeros((ts, 1), F32)
        for lanes in lane_blocks:
            xc = conv_ref[:, lanes] - mu
            total = total + jnp.sum(xc * xc, axis=-1, keepdims=True)
        rstd = lax.rsqrt(total * (1.0 / w) + EPS)

        sum_dxh = jnp.zeros((ts, 1), F32)
        sum_dxh_xhat = jnp.zeros((ts, 1), F32)
        for lanes in lane_blocks:
            xhat = (conv_ref[:, lanes] - mu) * rstd
            yb2 = xhat * lng_ref[:, lanes] + lnb_ref[:, lanes]
            b_z = col(p_ref, 6, lanes)
            s_bz, s_y2 = _sigmoid(b_z), _sigmoid(yb2)
            du_b = du_ref[:, w + lanes.start:w + lanes.stop]
            dp_ref[:, 6 * w + lanes.start:6 * w + lanes.stop] = (du_b * (yb2 * s_y2) * _dsilu(b_z, s_bz)).astype(BF16)
            d_yb2 = du_b * (b_z * s_bz) * _dsilu(yb2, s_y2)
            dlng_ref[:, lanes] += _sum_rows(d_yb2 * xhat)
            dlnb_ref[:, lanes] += _sum_rows(d_yb2)
            d_xh = d_yb2 * lng_ref[:, lanes]
            sum_dxh = sum_dxh + jnp.sum(d_xh, axis=-1, keepdims=True)
            sum_dxh_xhat = sum_dxh_xhat + jnp.sum(d_xh * xhat, axis=-1, keepdims=True)
            y1buf[:, lanes] = xhat
            dy0buf[:, lanes] = d_xh
        mean_dxh = sum_dxh * (1.0 / w)
        mean_dxh_xhat = sum_dxh_xhat * (1.0 / w)

        for lanes in lane_blocks:
            d_yb1 = rstd * (dy0buf[:, lanes] - mean_dxh - y1buf[:, lanes] * mean_dxh_xhat)
            dbias_ref[:, lanes] += _sum_rows(d_yb1)
            dsh[0, 0:ts, lanes] = d_yb1
            dsh[0, ts:ts + HALO_B, lanes] = carry_dy[:, lanes]
            carry_dy[:, lanes] = d_yb1[0:HALO_B, :]
        _fill_shifted_up(dsh, ts + HALO_B)
        _conv31(dsh, 0, SHIFTS, wb, None, dy0buf, ts, w)
        _conv31_weight_grad(dsh, ysh, wacc, ts, w)

        @pl.when(step == nt - 1)
        def _():
            for k in range(CONV_B):
                dbc_ref[k:k + 1, :] = _sum_rows(wacc[k])

        for lanes in lane_blocks:
            d_yb0 = dy0buf[:, lanes]
            s_g = _sigmoid(col(p_ref, 5, lanes))
            dp_ref[:, 4 * w + lanes.start:4 * w + lanes.stop] = (d_yb0 * s_g).astype(BF16)
            dp_ref[:, 5 * w + lanes.start:5 * w + lanes.stop] = (
                d_yb0 * col(p_ref, 4, lanes) * s_g * (1.0 - s_g)).astype(BF16)

    rev = lambda i: (nt - 1 - i, 0)
    tile = pl.BlockSpec((ts, 7 * w), rev)
    halo = pl.BlockSpec((HALO_B, 7 * w), lambda i: (jnp.maximum((nt - 1 - i) * (ts // HALO_B) - 1, 0), 0))
    small = [(CONV_A, w), (CONV_B, w), (1, w), (1, w), (1, w)]
    return _call(
        body, name=name, grid=(nt,),
        in_specs=[tile, halo, pl.BlockSpec((ts, 2 * w), rev), pl.BlockSpec((ts, w), rev)]
        + _small_specs([(CONV_A, w), (CONV_B, w), (1, w), (1, w)]),
        out_specs=[pl.BlockSpec((ts, 7 * w), rev)] + _small_specs(small),
        out_shape=[jax.ShapeDtypeStruct((s, 7 * w), BF16)] + [jax.ShapeDtypeStruct(sh, F32) for sh in small],
        scratch_shapes=[pltpu.VMEM((HALO_A + ts, w), F32), pltpu.VMEM((SHIFTS, HALO_B + ts, w), F32),
                        pltpu.VMEM((ts, w), F32),
                        pltpu.VMEM((ts + HALO_A, w), F32), pltpu.VMEM((SHIFTS, ts + HALO_B, w), F32),
                        pltpu.VMEM((ts, w), F32), pltpu.VMEM((CONV_B, SHIFTS, w), F32),
                        pltpu.VMEM((HALO_A, w), F32), pltpu.VMEM((HALO_B, w), F32),
                        pltpu.VMEM((CONV_B, SHIFTS, w), F32)],
        args=(p, p, du, conv, a_conv, b_conv, ln_g, ln_b), comm=comm)


def _trailing_sums(buf_a, buf_b, cols, win, rows, first_out):
    src, dst, shift = buf_a, buf_b, 1
    while True:
        last = 2 * shift >= win
        lo = first_out if last else 0
        val = src[PAD_P + lo:PAD_P + rows, cols] + src[PAD_P + lo - shift:PAD_P + rows - shift, cols]
        if last:
            return val
        dst[PAD_P + lo:PAD_P + rows, cols] = val
        src, dst, shift = dst, src, 2 * shift


def _leading_sums(buf_a, buf_b, cols, win, rows, n_out):
    src, dst, shift = buf_a, buf_b, 1
    while True:
        last = 2 * shift >= win
        hi = n_out if last else rows
        val = src[0:hi, cols] + src[shift:hi + shift, cols]
        if last:
            return val
        dst[0:hi, cols] = val
        src, dst, shift = dst, src, 2 * shift


def _pool_forward_tile(p_ref, halo_ref, first, tile_index, cw_ref, cb_ref, cs_ref, vbuf, vtmp, c, gc, ts):
    vbuf[PAD_P:PAD_P + HALO_P, :] = halo_ref[...] * jnp.where(first, 0.0, 1.0)
    vbuf[PAD_P + HALO_P:PAD_P + HALO_P + ts, :] = p_ref[:, 0:c]
    pos = tile_index * ts + lax.broadcasted_iota(jnp.int32, (ts, 1), 0) + 1
    pooled, inv, gout = [], [], []
    for g, win in enumerate(POOL_WINDOWS):
        cols = slice(g * gc, (g + 1) * gc)
        acc = _trailing_sums(vbuf, vtmp, cols, win, HALO_P + ts, HALO_P)
        inv_g = 1.0 / jnp.minimum(pos, win).astype(F32)
        pooled_g = (acc * inv_g - p_ref[:, cols]).astype(BF16)
        pooled.append(pooled_g)
        inv.append(inv_g)
        gout.append(jnp.dot(pooled_g, cw_ref[g], preferred_element_type=F32) + cb_ref[:, cols])
    return pooled, inv, gout


def _odd_mixer_fwd(p, cw, cb, cs, name):
    s = p.shape[0]
    c = p.shape[1] // 2
    gc = c // N_GROUPS
    ts = _row_tile(s, 256)

    def body(p_ref, halo_ref, cw_ref, cb_ref, cs_ref, u_ref, pooled_ref, vbuf, vtmp):
        i = pl.program_id(0)

        @pl.when(i == 0)
        def _():
            vbuf[0:PAD_P, :] = jnp.zeros((PAD_P, c), F32)
            vtmp[0:PAD_P, :] = jnp.zeros((PAD_P, c), F32)

        pooled, _, gout = _pool_forward_tile(p_ref, halo_ref, i == 0, i, cw_ref, cb_ref, cs_ref, vbuf, vtmp, c, gc, ts)
        for g in range(N_GROUPS):
            cols = slice(g * gc, (g + 1) * gc)
            z = p_ref[:, c + g * gc:c + (g + 1) * gc]
            u_ref[:, cols] = (gout[g] * cs_ref[:, cols] * (z * _sigmoid(z))).astype(BF16)
            pooled_ref[:, cols] = pooled[g]

    return pl.pallas_call(
        body, name=name, grid=(s // ts,),
        in_specs=[pl.BlockSpec((ts, 2 * c), lambda i: (i, 0)),
                  pl.BlockSpec((HALO_P, c), lambda i: (jnp.maximum(i * (ts // HALO_P) - 1, 0), 0)),
                  pl.BlockSpec((N_GROUPS, gc, gc), lambda i: (0, 0, 0)),
                  pl.BlockSpec((1, c), lambda i: (0, 0)), pl.BlockSpec((1, c), lambda i: (0, 0))],
        out_specs=[pl.BlockSpec((ts, c), lambda i: (i, 0))] * 2,
        out_shape=[jax.ShapeDtypeStruct((s, c), BF16)] * 2,
        scratch_shapes=[pltpu.VMEM((PAD_P + HALO_P + ts, c), F32)] * 2,
        compiler_params=_params("arbitrary"),
    )(p, p, cw, cb, cs)


def _odd_mixer_bwd(p, pooled, du, cw, cb, cs, name):
    s = p.shape[0]
    c = p.shape[1] // 2
    gc = c // N_GROUPS
    ts = _row_tile(s, 256)
    nt = s // ts

    def body(z_ref, pooled_ref, du_ref, cw_ref, cb_ref, cs_ref, dp_ref, dcw_ref, dcb_ref, dcs_ref,
             ebuf, etmp, carry_e):
        step = pl.program_id(0)
        tile_index = nt - 1 - step

        @pl.when(step == 0)
        def _():
            for ref in (dcw_ref, dcb_ref, dcs_ref, carry_e):
                ref[...] = jnp.zeros_like(ref)
            for ref in (ebuf, etmp):
                ref[ts + HALO_P:ts + HALO_P + PAD_P, :] = jnp.zeros((PAD_P, c), F32)

        pos = tile_index * ts + lax.broadcasted_iota(jnp.int32, (ts, 1), 0) + 1
        ebuf[ts:ts + HALO_P, :] = carry_e[...]
        for g, win in enumerate(POOL_WINDOWS):
            cols = slice(g * gc, (g + 1) * gc)
            pooled_g = pooled_ref[:, cols]
            gout_g = jnp.dot(pooled_g, cw_ref[g], preferred_element_type=F32) + cb_ref[:, cols]
            z = z_ref[:, cols]
            sz = _sigmoid(z)
            du_g = du_ref[:, cols]
            scale = cs_ref[:, cols]
            d_y = du_g * (z * sz)
            dp_ref[:, c + g * gc:c + (g + 1) * gc] = (du_g * (gout_g * scale) * _dsilu(z, sz)).astype(BF16)
            dcs_ref[:, cols] += _sum_rows(d_y * gout_g)
            d_gout = d_y * scale
            dcb_ref[:, cols] += _sum_rows(d_gout)
            d_gout_b = d_gout.astype(BF16)
            dcw_ref[g] += lax.dot_general(pooled_g, d_gout_b, (((0,), (0,)), ((), ())), preferred_element_type=F32)
            d_pool = lax.dot_general(d_gout_b, cw_ref[g], (((1,), (1,)), ((), ())), preferred_element_type=F32)
            e = d_pool * (1.0 / jnp.minimum(pos, win).astype(F32))
            ebuf[0:ts, cols] = e
            carry_e[:, cols] = e[0:HALO_P, :]
            d_v = _leading_sums(ebuf, etmp, cols, win, ts + HALO_P, ts) - d_pool
            dp_ref[:, cols] = d_v.astype(BF16)

    rev = lambda i: (nt - 1 - i, 0)
    small = [(N_GROUPS, gc, gc), (1, c), (1, c)]
    return pl.pallas_call(
        body, name=name, grid=(nt,),
        in_specs=[pl.BlockSpec((ts, c), lambda i: (nt - 1 - i, 1)),
                  pl.BlockSpec((ts, c), rev),
                  pl.BlockSpec((ts, c), rev),
                  pl.BlockSpec((N_GROUPS, gc, gc), lambda i: (0, 0, 0)),
                  pl.BlockSpec((1, c), lambda i: (0, 0)), pl.BlockSpec((1, c), lambda i: (0, 0))],
        out_specs=[pl.BlockSpec((ts, 2 * c), rev),
                   pl.BlockSpec((N_GROUPS, gc, gc), lambda i: (0, 0, 0)),
                   pl.BlockSpec((1, c), lambda i: (0, 0)), pl.BlockSpec((1, c), lambda i: (0, 0))],
        out_shape=[jax.ShapeDtypeStruct((s, 2 * c), BF16)] + [jax.ShapeDtypeStruct(sh, F32) for sh in small],
        scratch_shapes=[pltpu.VMEM((ts + HALO_P + PAD_P, c), F32)] * 2 + [pltpu.VMEM((HALO_P, c), F32)],
        compiler_params=_params("arbitrary"),
    )(p, pooled, du, cw, cb, cs)


def _cast_into_slot(a, coords, name):
    r, cols = a.shape
    tr = _row_tile(r // 2, 1024)
    per = r // 2 // tr

    def body(co_ref, a_ref, o_ref):
        o_ref[0, 0] = a_ref[...].astype(BF16)

    return pl.pallas_call(
        body, name=name,
        grid_spec=pltpu.PrefetchScalarGridSpec(
            num_scalar_prefetch=1, grid=(2, per),
            in_specs=[pl.BlockSpec((tr, cols), lambda h, i, co: (h * per + i, 0))],
            out_specs=pl.BlockSpec((1, 1, tr, cols), lambda h, i, co: (co[0], h, i, 0))),
        out_shape=jax.ShapeDtypeStruct((N_SHARDS, 2, r // 2, cols), BF16),
        compiler_params=_params("arbitrary", "arbitrary"),
    )(coords, a)


def _chip_sum(g, other, coords, name):
    n_sh, _, r2, cols = g.shape
    tr = _row_tile(r2, 512)

    def body(co_ref, g_ref, o_ref, sum_ref, mine_ref):
        v = (g_ref[0, 0].astype(F32) + o_ref[0].astype(F32)).astype(BF16)
        sum_ref[0] = v

        @pl.when(pl.program_id(1) == co_ref[0])
        def _():
            mine_ref[0] = v

    piece = pl.BlockSpec((1, tr, cols), lambda i, s, co: (s, i, 0))
    return pl.pallas_call(
        body, name=name,
        grid_spec=pltpu.PrefetchScalarGridSpec(
            num_scalar_prefetch=1, grid=(r2 // tr, n_sh),
            in_specs=[pl.BlockSpec((1, 1, tr, cols), lambda i, s, co: (s, co[1], i, 0)), piece],
            out_specs=[piece, pl.BlockSpec((1, tr, cols), lambda i, s, co: (co[0], i, 0))]),
        out_shape=[jax.ShapeDtypeStruct((n_sh, r2, cols), BF16)] * 2,
        compiler_params=_params("arbitrary", "arbitrary"),
    )(coords, g, other)


def _shard_sum(pieces, coords, name):
    n_sh, r2, cols = pieces.shape
    tr = _row_tile(r2, 512)

    def body(co_ref, p_ref, o_ref):
        acc = p_ref[0].astype(F32)
        for k in range(1, n_sh):
            acc = acc + p_ref[k].astype(F32)
        o_ref[0] = acc

    return pl.pallas_call(
        body, name=name,
        grid_spec=pltpu.PrefetchScalarGridSpec(
            num_scalar_prefetch=1, grid=(r2 // tr,),
            in_specs=[pl.BlockSpec((n_sh, tr, cols), lambda i, co: (0, i, 0))],
            out_specs=pl.BlockSpec((1, tr, cols), lambda i, co: (co[1], i, 0))),
        out_shape=jax.ShapeDtypeStruct((2, r2, cols), F32),
        compiler_params=_params("arbitrary"),
    )(coords, pieces)


def _sum_small(a, name):
    n, r, cols = a.shape

    def body(a_ref, o_ref):
        acc = a_ref[0]
        for k in range(1, n):
            acc = acc + a_ref[k]
        o_ref[...] = acc

    return pl.pallas_call(
        body, name=name,
        in_specs=[pl.BlockSpec((n, r, cols), lambda: (0, 0, 0))],
        out_specs=pl.BlockSpec((r, cols), lambda: (0, 0)),
        out_shape=jax.ShapeDtypeStruct((r, cols), F32),
        compiler_params=_params(),
    )(a)


def _adamw_step(w, g, m, v):
    m = ADAM_B1 * m + (1.0 - ADAM_B1) * g
    v = ADAM_B2 * v + (1.0 - ADAM_B2) * (g * g)
    m_hat = m / (1.0 - ADAM_B1 ** ADAM_STEP)
    v_hat = v / (1.0 - ADAM_B2 ** ADAM_STEP)
    return -ADAM_LR * (m_hat / (jnp.sqrt(v_hat) + ADAM_EPS) + ADAM_WD * w), m, v


def _adamw(w, g, m, v, name):
    r, cols = w.shape
    tr = _row_tile(r, 256) if r % SUBLANES_BF16 == 0 else r

    def body(w_ref, g_ref, m_ref, v_ref, go_ref, d_ref, nm_ref, nv_ref):
        g = g_ref[...]
        go_ref[...] = g
        d_ref[...], nm_ref[...], nv_ref[...] = _adamw_step(w_ref[...], g, m_ref[...], v_ref[...])

    blk = pl.BlockSpec((tr, cols), lambda i: (i, 0))
    return pl.pallas_call(
        body, name=name, grid=(r // tr,),
        in_specs=[blk] * 4, out_specs=[blk] * 4,
        out_shape=[jax.ShapeDtypeStruct((r, cols), F32)] * 4,
        compiler_params=_params("arbitrary"),
    )(w, g, m, v)


def _adamw_many(ws, gs, ms, vs, name):
    n = len(ws)

    def body(*refs):
        w_refs, g_refs, m_refs, v_refs = (refs[t * n:(t + 1) * n] for t in range(4))
        d_refs, nm_refs, nv_refs = (refs[(4 + t) * n:(5 + t) * n] for t in range(3))
        for k in range(n):
            d_refs[k][...], nm_refs[k][...], nv_refs[k][...] = _adamw_step(
                w_refs[k][...], g_refs[k][...], m_refs[k][...], v_refs[k][...])

    specs = [pl.BlockSpec(a.shape, lambda: (0, 0)) for a in ws]
    out = pl.pallas_call(
        body, name=name, in_specs=specs * 4, out_specs=specs * 3,
        out_shape=[jax.ShapeDtypeStruct(a.shape, F32) for a in ws] * 3,
        compiler_params=_params(),
    )(*ws, *gs, *ms, *vs)
    return out[:n], out[n:2 * n], out[2 * n:]


def _place():
    x, y, c = lax.axis_index("x"), lax.axis_index("y"), lax.axis_index("c")
    other_chips = [(1 - x, y), (x, 1 - y), (1 - x, 1 - y)]
    return x, y, c, other_chips


def _chip(xy):
    return 2 * xy[0] + xy[1]


def _remote(src, dst, send_sem, recv_sem, to):
    return pltpu.make_async_remote_copy(src_ref=src, dst_ref=dst, send_sem=send_sem, recv_sem=recv_sem,
                                        device_id=to, device_id_type=MESH)


def _gather_ici(ctx, k, j, start):
    (x, y, c, chips), b, send, recv = ctx
    if j < 2:
        chip, to = (_chip((x, y)) if start else _chip(chips[j])), (*chips[j], c)
    else:
        chip = 2 * (x ^ c) + (y ^ (1 - c)) if start else _chip(chips[2])
        to = (x ^ (1 - c), y ^ c, c)
    blk = b[k].at[chip, c]
    return _remote(blk, blk, send.at[6 * k + j], recv.at[6 * k + j], to)


def _gather_d2d(ctx, k, j, start):
    (x, y, c, chips), b, send, recv = ctx
    blk = b[k].at[_chip(chips[j]), c if start else 1 - c]
    return _remote(blk, blk, send.at[6 * k + 3 + j], recv.at[6 * k + 3 + j], (x, y, 1 - c))


def _gather_small(ctx, n, j, start):
    (x, y, c, chips), b, send, recv = ctx
    blk = b[n].at[_chip((x, y)) if start else _chip(chips[j])]
    return _remote(blk, blk, send.at[6 * n + j], recv.at[6 * n + j], (*chips[j], c))


def _gather_neighbours_landed(ctx, k):
    for j in range(2):
        _gather_ici(ctx, k, j, False).wait_recv()
    _gather_ici(ctx, k, 2, True).start()
    for j in range(2):
        _gather_d2d(ctx, k, j, True).start()


def _gather_diagonal_landed(ctx, k):
    _gather_ici(ctx, k, 2, False).wait_recv()
    _gather_d2d(ctx, k, 2, True).start()


def _gather_comm(bufs, relay_at, forward_at):
    n = len(bufs)

    def start(srcs, b, send, recv):
        for k in range(n):
            for j in range(2):
                _gather_ici((_place(), b, send, recv), k, j, True).start()

    def relay(srcs, b, send, recv):
        for k in range(n):
            _gather_neighbours_landed((_place(), b, send, recv), k)

    def forward(srcs, b, send, recv):
        for k in range(n):
            _gather_diagonal_landed((_place(), b, send, recv), k)

    def finish(srcs, b, send, recv):
        ctx = (_place(), b, send, recv)
        for k in range(n):
            for j in range(3):
                _gather_d2d(ctx, k, j, False).wait_recv()
                _gather_ici(ctx, k, j, True).wait_send()
                _gather_d2d(ctx, k, j, True).wait_send()

    return _Comm([], bufs, 6 * n, [(0, start), (relay_at, relay), (forward_at, forward)], finish)


def _rmsnorm(x, gain, name):
    s, d = x.shape
    tm = _row_tile(s, 512)

    def body(x_ref, g_ref, h_ref):
        xv = x_ref[...]
        r = lax.rsqrt(_mean_last(xv * xv) + EPS)
        h_ref[...] = (xv * r * g_ref[...]).astype(BF16)

    return pl.pallas_call(
        body, name=name, grid=(s // tm,),
        in_specs=[pl.BlockSpec((tm, d), lambda i: (i, 0)), pl.BlockSpec((1, d), lambda i: (0, 0))],
        out_specs=pl.BlockSpec((tm, d), lambda i: (i, 0)),
        out_shape=jax.ShapeDtypeStruct((s, d), BF16),
        compiler_params=_params("arbitrary"),
    )(x, gain)


def _gathered_in_proj(h, bufs, small, order, name):
    s, d = h.shape
    n_sh, _, r2, ns = bufs[0].shape
    assert d == 2 * r2
    n = len(bufs)
    tm = _row_tile(s, 512)
    n_i = s // tm
    hook_i = max(n_i - 2, 0)
    n_sems = 6 * n + 3

    def body(order_ref, h_ref, *rest):
        p_ref = rest[n + 1]
        b = rest[n + 2:2 * n + 3]
        w_vmem, w_sems, send, recv = rest[2 * n + 3:]
        j, i = pl.program_id(0), pl.program_id(1)
        ctx = (_place(), b, send, recv)

        def fetch(q):
            return pltpu.make_async_copy(b[0].at[order_ref[q]], w_vmem.at[q % 2], w_sems.at[q % 2])

        @pl.when((j == 0) & (i == 0))
        def _():
            for k in range(n):
                for peer in range(2):
                    _gather_ici(ctx, k, peer, True).start()
            for peer in range(3):
                _gather_small(ctx, n, peer, True).start()
            fetch(0).start()
            fetch(0).wait()

        for q in range(1, n_sh):
            @pl.when((j == q - 1) & (i == hook_i))
            def _(q=q):
                if q == 1:
                    _gather_neighbours_landed(ctx, 0)
                if q == 2:
                    for k in range(1, n):
                        _gather_neighbours_landed(ctx, k)
                if q == 3:
                    for k in range(n):
                        _gather_diagonal_landed(ctx, k)
                _gather_d2d(ctx, 0, q - 1, False).wait_recv()
                fetch(q).start()

            @pl.when((j == q) & (i == 0))
            def _(q=q):
                fetch(q).wait()

        wv = w_vmem.at[j % 2]
        p_ref[...] = (jnp.dot(h_ref[:, 0:r2], wv[0], preferred_element_type=F32)
                      + jnp.dot(h_ref[:, r2:d], wv[1], preferred_element_type=F32))

        @pl.when((j == n_sh - 1) & (i == n_i - 1))
        def _():
            for peer in range(3):
                _gather_small(ctx, n, peer, False).wait_recv()
                _gather_small(ctx, n, peer, True).wait_send()
            for k in range(n):
                for peer in range(3):
                    if k > 0:
                        _gather_d2d(ctx, k, peer, False).wait_recv()
                    _gather_ici(ctx, k, peer, True).wait_send()
                    _gather_d2d(ctx, k, peer, True).wait_send()

    all_bufs = list(bufs) + [small]
    out = pl.pallas_call(
        body, name=name,
        grid_spec=pltpu.PrefetchScalarGridSpec(
            num_scalar_prefetch=1, grid=(n_sh, n_i),
            in_specs=[pl.BlockSpec((tm, d), lambda j, i, o: (i, 0))] + [ANY] * (n + 1),
            out_specs=[pl.BlockSpec((tm, ns), lambda j, i, o: (i, o[j]))] + [ANY] * (n + 1),
            scratch_shapes=[pltpu.VMEM((2, 2, r2, ns), BF16), pltpu.SemaphoreType.DMA((2,)),
                            pltpu.SemaphoreType.DMA((n_sems,)), pltpu.SemaphoreType.DMA((n_sems,))]),
        out_shape=[jax.ShapeDtypeStruct((s, n_sh * ns), F32)]
        + [jax.ShapeDtypeStruct(a.shape, a.dtype) for a in all_bufs],
        input_output_aliases={2 + t: 1 + t for t in range(n + 1)},
        compiler_params=_params("arbitrary", "arbitrary"),
    )(order, h, *all_bufs)
    return out[0], list(out[1:])


def _exchange_comm(grads):
    n = len(grads)
    landing = [lax.empty((N_SHARDS,) + a.shape[2:], a.dtype) for a in grads]

    def copies(srcs, b, send, recv):
        x, y, c, _ = _place()
        return [_remote(srcs[k].at[s, 1 - c], b[k].at[s], send.at[N_SHARDS * k + s], recv.at[N_SHARDS * k + s],
                        (x, y, 1 - c)) for k in range(n) for s in range(N_SHARDS)]

    def start(srcs, b, send, recv):
        for cp in copies(srcs, b, send, recv):
            cp.start()

    def finish(srcs, b, send, recv):
        for cp in copies(srcs, b, send, recv):
            cp.wait()

    return _Comm(grads, landing, N_SHARDS * n, [(0, start)], finish)


def _scatter_comm(chip_sums, landing):
    n = len(chip_sums)

    def big(srcs, b, send, recv, k, j, start):
        x, y, c, chips = _place()
        dst = b[k].at[_chip((x, y)) if start else _chip(chips[j])]
        return _remote(srcs[k].at[_chip(chips[j])], dst, send.at[3 * k + j], recv.at[3 * k + j], (*chips[j], c))

    def start(srcs, b, send, recv):
        for k in range(n):
            for j in range(3):
                big(srcs, b, send, recv, k, j, True).start()

    def finish(srcs, b, send, recv):
        for k in range(n):
            for j in range(3):
                big(srcs, b, send, recv, k, j, False).wait_recv()
                big(srcs, b, send, recv, k, j, True).wait_send()

    return _Comm(chip_sums, landing, 3 * n, [(0, start)], finish)


def _join_comm(halves, small):
    n = len(halves)
    flips = [(fx, fy, fc) for fx in (0, 1) for fy in (0, 1) for fc in (0, 1)][1:]

    def half(b, send, recv, k, start):
        x, y, c, _ = _place()
        return _remote(b[k].at[c], b[k].at[c if start else 1 - c], send.at[k], recv.at[k], (x, y, 1 - c))

    def small_copy(b, send, recv, q, start):
        x, y, c, _ = _place()
        px, py, pc = x ^ flips[q][0], y ^ flips[q][1], c ^ flips[q][2]
        blk = b[n].at[4 * x + 2 * y + c if start else 4 * px + 2 * py + pc]
        return _remote(blk, blk, send.at[n + q], recv.at[n + q], (px, py, pc))

    def start(srcs, b, send, recv):
        for q in range(len(flips)):
            small_copy(b, send, recv, q, True).start()
        for k in range(n):
            half(b, send, recv, k, True).start()

    def finish(srcs, b, send, recv):
        for q in range(len(flips)):
            small_copy(b, send, recv, q, False).wait()
        for k in range(n):
            half(b, send, recv, k, False).wait()

    return _Comm([], list(halves) + [small], n + len(flips), [(0, start)], finish)


def _flat_rows(parts):
    flat = jnp.concatenate([p.reshape(-1) for p in parts])
    assert flat.shape[0] % LANES == 0
    return flat.reshape(-1, LANES)


def _unflatten(flat, shapes):
    out, at = [], 0
    for sh in shapes:
        size = 1
        for dim in sh:
            size *= dim
        out.append(flat[at:at + size].reshape(sh))
        at += size
    assert at == flat.shape[0], (at, flat.shape)
    return out


def _col_shards_to_full(a, rows):
    q = a.shape[1] // rows
    return a.reshape(N_SHARDS, rows, q).transpose(1, 0, 2).reshape(rows, N_SHARDS * q)


def _my_col_shard(full, chip):
    rows, cols = full.shape
    q = cols // N_SHARDS
    return lax.dynamic_index_in_dim(full.reshape(rows, N_SHARDS, q), chip, axis=1, keepdims=False)


def kernel(x, e_norm_pre, e_norm_post, e_w_in, e_a_conv, e_b_conv, e_b_conv_bias, e_b_ln_g, e_b_ln_b, e_w_out, o_norm_pre, o_norm_post, o_w_in, o_c_w, o_c_b, o_c_scale, o_w_out, loss_target, m_e_norm_pre, m_e_norm_post, m_e_w_in, m_e_a_conv, m_e_b_conv, m_e_b_conv_bias, m_e_b_ln_g, m_e_b_ln_b, m_e_w_out, m_o_norm_pre, m_o_norm_post, m_o_w_in, m_o_c_w, m_o_c_b, m_o_c_scale, m_o_w_out, v_e_norm_pre, v_e_norm_post, v_e_w_in, v_e_a_conv, v_e_b_conv, v_e_b_conv_bias, v_e_b_ln_g, v_e_b_ln_b, v_e_w_out, v_o_norm_pre, v_o_norm_post, v_o_w_in, v_o_c_w, v_o_c_b, v_o_c_scale, v_o_w_out):
    _, s, d = x.shape
    w = d // 2
    c = d
    gc = c // N_GROUPS
    wq, cq, gq = w // N_SHARDS, c // N_SHARDS, gc // N_SHARDS
    chip = 2 * lax.axis_index("x") + lax.axis_index("y")
    core = lax.axis_index("c")
    x2 = x.reshape(s, d)
    target = loss_target.reshape(s, d)

    big_w = [e_w_in[0], e_w_out[0], o_w_in[0], o_c_w[0].reshape(N_GROUPS * gq, gc), o_w_out[0]]
    big_m = [m_e_w_in[0], m_e_w_out[0], m_o_w_in[0], m_o_c_w[0].reshape(N_GROUPS * gq, gc), m_o_w_out[0]]
    big_v = [v_e_w_in[0], v_e_w_out[0], v_o_w_in[0], v_o_c_w[0].reshape(N_GROUPS * gq, gc), v_o_w_out[0]]
    coords = jnp.stack([chip, core]).astype(jnp.int32)
    slots = [_cast_into_slot(a, coords, "cast_%d" % k) for k, a in enumerate(big_w)]
    sharded_small = _flat_rows([e_a_conv[0], e_b_conv[0], o_norm_pre, o_norm_post, o_c_scale, o_c_b[0]])
    small_slots = lax.dynamic_update_index_in_dim(jnp.zeros((N_SHARDS,) + sharded_small.shape, F32), sharded_small,
                                                  chip, 0)
    xi, yi = lax.axis_index("x"), lax.axis_index("y")
    order = jnp.stack([chip, 2 * (1 - xi) + yi, 2 * xi + (1 - yi), 2 * (1 - xi) + (1 - yi)]).astype(jnp.int32)
    h0 = _rmsnorm(x2, e_norm_pre, "e_pre_norm")
    p0, (e_w_in_g, e_w_out_g, small_g4) = _gathered_in_proj(h0, slots[:2], small_slots, order, "e_in_proj")
    e_w_in_sm = e_w_in_g.reshape((N_SHARDS,) + big_w[0].shape)
    e_w_out_f = e_w_out_g.reshape(w + w, d)
    sm = small_g4.reshape(N_SHARDS, -1)
    at = [0]

    def take(rows, q):
        blk = sm[:, at[0]:at[0] + rows * q]
        at[0] += rows * q
        return _col_shards_to_full(blk, rows)

    a_conv_f = take(CONV_A, wq)
    b_conv_f = take(CONV_B, wq)
    o_pre_f = take(1, cq)
    o_post_f = take(1, cq)
    cs_f = take(1, cq)
    cb_f = take(N_GROUPS, gq).reshape(1, c)

    mixer_steps = s // _row_tile(s, 128)
    (u0, x1, y0, conv0), odd_g = _even_mixer_fwd(
        p0, e_w_out_f, x2, e_norm_post, a_conv_f, b_conv_f, e_b_conv_bias, e_b_ln_g, e_b_ln_b, "e_mixer_out_proj",
        comm=_gather_comm(slots[2:], mixer_steps // 2, (25 * mixer_steps) // 32))
    o_w_in_sm = odd_g[0].reshape((N_SHARDS,) + big_w[2].shape)
    cw_f = odd_g[1].reshape(N_SHARDS, N_GROUPS, gq, gc).transpose(1, 0, 2, 3).reshape(N_GROUPS, gc, gc)
    o_w_out_f = odd_g[2].reshape(c, d)
    p1, h1 = _norm_matmul(x1, o_pre_f, o_w_in_sm, "o_in_proj")
    u1, pooled1 = _odd_mixer_fwd(p1, cw_f, cb_f, cs_f, "o_mixer_fwd")
    d_y1, d_x2, d_o_post, loss_part = _matmul_post_loss(u1, o_w_out_f, x1, o_post_f, target, "o_out_proj_loss")

    def as_pieces(g, k):
        return g.reshape(N_SHARDS, 2, big_w[k].shape[0] // 2, big_w[k].shape[1])

    def chip_sums(ks, pieces, from_sibling):
        both = [_chip_sum(g, o, coords, "chip_sum_%d" % k) for k, g, o in zip(ks, pieces, from_sibling)]
        return [b[0] for b in both], [b[1] for b in both]

    g_o_w_out = _matmul_tn(u1, d_y1, 1, "o_w_out_grad")
    d_u1, _ = _matmul_nt(d_y1, o_w_out_f[None], "o_out_proj_bwd")
    d_p1, d_cw, d_cb, d_cs = _odd_mixer_bwd(p1, pooled1, d_u1, cw_f, cb_f, cs_f, "o_mixer_bwd")
    g_o_w_in = _matmul_tn(h1, d_p1, N_SHARDS, "o_w_in_grad")
    g_cw = d_cw.reshape(N_GROUPS, N_SHARDS, gq, gc).transpose(1, 0, 2, 3).astype(BF16)
    pieces_o = [as_pieces(g_o_w_in, 2), as_pieces(g_cw, 3), as_pieces(g_o_w_out, 4)]
    d_h1, sibling_o = _matmul_nt(d_p1, o_w_in_sm, "o_in_proj_bwd", comm=_exchange_comm(pieces_o))
    d_x1, d_o_pre, d_y0, d_e_post = _norm_bwd(d_h1, x1, o_pre_f, d_x2, "o_pre_norm_bwd", post=(y0, e_norm_post))

    pieces_e = [as_pieces(_matmul_tn(u0, d_y0, 1, "e_w_out_grad"), 1)]
    d_u0, sibling_e = _matmul_nt(d_y0, e_w_out_f[None], "e_out_proj_bwd", comm=_exchange_comm(pieces_e))
    sums_a, landing_a = chip_sums([1, 2, 3, 4], pieces_e + pieces_o, sibling_e + sibling_o)
    (d_p0, d_a_conv, d_b_conv, d_bias, d_ln_g, d_ln_b), landed_a = _even_mixer_bwd(
        p0, d_u0, conv0, a_conv_f, b_conv_f, e_b_ln_g, e_b_ln_b, "e_mixer_bwd",
        comm=_scatter_comm(sums_a, landing_a))
    pieces_b = [as_pieces(_matmul_tn(h0, d_p0, N_SHARDS, "e_w_in_grad"), 0)]
    sums_b, landing_b = chip_sums([0], pieces_b, _comm_only(_exchange_comm(pieces_b), "exchange_core_halves"))
    d_h0, landed_b = _matmul_nt(d_p0, e_w_in_sm, "e_in_proj_bwd", comm=_scatter_comm(sums_b, landing_b))
    grad_x, d_e_pre = _norm_bwd(d_h0, x2, e_norm_pre, d_x1, "e_pre_norm_bwd")

    landed = landed_b + landed_a
    reduced = [_shard_sum(sc, coords, "shard_sum_%d" % k) for k, sc in enumerate(landed)]
    small_parts = _flat_rows([loss_part[0], d_e_pre, d_e_post, d_bias, d_ln_g, d_ln_b, d_a_conv, d_b_conv,
                              d_o_pre, d_o_post, d_cs, d_cb])
    small_rows = lax.dynamic_update_index_in_dim(jnp.zeros((N_DEVICES,) + small_parts.shape, F32), small_parts,
                                                 2 * chip + core, 0)
    joined = _comm_only(_join_comm(reduced, small_rows), "join_core_halves")
    big_g = [j.reshape(a.shape) for j, a in zip(joined[:5], big_w)]
    small_sum = _sum_small(joined[5], "small_sum").reshape(-1)
    (loss_row, g_e_pre, g_e_post, g_bias, g_ln_g, g_ln_b, g_a_conv_f, g_b_conv_f, g_o_pre_f, g_o_post_f, g_cs_f,
     g_cb_f) = _unflatten(small_sum, [(LANES,), (1, d), (1, d), (1, w), (1, w), (1, w), (CONV_A, w), (CONV_B, w),
                                      (1, c), (1, c), (1, c), (1, c)])
    loss = loss_row[0]
    g_a_conv = _my_col_shard(g_a_conv_f, chip)
    g_b_conv = _my_col_shard(g_b_conv_f, chip)
    g_o_pre = _my_col_shard(g_o_pre_f, chip)
    g_o_post = _my_col_shard(g_o_post_f, chip)
    g_cs = _my_col_shard(g_cs_f, chip)
    g_cb = _my_col_shard(g_cb_f.reshape(N_GROUPS, gc), chip)

    big_upd = [_adamw(wt, g, m, v, "adamw_%d" % k) for k, (wt, g, m, v) in enumerate(zip(big_w, big_g, big_m, big_v))]
    big_g = [u[0] for u in big_upd]
    big_upd = [u[1:] for u in big_upd]
    small_w = [e_norm_pre, e_norm_post, e_b_conv_bias, e_b_ln_g, e_b_ln_b, e_a_conv[0], e_b_conv[0],
               o_norm_pre, o_norm_post, o_c_b[0], o_c_scale]
    small_m = [m_e_norm_pre, m_e_norm_post, m_e_b_conv_bias, m_e_b_ln_g, m_e_b_ln_b, m_e_a_conv[0], m_e_b_conv[0],
               m_o_norm_pre, m_o_norm_post, m_o_c_b[0], m_o_c_scale]
    small_v = [v_e_norm_pre, v_e_norm_post, v_e_b_conv_bias, v_e_b_ln_g, v_e_b_ln_b, v_e_a_conv[0], v_e_b_conv[0],
               v_o_norm_pre, v_o_norm_post, v_o_c_b[0], v_o_c_scale]
    small_g = [g_e_pre, g_e_post, g_bias, g_ln_g, g_ln_b, g_a_conv, g_b_conv, g_o_pre, g_o_post, g_cb, g_cs]
    small_delta, small_new_m, small_new_v = _adamw_many(small_w, small_g, small_m, small_v, "adamw_small")

    def ordered(small, big):
        (n_pre, n_post, bias, ln_g, ln_b, a_conv, b_conv, o_pre, o_post, cb, cs) = small
        (w_in, w_out, ow_in, cw, ow_out) = big
        return [n_pre, n_post, w_in[None], a_conv[None], b_conv[None], bias, ln_g, ln_b, w_out[None], o_pre, o_post,
                ow_in[None], cw.reshape(1, N_GROUPS, gq, gc), cb[None], cs, ow_out[None]]

    grads = ordered(small_g, big_g)
    deltas = ordered(small_delta, [u[0] for u in big_upd])
    new_m = ordered(small_new_m, [u[1] for u in big_upd])
    new_v = ordered(small_new_v, [u[2] for u in big_upd])
    return (loss, grad_x.reshape(1, s, d), *grads, *deltas, *new_m, *new_v)
```

```python
import functools

import jax
import jax.numpy as jnp
from jax import lax
from jax.experimental import pallas as pl
from jax.experimental.pallas import tpu as pltpu

F32 = jnp.float32
BF16 = jnp.bfloat16
MESH = pl.DeviceIdType.MESH

EPS = 1e-6
CONV_A = 3
CONV_B = 31
POOL_WINDOWS = (2, 4, 8, 16)
N_GROUPS = len(POOL_WINDOWS)
N_SHARDS = 4
N_DEVICES = 8
ADAM_LR = 0.001
ADAM_B1 = 0.9
ADAM_B2 = 0.999
ADAM_EPS = 1e-08
ADAM_WD = 0.01
ADAM_STEP = 10

LANES = 128
SUBLANES_BF16 = 16
HALO_A = 8
HALO_B = 32
HALO_P = 16
PAD_P = 8
SHIFTS = 8
ROW_BLOCK = 32
LANE_BLOCK = 256
LANE_PASS = 256
VMEM_LIMIT = 56 * 1024 * 1024
TN_ACC_BYTES = 8 * 1024 * 1024
EVEN_BWD_ROWS = 128


def _row_tile(n, pref):
    t = max(min(n, pref) // SUBLANES_BF16, 1) * SUBLANES_BF16
    while t > SUBLANES_BF16 and (n % t or t % SUBLANES_BF16):
        t -= SUBLANES_BF16
    assert n % t == 0, (n, pref)
    return t


def _col_chunk(n, pref):
    t = (min(n, pref) // LANES) * LANES
    while t > LANES and n % t:
        t -= LANES
    assert t >= LANES and n % t == 0, (n, pref)
    return t


def _params(*sem):
    return pltpu.CompilerParams(dimension_semantics=tuple(sem) if sem else None, vmem_limit_bytes=VMEM_LIMIT)


ANY = pl.BlockSpec(memory_space=pl.ANY)


class _Comm:
    def __init__(self, srcs, bufs, n_sems, phases, finish):
        self.srcs, self.bufs, self.n_sems, self.phases, self.finish = list(srcs), list(bufs), n_sems, phases, finish


def _call(body, *, name, grid, in_specs, out_specs, out_shape, args, scratch_shapes=(), comm=None):
    params = _params(*(("arbitrary",) * len(grid)))
    if comm is None:
        out = pl.pallas_call(body, name=name, grid=grid, in_specs=in_specs, out_specs=out_specs, out_shape=out_shape,
                             scratch_shapes=scratch_shapes, compiler_params=params)(*args)
        return list(out), []
    n_in, n_out, n_scr = len(in_specs), len(out_specs), len(scratch_shapes)
    ns, nb = len(comm.srcs), len(comm.bufs)
    total = 1
    for size in grid:
        total *= size

    def fused(*refs):
        ins, srcs = refs[:n_in], refs[n_in:n_in + ns]
        at = n_in + ns + nb
        outs, bufs = refs[at:at + n_out], refs[at + n_out:at + n_out + nb]
        scratch = refs[at + n_out + nb:at + n_out + nb + n_scr]
        send_sems, recv_sems = refs[-2:]
        step = 0
        for axis, size in enumerate(grid):
            step = step * size + pl.program_id(axis)
        for when, fn in comm.phases:
            pl.when(step == when)(functools.partial(fn, srcs, bufs, send_sems, recv_sems))
        body(*ins, *outs, *scratch)
        pl.when(step == total - 1)(functools.partial(comm.finish, srcs, bufs, send_sems, recv_sems))

    out = pl.pallas_call(
        fused, name=name, grid=grid,
        in_specs=list(in_specs) + [ANY] * (ns + nb), out_specs=list(out_specs) + [ANY] * nb,
        out_shape=list(out_shape) + [jax.ShapeDtypeStruct(b.shape, b.dtype) for b in comm.bufs],
        input_output_aliases={n_in + ns + i: n_out + i for i in range(nb)},
        scratch_shapes=list(scratch_shapes) + [pltpu.SemaphoreType.DMA((comm.n_sems,))] * 2,
        compiler_params=params,
    )(*args, *comm.srcs, *comm.bufs)
    return list(out[:n_out]), list(out[n_out:])


def _comm_only(comm, name):
    ns, nb = len(comm.srcs), len(comm.bufs)

    def body(*refs):
        srcs, bufs = refs[:ns], refs[ns + nb:ns + 2 * nb]
        send_sems, recv_sems = refs[-2:]
        for _, fn in comm.phases:
            fn(srcs, bufs, send_sems, recv_sems)
        comm.finish(srcs, bufs, send_sems, recv_sems)

    return pl.pallas_call(
        body, name=name, in_specs=[ANY] * (ns + nb), out_specs=[ANY] * nb,
        out_shape=[jax.ShapeDtypeStruct(b.shape, b.dtype) for b in comm.bufs],
        input_output_aliases={ns + i: i for i in range(nb)},
        scratch_shapes=[pltpu.SemaphoreType.DMA((comm.n_sems,))] * 2,
    )(*comm.srcs, *comm.bufs)


def _sigmoid(v):
    return jax.nn.sigmoid(v)


def _dsilu(v, s):
    return s * (1.0 + v * (1.0 - s))


def _mean_last(v):
    return jnp.mean(v, axis=-1, keepdims=True)


def _sum_rows(v):
    return jnp.sum(v, axis=0, keepdims=True)


def _norm_matmul(x, gain, w_sm, name):
    s, d = x.shape
    n_sh, _, ns = w_sm.shape
    tm = _row_tile(s, 1024)

    def body(x_ref, g_ref, w_ref, p_ref, h_ref):
        @pl.when(pl.program_id(1) == 0)
        def _():
            xv = x_ref[...]
            r = lax.rsqrt(_mean_last(xv * xv) + EPS)
            h_ref[...] = (xv * r * g_ref[...]).astype(BF16)

        p_ref[...] = jnp.dot(h_ref[...], w_ref[0], preferred_element_type=F32)

    return _call(
        body, name=name, grid=(s // tm, n_sh),
        in_specs=[pl.BlockSpec((tm, d), lambda i, j: (i, 0)),
                  pl.BlockSpec((1, d), lambda i, j: (0, 0)),
                  pl.BlockSpec((1, d, ns), lambda i, j: (j, 0, 0))],
        out_specs=[pl.BlockSpec((tm, ns), lambda i, j: (i, j)),
                   pl.BlockSpec((tm, d), lambda i, j: (i, 0))],
        out_shape=[jax.ShapeDtypeStruct((s, n_sh * ns), F32), jax.ShapeDtypeStruct((s, d), BF16)],
        args=(x, gain, w_sm))[0]


def _matmul_post_loss(u, w, x_res, gain, target, name):
    s, k = u.shape
    d = w.shape[1]
    tm = _row_tile(s, 512)
    half = tm // 2

    def body(u_ref, w_ref, x_ref, g_ref, t_ref, dy_ref, dout_ref, dg_ref, loss_ref):
        @pl.when(pl.program_id(0) == 0)
        def _():
            dg_ref[...] = jnp.zeros_like(dg_ref)
            loss_ref[...] = jnp.zeros_like(loss_ref)

        g = g_ref[...]
        for rows in (slice(0, half), slice(half, tm)):
            y = jnp.dot(u_ref[rows, :], w_ref[...], preferred_element_type=F32)
            r = lax.rsqrt(_mean_last(y * y) + EPS)
            n = y * r
            err = x_ref[rows, :] + n * g - t_ref[rows, :]
            loss_ref[...] += 0.5 * jnp.sum(_mean_last(err * err))
            dout = err * (1.0 / d)
            dout_ref[rows, :] = dout
            dg_ref[...] += _sum_rows(dout * n)
            dn = dout * g
            dy_ref[rows, :] = (r * (dn - n * _mean_last(dn * n))).astype(BF16)

    return pl.pallas_call(
        body, name=name, grid=(s // tm,),
        in_specs=[pl.BlockSpec((tm, k), lambda i: (i, 0)),
                  pl.BlockSpec((k, d), lambda i: (0, 0), pipeline_mode=pl.Buffered(1)),
                  pl.BlockSpec((tm, d), lambda i: (i, 0)),
                  pl.BlockSpec((1, d), lambda i: (0, 0)),
                  pl.BlockSpec((tm, d), lambda i: (i, 0))],
        out_specs=[pl.BlockSpec((tm, d), lambda i: (i, 0)),
                   pl.BlockSpec((tm, d), lambda i: (i, 0)),
                   pl.BlockSpec((1, d), lambda i: (0, 0)),
                   pl.BlockSpec((8, LANES), lambda i: (0, 0))],
        out_shape=[jax.ShapeDtypeStruct((s, d), BF16), jax.ShapeDtypeStruct((s, d), F32),
                   jax.ShapeDtypeStruct((1, d), F32), jax.ShapeDtypeStruct((8, LANES), F32)],
        compiler_params=_params("arbitrary"),
    )(u, w, x_res, gain, target)


def _matmul_nt(a, w_sm, name, comm=None):
    s, ncols = a.shape
    n_sh, r, ns = w_sm.shape
    assert ncols == n_sh * ns
    tm = _row_tile(s, 1024)
    nc = _col_chunk(ns, 1792)
    per = ns // nc
    steps = n_sh * per

    def body(a_ref, w_ref, o_ref):
        part = lax.dot_general(a_ref[...], w_ref[0], (((1,), (1,)), ((), ())), preferred_element_type=F32)

        @pl.when(pl.program_id(1) == 0)
        def _():
            o_ref[...] = part

        @pl.when(pl.program_id(1) > 0)
        def _():
            o_ref[...] += part

    out, bufs = _call(
        body, name=name, grid=(s // tm, steps),
        in_specs=[pl.BlockSpec((tm, nc), lambda i, j: (i, j)),
                  pl.BlockSpec((1, r, nc), lambda i, j: (j // per, 0, j % per))],
        out_specs=[pl.BlockSpec((tm, r), lambda i, j: (i, 0))],
        out_shape=[jax.ShapeDtypeStruct((s, r), F32)],
        args=(a, w_sm), comm=comm)
    return out[0], bufs


def _matmul_tn(a, b, n_sh, name):
    s, k = a.shape
    n = b.shape[1]
    ns = n // n_sh
    tk = _col_chunk(k, TN_ACC_BYTES // (4 * ns))
    ts = _row_tile(s, 2048)
    n_s = s // ts

    def body(a_ref, b_ref, o_ref, acc_ref):
        part = lax.dot_general(a_ref[...], b_ref[...], (((0,), (0,)), ((), ())), preferred_element_type=F32)

        @pl.when(pl.program_id(2) == 0)
        def _():
            acc_ref[...] = part

        @pl.when(pl.program_id(2) > 0)
        def _():
            acc_ref[...] += part

        @pl.when(pl.program_id(2) == n_s - 1)
        def _():
            o_ref[0] = acc_ref[...].astype(BF16)

    return pl.pallas_call(
        body, name=name, grid=(n_sh, k // tk, n_s),
        in_specs=[pl.BlockSpec((ts, tk), lambda j, i, t: (t, i)),
                  pl.BlockSpec((ts, ns), lambda j, i, t: (t, j))],
        out_specs=pl.BlockSpec((1, tk, ns), lambda j, i, t: (j, i, 0)),
        out_shape=jax.ShapeDtypeStruct((n_sh, k, ns), BF16),
        scratch_shapes=[pltpu.VMEM((tk, ns), F32)],
        compiler_params=_params("arbitrary", "arbitrary", "arbitrary"),
    )(a, b)


def _norm_bwd(dh, x, gain, dres, name, post=None):
    s, d = x.shape
    tm = _row_tile(s, 256)
    with_post = post is not None

    def rms_bwd(dout, v, g):
        r = lax.rsqrt(_mean_last(v * v) + EPS)
        n = v * r
        dn = dout * g
        return r * (dn - n * _mean_last(dn * n)), _sum_rows(dout * n)

    def body(*refs):
        if with_post:
            dh_ref, x_ref, g_ref, dres_ref, y_ref, gp_ref, dx_ref, dg_ref, dy_ref, dgp_ref = refs
        else:
            dh_ref, x_ref, g_ref, dres_ref, dx_ref, dg_ref = refs

        @pl.when(pl.program_id(0) == 0)
        def _():
            dg_ref[...] = jnp.zeros_like(dg_ref)
            if with_post:
                dgp_ref[...] = jnp.zeros_like(dgp_ref)

        dv, dg = rms_bwd(dh_ref[...], x_ref[...], g_ref[...])
        dx = dres_ref[...] + dv
        dx_ref[...] = dx
        dg_ref[...] += dg
        if with_post:
            dy, dgp = rms_bwd(dx, y_ref[...], gp_ref[...])
            dy_ref[...] = dy.astype(BF16)
            dgp_ref[...] += dgp

    row = pl.BlockSpec((tm, d), lambda i: (i, 0))
    vec = pl.BlockSpec((1, d), lambda i: (0, 0))
    in_specs = [row, row, vec, row]
    out_specs = [row, vec]
    out_shape = [jax.ShapeDtypeStruct((s, d), F32), jax.ShapeDtypeStruct((1, d), F32)]
    args = [dh, x, gain, dres]
    if with_post:
        in_specs += [row, vec]
        out_specs += [row, vec]
        out_shape += [jax.ShapeDtypeStruct((s, d), BF16), jax.ShapeDtypeStruct((1, d), F32)]
        args += list(post)
    return _call(body, name=name, grid=(s // tm,), in_specs=in_specs, out_specs=out_specs, out_shape=out_shape,
                 args=args)[0]


def _fill_shifted_down(sh, rows):
    for b in range(1, SHIFTS):
        sh[b, SHIFTS:rows, :] = sh[0, SHIFTS - b:rows - b, :]


def _fill_shifted_up(sh, rows):
    for b in range(1, SHIFTS):
        sh[b, 0:rows - SHIFTS, :] = sh[0, b:rows - SHIFTS + b, :]


def _for_blocks(ts, w, fn):
    lb = min(LANE_BLOCK, w)
    for l0 in range(0, w, lb):
        def rows(rb, carry, l0=l0):
            fn(pl.multiple_of(rb * ROW_BLOCK, ROW_BLOCK), slice(l0, l0 + lb))
            return carry

        lax.fori_loop(0, ts // ROW_BLOCK, rows, 0)


TAP_SPAN = SHIFTS * ((CONV_B - 1) // SHIFTS)
WINDOW = ROW_BLOCK + TAP_SPAN


def _taps_of(b):
    return [(a, SHIFTS * a + b) for a in range((CONV_B - 1 - b) // SHIFTS + 1)]


def _conv31(sh, base, step, wt_ref, bias_ref, out_ref, ts, w):
    low = min(0, step * (TAP_SPAN // SHIFTS))

    def block(r0, lanes):
        acc = [jnp.zeros((SHIFTS, lanes.stop - lanes.start), F32) for _ in range(ROW_BLOCK // SHIFTS)]
        for b in range(SHIFTS):
            window = sh[b, pl.ds(pl.multiple_of(r0 + (base + low), SHIFTS), WINDOW), lanes]
            for a, j in (_taps_of(b) if step > 0 else reversed(_taps_of(b))):
                at = step * a - low
                wt = wt_ref[CONV_B - 1 - j, :, lanes]
                acc = [v + wt * window[at + SHIFTS * r:at + SHIFTS * (r + 1), :] for r, v in enumerate(acc)]
        for r, v in enumerate(acc):
            if bias_ref is not None:
                v = v + bias_ref[:, lanes]
            out_ref[pl.ds(pl.multiple_of(r0 + SHIFTS * r, SHIFTS), SHIFTS), lanes] = v

    _for_blocks(ts, w, block)


def _conv31_weight_grad(d_sh, x_sh, wacc, ts, w):
    def block(r0, lanes):
        d = d_sh[0, pl.ds(r0, ROW_BLOCK), lanes]
        for b in range(SHIFTS):
            window = x_sh[b, pl.ds(pl.multiple_of(r0 + (HALO_B - TAP_SPAN), SHIFTS), WINDOW), lanes]
            for a, j in _taps_of(b):
                at = TAP_SPAN - SHIFTS * a
                prod = d * window[at:at + ROW_BLOCK, :]
                part = prod[0:SHIFTS, :]
                for q in range(1, ROW_BLOCK // SHIFTS):
                    part = part + prod[q * SHIFTS:(q + 1) * SHIFTS, :]
                wacc[CONV_B - 1 - j, :, lanes] += part

    _for_blocks(ts, w, block)


def _even_forward_tile(p_ref, halo_ref, first, a_conv_ref, b_conv_ref, bias_ref, lng_ref, lnb_ref, qbuf, ysh, y1buf,
                       wb, w, ts):
    @pl.when(pl.program_id(0) == 0)
    def _():
        for k in range(CONV_B):
            wb[k] = jnp.broadcast_to(b_conv_ref[k:k + 1, :], (SHIFTS, w))

    def col(ref, k, rows=slice(None)):
        return ref[rows, k * w:(k + 1) * w]

    a_x, a_b, a_c, a_z = col(p_ref, 0), col(p_ref, 1), col(p_ref, 2), col(p_ref, 3)
    b_val, b_gate, b_z = col(p_ref, 4), col(p_ref, 5), col(p_ref, 6)
    keep = jnp.where(first, 0.0, 1.0)

    rows_a = slice(HALO_B - HALO_A, HALO_B)
    qbuf[0:HALO_A, :] = col(halo_ref, 2, rows_a) * col(halo_ref, 0, rows_a) * keep
    qbuf[HALO_A:HALO_A + ts, :] = a_c * a_x
    cq = jnp.zeros((ts, w), F32)
    for j in range(CONV_A):
        cq = cq + a_conv_ref[CONV_A - 1 - j:CONV_A - j, :] * qbuf[HALO_A - j:HALO_A - j + ts, :]
    ya = a_b * cq

    ysh[0, 0:HALO_B, :] = col(halo_ref, 4) * _sigmoid(col(halo_ref, 5)) * keep
    ysh[0, HALO_B:HALO_B + ts, :] = b_val * _sigmoid(b_gate)
    _fill_shifted_down(ysh, HALO_B + ts)
    _conv31(ysh, HALO_B, -SHIFTS, wb, bias_ref, y1buf, ts, w)
    yb1 = y1buf[...]
    xc = yb1 - _mean_last(yb1)
    rstd = lax.rsqrt(_mean_last(xc * xc) + EPS)
    xhat = xc * rstd
    yb2 = xhat * lng_ref[...] + lnb_ref[...]
    return dict(a_x=a_x, a_b=a_b, a_c=a_c, a_z=a_z, b_val=b_val, b_gate=b_gate, b_z=b_z,
                cq=cq, ya=ya, rstd=rstd, xhat=xhat, yb2=yb2)


def _even_specs(s, w, ts):
    tile = pl.BlockSpec((ts, 7 * w), lambda i: (i, 0))
    halo = pl.BlockSpec((HALO_B, 7 * w), lambda i: (jnp.maximum(i * (ts // HALO_B) - 1, 0), 0))
    return tile, halo


def _small_specs(shapes, index=lambda i: (0, 0)):
    return [pl.BlockSpec(sh, index) for sh in shapes]


def _even_mixer_fwd(p, w_out, x_res, gain, a_conv, b_conv, bias, ln_g, ln_b, name, comm=None):
    s, d = x_res.shape
    w = p.shape[1] // 7
    ts = _row_tile(s, 128)
    assert ts % HALO_B == 0

    def body(p_ref, halo_ref, wout_ref, x_ref, g_ref, ac_ref, bc_ref, bias_ref, lng_ref, lnb_ref,
             u_ref, xn_ref, y_ref, xhat_ref, rstd_ref, qbuf, ysh, wb):
        first = pl.program_id(0) == 0
        f = _even_forward_tile(p_ref, halo_ref, first, ac_ref, bc_ref, bias_ref, lng_ref, lnb_ref, qbuf, ysh, xhat_ref,
                               wb, w, ts)
        xhat_ref[...] = f["xhat"]
        rstd_ref[...] = jnp.broadcast_to(f["rstd"], (ts, LANES))
        yb3 = f["yb2"] * _sigmoid(f["yb2"])
        u_a = (f["ya"] * (f["a_z"] * _sigmoid(f["a_z"]))).astype(BF16)
        u_b = (yb3 * (f["b_z"] * _sigmoid(f["b_z"]))).astype(BF16)
        u_ref[:, 0:w] = u_a
        u_ref[:, w:2 * w] = u_b
        y = (jnp.dot(u_a, wout_ref[0:w, :], preferred_element_type=F32)
             + jnp.dot(u_b, wout_ref[w:2 * w, :], preferred_element_type=F32))
        r = lax.rsqrt(_mean_last(y * y) + EPS)
        y_ref[...] = y
        xn_ref[...] = x_ref[...] + (y * r) * g_ref[...]

    tile, halo = _even_specs(s, w, ts)
    row = pl.BlockSpec((ts, d), lambda i: (i, 0))
    return _call(
        body, name=name, grid=(s // ts,),
        in_specs=[tile, halo, pl.BlockSpec((2 * w, d), lambda i: (0, 0)), row, pl.BlockSpec((1, d), lambda i: (0, 0))]
        + _small_specs([(CONV_A, w), (CONV_B, w), (1, w), (1, w), (1, w)]),
        out_specs=[pl.BlockSpec((ts, 2 * w), lambda i: (i, 0)), row, row, pl.BlockSpec((ts, w), lambda i: (i, 0)),
                   pl.BlockSpec((ts, LANES), lambda i: (i, 0))],
        out_shape=[jax.ShapeDtypeStruct((s, 2 * w), BF16), jax.ShapeDtypeStruct((s, d), F32),
                   jax.ShapeDtypeStruct((s, d), F32), jax.ShapeDtypeStruct((s, w), F32),
                   jax.ShapeDtypeStruct((s, LANES), F32)],
        scratch_shapes=[pltpu.VMEM((HALO_A + ts, w), F32), pltpu.VMEM((SHIFTS, HALO_B + ts, w), F32),
                        pltpu.VMEM((CONV_B, SHIFTS, w), F32)],
        args=(p, p, w_out, x_res, gain, a_conv, b_conv, bias, ln_g, ln_b), comm=comm)


def _even_mixer_bwd(p, du, xhat, rstd, a_conv, b_conv, ln_g, ln_b, name, comm=None):
    s = p.shape[0]
    w = p.shape[1] // 7
    ts = _row_tile(s, EVEN_BWD_ROWS)
    nt = s // ts
    assert ts % HALO_B == 0

    def body(p_ref, halo_ref, du_ref, xhat_ref, rstd_ref, ac_ref, bc_ref, lng_ref, lnb_ref,
             dp_ref, dac_ref, dbc_ref, dbias_ref, dlng_ref, dlnb_ref,
             qbuf, ysh, dqbuf, dsh, dy0buf, wacc, carry_dq, carry_dy, wb):
        step = pl.program_id(0)
        first = step == nt - 1

        @pl.when(step == 0)
        def _():
            for ref in (dac_ref, dbias_ref, dlng_ref, dlnb_ref, wacc, carry_dq, carry_dy):
                ref[...] = jnp.zeros_like(ref)

        @pl.when(step == 0)
        def _():
            for k in range(CONV_B):
                wb[k] = jnp.broadcast_to(bc_ref[k:k + 1, :], (SHIFTS, w))

        keep = jnp.where(first, 0.0, 1.0)
        lane_blocks = [slice(l0, l0 + min(LANE_PASS, w)) for l0 in range(0, w, min(LANE_PASS, w))]

        def col(ref, k, lanes, rows=slice(None)):
            return ref[rows, k * w + lanes.start:k * w + lanes.stop]


        rows_a = slice(HALO_B - HALO_A, HALO_B)
        for lanes in lane_blocks:
            a_x, a_b, a_c, a_z = (col(p_ref, k, lanes) for k in range(4))
            qbuf[0:HALO_A, lanes] = col(halo_ref, 2, lanes, rows_a) * col(halo_ref, 0, lanes, rows_a) * keep
            qbuf[HALO_A:HALO_A + ts, lanes] = a_c * a_x
            cq = jnp.zeros((ts, lanes.stop - lanes.start), F32)
            for j in range(CONV_A):
                cq = cq + ac_ref[CONV_A - 1 - j:CONV_A - j, lanes] * qbuf[HALO_A - j:HALO_A - j + ts, lanes]
            s_az = _sigmoid(a_z)
            du_a = du_ref[:, lanes]
            d_ya = du_a * (a_z * s_az)
            dp_ref[:, 3 * w + lanes.start:3 * w + lanes.stop] = (du_a * (a_b * cq) * _dsilu(a_z, s_az)).astype(BF16)
            dp_ref[:, 1 * w + lanes.start:1 * w + lanes.stop] = (d_ya * cq).astype(BF16)
            d_cq = d_ya * a_b
            dqbuf[0:ts, lanes] = d_cq
            dqbuf[ts:ts + HALO_A, lanes] = carry_dq[:, lanes]
            carry_dq[:, lanes] = d_cq[0:HALO_A, :]
            d_q = jnp.zeros_like(cq)
            for o in range(CONV_A):
                d_q = d_q + ac_ref[CONV_A - 1 - o:CONV_A - o, lanes] * dqbuf[o:o + ts, lanes]
            for j in range(CONV_A):
                k = CONV_A - 1 - j
                dac_ref[k:k + 1, lanes] += _sum_rows(d_cq * qbuf[HALO_A - j:HALO_A - j + ts, lanes])
            dp_ref[:, 2 * w + lanes.start:2 * w + lanes.stop] = (d_q * a_x).astype(BF16)
            dp_ref[:, 0 * w + lanes.start:0 * w + lanes.stop] = (d_q * a_c).astype(BF16)
            ysh[0, 0:HALO_B, lanes] = col(halo_ref, 4, lanes) * _sigmoid(col(halo_ref, 5, lanes)) * keep
            ysh[0, HALO_B:HALO_B + ts, lanes] = col(p_ref, 4, lanes) * _sigmoid(col(p_ref, 5, lanes))
        _fill_shifted_down(ysh, HALO_B + ts)

        rstd = rstd_ref[:, 0:1]
        sum_dxh = jnp.zeros((ts, 1), F32)
        sum_dxh_xhat = jnp.zeros((ts, 1), F32)
        for lanes in lane_blocks:
            xhat = xhat_ref[:, lanes]
            yb2 = xhat * lng_ref[:, lanes] + lnb_ref[:, lanes]
            b_z = col(p_ref, 6, lanes)
            s_bz, s_y2 = _sigmoid(b_z), _sigmoid(yb2)
            du_b = du_ref[:, w + lanes.start:w + lanes.stop]
            dp_ref[:, 6 * w + lanes.start:6 * w + lanes.stop] = (du_b * (yb2 * s_y2) * _dsilu(b_z, s_bz)).astype(BF16)
            d_yb2 = du_b * (b_z * s_bz) * _dsilu(yb2, s_y2)
            dlng_ref[:, lanes] += _sum_rows(d_yb2 * xhat)
            dlnb_ref[:, lanes] += _sum_rows(d_yb2)
            d_xh = d_yb2 * lng_ref[:, lanes]
            sum_dxh = sum_dxh + jnp.sum(d_xh, axis=-1, keepdims=True)
            sum_dxh_xhat = sum_dxh_xhat + jnp.sum(d_xh * xhat, axis=-1, keepdims=True)
            dy0buf[:, lanes] = d_xh
        mean_dxh = sum_dxh * (1.0 / w)
        mean_dxh_xhat = sum_dxh_xhat * (1.0 / w)

        for lanes in lane_blocks:
            d_yb1 = rstd * (dy0buf[:, lanes] - mean_dxh - xhat_ref[:, lanes] * mean_dxh_xhat)
            dbias_ref[:, lanes] += _sum_rows(d_yb1)
            dsh[0, 0:ts, lanes] = d_yb1
            dsh[0, ts:ts + HALO_B, lanes] = carry_dy[:, lanes]
            carry_dy[:, lanes] = d_yb1[0:HALO_B, :]
        _fill_shifted_up(dsh, ts + HALO_B)
        _conv31(dsh, 0, SHIFTS, wb, None, dy0buf, ts, w)
        _conv31_weight_grad(dsh, ysh, wacc, ts, w)

        @pl.when(step == nt - 1)
        def _():
            for k in range(CONV_B):
                dbc_ref[k:k + 1, :] = _sum_rows(wacc[k])

        for lanes in lane_blocks:
            d_yb0 = dy0buf[:, lanes]
            s_g = _sigmoid(col(p_ref, 5, lanes))
            dp_ref[:, 4 * w + lanes.start:4 * w + lanes.stop] = (d_yb0 * s_g).astype(BF16)
            dp_ref[:, 5 * w + lanes.start:5 * w + lanes.stop] = (
                d_yb0 * col(p_ref, 4, lanes) * s_g * (1.0 - s_g)).astype(BF16)

    rev = lambda i: (nt - 1 - i, 0)
    tile = pl.BlockSpec((ts, 7 * w), rev)
    halo = pl.BlockSpec((HALO_B, 7 * w), lambda i: (jnp.maximum((nt - 1 - i) * (ts // HALO_B) - 1, 0), 0))
    small = [(CONV_A, w), (CONV_B, w), (1, w), (1, w), (1, w)]
    return _call(
        body, name=name, grid=(nt,),
        in_specs=[tile, halo, pl.BlockSpec((ts, 2 * w), rev), pl.BlockSpec((ts, w), rev),
                  pl.BlockSpec((ts, LANES), rev)]
        + _small_specs([(CONV_A, w), (CONV_B, w), (1, w), (1, w)]),
        out_specs=[pl.BlockSpec((ts, 7 * w), rev)] + _small_specs(small),
        out_shape=[jax.ShapeDtypeStruct((s, 7 * w), BF16)] + [jax.ShapeDtypeStruct(sh, F32) for sh in small],
        scratch_shapes=[pltpu.VMEM((HALO_A + ts, w), F32), pltpu.VMEM((SHIFTS, HALO_B + ts, w), F32),
                        pltpu.VMEM((ts + HALO_A, w), F32), pltpu.VMEM((SHIFTS, ts + HALO_B, w), F32),
                        pltpu.VMEM((ts, w), F32), pltpu.VMEM((CONV_B, SHIFTS, w), F32),
                        pltpu.VMEM((HALO_A, w), F32), pltpu.VMEM((HALO_B, w), F32),
                        pltpu.VMEM((CONV_B, SHIFTS, w), F32)],
        args=(p, p, du, xhat, rstd, a_conv, b_conv, ln_g, ln_b), comm=comm)


def _trailing_sums(buf_a, buf_b, cols, win, rows, first_out):
    src, dst, shift = buf_a, buf_b, 1
    while True:
        last = 2 * shift >= win
        lo = first_out if last else 0
        val = src[PAD_P + lo:PAD_P + rows, cols] + src[PAD_P + lo - shift:PAD_P + rows - shift, cols]
        if last:
            return val
        dst[PAD_P + lo:PAD_P + rows, cols] = val
        src, dst, shift = dst, src, 2 * shift


def _leading_sums(buf_a, buf_b, cols, win, rows, n_out):
    src, dst, shift = buf_a, buf_b, 1
    while True:
        last = 2 * shift >= win
        hi = n_out if last else rows
        val = src[0:hi, cols] + src[shift:hi + shift, cols]
        if last:
            return val
        dst[0:hi, cols] = val
        src, dst, shift = dst, src, 2 * shift


def _pool_forward_tile(p_ref, halo_ref, first, tile_index, cw_ref, cb_ref, cs_ref, vbuf, vtmp, c, gc, ts):
    vbuf[PAD_P:PAD_P + HALO_P, :] = halo_ref[...] * jnp.where(first, 0.0, 1.0)
    vbuf[PAD_P + HALO_P:PAD_P + HALO_P + ts, :] = p_ref[:, 0:c]
    pos = tile_index * ts + lax.broadcasted_iota(jnp.int32, (ts, 1), 0) + 1
    pooled, inv, gout = [], [], []
    for g, win in enumerate(POOL_WINDOWS):
        cols = slice(g * gc, (g + 1) * gc)
        acc = _trailing_sums(vbuf, vtmp, cols, win, HALO_P + ts, HALO_P)
        inv_g = 1.0 / jnp.minimum(pos, win).astype(F32)
        pooled_g = (acc * inv_g - p_ref[:, cols]).astype(BF16)
        pooled.append(pooled_g)
        inv.append(inv_g)
        gout.append(jnp.dot(pooled_g, cw_ref[g], preferred_element_type=F32) + cb_ref[:, cols])
    return pooled, inv, gout


def _odd_mixer_fwd(p, cw, cb, cs, name):
    s = p.shape[0]
    c = p.shape[1] // 2
    gc = c // N_GROUPS
    ts = _row_tile(s, 256)

    def body(p_ref, halo_ref, cw_ref, cb_ref, cs_ref, u_ref, pooled_ref, vbuf, vtmp):
        i = pl.program_id(0)

        @pl.when(i == 0)
        def _():
            vbuf[0:PAD_P, :] = jnp.zeros((PAD_P, c), F32)
            vtmp[0:PAD_P, :] = jnp.zeros((PAD_P, c), F32)

        pooled, _, gout = _pool_forward_tile(p_ref, halo_ref, i == 0, i, cw_ref, cb_ref, cs_ref, vbuf, vtmp, c, gc, ts)
        for g in range(N_GROUPS):
            cols = slice(g * gc, (g + 1) * gc)
            z = p_ref[:, c + g * gc:c + (g + 1) * gc]
            u_ref[:, cols] = (gout[g] * cs_ref[:, cols] * (z * _sigmoid(z))).astype(BF16)
            pooled_ref[:, cols] = pooled[g]

    return pl.pallas_call(
        body, name=name, grid=(s // ts,),
        in_specs=[pl.BlockSpec((ts, 2 * c), lambda i: (i, 0)),
                  pl.BlockSpec((HALO_P, c), lambda i: (jnp.maximum(i * (ts // HALO_P) - 1, 0), 0)),
                  pl.BlockSpec((N_GROUPS, gc, gc), lambda i: (0, 0, 0)),
                  pl.BlockSpec((1, c), lambda i: (0, 0)), pl.BlockSpec((1, c), lambda i: (0, 0))],
        out_specs=[pl.BlockSpec((ts, c), lambda i: (i, 0))] * 2,
        out_shape=[jax.ShapeDtypeStruct((s, c), BF16)] * 2,
        scratch_shapes=[pltpu.VMEM((PAD_P + HALO_P + ts, c), F32)] * 2,
        compiler_params=_params("arbitrary"),
    )(p, p, cw, cb, cs)


def _odd_mixer_bwd(p, pooled, du, cw, cb, cs, name):
    s = p.shape[0]
    c = p.shape[1] // 2
    gc = c // N_GROUPS
    ts = _row_tile(s, 256)
    nt = s // ts

    def body(z_ref, pooled_ref, du_ref, cw_ref, cb_ref, cs_ref, dp_ref, dcw_ref, dcb_ref, dcs_ref,
             ebuf, etmp, carry_e):
        step = pl.program_id(0)
        tile_index = nt - 1 - step

        @pl.when(step == 0)
        def _():
            for ref in (dcw_ref, dcb_ref, dcs_ref, carry_e):
                ref[...] = jnp.zeros_like(ref)
            for ref in (ebuf, etmp):
                ref[ts + HALO_P:ts + HALO_P + PAD_P, :] = jnp.zeros((PAD_P, c), F32)

        pos = tile_index * ts + lax.broadcasted_iota(jnp.int32, (ts, 1), 0) + 1
        ebuf[ts:ts + HALO_P, :] = carry_e[...]
        for g, win in enumerate(POOL_WINDOWS):
            cols = slice(g * gc, (g + 1) * gc)
            pooled_g = pooled_ref[:, cols]
            gout_g = jnp.dot(pooled_g, cw_ref[g], preferred_element_type=F32) + cb_ref[:, cols]
            z = z_ref[:, cols]
            sz = _sigmoid(z)
            du_g = du_ref[:, cols]
            scale = cs_ref[:, cols]
            d_y = du_g * (z * sz)
            dp_ref[:, c + g * gc:c + (g + 1) * gc] = (du_g * (gout_g * scale) * _dsilu(z, sz)).astype(BF16)
            dcs_ref[:, cols] += _sum_rows(d_y * gout_g)
            d_gout = d_y * scale
            dcb_ref[:, cols] += _sum_rows(d_gout)
            d_gout_b = d_gout.astype(BF16)
            dcw_ref[g] += lax.dot_general(pooled_g, d_gout_b, (((0,), (0,)), ((), ())), preferred_element_type=F32)
            d_pool = lax.dot_general(d_gout_b, cw_ref[g], (((1,), (1,)), ((), ())), preferred_element_type=F32)
            e = d_pool * (1.0 / jnp.minimum(pos, win).astype(F32))
            ebuf[0:ts, cols] = e
            carry_e[:, cols] = e[0:HALO_P, :]
            d_v = _leading_sums(ebuf, etmp, cols, win, ts + HALO_P, ts) - d_pool
            dp_ref[:, cols] = d_v.astype(BF16)

    rev = lambda i: (nt - 1 - i, 0)
    small = [(N_GROUPS, gc, gc), (1, c), (1, c)]
    return pl.pallas_call(
        body, name=name, grid=(nt,),
        in_specs=[pl.BlockSpec((ts, c), lambda i: (nt - 1 - i, 1)),
                  pl.BlockSpec((ts, c), rev),
                  pl.BlockSpec((ts, c), rev),
                  pl.BlockSpec((N_GROUPS, gc, gc), lambda i: (0, 0, 0)),
                  pl.BlockSpec((1, c), lambda i: (0, 0)), pl.BlockSpec((1, c), lambda i: (0, 0))],
        out_specs=[pl.BlockSpec((ts, 2 * c), rev),
                   pl.BlockSpec((N_GROUPS, gc, gc), lambda i: (0, 0, 0)),
                   pl.BlockSpec((1, c), lambda i: (0, 0)), pl.BlockSpec((1, c), lambda i: (0, 0))],
        out_shape=[jax.ShapeDtypeStruct((s, 2 * c), BF16)] + [jax.ShapeDtypeStruct(sh, F32) for sh in small],
        scratch_shapes=[pltpu.VMEM((ts + HALO_P + PAD_P, c), F32)] * 2 + [pltpu.VMEM((HALO_P, c), F32)],
        compiler_params=_params("arbitrary"),
    )(p, pooled, du, cw, cb, cs)


def _cast_into_slot(a, coords, name):
    r, cols = a.shape
    tr = _row_tile(r // 2, 1024)
    per = r // 2 // tr

    def body(co_ref, a_ref, o_ref):
        o_ref[0, 0] = a_ref[...].astype(BF16)

    return pl.pallas_call(
        body, name=name,
        grid_spec=pltpu.PrefetchScalarGridSpec(
            num_scalar_prefetch=1, grid=(2, per),
            in_specs=[pl.BlockSpec((tr, cols), lambda h, i, co: (h * per + i, 0))],
            out_specs=pl.BlockSpec((1, 1, tr, cols), lambda h, i, co: (co[0], h, i, 0))),
        out_shape=jax.ShapeDtypeStruct((N_SHARDS, 2, r // 2, cols), BF16),
        compiler_params=_params("arbitrary", "arbitrary"),
    )(coords, a)


def _chip_sum(g, other, coords, name):
    n_sh, _, r2, cols = g.shape
    tr = _row_tile(r2, 512)

    def body(co_ref, g_ref, o_ref, sum_ref, mine_ref):
        v = (g_ref[0, 0].astype(F32) + o_ref[0].astype(F32)).astype(BF16)
        sum_ref[0] = v

        @pl.when(pl.program_id(1) == co_ref[0])
        def _():
            mine_ref[0] = v

    piece = pl.BlockSpec((1, tr, cols), lambda i, s, co: (s, i, 0))
    return pl.pallas_call(
        body, name=name,
        grid_spec=pltpu.PrefetchScalarGridSpec(
            num_scalar_prefetch=1, grid=(r2 // tr, n_sh),
            in_specs=[pl.BlockSpec((1, 1, tr, cols), lambda i, s, co: (s, co[1], i, 0)), piece],
            out_specs=[piece, pl.BlockSpec((1, tr, cols), lambda i, s, co: (co[0], i, 0))]),
        out_shape=[jax.ShapeDtypeStruct((n_sh, r2, cols), BF16)] * 2,
        compiler_params=_params("arbitrary", "arbitrary"),
    )(coords, g, other)


def _shard_sum(pieces, coords, name):
    n_sh, r2, cols = pieces.shape
    tr = _row_tile(r2, 512)

    def body(co_ref, p_ref, o_ref):
        acc = p_ref[0].astype(F32)
        for k in range(1, n_sh):
            acc = acc + p_ref[k].astype(F32)
        o_ref[0] = acc

    return pl.pallas_call(
        body, name=name,
        grid_spec=pltpu.PrefetchScalarGridSpec(
            num_scalar_prefetch=1, grid=(r2 // tr,),
            in_specs=[pl.BlockSpec((n_sh, tr, cols), lambda i, co: (0, i, 0))],
            out_specs=pl.BlockSpec((1, tr, cols), lambda i, co: (co[1], i, 0))),
        out_shape=jax.ShapeDtypeStruct((2, r2, cols), F32),
        compiler_params=_params("arbitrary"),
    )(coords, pieces)


def _sum_small(a, name):
    n, r, cols = a.shape

    def body(a_ref, o_ref):
        acc = a_ref[0]
        for k in range(1, n):
            acc = acc + a_ref[k]
        o_ref[...] = acc

    return pl.pallas_call(
        body, name=name,
        in_specs=[pl.BlockSpec((n, r, cols), lambda: (0, 0, 0))],
        out_specs=pl.BlockSpec((r, cols), lambda: (0, 0)),
        out_shape=jax.ShapeDtypeStruct((r, cols), F32),
        compiler_params=_params(),
    )(a)


def _adamw_step(w, g, m, v):
    m = ADAM_B1 * m + (1.0 - ADAM_B1) * g
    v = ADAM_B2 * v + (1.0 - ADAM_B2) * (g * g)
    m_hat = m / (1.0 - ADAM_B1 ** ADAM_STEP)
    v_hat = v / (1.0 - ADAM_B2 ** ADAM_STEP)
    return -ADAM_LR * (m_hat / (jnp.sqrt(v_hat) + ADAM_EPS) + ADAM_WD * w), m, v


def _adamw(w, g, m, v, name):
    r, cols = w.shape
    tr = _row_tile(r, 256) if r % SUBLANES_BF16 == 0 else r

    def body(w_ref, g_ref, m_ref, v_ref, go_ref, d_ref, nm_ref, nv_ref):
        g = g_ref[...]
        go_ref[...] = g
        d_ref[...], nm_ref[...], nv_ref[...] = _adamw_step(w_ref[...], g, m_ref[...], v_ref[...])

    blk = pl.BlockSpec((tr, cols), lambda i: (i, 0))
    return pl.pallas_call(
        body, name=name, grid=(r // tr,),
        in_specs=[blk] * 4, out_specs=[blk] * 4,
        out_shape=[jax.ShapeDtypeStruct((r, cols), F32)] * 4,
        compiler_params=_params("arbitrary"),
    )(w, g, m, v)


def _adamw_many(ws, gs, ms, vs, name):
    n = len(ws)

    def body(*refs):
        w_refs, g_refs, m_refs, v_refs = (refs[t * n:(t + 1) * n] for t in range(4))
        d_refs, nm_refs, nv_refs = (refs[(4 + t) * n:(5 + t) * n] for t in range(3))
        for k in range(n):
            d_refs[k][...], nm_refs[k][...], nv_refs[k][...] = _adamw_step(
                w_refs[k][...], g_refs[k][...], m_refs[k][...], v_refs[k][...])

    specs = [pl.BlockSpec(a.shape, lambda: (0, 0)) for a in ws]
    out = pl.pallas_call(
        body, name=name, in_specs=specs * 4, out_specs=specs * 3,
        out_shape=[jax.ShapeDtypeStruct(a.shape, F32) for a in ws] * 3,
        compiler_params=_params(),
    )(*ws, *gs, *ms, *vs)
    return out[:n], out[n:2 * n], out[2 * n:]


def _place():
    x, y, c = lax.axis_index("x"), lax.axis_index("y"), lax.axis_index("c")
    other_chips = [(1 - x, y), (x, 1 - y), (1 - x, 1 - y)]
    return x, y, c, other_chips


def _chip(xy):
    return 2 * xy[0] + xy[1]


def _remote(src, dst, send_sem, recv_sem, to):
    return pltpu.make_async_remote_copy(src_ref=src, dst_ref=dst, send_sem=send_sem, recv_sem=recv_sem,
                                        device_id=to, device_id_type=MESH)


def _gather_ici(ctx, k, j, start):
    (x, y, c, chips), b, send, recv = ctx
    if j < 2:
        chip, to = (_chip((x, y)) if start else _chip(chips[j])), (*chips[j], c)
    else:
        chip = 2 * (x ^ c) + (y ^ (1 - c)) if start else _chip(chips[2])
        to = (x ^ (1 - c), y ^ c, c)
    blk = b[k].at[chip, c]
    return _remote(blk, blk, send.at[6 * k + j], recv.at[6 * k + j], to)


def _gather_d2d(ctx, k, j, start):
    (x, y, c, chips), b, send, recv = ctx
    blk = b[k].at[_chip(chips[j]), c if start else 1 - c]
    return _remote(blk, blk, send.at[6 * k + 3 + j], recv.at[6 * k + 3 + j], (x, y, 1 - c))


def _gather_small(ctx, n, j, start):
    (x, y, c, chips), b, send, recv = ctx
    blk = b[n].at[_chip((x, y)) if start else _chip(chips[j])]
    return _remote(blk, blk, send.at[6 * n + j], recv.at[6 * n + j], (*chips[j], c))


def _gather_neighbours_landed(ctx, k):
    for j in range(2):
        _gather_ici(ctx, k, j, False).wait_recv()
    _gather_ici(ctx, k, 2, True).start()
    for j in range(2):
        _gather_d2d(ctx, k, j, True).start()


def _gather_diagonal_landed(ctx, k):
    _gather_ici(ctx, k, 2, False).wait_recv()
    _gather_d2d(ctx, k, 2, True).start()


def _gather_comm(bufs, relay_at, forward_at):
    n = len(bufs)

    def start(srcs, b, send, recv):
        for k in range(n):
            for j in range(2):
                _gather_ici((_place(), b, send, recv), k, j, True).start()

    def relay(srcs, b, send, recv):
        for k in range(n):
            _gather_neighbours_landed((_place(), b, send, recv), k)

    def forward(srcs, b, send, recv):
        for k in range(n):
            _gather_diagonal_landed((_place(), b, send, recv), k)

    def finish(srcs, b, send, recv):
        ctx = (_place(), b, send, recv)
        for k in range(n):
            for j in range(3):
                _gather_d2d(ctx, k, j, False).wait_recv()
                _gather_ici(ctx, k, j, True).wait_send()
                _gather_d2d(ctx, k, j, True).wait_send()

    return _Comm([], bufs, 6 * n, [(0, start), (relay_at, relay), (forward_at, forward)], finish)


def _rmsnorm(x, gain, name):
    s, d = x.shape
    tm = _row_tile(s, 512)

    def body(x_ref, g_ref, h_ref):
        xv = x_ref[...]
        r = lax.rsqrt(_mean_last(xv * xv) + EPS)
        h_ref[...] = (xv * r * g_ref[...]).astype(BF16)

    return pl.pallas_call(
        body, name=name, grid=(s // tm,),
        in_specs=[pl.BlockSpec((tm, d), lambda i: (i, 0)), pl.BlockSpec((1, d), lambda i: (0, 0))],
        out_specs=pl.BlockSpec((tm, d), lambda i: (i, 0)),
        out_shape=jax.ShapeDtypeStruct((s, d), BF16),
        compiler_params=_params("arbitrary"),
    )(x, gain)


def _gathered_in_proj(h, bufs, small, order, name):
    s, d = h.shape
    n_sh, _, r2, ns = bufs[0].shape
    assert d == 2 * r2
    n = len(bufs)
    tm = _row_tile(s, 512)
    n_i = s // tm
    hook_i = max(n_i - 2, 0)
    n_sems = 6 * n + 3

    def body(order_ref, h_ref, *rest):
        p_ref = rest[n + 1]
        b = rest[n + 2:2 * n + 3]
        w_vmem, w_sems, send, recv = rest[2 * n + 3:]
        j, i = pl.program_id(0), pl.program_id(1)
        ctx = (_place(), b, send, recv)

        def fetch(q):
            return pltpu.make_async_copy(b[0].at[order_ref[q]], w_vmem.at[q % 2], w_sems.at[q % 2])

        @pl.when((j == 0) & (i == 0))
        def _():
            for k in range(n):
                for peer in range(2):
                    _gather_ici(ctx, k, peer, True).start()
            for peer in range(3):
                _gather_small(ctx, n, peer, True).start()
            fetch(0).start()
            fetch(0).wait()

        for q in range(1, n_sh):
            @pl.when((j == q - 1) & (i == hook_i))
            def _(q=q):
                if q == 1:
                    _gather_neighbours_landed(ctx, 0)
                if q == 2:
                    for k in range(1, n):
                        _gather_neighbours_landed(ctx, k)
                if q == 3:
                    for k in range(n):
                        _gather_diagonal_landed(ctx, k)
                _gather_d2d(ctx, 0, q - 1, False).wait_recv()
                fetch(q).start()

            @pl.when((j == q) & (i == 0))
            def _(q=q):
                fetch(q).wait()

        wv = w_vmem.at[j % 2]
        p_ref[...] = (jnp.dot(h_ref[:, 0:r2], wv[0], preferred_element_type=F32)
                      + jnp.dot(h_ref[:, r2:d], wv[1], preferred_element_type=F32))

        @pl.when((j == n_sh - 1) & (i == n_i - 1))
        def _():
            for peer in range(3):
                _gather_small(ctx, n, peer, False).wait_recv()
                _gather_small(ctx, n, peer, True).wait_send()
            for k in range(n):
                for peer in range(3):
                    if k > 0:
                        _gather_d2d(ctx, k, peer, False).wait_recv()
                    _gather_ici(ctx, k, peer, True).wait_send()
                    _gather_d2d(ctx, k, peer, True).wait_send()

    all_bufs = list(bufs) + [small]
    out = pl.pallas_call(
        body, name=name,
        grid_spec=pltpu.PrefetchScalarGridSpec(
            num_scalar_prefetch=1, grid=(n_sh, n_i),
            in_specs=[pl.BlockSpec((tm, d), lambda j, i, o: (i, 0))] + [ANY] * (n + 1),
            out_specs=[pl.BlockSpec((tm, ns), lambda j, i, o: (i, o[j]))] + [ANY] * (n + 1),
            scratch_shapes=[pltpu.VMEM((2, 2, r2, ns), BF16), pltpu.SemaphoreType.DMA((2,)),
                            pltpu.SemaphoreType.DMA((n_sems,)), pltpu.SemaphoreType.DMA((n_sems,))]),
        out_shape=[jax.ShapeDtypeStruct((s, n_sh * ns), F32)]
        + [jax.ShapeDtypeStruct(a.shape, a.dtype) for a in all_bufs],
        input_output_aliases={2 + t: 1 + t for t in range(n + 1)},
        compiler_params=_params("arbitrary", "arbitrary"),
    )(order, h, *all_bufs)
    return out[0], list(out[1:])


def _exchange_comm(grads):
    n = len(grads)
    landing = [lax.empty((N_SHARDS,) + a.shape[2:], a.dtype) for a in grads]

    def copies(srcs, b, send, recv):
        x, y, c, _ = _place()
        return [_remote(srcs[k].at[s, 1 - c], b[k].at[s], send.at[N_SHARDS * k + s], recv.at[N_SHARDS * k + s],
                        (x, y, 1 - c)) for k in range(n) for s in range(N_SHARDS)]

    def start(srcs, b, send, recv):
        for cp in copies(srcs, b, send, recv):
            cp.start()

    def finish(srcs, b, send, recv):
        for cp in copies(srcs, b, send, recv):
            cp.wait()

    return _Comm(grads, landing, N_SHARDS * n, [(0, start)], finish)


def _scatter_comm(chip_sums, landing):
    n = len(chip_sums)

    def big(srcs, b, send, recv, k, j, start):
        x, y, c, chips = _place()
        dst = b[k].at[_chip((x, y)) if start else _chip(chips[j])]
        return _remote(srcs[k].at[_chip(chips[j])], dst, send.at[3 * k + j], recv.at[3 * k + j], (*chips[j], c))

    def start(srcs, b, send, recv):
        for k in range(n):
            for j in range(3):
                big(srcs, b, send, recv, k, j, True).start()

    def finish(srcs, b, send, recv):
        for k in range(n):
            for j in range(3):
                big(srcs, b, send, recv, k, j, False).wait_recv()
                big(srcs, b, send, recv, k, j, True).wait_send()

    return _Comm(chip_sums, landing, 3 * n, [(0, start)], finish)


def _join_comm(halves, small):
    n = len(halves)
    flips = [(fx, fy, fc) for fx in (0, 1) for fy in (0, 1) for fc in (0, 1)][1:]

    def half(b, send, recv, k, start):
        x, y, c, _ = _place()
        return _remote(b[k].at[c], b[k].at[c if start else 1 - c], send.at[k], recv.at[k], (x, y, 1 - c))

    def small_copy(b, send, recv, q, start):
        x, y, c, _ = _place()
        px, py, pc = x ^ flips[q][0], y ^ flips[q][1], c ^ flips[q][2]
        blk = b[n].at[4 * x + 2 * y + c if start else 4 * px + 2 * py + pc]
        return _remote(blk, blk, send.at[n + q], recv.at[n + q], (px, py, pc))

    def start(srcs, b, send, recv):
        for q in range(len(flips)):
            small_copy(b, send, recv, q, True).start()
        for k in range(n):
            half(b, send, recv, k, True).start()

    def finish(srcs, b, send, recv):
        for q in range(len(flips)):
            small_copy(b, send, recv, q, False).wait()
        for k in range(n):
            half(b, send, recv, k, False).wait()

    return _Comm([], list(halves) + [small], n + len(flips), [(0, start)], finish)


def _flat_rows(parts):
    flat = jnp.concatenate([p.reshape(-1) for p in parts])
    assert flat.shape[0] % LANES == 0
    return flat.reshape(-1, LANES)


def _unflatten(flat, shapes):
    out, at = [], 0
    for sh in shapes:
        size = 1
        for dim in sh:
            size *= dim
        out.append(flat[at:at + size].reshape(sh))
        at += size
    assert at == flat.shape[0], (at, flat.shape)
    return out


def _col_shards_to_full(a, rows):
    q = a.shape[1] // rows
    return a.reshape(N_SHARDS, rows, q).transpose(1, 0, 2).reshape(rows, N_SHARDS * q)


def _my_col_shard(full, chip):
    rows, cols = full.shape
    q = cols // N_SHARDS
    return lax.dynamic_index_in_dim(full.reshape(rows, N_SHARDS, q), chip, axis=1, keepdims=False)


def kernel(x, e_norm_pre, e_norm_post, e_w_in, e_a_conv, e_b_conv, e_b_conv_bias, e_b_ln_g, e_b_ln_b, e_w_out, o_norm_pre, o_norm_post, o_w_in, o_c_w, o_c_b, o_c_scale, o_w_out, loss_target, m_e_norm_pre, m_e_norm_post, m_e_w_in, m_e_a_conv, m_e_b_conv, m_e_b_conv_bias, m_e_b_ln_g, m_e_b_ln_b, m_e_w_out, m_o_norm_pre, m_o_norm_post, m_o_w_in, m_o_c_w, m_o_c_b, m_o_c_scale, m_o_w_out, v_e_norm_pre, v_e_norm_post, v_e_w_in, v_e_a_conv, v_e_b_conv, v_e_b_conv_bias, v_e_b_ln_g, v_e_b_ln_b, v_e_w_out, v_o_norm_pre, v_o_norm_post, v_o_w_in, v_o_c_w, v_o_c_b, v_o_c_scale, v_o_w_out):
    _, s, d = x.shape
    w = d // 2
    c = d
    gc = c // N_GROUPS
    wq, cq, gq = w // N_SHARDS, c // N_SHARDS, gc // N_SHARDS
    chip = 2 * lax.axis_index("x") + lax.axis_index("y")
    core = lax.axis_index("c")
    x2 = x.reshape(s, d)
    target = loss_target.reshape(s, d)

    big_w = [e_w_in[0], e_w_out[0], o_w_in[0], o_c_w[0].reshape(N_GROUPS * gq, gc), o_w_out[0]]
    big_m = [m_e_w_in[0], m_e_w_out[0], m_o_w_in[0], m_o_c_w[0].reshape(N_GROUPS * gq, gc), m_o_w_out[0]]
    big_v = [v_e_w_in[0], v_e_w_out[0], v_o_w_in[0], v_o_c_w[0].reshape(N_GROUPS * gq, gc), v_o_w_out[0]]
    coords = jnp.stack([chip, core]).astype(jnp.int32)
    slots = [_cast_into_slot(a, coords, "cast_%d" % k) for k, a in enumerate(big_w)]
    sharded_small = _flat_rows([e_a_conv[0], e_b_conv[0], o_norm_pre, o_norm_post, o_c_scale, o_c_b[0]])
    small_slots = lax.dynamic_update_index_in_dim(jnp.zeros((N_SHARDS,) + sharded_small.shape, F32), sharded_small,
                                                  chip, 0)
    xi, yi = lax.axis_index("x"), lax.axis_index("y")
    order = jnp.stack([chip, 2 * (1 - xi) + yi, 2 * xi + (1 - yi), 2 * (1 - xi) + (1 - yi)]).astype(jnp.int32)
    h0 = _rmsnorm(x2, e_norm_pre, "e_pre_norm")
    p0, (e_w_in_g, e_w_out_g, small_g4) = _gathered_in_proj(h0, slots[:2], small_slots, order, "e_in_proj")
    e_w_in_sm = e_w_in_g.reshape((N_SHARDS,) + big_w[0].shape)
    e_w_out_f = e_w_out_g.reshape(w + w, d)
    sm = small_g4.reshape(N_SHARDS, -1)
    at = [0]

    def take(rows, q):
        blk = sm[:, at[0]:at[0] + rows * q]
        at[0] += rows * q
        return _col_shards_to_full(blk, rows)

    a_conv_f = take(CONV_A, wq)
    b_conv_f = take(CONV_B, wq)
    o_pre_f = take(1, cq)
    o_post_f = take(1, cq)
    cs_f = take(1, cq)
    cb_f = take(N_GROUPS, gq).reshape(1, c)

    mixer_steps = s // _row_tile(s, 128)
    (u0, x1, y0, xhat0, rstd0), odd_g = _even_mixer_fwd(
        p0, e_w_out_f, x2, e_norm_post, a_conv_f, b_conv_f, e_b_conv_bias, e_b_ln_g, e_b_ln_b, "e_mixer_out_proj",
        comm=_gather_comm(slots[2:], mixer_steps // 2, (25 * mixer_steps) // 32))
    o_w_in_sm = odd_g[0].reshape((N_SHARDS,) + big_w[2].shape)
    cw_f = odd_g[1].reshape(N_SHARDS, N_GROUPS, gq, gc).transpose(1, 0, 2, 3).reshape(N_GROUPS, gc, gc)
    o_w_out_f = odd_g[2].reshape(c, d)
    p1, h1 = _norm_matmul(x1, o_pre_f, o_w_in_sm, "o_in_proj")
    u1, pooled1 = _odd_mixer_fwd(p1, cw_f, cb_f, cs_f, "o_mixer_fwd")
    d_y1, d_x2, d_o_post, loss_part = _matmul_post_loss(u1, o_w_out_f, x1, o_post_f, target, "o_out_proj_loss")

    def as_pieces(g, k):
        return g.reshape(N_SHARDS, 2, big_w[k].shape[0] // 2, big_w[k].shape[1])

    def chip_sums(ks, pieces, from_sibling):
        both = [_chip_sum(g, o, coords, "chip_sum_%d" % k) for k, g, o in zip(ks, pieces, from_sibling)]
        return [b[0] for b in both], [b[1] for b in both]

    g_o_w_out = _matmul_tn(u1, d_y1, 1, "o_w_out_grad")
    d_u1, _ = _matmul_nt(d_y1, o_w_out_f[None], "o_out_proj_bwd")
    d_p1, d_cw, d_cb, d_cs = _odd_mixer_bwd(p1, pooled1, d_u1, cw_f, cb_f, cs_f, "o_mixer_bwd")
    g_o_w_in = _matmul_tn(h1, d_p1, N_SHARDS, "o_w_in_grad")
    g_cw = d_cw.reshape(N_GROUPS, N_SHARDS, gq, gc).transpose(1, 0, 2, 3).astype(BF16)
    pieces_o = [as_pieces(g_o_w_in, 2), as_pieces(g_cw, 3), as_pieces(g_o_w_out, 4)]
    d_h1, sibling_o = _matmul_nt(d_p1, o_w_in_sm, "o_in_proj_bwd", comm=_exchange_comm(pieces_o))
    d_x1, d_o_pre, d_y0, d_e_post = _norm_bwd(d_h1, x1, o_pre_f, d_x2, "o_pre_norm_bwd", post=(y0, e_norm_post))

    pieces_e = [as_pieces(_matmul_tn(u0, d_y0, 1, "e_w_out_grad"), 1)]
    d_u0, sibling_e = _matmul_nt(d_y0, e_w_out_f[None], "e_out_proj_bwd", comm=_exchange_comm(pieces_e))
    sums_a, landing_a = chip_sums([1, 2, 3, 4], pieces_e + pieces_o, sibling_e + sibling_o)
    (d_p0, d_a_conv, d_b_conv, d_bias, d_ln_g, d_ln_b), landed_a = _even_mixer_bwd(
        p0, d_u0, xhat0, rstd0, a_conv_f, b_conv_f, e_b_ln_g, e_b_ln_b, "e_mixer_bwd",
        comm=_scatter_comm(sums_a, landing_a))
    pieces_b = [as_pieces(_matmul_tn(h0, d_p0, N_SHARDS, "e_w_in_grad"), 0)]
    sums_b, landing_b = chip_sums([0], pieces_b, _comm_only(_exchange_comm(pieces_b), "exchange_core_halves"))
    d_h0, landed_b = _matmul_nt(d_p0, e_w_in_sm, "e_in_proj_bwd", comm=_scatter_comm(sums_b, landing_b))
    grad_x, d_e_pre = _norm_bwd(d_h0, x2, e_norm_pre, d_x1, "e_pre_norm_bwd")

    landed = landed_b + landed_a
    reduced = [_shard_sum(sc, coords, "shard_sum_%d" % k) for k, sc in enumerate(landed)]
    small_parts = _flat_rows([loss_part[0], d_e_pre, d_e_post, d_bias, d_ln_g, d_ln_b, d_a_conv, d_b_conv,
                              d_o_pre, d_o_post, d_cs, d_cb])
    small_rows = lax.dynamic_update_index_in_dim(jnp.zeros((N_DEVICES,) + small_parts.shape, F32), small_parts,
                                                 2 * chip + core, 0)
    joined = _comm_only(_join_comm(reduced, small_rows), "join_core_halves")
    big_g = [j.reshape(a.shape) for j, a in zip(joined[:5], big_w)]
    small_sum = _sum_small(joined[5], "small_sum").reshape(-1)
    (loss_row, g_e_pre, g_e_post, g_bias, g_ln_g, g_ln_b, g_a_conv_f, g_b_conv_f, g_o_pre_f, g_o_post_f, g_cs_f,
     g_cb_f) = _unflatten(small_sum, [(LANES,), (1, d), (1, d), (1, w), (1, w), (1, w), (CONV_A, w), (CONV_B, w),
                                      (1, c), (1, c), (1, c), (1, c)])
    loss = loss_row[0]
    g_a_conv = _my_col_shard(g_a_conv_f, chip)
    g_b_conv = _my_col_shard(g_b_conv_f, chip)
    g_o_pre = _my_col_shard(g_o_pre_f, chip)
    g_o_post = _my_col_shard(g_o_post_f, chip)
    g_cs = _my_col_shard(g_cs_f, chip)
    g_cb = _my_col_shard(g_cb_f.reshape(N_GROUPS, gc), chip)

    big_upd = [_adamw(wt, g, m, v, "adamw_%d" % k) for k, (wt, g, m, v) in enumerate(zip(big_w, big_g, big_m, big_v))]
    big_g = [u[0] for u in big_upd]
    big_upd = [u[1:] for u in big_upd]
    small_w = [e_norm_pre, e_norm_post, e_b_conv_bias, e_b_ln_g, e_b_ln_b, e_a_conv[0], e_b_conv[0],
               o_norm_pre, o_norm_post, o_c_b[0], o_c_scale]
    small_m = [m_e_norm_pre, m_e_norm_post, m_e_b_conv_bias, m_e_b_ln_g, m_e_b_ln_b, m_e_a_conv[0], m_e_b_conv[0],
               m_o_norm_pre, m_o_norm_post, m_o_c_b[0], m_o_c_scale]
    small_v = [v_e_norm_pre, v_e_norm_post, v_e_b_conv_bias, v_e_b_ln_g, v_e_b_ln_b, v_e_a_conv[0], v_e_b_conv[0],
               v_o_norm_pre, v_o_norm_post, v_o_c_b[0], v_o_c_scale]
    small_g = [g_e_pre, g_e_post, g_bias, g_ln_g, g_ln_b, g_a_conv, g_b_conv, g_o_pre, g_o_post, g_cb, g_cs]
    small_delta, small_new_m, small_new_v = _adamw_many(small_w, small_g, small_m, small_v, "adamw_small")

    def ordered(small, big):
        (n_pre, n_post, bias, ln_g, ln_b, a_conv, b_conv, o_pre, o_post, cb, cs) = small
        (w_in, w_out, ow_in, cw, ow_out) = big
        return [n_pre, n_post, w_in[None], a_conv[None], b_conv[None], bias, ln_g, ln_b, w_out[None], o_pre, o_post,
                ow_in[None], cw.reshape(1, N_GROUPS, gq, gc), cb[None], cs, ow_out[None]]

    grads = ordered(small_g, big_g)
    deltas = ordered(small_delta, [u[0] for u in big_upd])
    new_m = ordered(small_new_m, [u[1] for u in big_upd])
    new_v = ordered(small_new_v, [u[2] for u in big_upd])
    return (loss, grad_x.reshape(1, s, d), *grads, *deltas, *new_m, *new_v)
```

```python
import functools

import jax
import jax.numpy as jnp
from jax import lax
from jax.experimental import pallas as pl
from jax.experimental.pallas import tpu as pltpu

F32 = jnp.float32
BF16 = jnp.bfloat16
MESH = pl.DeviceIdType.MESH

EPS = 1e-6
CONV_A = 3
CONV_B = 31
POOL_WINDOWS = (2, 4, 8, 16)
N_GROUPS = len(POOL_WINDOWS)
N_SHARDS = 4
N_DEVICES = 8
ADAM_LR = 0.001
ADAM_B1 = 0.9
ADAM_B2 = 0.999
ADAM_EPS = 1e-08
ADAM_WD = 0.01
ADAM_STEP = 10

LANES = 128
SUBLANES_BF16 = 16
HALO_A = 8
HALO_B = 32
HALO_P = 16
PAD_P = 8
SHIFTS = 8
ROW_BLOCK = 32
LANE_BLOCK = 256
LANE_PASS = 256
VMEM_LIMIT = 56 * 1024 * 1024
TN_ACC_BYTES = 8 * 1024 * 1024
EVEN_BWD_ROWS = 128


def _row_tile(n, pref):
    t = max(min(n, pref) // SUBLANES_BF16, 1) * SUBLANES_BF16
    while t > SUBLANES_BF16 and (n % t or t % SUBLANES_BF16):
        t -= SUBLANES_BF16
    assert n % t == 0, (n, pref)
    return t


def _col_chunk(n, pref):
    t = (min(n, pref) // LANES) * LANES
    while t > LANES and n % t:
        t -= LANES
    assert t >= LANES and n % t == 0, (n, pref)
    return t


def _params(*sem):
    return pltpu.CompilerParams(dimension_semantics=tuple(sem) if sem else None, vmem_limit_bytes=VMEM_LIMIT)


ANY = pl.BlockSpec(memory_space=pl.ANY)


class _Comm:
    def __init__(self, srcs, bufs, n_sems, phases, finish):
        self.srcs, self.bufs, self.n_sems, self.phases, self.finish = list(srcs), list(bufs), n_sems, phases, finish


def _call(body, *, name, grid, in_specs, out_specs, out_shape, args, scratch_shapes=(), comm=None):
    params = _params(*(("arbitrary",) * len(grid)))
    if comm is None:
        out = pl.pallas_call(body, name=name, grid=grid, in_specs=in_specs, out_specs=out_specs, out_shape=out_shape,
                             scratch_shapes=scratch_shapes, compiler_params=params)(*args)
        return list(out), []
    n_in, n_out, n_scr = len(in_specs), len(out_specs), len(scratch_shapes)
    ns, nb = len(comm.srcs), len(comm.bufs)
    total = 1
    for size in grid:
        total *= size

    def fused(*refs):
        ins, srcs = refs[:n_in], refs[n_in:n_in + ns]
        at = n_in + ns + nb
        outs, bufs = refs[at:at + n_out], refs[at + n_out:at + n_out + nb]
        scratch = refs[at + n_out + nb:at + n_out + nb + n_scr]
        send_sems, recv_sems = refs[-2:]
        step = 0
        for axis, size in enumerate(grid):
            step = step * size + pl.program_id(axis)
        for when, fn in comm.phases:
            pl.when(step == when)(functools.partial(fn, srcs, bufs, send_sems, recv_sems))
        body(*ins, *outs, *scratch)
        pl.when(step == total - 1)(functools.partial(comm.finish, srcs, bufs, send_sems, recv_sems))

    out = pl.pallas_call(
        fused, name=name, grid=grid,
        in_specs=list(in_specs) + [ANY] * (ns + nb), out_specs=list(out_specs) + [ANY] * nb,
        out_shape=list(out_shape) + [jax.ShapeDtypeStruct(b.shape, b.dtype) for b in comm.bufs],
        input_output_aliases={n_in + ns + i: n_out + i for i in range(nb)},
        scratch_shapes=list(scratch_shapes) + [pltpu.SemaphoreType.DMA((comm.n_sems,))] * 2,
        compiler_params=params,
    )(*args, *comm.srcs, *comm.bufs)
    return list(out[:n_out]), list(out[n_out:])


def _comm_only(comm, name):
    ns, nb = len(comm.srcs), len(comm.bufs)

    def body(*refs):
        srcs, bufs = refs[:ns], refs[ns + nb:ns + 2 * nb]
        send_sems, recv_sems = refs[-2:]
        for _, fn in comm.phases:
            fn(srcs, bufs, send_sems, recv_sems)
        comm.finish(srcs, bufs, send_sems, recv_sems)

    return pl.pallas_call(
        body, name=name, in_specs=[ANY] * (ns + nb), out_specs=[ANY] * nb,
        out_shape=[jax.ShapeDtypeStruct(b.shape, b.dtype) for b in comm.bufs],
        input_output_aliases={ns + i: i for i in range(nb)},
        scratch_shapes=[pltpu.SemaphoreType.DMA((comm.n_sems,))] * 2,
    )(*comm.srcs, *comm.bufs)


def _sigmoid(v):
    return jax.nn.sigmoid(v)


def _dsilu(v, s):
    return s * (1.0 + v * (1.0 - s))


def _mean_last(v):
    return jnp.mean(v, axis=-1, keepdims=True)


def _sum_rows(v):
    return jnp.sum(v, axis=0, keepdims=True)


def _norm_matmul(x, gain, w_sm, name):
    s, d = x.shape
    n_sh, _, ns = w_sm.shape
    tm = _row_tile(s, 1024)

    def body(x_ref, g_ref, w_ref, p_ref, h_ref):
        @pl.when(pl.program_id(1) == 0)
        def _():
            xv = x_ref[...]
            r = lax.rsqrt(_mean_last(xv * xv) + EPS)
            h_ref[...] = (xv * r * g_ref[...]).astype(BF16)

        p_ref[...] = jnp.dot(h_ref[...], w_ref[0], preferred_element_type=F32)

    return _call(
        body, name=name, grid=(s // tm, n_sh),
        in_specs=[pl.BlockSpec((tm, d), lambda i, j: (i, 0)),
                  pl.BlockSpec((1, d), lambda i, j: (0, 0)),
                  pl.BlockSpec((1, d, ns), lambda i, j: (j, 0, 0))],
        out_specs=[pl.BlockSpec((tm, ns), lambda i, j: (i, j)),
                   pl.BlockSpec((tm, d), lambda i, j: (i, 0))],
        out_shape=[jax.ShapeDtypeStruct((s, n_sh * ns), F32), jax.ShapeDtypeStruct((s, d), BF16)],
        args=(x, gain, w_sm))[0]


def _matmul_post_loss(u, w, x_res, gain, target, name):
    s, k = u.shape
    d = w.shape[1]
    tm = _row_tile(s, 512)
    half = tm // 2

    def body(u_ref, w_ref, x_ref, g_ref, t_ref, dy_ref, dout_ref, dg_ref, loss_ref):
        @pl.when(pl.program_id(0) == 0)
        def _():
            dg_ref[...] = jnp.zeros_like(dg_ref)
            loss_ref[...] = jnp.zeros_like(loss_ref)

        g = g_ref[...]
        for rows in (slice(0, half), slice(half, tm)):
            y = jnp.dot(u_ref[rows, :], w_ref[...], preferred_element_type=F32)
            r = lax.rsqrt(_mean_last(y * y) + EPS)
            n = y * r
            err = x_ref[rows, :] + n * g - t_ref[rows, :]
            loss_ref[...] += 0.5 * jnp.sum(_mean_last(err * err))
            dout = err * (1.0 / d)
            dout_ref[rows, :] = dout
            dg_ref[...] += _sum_rows(dout * n)
            dn = dout * g
            dy_ref[rows, :] = (r * (dn - n * _mean_last(dn * n))).astype(BF16)

    return pl.pallas_call(
        body, name=name, grid=(s // tm,),
        in_specs=[pl.BlockSpec((tm, k), lambda i: (i, 0)),
                  pl.BlockSpec((k, d), lambda i: (0, 0), pipeline_mode=pl.Buffered(1)),
                  pl.BlockSpec((tm, d), lambda i: (i, 0)),
                  pl.BlockSpec((1, d), lambda i: (0, 0)),
                  pl.BlockSpec((tm, d), lambda i: (i, 0))],
        out_specs=[pl.BlockSpec((tm, d), lambda i: (i, 0)),
                   pl.BlockSpec((tm, d), lambda i: (i, 0)),
                   pl.BlockSpec((1, d), lambda i: (0, 0)),
                   pl.BlockSpec((8, LANES), lambda i: (0, 0))],
        out_shape=[jax.ShapeDtypeStruct((s, d), BF16), jax.ShapeDtypeStruct((s, d), F32),
                   jax.ShapeDtypeStruct((1, d), F32), jax.ShapeDtypeStruct((8, LANES), F32)],
        compiler_params=_params("arbitrary"),
    )(u, w, x_res, gain, target)


def _matmul_nt(a, w_sm, name, comm=None):
    s, ncols = a.shape
    n_sh, r, ns = w_sm.shape
    assert ncols == n_sh * ns
    tm = _row_tile(s, 1024)
    nc = _col_chunk(ns, 1792)
    per = ns // nc
    steps = n_sh * per

    def body(a_ref, w_ref, o_ref):
        part = lax.dot_general(a_ref[...], w_ref[0], (((1,), (1,)), ((), ())), preferred_element_type=F32)

        @pl.when(pl.program_id(1) == 0)
        def _():
            o_ref[...] = part

        @pl.when(pl.program_id(1) > 0)
        def _():
            o_ref[...] += part

    out, bufs = _call(
        body, name=name, grid=(s // tm, steps),
        in_specs=[pl.BlockSpec((tm, nc), lambda i, j: (i, j)),
                  pl.BlockSpec((1, r, nc), lambda i, j: (j // per, 0, j % per))],
        out_specs=[pl.BlockSpec((tm, r), lambda i, j: (i, 0))],
        out_shape=[jax.ShapeDtypeStruct((s, r), F32)],
        args=(a, w_sm), comm=comm)
    return out[0], bufs


def _matmul_tn(a, b, n_sh, name):
    s, k = a.shape
    n = b.shape[1]
    ns = n // n_sh
    tk = _col_chunk(k, TN_ACC_BYTES // (4 * ns))
    ts = _row_tile(s, 2048)
    n_s = s // ts

    def body(a_ref, b_ref, o_ref, acc_ref):
        part = lax.dot_general(a_ref[...], b_ref[...], (((0,), (0,)), ((), ())), preferred_element_type=F32)

        @pl.when(pl.program_id(2) == 0)
        def _():
            acc_ref[...] = part

        @pl.when(pl.program_id(2) > 0)
        def _():
            acc_ref[...] += part

        @pl.when(pl.program_id(2) == n_s - 1)
        def _():
            o_ref[0] = acc_ref[...].astype(BF16)

    return pl.pallas_call(
        body, name=name, grid=(n_sh, k // tk, n_s),
        in_specs=[pl.BlockSpec((ts, tk), lambda j, i, t: (t, i)),
                  pl.BlockSpec((ts, ns), lambda j, i, t: (t, j))],
        out_specs=pl.BlockSpec((1, tk, ns), lambda j, i, t: (j, i, 0)),
        out_shape=jax.ShapeDtypeStruct((n_sh, k, ns), BF16),
        scratch_shapes=[pltpu.VMEM((tk, ns), F32)],
        compiler_params=_params("arbitrary", "arbitrary", "arbitrary"),
    )(a, b)


def _norm_bwd(dh, x, gain, dres, name, post=None):
    s, d = x.shape
    tm = _row_tile(s, 256)
    with_post = post is not None

    def rms_bwd(dout, v, g):
        r = lax.rsqrt(_mean_last(v * v) + EPS)
        n = v * r
        dn = dout * g
        return r * (dn - n * _mean_last(dn * n)), _sum_rows(dout * n)

    def body(*refs):
        if with_post:
            dh_ref, x_ref, g_ref, dres_ref, y_ref, gp_ref, dx_ref, dg_ref, dy_ref, dgp_ref = refs
        else:
            dh_ref, x_ref, g_ref, dres_ref, dx_ref, dg_ref = refs

        @pl.when(pl.program_id(0) == 0)
        def _():
            dg_ref[...] = jnp.zeros_like(dg_ref)
            if with_post:
                dgp_ref[...] = jnp.zeros_like(dgp_ref)

        dv, dg = rms_bwd(dh_ref[...], x_ref[...], g_ref[...])
        dx = dres_ref[...] + dv
        dx_ref[...] = dx
        dg_ref[...] += dg
        if with_post:
            dy, dgp = rms_bwd(dx, y_ref[...], gp_ref[...])
            dy_ref[...] = dy.astype(BF16)
            dgp_ref[...] += dgp

    row = pl.BlockSpec((tm, d), lambda i: (i, 0))
    vec = pl.BlockSpec((1, d), lambda i: (0, 0))
    in_specs = [row, row, vec, row]
    out_specs = [row, vec]
    out_shape = [jax.ShapeDtypeStruct((s, d), F32), jax.ShapeDtypeStruct((1, d), F32)]
    args = [dh, x, gain, dres]
    if with_post:
        in_specs += [row, vec]
        out_specs += [row, vec]
        out_shape += [jax.ShapeDtypeStruct((s, d), BF16), jax.ShapeDtypeStruct((1, d), F32)]
        args += list(post)
    return _call(body, name=name, grid=(s // tm,), in_specs=in_specs, out_specs=out_specs, out_shape=out_shape,
                 args=args)[0]


def _fill_shifted_down(sh, rows):
    for b in range(1, SHIFTS):
        sh[b, SHIFTS:rows, :] = sh[0, SHIFTS - b:rows - b, :]


def _fill_shifted_up(sh, rows):
    for b in range(1, SHIFTS):
        sh[b, 0:rows - SHIFTS, :] = sh[0, b:rows - SHIFTS + b, :]


def _for_blocks(ts, w, fn):
    lb = min(LANE_BLOCK, w)
    for l0 in range(0, w, lb):
        def rows(rb, carry, l0=l0):
            fn(pl.multiple_of(rb * ROW_BLOCK, ROW_BLOCK), slice(l0, l0 + lb))
            return carry

        lax.fori_loop(0, ts // ROW_BLOCK, rows, 0)


TAP_SPAN = SHIFTS * ((CONV_B - 1) // SHIFTS)
WINDOW = ROW_BLOCK + TAP_SPAN


def _taps_of(b):
    return [(a, SHIFTS * a + b) for a in range((CONV_B - 1 - b) // SHIFTS + 1)]


def _conv31(sh, base, step, wt_ref, bias_ref, out_ref, ts, w):
    low = min(0, step * (TAP_SPAN // SHIFTS))

    def block(r0, lanes):
        acc = [jnp.zeros((SHIFTS, lanes.stop - lanes.start), F32) for _ in range(ROW_BLOCK // SHIFTS)]
        for b in range(SHIFTS):
            window = sh[b, pl.ds(pl.multiple_of(r0 + (base + low), SHIFTS), WINDOW), lanes]
            for a, j in (_taps_of(b) if step > 0 else reversed(_taps_of(b))):
                at = step * a - low
                wt = wt_ref[CONV_B - 1 - j, :, lanes]
                acc = [v + wt * window[at + SHIFTS * r:at + SHIFTS * (r + 1), :] for r, v in enumerate(acc)]
        for r, v in enumerate(acc):
            if bias_ref is not None:
                v = v + bias_ref[:, lanes]
            out_ref[pl.ds(pl.multiple_of(r0 + SHIFTS * r, SHIFTS), SHIFTS), lanes] = v

    _for_blocks(ts, w, block)


def _conv31_weight_grad(d_sh, x_sh, wacc, ts, w):
    def block(r0, lanes):
        d = d_sh[0, pl.ds(r0, ROW_BLOCK), lanes]
        for b in range(SHIFTS):
            window = x_sh[b, pl.ds(pl.multiple_of(r0 + (HALO_B - TAP_SPAN), SHIFTS), WINDOW), lanes]
            for a, j in _taps_of(b):
                at = TAP_SPAN - SHIFTS * a
                prod = d * window[at:at + ROW_BLOCK, :]
                part = prod[0:SHIFTS, :]
                for q in range(1, ROW_BLOCK // SHIFTS):
                    part = part + prod[q * SHIFTS:(q + 1) * SHIFTS, :]
                wacc[CONV_B - 1 - j, :, lanes] += part

    _for_blocks(ts, w, block)


def _even_forward_tile(p_ref, halo_ref, first, a_conv_ref, b_conv_ref, bias_ref, lng_ref, lnb_ref, qbuf, ysh, y1buf,
                       wb, w, ts):
    @pl.when(pl.program_id(0) == 0)
    def _():
        for k in range(CONV_B):
            wb[k] = jnp.broadcast_to(b_conv_ref[k:k + 1, :], (SHIFTS, w))

    def col(ref, k, rows=slice(None)):
        return ref[rows, k * w:(k + 1) * w]

    a_x, a_b, a_c, a_z = col(p_ref, 0), col(p_ref, 1), col(p_ref, 2), col(p_ref, 3)
    b_val, b_gate, b_z = col(p_ref, 4), col(p_ref, 5), col(p_ref, 6)
    keep = jnp.where(first, 0.0, 1.0)

    rows_a = slice(HALO_B - HALO_A, HALO_B)
    qbuf[0:HALO_A, :] = col(halo_ref, 2, rows_a) * col(halo_ref, 0, rows_a) * keep
    qbuf[HALO_A:HALO_A + ts, :] = a_c * a_x
    cq = jnp.zeros((ts, w), F32)
    for j in range(CONV_A):
        cq = cq + a_conv_ref[CONV_A - 1 - j:CONV_A - j, :] * qbuf[HALO_A - j:HALO_A - j + ts, :]
    ya = a_b * cq

    ysh[0, 0:HALO_B, :] = col(halo_ref, 4) * _sigmoid(col(halo_ref, 5)) * keep
    ysh[0, HALO_B:HALO_B + ts, :] = b_val * _sigmoid(b_gate)
    _fill_shifted_down(ysh, HALO_B + ts)
    _conv31(ysh, HALO_B, -SHIFTS, wb, bias_ref, y1buf, ts, w)
    yb1 = y1buf[...]
    xc = yb1 - _mean_last(yb1)
    rstd = lax.rsqrt(_mean_last(xc * xc) + EPS)
    xhat = xc * rstd
    yb2 = xhat * lng_ref[...] + lnb_ref[...]
    return dict(a_x=a_x, a_b=a_b, a_c=a_c, a_z=a_z, b_val=b_val, b_gate=b_gate, b_z=b_z,
                cq=cq, ya=ya, rstd=rstd, xhat=xhat, yb2=yb2)


def _even_specs(s, w, ts):
    tile = pl.BlockSpec((ts, 7 * w), lambda i: (i, 0))
    halo = pl.BlockSpec((HALO_B, 7 * w), lambda i: (jnp.maximum(i * (ts // HALO_B) - 1, 0), 0))
    return tile, halo


def _small_specs(shapes, index=lambda i: (0, 0)):
    return [pl.BlockSpec(sh, index) for sh in shapes]


def _even_mixer_fwd(p, w_out, x_res, gain, a_conv, b_conv, bias, ln_g, ln_b, name, comm=None):
    s, d = x_res.shape
    w = p.shape[1] // 7
    ts = _row_tile(s, 128)
    assert ts % HALO_B == 0

    def body(p_ref, halo_ref, wout_ref, x_ref, g_ref, ac_ref, bc_ref, bias_ref, lng_ref, lnb_ref,
             u_ref, xn_ref, y_ref, conv_ref, qbuf, ysh, wb):
        first = pl.program_id(0) == 0
        f = _even_forward_tile(p_ref, halo_ref, first, ac_ref, bc_ref, bias_ref, lng_ref, lnb_ref, qbuf, ysh, conv_ref,
                               wb, w, ts)
        yb3 = f["yb2"] * _sigmoid(f["yb2"])
        u_a = (f["ya"] * (f["a_z"] * _sigmoid(f["a_z"]))).astype(BF16)
        u_b = (yb3 * (f["b_z"] * _sigmoid(f["b_z"]))).astype(BF16)
        u_ref[:, 0:w] = u_a
        u_ref[:, w:2 * w] = u_b
        y = (jnp.dot(u_a, wout_ref[0:w, :], preferred_element_type=F32)
             + jnp.dot(u_b, wout_ref[w:2 * w, :], preferred_element_type=F32))
        r = lax.rsqrt(_mean_last(y * y) + EPS)
        y_ref[...] = y
        xn_ref[...] = x_ref[...] + (y * r) * g_ref[...]

    tile, halo = _even_specs(s, w, ts)
    row = pl.BlockSpec((ts, d), lambda i: (i, 0))
    return _call(
        body, name=name, grid=(s // ts,),
        in_specs=[tile, halo, pl.BlockSpec((2 * w, d), lambda i: (0, 0)), row, pl.BlockSpec((1, d), lambda i: (0, 0))]
        + _small_specs([(CONV_A, w), (CONV_B, w), (1, w), (1, w), (1, w)]),
        out_specs=[pl.BlockSpec((ts, 2 * w), lambda i: (i, 0)), row, row, pl.BlockSpec((ts, w), lambda i: (i, 0))],
        out_shape=[jax.ShapeDtypeStruct((s, 2 * w), BF16), jax.ShapeDtypeStruct((s, d), F32),
                   jax.ShapeDtypeStruct((s, d), F32), jax.ShapeDtypeStruct((s, w), F32)],
        scratch_shapes=[pltpu.VMEM((HALO_A + ts, w), F32), pltpu.VMEM((SHIFTS, HALO_B + ts, w), F32),
                        pltpu.VMEM((CONV_B, SHIFTS, w), F32)],
        args=(p, p, w_out, x_res, gain, a_conv, b_conv, bias, ln_g, ln_b), comm=comm)


def _even_mixer_bwd(p, du, conv, a_conv, b_conv, ln_g, ln_b, name, comm=None):
    s = p.shape[0]
    w = p.shape[1] // 7
    ts = _row_tile(s, EVEN_BWD_ROWS)
    nt = s // ts
    assert ts % HALO_B == 0

    def body(p_ref, halo_ref, du_ref, conv_ref, ac_ref, bc_ref, lng_ref, lnb_ref,
             dp_ref, dac_ref, dbc_ref, dbias_ref, dlng_ref, dlnb_ref,
             qbuf, ysh, y1buf, dqbuf, dsh, dy0buf, wacc, carry_dq, carry_dy, wb):
        step = pl.program_id(0)
        first = step == nt - 1

        @pl.when(step == 0)
        def _():
            for ref in (dac_ref, dbias_ref, dlng_ref, dlnb_ref, wacc, carry_dq, carry_dy):
                ref[...] = jnp.zeros_like(ref)

        @pl.when(step == 0)
        def _():
            for k in range(CONV_B):
                wb[k] = jnp.broadcast_to(bc_ref[k:k + 1, :], (SHIFTS, w))

        keep = jnp.where(first, 0.0, 1.0)
        lane_blocks = [slice(l0, l0 + min(LANE_PASS, w)) for l0 in range(0, w, min(LANE_PASS, w))]

        def col(ref, k, lanes, rows=slice(None)):
            return ref[rows, k * w + lanes.start:k * w + lanes.stop]


        rows_a = slice(HALO_B - HALO_A, HALO_B)
        for lanes in lane_blocks:
            a_x, a_b, a_c, a_z = (col(p_ref, k, lanes) for k in range(4))
            qbuf[0:HALO_A, lanes] = col(halo_ref, 2, lanes, rows_a) * col(halo_ref, 0, lanes, rows_a) * keep
            qbuf[HALO_A:HALO_A + ts, lanes] = a_c * a_x
            cq = jnp.zeros((ts, lanes.stop - lanes.start), F32)
            for j in range(CONV_A):
                cq = cq + ac_ref[CONV_A - 1 - j:CONV_A - j, lanes] * qbuf[HALO_A - j:HALO_A - j + ts, lanes]
            s_az = _sigmoid(a_z)
            du_a = du_ref[:, lanes]
            d_ya = du_a * (a_z * s_az)
            dp_ref[:, 3 * w + lanes.start:3 * w + lanes.stop] = (du_a * (a_b * cq) * _dsilu(a_z, s_az)).astype(BF16)
            dp_ref[:, 1 * w + lanes.start:1 * w + lanes.stop] = (d_ya * cq).astype(BF16)
            d_cq = d_ya * a_b
            dqbuf[0:ts, lanes] = d_cq
            dqbuf[ts:ts + HALO_A, lanes] = carry_dq[:, lanes]
            carry_dq[:, lanes] = d_cq[0:HALO_A, :]
            d_q = jnp.zeros_like(cq)
            for o in range(CONV_A):
                d_q = d_q + ac_ref[CONV_A - 1 - o:CONV_A - o, lanes] * dqbuf[o:o + ts, lanes]
            for j in range(CONV_A):
                k = CONV_A - 1 - j
                dac_ref[k:k + 1, lanes] += _sum_rows(d_cq * qbuf[HALO_A - j:HALO_A - j + ts, lanes])
            dp_ref[:, 2 * w + lanes.start:2 * w + lanes.stop] = (d_q * a_x).astype(BF16)
            dp_ref[:, 0 * w + lanes.start:0 * w + lanes.stop] = (d_q * a_c).astype(BF16)
            ysh[0, 0:HALO_B, lanes] = col(halo_ref, 4, lanes) * _sigmoid(col(halo_ref, 5, lanes)) * keep
            ysh[0, HALO_B:HALO_B + ts, lanes] = col(p_ref, 4, lanes) * _sigmoid(col(p_ref, 5, lanes))
        _fill_shifted_down(ysh, HALO_B + ts)

        total = jnp.zeros((ts, 1), F32)
        for lanes in lane_blocks:
            total = total + jnp.sum(conv_ref[:, lanes], axis=-1, keepdims=True)
        mu = total * (1.0 / w)
        total = jnp.zeros((ts, 1), F32)
        for lanes in lane_blocks:
            xc = conv_ref[:, lanes] - mu
            total = total + jnp.sum(xc * xc, axis=-1, keepdims=True)
        rstd = lax.rsqrt(total * (1.0 / w) + EPS)

        sum_dxh = jnp.zeros((ts, 1), F32)
        sum_dxh_xhat = jnp.zeros((ts, 1), F32)
        for lanes in lane_blocks:
            xhat = (conv_ref[:, lanes] - mu) * rstd
            yb2 = xhat * lng_ref[:, lanes] + lnb_ref[:, lanes]
            b_z = col(p_ref, 6, lanes)
            s_bz, s_y2 = _sigmoid(b_z), _sigmoid(yb2)
            du_b = du_ref[:, w + lanes.start:w + lanes.stop]
            dp_ref[:, 6 * w + lanes.start:6 * w + lanes.stop] = (du_b * (yb2 * s_y2) * _dsilu(b_z, s_bz)).astype(BF16)
            d_yb2 = du_b * (b_z * s_bz) * _dsilu(yb2, s_y2)
            dlng_ref[:, lanes] += _sum_rows(d_yb2 * xhat)
            dlnb_ref[:, lanes] += _sum_rows(d_yb2)
            d_xh = d_yb2 * lng_ref[:, lanes]
            sum_dxh = sum_dxh + jnp.sum(d_xh, axis=-1, keepdims=True)
            sum_dxh_xhat = sum_dxh_xhat + jnp.sum(d_xh * xhat, axis=-1, keepdims=True)
            y1buf[:, lanes] = xhat
            dy0buf[:, lanes] = d_xh
        mean_dxh = sum_dxh * (1.0 / w)
        mean_dxh_xhat = sum_dxh_xhat * (1.0 / w)

        for lanes in lane_blocks:
            d_yb1 = rstd * (dy0buf[:, lanes] - mean_dxh - y1buf[:, lanes] * mean_dxh_xhat)
            dbias_ref[:, lanes] += _sum_rows(d_yb1)
            dsh[0, 0:ts, lanes] = d_yb1
            dsh[0, ts:ts + HALO_B, lanes] = carry_dy[:, lanes]
            carry_dy[:, lanes] = d_yb1[0:HALO_B, :]
        _fill_shifted_up(dsh, ts + HALO_B)
        _conv31(dsh, 0, SHIFTS, wb, None, dy0buf, ts, w)
        _conv31_weight_grad(dsh, ysh, wacc, ts, w)

        @pl.when(step == nt - 1)
        def _():
            for k in range(CONV_B):
                dbc_ref[k:k + 1, :] = _sum_rows(wacc[k])

        for lanes in lane_blocks:
            d_yb0 = dy0buf[:, lanes]
            s_g = _sigmoid(col(p_ref, 5, lanes))
            dp_ref[:, 4 * w + lanes.start:4 * w + lanes.stop] = (d_yb0 * s_g).astype(BF16)
            dp_ref[:, 5 * w + lanes.start:5 * w + lanes.stop] = (
                d_yb0 * col(p_ref, 4, lanes) * s_g * (1.0 - s_g)).astype(BF16)

    rev = lambda i: (nt - 1 - i, 0)
    tile = pl.BlockSpec((ts, 7 * w), rev)
    halo = pl.BlockSpec((HALO_B, 7 * w), lambda i: (jnp.maximum((nt - 1 - i) * (ts // HALO_B) - 1, 0), 0))
    small = [(CONV_A, w), (CONV_B, w), (1, w), (1, w), (1, w)]
    return _call(
        body, name=name, grid=(nt,),
        in_specs=[tile, halo, pl.BlockSpec((ts, 2 * w), rev), pl.BlockSpec((ts, w), rev)]
        + _small_specs([(CONV_A, w), (CONV_B, w), (1, w), (1, w)]),
        out_specs=[pl.BlockSpec((ts, 7 * w), rev)] + _small_specs(small),
        out_shape=[jax.ShapeDtypeStruct((s, 7 * w), BF16)] + [jax.ShapeDtypeStruct(sh, F32) for sh in small],
        scratch_shapes=[pltpu.VMEM((HALO_A + ts, w), F32), pltpu.VMEM((SHIFTS, HALO_B + ts, w), F32),
                        pltpu.VMEM((ts, w), F32),
                        pltpu.VMEM((ts + HALO_A, w), F32), pltpu.VMEM((SHIFTS, ts + HALO_B, w), F32),
                        pltpu.VMEM((ts, w), F32), pltpu.VMEM((CONV_B, SHIFTS, w), F32),
                        pltpu.VMEM((HALO_A, w), F32), pltpu.VMEM((HALO_B, w), F32),
                        pltpu.VMEM((CONV_B, SHIFTS, w), F32)],
        args=(p, p, du, conv, a_conv, b_conv, ln_g, ln_b), comm=comm)


def _trailing_sums(buf_a, buf_b, cols, win, rows, first_out):
    src, dst, shift = buf_a, buf_b, 1
    while True:
        last = 2 * shift >= win
        lo = first_out if last else 0
        val = src[PAD_P + lo:PAD_P + rows, cols] + src[PAD_P + lo - shift:PAD_P + rows - shift, cols]
        if last:
            return val
        dst[PAD_P + lo:PAD_P + rows, cols] = val
        src, dst, shift = dst, src, 2 * shift


def _leading_sums(buf_a, buf_b, cols, win, rows, n_out):
    src, dst, shift = buf_a, buf_b, 1
    while True:
        last = 2 * shift >= win
        hi = n_out if last else rows
        val = src[0:hi, cols] + src[shift:hi + shift, cols]
        if last:
            return val
        dst[0:hi, cols] = val
        src, dst, shift = dst, src, 2 * shift


def _pool_forward_tile(p_ref, halo_ref, first, tile_index, cw_ref, cb_ref, cs_ref, vbuf, vtmp, c, gc, ts):
    vbuf[PAD_P:PAD_P + HALO_P, :] = halo_ref[...] * jnp.where(first, 0.0, 1.0)
    vbuf[PAD_P + HALO_P:PAD_P + HALO_P + ts, :] = p_ref[:, 0:c]
    pos = tile_index * ts + lax.broadcasted_iota(jnp.int32, (ts, 1), 0) + 1
    pooled, inv, gout = [], [], []
    for g, win in enumerate(POOL_WINDOWS):
        cols = slice(g * gc, (g + 1) * gc)
        acc = _trailing_sums(vbuf, vtmp, cols, win, HALO_P + ts, HALO_P)
        inv_g = 1.0 / jnp.minimum(pos, win).astype(F32)
        pooled_g = (acc * inv_g - p_ref[:, cols]).astype(BF16)
        pooled.append(pooled_g)
        inv.append(inv_g)
        gout.append(jnp.dot(pooled_g, cw_ref[g], preferred_element_type=F32) + cb_ref[:, cols])
    return pooled, inv, gout


def _odd_mixer_fwd(p, cw, cb, cs, name):
    s = p.shape[0]
    c = p.shape[1] // 2
    gc = c // N_GROUPS
    ts = _row_tile(s, 256)

    def body(p_ref, halo_ref, cw_ref, cb_ref, cs_ref, u_ref, pooled_ref, vbuf, vtmp):
        i = pl.program_id(0)

        @pl.when(i == 0)
        def _():
            vbuf[0:PAD_P, :] = jnp.zeros((PAD_P, c), F32)
            vtmp[0:PAD_P, :] = jnp.zeros((PAD_P, c), F32)

        pooled, _, gout = _pool_forward_tile(p_ref, halo_ref, i == 0, i, cw_ref, cb_ref, cs_ref, vbuf, vtmp, c, gc, ts)
        for g in range(N_GROUPS):
            cols = slice(g * gc, (g + 1) * gc)
            z = p_ref[:, c + g * gc:c + (g + 1) * gc]
            u_ref[:, cols] = (gout[g] * cs_ref[:, cols] * (z * _sigmoid(z))).astype(BF16)
            pooled_ref[:, cols] = pooled[g]

    return pl.pallas_call(
        body, name=name, grid=(s // ts,),
        in_specs=[pl.BlockSpec((ts, 2 * c), lambda i: (i, 0)),
                  pl.BlockSpec((HALO_P, c), lambda i: (jnp.maximum(i * (ts // HALO_P) - 1, 0), 0)),
                  pl.BlockSpec((N_GROUPS, gc, gc), lambda i: (0, 0, 0)),
                  pl.BlockSpec((1, c), lambda i: (0, 0)), pl.BlockSpec((1, c), lambda i: (0, 0))],
        out_specs=[pl.BlockSpec((ts, c), lambda i: (i, 0))] * 2,
        out_shape=[jax.ShapeDtypeStruct((s, c), BF16)] * 2,
        scratch_shapes=[pltpu.VMEM((PAD_P + HALO_P + ts, c), F32)] * 2,
        compiler_params=_params("arbitrary"),
    )(p, p, cw, cb, cs)


def _odd_mixer_bwd(p, pooled, du, cw, cb, cs, name):
    s = p.shape[0]
    c = p.shape[1] // 2
    gc = c // N_GROUPS
    ts = _row_tile(s, 256)
    nt = s // ts

    def body(z_ref, pooled_ref, du_ref, cw_ref, cb_ref, cs_ref, dp_ref, dcw_ref, dcb_ref, dcs_ref,
             ebuf, etmp, carry_e):
        step = pl.program_id(0)
        tile_index = nt - 1 - step

        @pl.when(step == 0)
        def _():
            for ref in (dcw_ref, dcb_ref, dcs_ref, carry_e):
                ref[...] = jnp.zeros_like(ref)
            for ref in (ebuf, etmp):
                ref[ts + HALO_P:ts + HALO_P + PAD_P, :] = jnp.zeros((PAD_P, c), F32)

        pos = tile_index * ts + lax.broadcasted_iota(jnp.int32, (ts, 1), 0) + 1
        ebuf[ts:ts + HALO_P, :] = carry_e[...]
        for g, win in enumerate(POOL_WINDOWS):
            cols = slice(g * gc, (g + 1) * gc)
            pooled_g = pooled_ref[:, cols]
            d_pool = jnp.zeros((ts, gc), F32)
            for part in range(0, gc, max(gc // 2, LANES)):
                sub = slice(part, part + max(gc // 2, LANES))
                lanes = slice(g * gc + sub.start, g * gc + sub.stop)
                gout_h = jnp.dot(pooled_g, cw_ref[g, :, sub], preferred_element_type=F32) + cb_ref[:, lanes]
                z = z_ref[:, lanes]
                sz = _sigmoid(z)
                du_h = du_ref[:, lanes]
                scale = cs_ref[:, lanes]
                d_y = du_h * (z * sz)
                dp_ref[:, c + lanes.start:c + lanes.stop] = (du_h * (gout_h * scale) * _dsilu(z, sz)).astype(BF16)
                dcs_ref[:, lanes] += _sum_rows(d_y * gout_h)
                d_gout = d_y * scale
                dcb_ref[:, lanes] += _sum_rows(d_gout)
                d_gout_b = d_gout.astype(BF16)
                dcw_ref[g, :, sub] += lax.dot_general(pooled_g, d_gout_b, (((0,), (0,)), ((), ())),
                                                      preferred_element_type=F32)
                d_pool = d_pool + lax.dot_general(d_gout_b, cw_ref[g, :, sub], (((1,), (1,)), ((), ())),
                                                  preferred_element_type=F32)
            e = d_pool * (1.0 / jnp.minimum(pos, win).astype(F32))
            ebuf[0:ts, cols] = e
            carry_e[:, cols] = e[0:HALO_P, :]
            d_v = _leading_sums(ebuf, etmp, cols, win, ts + HALO_P, ts) - d_pool
            dp_ref[:, cols] = d_v.astype(BF16)

    rev = lambda i: (nt - 1 - i, 0)
    small = [(N_GROUPS, gc, gc), (1, c), (1, c)]
    return pl.pallas_call(
        body, name=name, grid=(nt,),
        in_specs=[pl.BlockSpec((ts, c), lambda i: (nt - 1 - i, 1)),
                  pl.BlockSpec((ts, c), rev),
                  pl.BlockSpec((ts, c), rev),
                  pl.BlockSpec((N_GROUPS, gc, gc), lambda i: (0, 0, 0)),
                  pl.BlockSpec((1, c), lambda i: (0, 0)), pl.BlockSpec((1, c), lambda i: (0, 0))],
        out_specs=[pl.BlockSpec((ts, 2 * c), rev),
                   pl.BlockSpec((N_GROUPS, gc, gc), lambda i: (0, 0, 0)),
                   pl.BlockSpec((1, c), lambda i: (0, 0)), pl.BlockSpec((1, c), lambda i: (0, 0))],
        out_shape=[jax.ShapeDtypeStruct((s, 2 * c), BF16)] + [jax.ShapeDtypeStruct(sh, F32) for sh in small],
        scratch_shapes=[pltpu.VMEM((ts + HALO_P + PAD_P, c), F32)] * 2 + [pltpu.VMEM((HALO_P, c), F32)],
        compiler_params=_params("arbitrary"),
    )(p, pooled, du, cw, cb, cs)


def _cast_into_slot(a, coords, name):
    r, cols = a.shape
    tr = _row_tile(r // 2, 1024)
    per = r // 2 // tr

    def body(co_ref, a_ref, o_ref):
        o_ref[0, 0] = a_ref[...].astype(BF16)

    return pl.pallas_call(
        body, name=name,
        grid_spec=pltpu.PrefetchScalarGridSpec(
            num_scalar_prefetch=1, grid=(2, per),
            in_specs=[pl.BlockSpec((tr, cols), lambda h, i, co: (h * per + i, 0))],
            out_specs=pl.BlockSpec((1, 1, tr, cols), lambda h, i, co: (co[0], h, i, 0))),
        out_shape=jax.ShapeDtypeStruct((N_SHARDS, 2, r // 2, cols), BF16),
        compiler_params=_params("arbitrary", "arbitrary"),
    )(coords, a)


def _chip_sum(g, other, coords, name):
    n_sh, _, r2, cols = g.shape
    tr = _row_tile(r2, 512)

    def body(co_ref, g_ref, o_ref, sum_ref, mine_ref):
        v = (g_ref[0, 0].astype(F32) + o_ref[0].astype(F32)).astype(BF16)
        sum_ref[0] = v

        @pl.when(pl.program_id(1) == co_ref[0])
        def _():
            mine_ref[0] = v

    piece = pl.BlockSpec((1, tr, cols), lambda i, s, co: (s, i, 0))
    return pl.pallas_call(
        body, name=name,
        grid_spec=pltpu.PrefetchScalarGridSpec(
            num_scalar_prefetch=1, grid=(r2 // tr, n_sh),
            in_specs=[pl.BlockSpec((1, 1, tr, cols), lambda i, s, co: (s, co[1], i, 0)), piece],
            out_specs=[piece, pl.BlockSpec((1, tr, cols), lambda i, s, co: (co[0], i, 0))]),
        out_shape=[jax.ShapeDtypeStruct((n_sh, r2, cols), BF16)] * 2,
        compiler_params=_params("arbitrary", "arbitrary"),
    )(coords, g, other)


def _shard_sum(pieces, coords, name):
    n_sh, r2, cols = pieces.shape
    tr = _row_tile(r2, 512)

    def body(co_ref, p_ref, o_ref):
        acc = p_ref[0].astype(F32)
        for k in range(1, n_sh):
            acc = acc + p_ref[k].astype(F32)
        o_ref[0] = acc

    return pl.pallas_call(
        body, name=name,
        grid_spec=pltpu.PrefetchScalarGridSpec(
            num_scalar_prefetch=1, grid=(r2 // tr,),
            in_specs=[pl.BlockSpec((n_sh, tr, cols), lambda i, co: (0, i, 0))],
            out_specs=pl.BlockSpec((1, tr, cols), lambda i, co: (co[1], i, 0))),
        out_shape=jax.ShapeDtypeStruct((2, r2, cols), F32),
        compiler_params=_params("arbitrary"),
    )(coords, pieces)


def _sum_small(a, name):
    n, r, cols = a.shape

    def body(a_ref, o_ref):
        acc = a_ref[0]
        for k in range(1, n):
            acc = acc + a_ref[k]
        o_ref[...] = acc

    return pl.pallas_call(
        body, name=name,
        in_specs=[pl.BlockSpec((n, r, cols), lambda: (0, 0, 0))],
        out_specs=pl.BlockSpec((r, cols), lambda: (0, 0)),
        out_shape=jax.ShapeDtypeStruct((r, cols), F32),
        compiler_params=_params(),
    )(a)


def _adamw_step(w, g, m, v):
    m = ADAM_B1 * m + (1.0 - ADAM_B1) * g
    v = ADAM_B2 * v + (1.0 - ADAM_B2) * (g * g)
    m_hat = m / (1.0 - ADAM_B1 ** ADAM_STEP)
    v_hat = v / (1.0 - ADAM_B2 ** ADAM_STEP)
    return -ADAM_LR * (m_hat / (jnp.sqrt(v_hat) + ADAM_EPS) + ADAM_WD * w), m, v


def _adamw(w, g, m, v, name):
    r, cols = w.shape
    tr = _row_tile(r, 256) if r % SUBLANES_BF16 == 0 else r

    def body(w_ref, g_ref, m_ref, v_ref, go_ref, d_ref, nm_ref, nv_ref):
        g = g_ref[...]
        go_ref[...] = g
        d_ref[...], nm_ref[...], nv_ref[...] = _adamw_step(w_ref[...], g, m_ref[...], v_ref[...])

    blk = pl.BlockSpec((tr, cols), lambda i: (i, 0))
    return pl.pallas_call(
        body, name=name, grid=(r // tr,),
        in_specs=[blk] * 4, out_specs=[blk] * 4,
        out_shape=[jax.ShapeDtypeStruct((r, cols), F32)] * 4,
        compiler_params=_params("arbitrary"),
    )(w, g, m, v)


def _adamw_many(ws, gs, ms, vs, name):
    n = len(ws)

    def body(*refs):
        w_refs, g_refs, m_refs, v_refs = (refs[t * n:(t + 1) * n] for t in range(4))
        d_refs, nm_refs, nv_refs = (refs[(4 + t) * n:(5 + t) * n] for t in range(3))
        for k in range(n):
            d_refs[k][...], nm_refs[k][...], nv_refs[k][...] = _adamw_step(
                w_refs[k][...], g_refs[k][...], m_refs[k][...], v_refs[k][...])

    specs = [pl.BlockSpec(a.shape, lambda: (0, 0)) for a in ws]
    out = pl.pallas_call(
        body, name=name, in_specs=specs * 4, out_specs=specs * 3,
        out_shape=[jax.ShapeDtypeStruct(a.shape, F32) for a in ws] * 3,
        compiler_params=_params(),
    )(*ws, *gs, *ms, *vs)
    return out[:n], out[n:2 * n], out[2 * n:]


def _place():
    x, y, c = lax.axis_index("x"), lax.axis_index("y"), lax.axis_index("c")
    other_chips = [(1 - x, y), (x, 1 - y), (1 - x, 1 - y)]
    return x, y, c, other_chips


def _chip(xy):
    return 2 * xy[0] + xy[1]


def _remote(src, dst, send_sem, recv_sem, to):
    return pltpu.make_async_remote_copy(src_ref=src, dst_ref=dst, send_sem=send_sem, recv_sem=recv_sem,
                                        device_id=to, device_id_type=MESH)


def _gather_ici(ctx, k, j, start):
    (x, y, c, chips), b, send, recv = ctx
    if j < 2:
        chip, to = (_chip((x, y)) if start else _chip(chips[j])), (*chips[j], c)
    else:
        chip = 2 * (x ^ c) + (y ^ (1 - c)) if start else _chip(chips[2])
        to = (x ^ (1 - c), y ^ c, c)
    blk = b[k].at[chip, c]
    return _remote(blk, blk, send.at[6 * k + j], recv.at[6 * k + j], to)


def _gather_d2d(ctx, k, j, start):
    (x, y, c, chips), b, send, recv = ctx
    blk = b[k].at[_chip(chips[j]), c if start else 1 - c]
    return _remote(blk, blk, send.at[6 * k + 3 + j], recv.at[6 * k + 3 + j], (x, y, 1 - c))


def _gather_small(ctx, n, j, start):
    (x, y, c, chips), b, send, recv = ctx
    blk = b[n].at[_chip((x, y)) if start else _chip(chips[j])]
    return _remote(blk, blk, send.at[6 * n + j], recv.at[6 * n + j], (*chips[j], c))


def _gather_neighbours_landed(ctx, k):
    for j in range(2):
        _gather_ici(ctx, k, j, False).wait_recv()
    _gather_ici(ctx, k, 2, True).start()
    for j in range(2):
        _gather_d2d(ctx, k, j, True).start()


def _gather_diagonal_landed(ctx, k):
    _gather_ici(ctx, k, 2, False).wait_recv()
    _gather_d2d(ctx, k, 2, True).start()


def _gather_comm(bufs, relay_at, forward_at):
    n = len(bufs)

    def start(srcs, b, send, recv):
        for k in range(n):
            for j in range(2):
                _gather_ici((_place(), b, send, recv), k, j, True).start()

    def relay(srcs, b, send, recv):
        for k in range(n):
            _gather_neighbours_landed((_place(), b, send, recv), k)

    def forward(srcs, b, send, recv):
        for k in range(n):
            _gather_diagonal_landed((_place(), b, send, recv), k)

    def finish(srcs, b, send, recv):
        ctx = (_place(), b, send, recv)
        for k in range(n):
            for j in range(3):
                _gather_d2d(ctx, k, j, False).wait_recv()
                _gather_ici(ctx, k, j, True).wait_send()
                _gather_d2d(ctx, k, j, True).wait_send()

    return _Comm([], bufs, 6 * n, [(0, start), (relay_at, relay), (forward_at, forward)], finish)


def _rmsnorm(x, gain, name):
    s, d = x.shape
    tm = _row_tile(s, 512)

    def body(x_ref, g_ref, h_ref):
        xv = x_ref[...]
        r = lax.rsqrt(_mean_last(xv * xv) + EPS)
        h_ref[...] = (xv * r * g_ref[...]).astype(BF16)

    return pl.pallas_call(
        body, name=name, grid=(s // tm,),
        in_specs=[pl.BlockSpec((tm, d), lambda i: (i, 0)), pl.BlockSpec((1, d), lambda i: (0, 0))],
        out_specs=pl.BlockSpec((tm, d), lambda i: (i, 0)),
        out_shape=jax.ShapeDtypeStruct((s, d), BF16),
        compiler_params=_params("arbitrary"),
    )(x, gain)


def _gathered_in_proj(h, bufs, small, order, name):
    s, d = h.shape
    n_sh, _, r2, ns = bufs[0].shape
    assert d == 2 * r2
    n = len(bufs)
    tm = _row_tile(s, 512)
    n_i = s // tm
    hook_i = max(n_i - 2, 0)
    n_sems = 6 * n + 3

    def body(order_ref, h_ref, *rest):
        p_ref = rest[n + 1]
        b = rest[n + 2:2 * n + 3]
        w_vmem, w_sems, send, recv = rest[2 * n + 3:]
        j, i = pl.program_id(0), pl.program_id(1)
        ctx = (_place(), b, send, recv)

        def fetch(q):
            return pltpu.make_async_copy(b[0].at[order_ref[q]], w_vmem.at[q % 2], w_sems.at[q % 2])

        @pl.when((j == 0) & (i == 0))
        def _():
            for k in range(n):
                for peer in range(2):
                    _gather_ici(ctx, k, peer, True).start()
            for peer in range(3):
                _gather_small(ctx, n, peer, True).start()
            fetch(0).start()
            fetch(0).wait()

        for q in range(1, n_sh):
            @pl.when((j == q - 1) & (i == hook_i))
            def _(q=q):
                if q == 1:
                    _gather_neighbours_landed(ctx, 0)
                if q == 2:
                    for k in range(1, n):
                        _gather_neighbours_landed(ctx, k)
                if q == 3:
                    for k in range(n):
                        _gather_diagonal_landed(ctx, k)
                _gather_d2d(ctx, 0, q - 1, False).wait_recv()
                fetch(q).start()

            @pl.when((j == q) & (i == 0))
            def _(q=q):
                fetch(q).wait()

        wv = w_vmem.at[j % 2]
        p_ref[...] = (jnp.dot(h_ref[:, 0:r2], wv[0], preferred_element_type=F32)
                      + jnp.dot(h_ref[:, r2:d], wv[1], preferred_element_type=F32))

        @pl.when((j == n_sh - 1) & (i == n_i - 1))
        def _():
            for peer in range(3):
                _gather_small(ctx, n, peer, False).wait_recv()
                _gather_small(ctx, n, peer, True).wait_send()
            for k in range(n):
                for peer in range(3):
                    if k > 0:
                        _gather_d2d(ctx, k, peer, False).wait_recv()
                    _gather_ici(ctx, k, peer, True).wait_send()
                    _gather_d2d(ctx, k, peer, True).wait_send()

    all_bufs = list(bufs) + [small]
    out = pl.pallas_call(
        body, name=name,
        grid_spec=pltpu.PrefetchScalarGridSpec(
            num_scalar_prefetch=1, grid=(n_sh, n_i),
            in_specs=[pl.BlockSpec((tm, d), lambda j, i, o: (i, 0))] + [ANY] * (n + 1),
            out_specs=[pl.BlockSpec((tm, ns), lambda j, i, o: (i, o[j]))] + [ANY] * (n + 1),
            scratch_shapes=[pltpu.VMEM((2, 2, r2, ns), BF16), pltpu.SemaphoreType.DMA((2,)),
                            pltpu.SemaphoreType.DMA((n_sems,)), pltpu.SemaphoreType.DMA((n_sems,))]),
        out_shape=[jax.ShapeDtypeStruct((s, n_sh * ns), F32)]
        + [jax.ShapeDtypeStruct(a.shape, a.dtype) for a in all_bufs],
        input_output_aliases={2 + t: 1 + t for t in range(n + 1)},
        compiler_params=_params("arbitrary", "arbitrary"),
    )(order, h, *all_bufs)
    return out[0], list(out[1:])


def _exchange_comm(grads):
    n = len(grads)
    landing = [lax.empty((N_SHARDS,) + a.shape[2:], a.dtype) for a in grads]

    def copies(srcs, b, send, recv):
        x, y, c, _ = _place()
        return [_remote(srcs[k].at[s, 1 - c], b[k].at[s], send.at[N_SHARDS * k + s], recv.at[N_SHARDS * k + s],
                        (x, y, 1 - c)) for k in range(n) for s in range(N_SHARDS)]

    def start(srcs, b, send, recv):
        for cp in copies(srcs, b, send, recv):
            cp.start()

    def finish(srcs, b, send, recv):
        for cp in copies(srcs, b, send, recv):
            cp.wait()

    return _Comm(grads, landing, N_SHARDS * n, [(0, start)], finish)


def _scatter_comm(chip_sums, landing):
    n = len(chip_sums)

    def big(srcs, b, send, recv, k, j, start):
        x, y, c, chips = _place()
        dst = b[k].at[_chip((x, y)) if start else _chip(chips[j])]
        return _remote(srcs[k].at[_chip(chips[j])], dst, send.at[3 * k + j], recv.at[3 * k + j], (*chips[j], c))

    def start(srcs, b, send, recv):
        for k in range(n):
            for j in range(3):
                big(srcs, b, send, recv, k, j, True).start()

    def finish(srcs, b, send, recv):
        for k in range(n):
            for j in range(3):
                big(srcs, b, send, recv, k, j, False).wait_recv()
                big(srcs, b, send, recv, k, j, True).wait_send()

    return _Comm(chip_sums, landing, 3 * n, [(0, start)], finish)


def _join_comm(halves, small):
    n = len(halves)
    flips = [(fx, fy, fc) for fx in (0, 1) for fy in (0, 1) for fc in (0, 1)][1:]

    def half(b, send, recv, k, start):
        x, y, c, _ = _place()
        return _remote(b[k].at[c], b[k].at[c if start else 1 - c], send.at[k], recv.at[k], (x, y, 1 - c))

    def small_copy(b, send, recv, q, start):
        x, y, c, _ = _place()
        px, py, pc = x ^ flips[q][0], y ^ flips[q][1], c ^ flips[q][2]
        blk = b[n].at[4 * x + 2 * y + c if start else 4 * px + 2 * py + pc]
        return _remote(blk, blk, send.at[n + q], recv.at[n + q], (px, py, pc))

    def start(srcs, b, send, recv):
        for q in range(len(flips)):
            small_copy(b, send, recv, q, True).start()
        for k in range(n):
            half(b, send, recv, k, True).start()

    def finish(srcs, b, send, recv):
        for q in range(len(flips)):
            small_copy(b, send, recv, q, False).wait()
        for k in range(n):
            half(b, send, recv, k, False).wait()

    return _Comm([], list(halves) + [small], n + len(flips), [(0, start)], finish)


def _flat_rows(parts):
    flat = jnp.concatenate([p.reshape(-1) for p in parts])
    assert flat.shape[0] % LANES == 0
    return flat.reshape(-1, LANES)


def _unflatten(flat, shapes):
    out, at = [], 0
    for sh in shapes:
        size = 1
        for dim in sh:
            size *= dim
        out.append(flat[at:at + size].reshape(sh))
        at += size
    assert at == flat.shape[0], (at, flat.shape)
    return out


def _col_shards_to_full(a, rows):
    q = a.shape[1] // rows
    return a.reshape(N_SHARDS, rows, q).transpose(1, 0, 2).reshape(rows, N_SHARDS * q)


def _my_col_shard(full, chip):
    rows, cols = full.shape
    q = cols // N_SHARDS
    return lax.dynamic_index_in_dim(full.reshape(rows, N_SHARDS, q), chip, axis=1, keepdims=False)


def kernel(x, e_norm_pre, e_norm_post, e_w_in, e_a_conv, e_b_conv, e_b_conv_bias, e_b_ln_g, e_b_ln_b, e_w_out, o_norm_pre, o_norm_post, o_w_in, o_c_w, o_c_b, o_c_scale, o_w_out, loss_target, m_e_norm_pre, m_e_norm_post, m_e_w_in, m_e_a_conv, m_e_b_conv, m_e_b_conv_bias, m_e_b_ln_g, m_e_b_ln_b, m_e_w_out, m_o_norm_pre, m_o_norm_post, m_o_w_in, m_o_c_w, m_o_c_b, m_o_c_scale, m_o_w_out, v_e_norm_pre, v_e_norm_post, v_e_w_in, v_e_a_conv, v_e_b_conv, v_e_b_conv_bias, v_e_b_ln_g, v_e_b_ln_b, v_e_w_out, v_o_norm_pre, v_o_norm_post, v_o_w_in, v_o_c_w, v_o_c_b, v_o_c_scale, v_o_w_out):
    _, s, d = x.shape
    w = d // 2
    c = d
    gc = c // N_GROUPS
    wq, cq, gq = w // N_SHARDS, c // N_SHARDS, gc // N_SHARDS
    chip = 2 * lax.axis_index("x") + lax.axis_index("y")
    core = lax.axis_index("c")
    x2 = x.reshape(s, d)
    target = loss_target.reshape(s, d)

    big_w = [e_w_in[0], e_w_out[0], o_w_in[0], o_c_w[0].reshape(N_GROUPS * gq, gc), o_w_out[0]]
    big_m = [m_e_w_in[0], m_e_w_out[0], m_o_w_in[0], m_o_c_w[0].reshape(N_GROUPS * gq, gc), m_o_w_out[0]]
    big_v = [v_e_w_in[0], v_e_w_out[0], v_o_w_in[0], v_o_c_w[0].reshape(N_GROUPS * gq, gc), v_o_w_out[0]]
    coords = jnp.stack([chip, core]).astype(jnp.int32)
    slots = [_cast_into_slot(a, coords, "cast_%d" % k) for k, a in enumerate(big_w)]
    sharded_small = _flat_rows([e_a_conv[0], e_b_conv[0], o_norm_pre, o_norm_post, o_c_scale, o_c_b[0]])
    small_slots = lax.dynamic_update_index_in_dim(jnp.zeros((N_SHARDS,) + sharded_small.shape, F32), sharded_small,
                                                  chip, 0)
    xi, yi = lax.axis_index("x"), lax.axis_index("y")
    order = jnp.stack([chip, 2 * (1 - xi) + yi, 2 * xi + (1 - yi), 2 * (1 - xi) + (1 - yi)]).astype(jnp.int32)
    h0 = _rmsnorm(x2, e_norm_pre, "e_pre_norm")
    p0, (e_w_in_g, e_w_out_g, small_g4) = _gathered_in_proj(h0, slots[:2], small_slots, order, "e_in_proj")
    e_w_in_sm = e_w_in_g.reshape((N_SHARDS,) + big_w[0].shape)
    e_w_out_f = e_w_out_g.reshape(w + w, d)
    sm = small_g4.reshape(N_SHARDS, -1)
    at = [0]

    def take(rows, q):
        blk = sm[:, at[0]:at[0] + rows * q]
        at[0] += rows * q
        return _col_shards_to_full(blk, rows)

    a_conv_f = take(CONV_A, wq)
    b_conv_f = take(CONV_B, wq)
    o_pre_f = take(1, cq)
    o_post_f = take(1, cq)
    cs_f = take(1, cq)
    cb_f = take(N_GROUPS, gq).reshape(1, c)

    mixer_steps = s // _row_tile(s, 128)
    (u0, x1, y0, conv0), odd_g = _even_mixer_fwd(
        p0, e_w_out_f, x2, e_norm_post, a_conv_f, b_conv_f, e_b_conv_bias, e_b_ln_g, e_b_ln_b, "e_mixer_out_proj",
        comm=_gather_comm(slots[2:], mixer_steps // 2, (25 * mixer_steps) // 32))
    o_w_in_sm = odd_g[0].reshape((N_SHARDS,) + big_w[2].shape)
    cw_f = odd_g[1].reshape(N_SHARDS, N_GROUPS, gq, gc).transpose(1, 0, 2, 3).reshape(N_GROUPS, gc, gc)
    o_w_out_f = odd_g[2].reshape(c, d)
    p1, h1 = _norm_matmul(x1, o_pre_f, o_w_in_sm, "o_in_proj")
    u1, pooled1 = _odd_mixer_fwd(p1, cw_f, cb_f, cs_f, "o_mixer_fwd")
    d_y1, d_x2, d_o_post, loss_part = _matmul_post_loss(u1, o_w_out_f, x1, o_post_f, target, "o_out_proj_loss")

    def as_pieces(g, k):
        return g.reshape(N_SHARDS, 2, big_w[k].shape[0] // 2, big_w[k].shape[1])

    def chip_sums(ks, pieces, from_sibling):
        both = [_chip_sum(g, o, coords, "chip_sum_%d" % k) for k, g, o in zip(ks, pieces, from_sibling)]
        return [b[0] for b in both], [b[1] for b in both]

    g_o_w_out = _matmul_tn(u1, d_y1, 1, "o_w_out_grad")
    d_u1, _ = _matmul_nt(d_y1, o_w_out_f[None], "o_out_proj_bwd")
    d_p1, d_cw, d_cb, d_cs = _odd_mixer_bwd(p1, pooled1, d_u1, cw_f, cb_f, cs_f, "o_mixer_bwd")
    g_o_w_in = _matmul_tn(h1, d_p1, N_SHARDS, "o_w_in_grad")
    g_cw = d_cw.reshape(N_GROUPS, N_SHARDS, gq, gc).transpose(1, 0, 2, 3).astype(BF16)
    pieces_o = [as_pieces(g_o_w_in, 2), as_pieces(g_cw, 3), as_pieces(g_o_w_out, 4)]
    d_h1, sibling_o = _matmul_nt(d_p1, o_w_in_sm, "o_in_proj_bwd", comm=_exchange_comm(pieces_o))
    d_x1, d_o_pre, d_y0, d_e_post = _norm_bwd(d_h1, x1, o_pre_f, d_x2, "o_pre_norm_bwd", post=(y0, e_norm_post))

    pieces_e = [as_pieces(_matmul_tn(u0, d_y0, 1, "e_w_out_grad"), 1)]
    d_u0, sibling_e = _matmul_nt(d_y0, e_w_out_f[None], "e_out_proj_bwd", comm=_exchange_comm(pieces_e))
    sums_a, landing_a = chip_sums([1, 2, 3, 4], pieces_e + pieces_o, sibling_e + sibling_o)
    (d_p0, d_a_conv, d_b_conv, d_bias, d_ln_g, d_ln_b), landed_a = _even_mixer_bwd(
        p0, d_u0, conv0, a_conv_f, b_conv_f, e_b_ln_g, e_b_ln_b, "e_mixer_bwd",
        comm=_scatter_comm(sums_a, landing_a))
    pieces_b = [as_pieces(_matmul_tn(h0, d_p0, N_SHARDS, "e_w_in_grad"), 0)]
    sums_b, landing_b = chip_sums([0], pieces_b, _comm_only(_exchange_comm(pieces_b), "exchange_core_halves"))
    d_h0, landed_b = _matmul_nt(d_p0, e_w_in_sm, "e_in_proj_bwd", comm=_scatter_comm(sums_b, landing_b))
    grad_x, d_e_pre = _norm_bwd(d_h0, x2, e_norm_pre, d_x1, "e_pre_norm_bwd")

    landed = landed_b + landed_a
    reduced = [_shard_sum(sc, coords, "shard_sum_%d" % k) for k, sc in enumerate(landed)]
    small_parts = _flat_rows([loss_part[0], d_e_pre, d_e_post, d_bias, d_ln_g, d_ln_b, d_a_conv, d_b_conv,
                              d_o_pre, d_o_post, d_cs, d_cb])
    small_rows = lax.dynamic_update_index_in_dim(jnp.zeros((N_DEVICES,) + small_parts.shape, F32), small_parts,
                                                 2 * chip + core, 0)
    joined = _comm_only(_join_comm(reduced, small_rows), "join_core_halves")
    big_g = [j.reshape(a.shape) for j, a in zip(joined[:5], big_w)]
    small_sum = _sum_small(joined[5], "small_sum").reshape(-1)
    (loss_row, g_e_pre, g_e_post, g_bias, g_ln_g, g_ln_b, g_a_conv_f, g_b_conv_f, g_o_pre_f, g_o_post_f, g_cs_f,
     g_cb_f) = _unflatten(small_sum, [(LANES,), (1, d), (1, d), (1, w), (1, w), (1, w), (CONV_A, w), (CONV_B, w),
                                      (1, c), (1, c), (1, c), (1, c)])
    loss = loss_row[0]
    g_a_conv = _my_col_shard(g_a_conv_f, chip)
    g_b_conv = _my_col_shard(g_b_conv_f, chip)
    g_o_pre = _my_col_shard(g_o_pre_f, chip)
    g_o_post = _my_col_shard(g_o_post_f, chip)
    g_cs = _my_col_shard(g_cs_f, chip)
    g_cb = _my_col_shard(g_cb_f.reshape(N_GROUPS, gc), chip)

    big_upd = [_adamw(wt, g, m, v, "adamw_%d" % k) for k, (wt, g, m, v) in enumerate(zip(big_w, big_g, big_m, big_v))]
    big_g = [u[0] for u in big_upd]
    big_upd = [u[1:] for u in big_upd]
    small_w = [e_norm_pre, e_norm_post, e_b_conv_bias, e_b_ln_g, e_b_ln_b, e_a_conv[0], e_b_conv[0],
               o_norm_pre, o_norm_post, o_c_b[0], o_c_scale]
    small_m = [m_e_norm_pre, m_e_norm_post, m_e_b_conv_bias, m_e_b_ln_g, m_e_b_ln_b, m_e_a_conv[0], m_e_b_conv[0],
               m_o_norm_pre, m_o_norm_post, m_o_c_b[0], m_o_c_scale]
    small_v = [v_e_norm_pre, v_e_norm_post, v_e_b_conv_bias, v_e_b_ln_g, v_e_b_ln_b, v_e_a_conv[0], v_e_b_conv[0],
               v_o_norm_pre, v_o_norm_post, v_o_c_b[0], v_o_c_scale]
    small_g = [g_e_pre, g_e_post, g_bias, g_ln_g, g_ln_b, g_a_conv, g_b_conv, g_o_pre, g_o_post, g_cb, g_cs]
    small_delta, small_new_m, small_new_v = _adamw_many(small_w, small_g, small_m, small_v, "adamw_small")

    def ordered(small, big):
        (n_pre, n_post, bias, ln_g, ln_b, a_conv, b_conv, o_pre, o_post, cb, cs) = small
        (w_in, w_out, ow_in, cw, ow_out) = big
        return [n_pre, n_post, w_in[None], a_conv[None], b_conv[None], bias, ln_g, ln_b, w_out[None], o_pre, o_post,
                ow_in[None], cw.reshape(1, N_GROUPS, gq, gc), cb[None], cs, ow_out[None]]

    grads = ordered(small_g, big_g)
    deltas = ordered(small_delta, [u[0] for u in big_upd])
    new_m = ordered(small_new_m, [u[1] for u in big_upd])
    new_v = ordered(small_new_v, [u[2] for u in big_upd])
    return (loss, grad_x.reshape(1, s, d), *grads, *deltas, *new_m, *new_v)
```
